```python
import math
import jax
import jax.numpy as jnp
from jax import lax
import numpy as np

D_MODEL = 2048
BATCH = 4
SEQ = 4096
DEPTH = 2

CTX_LEN = 256
GRID_W = 64
Q_BLOCK = 128
ROPE_THETA = 10000.0
EPS = 1e-6

MLA_HEADS = 8
MLA_Q_LORA = 512
MLA_KV_LORA = 256
MLA_NOPE_DIM = 128
MLA_ROPE_DIM = 64
MLA_V_DIM = 128
MLA_WIDTH = MLA_HEADS * MLA_V_DIM

GQA_HEADS = 8
GQA_KV_HEADS = 2
GQA_GROUP = GQA_HEADS // GQA_KV_HEADS
GQA_HEAD_DIM = 128
GQA_WIDTH = GQA_HEADS * GQA_HEAD_DIM
GQA_KV_WIDTH = GQA_KV_HEADS * GQA_HEAD_DIM

SSD_INNER = D_MODEL
SSD_HEAD_DIM = 64
SSD_HEADS = SSD_INNER // SSD_HEAD_DIM
SSD_GROUPS = 4
SSD_HEADS_PER_GROUP = SSD_HEADS // SSD_GROUPS
SSD_STATE = 128
SSD_CONV = 5
SSD_CHUNK = 128
SSD_CONV_DIM = SSD_INNER + 2 * SSD_GROUPS * SSD_STATE

N_BRANCH = 3
IN_SPLITS = (MLA_Q_LORA, MLA_KV_LORA, MLA_ROPE_DIM, MLA_WIDTH,
             GQA_WIDTH, GQA_KV_WIDTH, GQA_KV_WIDTH, GQA_WIDTH,
             SSD_INNER, SSD_CONV_DIM, 2 * SSD_HEADS,
             N_BRANCH * D_MODEL)
IN_WIDTH = sum(IN_SPLITS)
IN_OFFSETS = tuple(int(v) for v in np.cumsum(IN_SPLITS)[:-1])

DEEPNORM_ALPHA = (2 * DEPTH) ** 0.25
DEEPNORM_BETA = (8 * DEPTH) ** -0.25

kernel_name = 'hybrid_mla_gqa_ssd_prefix_trunk'


def layer_norm(x):
    xf = x.astype(jnp.float32)
    mu = jnp.mean(xf, axis=-1, keepdims=True)
    var = jnp.mean(jnp.square(xf - mu), axis=-1, keepdims=True)
    return ((xf - mu) * lax.rsqrt(var + EPS)).astype(x.dtype)


def layer_norm_affine(x, g, b):
    return layer_norm(x) * g + b


def rms_norm(x, g):
    xf = x.astype(jnp.float32)
    y = xf * lax.rsqrt(jnp.mean(jnp.square(xf), axis=-1, keepdims=True) + EPS)
    return (y * g).astype(x.dtype)


def modulate(x, shift, scale):
    return layer_norm(x) * (1.0 + scale) + shift


def split_columns(p):
    return jnp.split(p, IN_OFFSETS, axis=-1)


def axial_rope_tables(rows, dim):
    row, col = jnp.meshgrid(jnp.arange(rows, dtype=jnp.float32),
                            jnp.arange(GRID_W, dtype=jnp.float32), indexing='ij')
    half = dim // 2
    inv_freq = ROPE_THETA ** (-jnp.arange(0, half, 2, dtype=jnp.float32) / half)
    ang_r = row.reshape(-1, 1) * inv_freq
    ang_c = col.reshape(-1, 1) * inv_freq
    ang = jnp.concatenate([ang_r, ang_r, ang_c, ang_c], axis=-1)
    return jnp.cos(ang), jnp.sin(ang)


def apply_rope(x, cos, sin):
    x1, x2, x3, x4 = jnp.split(x, 4, axis=-1)
    rot = jnp.concatenate([-x2, x1, -x4, x3], axis=-1)
    shape = (1, x.shape[1]) + (1,) * (x.ndim - 3) + (x.shape[-1],)
    return (x * cos.reshape(shape) + rot * sin.reshape(shape)).astype(x.dtype)


def mla_q(cq, q_norm, w_uq, cos, sin):
    b, n, _ = cq.shape
    q = (rms_norm(cq, q_norm) @ w_uq).reshape(b, n, MLA_HEADS, MLA_NOPE_DIM + MLA_ROPE_DIM)
    q_nope, q_pe = q[..., :MLA_NOPE_DIM], q[..., MLA_NOPE_DIM:]
    if cos is not None:
        q_pe = apply_rope(q_pe, cos, sin)
    return jnp.concatenate([q_nope, q_pe], axis=-1)[:, :, :, None, :]


def mla_kv(ckv, kr, kv_norm, w_ukv, cos, sin):
    b, n, _ = ckv.shape
    kv = (rms_norm(ckv, kv_norm) @ w_ukv).reshape(b, n, MLA_HEADS, MLA_NOPE_DIM + MLA_V_DIM)
    k_nope, v = kv[..., :MLA_NOPE_DIM], kv[..., MLA_NOPE_DIM:]
    k_pe = kr[:, :, None, :]
    if cos is not None:
        k_pe = apply_rope(k_pe, cos, sin)
    k = jnp.concatenate([k_nope, jnp.broadcast_to(k_pe, (b, n, MLA_HEADS, MLA_ROPE_DIM))], axis=-1)
    return k, v


def gqa_q(gq, q_norm, cos, sin):
    b, n, _ = gq.shape
    q = rms_norm(gq.reshape(b, n, GQA_KV_HEADS, GQA_GROUP, GQA_HEAD_DIM), q_norm)
    if cos is not None:
        q = apply_rope(q, cos, sin)
    return q


def gqa_kv(gk, gv, k_norm, cos, sin):
    b, n, _ = gk.shape
    k = rms_norm(gk.reshape(b, n, GQA_KV_HEADS, GQA_HEAD_DIM), k_norm)
    if cos is not None:
        k = apply_rope(k, cos, sin)
    return k, gv.reshape(b, n, GQA_KV_HEADS, GQA_HEAD_DIM)


def attend_latent(q, k_lat, v_lat, k_ctx, v_ctx):
    b, n, kvh, grp, dk = q.shape
    scale = dk ** -0.5
    n_keys = k_lat.shape[1]
    q_blocks = jnp.moveaxis(q.reshape(b, n // Q_BLOCK, Q_BLOCK, kvh, grp, dk), 1, 0)

    def one_block(qb):
        logits = jnp.concatenate([jnp.einsum('bqhgd,bkhd->bhgqk', qb, k_lat),
                                  jnp.einsum('bqhgd,bkhd->bhgqk', qb, k_ctx)], axis=-1)
        p = jax.nn.softmax(logits.astype(jnp.float32) * scale, axis=-1).astype(v_lat.dtype)
        return (jnp.einsum('bhgqk,bkhd->bqhgd', p[..., :n_keys], v_lat)
                + jnp.einsum('bhgqk,bkhd->bqhgd', p[..., n_keys:], v_ctx))

    out = lax.map(one_block, q_blocks)
    return jnp.moveaxis(out, 0, 1).reshape(b, n, -1)


def attend_context(q, k_ctx, v_ctx):
    b, n = q.shape[:2]
    logits = jnp.einsum('bqhgd,bkhd->bhgqk', q, k_ctx).astype(jnp.float32) * q.shape[-1] ** -0.5
    p = jax.nn.softmax(logits, axis=-1).astype(v_ctx.dtype)
    return jnp.einsum('bhgqk,bkhd->bqhgd', p, v_ctx).reshape(b, n, -1)


def centred_depthwise_conv(x, w, bias):
    pad = SSD_CONV // 2
    y = lax.conv_general_dilated(x, w[:, None, :].astype(x.dtype), window_strides=(1,),
                                 padding=((pad, pad),), dimension_numbers=('NWC', 'WIO', 'NWC'),
                                 feature_group_count=x.shape[-1])
    return jax.nn.silu(y + bias)


def ssd_scan(xs, dt, a, bm, cm, h0, want_y):
    b, n, g, e, p = xs.shape
    ns = bm.shape[-1]
    nc = n // SSD_CHUNK
    xs = xs.reshape(b, nc, SSD_CHUNK, g, e, p)
    dt = dt.reshape(b, nc, SSD_CHUNK, g, e)
    bm = bm.reshape(b, nc, SSD_CHUNK, g, ns)
    cm = cm.reshape(b, nc, SSD_CHUNK, g, ns)
    xdt = xs * dt[..., None]
    a_cum = jnp.cumsum(dt * a, axis=2)
    decay_to_end = jnp.exp(a_cum[:, :, -1:] - a_cum)
    chunk_states = jnp.einsum('bcjgn,bcjgep->bcgepn', bm, xdt * decay_to_end[..., None])
    chunk_decay = jnp.exp(a_cum[:, :, -1])
    seq_in = (jnp.moveaxis(chunk_decay, 1, 0), jnp.moveaxis(chunk_states, 1, 0))
    if not want_y:
        h_final, _ = lax.scan(lambda h, s: (s[0][..., None, None] * h + s[1], None), h0, seq_in)
        return None, h_final

    def carry_step(h, s):
        dec, st, c_blk, a_blk = s
        y_off = jnp.einsum('bign,bgepn->bigep', c_blk, h) * jnp.exp(a_blk)[..., None]
        return dec[..., None, None] * h + st, y_off

    h_final, y_off = lax.scan(carry_step, h0,
                              seq_in + (jnp.moveaxis(cm, 1, 0), jnp.moveaxis(a_cum, 1, 0)))
    y_off = jnp.moveaxis(y_off, 0, 1)
    lower_tri = jnp.tril(jnp.ones((SSD_CHUNK, SSD_CHUNK), dtype=bool))
    seg = a_cum[:, :, :, None] - a_cum[:, :, None, :]
    decay = jnp.exp(jnp.where(lower_tri[:, :, None, None], seg, -jnp.inf))
    cb = jnp.einsum('bcign,bcjgn->bcijg', cm, bm)
    y_diag = jnp.einsum('bcijge,bcjgep->bcigep', cb[..., None] * decay, xdt)
    return (y_diag + y_off).reshape(b, n, g * e * p), h_final


def ssd_mixer(xbc_lat, dt_lat, xbc_ctx, dt_ctx, conv_w, conv_b, a_log, dt_bias, d_skip, with_ctx_out):
    f32 = jnp.float32
    G, E, P, N = SSD_GROUPS, SSD_HEADS_PER_GROUP, SSD_HEAD_DIM, SSD_STATE
    a = -jnp.exp(a_log.astype(f32)).reshape(2, G, E)
    d = d_skip.astype(f32).reshape(G, E, 1)

    def prep(xbc, dt_raw):
        b, n, _ = xbc.shape
        xbc = centred_depthwise_conv(xbc, conv_w, conv_b)
        xs, bm, cm = jnp.split(xbc, (SSD_INNER, SSD_INNER + G * N), axis=-1)
        dt = jax.nn.softplus(dt_raw.astype(f32).reshape(b, n, 2, G, E)
                             + dt_bias.astype(f32).reshape(2, G, E))
        return (xs.astype(f32).reshape(b, n, G, E, P), bm.astype(f32).reshape(b, n, G, N),
                cm.astype(f32).reshape(b, n, G, N), dt)

    def flip(t):
        return jnp.flip(t, axis=1)

    def bidirectional(xs, bm, cm, dt, h_fwd, h_bwd, want_y):
        y_f, hf = ssd_scan(xs, dt[:, :, 0], a[0], bm, cm, h_fwd, want_y)
        y_b, hb = ssd_scan(flip(xs), flip(dt[:, :, 1]), a[1], flip(bm), flip(cm), h_bwd, want_y)
        y = None
        if want_y:
            b, n = xs.shape[:2]
            y = y_f + flip(y_b) + (d * xs).reshape(b, n, SSD_INNER)
        return y, hf, hb

    xc, bc, cc, dtc = prep(xbc_ctx, dt_ctx)
    h0 = jnp.zeros((xc.shape[0], G, E, P, N), f32)
    y_ctx, h_f, h_b = bidirectional(xc, bc, cc, dtc, h0, h0, with_ctx_out)
    xl, bl, cl, dtl = prep(xbc_lat, dt_lat)
    y_lat, _, _ = bidirectional(xl, bl, cl, dtl, h_f, h_b, True)
    if with_ctx_out:
        y_ctx = y_ctx.astype(xbc_ctx.dtype)
    return y_lat.astype(xbc_lat.dtype), y_ctx


def gated_group_rms_norm(y, z, g):
    b, n, _ = y.shape
    v = (y * jax.nn.silu(z)).astype(jnp.float32).reshape(b, n, SSD_GROUPS, -1)
    v = v * lax.rsqrt(jnp.mean(jnp.square(v), axis=-1, keepdims=True) + EPS)
    return (v.reshape(b, n, -1) * g).astype(y.dtype)


def merge_branches(ya, ga, yb, gb, yc, zc, merge_logits, ssd_norm, w_br_a, w_br_b, w_br_c, w_out):
    branch_a = (ya * jax.nn.silu(ga)) @ w_br_a
    branch_b = (yb * jax.nn.silu(gb)) @ w_br_b
    branch_c = gated_group_rms_norm(yc, zc, ssd_norm) @ w_br_c
    g_a, g_b, g_c = jnp.split(jax.nn.sigmoid(merge_logits), N_BRANCH, axis=-1)
    return (g_a * branch_a + g_b * branch_b + g_c * branch_c) @ w_out


def setup_inputs(seed: int = 0) -> dict:
    key = jax.random.key(seed)
    ks = jax.random.split(key, 26)
    f32 = jnp.float32
    L, D = DEPTH, D_MODEL

    def dense(k, shape, fan_in, mult=1.0):
        return jax.random.normal(k, shape, f32) * (mult * fan_in ** -0.5)

    def gain(k, shape):
        return 1.0 + 0.02 * jax.random.normal(k, shape, f32)

    def small(k, shape):
        return 0.02 * jax.random.normal(k, shape, f32)

    dt0 = jnp.exp(jax.random.uniform(ks[16], (L, 2, SSD_HEADS), f32, math.log(1e-3), math.log(1e-1)))
    return {
        'x': jax.random.normal(ks[0], (BATCH, SEQ, D), f32),
        'c': jax.random.normal(ks[1], (BATCH, D), f32),
        'ctx': jax.random.normal(ks[2], (BATCH, CTX_LEN, D), f32),
        'c_ctx': jax.random.normal(ks[3], (D,), f32),
        'w_mod': dense(ks[4], (L, D, 3 * D), D, 0.5),
        'b_mod': small(ks[5], (L, 3 * D)),
        'w_in': dense(ks[6], (L, D, IN_WIDTH), D),
        'mla_q_norm': gain(ks[7], (L, MLA_Q_LORA)),
        'mla_w_uq': dense(ks[8], (L, MLA_Q_LORA, MLA_HEADS * (MLA_NOPE_DIM + MLA_ROPE_DIM)), MLA_Q_LORA),
        'mla_kv_norm': gain(ks[9], (L, MLA_KV_LORA)),
        'mla_w_ukv': dense(ks[10], (L, MLA_KV_LORA, MLA_HEADS * (MLA_NOPE_DIM + MLA_V_DIM)), MLA_KV_LORA),
        'gqa_q_norm': gain(ks[11], (L, GQA_HEAD_DIM)),
        'gqa_k_norm': gain(ks[12], (L, GQA_HEAD_DIM)),
        'ssd_conv_w': dense(ks[13], (L, SSD_CONV, SSD_CONV_DIM), SSD_CONV),
        'ssd_conv_b': small(ks[14], (L, SSD_CONV_DIM)),
        'ssd_a_log': jnp.log(jax.random.uniform(ks[15], (L, 2, SSD_HEADS), f32, 1.0, 16.0)),
        'ssd_dt_bias': dt0 + jnp.log(-jnp.expm1(-dt0)),
        'ssd_d': gain(ks[17], (L, SSD_HEADS)),
        'ssd_norm': gain(ks[18], (L, SSD_INNER)),
        'w_br_a': dense(ks[19], (L, MLA_WIDTH, D), MLA_WIDTH, DEEPNORM_BETA),
        'w_br_b': dense(ks[20], (L, GQA_WIDTH, D), GQA_WIDTH, DEEPNORM_BETA),
        'w_br_c': dense(ks[21], (L, SSD_INNER, D), SSD_INNER, DEEPNORM_BETA),
        'w_out': dense(ks[22], (L, D, D), D, DEEPNORM_BETA),
        'ln_g': gain(ks[23], (L, D)),
        'ln_b': small(ks[24], (L, D)),
    }


def reference(x, c, ctx, c_ctx, w_mod, b_mod, w_in, mla_q_norm, mla_w_uq, mla_kv_norm, mla_w_ukv,
              gqa_q_norm, gqa_k_norm, ssd_conv_w, ssd_conv_b, ssd_a_log, ssd_dt_bias, ssd_d, ssd_norm,
              w_br_a, w_br_b, w_br_c, w_out, ln_g, ln_b):
    ROWS = x.shape[1] // GRID_W
    cos_a, sin_a = axial_rope_tables(ROWS, MLA_ROPE_DIM)
    cos_b, sin_b = axial_rope_tables(ROWS, GQA_HEAD_DIM)
    h_ctx = ctx
    for l in range(DEPTH):
        ctx_out = l < DEPTH - 1
        mod = jax.nn.silu(c) @ w_mod[l] + b_mod[l]
        shift, scale, gate = jnp.split(mod[:, None, :], 3, axis=-1)
        shift_c, scale_c, gate_c = jnp.split(jax.nn.silu(c_ctx) @ w_mod[l] + b_mod[l], 3, axis=-1)
        (cq, ckv, kr, ga, gq, gk, gv, gb, z, xbc, dtr, mg) = split_columns(modulate(x, shift, scale) @ w_in[l])
        (cq_c, ckv_c, kr_c, ga_c, gq_c, gk_c, gv_c, gb_c, z_c, xbc_c, dtr_c, mg_c) = split_columns(
            modulate(h_ctx, shift_c, scale_c) @ w_in[l])

        ka, va = mla_kv(ckv, kr, mla_kv_norm[l], mla_w_ukv[l], cos_a, sin_a)
        ka_c, va_c = mla_kv(ckv_c, kr_c, mla_kv_norm[l], mla_w_ukv[l], None, None)
        ya = attend_latent(mla_q(cq, mla_q_norm[l], mla_w_uq[l], cos_a, sin_a), ka, va, ka_c, va_c)
        kb, vb = gqa_kv(gk, gv, gqa_k_norm[l], cos_b, sin_b)
        kb_c, vb_c = gqa_kv(gk_c, gv_c, gqa_k_norm[l], None, None)
        yb = attend_latent(gqa_q(gq, gqa_q_norm[l], cos_b, sin_b), kb, vb, kb_c, vb_c)
        yc, yc_c = ssd_mixer(xbc, dtr, xbc_c, dtr_c, ssd_conv_w[l], ssd_conv_b[l], ssd_a_log[l],
                             ssd_dt_bias[l], ssd_d[l], ctx_out)
        out = merge_branches(ya, ga, yb, gb, yc, z, mg, ssd_norm[l], w_br_a[l], w_br_b[l], w_br_c[l], w_out[l])

        if ctx_out:
            ya_c = attend_context(mla_q(cq_c, mla_q_norm[l], mla_w_uq[l], None, None), ka_c, va_c)
            yb_c = attend_context(gqa_q(gq_c, gqa_q_norm[l], None, None), kb_c, vb_c)
            out_c = merge_branches(ya_c, ga_c, yb_c, gb_c, yc_c, z_c, mg_c, ssd_norm[l],
                                   w_br_a[l], w_br_b[l], w_br_c[l], w_out[l])
            h_ctx = layer_norm_affine(DEEPNORM_ALPHA * h_ctx + gate_c * out_c, ln_g[l], ln_b[l])
        x = layer_norm_affine(DEEPNORM_ALPHA * x + gate * out, ln_g[l], ln_b[l])
    return x
```

```python
import functools
import math

import numpy as np
import jax
import jax.numpy as jnp
from jax import lax
from jax.experimental import pallas as pl
from jax.experimental.pallas import tpu as pltpu

F32 = jnp.float32
BF16 = jnp.bfloat16
HIGHEST = lax.Precision.HIGHEST

D_MODEL = 2048
DEPTH = 2
GRID_W = 64
ROPE_THETA = 10000.0
EPS = 1e-6

MLA_HEADS = 8
MLA_Q_LORA = 512
MLA_KV_LORA = 256
MLA_NOPE = 128
MLA_ROPE = 64
MLA_V = 128
MLA_QK = MLA_NOPE + MLA_ROPE
MLA_QK_PAD = 256
MLA_WIDTH = MLA_HEADS * MLA_V

GQA_HEADS = 8
GQA_KV_HEADS = 2
GQA_GROUP = GQA_HEADS // GQA_KV_HEADS
GQA_DIM = 128
GQA_WIDTH = GQA_HEADS * GQA_DIM
GQA_KV_WIDTH = GQA_KV_HEADS * GQA_DIM

SSD_INNER = D_MODEL
SSD_P = 64
SSD_HEADS = SSD_INNER // SSD_P
SSD_G = 4
SSD_E = SSD_HEADS // SSD_G
SSD_N = 128
SSD_CONV = 5
SSD_Q = 128
SSD_GW = SSD_E * SSD_P
SSD_CONV_DIM = SSD_INNER + 2 * SSD_G * SSD_N

N_BRANCH = 3
IN_SPLITS = (MLA_Q_LORA, MLA_KV_LORA, MLA_ROPE, MLA_WIDTH, GQA_WIDTH, GQA_KV_WIDTH, GQA_KV_WIDTH,
             GQA_WIDTH, SSD_INNER, SSD_CONV_DIM, 2 * SSD_HEADS, N_BRANCH * D_MODEL)
IN_NAMES = ('cq', 'ckv', 'kr', 'ga', 'gq', 'gk', 'gv', 'gb', 'z', 'xbc', 'dtr', 'mg')
IN_OFFSETS = dict(zip(IN_NAMES, np.concatenate([[0], np.cumsum(IN_SPLITS)[:-1]]).tolist()))
IN_WIDTHS = dict(zip(IN_NAMES, IN_SPLITS))
P_ORDER = ('mg', 'z', 'xbc', 'ga', 'gq', 'gb', 'cq', 'ckv', 'gk', 'gv', 'kr', 'dtr')
P_OFF = {}
_o = 0
for _n in P_ORDER:
    P_OFF[_n] = _o
    _o += IN_WIDTHS[_n]
P_USED = _o
P_TN = 512
P_WIDTH = -(-P_USED // P_TN) * P_TN

DEEPNORM_ALPHA = (2 * DEPTH) ** 0.25

VMEM_LIMIT = 56 * 2 ** 20


def _params(*sem):
    return pltpu.CompilerParams(dimension_semantics=sem, vmem_limit_bytes=VMEM_LIMIT)


def _tile(n, target, align=8):
    t = min(n, target)
    while t > align and (n % t or t % align):
        t -= align
    assert n % t == 0, (n, target)
    return t


def _silu(v):
    return v * jax.nn.sigmoid(v)


def _softplus(v):
    return jnp.maximum(v, 0.0) + jnp.log1p(jnp.exp(-jnp.abs(v)))


def _layer_norm(v):
    mu = jnp.mean(v, axis=-1, keepdims=True)
    vc = v - mu
    var = jnp.mean(vc * vc, axis=-1, keepdims=True)
    return vc * lax.rsqrt(var + EPS)


def _rms(v):
    return v * lax.rsqrt(jnp.mean(v * v, axis=-1, keepdims=True) + EPS)


def _mod_kernel(c_ref, w_ref, b_ref, o_ref):
    a = _silu(c_ref[...]).astype(BF16)
    o_ref[...] = jnp.dot(a, w_ref[...].astype(BF16), preferred_element_type=F32) + b_ref[...]


def _mod_rows(c_rows, w_mod, b_mod):
    r, d = c_rows.shape
    n = w_mod.shape[1]
    tn = 512
    return pl.pallas_call(
        _mod_kernel,
        grid=(n // tn,),
        in_specs=[pl.BlockSpec((r, d), lambda j: (0, 0)),
                  pl.BlockSpec((d, tn), lambda j: (0, j)),
                  pl.BlockSpec((1, tn), lambda j: (0, j))],
        out_specs=pl.BlockSpec((r, tn), lambda j: (0, j)),
        out_shape=jax.ShapeDtypeStruct((r, n), F32),
        compiler_params=_params("arbitrary"),
        name="mod_rows",
    )(c_rows, w_mod, b_mod.reshape(1, n))


def _ln_mod_kernel(x_ref, sh_ref, sc_ref, o_ref):
    y = _layer_norm(x_ref[0])
    o_ref[0] = (y * (1.0 + sc_ref[0]) + sh_ref[0]).astype(o_ref.dtype)


def _mod_row_index(nb, ctx_tiles):
    return lambda b, i: jnp.where(i < ctx_tiles, nb, b)


def _ln_mod(xc, tab, nb, ctx_len):
    b, t, d = xc.shape
    tm = _tile(ctx_len, 256)
    row = _mod_row_index(nb, ctx_len // tm)
    return pl.pallas_call(
        _ln_mod_kernel,
        grid=(b, t // tm),
        in_specs=[pl.BlockSpec((1, tm, d), lambda bi, i: (bi, i, 0)),
                  pl.BlockSpec((1, 1, d), lambda bi, i: (row(bi, i), 0, 0)),
                  pl.BlockSpec((1, 1, d), lambda bi, i: (row(bi, i), 0, 1))],
        out_specs=pl.BlockSpec((1, tm, d), lambda bi, i: (bi, i, 0)),
        out_shape=jax.ShapeDtypeStruct((b, t, d), BF16),
        compiler_params=_params("parallel", "parallel"),
        name="ln_mod",
    )(xc, tab, tab)


def _matmul_kernel(x_ref, w_ref, o_ref):
    o_ref[...] = jnp.dot(x_ref[...], w_ref[...], preferred_element_type=F32).astype(o_ref.dtype)


def _in_proj(xm2, wp, out_dtype):
    m, k = xm2.shape
    n = wp.shape[1]
    tm = _tile(m, 1024)
    tn = P_TN
    return pl.pallas_call(
        _matmul_kernel,
        grid=(m // tm, n // tn),
        in_specs=[pl.BlockSpec((tm, k), lambda i, j: (i, 0)),
                  pl.BlockSpec((k, tn), lambda i, j: (0, j))],
        out_specs=pl.BlockSpec((tm, tn), lambda i, j: (i, j)),
        out_shape=jax.ShapeDtypeStruct((m, n), out_dtype),
        compiler_params=_params("parallel", "arbitrary"),
        name="in_proj",
    )(xm2, wp)


def _mla_prep_kernel(cq_ref, ckv_ref, kr_ref, qn_ref, kvn_ref, wqm_ref, wqr_ref, wkv_ref,
                     cosq_ref, sinq_ref, ck_ref, s1_ref, s2_ref, qa_ref, ka_ref, va_ref):
    cqn = (_rms(cq_ref[0].astype(F32)) * qn_ref[...]).astype(BF16)
    qm = jnp.dot(cqn, wqm_ref[...], preferred_element_type=F32)
    qr = jnp.dot(cqn, wqr_ref[...], preferred_element_type=F32)
    cosq = cosq_ref[...]
    sinq = sinq_ref[...]
    for h in range(MLA_HEADS):
        sl = slice(h * MLA_QK_PAD, (h + 1) * MLA_QK_PAD)
        qa_ref[0, h] = (qm[:, sl] * cosq + qr[:, sl] * sinq).astype(qa_ref.dtype)
    ckvn = (_rms(ckv_ref[0].astype(F32)) * kvn_ref[...]).astype(BF16)
    kv = jnp.dot(ckvn, wkv_ref[...], preferred_element_type=F32)
    kr = kr_ref[0].astype(F32)
    kpe = (kr * ck_ref[...] + pltpu.roll(kr, 128 - 16, axis=1) * s1_ref[...]
           + pltpu.roll(kr, 16, axis=1) * s2_ref[...]).astype(ka_ref.dtype)
    for h in range(MLA_HEADS):
        base = h * (MLA_NOPE + MLA_V)
        ka_ref[0, h, :, 0:MLA_NOPE] = kv[:, base:base + MLA_NOPE].astype(ka_ref.dtype)
        ka_ref[0, h, :, MLA_NOPE:MLA_QK_PAD] = kpe
        va_ref[0, h] = kv[:, base + MLA_NOPE:base + MLA_NOPE + MLA_V].astype(va_ref.dtype)


def _mla_prep(p3, qn, kvn, wqm, wqr, wkv, tabs, ctx_len):
    b, t, _ = p3.shape
    tm = _tile(ctx_len, 256)
    cosq, sinq, ck, s1, s2 = tabs

    def col(name, width):
        blk = P_OFF[name] // width
        return pl.BlockSpec((1, tm, width), lambda bi, i: (bi, i, blk))

    def full(a):
        return pl.BlockSpec(a.shape, lambda bi, i: (0,) * a.ndim)

    def rows(a):
        return pl.BlockSpec((tm, a.shape[1]), lambda bi, i: (i, 0))

    hm = lambda w: pl.BlockSpec((1, MLA_HEADS, tm, w), lambda bi, i: (bi, 0, i, 0))
    return pl.pallas_call(
        _mla_prep_kernel,
        grid=(b, t // tm),
        in_specs=[col('cq', MLA_Q_LORA), col('ckv', MLA_KV_LORA), col('kr', 128),
                  full(qn), full(kvn), full(wqm), full(wqr), full(wkv),
                  rows(cosq), rows(sinq), rows(ck), rows(s1), rows(s2)],
        out_specs=[hm(MLA_QK_PAD), hm(MLA_QK_PAD), hm(MLA_V)],
        out_shape=[jax.ShapeDtypeStruct((b, MLA_HEADS, t, MLA_QK_PAD), BF16),
                   jax.ShapeDtypeStruct((b, MLA_HEADS, t, MLA_QK_PAD), BF16),
                   jax.ShapeDtypeStruct((b, MLA_HEADS, t, MLA_V), BF16)],
        compiler_params=_params("parallel", "parallel"),
        name="mla_prep",
    )(p3, p3, p3, qn, kvn, wqm, wqr, wkv, cosq, sinq, ck, s1, s2)


def _rope128(y, c, s1, s2):
    return y * c + pltpu.roll(y, 128 - 32, axis=1) * s1 + pltpu.roll(y, 32, axis=1) * s2


def _gqa_prep_kernel(gq_ref, gk_ref, gv_ref, qn_ref, kn_ref, cq_ref, s1q_ref, s2q_ref,
                     ck_ref, s1k_ref, s2k_ref, qb_ref, kb_ref, vb_ref):
    gq = gq_ref[0].astype(F32)
    for h in range(GQA_HEADS):
        y = _rms(gq[:, h * GQA_DIM:(h + 1) * GQA_DIM]) * qn_ref[...]
        qb_ref[0, h] = _rope128(y, cq_ref[...], s1q_ref[...], s2q_ref[...]).astype(qb_ref.dtype)
    gk = gk_ref[0].astype(F32)
    gv = gv_ref[0]
    for h in range(GQA_KV_HEADS):
        y = _rms(gk[:, h * GQA_DIM:(h + 1) * GQA_DIM]) * kn_ref[...]
        kb_ref[0, h] = _rope128(y, ck_ref[...], s1k_ref[...], s2k_ref[...]).astype(kb_ref.dtype)
        vb_ref[0, h] = gv[:, h * GQA_DIM:(h + 1) * GQA_DIM].astype(vb_ref.dtype)


def _gqa_prep(p3, qn, kn, tabs, ctx_len):
    b, t, _ = p3.shape
    tm = _tile(ctx_len, 256)

    def col(name, width):
        blk = P_OFF[name] // width
        return pl.BlockSpec((1, tm, width), lambda bi, i: (bi, i, blk))

    def full(a):
        return pl.BlockSpec(a.shape, lambda bi, i: (0,) * a.ndim)

    def rows(a):
        return pl.BlockSpec((tm, a.shape[1]), lambda bi, i: (i, 0))

    hm = lambda nh: pl.BlockSpec((1, nh, tm, GQA_DIM), lambda bi, i: (bi, 0, i, 0))
    return pl.pallas_call(
        _gqa_prep_kernel,
        grid=(b, t // tm),
        in_specs=[col('gq', GQA_WIDTH), col('gk', GQA_KV_WIDTH), col('gv', GQA_KV_WIDTH),
                  full(qn), full(kn)] + [rows(a) for a in tabs],
        out_specs=[hm(GQA_HEADS), hm(GQA_KV_HEADS), hm(GQA_KV_HEADS)],
        out_shape=[jax.ShapeDtypeStruct((b, GQA_HEADS, t, GQA_DIM), BF16),
                   jax.ShapeDtypeStruct((b, GQA_KV_HEADS, t, GQA_DIM), BF16),
                   jax.ShapeDtypeStruct((b, GQA_KV_HEADS, t, GQA_DIM), BF16)],
        compiler_params=_params("parallel", "parallel"),
        name="gqa_prep",
    )(p3, p3, p3, qn, kn, *tabs)


def _attn_kernel(q_ref, k_ref, v_ref, o_ref, m_sc, l_sc, acc_sc, *, group, tq, tk, ctx_len,
                 n_lat_chunks, ctx_q_tiles, q_off):
    dk = q_ref.shape[-1]
    dv = v_ref.shape[-1]
    q = q_ref[0].reshape(group * tq, dk)
    m_sc[...] = jnp.full(m_sc.shape, -jnp.inf, F32)
    l_sc[...] = jnp.zeros(l_sc.shape, F32)
    acc_sc[...] = jnp.zeros(acc_sc.shape, F32)

    def step(start, size):
        k = k_ref[0, 0, pl.ds(start, size), :]
        v = v_ref[0, 0, pl.ds(start, size), :]
        s = lax.dot_general(q, k, (((1,), (1,)), ((), ())), preferred_element_type=F32)
        m_prev = m_sc[...]
        m_new = jnp.maximum(m_prev, jnp.max(s, axis=-1, keepdims=True))
        alpha = jnp.exp(m_prev - m_new)
        p = jnp.exp(s - m_new)
        l_sc[...] = alpha * l_sc[...] + jnp.sum(p, axis=-1, keepdims=True)
        acc_sc[...] = alpha * acc_sc[...] + jnp.dot(p.astype(v.dtype), v, preferred_element_type=F32)
        m_sc[...] = m_new

    step(0, ctx_len)

    def lat_body(j, carry):
        step(pl.multiple_of(ctx_len + j * tk, math.gcd(ctx_len, tk)), tk)
        return carry

    if ctx_q_tiles > q_off:
        @pl.when(pl.program_id(2) + q_off >= ctx_q_tiles)
        def _():
            lax.fori_loop(0, n_lat_chunks, lat_body, 0)
    else:
        lax.fori_loop(0, n_lat_chunks, lat_body, 0)

    o = acc_sc[...] / l_sc[...]
    for g in range(group):
        o_ref[0, :, g * dv:(g + 1) * dv] = o[g * tq:(g + 1) * tq, :].astype(o_ref.dtype)


def _attention(q, k, v, ctx_len, tq, skip_ctx_queries, out_dtype):
    b, hq, t, dk = q.shape
    hkv, dv = k.shape[1], v.shape[-1]
    group = hq // hkv
    tq = _tile(ctx_len, tq)
    seq = t - ctx_len
    tk = _tile(seq, 512, 128)
    ctx_q_tiles = ctx_len // tq
    q_off = ctx_q_tiles if skip_ctx_queries else 0
    nq = t // tq - q_off
    kern = functools.partial(_attn_kernel, group=group, tq=tq, tk=tk, ctx_len=ctx_len,
                             n_lat_chunks=seq // tk, ctx_q_tiles=ctx_q_tiles, q_off=q_off)
    rows = group * tq
    return pl.pallas_call(
        kern,
        grid=(b, hkv, nq),
        in_specs=[pl.BlockSpec((1, group, tq, dk), lambda bi, h, i: (bi, h, i + q_off, 0)),
                  pl.BlockSpec((1, 1, t, dk), lambda bi, h, i: (bi, h, 0, 0)),
                  pl.BlockSpec((1, 1, t, dv), lambda bi, h, i: (bi, h, 0, 0))],
        out_specs=pl.BlockSpec((1, tq, group * dv), lambda bi, h, i: (bi, i, h)),
        out_shape=jax.ShapeDtypeStruct((b, nq * tq, hq * dv), out_dtype),
        scratch_shapes=[pltpu.VMEM((rows, 1), F32), pltpu.VMEM((rows, 1), F32),
                        pltpu.VMEM((rows, dv), F32)],
        compiler_params=_params("parallel", "parallel", "arbitrary"),
        name="attention_dk%d" % dk,
    )(q, k, v)


def _conv_kernel(x_ref, w_ref, b_ref, o_ref, pad_sc, *, ctx_len, rows):
    t = x_ref.shape[1]
    nch = x_ref.shape[2]
    halo = 8
    segs = ((0, ctx_len), (ctx_len, t))
    zeros = jnp.zeros((halo, nch), F32)
    for si, (lo, hi) in enumerate(segs):
        pad_sc[lo + si * halo:lo + (si + 1) * halo, :] = zeros
        for r0 in range(lo, hi, rows):
            pad_sc[r0 + (si + 1) * halo:r0 + (si + 1) * halo + rows, :] = x_ref[0, r0:r0 + rows, :].astype(F32)
    pad_sc[t + 2 * halo:t + 3 * halo, :] = zeros
    w = w_ref[...]
    bias = b_ref[...]
    for si, (lo, hi) in enumerate(segs):
        for r0 in range(lo, hi, rows):
            base = r0 + (si + 1) * halo - SSD_CONV // 2
            acc = bias + w[0:1, :] * pad_sc[base:base + rows, :]
            for kk in range(1, SSD_CONV):
                acc = acc + w[kk:kk + 1, :] * pad_sc[base + kk:base + kk + rows, :]
            o_ref[0, r0:r0 + rows, :] = _silu(acc).astype(o_ref.dtype)


def _conv(p3, conv_w, conv_b, ctx_len, out_dtype):
    b, t, _ = p3.shape
    nch = 256
    rows = _tile(ctx_len, 256)
    blk0 = P_OFF['xbc'] // nch
    kern = functools.partial(_conv_kernel, ctx_len=ctx_len, rows=rows)
    return pl.pallas_call(
        kern,
        grid=(b, SSD_CONV_DIM // nch),
        in_specs=[pl.BlockSpec((1, t, nch), lambda bi, j: (bi, 0, blk0 + j)),
                  pl.BlockSpec((SSD_CONV, nch), lambda bi, j: (0, j)),
                  pl.BlockSpec((1, nch), lambda bi, j: (0, j))],
        out_specs=pl.BlockSpec((1, t, nch), lambda bi, j: (bi, 0, j)),
        out_shape=jax.ShapeDtypeStruct((b, t, SSD_CONV_DIM), out_dtype),
        scratch_shapes=[pltpu.VMEM((t + 24, nch), F32)],
        compiler_params=_params("parallel", "parallel"),
        name="ssd_conv",
    )(p3, conv_w, conv_b.reshape(1, SSD_CONV_DIM))


def _ssd_direction(xs, bm, cm, dt_col_raw, dt_row_raw, bias_col, bias_row, a_col, a_row, h_ref,
                   backward):
    q = xs.shape[0]
    ri = lax.broadcasted_iota(jnp.int32, (q, q), 0)
    ci = lax.broadcasted_iota(jnp.int32, (q, q), 1)
    low = ri >= ci
    keep = (ri <= ci) if backward else low
    tri_col = jnp.where(keep, 1.0, 0.0).astype(F32)
    tri_row = jnp.where((ri >= ci) if backward else (ri <= ci), 1.0, 0.0).astype(F32)
    expand = jnp.where(lax.broadcasted_iota(jnp.int32, (SSD_E, SSD_GW), 1) // SSD_P
                       == lax.broadcasted_iota(jnp.int32, (SSD_E, SSD_GW), 0), 1.0, 0.0).astype(F32)

    dt_c = _softplus(dt_col_raw + bias_col)
    dt_r = _softplus(dt_row_raw + bias_row)
    cum_c = jnp.dot(tri_col, dt_c * a_col, precision=HIGHEST, preferred_element_type=F32)
    cum_r = jnp.dot(dt_r * a_row, tri_row, precision=HIGHEST, preferred_element_type=F32)
    cum_full = jnp.dot(cum_c, expand, precision=HIGHEST, preferred_element_type=F32)
    dt_full = jnp.dot(dt_c, expand, precision=HIGHEST, preferred_element_type=F32)
    total = cum_full[0:1, :] if backward else cum_full[q - 1:q, :]

    xdt = xs * dt_full
    cmb = cm.astype(BF16)
    cb = lax.dot_general(cmb, bm.astype(BF16), (((1,), (1,)), ((), ())), preferred_element_type=F32)
    h = h_ref[...]
    y_off = jnp.dot(cmb, h.astype(BF16), preferred_element_type=F32) * jnp.exp(cum_full)
    wgt = (xdt * jnp.exp(total - cum_full)).astype(BF16)
    h_ref[...] = jnp.exp(total) * h + jnp.dot(bm.T.astype(BF16), wgt, preferred_element_type=F32)

    xdt_b = xdt.astype(BF16)
    lane = lax.broadcasted_iota(jnp.int32, (q, 2 * SSD_P), 1)
    parts = []
    for pair in range(SSD_E // 2):
        x_pair = xdt_b[:, pair * 2 * SSD_P:(pair + 1) * 2 * SSD_P]
        ys = []
        for e in (2 * pair, 2 * pair + 1):
            seg = cum_c[:, e:e + 1] - cum_r[e:e + 1, :]
            dec = jnp.exp(jnp.where(keep, seg, -jnp.inf))
            ys.append(jnp.dot((cb * dec).astype(BF16), x_pair, preferred_element_type=F32))
        parts.append(jnp.where(lane < SSD_P, ys[0], ys[1]))
    return jnp.concatenate(parts, axis=1) + y_off


def _ssd_kernel(xf_ref, bf_ref, cf_ref, xb_ref, bb_ref, cb_ref, dcf_ref, drf_ref, dcb_ref, drb_ref,
                bias_c_ref, bias_r_ref, a_c_ref, a_r_ref, d_ref, yf_ref, yb_ref, hf_sc, hb_sc):
    @pl.when(pl.program_id(2) == 0)
    def _():
        hf_sc[...] = jnp.zeros(hf_sc.shape, F32)
        hb_sc[...] = jnp.zeros(hb_sc.shape, F32)

    xs = xf_ref[0].astype(F32)
    yf = _ssd_direction(xs, bf_ref[0].astype(F32), cf_ref[0].astype(F32), dcf_ref[0, 0, 0], drf_ref[0, 0, 0],
                        bias_c_ref[0, 0], bias_r_ref[0, 0], a_c_ref[0, 0], a_r_ref[0, 0], hf_sc, False)
    yf_ref[0] = (yf + d_ref[0] * xs).astype(yf_ref.dtype)
    yb = _ssd_direction(xb_ref[0].astype(F32), bb_ref[0].astype(F32), cb_ref[0].astype(F32),
                        dcb_ref[0, 0, 0], drb_ref[0, 0, 0],
                        bias_c_ref[1, 0], bias_r_ref[1, 0], a_c_ref[1, 0], a_r_ref[1, 0], hb_sc, True)
    yb_ref[0] = yb.astype(yb_ref.dtype)


def _ssd(xc, dt_col, dt_row, bias_c, bias_r, a_c, a_r, d_full, ctx_len, out_dtype):
    b, t, _ = xc.shape
    nchunk = t // SSD_Q
    nctx = ctx_len // SSD_Q
    bblk = SSD_INNER // SSD_N
    cblk = bblk + SSD_G

    def bidx(c):
        return jnp.where(c < nctx, nctx - 1 - c, nchunk - 1 - (c - nctx))

    fx = lambda bi, g, c: (bi, c, g)
    fb = lambda bi, g, c: (bi, c, bblk + g)
    fc = lambda bi, g, c: (bi, c, cblk + g)
    bx = lambda bi, g, c: (bi, bidx(c), g)
    bb = lambda bi, g, c: (bi, bidx(c), bblk + g)
    bc = lambda bi, g, c: (bi, bidx(c), cblk + g)
    small = lambda a: pl.BlockSpec((2, 1) + a.shape[2:], lambda bi, g, c: (0, g, 0, 0))
    return pl.pallas_call(
        _ssd_kernel,
        grid=(b, SSD_G, nchunk),
        in_specs=[pl.BlockSpec((1, SSD_Q, SSD_GW), fx), pl.BlockSpec((1, SSD_Q, SSD_N), fb),
                  pl.BlockSpec((1, SSD_Q, SSD_N), fc),
                  pl.BlockSpec((1, SSD_Q, SSD_GW), bx), pl.BlockSpec((1, SSD_Q, SSD_N), bb),
                  pl.BlockSpec((1, SSD_Q, SSD_N), bc),
                  pl.BlockSpec((1, 1, 1, SSD_Q, SSD_E), lambda bi, g, c: (bi, 0, g, c, 0)),
                  pl.BlockSpec((1, 1, 1, SSD_E, SSD_Q), lambda bi, g, c: (bi, 0, g, 0, c)),
                  pl.BlockSpec((1, 1, 1, SSD_Q, SSD_E), lambda bi, g, c: (bi, 1, g, bidx(c), 0)),
                  pl.BlockSpec((1, 1, 1, SSD_E, SSD_Q), lambda bi, g, c: (bi, 1, g, 0, bidx(c))),
                  small(bias_c), small(bias_r), small(a_c), small(a_r),
                  pl.BlockSpec((1, 1, SSD_GW), lambda bi, g, c: (g, 0, 0))],
        out_specs=[pl.BlockSpec((1, SSD_Q, SSD_GW), fx), pl.BlockSpec((1, SSD_Q, SSD_GW), bx)],
        out_shape=[jax.ShapeDtypeStruct((b, t, SSD_INNER), out_dtype)] * 2,
        scratch_shapes=[pltpu.VMEM((SSD_N, SSD_GW), F32), pltpu.VMEM((SSD_N, SSD_GW), F32)],
        compiler_params=_params("parallel", "parallel", "arbitrary"),
        name="ssd_scan",
    )(xc, xc, xc, xc, xc, xc, dt_col, dt_row, dt_col, dt_row, bias_c, bias_r, a_c, a_r, d_full)


def _merge_kernel(ya_ref, ga_ref, yb_ref, gb_ref, yf_ref, ybk_ref, z_ref, nrm_ref,
                  m0_ref, m1_ref, m2_ref, wa_ref, wb_ref, wc_ref, u_ref, a_sc, b_sc, c_sc):
    @pl.when(pl.program_id(2) == 0)
    def _():
        a_sc[...] = (ya_ref[0].astype(F32) * _silu(ga_ref[0].astype(F32))).astype(BF16)
        b_sc[...] = (yb_ref[0].astype(F32) * _silu(gb_ref[0].astype(F32))).astype(BF16)
        v = (yf_ref[0].astype(F32) + ybk_ref[0].astype(F32)) * _silu(z_ref[0].astype(F32))
        for g in range(SSD_G):
            sl = slice(g * SSD_GW, (g + 1) * SSD_GW)
            c_sc[:, sl] = (_rms(v[:, sl]) * nrm_ref[:, sl]).astype(BF16)

    br_a = jnp.dot(a_sc[...], wa_ref[...], preferred_element_type=F32)
    br_b = jnp.dot(b_sc[...], wb_ref[...], preferred_element_type=F32)
    br_c = jnp.dot(c_sc[...], wc_ref[...], preferred_element_type=F32)
    u = (jax.nn.sigmoid(m0_ref[0].astype(F32)) * br_a + jax.nn.sigmoid(m1_ref[0].astype(F32)) * br_b
         + jax.nn.sigmoid(m2_ref[0].astype(F32)) * br_c)
    u_ref[0] = u.astype(u_ref.dtype)


def _merge(ya, yb, yf, ybk, p3, ssd_norm, wa, wb, wc, row_off):
    b, rows, _ = ya.shape
    tm = _tile(rows, 256)
    tn = 512
    d = D_MODEL
    ro = row_off // tm
    nj = d // tn

    def col(name, width):
        blk = P_OFF[name] // width
        return pl.BlockSpec((1, tm, width), lambda bi, i, j: (bi, i + ro, blk))

    def mgate(k):
        return pl.BlockSpec((1, tm, tn), lambda bi, i, j: (bi, i + ro, k * nj + j))

    loc = lambda w: pl.BlockSpec((1, tm, w), lambda bi, i, j: (bi, i, 0))
    glob = lambda w: pl.BlockSpec((1, tm, w), lambda bi, i, j: (bi, i + ro, 0))
    wspec = lambda k: pl.BlockSpec((k, tn), lambda bi, i, j: (0, j))
    return pl.pallas_call(
        _merge_kernel,
        grid=(b, rows // tm, nj),
        in_specs=[loc(MLA_WIDTH), col('ga', MLA_WIDTH), loc(GQA_WIDTH), col('gb', GQA_WIDTH),
                  glob(SSD_INNER), glob(SSD_INNER), col('z', SSD_INNER),
                  pl.BlockSpec((1, SSD_INNER), lambda bi, i, j: (0, 0)),
                  mgate(0), mgate(1), mgate(2), wspec(MLA_WIDTH), wspec(GQA_WIDTH), wspec(SSD_INNER)],
        out_specs=pl.BlockSpec((1, tm, tn), lambda bi, i, j: (bi, i, j)),
        out_shape=jax.ShapeDtypeStruct((b, rows, d), BF16),
        scratch_shapes=[pltpu.VMEM((tm, MLA_WIDTH), BF16), pltpu.VMEM((tm, GQA_WIDTH), BF16),
                        pltpu.VMEM((tm, SSD_INNER), BF16)],
        compiler_params=_params("parallel", "parallel", "arbitrary"),
        name="merge",
    )(ya, p3, yb, p3, yf, ybk, p3, ssd_norm.reshape(1, SSD_INNER), p3, p3, p3, wa, wb, wc)


def _out_ln_kernel(u_ref, w_ref, x_ref, gate_ref, g_ref, b_ref, *rest, with_next):
    out = jnp.dot(u_ref[0], w_ref[...], preferred_element_type=F32)
    r = DEEPNORM_ALPHA * x_ref[0] + gate_ref[0] * out
    xn = _layer_norm(r) * g_ref[...] + b_ref[...]
    if with_next:
        sh_ref, sc_ref, xo_ref, xm_ref = rest
        xo_ref[0] = xn
        xm_ref[0] = (_layer_norm(xn) * (1.0 + sc_ref[0]) + sh_ref[0]).astype(xm_ref.dtype)
    else:
        (xo_ref,) = rest
        xo_ref[0] = xn


def _out_ln(u, w_out, xc, tab, ln_g, ln_b, next_tab, nb, ctx_len, row_off):
    b, rows, d = u.shape
    tm = _tile(ctx_len, 256)
    ro = row_off // tm
    row = _mod_row_index(nb, ctx_len // tm)
    with_next = next_tab is not None
    loc = pl.BlockSpec((1, tm, d), lambda bi, i: (bi, i, 0))
    vec = pl.BlockSpec((1, d), lambda bi, i: (0, 0))
    in_specs = [loc, pl.BlockSpec((d, d), lambda bi, i: (0, 0)),
                pl.BlockSpec((1, tm, d), lambda bi, i: (bi, i + ro, 0)),
                pl.BlockSpec((1, 1, d), lambda bi, i: (row(bi, i + ro), 0, 2)), vec, vec]
    args = [u, w_out, xc, tab, ln_g.reshape(1, d), ln_b.reshape(1, d)]
    out_specs = [loc]
    out_shape = [jax.ShapeDtypeStruct((b, rows, d), F32)]
    if with_next:
        in_specs += [pl.BlockSpec((1, 1, d), lambda bi, i: (row(bi, i + ro), 0, 0)),
                     pl.BlockSpec((1, 1, d), lambda bi, i: (row(bi, i + ro), 0, 1))]
        args += [next_tab, next_tab]
        out_specs.append(loc)
        out_shape.append(jax.ShapeDtypeStruct((b, rows, d), BF16))
    return pl.pallas_call(
        functools.partial(_out_ln_kernel, with_next=with_next),
        grid=(b, rows // tm),
        in_specs=in_specs,
        out_specs=out_specs,
        out_shape=out_shape,
        compiler_params=_params("parallel", "parallel"),
        name="out_ln",
    )(*args)


def _rope_angles(rows, dim):
    row, col = jnp.meshgrid(jnp.arange(rows, dtype=F32), jnp.arange(GRID_W, dtype=F32), indexing='ij')
    half = dim // 2
    inv_freq = ROPE_THETA ** (-jnp.arange(0, half, 2, dtype=F32) / half)
    ang_r = row.reshape(-1, 1) * inv_freq
    ang_c = col.reshape(-1, 1) * inv_freq
    return jnp.concatenate([ang_r, ang_r, ang_c, ang_c], axis=-1)


def _rope_tables(seq, ctx_len, dim):
    ang = _rope_angles(seq // GRID_W, dim)
    cos = jnp.concatenate([jnp.ones((ctx_len, dim), F32), jnp.cos(ang)], axis=0)
    sin = jnp.concatenate([jnp.zeros((ctx_len, dim), F32), jnp.sin(ang)], axis=0)
    return cos, sin


def _roll_tables(cos, sin, dim, scale):
    t = cos.shape[0]
    quarter = dim // 4
    first = (jnp.arange(dim) % (2 * quarter)) < quarter
    s1 = jnp.where(first, -sin, 0.0)
    s2 = jnp.where(first, 0.0, sin)
    pad = lambda a: jnp.pad(a * scale, ((0, 0), (0, 128 - dim)))
    return pad(cos), pad(s1), pad(s2)


def _rot_matrix(dim):
    quarter = dim // 4
    r = np.zeros((dim, dim), np.float32)
    for i in range(dim):
        blk = i // quarter
        if blk % 2 == 0:
            r[i + quarter, i] = -1.0
        else:
            r[i - quarter, i] = 1.0
    return jnp.asarray(r)


def _mla_q_weights(w_uq):
    w = w_uq.reshape(MLA_Q_LORA, MLA_HEADS, MLA_QK)
    nope, pe = w[..., :MLA_NOPE], w[..., MLA_NOPE:]
    pe_rot = jnp.einsum('khd,de->khe', pe, _rot_matrix(MLA_ROPE), precision=HIGHEST)
    zpad = jnp.zeros((MLA_Q_LORA, MLA_HEADS, MLA_QK_PAD - MLA_QK), F32)
    main = jnp.concatenate([nope, pe, zpad], axis=-1)
    rot = jnp.concatenate([jnp.zeros_like(nope), pe_rot, zpad], axis=-1)
    return (main.reshape(MLA_Q_LORA, -1).astype(BF16), rot.reshape(MLA_Q_LORA, -1).astype(BF16))


def _permute_w_in(w):
    wb = w.astype(BF16)
    parts = [wb[:, IN_OFFSETS[n]:IN_OFFSETS[n] + IN_WIDTHS[n]] for n in P_ORDER]
    parts.append(jnp.zeros((w.shape[0], P_WIDTH - P_USED), BF16))
    return jnp.concatenate(parts, axis=1)


P_DTYPE = F32
Y_DTYPE = F32


def kernel(x, c, ctx, c_ctx, w_mod, b_mod, w_in, mla_q_norm, mla_w_uq, mla_kv_norm, mla_w_ukv,
           gqa_q_norm, gqa_k_norm, ssd_conv_w, ssd_conv_b, ssd_a_log, ssd_dt_bias, ssd_d, ssd_norm,
           w_br_a, w_br_b, w_br_c, w_out, ln_g, ln_b):
    nb, seq, d = x.shape
    ctx_len = ctx.shape[1]
    t = ctx_len + seq
    depth = w_in.shape[0]
    assert d == D_MODEL and nb < 8 and seq % GRID_W == 0
    assert ctx_len % SSD_Q == 0 and seq % SSD_Q == 0

    cos_a, sin_a = _rope_tables(seq, ctx_len, MLA_ROPE)
    cos_b, sin_b = _rope_tables(seq, ctx_len, GQA_DIM)
    sq = MLA_QK ** -0.5
    zq = jnp.zeros((t, MLA_QK_PAD - MLA_QK), F32)
    cosq = jnp.concatenate([jnp.full((t, MLA_NOPE), sq, F32), cos_a * sq, zq], axis=1)
    sinq = jnp.concatenate([jnp.zeros((t, MLA_NOPE), F32), sin_a * sq, zq], axis=1)
    mla_tabs = (cosq, sinq) + _roll_tables(cos_a, sin_a, MLA_ROPE, 1.0)
    gqa_tabs = _roll_tables(cos_b, sin_b, GQA_DIM, GQA_DIM ** -0.5) + _roll_tables(cos_b, sin_b, GQA_DIM, 1.0)

    c_rows = jnp.zeros((8, d), F32).at[:nb].set(c).at[nb].set(c_ctx)
    tabs = [_mod_rows(c_rows, w_mod[l], b_mod[l]).reshape(8, 1, 3 * d) for l in range(depth)]

    xc = jnp.concatenate([ctx, x], axis=1)
    xm = _ln_mod(xc, tabs[0], nb, ctx_len)

    for l in range(depth):
        last = l == depth - 1
        wp = _permute_w_in(w_in[l])
        p3 = _in_proj(xm.reshape(nb * t, d), wp, P_DTYPE).reshape(nb, t, P_WIDTH)

        wqm, wqr = _mla_q_weights(mla_w_uq[l])
        qa, ka, va = _mla_prep(p3, mla_q_norm[l].reshape(1, -1), mla_kv_norm[l].reshape(1, -1),
                               wqm, wqr, mla_w_ukv[l].astype(BF16), mla_tabs, ctx_len)
        ya = _attention(qa, ka, va, ctx_len, 256, last, Y_DTYPE)
        qb, kb, vb = _gqa_prep(p3, gqa_q_norm[l].reshape(1, -1), gqa_k_norm[l].reshape(1, -1),
                               gqa_tabs, ctx_len)
        yb = _attention(qb, kb, vb, ctx_len, 128, last, Y_DTYPE)

        xconv = _conv(p3, ssd_conv_w[l], ssd_conv_b[l], ctx_len, Y_DTYPE)
        dtr = p3[:, :, P_OFF['dtr']:P_OFF['dtr'] + 2 * SSD_HEADS].astype(F32)
        dt5 = dtr.reshape(nb, t, 2, SSD_G, SSD_E)
        dt_col = jnp.transpose(dt5, (0, 2, 3, 1, 4))
        dt_row = jnp.transpose(dt5, (0, 2, 3, 4, 1))
        bias = ssd_dt_bias[l].astype(F32).reshape(2, SSD_G, SSD_E)
        a = -jnp.exp(ssd_a_log[l].astype(F32)).reshape(2, SSD_G, SSD_E)
        d_full = jnp.repeat(ssd_d[l].astype(F32), SSD_P).reshape(SSD_G, 1, SSD_GW)
        yf, ybk = _ssd(xconv, dt_col, dt_row, bias[:, :, None, :], bias[:, :, :, None],
                       a[:, :, None, :], a[:, :, :, None], d_full, ctx_len, Y_DTYPE)

        row_off = ctx_len if last else 0
        u = _merge(ya, yb, yf, ybk, p3, ssd_norm[l], w_br_a[l].astype(BF16), w_br_b[l].astype(BF16),
                   w_br_c[l].astype(BF16), row_off)
        if last:
            (xo,) = _out_ln(u, w_out[l].astype(BF16), xc, tabs[l], ln_g[l], ln_b[l], None, nb,
                            ctx_len, row_off)
            return xo
        xc, xm = _out_ln(u, w_out[l].astype(BF16), xc, tabs[l], ln_g[l], ln_b[l], tabs[l + 1], nb,
                         ctx_len, row_off)
```

```python
import functools
import math

import numpy as np
import jax
import jax.numpy as jnp
from jax import lax
from jax.experimental import pallas as pl
from jax.experimental.pallas import tpu as pltpu

F32 = jnp.float32
BF16 = jnp.bfloat16
HIGHEST = lax.Precision.HIGHEST

D_MODEL = 2048
DEPTH = 2
GRID_W = 64
ROPE_THETA = 10000.0
EPS = 1e-6

MLA_HEADS = 8
MLA_Q_LORA = 512
MLA_KV_LORA = 256
MLA_NOPE = 128
MLA_ROPE = 64
MLA_V = 128
MLA_QK = MLA_NOPE + MLA_ROPE
MLA_QK_PAD = 256
MLA_WIDTH = MLA_HEADS * MLA_V

GQA_HEADS = 8
GQA_KV_HEADS = 2
GQA_GROUP = GQA_HEADS // GQA_KV_HEADS
GQA_DIM = 128
GQA_WIDTH = GQA_HEADS * GQA_DIM
GQA_KV_WIDTH = GQA_KV_HEADS * GQA_DIM

SSD_INNER = D_MODEL
SSD_P = 64
SSD_HEADS = SSD_INNER // SSD_P
SSD_G = 4
SSD_E = SSD_HEADS // SSD_G
SSD_N = 128
SSD_CONV = 5
SSD_Q = 128
SSD_GW = SSD_E * SSD_P
SSD_CONV_DIM = SSD_INNER + 2 * SSD_G * SSD_N

N_BRANCH = 3
IN_SPLITS = (MLA_Q_LORA, MLA_KV_LORA, MLA_ROPE, MLA_WIDTH, GQA_WIDTH, GQA_KV_WIDTH, GQA_KV_WIDTH,
             GQA_WIDTH, SSD_INNER, SSD_CONV_DIM, 2 * SSD_HEADS, N_BRANCH * D_MODEL)
IN_NAMES = ('cq', 'ckv', 'kr', 'ga', 'gq', 'gk', 'gv', 'gb', 'z', 'xbc', 'dtr', 'mg')
IN_OFFSETS = dict(zip(IN_NAMES, np.concatenate([[0], np.cumsum(IN_SPLITS)[:-1]]).tolist()))
IN_WIDTHS = dict(zip(IN_NAMES, IN_SPLITS))
P_ORDER = ('mg', 'z', 'xbc', 'ga', 'gq', 'gb', 'cq', 'ckv', 'gk', 'gv', 'kr', 'dtr')
P_OFF = {}
_o = 0
for _n in P_ORDER:
    P_OFF[_n] = _o
    _o += IN_WIDTHS[_n]
P_USED = _o
P_TN = 512
P_WIDTH = -(-P_USED // P_TN) * P_TN

DEEPNORM_ALPHA = (2 * DEPTH) ** 0.25

VMEM_LIMIT = 56 * 2 ** 20


def _params(*sem):
    return pltpu.CompilerParams(dimension_semantics=sem, vmem_limit_bytes=VMEM_LIMIT)


def _tile(n, target, align=8):
    t = min(n, target)
    while t > align and (n % t or t % align):
        t -= align
    assert n % t == 0, (n, target)
    return t


def _silu(v):
    return v * jax.nn.sigmoid(v)


def _softplus(v):
    return jnp.maximum(v, 0.0) + jnp.log1p(jnp.exp(-jnp.abs(v)))


def _layer_norm(v):
    mu = jnp.mean(v, axis=-1, keepdims=True)
    vc = v - mu
    var = jnp.mean(vc * vc, axis=-1, keepdims=True)
    return vc * lax.rsqrt(var + EPS)


def _rms(v):
    return v * lax.rsqrt(jnp.mean(v * v, axis=-1, keepdims=True) + EPS)


def _mod_kernel(c_ref, w_ref, b_ref, o_ref):
    a = _silu(c_ref[...]).astype(BF16)
    o_ref[...] = jnp.dot(a, w_ref[...].astype(BF16), preferred_element_type=F32) + b_ref[...]


def _mod_rows(c_rows, w_mod, b_mod):
    r, d = c_rows.shape
    n = w_mod.shape[1]
    tn = 512
    return pl.pallas_call(
        _mod_kernel,
        grid=(n // tn,),
        in_specs=[pl.BlockSpec((r, d), lambda j: (0, 0)),
                  pl.BlockSpec((d, tn), lambda j: (0, j)),
                  pl.BlockSpec((1, tn), lambda j: (0, j))],
        out_specs=pl.BlockSpec((r, tn), lambda j: (0, j)),
        out_shape=jax.ShapeDtypeStruct((r, n), F32),
        compiler_params=_params("arbitrary"),
        name="mod_rows",
    )(c_rows, w_mod, b_mod.reshape(1, n))


def _ln_mod_kernel(x_ref, sh_ref, sc_ref, o_ref):
    y = _layer_norm(x_ref[0])
    o_ref[0] = (y * (1.0 + sc_ref[0]) + sh_ref[0]).astype(o_ref.dtype)


def _mod_row_index(nb, ctx_tiles):
    return lambda b, i: jnp.where(i < ctx_tiles, nb, b)


def _ln_mod(xc, tab, nb, ctx_len):
    b, t, d = xc.shape
    tm = _tile(ctx_len, 256)
    row = _mod_row_index(nb, ctx_len // tm)
    return pl.pallas_call(
        _ln_mod_kernel,
        grid=(b, t // tm),
        in_specs=[pl.BlockSpec((1, tm, d), lambda bi, i: (bi, i, 0)),
                  pl.BlockSpec((1, 1, d), lambda bi, i: (row(bi, i), 0, 0)),
                  pl.BlockSpec((1, 1, d), lambda bi, i: (row(bi, i), 0, 1))],
        out_specs=pl.BlockSpec((1, tm, d), lambda bi, i: (bi, i, 0)),
        out_shape=jax.ShapeDtypeStruct((b, t, d), BF16),
        compiler_params=_params("parallel", "parallel"),
        name="ln_mod",
    )(xc, tab, tab)


def _matmul_kernel(x_ref, w_ref, o_ref):
    o_ref[...] = jnp.dot(x_ref[...], w_ref[...], preferred_element_type=F32).astype(o_ref.dtype)


def _in_proj(xm2, wp, out_dtype):
    m, k = xm2.shape
    n = wp.shape[1]
    tm = _tile(m, 1024)
    tn = P_TN
    return pl.pallas_call(
        _matmul_kernel,
        grid=(m // tm, n // tn),
        in_specs=[pl.BlockSpec((tm, k), lambda i, j: (i, 0)),
                  pl.BlockSpec((k, tn), lambda i, j: (0, j))],
        out_specs=pl.BlockSpec((tm, tn), lambda i, j: (i, j)),
        out_shape=jax.ShapeDtypeStruct((m, n), out_dtype),
        compiler_params=_params("parallel", "arbitrary"),
        name="in_proj",
    )(xm2, wp)


def _mla_prep_kernel(cq_ref, ckv_ref, kr_ref, qn_ref, kvn_ref, wqm_ref, wqr_ref, wkv_ref,
                     cosq_ref, sinq_ref, ck_ref, s1_ref, s2_ref, qa_ref, ka_ref, va_ref):
    cqn = (_rms(cq_ref[0].astype(F32)) * qn_ref[...]).astype(BF16)
    qm = jnp.dot(cqn, wqm_ref[...], preferred_element_type=F32)
    qr = jnp.dot(cqn, wqr_ref[...], preferred_element_type=F32)
    cosq = cosq_ref[...]
    sinq = sinq_ref[...]
    for h in range(MLA_HEADS):
        sl = slice(h * MLA_QK_PAD, (h + 1) * MLA_QK_PAD)
        qa_ref[0, h] = (qm[:, sl] * cosq + qr[:, sl] * sinq).astype(qa_ref.dtype)
    ckvn = (_rms(ckv_ref[0].astype(F32)) * kvn_ref[...]).astype(BF16)
    kv = jnp.dot(ckvn, wkv_ref[...], preferred_element_type=F32)
    kr = kr_ref[0].astype(F32)
    kpe = (kr * ck_ref[...] + pltpu.roll(kr, 128 - 16, axis=1) * s1_ref[...]
           + pltpu.roll(kr, 16, axis=1) * s2_ref[...])
    kpe_t = kpe.T.astype(ka_ref.dtype)
    for h in range(MLA_HEADS):
        base = h * (MLA_NOPE + MLA_V)
        ka_ref[0, h, 0:MLA_NOPE, :] = kv[:, base:base + MLA_NOPE].T.astype(ka_ref.dtype)
        ka_ref[0, h, MLA_NOPE:MLA_QK_PAD, :] = kpe_t
        va_ref[0, h] = kv[:, base + MLA_NOPE:base + MLA_NOPE + MLA_V].astype(va_ref.dtype)


def _mla_prep(p3, qn, kvn, wqm, wqr, wkv, tabs, ctx_len):
    b, t, _ = p3.shape
    tm = _tile(ctx_len, 256)
    cosq, sinq, ck, s1, s2 = tabs

    def col(name, width):
        blk = P_OFF[name] // width
        return pl.BlockSpec((1, tm, width), lambda bi, i: (bi, i, blk))

    def full(a):
        return pl.BlockSpec(a.shape, lambda bi, i: (0,) * a.ndim)

    def rows(a):
        return pl.BlockSpec((tm, a.shape[1]), lambda bi, i: (i, 0))

    hm = lambda w: pl.BlockSpec((1, MLA_HEADS, tm, w), lambda bi, i: (bi, 0, i, 0))
    return pl.pallas_call(
        _mla_prep_kernel,
        grid=(b, t // tm),
        in_specs=[col('cq', MLA_Q_LORA), col('ckv', MLA_KV_LORA), col('kr', 128),
                  full(qn), full(kvn), full(wqm), full(wqr), full(wkv),
                  rows(cosq), rows(sinq), rows(ck), rows(s1), rows(s2)],
        out_specs=[hm(MLA_QK_PAD),
                   pl.BlockSpec((1, MLA_HEADS, MLA_QK_PAD, tm), lambda bi, i: (bi, 0, 0, i)),
                   hm(MLA_V)],
        out_shape=[jax.ShapeDtypeStruct((b, MLA_HEADS, t, MLA_QK_PAD), BF16),
                   jax.ShapeDtypeStruct((b, MLA_HEADS, MLA_QK_PAD, t), BF16),
                   jax.ShapeDtypeStruct((b, MLA_HEADS, t, MLA_V), BF16)],
        compiler_params=_params("parallel", "parallel"),
        name="mla_prep",
    )(p3, p3, p3, qn, kvn, wqm, wqr, wkv, cosq, sinq, ck, s1, s2)


def _rope128(y, c, s1, s2):
    return y * c + pltpu.roll(y, 128 - 32, axis=1) * s1 + pltpu.roll(y, 32, axis=1) * s2


def _gqa_prep_kernel(gq_ref, gk_ref, gv_ref, qn_ref, kn_ref, cq_ref, s1q_ref, s2q_ref,
                     ck_ref, s1k_ref, s2k_ref, qb_ref, kb_ref, vb_ref):
    gq = gq_ref[0].astype(F32)
    for h in range(GQA_HEADS):
        y = _rms(gq[:, h * GQA_DIM:(h + 1) * GQA_DIM]) * qn_ref[...]
        qb_ref[0, h] = _rope128(y, cq_ref[...], s1q_ref[...], s2q_ref[...]).astype(qb_ref.dtype)
    gk = gk_ref[0].astype(F32)
    gv = gv_ref[0]
    for h in range(GQA_KV_HEADS):
        y = _rms(gk[:, h * GQA_DIM:(h + 1) * GQA_DIM]) * kn_ref[...]
        kb_ref[0, h] = _rope128(y, ck_ref[...], s1k_ref[...], s2k_ref[...]).T.astype(kb_ref.dtype)
        vb_ref[0, h] = gv[:, h * GQA_DIM:(h + 1) * GQA_DIM].astype(vb_ref.dtype)


def _gqa_prep(p3, qn, kn, tabs, ctx_len):
    b, t, _ = p3.shape
    tm = _tile(ctx_len, 256)

    def col(name, width):
        blk = P_OFF[name] // width
        return pl.BlockSpec((1, tm, width), lambda bi, i: (bi, i, blk))

    def full(a):
        return pl.BlockSpec(a.shape, lambda bi, i: (0,) * a.ndim)

    def rows(a):
        return pl.BlockSpec((tm, a.shape[1]), lambda bi, i: (i, 0))

    hm = lambda nh: pl.BlockSpec((1, nh, tm, GQA_DIM), lambda bi, i: (bi, 0, i, 0))
    return pl.pallas_call(
        _gqa_prep_kernel,
        grid=(b, t // tm),
        in_specs=[col('gq', GQA_WIDTH), col('gk', GQA_KV_WIDTH), col('gv', GQA_KV_WIDTH),
                  full(qn), full(kn)] + [rows(a) for a in tabs],
        out_specs=[hm(GQA_HEADS),
                   pl.BlockSpec((1, GQA_KV_HEADS, GQA_DIM, tm), lambda bi, i: (bi, 0, 0, i)),
                   hm(GQA_KV_HEADS)],
        out_shape=[jax.ShapeDtypeStruct((b, GQA_HEADS, t, GQA_DIM), BF16),
                   jax.ShapeDtypeStruct((b, GQA_KV_HEADS, GQA_DIM, t), BF16),
                   jax.ShapeDtypeStruct((b, GQA_KV_HEADS, t, GQA_DIM), BF16)],
        compiler_params=_params("parallel", "parallel"),
        name="gqa_prep",
    )(p3, p3, p3, qn, kn, *tabs)


ATTN_KC = 256
ATTN_RB = 128
LOG2E = math.log2(math.e)


def _attn_rows(q, kt_ref, v_ref, s_sc, p_sc, nkeys):
    rows = q.shape[0]
    for c0 in range(0, nkeys, ATTN_KC):
        s_sc[:, c0:c0 + ATTN_KC] = jnp.dot(q, kt_ref[0, 0, :, c0:c0 + ATTN_KC],
                                           preferred_element_type=F32)
    inv_l = []
    for r0 in range(0, rows, ATTN_RB):
        mpart = s_sc[r0:r0 + ATTN_RB, 0:128]
        for j in range(128, nkeys, 128):
            mpart = jnp.maximum(mpart, s_sc[r0:r0 + ATTN_RB, j:j + 128])
        m = jnp.broadcast_to(jnp.max(mpart, axis=-1, keepdims=True), (ATTN_RB, 128))
        lpart = jnp.zeros((ATTN_RB, 128), F32)
        for j in range(0, nkeys, 128):
            p = jnp.exp2(s_sc[r0:r0 + ATTN_RB, j:j + 128] - m)
            lpart = lpart + p
            p_sc[r0:r0 + ATTN_RB, j:j + 128] = p.astype(p_sc.dtype)
        inv_l.append(1.0 / jnp.sum(lpart, axis=-1, keepdims=True))
    o = jnp.dot(p_sc[:, 0:nkeys], v_ref[0, 0, 0:nkeys, :], preferred_element_type=F32)
    return o * jnp.concatenate(inv_l, axis=0)


def _attn_kernel(q_ref, kt_ref, v_ref, o_ref, s_sc, p_sc, *, group, tq, ctx_len, ctx_q_tiles, q_off):
    dk = q_ref.shape[-1]
    dv = v_ref.shape[-1]
    t = v_ref.shape[2]
    q = q_ref[0].reshape(group * tq, dk)

    def run(nkeys):
        o = _attn_rows(q, kt_ref, v_ref, s_sc, p_sc, nkeys)
        for g in range(group):
            o_ref[0, :, g * dv:(g + 1) * dv] = o[g * tq:(g + 1) * tq, :].astype(o_ref.dtype)

    if ctx_q_tiles > q_off:
        is_ctx = pl.program_id(2) + q_off < ctx_q_tiles
        pl.when(is_ctx)(lambda: run(ctx_len))
        pl.when(jnp.logical_not(is_ctx))(lambda: run(t))
    else:
        run(t)


def _attention(q, kt, v, ctx_len, tq, skip_ctx_queries, out_dtype):
    b, hq, t, dk = q.shape
    hkv, dv = kt.shape[1], v.shape[-1]
    group = hq // hkv
    tq = _tile(ctx_len, tq)
    ctx_q_tiles = ctx_len // tq
    q_off = ctx_q_tiles if skip_ctx_queries else 0
    nq = t // tq - q_off
    kern = functools.partial(_attn_kernel, group=group, tq=tq, ctx_len=ctx_len,
                             ctx_q_tiles=ctx_q_tiles, q_off=q_off)
    rows = group * tq
    assert rows % ATTN_RB == 0 and t % ATTN_KC == 0 and ctx_len % ATTN_KC == 0
    return pl.pallas_call(
        kern,
        grid=(b, hkv, nq),
        in_specs=[pl.BlockSpec((1, group, tq, dk), lambda bi, h, i: (bi, h, i + q_off, 0)),
                  pl.BlockSpec((1, 1, dk, t), lambda bi, h, i: (bi, h, 0, 0)),
                  pl.BlockSpec((1, 1, t, dv), lambda bi, h, i: (bi, h, 0, 0))],
        out_specs=pl.BlockSpec((1, tq, group * dv), lambda bi, h, i: (bi, i, h)),
        out_shape=jax.ShapeDtypeStruct((b, nq * tq, hq * dv), out_dtype),
        scratch_shapes=[pltpu.VMEM((rows, t), F32), pltpu.VMEM((rows, t), BF16)],
        compiler_params=_params("parallel", "parallel", "arbitrary"),
        name="attention_dk%d" % dk,
    )(q, kt, v)


def _conv_kernel(x_ref, w_ref, b_ref, o_ref, pad_sc, *, ctx_len, rows):
    t = x_ref.shape[1]
    nch = x_ref.shape[2]
    halo = 8
    segs = ((0, ctx_len), (ctx_len, t))
    zeros = jnp.zeros((halo, nch), F32)
    for si, (lo, hi) in enumerate(segs):
        pad_sc[lo + si * halo:lo + (si + 1) * halo, :] = zeros
        for r0 in range(lo, hi, rows):
            pad_sc[r0 + (si + 1) * halo:r0 + (si + 1) * halo + rows, :] = x_ref[0, r0:r0 + rows, :].astype(F32)
    pad_sc[t + 2 * halo:t + 3 * halo, :] = zeros
    w = w_ref[...]
    bias = b_ref[...]
    for si, (lo, hi) in enumerate(segs):
        for r0 in range(lo, hi, rows):
            base = r0 + (si + 1) * halo - SSD_CONV // 2
            acc = bias + w[0:1, :] * pad_sc[base:base + rows, :]
            for kk in range(1, SSD_CONV):
                acc = acc + w[kk:kk + 1, :] * pad_sc[base + kk:base + kk + rows, :]
            o_ref[0, r0:r0 + rows, :] = _silu(acc).astype(o_ref.dtype)


def _conv(p3, conv_w, conv_b, ctx_len, out_dtype):
    b, t, _ = p3.shape
    nch = 256
    rows = _tile(ctx_len, 256)
    blk0 = P_OFF['xbc'] // nch
    kern = functools.partial(_conv_kernel, ctx_len=ctx_len, rows=rows)
    return pl.pallas_call(
        kern,
        grid=(b, SSD_CONV_DIM // nch),
        in_specs=[pl.BlockSpec((1, t, nch), lambda bi, j: (bi, 0, blk0 + j)),
                  pl.BlockSpec((SSD_CONV, nch), lambda bi, j: (0, j)),
                  pl.BlockSpec((1, nch), lambda bi, j: (0, j))],
        out_specs=pl.BlockSpec((1, t, nch), lambda bi, j: (bi, 0, j)),
        out_shape=jax.ShapeDtypeStruct((b, t, SSD_CONV_DIM), out_dtype),
        scratch_shapes=[pltpu.VMEM((t + 24, nch), F32)],
        compiler_params=_params("parallel", "parallel"),
        name="ssd_conv",
    )(p3, conv_w, conv_b.reshape(1, SSD_CONV_DIM))


def _ssd_direction(xs, bm, cm, dt_col_raw, dt_row_raw, bias_col, bias_row, a_col, a_row, h_ref,
                   backward):
    q = xs.shape[0]
    ri = lax.broadcasted_iota(jnp.int32, (q, q), 0)
    ci = lax.broadcasted_iota(jnp.int32, (q, q), 1)
    low = ri >= ci
    keep = (ri <= ci) if backward else low
    tri_col = jnp.where(keep, 1.0, 0.0).astype(F32)
    tri_row = jnp.where((ri >= ci) if backward else (ri <= ci), 1.0, 0.0).astype(F32)
    expand = jnp.where(lax.broadcasted_iota(jnp.int32, (SSD_E, SSD_GW), 1) // SSD_P
                       == lax.broadcasted_iota(jnp.int32, (SSD_E, SSD_GW), 0), 1.0, 0.0).astype(F32)

    dt_c = _softplus(dt_col_raw + bias_col)
    dt_r = _softplus(dt_row_raw + bias_row)
    cum_c = jnp.dot(tri_col, dt_c * a_col, precision=HIGHEST, preferred_element_type=F32)
    cum_r = jnp.dot(dt_r * a_row, tri_row, precision=HIGHEST, preferred_element_type=F32)
    cum_full = jnp.dot(cum_c, expand, precision=HIGHEST, preferred_element_type=F32)
    dt_full = jnp.dot(dt_c, expand, precision=HIGHEST, preferred_element_type=F32)
    total = cum_full[0:1, :] if backward else cum_full[q - 1:q, :]

    xdt = xs * dt_full
    cmb = cm.astype(BF16)
    cb = lax.dot_general(cmb, bm.astype(BF16), (((1,), (1,)), ((), ())), preferred_element_type=F32)
    h = h_ref[...]
    y_off = jnp.dot(cmb, h.astype(BF16), preferred_element_type=F32) * jnp.exp(cum_full)
    wgt = (xdt * jnp.exp(total - cum_full)).astype(BF16)
    h_ref[...] = jnp.exp(total) * h + jnp.dot(bm.T.astype(BF16), wgt, preferred_element_type=F32)

    xdt_b = xdt.astype(BF16)
    lane = lax.broadcasted_iota(jnp.int32, (q, 2 * SSD_P), 1)
    parts = []
    for pair in range(SSD_E // 2):
        x_pair = xdt_b[:, pair * 2 * SSD_P:(pair + 1) * 2 * SSD_P]
        ys = []
        for e in (2 * pair, 2 * pair + 1):
            seg = cum_c[:, e:e + 1] - cum_r[e:e + 1, :]
            dec = jnp.exp(jnp.where(keep, seg, -jnp.inf))
            ys.append(jnp.dot((cb * dec).astype(BF16), x_pair, preferred_element_type=F32))
        parts.append(jnp.where(lane < SSD_P, ys[0], ys[1]))
    return jnp.concatenate(parts, axis=1) + y_off


def _ssd_kernel(xf_ref, bf_ref, cf_ref, xb_ref, bb_ref, cb_ref, dcf_ref, drf_ref, dcb_ref, drb_ref,
                bias_c_ref, bias_r_ref, a_c_ref, a_r_ref, d_ref, yf_ref, yb_ref, hf_sc, hb_sc):
    @pl.when(pl.program_id(2) == 0)
    def _():
        hf_sc[...] = jnp.zeros(hf_sc.shape, F32)
        hb_sc[...] = jnp.zeros(hb_sc.shape, F32)

    xs = xf_ref[0].astype(F32)
    yf = _ssd_direction(xs, bf_ref[0].astype(F32), cf_ref[0].astype(F32), dcf_ref[0, 0, 0], drf_ref[0, 0, 0],
                        bias_c_ref[0, 0], bias_r_ref[0, 0], a_c_ref[0, 0], a_r_ref[0, 0], hf_sc, False)
    yf_ref[0] = (yf + d_ref[0] * xs).astype(yf_ref.dtype)
    yb = _ssd_direction(xb_ref[0].astype(F32), bb_ref[0].astype(F32), cb_ref[0].astype(F32),
                        dcb_ref[0, 0, 0], drb_ref[0, 0, 0],
                        bias_c_ref[1, 0], bias_r_ref[1, 0], a_c_ref[1, 0], a_r_ref[1, 0], hb_sc, True)
    yb_ref[0] = yb.astype(yb_ref.dtype)


def _ssd(xc, dt_col, dt_row, bias_c, bias_r, a_c, a_r, d_full, ctx_len, out_dtype):
    b, t, _ = xc.shape
    nchunk = t // SSD_Q
    nctx = ctx_len // SSD_Q
    bblk = SSD_INNER // SSD_N
    cblk = bblk + SSD_G

    def bidx(c):
        return jnp.where(c < nctx, nctx - 1 - c, nchunk - 1 - (c - nctx))

    fx = lambda bi, g, c: (bi, c, g)
    fb = lambda bi, g, c: (bi, c, bblk + g)
    fc = lambda bi, g, c: (bi, c, cblk + g)
    bx = lambda bi, g, c: (bi, bidx(c), g)
    bb = lambda bi, g, c: (bi, bidx(c), bblk + g)
    bc = lambda bi, g, c: (bi, bidx(c), cblk + g)
    small = lambda a: pl.BlockSpec((2, 1) + a.shape[2:], lambda bi, g, c: (0, g, 0, 0))
    return pl.pallas_call(
        _ssd_kernel,
        grid=(b, SSD_G, nchunk),
        in_specs=[pl.BlockSpec((1, SSD_Q, SSD_GW), fx), pl.BlockSpec((1, SSD_Q, SSD_N), fb),
                  pl.BlockSpec((1, SSD_Q, SSD_N), fc),
                  pl.BlockSpec((1, SSD_Q, SSD_GW), bx), pl.BlockSpec((1, SSD_Q, SSD_N), bb),
                  pl.BlockSpec((1, SSD_Q, SSD_N), bc),
                  pl.BlockSpec((1, 1, 1, SSD_Q, SSD_E), lambda bi, g, c: (bi, 0, g, c, 0)),
                  pl.BlockSpec((1, 1, 1, SSD_E, SSD_Q), lambda bi, g, c: (bi, 0, g, 0, c)),
                  pl.BlockSpec((1, 1, 1, SSD_Q, SSD_E), lambda bi, g, c: (bi, 1, g, bidx(c), 0)),
                  pl.BlockSpec((1, 1, 1, SSD_E, SSD_Q), lambda bi, g, c: (bi, 1, g, 0, bidx(c))),
                  small(bias_c), small(bias_r), small(a_c), small(a_r),
                  pl.BlockSpec((1, 1, SSD_GW), lambda bi, g, c: (g, 0, 0))],
        out_specs=[pl.BlockSpec((1, SSD_Q, SSD_GW), fx), pl.BlockSpec((1, SSD_Q, SSD_GW), bx)],
        out_shape=[jax.ShapeDtypeStruct((b, t, SSD_INNER), out_dtype)] * 2,
        scratch_shapes=[pltpu.VMEM((SSD_N, SSD_GW), F32), pltpu.VMEM((SSD_N, SSD_GW), F32)],
        compiler_params=_params("parallel", "parallel", "arbitrary"),
        name="ssd_scan",
    )(xc, xc, xc, xc, xc, xc, dt_col, dt_row, dt_col, dt_row, bias_c, bias_r, a_c, a_r, d_full)


def _merge_kernel(ya_ref, ga_ref, yb_ref, gb_ref, yf_ref, ybk_ref, z_ref, nrm_ref,
                  m0_ref, m1_ref, m2_ref, wa_ref, wb_ref, wc_ref, u_ref, a_sc, b_sc, c_sc):
    @pl.when(pl.program_id(2) == 0)
    def _():
        a_sc[...] = (ya_ref[0].astype(F32) * _silu(ga_ref[0].astype(F32))).astype(BF16)
        b_sc[...] = (yb_ref[0].astype(F32) * _silu(gb_ref[0].astype(F32))).astype(BF16)
        v = (yf_ref[0].astype(F32) + ybk_ref[0].astype(F32)) * _silu(z_ref[0].astype(F32))
        for g in range(SSD_G):
            sl = slice(g * SSD_GW, (g + 1) * SSD_GW)
            c_sc[:, sl] = (_rms(v[:, sl]) * nrm_ref[:, sl]).astype(BF16)

    br_a = jnp.dot(a_sc[...], wa_ref[...], preferred_element_type=F32)
    br_b = jnp.dot(b_sc[...], wb_ref[...], preferred_element_type=F32)
    br_c = jnp.dot(c_sc[...], wc_ref[...], preferred_element_type=F32)
    u = (jax.nn.sigmoid(m0_ref[0].astype(F32)) * br_a + jax.nn.sigmoid(m1_ref[0].astype(F32)) * br_b
         + jax.nn.sigmoid(m2_ref[0].astype(F32)) * br_c)
    u_ref[0] = u.astype(u_ref.dtype)


def _merge(ya, yb, yf, ybk, p3, ssd_norm, wa, wb, wc, row_off):
    b, rows, _ = ya.shape
    tm = _tile(rows, 256)
    tn = 512
    d = D_MODEL
    ro = row_off // tm
    nj = d // tn

    def col(name, width):
        blk = P_OFF[name] // width
        return pl.BlockSpec((1, tm, width), lambda bi, i, j: (bi, i + ro, blk))

    def mgate(k):
        return pl.BlockSpec((1, tm, tn), lambda bi, i, j: (bi, i + ro, k * nj + j))

    loc = lambda w: pl.BlockSpec((1, tm, w), lambda bi, i, j: (bi, i, 0))
    glob = lambda w: pl.BlockSpec((1, tm, w), lambda bi, i, j: (bi, i + ro, 0))
    wspec = lambda k: pl.BlockSpec((k, tn), lambda bi, i, j: (0, j))
    return pl.pallas_call(
        _merge_kernel,
        grid=(b, rows // tm, nj),
        in_specs=[loc(MLA_WIDTH), col('ga', MLA_WIDTH), loc(GQA_WIDTH), col('gb', GQA_WIDTH),
                  glob(SSD_INNER), glob(SSD_INNER), col('z', SSD_INNER),
                  pl.BlockSpec((1, SSD_INNER), lambda bi, i, j: (0, 0)),
                  mgate(0), mgate(1), mgate(2), wspec(MLA_WIDTH), wspec(GQA_WIDTH), wspec(SSD_INNER)],
        out_specs=pl.BlockSpec((1, tm, tn), lambda bi, i, j: (bi, i, j)),
        out_shape=jax.ShapeDtypeStruct((b, rows, d), BF16),
        scratch_shapes=[pltpu.VMEM((tm, MLA_WIDTH), BF16), pltpu.VMEM((tm, GQA_WIDTH), BF16),
                        pltpu.VMEM((tm, SSD_INNER), BF16)],
        compiler_params=_params("parallel", "parallel", "arbitrary"),
        name="merge",
    )(ya, p3, yb, p3, yf, ybk, p3, ssd_norm.reshape(1, SSD_INNER), p3, p3, p3, wa, wb, wc)


def _out_ln_kernel(u_ref, w_ref, x_ref, gate_ref, g_ref, b_ref, *rest, with_next):
    out = jnp.dot(u_ref[0], w_ref[...], preferred_element_type=F32)
    r = DEEPNORM_ALPHA * x_ref[0] + gate_ref[0] * out
    xn = _layer_norm(r) * g_ref[...] + b_ref[...]
    if with_next:
        sh_ref, sc_ref, xo_ref, xm_ref = rest
        xo_ref[0] = xn
        xm_ref[0] = (_layer_norm(xn) * (1.0 + sc_ref[0]) + sh_ref[0]).astype(xm_ref.dtype)
    else:
        (xo_ref,) = rest
        xo_ref[0] = xn


def _out_ln(u, w_out, xc, tab, ln_g, ln_b, next_tab, nb, ctx_len, row_off):
    b, rows, d = u.shape
    tm = _tile(ctx_len, 256)
    ro = row_off // tm
    row = _mod_row_index(nb, ctx_len // tm)
    with_next = next_tab is not None
    loc = pl.BlockSpec((1, tm, d), lambda bi, i: (bi, i, 0))
    vec = pl.BlockSpec((1, d), lambda bi, i: (0, 0))
    in_specs = [loc, pl.BlockSpec((d, d), lambda bi, i: (0, 0)),
                pl.BlockSpec((1, tm, d), lambda bi, i: (bi, i + ro, 0)),
                pl.BlockSpec((1, 1, d), lambda bi, i: (row(bi, i + ro), 0, 2)), vec, vec]
    args = [u, w_out, xc, tab, ln_g.reshape(1, d), ln_b.reshape(1, d)]
    out_specs = [loc]
    out_shape = [jax.ShapeDtypeStruct((b, rows, d), F32)]
    if with_next:
        in_specs += [pl.BlockSpec((1, 1, d), lambda bi, i: (row(bi, i + ro), 0, 0)),
                     pl.BlockSpec((1, 1, d), lambda bi, i: (row(bi, i + ro), 0, 1))]
        args += [next_tab, next_tab]
        out_specs.append(loc)
        out_shape.append(jax.ShapeDtypeStruct((b, rows, d), BF16))
    return pl.pallas_call(
        functools.partial(_out_ln_kernel, with_next=with_next),
        grid=(b, rows // tm),
        in_specs=in_specs,
        out_specs=out_specs,
        out_shape=out_shape,
        compiler_params=_params("parallel", "parallel"),
        name="out_ln",
    )(*args)


def _rope_angles(rows, dim):
    row, col = jnp.meshgrid(jnp.arange(rows, dtype=F32), jnp.arange(GRID_W, dtype=F32), indexing='ij')
    half = dim // 2
    inv_freq = ROPE_THETA ** (-jnp.arange(0, half, 2, dtype=F32) / half)
    ang_r = row.reshape(-1, 1) * inv_freq
    ang_c = col.reshape(-1, 1) * inv_freq
    return jnp.concatenate([ang_r, ang_r, ang_c, ang_c], axis=-1)


def _rope_tables(seq, ctx_len, dim):
    ang = _rope_angles(seq // GRID_W, dim)
    cos = jnp.concatenate([jnp.ones((ctx_len, dim), F32), jnp.cos(ang)], axis=0)
    sin = jnp.concatenate([jnp.zeros((ctx_len, dim), F32), jnp.sin(ang)], axis=0)
    return cos, sin


def _roll_tables(cos, sin, dim, scale):
    t = cos.shape[0]
    quarter = dim // 4
    first = (jnp.arange(dim) % (2 * quarter)) < quarter
    s1 = jnp.where(first, -sin, 0.0)
    s2 = jnp.where(first, 0.0, sin)
    pad = lambda a: jnp.pad(a * scale, ((0, 0), (0, 128 - dim)))
    return pad(cos), pad(s1), pad(s2)


def _rot_matrix(dim):
    quarter = dim // 4
    r = np.zeros((dim, dim), np.float32)
    for i in range(dim):
        blk = i // quarter
        if blk % 2 == 0:
            r[i + quarter, i] = -1.0
        else:
            r[i - quarter, i] = 1.0
    return jnp.asarray(r)


def _mla_q_weights(w_uq):
    w = w_uq.reshape(MLA_Q_LORA, MLA_HEADS, MLA_QK)
    nope, pe = w[..., :MLA_NOPE], w[..., MLA_NOPE:]
    pe_rot = jnp.einsum('khd,de->khe', pe, _rot_matrix(MLA_ROPE), precision=HIGHEST)
    zpad = jnp.zeros((MLA_Q_LORA, MLA_HEADS, MLA_QK_PAD - MLA_QK), F32)
    main = jnp.concatenate([nope, pe, zpad], axis=-1)
    rot = jnp.concatenate([jnp.zeros_like(nope), pe_rot, zpad], axis=-1)
    return (main.reshape(MLA_Q_LORA, -1).astype(BF16), rot.reshape(MLA_Q_LORA, -1).astype(BF16))


def _permute_w_in(w):
    wb = w.astype(BF16)
    parts = [wb[:, IN_OFFSETS[n]:IN_OFFSETS[n] + IN_WIDTHS[n]] for n in P_ORDER]
    parts.append(jnp.zeros((w.shape[0], P_WIDTH - P_USED), BF16))
    return jnp.concatenate(parts, axis=1)


P_DTYPE = F32
Y_DTYPE = F32


def kernel(x, c, ctx, c_ctx, w_mod, b_mod, w_in, mla_q_norm, mla_w_uq, mla_kv_norm, mla_w_ukv,
           gqa_q_norm, gqa_k_norm, ssd_conv_w, ssd_conv_b, ssd_a_log, ssd_dt_bias, ssd_d, ssd_norm,
           w_br_a, w_br_b, w_br_c, w_out, ln_g, ln_b):
    nb, seq, d = x.shape
    ctx_len = ctx.shape[1]
    t = ctx_len + seq
    depth = w_in.shape[0]
    assert d == D_MODEL and nb < 8 and seq % GRID_W == 0
    assert ctx_len % SSD_Q == 0 and seq % SSD_Q == 0

    cos_a, sin_a = _rope_tables(seq, ctx_len, MLA_ROPE)
    cos_b, sin_b = _rope_tables(seq, ctx_len, GQA_DIM)
    sq = MLA_QK ** -0.5 * LOG2E
    zq = jnp.zeros((t, MLA_QK_PAD - MLA_QK), F32)
    cosq = jnp.concatenate([jnp.full((t, MLA_NOPE), sq, F32), cos_a * sq, zq], axis=1)
    sinq = jnp.concatenate([jnp.zeros((t, MLA_NOPE), F32), sin_a * sq, zq], axis=1)
    mla_tabs = (cosq, sinq) + _roll_tables(cos_a, sin_a, MLA_ROPE, 1.0)
    gqa_tabs = (_roll_tables(cos_b, sin_b, GQA_DIM, GQA_DIM ** -0.5 * LOG2E)
                + _roll_tables(cos_b, sin_b, GQA_DIM, 1.0))

    c_rows = jnp.zeros((8, d), F32).at[:nb].set(c).at[nb].set(c_ctx)
    tabs = [_mod_rows(c_rows, w_mod[l], b_mod[l]).reshape(8, 1, 3 * d) for l in range(depth)]

    xc = jnp.concatenate([ctx, x], axis=1)
    xm = _ln_mod(xc, tabs[0], nb, ctx_len)

    for l in range(depth):
        last = l == depth - 1
        wp = _permute_w_in(w_in[l])
        p3 = _in_proj(xm.reshape(nb * t, d), wp, P_DTYPE).reshape(nb, t, P_WIDTH)

        wqm, wqr = _mla_q_weights(mla_w_uq[l])
        qa, ka, va = _mla_prep(p3, mla_q_norm[l].reshape(1, -1), mla_kv_norm[l].reshape(1, -1),
                               wqm, wqr, mla_w_ukv[l].astype(BF16), mla_tabs, ctx_len)
        ya = _attention(qa, ka, va, ctx_len, 256, last, Y_DTYPE)
        qb, kb, vb = _gqa_prep(p3, gqa_q_norm[l].reshape(1, -1), gqa_k_norm[l].reshape(1, -1),
                               gqa_tabs, ctx_len)
        yb = _attention(qb, kb, vb, ctx_len, 128, last, Y_DTYPE)

        xconv = _conv(p3, ssd_conv_w[l], ssd_conv_b[l], ctx_len, Y_DTYPE)
        dtr = p3[:, :, P_OFF['dtr']:P_OFF['dtr'] + 2 * SSD_HEADS].astype(F32)
        dt5 = dtr.reshape(nb, t, 2, SSD_G, SSD_E)
        dt_col = jnp.transpose(dt5, (0, 2, 3, 1, 4))
        dt_row = jnp.transpose(dt5, (0, 2, 3, 4, 1))
        bias = ssd_dt_bias[l].astype(F32).reshape(2, SSD_G, SSD_E)
        a = -jnp.exp(ssd_a_log[l].astype(F32)).reshape(2, SSD_G, SSD_E)
        d_full = jnp.repeat(ssd_d[l].astype(F32), SSD_P).reshape(SSD_G, 1, SSD_GW)
        yf, ybk = _ssd(xconv, dt_col, dt_row, bias[:, :, None, :], bias[:, :, :, None],
                       a[:, :, None, :], a[:, :, :, None], d_full, ctx_len, Y_DTYPE)

        row_off = ctx_len if last else 0
        u = _merge(ya, yb, yf, ybk, p3, ssd_norm[l], w_br_a[l].astype(BF16), w_br_b[l].astype(BF16),
                   w_br_c[l].astype(BF16), row_off)
        if last:
            (xo,) = _out_ln(u, w_out[l].astype(BF16), xc, tabs[l], ln_g[l], ln_b[l], None, nb,
                            ctx_len, row_off)
            return xo
        xc, xm = _out_ln(u, w_out[l].astype(BF16), xc, tabs[l], ln_g[l], ln_b[l], tabs[l + 1], nb,
                         ctx_len, row_off)
```

```python
import functools
import math

import numpy as np
import jax
import jax.numpy as jnp
from jax import lax
from jax.experimental import pallas as pl
from jax.experimental.pallas import tpu as pltpu

F32 = jnp.float32
BF16 = jnp.bfloat16
HIGHEST = lax.Precision.HIGHEST

D_MODEL = 2048
DEPTH = 2
GRID_W = 64
ROPE_THETA = 10000.0
EPS = 1e-6

MLA_HEADS = 8
MLA_Q_LORA = 512
MLA_KV_LORA = 256
MLA_NOPE = 128
MLA_ROPE = 64
MLA_V = 128
MLA_QK = MLA_NOPE + MLA_ROPE
MLA_QK_PAD = 256
MLA_WIDTH = MLA_HEADS * MLA_V

GQA_HEADS = 8
GQA_KV_HEADS = 2
GQA_GROUP = GQA_HEADS // GQA_KV_HEADS
GQA_DIM = 128
GQA_WIDTH = GQA_HEADS * GQA_DIM
GQA_KV_WIDTH = GQA_KV_HEADS * GQA_DIM

SSD_INNER = D_MODEL
SSD_P = 64
SSD_HEADS = SSD_INNER // SSD_P
SSD_G = 4
SSD_E = SSD_HEADS // SSD_G
SSD_N = 128
SSD_CONV = 5
SSD_Q = 128
SSD_ROWS = 256
SSD_GW = SSD_E * SSD_P
SSD_CONV_DIM = SSD_INNER + 2 * SSD_G * SSD_N

N_BRANCH = 3
IN_SPLITS = (MLA_Q_LORA, MLA_KV_LORA, MLA_ROPE, MLA_WIDTH, GQA_WIDTH, GQA_KV_WIDTH, GQA_KV_WIDTH,
             GQA_WIDTH, SSD_INNER, SSD_CONV_DIM, 2 * SSD_HEADS, N_BRANCH * D_MODEL)
IN_NAMES = ('cq', 'ckv', 'kr', 'ga', 'gq', 'gk', 'gv', 'gb', 'z', 'xbc', 'dtr', 'mg')
IN_OFFSETS = dict(zip(IN_NAMES, np.concatenate([[0], np.cumsum(IN_SPLITS)[:-1]]).tolist()))
IN_WIDTHS = dict(zip(IN_NAMES, IN_SPLITS))
P_ORDER = ('mg', 'z', 'xbc', 'ga', 'gq', 'gb', 'cq', 'ckv', 'gk', 'gv', 'kr', 'dtr')
P_OFF = {}
_o = 0
for _n in P_ORDER:
    P_OFF[_n] = _o
    _o += IN_WIDTHS[_n]
P_USED = _o
P_TN = 512
P_WIDTH = -(-P_USED // P_TN) * P_TN

DEEPNORM_ALPHA = (2 * DEPTH) ** 0.25

VMEM_LIMIT = 56 * 2 ** 20


def _params(*sem):
    return pltpu.CompilerParams(dimension_semantics=sem, vmem_limit_bytes=VMEM_LIMIT)


def _tile(n, target, align=8):
    t = min(n, target)
    while t > align and (n % t or t % align):
        t -= align
    assert n % t == 0, (n, target)
    return t


def _silu(v):
    return v * jax.nn.sigmoid(v)


def _softplus(v):
    return jnp.maximum(v, 0.0) + jnp.log1p(jnp.exp(-jnp.abs(v)))


def _layer_norm(v):
    mu = jnp.mean(v, axis=-1, keepdims=True)
    vc = v - mu
    var = jnp.mean(vc * vc, axis=-1, keepdims=True)
    return vc * lax.rsqrt(var + EPS)


def _rms(v):
    return v * lax.rsqrt(jnp.mean(v * v, axis=-1, keepdims=True) + EPS)


def _mod_kernel(c_ref, w_ref, b_ref, o_ref):
    a = _silu(c_ref[...]).astype(BF16)
    o_ref[...] = jnp.dot(a, w_ref[...].astype(BF16), preferred_element_type=F32) + b_ref[...]


def _mod_rows(c_rows, w_mod, b_mod):
    r, d = c_rows.shape
    n = w_mod.shape[1]
    tn = 512
    return pl.pallas_call(
        _mod_kernel,
        grid=(n // tn,),
        in_specs=[pl.BlockSpec((r, d), lambda j: (0, 0)),
                  pl.BlockSpec((d, tn), lambda j: (0, j)),
                  pl.BlockSpec((1, tn), lambda j: (0, j))],
        out_specs=pl.BlockSpec((r, tn), lambda j: (0, j)),
        out_shape=jax.ShapeDtypeStruct((r, n), F32),
        compiler_params=_params("arbitrary"),
        name="mod_rows",
    )(c_rows, w_mod, b_mod.reshape(1, n))


def _ln_mod_kernel(x_ref, sh_ref, sc_ref, o_ref):
    y = _layer_norm(x_ref[0])
    o_ref[0] = (y * (1.0 + sc_ref[0]) + sh_ref[0]).astype(o_ref.dtype)


def _mod_row_index(nb, ctx_tiles):
    return lambda b, i: jnp.where(i < ctx_tiles, nb, b)


def _ln_mod(xc, tab, nb, ctx_len):
    b, t, d = xc.shape
    tm = _tile(ctx_len, 256)
    row = _mod_row_index(nb, ctx_len // tm)
    return pl.pallas_call(
        _ln_mod_kernel,
        grid=(b, t // tm),
        in_specs=[pl.BlockSpec((1, tm, d), lambda bi, i: (bi, i, 0)),
                  pl.BlockSpec((1, 1, d), lambda bi, i: (row(bi, i), 0, 0)),
                  pl.BlockSpec((1, 1, d), lambda bi, i: (row(bi, i), 0, 1))],
        out_specs=pl.BlockSpec((1, tm, d), lambda bi, i: (bi, i, 0)),
        out_shape=jax.ShapeDtypeStruct((b, t, d), BF16),
        compiler_params=_params("parallel", "parallel"),
        name="ln_mod",
    )(xc, tab, tab)


def _matmul_kernel(x_ref, w_ref, o_ref):
    o_ref[...] = jnp.dot(x_ref[...], w_ref[...], preferred_element_type=F32).astype(o_ref.dtype)


def _in_proj(xm2, wp, out_dtype):
    m, k = xm2.shape
    n = wp.shape[1]
    tm = _tile(m, 1024)
    tn = P_TN
    return pl.pallas_call(
        _matmul_kernel,
        grid=(m // tm, n // tn),
        in_specs=[pl.BlockSpec((tm, k), lambda i, j: (i, 0)),
                  pl.BlockSpec((k, tn), lambda i, j: (0, j))],
        out_specs=pl.BlockSpec((tm, tn), lambda i, j: (i, j)),
        out_shape=jax.ShapeDtypeStruct((m, n), out_dtype),
        compiler_params=_params("parallel", "arbitrary"),
        name="in_proj",
    )(xm2, wp)


def _mla_prep_kernel(cq_ref, ckv_ref, kr_ref, qn_ref, kvn_ref, wqm_ref, wqr_ref, wkv_ref,
                     cosq_ref, sinq_ref, ck_ref, s1_ref, s2_ref, qa_ref, ka_ref, va_ref):
    cqn = (_rms(cq_ref[0].astype(F32)) * qn_ref[...]).astype(BF16)
    qm = jnp.dot(cqn, wqm_ref[...], preferred_element_type=F32)
    qr = jnp.dot(cqn, wqr_ref[...], preferred_element_type=F32)
    cosq = cosq_ref[...]
    sinq = sinq_ref[...]
    for h in range(MLA_HEADS):
        sl = slice(h * MLA_QK_PAD, (h + 1) * MLA_QK_PAD)
        qa_ref[0, h] = (qm[:, sl] * cosq + qr[:, sl] * sinq).astype(qa_ref.dtype)
    ckvn = (_rms(ckv_ref[0].astype(F32)) * kvn_ref[...]).astype(BF16)
    kv = jnp.dot(ckvn, wkv_ref[...], preferred_element_type=F32)
    kr = kr_ref[0].astype(F32)
    kpe = (kr * ck_ref[...] + pltpu.roll(kr, 128 - 16, axis=1) * s1_ref[...]
           + pltpu.roll(kr, 16, axis=1) * s2_ref[...])
    kpe_t = kpe.T.astype(ka_ref.dtype)
    for h in range(MLA_HEADS):
        base = h * (MLA_NOPE + MLA_V)
        ka_ref[0, h, 0:MLA_NOPE, :] = kv[:, base:base + MLA_NOPE].T.astype(ka_ref.dtype)
        ka_ref[0, h, MLA_NOPE:MLA_QK_PAD, :] = kpe_t
        va_ref[0, h] = kv[:, base + MLA_NOPE:base + MLA_NOPE + MLA_V].astype(va_ref.dtype)


def _mla_prep(p3, qn, kvn, wqm, wqr, wkv, tabs, ctx_len):
    b, t, _ = p3.shape
    tm = _tile(ctx_len, 256)
    cosq, sinq, ck, s1, s2 = tabs

    def col(name, width):
        blk = P_OFF[name] // width
        return pl.BlockSpec((1, tm, width), lambda bi, i: (bi, i, blk))

    def full(a):
        return pl.BlockSpec(a.shape, lambda bi, i: (0,) * a.ndim)

    def rows(a):
        return pl.BlockSpec((tm, a.shape[1]), lambda bi, i: (i, 0))

    hm = lambda w: pl.BlockSpec((1, MLA_HEADS, tm, w), lambda bi, i: (bi, 0, i, 0))
    return pl.pallas_call(
        _mla_prep_kernel,
        grid=(b, t // tm),
        in_specs=[col('cq', MLA_Q_LORA), col('ckv', MLA_KV_LORA), col('kr', 128),
                  full(qn), full(kvn), full(wqm), full(wqr), full(wkv),
                  rows(cosq), rows(sinq), rows(ck), rows(s1), rows(s2)],
        out_specs=[hm(MLA_QK_PAD),
                   pl.BlockSpec((1, MLA_HEADS, MLA_QK_PAD, tm), lambda bi, i: (bi, 0, 0, i)),
                   hm(MLA_V)],
        out_shape=[jax.ShapeDtypeStruct((b, MLA_HEADS, t, MLA_QK_PAD), BF16),
                   jax.ShapeDtypeStruct((b, MLA_HEADS, MLA_QK_PAD, t), BF16),
                   jax.ShapeDtypeStruct((b, MLA_HEADS, t, MLA_V), BF16)],
        compiler_params=_params("parallel", "parallel"),
        name="mla_prep",
    )(p3, p3, p3, qn, kvn, wqm, wqr, wkv, cosq, sinq, ck, s1, s2)


def _rope128(y, c, s1, s2):
    return y * c + pltpu.roll(y, 128 - 32, axis=1) * s1 + pltpu.roll(y, 32, axis=1) * s2


def _gqa_prep_kernel(gq_ref, gk_ref, gv_ref, qn_ref, kn_ref, cq_ref, s1q_ref, s2q_ref,
                     ck_ref, s1k_ref, s2k_ref, qb_ref, kb_ref, vb_ref):
    gq = gq_ref[0].astype(F32)
    for h in range(GQA_HEADS):
        y = _rms(gq[:, h * GQA_DIM:(h + 1) * GQA_DIM]) * qn_ref[...]
        qb_ref[0, h] = _rope128(y, cq_ref[...], s1q_ref[...], s2q_ref[...]).astype(qb_ref.dtype)
    gk = gk_ref[0].astype(F32)
    gv = gv_ref[0]
    for h in range(GQA_KV_HEADS):
        y = _rms(gk[:, h * GQA_DIM:(h + 1) * GQA_DIM]) * kn_ref[...]
        kb_ref[0, h] = _rope128(y, ck_ref[...], s1k_ref[...], s2k_ref[...]).T.astype(kb_ref.dtype)
        vb_ref[0, h] = gv[:, h * GQA_DIM:(h + 1) * GQA_DIM].astype(vb_ref.dtype)


def _gqa_prep(p3, qn, kn, tabs, ctx_len):
    b, t, _ = p3.shape
    tm = _tile(ctx_len, 256)

    def col(name, width):
        blk = P_OFF[name] // width
        return pl.BlockSpec((1, tm, width), lambda bi, i: (bi, i, blk))

    def full(a):
        return pl.BlockSpec(a.shape, lambda bi, i: (0,) * a.ndim)

    def rows(a):
        return pl.BlockSpec((tm, a.shape[1]), lambda bi, i: (i, 0))

    hm = lambda nh: pl.BlockSpec((1, nh, tm, GQA_DIM), lambda bi, i: (bi, 0, i, 0))
    return pl.pallas_call(
        _gqa_prep_kernel,
        grid=(b, t // tm),
        in_specs=[col('gq', GQA_WIDTH), col('gk', GQA_KV_WIDTH), col('gv', GQA_KV_WIDTH),
                  full(qn), full(kn)] + [rows(a) for a in tabs],
        out_specs=[hm(GQA_HEADS),
                   pl.BlockSpec((1, GQA_KV_HEADS, GQA_DIM, tm), lambda bi, i: (bi, 0, 0, i)),
                   hm(GQA_KV_HEADS)],
        out_shape=[jax.ShapeDtypeStruct((b, GQA_HEADS, t, GQA_DIM), BF16),
                   jax.ShapeDtypeStruct((b, GQA_KV_HEADS, GQA_DIM, t), BF16),
                   jax.ShapeDtypeStruct((b, GQA_KV_HEADS, t, GQA_DIM), BF16)],
        compiler_params=_params("parallel", "parallel"),
        name="gqa_prep",
    )(p3, p3, p3, qn, kn, *tabs)


ATTN_KC = 256
ATTN_RB = 128
LOG2E = math.log2(math.e)


def _attn_rows(q, kt_ref, v_ref, s_sc, p_sc, nkeys):
    rows = q.shape[0]
    for c0 in range(0, nkeys, ATTN_KC):
        s_sc[:, c0:c0 + ATTN_KC] = jnp.dot(q, kt_ref[0, 0, :, c0:c0 + ATTN_KC],
                                           preferred_element_type=F32)
    inv_l = []
    for r0 in range(0, rows, ATTN_RB):
        mpart = s_sc[r0:r0 + ATTN_RB, 0:128]
        for j in range(128, nkeys, 128):
            mpart = jnp.maximum(mpart, s_sc[r0:r0 + ATTN_RB, j:j + 128])
        m = jnp.broadcast_to(jnp.max(mpart, axis=-1, keepdims=True), (ATTN_RB, 128))
        lpart = jnp.zeros((ATTN_RB, 128), F32)
        for j in range(0, nkeys, 128):
            p = jnp.exp2(s_sc[r0:r0 + ATTN_RB, j:j + 128] - m)
            lpart = lpart + p
            p_sc[r0:r0 + ATTN_RB, j:j + 128] = p.astype(p_sc.dtype)
        inv_l.append(1.0 / jnp.sum(lpart, axis=-1, keepdims=True))
    o = jnp.dot(p_sc[:, 0:nkeys], v_ref[0, 0, 0:nkeys, :], preferred_element_type=F32)
    return o * jnp.concatenate(inv_l, axis=0)


def _attn_kernel(q_ref, kt_ref, v_ref, o_ref, s_sc, p_sc, *, group, tq, ctx_len, ctx_q_tiles, q_off):
    dk = q_ref.shape[-1]
    dv = v_ref.shape[-1]
    t = v_ref.shape[2]
    q = q_ref[0].reshape(group * tq, dk)

    def run(nkeys):
        o = _attn_rows(q, kt_ref, v_ref, s_sc, p_sc, nkeys)
        for g in range(group):
            o_ref[0, :, g * dv:(g + 1) * dv] = o[g * tq:(g + 1) * tq, :].astype(o_ref.dtype)

    if ctx_q_tiles > q_off:
        is_ctx = pl.program_id(2) + q_off < ctx_q_tiles
        pl.when(is_ctx)(lambda: run(ctx_len))
        pl.when(jnp.logical_not(is_ctx))(lambda: run(t))
    else:
        run(t)


def _attention(q, kt, v, ctx_len, tq, skip_ctx_queries, out_dtype):
    b, hq, t, dk = q.shape
    hkv, dv = kt.shape[1], v.shape[-1]
    group = hq // hkv
    tq = _tile(ctx_len, tq)
    ctx_q_tiles = ctx_len // tq
    q_off = ctx_q_tiles if skip_ctx_queries else 0
    nq = t // tq - q_off
    kern = functools.partial(_attn_kernel, group=group, tq=tq, ctx_len=ctx_len,
                             ctx_q_tiles=ctx_q_tiles, q_off=q_off)
    rows = group * tq
    assert rows % ATTN_RB == 0 and t % ATTN_KC == 0 and ctx_len % ATTN_KC == 0
    return pl.pallas_call(
        kern,
        grid=(b, hkv, nq),
        in_specs=[pl.BlockSpec((1, group, tq, dk), lambda bi, h, i: (bi, h, i + q_off, 0)),
                  pl.BlockSpec((1, 1, dk, t), lambda bi, h, i: (bi, h, 0, 0)),
                  pl.BlockSpec((1, 1, t, dv), lambda bi, h, i: (bi, h, 0, 0))],
        out_specs=pl.BlockSpec((1, tq, group * dv), lambda bi, h, i: (bi, i, h)),
        out_shape=jax.ShapeDtypeStruct((b, nq * tq, hq * dv), out_dtype),
        scratch_shapes=[pltpu.VMEM((rows, t), F32), pltpu.VMEM((rows, t), BF16)],
        compiler_params=_params("parallel", "parallel", "arbitrary"),
        name="attention_dk%d" % dk,
    )(q, kt, v)


def _conv_kernel(x_ref, w_ref, b_ref, o_ref, pad_sc, *, ctx_len, rows):
    t = x_ref.shape[1]
    nch = x_ref.shape[2]
    halo = 8
    segs = ((0, ctx_len), (ctx_len, t))
    zeros = jnp.zeros((halo, nch), F32)
    for si, (lo, hi) in enumerate(segs):
        pad_sc[lo + si * halo:lo + (si + 1) * halo, :] = zeros
        for r0 in range(lo, hi, rows):
            pad_sc[r0 + (si + 1) * halo:r0 + (si + 1) * halo + rows, :] = x_ref[0, r0:r0 + rows, :].astype(F32)
    pad_sc[t + 2 * halo:t + 3 * halo, :] = zeros
    w = w_ref[...]
    bias = b_ref[...]
    for si, (lo, hi) in enumerate(segs):
        for r0 in range(lo, hi, rows):
            base = r0 + (si + 1) * halo - SSD_CONV // 2
            acc = bias + w[0:1, :] * pad_sc[base:base + rows, :]
            for kk in range(1, SSD_CONV):
                acc = acc + w[kk:kk + 1, :] * pad_sc[base + kk:base + kk + rows, :]
            o_ref[0, r0:r0 + rows, :] = _silu(acc).astype(o_ref.dtype)


def _conv(p3, conv_w, conv_b, ctx_len, out_dtype):
    b, t, _ = p3.shape
    nch = 256
    rows = _tile(ctx_len, 256)
    blk0 = P_OFF['xbc'] // nch
    kern = functools.partial(_conv_kernel, ctx_len=ctx_len, rows=rows)
    return pl.pallas_call(
        kern,
        grid=(b, SSD_CONV_DIM // nch),
        in_specs=[pl.BlockSpec((1, t, nch), lambda bi, j: (bi, 0, blk0 + j)),
                  pl.BlockSpec((SSD_CONV, nch), lambda bi, j: (0, j)),
                  pl.BlockSpec((1, nch), lambda bi, j: (0, j))],
        out_specs=pl.BlockSpec((1, t, nch), lambda bi, j: (bi, 0, j)),
        out_shape=jax.ShapeDtypeStruct((b, t, SSD_CONV_DIM), out_dtype),
        scratch_shapes=[pltpu.VMEM((t + 24, nch), F32)],
        compiler_params=_params("parallel", "parallel"),
        name="ssd_conv",
    )(p3, conv_w, conv_b.reshape(1, SSD_CONV_DIM))


def _split3(v):
    hi = v.astype(BF16)
    r1 = v - hi.astype(F32)
    mid = r1.astype(BF16)
    lo = (r1 - mid.astype(F32)).astype(BF16)
    return hi, mid, lo


def _expand_heads(col):
    q = col.shape[0]
    first = lax.broadcasted_iota(jnp.int32, (q, 2 * SSD_P), 1) < SSD_P
    tiles = []
    for e in range(0, SSD_E, 2):
        tiles.append(jnp.where(first, jnp.broadcast_to(col[:, e:e + 1], (q, 2 * SSD_P)),
                               jnp.broadcast_to(col[:, e + 1:e + 2], (q, 2 * SSD_P))))
    return jnp.concatenate(tiles, axis=1)


def _ssd_direction(xs, bm, cm, dt_col_raw, dt_row_raw, bias_col, bias_row, a_col, a_row, h_ref,
                   backward):
    q = xs.shape[0]
    ri = lax.broadcasted_iota(jnp.int32, (q, q), 0)
    ci = lax.broadcasted_iota(jnp.int32, (q, q), 1)
    low = ri >= ci
    keep = (ri <= ci) if backward else low
    tri_col = jnp.where(keep, 1.0, 0.0).astype(BF16)
    tri_row = jnp.where((ri >= ci) if backward else (ri <= ci), 1.0, 0.0).astype(BF16)

    dt_c = _softplus(dt_col_raw + bias_col)
    dt_r = _softplus(dt_row_raw + bias_row)
    cum_c = sum(jnp.dot(tri_col, part, preferred_element_type=F32) for part in _split3(dt_c * a_col))
    cum_r = sum(jnp.dot(part, tri_row, preferred_element_type=F32) for part in _split3(dt_r * a_row))
    cum_full = _expand_heads(cum_c)
    dt_full = _expand_heads(dt_c)
    total = cum_full[0:1, :] if backward else cum_full[q - 1:q, :]

    xdt = xs * dt_full
    cmb = cm.astype(BF16)
    cb = lax.dot_general(cmb, bm.astype(BF16), (((1,), (1,)), ((), ())), preferred_element_type=F32)
    h = h_ref[...]
    y_off = jnp.dot(cmb, h.astype(BF16), preferred_element_type=F32) * jnp.exp(cum_full)
    wgt = (xdt * jnp.exp(total - cum_full)).astype(BF16)
    h_ref[...] = jnp.exp(total) * h + jnp.dot(bm.T.astype(BF16), wgt, preferred_element_type=F32)

    xdt_b = xdt.astype(BF16)
    lane = lax.broadcasted_iota(jnp.int32, (q, 2 * SSD_P), 1)
    parts = []
    for pair in range(SSD_E // 2):
        x_pair = xdt_b[:, pair * 2 * SSD_P:(pair + 1) * 2 * SSD_P]
        ys = []
        for e in (2 * pair, 2 * pair + 1):
            seg = cum_c[:, e:e + 1] - cum_r[e:e + 1, :]
            dec = jnp.exp(jnp.where(keep, seg, -jnp.inf))
            ys.append(jnp.dot((cb * dec).astype(BF16), x_pair, preferred_element_type=F32))
        parts.append(jnp.where(lane < SSD_P, ys[0], ys[1]))
    return jnp.concatenate(parts, axis=1) + y_off


def _ssd_kernel(xf_ref, bf_ref, cf_ref, xb_ref, bb_ref, cb_ref, dcf_ref, drf_ref, dcb_ref, drb_ref,
                bias_c_ref, bias_r_ref, a_c_ref, a_r_ref, d_ref, yf_ref, yb_ref, hf_sc, hb_sc):
    @pl.when(pl.program_id(2) == 0)
    def _():
        hf_sc[...] = jnp.zeros(hf_sc.shape, F32)
        hb_sc[...] = jnp.zeros(hb_sc.shape, F32)

    nsub = xf_ref.shape[1] // SSD_Q
    for i in range(nsub):
        rf = slice(i * SSD_Q, (i + 1) * SSD_Q)
        rb = slice((nsub - 1 - i) * SSD_Q, (nsub - i) * SSD_Q)
        xs = xf_ref[0, rf, :].astype(F32)
        yf = _ssd_direction(xs, bf_ref[0, rf, :].astype(F32), cf_ref[0, rf, :].astype(F32),
                            dcf_ref[0, 0, 0, rf, :], drf_ref[0, 0, 0, :, rf],
                            bias_c_ref[0, 0], bias_r_ref[0, 0], a_c_ref[0, 0], a_r_ref[0, 0], hf_sc, False)
        yf_ref[0, rf, :] = (yf + d_ref[0] * xs).astype(yf_ref.dtype)
        yb = _ssd_direction(xb_ref[0, rb, :].astype(F32), bb_ref[0, rb, :].astype(F32),
                            cb_ref[0, rb, :].astype(F32), dcb_ref[0, 0, 0, rb, :], drb_ref[0, 0, 0, :, rb],
                            bias_c_ref[1, 0], bias_r_ref[1, 0], a_c_ref[1, 0], a_r_ref[1, 0], hb_sc, True)
        yb_ref[0, rb, :] = yb.astype(yb_ref.dtype)


def _ssd(xc, dt_col, dt_row, bias_c, bias_r, a_c, a_r, d_full, ctx_len, out_dtype):
    b, t, _ = xc.shape
    rows = _tile(ctx_len, SSD_ROWS, SSD_Q)
    nblk = t // rows
    nctx = ctx_len // rows
    bblk = SSD_INNER // SSD_N
    cblk = bblk + SSD_G

    def bidx(c):
        return jnp.where(c < nctx, nctx - 1 - c, nblk - 1 - (c - nctx))

    fx = lambda bi, g, c: (bi, c, g)
    fb = lambda bi, g, c: (bi, c, bblk + g)
    fc = lambda bi, g, c: (bi, c, cblk + g)
    bx = lambda bi, g, c: (bi, bidx(c), g)
    bb = lambda bi, g, c: (bi, bidx(c), bblk + g)
    bc = lambda bi, g, c: (bi, bidx(c), cblk + g)
    small = lambda a: pl.BlockSpec((2, 1) + a.shape[2:], lambda bi, g, c: (0, g, 0, 0))
    return pl.pallas_call(
        _ssd_kernel,
        grid=(b, SSD_G, nblk),
        in_specs=[pl.BlockSpec((1, rows, SSD_GW), fx), pl.BlockSpec((1, rows, SSD_N), fb),
                  pl.BlockSpec((1, rows, SSD_N), fc),
                  pl.BlockSpec((1, rows, SSD_GW), bx), pl.BlockSpec((1, rows, SSD_N), bb),
                  pl.BlockSpec((1, rows, SSD_N), bc),
                  pl.BlockSpec((1, 1, 1, rows, SSD_E), lambda bi, g, c: (bi, 0, g, c, 0)),
                  pl.BlockSpec((1, 1, 1, SSD_E, rows), lambda bi, g, c: (bi, 0, g, 0, c)),
                  pl.BlockSpec((1, 1, 1, rows, SSD_E), lambda bi, g, c: (bi, 1, g, bidx(c), 0)),
                  pl.BlockSpec((1, 1, 1, SSD_E, rows), lambda bi, g, c: (bi, 1, g, 0, bidx(c))),
                  small(bias_c), small(bias_r), small(a_c), small(a_r),
                  pl.BlockSpec((1, 1, SSD_GW), lambda bi, g, c: (g, 0, 0))],
        out_specs=[pl.BlockSpec((1, rows, SSD_GW), fx), pl.BlockSpec((1, rows, SSD_GW), bx)],
        out_shape=[jax.ShapeDtypeStruct((b, t, SSD_INNER), out_dtype)] * 2,
        scratch_shapes=[pltpu.VMEM((SSD_N, SSD_GW), F32), pltpu.VMEM((SSD_N, SSD_GW), F32)],
        compiler_params=_params("parallel", "parallel", "arbitrary"),
        name="ssd_scan",
    )(xc, xc, xc, xc, xc, xc, dt_col, dt_row, dt_col, dt_row, bias_c, bias_r, a_c, a_r, d_full)


MERGE_TN = 512


def _merge_kernel(ya_ref, ga_ref, yb_ref, gb_ref, yf_ref, ybk_ref, z_ref, nrm_ref,
                  mg_ref, wa_ref, wb_ref, wc_ref, u_ref, c_sc):
    a_in = (ya_ref[0].astype(F32) * _silu(ga_ref[0].astype(F32))).astype(BF16)
    b_in = (yb_ref[0].astype(F32) * _silu(gb_ref[0].astype(F32))).astype(BF16)
    v = (yf_ref[0].astype(F32) + ybk_ref[0].astype(F32)) * _silu(z_ref[0].astype(F32))
    for g in range(SSD_G):
        sl = slice(g * SSD_GW, (g + 1) * SSD_GW)
        c_sc[:, sl] = (_rms(v[:, sl]) * nrm_ref[:, sl]).astype(BF16)
    c_in = c_sc[...]
    d = u_ref.shape[-1]
    for j in range(0, d, MERGE_TN):
        sl = slice(j, j + MERGE_TN)
        br_a = jnp.dot(a_in, wa_ref[:, sl], preferred_element_type=F32)
        br_b = jnp.dot(b_in, wb_ref[:, sl], preferred_element_type=F32)
        br_c = jnp.dot(c_in, wc_ref[:, sl], preferred_element_type=F32)
        u = (jax.nn.sigmoid(mg_ref[0, :, j:j + MERGE_TN].astype(F32)) * br_a
             + jax.nn.sigmoid(mg_ref[0, :, d + j:d + j + MERGE_TN].astype(F32)) * br_b
             + jax.nn.sigmoid(mg_ref[0, :, 2 * d + j:2 * d + j + MERGE_TN].astype(F32)) * br_c)
        u_ref[0, :, sl] = u.astype(u_ref.dtype)


def _merge(ya, yb, yf, ybk, p3, ssd_norm, wa, wb, wc, row_off):
    b, rows, _ = ya.shape
    tm = _tile(rows, 256)
    d = D_MODEL
    ro = row_off // tm
    assert P_OFF['mg'] == 0

    def col(name, width):
        blk = P_OFF[name] // width
        return pl.BlockSpec((1, tm, width), lambda bi, i: (bi, i + ro, blk))

    loc = lambda w: pl.BlockSpec((1, tm, w), lambda bi, i: (bi, i, 0))
    glob = lambda w: pl.BlockSpec((1, tm, w), lambda bi, i: (bi, i + ro, 0))
    wspec = lambda k: pl.BlockSpec((k, d), lambda bi, i: (0, 0), pipeline_mode=pl.Buffered(1))
    return pl.pallas_call(
        _merge_kernel,
        grid=(b, rows // tm),
        in_specs=[loc(MLA_WIDTH), col('ga', MLA_WIDTH), loc(GQA_WIDTH), col('gb', GQA_WIDTH),
                  glob(SSD_INNER), glob(SSD_INNER), col('z', SSD_INNER),
                  pl.BlockSpec((1, SSD_INNER), lambda bi, i: (0, 0)),
                  glob(N_BRANCH * d), wspec(MLA_WIDTH), wspec(GQA_WIDTH), wspec(SSD_INNER)],
        out_specs=pl.BlockSpec((1, tm, d), lambda bi, i: (bi, i, 0)),
        out_shape=jax.ShapeDtypeStruct((b, rows, d), BF16),
        scratch_shapes=[pltpu.VMEM((tm, SSD_INNER), BF16)],
        compiler_params=_params("parallel", "parallel"),
        name="merge",
    )(ya, p3, yb, p3, yf, ybk, p3, ssd_norm.reshape(1, SSD_INNER), p3, wa, wb, wc)


def _out_ln_kernel(u_ref, w_ref, x_ref, gate_ref, g_ref, b_ref, *rest, with_next):
    out = jnp.dot(u_ref[0], w_ref[...], preferred_element_type=F32)
    r = DEEPNORM_ALPHA * x_ref[0] + gate_ref[0] * out
    xn = _layer_norm(r) * g_ref[...] + b_ref[...]
    if with_next:
        sh_ref, sc_ref, xo_ref, xm_ref = rest
        xo_ref[0] = xn
        xm_ref[0] = (_layer_norm(xn) * (1.0 + sc_ref[0]) + sh_ref[0]).astype(xm_ref.dtype)
    else:
        (xo_ref,) = rest
        xo_ref[0] = xn


def _out_ln(u, w_out, xc, tab, ln_g, ln_b, next_tab, nb, ctx_len, row_off):
    b, rows, d = u.shape
    tm = _tile(ctx_len, 256)
    ro = row_off // tm
    row = _mod_row_index(nb, ctx_len // tm)
    with_next = next_tab is not None
    loc = pl.BlockSpec((1, tm, d), lambda bi, i: (bi, i, 0))
    vec = pl.BlockSpec((1, d), lambda bi, i: (0, 0))
    in_specs = [loc, pl.BlockSpec((d, d), lambda bi, i: (0, 0)),
                pl.BlockSpec((1, tm, d), lambda bi, i: (bi, i + ro, 0)),
                pl.BlockSpec((1, 1, d), lambda bi, i: (row(bi, i + ro), 0, 2)), vec, vec]
    args = [u, w_out, xc, tab, ln_g.reshape(1, d), ln_b.reshape(1, d)]
    out_specs = [loc]
    out_shape = [jax.ShapeDtypeStruct((b, rows, d), F32)]
    if with_next:
        in_specs += [pl.BlockSpec((1, 1, d), lambda bi, i: (row(bi, i + ro), 0, 0)),
                     pl.BlockSpec((1, 1, d), lambda bi, i: (row(bi, i + ro), 0, 1))]
        args += [next_tab, next_tab]
        out_specs.append(loc)
        out_shape.append(jax.ShapeDtypeStruct((b, rows, d), BF16))
    return pl.pallas_call(
        functools.partial(_out_ln_kernel, with_next=with_next),
        grid=(b, rows // tm),
        in_specs=in_specs,
        out_specs=out_specs,
        out_shape=out_shape,
        compiler_params=_params("parallel", "parallel"),
        name="out_ln",
    )(*args)


def _rope_angles(rows, dim):
    row, col = jnp.meshgrid(jnp.arange(rows, dtype=F32), jnp.arange(GRID_W, dtype=F32), indexing='ij')
    half = dim // 2
    inv_freq = ROPE_THETA ** (-jnp.arange(0, half, 2, dtype=F32) / half)
    ang_r = row.reshape(-1, 1) * inv_freq
    ang_c = col.reshape(-1, 1) * inv_freq
    return jnp.concatenate([ang_r, ang_r, ang_c, ang_c], axis=-1)


def _rope_tables(seq, ctx_len, dim):
    ang = _rope_angles(seq // GRID_W, dim)
    cos = jnp.concatenate([jnp.ones((ctx_len, dim), F32), jnp.cos(ang)], axis=0)
    sin = jnp.concatenate([jnp.zeros((ctx_len, dim), F32), jnp.sin(ang)], axis=0)
    return cos, sin


def _roll_tables(cos, sin, dim, scale):
    t = cos.shape[0]
    quarter = dim // 4
    first = (jnp.arange(dim) % (2 * quarter)) < quarter
    s1 = jnp.where(first, -sin, 0.0)
    s2 = jnp.where(first, 0.0, sin)
    pad = lambda a: jnp.pad(a * scale, ((0, 0), (0, 128 - dim)))
    return pad(cos), pad(s1), pad(s2)


def _rot_matrix(dim):
    quarter = dim // 4
    r = np.zeros((dim, dim), np.float32)
    for i in range(dim):
        blk = i // quarter
        if blk % 2 == 0:
            r[i + quarter, i] = -1.0
        else:
            r[i - quarter, i] = 1.0
    return jnp.asarray(r)


def _mla_q_weights(w_uq):
    w = w_uq.reshape(MLA_Q_LORA, MLA_HEADS, MLA_QK)
    nope, pe = w[..., :MLA_NOPE], w[..., MLA_NOPE:]
    pe_rot = jnp.einsum('khd,de->khe', pe, _rot_matrix(MLA_ROPE), precision=HIGHEST)
    zpad = jnp.zeros((MLA_Q_LORA, MLA_HEADS, MLA_QK_PAD - MLA_QK), F32)
    main = jnp.concatenate([nope, pe, zpad], axis=-1)
    rot = jnp.concatenate([jnp.zeros_like(nope), pe_rot, zpad], axis=-1)
    return (main.reshape(MLA_Q_LORA, -1).astype(BF16), rot.reshape(MLA_Q_LORA, -1).astype(BF16))


def _permute_w_in(w):
    wb = w.astype(BF16)
    parts = [wb[:, IN_OFFSETS[n]:IN_OFFSETS[n] + IN_WIDTHS[n]] for n in P_ORDER]
    parts.append(jnp.zeros((w.shape[0], P_WIDTH - P_USED), BF16))
    return jnp.concatenate(parts, axis=1)


P_DTYPE = BF16
Y_DTYPE = BF16


def kernel(x, c, ctx, c_ctx, w_mod, b_mod, w_in, mla_q_norm, mla_w_uq, mla_kv_norm, mla_w_ukv,
           gqa_q_norm, gqa_k_norm, ssd_conv_w, ssd_conv_b, ssd_a_log, ssd_dt_bias, ssd_d, ssd_norm,
           w_br_a, w_br_b, w_br_c, w_out, ln_g, ln_b):
    nb, seq, d = x.shape
    ctx_len = ctx.shape[1]
    t = ctx_len + seq
    depth = w_in.shape[0]
    assert d == D_MODEL and nb < 8 and seq % GRID_W == 0
    assert ctx_len % SSD_Q == 0 and seq % SSD_Q == 0

    cos_a, sin_a = _rope_tables(seq, ctx_len, MLA_ROPE)
    cos_b, sin_b = _rope_tables(seq, ctx_len, GQA_DIM)
    sq = MLA_QK ** -0.5 * LOG2E
    zq = jnp.zeros((t, MLA_QK_PAD - MLA_QK), F32)
    cosq = jnp.concatenate([jnp.full((t, MLA_NOPE), sq, F32), cos_a * sq, zq], axis=1)
    sinq = jnp.concatenate([jnp.zeros((t, MLA_NOPE), F32), sin_a * sq, zq], axis=1)
    mla_tabs = (cosq, sinq) + _roll_tables(cos_a, sin_a, MLA_ROPE, 1.0)
    gqa_tabs = (_roll_tables(cos_b, sin_b, GQA_DIM, GQA_DIM ** -0.5 * LOG2E)
                + _roll_tables(cos_b, sin_b, GQA_DIM, 1.0))

    c_rows = jnp.zeros((8, d), F32).at[:nb].set(c).at[nb].set(c_ctx)
    tabs = [_mod_rows(c_rows, w_mod[l], b_mod[l]).reshape(8, 1, 3 * d) for l in range(depth)]

    xc = jnp.concatenate([ctx, x], axis=1)
    xm = _ln_mod(xc, tabs[0], nb, ctx_len)

    for l in range(depth):
        last = l == depth - 1
        wp = _permute_w_in(w_in[l])
        p3 = _in_proj(xm.reshape(nb * t, d), wp, P_DTYPE).reshape(nb, t, P_WIDTH)

        wqm, wqr = _mla_q_weights(mla_w_uq[l])
        qa, ka, va = _mla_prep(p3, mla_q_norm[l].reshape(1, -1), mla_kv_norm[l].reshape(1, -1),
                               wqm, wqr, mla_w_ukv[l].astype(BF16), mla_tabs, ctx_len)
        ya = _attention(qa, ka, va, ctx_len, 256, last, Y_DTYPE)
        qb, kb, vb = _gqa_prep(p3, gqa_q_norm[l].reshape(1, -1), gqa_k_norm[l].reshape(1, -1),
                               gqa_tabs, ctx_len)
        yb = _attention(qb, kb, vb, ctx_len, 128, last, Y_DTYPE)

        xconv = _conv(p3, ssd_conv_w[l], ssd_conv_b[l], ctx_len, Y_DTYPE)
        dtr = p3[:, :, P_OFF['dtr']:P_OFF['dtr'] + 2 * SSD_HEADS].astype(F32)
        dt5 = dtr.reshape(nb, t, 2, SSD_G, SSD_E)
        dt_col = jnp.transpose(dt5, (0, 2, 3, 1, 4))
        dt_row = jnp.transpose(dt5, (0, 2, 3, 4, 1))
        bias = ssd_dt_bias[l].astype(F32).reshape(2, SSD_G, SSD_E)
        a = -jnp.exp(ssd_a_log[l].astype(F32)).reshape(2, SSD_G, SSD_E)
        d_full = jnp.repeat(ssd_d[l].astype(F32), SSD_P).reshape(SSD_G, 1, SSD_GW)
        yf, ybk = _ssd(xconv, dt_col, dt_row, bias[:, :, None, :], bias[:, :, :, None],
                       a[:, :, None, :], a[:, :, :, None], d_full, ctx_len, Y_DTYPE)

        row_off = ctx_len if last else 0
        u = _merge(ya, yb, yf, ybk, p3, ssd_norm[l], w_br_a[l].astype(BF16), w_br_b[l].astype(BF16),
                   w_br_c[l].astype(BF16), row_off)
        if last:
            (xo,) = _out_ln(u, w_out[l].astype(BF16), xc, tabs[l], ln_g[l], ln_b[l], None, nb,
                            ctx_len, row_off)
            return xo
        xc, xm = _out_ln(u, w_out[l].astype(BF16), xc, tabs[l], ln_g[l], ln_b[l], tabs[l + 1], nb,
                         ctx_len, row_off)
```

```python
import functools
import math

import numpy as np
import jax
import jax.numpy as jnp
from jax import lax
from jax.experimental import pallas as pl
from jax.experimental.pallas import tpu as pltpu

F32 = jnp.float32
BF16 = jnp.bfloat16
HIGHEST = lax.Precision.HIGHEST

D_MODEL = 2048
DEPTH = 2
GRID_W = 64
ROPE_THETA = 10000.0
EPS = 1e-6

MLA_HEADS = 8
MLA_Q_LORA = 512
MLA_KV_LORA = 256
MLA_NOPE = 128
MLA_ROPE = 64
MLA_V = 128
MLA_QK = MLA_NOPE + MLA_ROPE
MLA_QK_PAD = 256
MLA_WIDTH = MLA_HEADS * MLA_V

GQA_HEADS = 8
GQA_KV_HEADS = 2
GQA_GROUP = GQA_HEADS // GQA_KV_HEADS
GQA_DIM = 128
GQA_WIDTH = GQA_HEADS * GQA_DIM
GQA_KV_WIDTH = GQA_KV_HEADS * GQA_DIM

SSD_INNER = D_MODEL
SSD_P = 64
SSD_HEADS = SSD_INNER // SSD_P
SSD_G = 4
SSD_E = SSD_HEADS // SSD_G
SSD_N = 128
SSD_CONV = 5
SSD_Q = 128
SSD_ROWS = 256
SSD_GW = SSD_E * SSD_P
SSD_CONV_DIM = SSD_INNER + 2 * SSD_G * SSD_N

N_BRANCH = 3
IN_SPLITS = (MLA_Q_LORA, MLA_KV_LORA, MLA_ROPE, MLA_WIDTH, GQA_WIDTH, GQA_KV_WIDTH, GQA_KV_WIDTH,
             GQA_WIDTH, SSD_INNER, SSD_CONV_DIM, 2 * SSD_HEADS, N_BRANCH * D_MODEL)
IN_NAMES = ('cq', 'ckv', 'kr', 'ga', 'gq', 'gk', 'gv', 'gb', 'z', 'xbc', 'dtr', 'mg')
IN_OFFSETS = dict(zip(IN_NAMES, np.concatenate([[0], np.cumsum(IN_SPLITS)[:-1]]).tolist()))
IN_WIDTHS = dict(zip(IN_NAMES, IN_SPLITS))
P_ORDER = ('mg', 'z', 'xbc', 'ga', 'gq', 'gb', 'cq', 'ckv', 'gk', 'gv', 'kr', 'dtr')
P_OFF = {}
_o = 0
for _n in P_ORDER:
    P_OFF[_n] = _o
    _o += IN_WIDTHS[_n]
P_USED = _o
P_TN = 512
P_WIDTH = -(-P_USED // P_TN) * P_TN

DEEPNORM_ALPHA = (2 * DEPTH) ** 0.25

VMEM_LIMIT = 56 * 2 ** 20


def _params(*sem):
    return pltpu.CompilerParams(dimension_semantics=sem, vmem_limit_bytes=VMEM_LIMIT)


def _tile(n, target, align=8):
    t = min(n, target)
    while t > align and (n % t or t % align):
        t -= align
    assert n % t == 0, (n, target)
    return t


def _silu(v):
    return v * jax.nn.sigmoid(v)


def _softplus(v):
    return jnp.maximum(v, 0.0) + jnp.log1p(jnp.exp(-jnp.abs(v)))


def _layer_norm(v):
    mu = jnp.mean(v, axis=-1, keepdims=True)
    vc = v - mu
    var = jnp.mean(vc * vc, axis=-1, keepdims=True)
    return vc * lax.rsqrt(var + EPS)


def _rms(v):
    return v * lax.rsqrt(jnp.mean(v * v, axis=-1, keepdims=True) + EPS)


def _mod_kernel(c_ref, w_ref, b_ref, o_ref):
    a = _silu(c_ref[...]).astype(BF16)
    o_ref[...] = jnp.dot(a, w_ref[...].astype(BF16), preferred_element_type=F32) + b_ref[...]


def _mod_rows(c_rows, w_mod, b_mod):
    r, d = c_rows.shape
    n = w_mod.shape[1]
    tn = 512
    return pl.pallas_call(
        _mod_kernel,
        grid=(n // tn,),
        in_specs=[pl.BlockSpec((r, d), lambda j: (0, 0)),
                  pl.BlockSpec((d, tn), lambda j: (0, j)),
                  pl.BlockSpec((1, tn), lambda j: (0, j))],
        out_specs=pl.BlockSpec((r, tn), lambda j: (0, j)),
        out_shape=jax.ShapeDtypeStruct((r, n), F32),
        compiler_params=_params("arbitrary"),
        name="mod_rows",
    )(c_rows, w_mod, b_mod.reshape(1, n))


def _ln_mod_kernel(x_ref, sh_ref, sc_ref, o_ref):
    y = _layer_norm(x_ref[0])
    o_ref[0] = (y * (1.0 + sc_ref[0]) + sh_ref[0]).astype(o_ref.dtype)


def _mod_row_index(nb, ctx_tiles):
    return lambda b, i: jnp.where(i < ctx_tiles, nb, b)


def _ln_mod(xc, tab, nb, ctx_len):
    b, t, d = xc.shape
    tm = _tile(ctx_len, 256)
    row = _mod_row_index(nb, ctx_len // tm)
    return pl.pallas_call(
        _ln_mod_kernel,
        grid=(b, t // tm),
        in_specs=[pl.BlockSpec((1, tm, d), lambda bi, i: (bi, i, 0)),
                  pl.BlockSpec((1, 1, d), lambda bi, i: (row(bi, i), 0, 0)),
                  pl.BlockSpec((1, 1, d), lambda bi, i: (row(bi, i), 0, 1))],
        out_specs=pl.BlockSpec((1, tm, d), lambda bi, i: (bi, i, 0)),
        out_shape=jax.ShapeDtypeStruct((b, t, d), BF16),
        compiler_params=_params("parallel", "parallel"),
        name="ln_mod",
    )(xc, tab, tab)


def _matmul_kernel(x_ref, w_ref, o_ref):
    o_ref[...] = jnp.dot(x_ref[...], w_ref[...], preferred_element_type=F32).astype(o_ref.dtype)


def _in_proj(xm2, wp, out_dtype):
    m, k = xm2.shape
    n = wp.shape[1]
    tm = _tile(m, 1024)
    tn = P_TN
    return pl.pallas_call(
        _matmul_kernel,
        grid=(m // tm, n // tn),
        in_specs=[pl.BlockSpec((tm, k), lambda i, j: (i, 0)),
                  pl.BlockSpec((k, tn), lambda i, j: (0, j))],
        out_specs=pl.BlockSpec((tm, tn), lambda i, j: (i, j)),
        out_shape=jax.ShapeDtypeStruct((m, n), out_dtype),
        compiler_params=_params("parallel", "arbitrary"),
        name="in_proj",
    )(xm2, wp)


def _mla_prep_kernel(cq_ref, ckv_ref, kr_ref, qn_ref, kvn_ref, wqm_ref, wqr_ref, wkv_ref,
                     cosq_ref, sinq_ref, ck_ref, s1_ref, s2_ref, qa_ref, ka_ref, va_ref):
    cqn = (_rms(cq_ref[0].astype(F32)) * qn_ref[...]).astype(BF16)
    qm = jnp.dot(cqn, wqm_ref[...], preferred_element_type=F32)
    qr = jnp.dot(cqn, wqr_ref[...], preferred_element_type=F32)
    cosq = cosq_ref[...]
    sinq = sinq_ref[...]
    for h in range(MLA_HEADS):
        sl = slice(h * MLA_QK_PAD, (h + 1) * MLA_QK_PAD)
        qa_ref[0, h] = (qm[:, sl] * cosq + qr[:, sl] * sinq).T.astype(qa_ref.dtype)
    ckvn = (_rms(ckv_ref[0].astype(F32)) * kvn_ref[...]).astype(BF16)
    kv = jnp.dot(ckvn, wkv_ref[...], preferred_element_type=F32)
    kr = kr_ref[0].astype(F32)
    kpe = (kr * ck_ref[...] + pltpu.roll(kr, 128 - 16, axis=1) * s1_ref[...]
           + pltpu.roll(kr, 16, axis=1) * s2_ref[...]).astype(ka_ref.dtype)
    for h in range(MLA_HEADS):
        base = h * (MLA_NOPE + MLA_V)
        ka_ref[0, h, :, 0:MLA_NOPE] = kv[:, base:base + MLA_NOPE].astype(ka_ref.dtype)
        ka_ref[0, h, :, MLA_NOPE:MLA_QK_PAD] = kpe
        va_ref[0, h] = kv[:, base + MLA_NOPE:base + MLA_NOPE + MLA_V].T.astype(va_ref.dtype)


def _mla_prep(p3, qn, kvn, wqm, wqr, wkv, tabs, ctx_len):
    b, t, _ = p3.shape
    tm = _tile(ctx_len, 256)
    cosq, sinq, ck, s1, s2 = tabs

    def col(name, width):
        blk = P_OFF[name] // width
        return pl.BlockSpec((1, tm, width), lambda bi, i: (bi, i, blk))

    def full(a):
        return pl.BlockSpec(a.shape, lambda bi, i: (0,) * a.ndim)

    def rows(a):
        return pl.BlockSpec((tm, a.shape[1]), lambda bi, i: (i, 0))

    hm = lambda w: pl.BlockSpec((1, MLA_HEADS, tm, w), lambda bi, i: (bi, 0, i, 0))
    return pl.pallas_call(
        _mla_prep_kernel,
        grid=(b, t // tm),
        in_specs=[col('cq', MLA_Q_LORA), col('ckv', MLA_KV_LORA), col('kr', 128),
                  full(qn), full(kvn), full(wqm), full(wqr), full(wkv),
                  rows(cosq), rows(sinq), rows(ck), rows(s1), rows(s2)],
        out_specs=[pl.BlockSpec((1, MLA_HEADS, MLA_QK_PAD, tm), lambda bi, i: (bi, 0, 0, i)),
                   hm(MLA_QK_PAD),
                   pl.BlockSpec((1, MLA_HEADS, MLA_V, tm), lambda bi, i: (bi, 0, 0, i))],
        out_shape=[jax.ShapeDtypeStruct((b, MLA_HEADS, MLA_QK_PAD, t), BF16),
                   jax.ShapeDtypeStruct((b, MLA_HEADS, t, MLA_QK_PAD), BF16),
                   jax.ShapeDtypeStruct((b, MLA_HEADS, MLA_V, t), BF16)],
        compiler_params=_params("parallel", "parallel"),
        name="mla_prep",
    )(p3, p3, p3, qn, kvn, wqm, wqr, wkv, cosq, sinq, ck, s1, s2)


def _rope128(y, c, s1, s2):
    return y * c + pltpu.roll(y, 128 - 32, axis=1) * s1 + pltpu.roll(y, 32, axis=1) * s2


def _gqa_prep_kernel(gq_ref, gk_ref, gv_ref, qn_ref, kn_ref, cq_ref, s1q_ref, s2q_ref,
                     ck_ref, s1k_ref, s2k_ref, qb_ref, kb_ref, vb_ref):
    gq = gq_ref[0].astype(F32)
    for h in range(GQA_HEADS):
        y = _rms(gq[:, h * GQA_DIM:(h + 1) * GQA_DIM]) * qn_ref[...]
        qb_ref[0, h] = _rope128(y, cq_ref[...], s1q_ref[...], s2q_ref[...]).T.astype(qb_ref.dtype)
    gk = gk_ref[0].astype(F32)
    gv = gv_ref[0].astype(F32)
    for h in range(GQA_KV_HEADS):
        y = _rms(gk[:, h * GQA_DIM:(h + 1) * GQA_DIM]) * kn_ref[...]
        kb_ref[0, h] = _rope128(y, ck_ref[...], s1k_ref[...], s2k_ref[...]).astype(kb_ref.dtype)
        vb_ref[0, h] = gv[:, h * GQA_DIM:(h + 1) * GQA_DIM].T.astype(vb_ref.dtype)


def _gqa_prep(p3, qn, kn, tabs, ctx_len):
    b, t, _ = p3.shape
    tm = _tile(ctx_len, 256)

    def col(name, width):
        blk = P_OFF[name] // width
        return pl.BlockSpec((1, tm, width), lambda bi, i: (bi, i, blk))

    def full(a):
        return pl.BlockSpec(a.shape, lambda bi, i: (0,) * a.ndim)

    def rows(a):
        return pl.BlockSpec((tm, a.shape[1]), lambda bi, i: (i, 0))

    hm = lambda nh: pl.BlockSpec((1, nh, tm, GQA_DIM), lambda bi, i: (bi, 0, i, 0))
    hmt = lambda nh: pl.BlockSpec((1, nh, GQA_DIM, tm), lambda bi, i: (bi, 0, 0, i))
    return pl.pallas_call(
        _gqa_prep_kernel,
        grid=(b, t // tm),
        in_specs=[col('gq', GQA_WIDTH), col('gk', GQA_KV_WIDTH), col('gv', GQA_KV_WIDTH),
                  full(qn), full(kn)] + [rows(a) for a in tabs],
        out_specs=[hmt(GQA_HEADS), hm(GQA_KV_HEADS), hmt(GQA_KV_HEADS)],
        out_shape=[jax.ShapeDtypeStruct((b, GQA_HEADS, GQA_DIM, t), BF16),
                   jax.ShapeDtypeStruct((b, GQA_KV_HEADS, t, GQA_DIM), BF16),
                   jax.ShapeDtypeStruct((b, GQA_KV_HEADS, GQA_DIM, t), BF16)],
        compiler_params=_params("parallel", "parallel"),
        name="gqa_prep",
    )(p3, p3, p3, qn, kn, *tabs)


ATTN_KC = 256
ATTN_QB = 128
LOG2E = math.log2(math.e)


def _attn_cols(qt, k_ref, vt_ref, s_sc, p_sc, nkeys):
    r = qt.shape[1]
    s_sc[0:nkeys, :] = jnp.dot(k_ref[0, 0, 0:nkeys, :], qt, preferred_element_type=F32)
    inv_l = []
    for c0 in range(0, r, ATTN_QB):
        cols = slice(c0, c0 + ATTN_QB)
        m8 = jnp.max(s_sc[0:nkeys, cols].reshape(nkeys // 8, 8, ATTN_QB), axis=0)
        m = jnp.max(m8, axis=0, keepdims=True)
        l8 = jnp.zeros((8, ATTN_QB), F32)
        for k0 in range(0, nkeys, ATTN_KC):
            p = jnp.exp2(s_sc[k0:k0 + ATTN_KC, cols] - m)
            l8 = l8 + jnp.sum(p.reshape(ATTN_KC // 8, 8, ATTN_QB), axis=0)
            p_sc[k0:k0 + ATTN_KC, cols] = p.astype(p_sc.dtype)
        inv_l.append(1.0 / jnp.sum(l8, axis=0, keepdims=True))
    ot = jnp.dot(vt_ref[0, 0, :, 0:nkeys], p_sc[0:nkeys, :], preferred_element_type=F32)
    return ot * jnp.concatenate(inv_l, axis=1)


def _attn_kernel(qt_ref, k_ref, vt_ref, o_ref, s_sc, p_sc, *, group, tq, ctx_len, ctx_q_tiles, q_off):
    dv = vt_ref.shape[2]
    t = k_ref.shape[2]
    qt = jnp.concatenate([qt_ref[0, g] for g in range(group)], axis=1)

    def run(nkeys):
        ot = _attn_cols(qt, k_ref, vt_ref, s_sc, p_sc, nkeys)
        for g in range(group):
            o_ref[0, :, g * dv:(g + 1) * dv] = ot[:, g * tq:(g + 1) * tq].T.astype(o_ref.dtype)

    if ctx_q_tiles > q_off:
        is_ctx = pl.program_id(2) + q_off < ctx_q_tiles
        pl.when(is_ctx)(lambda: run(ctx_len))
        pl.when(jnp.logical_not(is_ctx))(lambda: run(t))
    else:
        run(t)


def _attention(qt, k, vt, ctx_len, tq, skip_ctx_queries, out_dtype):
    b, hq, dk, t = qt.shape
    hkv, dv = k.shape[1], vt.shape[2]
    group = hq // hkv
    tq = _tile(ctx_len, tq, 128)
    ctx_q_tiles = ctx_len // tq
    q_off = ctx_q_tiles if skip_ctx_queries else 0
    nq = t // tq - q_off
    kern = functools.partial(_attn_kernel, group=group, tq=tq, ctx_len=ctx_len,
                             ctx_q_tiles=ctx_q_tiles, q_off=q_off)
    rows = group * tq
    assert rows % ATTN_QB == 0 and t % ATTN_KC == 0 and ctx_len % ATTN_KC == 0
    return pl.pallas_call(
        kern,
        grid=(b, hkv, nq),
        in_specs=[pl.BlockSpec((1, group, dk, tq), lambda bi, h, i: (bi, h, 0, i + q_off)),
                  pl.BlockSpec((1, 1, t, dk), lambda bi, h, i: (bi, h, 0, 0)),
                  pl.BlockSpec((1, 1, dv, t), lambda bi, h, i: (bi, h, 0, 0))],
        out_specs=pl.BlockSpec((1, tq, group * dv), lambda bi, h, i: (bi, i, h)),
        out_shape=jax.ShapeDtypeStruct((b, nq * tq, hq * dv), out_dtype),
        scratch_shapes=[pltpu.VMEM((t, rows), F32), pltpu.VMEM((t, rows), BF16)],
        compiler_params=_params("parallel", "parallel", "arbitrary"),
        name="attention_dk%d" % dk,
    )(qt, k, vt)


def _conv_kernel(x_ref, w_ref, b_ref, o_ref, pad_sc, *, ctx_len, rows):
    t = x_ref.shape[1]
    nch = x_ref.shape[2]
    halo = 8
    segs = ((0, ctx_len), (ctx_len, t))
    zeros = jnp.zeros((halo, nch), F32)
    for si, (lo, hi) in enumerate(segs):
        pad_sc[lo + si * halo:lo + (si + 1) * halo, :] = zeros
        for r0 in range(lo, hi, rows):
            pad_sc[r0 + (si + 1) * halo:r0 + (si + 1) * halo + rows, :] = x_ref[0, r0:r0 + rows, :].astype(F32)
    pad_sc[t + 2 * halo:t + 3 * halo, :] = zeros
    w = w_ref[...]
    bias = b_ref[...]
    for si, (lo, hi) in enumerate(segs):
        for r0 in range(lo, hi, rows):
            base = r0 + (si + 1) * halo - SSD_CONV // 2
            acc = bias + w[0:1, :] * pad_sc[base:base + rows, :]
            for kk in range(1, SSD_CONV):
                acc = acc + w[kk:kk + 1, :] * pad_sc[base + kk:base + kk + rows, :]
            o_ref[0, r0:r0 + rows, :] = _silu(acc).astype(o_ref.dtype)


def _conv(p3, conv_w, conv_b, ctx_len, out_dtype):
    b, t, _ = p3.shape
    nch = 256
    rows = _tile(ctx_len, 256)
    blk0 = P_OFF['xbc'] // nch
    kern = functools.partial(_conv_kernel, ctx_len=ctx_len, rows=rows)
    return pl.pallas_call(
        kern,
        grid=(b, SSD_CONV_DIM // nch),
        in_specs=[pl.BlockSpec((1, t, nch), lambda bi, j: (bi, 0, blk0 + j)),
                  pl.BlockSpec((SSD_CONV, nch), lambda bi, j: (0, j)),
                  pl.BlockSpec((1, nch), lambda bi, j: (0, j))],
        out_specs=pl.BlockSpec((1, t, nch), lambda bi, j: (bi, 0, j)),
        out_shape=jax.ShapeDtypeStruct((b, t, SSD_CONV_DIM), out_dtype),
        scratch_shapes=[pltpu.VMEM((t + 24, nch), F32)],
        compiler_params=_params("parallel", "parallel"),
        name="ssd_conv",
    )(p3, conv_w, conv_b.reshape(1, SSD_CONV_DIM))


def _split3(v):
    hi = v.astype(BF16)
    r1 = v - hi.astype(F32)
    mid = r1.astype(BF16)
    lo = (r1 - mid.astype(F32)).astype(BF16)
    return hi, mid, lo


def _expand_heads(col):
    q = col.shape[0]
    first = lax.broadcasted_iota(jnp.int32, (q, 2 * SSD_P), 1) < SSD_P
    tiles = []
    for e in range(0, SSD_E, 2):
        tiles.append(jnp.where(first, jnp.broadcast_to(col[:, e:e + 1], (q, 2 * SSD_P)),
                               jnp.broadcast_to(col[:, e + 1:e + 2], (q, 2 * SSD_P))))
    return jnp.concatenate(tiles, axis=1)


def _ssd_direction(xs, bm, cm, dt_col_raw, dt_row_raw, bias_col, bias_row, a_col, a_row, h_ref,
                   backward):
    q = xs.shape[0]
    ri = lax.broadcasted_iota(jnp.int32, (q, q), 0)
    ci = lax.broadcasted_iota(jnp.int32, (q, q), 1)
    low = ri >= ci
    keep = (ri <= ci) if backward else low
    tri_col = jnp.where(keep, 1.0, 0.0).astype(BF16)
    tri_row = jnp.where((ri >= ci) if backward else (ri <= ci), 1.0, 0.0).astype(BF16)

    dt_c = _softplus(dt_col_raw + bias_col)
    dt_r = _softplus(dt_row_raw + bias_row)
    cum_c = sum(jnp.dot(tri_col, part, preferred_element_type=F32) for part in _split3(dt_c * a_col))
    cum_r = sum(jnp.dot(part, tri_row, preferred_element_type=F32) for part in _split3(dt_r * a_row))
    cum_full = _expand_heads(cum_c)
    dt_full = _expand_heads(dt_c)
    total = cum_full[0:1, :] if backward else cum_full[q - 1:q, :]

    xdt = xs * dt_full
    cmb = cm.astype(BF16)
    cb = lax.dot_general(cmb, bm.astype(BF16), (((1,), (1,)), ((), ())), preferred_element_type=F32)
    h = h_ref[...]
    y_off = jnp.dot(cmb, h.astype(BF16), preferred_element_type=F32) * jnp.exp(cum_full)
    wgt = (xdt * jnp.exp(total - cum_full)).astype(BF16)
    h_ref[...] = jnp.exp(total) * h + jnp.dot(bm.T.astype(BF16), wgt, preferred_element_type=F32)

    xdt_b = xdt.astype(BF16)
    lane = lax.broadcasted_iota(jnp.int32, (q, 2 * SSD_P), 1)
    parts = []
    for pair in range(SSD_E // 2):
        x_pair = xdt_b[:, pair * 2 * SSD_P:(pair + 1) * 2 * SSD_P]
        ys = []
        for e in (2 * pair, 2 * pair + 1):
            seg = cum_c[:, e:e + 1] - cum_r[e:e + 1, :]
            dec = jnp.exp(jnp.where(keep, seg, -jnp.inf))
            ys.append(jnp.dot((cb * dec).astype(BF16), x_pair, preferred_element_type=F32))
        parts.append(jnp.where(lane < SSD_P, ys[0], ys[1]))
    return jnp.concatenate(parts, axis=1) + y_off


def _ssd_kernel(xf_ref, bf_ref, cf_ref, xb_ref, bb_ref, cb_ref, dcf_ref, drf_ref, dcb_ref, drb_ref,
                bias_c_ref, bias_r_ref, a_c_ref, a_r_ref, d_ref, yf_ref, yb_ref, hf_sc, hb_sc):
    @pl.when(pl.program_id(2) == 0)
    def _():
        hf_sc[...] = jnp.zeros(hf_sc.shape, F32)
        hb_sc[...] = jnp.zeros(hb_sc.shape, F32)

    nsub = xf_ref.shape[1] // SSD_Q
    for i in range(nsub):
        rf = slice(i * SSD_Q, (i + 1) * SSD_Q)
        rb = slice((nsub - 1 - i) * SSD_Q, (nsub - i) * SSD_Q)
        xs = xf_ref[0, rf, :].astype(F32)
        yf = _ssd_direction(xs, bf_ref[0, rf, :].astype(F32), cf_ref[0, rf, :].astype(F32),
                            dcf_ref[0, 0, 0, rf, :], drf_ref[0, 0, 0, :, rf],
                            bias_c_ref[0, 0], bias_r_ref[0, 0], a_c_ref[0, 0], a_r_ref[0, 0], hf_sc, False)
        yf_ref[0, rf, :] = (yf + d_ref[0] * xs).astype(yf_ref.dtype)
        yb = _ssd_direction(xb_ref[0, rb, :].astype(F32), bb_ref[0, rb, :].astype(F32),
                            cb_ref[0, rb, :].astype(F32), dcb_ref[0, 0, 0, rb, :], drb_ref[0, 0, 0, :, rb],
                            bias_c_ref[1, 0], bias_r_ref[1, 0], a_c_ref[1, 0], a_r_ref[1, 0], hb_sc, True)
        yb_ref[0, rb, :] = yb.astype(yb_ref.dtype)


def _ssd(xc, dt_col, dt_row, bias_c, bias_r, a_c, a_r, d_full, ctx_len, out_dtype):
    b, t, _ = xc.shape
    rows = _tile(ctx_len, SSD_ROWS, SSD_Q)
    nblk = t // rows
    nctx = ctx_len // rows
    bblk = SSD_INNER // SSD_N
    cblk = bblk + SSD_G

    def bidx(c):
        return jnp.where(c < nctx, nctx - 1 - c, nblk - 1 - (c - nctx))

    fx = lambda bi, g, c: (bi, c, g)
    fb = lambda bi, g, c: (bi, c, bblk + g)
    fc = lambda bi, g, c: (bi, c, cblk + g)
    bx = lambda bi, g, c: (bi, bidx(c), g)
    bb = lambda bi, g, c: (bi, bidx(c), bblk + g)
    bc = lambda bi, g, c: (bi, bidx(c), cblk + g)
    small = lambda a: pl.BlockSpec((2, 1) + a.shape[2:], lambda bi, g, c: (0, g, 0, 0))
    return pl.pallas_call(
        _ssd_kernel,
        grid=(b, SSD_G, nblk),
        in_specs=[pl.BlockSpec((1, rows, SSD_GW), fx), pl.BlockSpec((1, rows, SSD_N), fb),
                  pl.BlockSpec((1, rows, SSD_N), fc),
                  pl.BlockSpec((1, rows, SSD_GW), bx), pl.BlockSpec((1, rows, SSD_N), bb),
                  pl.BlockSpec((1, rows, SSD_N), bc),
                  pl.BlockSpec((1, 1, 1, rows, SSD_E), lambda bi, g, c: (bi, 0, g, c, 0)),
                  pl.BlockSpec((1, 1, 1, SSD_E, rows), lambda bi, g, c: (bi, 0, g, 0, c)),
                  pl.BlockSpec((1, 1, 1, rows, SSD_E), lambda bi, g, c: (bi, 1, g, bidx(c), 0)),
                  pl.BlockSpec((1, 1, 1, SSD_E, rows), lambda bi, g, c: (bi, 1, g, 0, bidx(c))),
                  small(bias_c), small(bias_r), small(a_c), small(a_r),
                  pl.BlockSpec((1, 1, SSD_GW), lambda bi, g, c: (g, 0, 0))],
        out_specs=[pl.BlockSpec((1, rows, SSD_GW), fx), pl.BlockSpec((1, rows, SSD_GW), bx)],
        out_shape=[jax.ShapeDtypeStruct((b, t, SSD_INNER), out_dtype)] * 2,
        scratch_shapes=[pltpu.VMEM((SSD_N, SSD_GW), F32), pltpu.VMEM((SSD_N, SSD_GW), F32)],
        compiler_params=_params("parallel", "parallel", "arbitrary"),
        name="ssd_scan",
    )(xc, xc, xc, xc, xc, xc, dt_col, dt_row, dt_col, dt_row, bias_c, bias_r, a_c, a_r, d_full)


MERGE_TN = 512


def _merge_kernel(ya_ref, ga_ref, yb_ref, gb_ref, yf_ref, ybk_ref, z_ref, nrm_ref,
                  mg_ref, wa_ref, wb_ref, wc_ref, u_ref, c_sc):
    a_in = (ya_ref[0].astype(F32) * _silu(ga_ref[0].astype(F32))).astype(BF16)
    b_in = (yb_ref[0].astype(F32) * _silu(gb_ref[0].astype(F32))).astype(BF16)
    v = (yf_ref[0].astype(F32) + ybk_ref[0].astype(F32)) * _silu(z_ref[0].astype(F32))
    for g in range(SSD_G):
        sl = slice(g * SSD_GW, (g + 1) * SSD_GW)
        c_sc[:, sl] = (_rms(v[:, sl]) * nrm_ref[:, sl]).astype(BF16)
    c_in = c_sc[...]
    d = u_ref.shape[-1]
    for j in range(0, d, MERGE_TN):
        sl = slice(j, j + MERGE_TN)
        br_a = jnp.dot(a_in, wa_ref[:, sl], preferred_element_type=F32)
        br_b = jnp.dot(b_in, wb_ref[:, sl], preferred_element_type=F32)
        br_c = jnp.dot(c_in, wc_ref[:, sl], preferred_element_type=F32)
        u = (jax.nn.sigmoid(mg_ref[0, :, j:j + MERGE_TN].astype(F32)) * br_a
             + jax.nn.sigmoid(mg_ref[0, :, d + j:d + j + MERGE_TN].astype(F32)) * br_b
             + jax.nn.sigmoid(mg_ref[0, :, 2 * d + j:2 * d + j + MERGE_TN].astype(F32)) * br_c)
        u_ref[0, :, sl] = u.astype(u_ref.dtype)


def _merge(ya, yb, yf, ybk, p3, ssd_norm, wa, wb, wc, row_off):
    b, rows, _ = ya.shape
    tm = _tile(rows, 256)
    d = D_MODEL
    ro = row_off // tm
    assert P_OFF['mg'] == 0

    def col(name, width):
        blk = P_OFF[name] // width
        return pl.BlockSpec((1, tm, width), lambda bi, i: (bi, i + ro, blk))

    loc = lambda w: pl.BlockSpec((1, tm, w), lambda bi, i: (bi, i, 0))
    glob = lambda w: pl.BlockSpec((1, tm, w), lambda bi, i: (bi, i + ro, 0))
    wspec = lambda k: pl.BlockSpec((k, d), lambda bi, i: (0, 0), pipeline_mode=pl.Buffered(1))
    return pl.pallas_call(
        _merge_kernel,
        grid=(b, rows // tm),
        in_specs=[loc(MLA_WIDTH), col('ga', MLA_WIDTH), loc(GQA_WIDTH), col('gb', GQA_WIDTH),
                  glob(SSD_INNER), glob(SSD_INNER), col('z', SSD_INNER),
                  pl.BlockSpec((1, SSD_INNER), lambda bi, i: (0, 0)),
                  glob(N_BRANCH * d), wspec(MLA_WIDTH), wspec(GQA_WIDTH), wspec(SSD_INNER)],
        out_specs=pl.BlockSpec((1, tm, d), lambda bi, i: (bi, i, 0)),
        out_shape=jax.ShapeDtypeStruct((b, rows, d), BF16),
        scratch_shapes=[pltpu.VMEM((tm, SSD_INNER), BF16)],
        compiler_params=_params("parallel", "parallel"),
        name="merge",
    )(ya, p3, yb, p3, yf, ybk, p3, ssd_norm.reshape(1, SSD_INNER), p3, wa, wb, wc)


def _out_ln_kernel(u_ref, w_ref, x_ref, gate_ref, g_ref, b_ref, *rest, with_next):
    out = jnp.dot(u_ref[0], w_ref[...], preferred_element_type=F32)
    r = DEEPNORM_ALPHA * x_ref[0] + gate_ref[0] * out
    xn = _layer_norm(r) * g_ref[...] + b_ref[...]
    if with_next:
        sh_ref, sc_ref, xo_ref, xm_ref = rest
        xo_ref[0] = xn
        xm_ref[0] = (_layer_norm(xn) * (1.0 + sc_ref[0]) + sh_ref[0]).astype(xm_ref.dtype)
    else:
        (xo_ref,) = rest
        xo_ref[0] = xn


def _out_ln(u, w_out, xc, tab, ln_g, ln_b, next_tab, nb, ctx_len, row_off):
    b, rows, d = u.shape
    tm = _tile(ctx_len, 256)
    ro = row_off // tm
    row = _mod_row_index(nb, ctx_len // tm)
    with_next = next_tab is not None
    loc = pl.BlockSpec((1, tm, d), lambda bi, i: (bi, i, 0))
    vec = pl.BlockSpec((1, d), lambda bi, i: (0, 0))
    in_specs = [loc, pl.BlockSpec((d, d), lambda bi, i: (0, 0)),
                pl.BlockSpec((1, tm, d), lambda bi, i: (bi, i + ro, 0)),
                pl.BlockSpec((1, 1, d), lambda bi, i: (row(bi, i + ro), 0, 2)), vec, vec]
    args = [u, w_out, xc, tab, ln_g.reshape(1, d), ln_b.reshape(1, d)]
    out_specs = [loc]
    out_shape = [jax.ShapeDtypeStruct((b, rows, d), F32)]
    if with_next:
        in_specs += [pl.BlockSpec((1, 1, d), lambda bi, i: (row(bi, i + ro), 0, 0)),
                     pl.BlockSpec((1, 1, d), lambda bi, i: (row(bi, i + ro), 0, 1))]
        args += [next_tab, next_tab]
        out_specs.append(loc)
        out_shape.append(jax.ShapeDtypeStruct((b, rows, d), BF16))
    return pl.pallas_call(
        functools.partial(_out_ln_kernel, with_next=with_next),
        grid=(b, rows // tm),
        in_specs=in_specs,
        out_specs=out_specs,
        out_shape=out_shape,
        compiler_params=_params("parallel", "parallel"),
        name="out_ln",
    )(*args)


def _rope_angles(rows, dim):
    row, col = jnp.meshgrid(jnp.arange(rows, dtype=F32), jnp.arange(GRID_W, dtype=F32), indexing='ij')
    half = dim // 2
    inv_freq = ROPE_THETA ** (-jnp.arange(0, half, 2, dtype=F32) / half)
    ang_r = row.reshape(-1, 1) * inv_freq
    ang_c = col.reshape(-1, 1) * inv_freq
    return jnp.concatenate([ang_r, ang_r, ang_c, ang_c], axis=-1)


def _rope_tables(seq, ctx_len, dim):
    ang = _rope_angles(seq // GRID_W, dim)
    cos = jnp.concatenate([jnp.ones((ctx_len, dim), F32), jnp.cos(ang)], axis=0)
    sin = jnp.concatenate([jnp.zeros((ctx_len, dim), F32), jnp.sin(ang)], axis=0)
    return cos, sin


def _roll_tables(cos, sin, dim, scale):
    t = cos.shape[0]
    quarter = dim // 4
    first = (jnp.arange(dim) % (2 * quarter)) < quarter
    s1 = jnp.where(first, -sin, 0.0)
    s2 = jnp.where(first, 0.0, sin)
    pad = lambda a: jnp.pad(a * scale, ((0, 0), (0, 128 - dim)))
    return pad(cos), pad(s1), pad(s2)


def _rot_matrix(dim):
    quarter = dim // 4
    r = np.zeros((dim, dim), np.float32)
    for i in range(dim):
        blk = i // quarter
        if blk % 2 == 0:
            r[i + quarter, i] = -1.0
        else:
            r[i - quarter, i] = 1.0
    return jnp.asarray(r)


def _mla_q_weights(w_uq):
    w = w_uq.reshape(MLA_Q_LORA, MLA_HEADS, MLA_QK)
    nope, pe = w[..., :MLA_NOPE], w[..., MLA_NOPE:]
    pe_rot = jnp.einsum('khd,de->khe', pe, _rot_matrix(MLA_ROPE), precision=HIGHEST)
    zpad = jnp.zeros((MLA_Q_LORA, MLA_HEADS, MLA_QK_PAD - MLA_QK), F32)
    main = jnp.concatenate([nope, pe, zpad], axis=-1)
    rot = jnp.concatenate([jnp.zeros_like(nope), pe_rot, zpad], axis=-1)
    return (main.reshape(MLA_Q_LORA, -1).astype(BF16), rot.reshape(MLA_Q_LORA, -1).astype(BF16))


def _permute_w_in(w):
    wb = w.astype(BF16)
    parts = [wb[:, IN_OFFSETS[n]:IN_OFFSETS[n] + IN_WIDTHS[n]] for n in P_ORDER]
    parts.append(jnp.zeros((w.shape[0], P_WIDTH - P_USED), BF16))
    return jnp.concatenate(parts, axis=1)


P_DTYPE = BF16
Y_DTYPE = BF16


def kernel(x, c, ctx, c_ctx, w_mod, b_mod, w_in, mla_q_norm, mla_w_uq, mla_kv_norm, mla_w_ukv,
           gqa_q_norm, gqa_k_norm, ssd_conv_w, ssd_conv_b, ssd_a_log, ssd_dt_bias, ssd_d, ssd_norm,
           w_br_a, w_br_b, w_br_c, w_out, ln_g, ln_b):
    nb, seq, d = x.shape
    ctx_len = ctx.shape[1]
    t = ctx_len + seq
    depth = w_in.shape[0]
    assert d == D_MODEL and nb < 8 and seq % GRID_W == 0
    assert ctx_len % SSD_Q == 0 and seq % SSD_Q == 0

    cos_a, sin_a = _rope_tables(seq, ctx_len, MLA_ROPE)
    cos_b, sin_b = _rope_tables(seq, ctx_len, GQA_DIM)
    sq = MLA_QK ** -0.5 * LOG2E
    zq = jnp.zeros((t, MLA_QK_PAD - MLA_QK), F32)
    cosq = jnp.concatenate([jnp.full((t, MLA_NOPE), sq, F32), cos_a * sq, zq], axis=1)
    sinq = jnp.concatenate([jnp.zeros((t, MLA_NOPE), F32), sin_a * sq, zq], axis=1)
    mla_tabs = (cosq, sinq) + _roll_tables(cos_a, sin_a, MLA_ROPE, 1.0)
    gqa_tabs = (_roll_tables(cos_b, sin_b, GQA_DIM, GQA_DIM ** -0.5 * LOG2E)
                + _roll_tables(cos_b, sin_b, GQA_DIM, 1.0))

    c_rows = jnp.zeros((8, d), F32).at[:nb].set(c).at[nb].set(c_ctx)
    tabs = [_mod_rows(c_rows, w_mod[l], b_mod[l]).reshape(8, 1, 3 * d) for l in range(depth)]

    xc = jnp.concatenate([ctx, x], axis=1)
    xm = _ln_mod(xc, tabs[0], nb, ctx_len)

    for l in range(depth):
        last = l == depth - 1
        wp = _permute_w_in(w_in[l])
        p3 = _in_proj(xm.reshape(nb * t, d), wp, P_DTYPE).reshape(nb, t, P_WIDTH)

        wqm, wqr = _mla_q_weights(mla_w_uq[l])
        qa, ka, va = _mla_prep(p3, mla_q_norm[l].reshape(1, -1), mla_kv_norm[l].reshape(1, -1),
                               wqm, wqr, mla_w_ukv[l].astype(BF16), mla_tabs, ctx_len)
        ya = _attention(qa, ka, va, ctx_len, 256, last, Y_DTYPE)
        qb, kb, vb = _gqa_prep(p3, gqa_q_norm[l].reshape(1, -1), gqa_k_norm[l].reshape(1, -1),
                               gqa_tabs, ctx_len)
        yb = _attention(qb, kb, vb, ctx_len, 128, last, Y_DTYPE)

        xconv = _conv(p3, ssd_conv_w[l], ssd_conv_b[l], ctx_len, Y_DTYPE)
        dtr = p3[:, :, P_OFF['dtr']:P_OFF['dtr'] + 2 * SSD_HEADS].astype(F32)
        dt5 = dtr.reshape(nb, t, 2, SSD_G, SSD_E)
        dt_col = jnp.transpose(dt5, (0, 2, 3, 1, 4))
        dt_row = jnp.transpose(dt5, (0, 2, 3, 4, 1))
        bias = ssd_dt_bias[l].astype(F32).reshape(2, SSD_G, SSD_E)
        a = -jnp.exp(ssd_a_log[l].astype(F32)).reshape(2, SSD_G, SSD_E)
        d_full = jnp.repeat(ssd_d[l].astype(F32), SSD_P).reshape(SSD_G, 1, SSD_GW)
        yf, ybk = _ssd(xconv, dt_col, dt_row, bias[:, :, None, :], bias[:, :, :, None],
                       a[:, :, None, :], a[:, :, :, None], d_full, ctx_len, Y_DTYPE)

        row_off = ctx_len if last else 0
        u = _merge(ya, yb, yf, ybk, p3, ssd_norm[l], w_br_a[l].astype(BF16), w_br_b[l].astype(BF16),
                   w_br_c[l].astype(BF16), row_off)
        if last:
            (xo,) = _out_ln(u, w_out[l].astype(BF16), xc, tabs[l], ln_g[l], ln_b[l], None, nb,
                            ctx_len, row_off)
            return xo
        xc, xm = _out_ln(u, w_out[l].astype(BF16), xc, tabs[l], ln_g[l], ln_b[l], tabs[l + 1], nb,
                         ctx_len, row_off)
```

```python
import functools
import math

import numpy as np
import jax
import jax.numpy as jnp
from jax import lax
from jax.experimental import pallas as pl
from jax.experimental.pallas import tpu as pltpu

F32 = jnp.float32
BF16 = jnp.bfloat16
HIGHEST = lax.Precision.HIGHEST

D_MODEL = 2048
DEPTH = 2
GRID_W = 64
ROPE_THETA = 10000.0
EPS = 1e-6

MLA_HEADS = 8
MLA_Q_LORA = 512
MLA_KV_LORA = 256
MLA_NOPE = 128
MLA_ROPE = 64
MLA_V = 128
MLA_QK = MLA_NOPE + MLA_ROPE
MLA_QK_PAD = 256
MLA_WIDTH = MLA_HEADS * MLA_V

GQA_HEADS = 8
GQA_KV_HEADS = 2
GQA_GROUP = GQA_HEADS // GQA_KV_HEADS
GQA_DIM = 128
GQA_WIDTH = GQA_HEADS * GQA_DIM
GQA_KV_WIDTH = GQA_KV_HEADS * GQA_DIM

SSD_INNER = D_MODEL
SSD_P = 64
SSD_HEADS = SSD_INNER // SSD_P
SSD_G = 4
SSD_E = SSD_HEADS // SSD_G
SSD_N = 128
SSD_CONV = 5
SSD_Q = 128
SSD_ROWS = 256
SSD_GW = SSD_E * SSD_P
SSD_CONV_DIM = SSD_INNER + 2 * SSD_G * SSD_N

N_BRANCH = 3
IN_SPLITS = (MLA_Q_LORA, MLA_KV_LORA, MLA_ROPE, MLA_WIDTH, GQA_WIDTH, GQA_KV_WIDTH, GQA_KV_WIDTH,
             GQA_WIDTH, SSD_INNER, SSD_CONV_DIM, 2 * SSD_HEADS, N_BRANCH * D_MODEL)
IN_NAMES = ('cq', 'ckv', 'kr', 'ga', 'gq', 'gk', 'gv', 'gb', 'z', 'xbc', 'dtr', 'mg')
IN_OFFSETS = dict(zip(IN_NAMES, np.concatenate([[0], np.cumsum(IN_SPLITS)[:-1]]).tolist()))
IN_WIDTHS = dict(zip(IN_NAMES, IN_SPLITS))
P_ORDER = ('mg', 'z', 'xbc', 'ga', 'gq', 'gb', 'cq', 'ckv', 'gk', 'gv', 'kr', 'dtr')
P_OFF = {}
_o = 0
for _n in P_ORDER:
    P_OFF[_n] = _o
    _o += IN_WIDTHS[_n]
P_USED = _o
P_TN = 512
P_WIDTH = -(-P_USED // P_TN) * P_TN

DEEPNORM_ALPHA = (2 * DEPTH) ** 0.25

VMEM_LIMIT = 56 * 2 ** 20


def _params(*sem):
    return pltpu.CompilerParams(dimension_semantics=sem, vmem_limit_bytes=VMEM_LIMIT)


def _tile(n, target, align=8):
    t = min(n, target)
    while t > align and (n % t or t % align):
        t -= align
    assert n % t == 0, (n, target)
    return t


def _silu(v):
    return v * jax.nn.sigmoid(v)


def _softplus(v):
    return jnp.maximum(v, 0.0) + jnp.log1p(jnp.exp(-jnp.abs(v)))


def _layer_norm(v):
    mu = jnp.mean(v, axis=-1, keepdims=True)
    vc = v - mu
    var = jnp.mean(vc * vc, axis=-1, keepdims=True)
    return vc * lax.rsqrt(var + EPS)


def _rms(v):
    return v * lax.rsqrt(jnp.mean(v * v, axis=-1, keepdims=True) + EPS)


def _mod_kernel(c_ref, w_ref, b_ref, o_ref):
    a = _silu(c_ref[...]).astype(BF16)
    o_ref[...] = jnp.dot(a, w_ref[...].astype(BF16), preferred_element_type=F32) + b_ref[...]


def _mod_rows(c_rows, w_mod, b_mod):
    r, d = c_rows.shape
    n = w_mod.shape[1]
    tn = 512
    return pl.pallas_call(
        _mod_kernel,
        grid=(n // tn,),
        in_specs=[pl.BlockSpec((r, d), lambda j: (0, 0)),
                  pl.BlockSpec((d, tn), lambda j: (0, j)),
                  pl.BlockSpec((1, tn), lambda j: (0, j))],
        out_specs=pl.BlockSpec((r, tn), lambda j: (0, j)),
        out_shape=jax.ShapeDtypeStruct((r, n), F32),
        compiler_params=_params("arbitrary"),
        name="mod_rows",
    )(c_rows, w_mod, b_mod.reshape(1, n))


def _ln_mod_kernel(x_ref, sh_ref, sc_ref, o_ref):
    y = _layer_norm(x_ref[0])
    o_ref[0] = (y * (1.0 + sc_ref[0]) + sh_ref[0]).astype(o_ref.dtype)


def _mod_row_index(nb, ctx_tiles):
    return lambda b, i: jnp.where(i < ctx_tiles, nb, b)


def _ln_mod(xc, tab, nb, ctx_len):
    b, t, d = xc.shape
    tm = _tile(ctx_len, 256)
    row = _mod_row_index(nb, ctx_len // tm)
    return pl.pallas_call(
        _ln_mod_kernel,
        grid=(b, t // tm),
        in_specs=[pl.BlockSpec((1, tm, d), lambda bi, i: (bi, i, 0)),
                  pl.BlockSpec((1, 1, d), lambda bi, i: (row(bi, i), 0, 0)),
                  pl.BlockSpec((1, 1, d), lambda bi, i: (row(bi, i), 0, 1))],
        out_specs=pl.BlockSpec((1, tm, d), lambda bi, i: (bi, i, 0)),
        out_shape=jax.ShapeDtypeStruct((b, t, d), BF16),
        compiler_params=_params("parallel", "parallel"),
        name="ln_mod",
    )(xc, tab, tab)


def _matmul_kernel(x_ref, w_ref, o_ref):
    o_ref[...] = jnp.dot(x_ref[...], w_ref[...], preferred_element_type=F32).astype(o_ref.dtype)


def _in_proj(xm2, wp, out_dtype):
    m, k = xm2.shape
    n = wp.shape[1]
    tm = _tile(m, 1024)
    tn = P_TN
    return pl.pallas_call(
        _matmul_kernel,
        grid=(m // tm, n // tn),
        in_specs=[pl.BlockSpec((tm, k), lambda i, j: (i, 0)),
                  pl.BlockSpec((k, tn), lambda i, j: (0, j))],
        out_specs=pl.BlockSpec((tm, tn), lambda i, j: (i, j)),
        out_shape=jax.ShapeDtypeStruct((m, n), out_dtype),
        compiler_params=_params("parallel", "arbitrary"),
        name="in_proj",
    )(xm2, wp)


def _mla_prep_kernel(cq_ref, ckv_ref, kr_ref, qn_ref, kvn_ref, wqm_ref, wqr_ref, wkv_ref,
                     cosq_ref, sinq_ref, ck_ref, s1_ref, s2_ref, qa_ref, ka_ref, va_ref):
    cqn = (_rms(cq_ref[0].astype(F32)) * qn_ref[...]).astype(BF16)
    qm = jnp.dot(cqn, wqm_ref[...], preferred_element_type=F32)
    qr = jnp.dot(cqn, wqr_ref[...], preferred_element_type=F32)
    cosq = cosq_ref[...]
    sinq = sinq_ref[...]
    for h in range(MLA_HEADS):
        sl = slice(h * MLA_QK_PAD, (h + 1) * MLA_QK_PAD)
        qa_ref[0, h] = (qm[:, sl] * cosq + qr[:, sl] * sinq).T.astype(qa_ref.dtype)
    ckvn = (_rms(ckv_ref[0].astype(F32)) * kvn_ref[...]).astype(BF16)
    kv = jnp.dot(ckvn, wkv_ref[...], preferred_element_type=F32)
    kr = kr_ref[0].astype(F32)
    kpe = (kr * ck_ref[...] + pltpu.roll(kr, 128 - 16, axis=1) * s1_ref[...]
           + pltpu.roll(kr, 16, axis=1) * s2_ref[...]).astype(ka_ref.dtype)
    for h in range(MLA_HEADS):
        base = h * (MLA_NOPE + MLA_V)
        ka_ref[0, h, :, 0:MLA_NOPE] = kv[:, base:base + MLA_NOPE].astype(ka_ref.dtype)
        ka_ref[0, h, :, MLA_NOPE:MLA_QK_PAD] = kpe
        va_ref[0, h] = kv[:, base + MLA_NOPE:base + MLA_NOPE + MLA_V].T.astype(va_ref.dtype)


def _mla_prep(p3, qn, kvn, wqm, wqr, wkv, tabs, ctx_len):
    b, t, _ = p3.shape
    tm = _tile(ctx_len, 256)
    cosq, sinq, ck, s1, s2 = tabs

    def col(name, width):
        blk = P_OFF[name] // width
        return pl.BlockSpec((1, tm, width), lambda bi, i: (bi, i, blk))

    def full(a):
        return pl.BlockSpec(a.shape, lambda bi, i: (0,) * a.ndim)

    def rows(a):
        return pl.BlockSpec((tm, a.shape[1]), lambda bi, i: (i, 0))

    hm = lambda w: pl.BlockSpec((1, MLA_HEADS, tm, w), lambda bi, i: (bi, 0, i, 0))
    return pl.pallas_call(
        _mla_prep_kernel,
        grid=(b, t // tm),
        in_specs=[col('cq', MLA_Q_LORA), col('ckv', MLA_KV_LORA), col('kr', 128),
                  full(qn), full(kvn), full(wqm), full(wqr), full(wkv),
                  rows(cosq), rows(sinq), rows(ck), rows(s1), rows(s2)],
        out_specs=[pl.BlockSpec((1, MLA_HEADS, MLA_QK_PAD, tm), lambda bi, i: (bi, 0, 0, i)),
                   hm(MLA_QK_PAD),
                   pl.BlockSpec((1, MLA_HEADS, MLA_V, tm), lambda bi, i: (bi, 0, 0, i))],
        out_shape=[jax.ShapeDtypeStruct((b, MLA_HEADS, MLA_QK_PAD, t), BF16),
                   jax.ShapeDtypeStruct((b, MLA_HEADS, t, MLA_QK_PAD), BF16),
                   jax.ShapeDtypeStruct((b, MLA_HEADS, MLA_V, t), BF16)],
        compiler_params=_params("parallel", "parallel"),
        name="mla_prep",
    )(p3, p3, p3, qn, kvn, wqm, wqr, wkv, cosq, sinq, ck, s1, s2)


def _rope128(y, c, s1, s2):
    return y * c + pltpu.roll(y, 128 - 32, axis=1) * s1 + pltpu.roll(y, 32, axis=1) * s2


def _gqa_prep_kernel(gq_ref, gk_ref, gv_ref, qn_ref, kn_ref, cq_ref, s1q_ref, s2q_ref,
                     ck_ref, s1k_ref, s2k_ref, qb_ref, kb_ref, vb_ref):
    gq = gq_ref[0].astype(F32)
    for h in range(GQA_HEADS):
        y = _rms(gq[:, h * GQA_DIM:(h + 1) * GQA_DIM]) * qn_ref[...]
        qb_ref[0, h] = _rope128(y, cq_ref[...], s1q_ref[...], s2q_ref[...]).T.astype(qb_ref.dtype)
    gk = gk_ref[0].astype(F32)
    gv = gv_ref[0].astype(F32)
    for h in range(GQA_KV_HEADS):
        y = _rms(gk[:, h * GQA_DIM:(h + 1) * GQA_DIM]) * kn_ref[...]
        kb_ref[0, h] = _rope128(y, ck_ref[...], s1k_ref[...], s2k_ref[...]).astype(kb_ref.dtype)
        vb_ref[0, h] = gv[:, h * GQA_DIM:(h + 1) * GQA_DIM].T.astype(vb_ref.dtype)


def _gqa_prep(p3, qn, kn, tabs, ctx_len):
    b, t, _ = p3.shape
    tm = _tile(ctx_len, 256)

    def col(name, width):
        blk = P_OFF[name] // width
        return pl.BlockSpec((1, tm, width), lambda bi, i: (bi, i, blk))

    def full(a):
        return pl.BlockSpec(a.shape, lambda bi, i: (0,) * a.ndim)

    def rows(a):
        return pl.BlockSpec((tm, a.shape[1]), lambda bi, i: (i, 0))

    hm = lambda nh: pl.BlockSpec((1, nh, tm, GQA_DIM), lambda bi, i: (bi, 0, i, 0))
    hmt = lambda nh: pl.BlockSpec((1, nh, GQA_DIM, tm), lambda bi, i: (bi, 0, 0, i))
    return pl.pallas_call(
        _gqa_prep_kernel,
        grid=(b, t // tm),
        in_specs=[col('gq', GQA_WIDTH), col('gk', GQA_KV_WIDTH), col('gv', GQA_KV_WIDTH),
                  full(qn), full(kn)] + [rows(a) for a in tabs],
        out_specs=[hmt(GQA_HEADS), hm(GQA_KV_HEADS), hmt(GQA_KV_HEADS)],
        out_shape=[jax.ShapeDtypeStruct((b, GQA_HEADS, GQA_DIM, t), BF16),
                   jax.ShapeDtypeStruct((b, GQA_KV_HEADS, t, GQA_DIM), BF16),
                   jax.ShapeDtypeStruct((b, GQA_KV_HEADS, GQA_DIM, t), BF16)],
        compiler_params=_params("parallel", "parallel"),
        name="gqa_prep",
    )(p3, p3, p3, qn, kn, *tabs)


ATTN_KC = 256
ATTN_QB = 128
LOG2E = math.log2(math.e)


def _attn_cols(qt, k_ref, vt_ref, s_sc, p_sc, nkeys):
    nqb = qt.shape[1] // ATTN_QB
    st = jnp.dot(k_ref[0, 0, 0:nkeys, :], qt, preferred_element_type=F32)
    for b in range(nqb):
        s_sc[b, 0:nkeys, :] = st[:, b * ATTN_QB:(b + 1) * ATTN_QB]
    inv_l = []
    for b in range(nqb):
        m8 = jnp.max(s_sc[b, 0:nkeys, :].reshape(nkeys // 8, 8, ATTN_QB), axis=0)
        m = jnp.max(m8, axis=0, keepdims=True)
        l8 = jnp.zeros((8, ATTN_QB), F32)
        for k0 in range(0, nkeys, ATTN_KC):
            p = jnp.exp2(s_sc[b, k0:k0 + ATTN_KC, :] - m)
            l8 = l8 + jnp.sum(p.reshape(ATTN_KC // 8, 8, ATTN_QB), axis=0)
            p_sc[b, k0:k0 + ATTN_KC, :] = p.astype(p_sc.dtype)
        inv_l.append(1.0 / jnp.sum(l8, axis=0, keepdims=True))
    pt = jnp.concatenate([p_sc[b, 0:nkeys, :] for b in range(nqb)], axis=1)
    ot = jnp.dot(vt_ref[0, 0, :, 0:nkeys], pt, preferred_element_type=F32)
    return ot * jnp.concatenate(inv_l, axis=1)


def _attn_kernel(qt_ref, k_ref, vt_ref, o_ref, s_sc, p_sc, *, group, tq, ctx_len, ctx_q_tiles, q_off):
    dv = vt_ref.shape[2]
    t = k_ref.shape[2]
    qt = jnp.concatenate([qt_ref[0, g] for g in range(group)], axis=1)

    def run(nkeys):
        ot = _attn_cols(qt, k_ref, vt_ref, s_sc, p_sc, nkeys)
        for g in range(group):
            o_ref[0, :, g * dv:(g + 1) * dv] = ot[:, g * tq:(g + 1) * tq].T.astype(o_ref.dtype)

    if ctx_q_tiles > q_off:
        is_ctx = pl.program_id(2) + q_off < ctx_q_tiles
        pl.when(is_ctx)(lambda: run(ctx_len))
        pl.when(jnp.logical_not(is_ctx))(lambda: run(t))
    else:
        run(t)


def _attention(qt, k, vt, ctx_len, tq, skip_ctx_queries, out_dtype):
    b, hq, dk, t = qt.shape
    hkv, dv = k.shape[1], vt.shape[2]
    group = hq // hkv
    tq = _tile(ctx_len, tq, 128)
    ctx_q_tiles = ctx_len // tq
    q_off = ctx_q_tiles if skip_ctx_queries else 0
    nq = t // tq - q_off
    kern = functools.partial(_attn_kernel, group=group, tq=tq, ctx_len=ctx_len,
                             ctx_q_tiles=ctx_q_tiles, q_off=q_off)
    rows = group * tq
    assert rows % ATTN_QB == 0 and t % ATTN_KC == 0 and ctx_len % ATTN_KC == 0
    return pl.pallas_call(
        kern,
        grid=(b, hkv, nq),
        in_specs=[pl.BlockSpec((1, group, dk, tq), lambda bi, h, i: (bi, h, 0, i + q_off)),
                  pl.BlockSpec((1, 1, t, dk), lambda bi, h, i: (bi, h, 0, 0)),
                  pl.BlockSpec((1, 1, dv, t), lambda bi, h, i: (bi, h, 0, 0))],
        out_specs=pl.BlockSpec((1, tq, group * dv), lambda bi, h, i: (bi, i, h)),
        out_shape=jax.ShapeDtypeStruct((b, nq * tq, hq * dv), out_dtype),
        scratch_shapes=[pltpu.VMEM((rows // ATTN_QB, t, ATTN_QB), F32),
                        pltpu.VMEM((rows // ATTN_QB, t, ATTN_QB), BF16)],
        compiler_params=_params("parallel", "parallel", "arbitrary"),
        name="attention_dk%d" % dk,
    )(qt, k, vt)


def _conv_kernel(x_ref, w_ref, b_ref, o_ref, pad_sc, *, ctx_len, rows):
    t = x_ref.shape[1]
    nch = x_ref.shape[2]
    halo = 8
    segs = ((0, ctx_len), (ctx_len, t))
    zeros = jnp.zeros((halo, nch), F32)
    for si, (lo, hi) in enumerate(segs):
        pad_sc[lo + si * halo:lo + (si + 1) * halo, :] = zeros
        for r0 in range(lo, hi, rows):
            pad_sc[r0 + (si + 1) * halo:r0 + (si + 1) * halo + rows, :] = x_ref[0, r0:r0 + rows, :].astype(F32)
    pad_sc[t + 2 * halo:t + 3 * halo, :] = zeros
    w = w_ref[...]
    bias = b_ref[...]
    for si, (lo, hi) in enumerate(segs):
        for r0 in range(lo, hi, rows):
            base = r0 + (si + 1) * halo - SSD_CONV // 2
            acc = bias + w[0:1, :] * pad_sc[base:base + rows, :]
            for kk in range(1, SSD_CONV):
                acc = acc + w[kk:kk + 1, :] * pad_sc[base + kk:base + kk + rows, :]
            o_ref[0, r0:r0 + rows, :] = _silu(acc).astype(o_ref.dtype)


def _conv(p3, conv_w, conv_b, ctx_len, out_dtype):
    b, t, _ = p3.shape
    nch = 256
    rows = _tile(ctx_len, 256)
    blk0 = P_OFF['xbc'] // nch
    kern = functools.partial(_conv_kernel, ctx_len=ctx_len, rows=rows)
    return pl.pallas_call(
        kern,
        grid=(b, SSD_CONV_DIM // nch),
        in_specs=[pl.BlockSpec((1, t, nch), lambda bi, j: (bi, 0, blk0 + j)),
                  pl.BlockSpec((SSD_CONV, nch), lambda bi, j: (0, j)),
                  pl.BlockSpec((1, nch), lambda bi, j: (0, j))],
        out_specs=pl.BlockSpec((1, t, nch), lambda bi, j: (bi, 0, j)),
        out_shape=jax.ShapeDtypeStruct((b, t, SSD_CONV_DIM), out_dtype),
        scratch_shapes=[pltpu.VMEM((t + 24, nch), F32)],
        compiler_params=_params("parallel", "parallel"),
        name="ssd_conv",
    )(p3, conv_w, conv_b.reshape(1, SSD_CONV_DIM))


def _split3(v):
    hi = v.astype(BF16)
    r1 = v - hi.astype(F32)
    mid = r1.astype(BF16)
    lo = (r1 - mid.astype(F32)).astype(BF16)
    return hi, mid, lo


def _expand_heads(col):
    q = col.shape[0]
    first = lax.broadcasted_iota(jnp.int32, (q, 2 * SSD_P), 1) < SSD_P
    tiles = []
    for e in range(0, SSD_E, 2):
        tiles.append(jnp.where(first, jnp.broadcast_to(col[:, e:e + 1], (q, 2 * SSD_P)),
                               jnp.broadcast_to(col[:, e + 1:e + 2], (q, 2 * SSD_P))))
    return jnp.concatenate(tiles, axis=1)


def _ssd_direction(xs, bm, cm, dt_col_raw, dt_row_raw, bias_col, bias_row, a_col, a_row, h_ref,
                   backward):
    q = xs.shape[0]
    ri = lax.broadcasted_iota(jnp.int32, (q, q), 0)
    ci = lax.broadcasted_iota(jnp.int32, (q, q), 1)
    low = ri >= ci
    keep = (ri <= ci) if backward else low
    tri_col = jnp.where(keep, 1.0, 0.0).astype(BF16)
    tri_row = jnp.where((ri >= ci) if backward else (ri <= ci), 1.0, 0.0).astype(BF16)

    dt_c = _softplus(dt_col_raw + bias_col)
    dt_r = _softplus(dt_row_raw + bias_row)
    cum_c = sum(jnp.dot(tri_col, part, preferred_element_type=F32) for part in _split3(dt_c * a_col))
    cum_r = sum(jnp.dot(part, tri_row, preferred_element_type=F32) for part in _split3(dt_r * a_row))
    cum_full = _expand_heads(cum_c)
    dt_full = _expand_heads(dt_c)
    total = cum_full[0:1, :] if backward else cum_full[q - 1:q, :]

    xdt = xs * dt_full
    cmb = cm.astype(BF16)
    cb = lax.dot_general(cmb, bm.astype(BF16), (((1,), (1,)), ((), ())), preferred_element_type=F32)
    h = h_ref[...]
    y_off = jnp.dot(cmb, h.astype(BF16), preferred_element_type=F32) * jnp.exp(cum_full)
    wgt = (xdt * jnp.exp(total - cum_full)).astype(BF16)
    h_ref[...] = jnp.exp(total) * h + jnp.dot(bm.T.astype(BF16), wgt, preferred_element_type=F32)

    xdt_b = xdt.astype(BF16)
    lane = lax.broadcasted_iota(jnp.int32, (q, 2 * SSD_P), 1)
    parts = []
    for pair in range(SSD_E // 2):
        x_pair = xdt_b[:, pair * 2 * SSD_P:(pair + 1) * 2 * SSD_P]
        ys = []
        for e in (2 * pair, 2 * pair + 1):
            seg = cum_c[:, e:e + 1] - cum_r[e:e + 1, :]
            dec = jnp.exp(jnp.where(keep, seg, -jnp.inf))
            ys.append(jnp.dot((cb * dec).astype(BF16), x_pair, preferred_element_type=F32))
        parts.append(jnp.where(lane < SSD_P, ys[0], ys[1]))
    return jnp.concatenate(parts, axis=1) + y_off


def _ssd_kernel(xf_ref, bf_ref, cf_ref, xb_ref, bb_ref, cb_ref, dcf_ref, drf_ref, dcb_ref, drb_ref,
                bias_c_ref, bias_r_ref, a_c_ref, a_r_ref, d_ref, yf_ref, yb_ref, hf_sc, hb_sc):
    @pl.when(pl.program_id(2) == 0)
    def _():
        hf_sc[...] = jnp.zeros(hf_sc.shape, F32)
        hb_sc[...] = jnp.zeros(hb_sc.shape, F32)

    nsub = xf_ref.shape[1] // SSD_Q
    for i in range(nsub):
        rf = slice(i * SSD_Q, (i + 1) * SSD_Q)
        rb = slice((nsub - 1 - i) * SSD_Q, (nsub - i) * SSD_Q)
        xs = xf_ref[0, rf, :].astype(F32)
        yf = _ssd_direction(xs, bf_ref[0, rf, :].astype(F32), cf_ref[0, rf, :].astype(F32),
                            dcf_ref[0, 0, 0, rf, :], drf_ref[0, 0, 0, :, rf],
                            bias_c_ref[0, 0], bias_r_ref[0, 0], a_c_ref[0, 0], a_r_ref[0, 0], hf_sc, False)
        yf_ref[0, rf, :] = (yf + d_ref[0] * xs).astype(yf_ref.dtype)
        yb = _ssd_direction(xb_ref[0, rb, :].astype(F32), bb_ref[0, rb, :].astype(F32),
                            cb_ref[0, rb, :].astype(F32), dcb_ref[0, 0, 0, rb, :], drb_ref[0, 0, 0, :, rb],
                            bias_c_ref[1, 0], bias_r_ref[1, 0], a_c_ref[1, 0], a_r_ref[1, 0], hb_sc, True)
        yb_ref[0, rb, :] = yb.astype(yb_ref.dtype)


def _ssd(xc, dt_col, dt_row, bias_c, bias_r, a_c, a_r, d_full, ctx_len, out_dtype):
    b, t, _ = xc.shape
    rows = _tile(ctx_len, SSD_ROWS, SSD_Q)
    nblk = t // rows
    nctx = ctx_len // rows
    bblk = SSD_INNER // SSD_N
    cblk = bblk + SSD_G

    def bidx(c):
        return jnp.where(c < nctx, nctx - 1 - c, nblk - 1 - (c - nctx))

    fx = lambda bi, g, c: (bi, c, g)
    fb = lambda bi, g, c: (bi, c, bblk + g)
    fc = lambda bi, g, c: (bi, c, cblk + g)
    bx = lambda bi, g, c: (bi, bidx(c), g)
    bb = lambda bi, g, c: (bi, bidx(c), bblk + g)
    bc = lambda bi, g, c: (bi, bidx(c), cblk + g)
    small = lambda a: pl.BlockSpec((2, 1) + a.shape[2:], lambda bi, g, c: (0, g, 0, 0))
    return pl.pallas_call(
        _ssd_kernel,
        grid=(b, SSD_G, nblk),
        in_specs=[pl.BlockSpec((1, rows, SSD_GW), fx), pl.BlockSpec((1, rows, SSD_N), fb),
                  pl.BlockSpec((1, rows, SSD_N), fc),
                  pl.BlockSpec((1, rows, SSD_GW), bx), pl.BlockSpec((1, rows, SSD_N), bb),
                  pl.BlockSpec((1, rows, SSD_N), bc),
                  pl.BlockSpec((1, 1, 1, rows, SSD_E), lambda bi, g, c: (bi, 0, g, c, 0)),
                  pl.BlockSpec((1, 1, 1, SSD_E, rows), lambda bi, g, c: (bi, 0, g, 0, c)),
                  pl.BlockSpec((1, 1, 1, rows, SSD_E), lambda bi, g, c: (bi, 1, g, bidx(c), 0)),
                  pl.BlockSpec((1, 1, 1, SSD_E, rows), lambda bi, g, c: (bi, 1, g, 0, bidx(c))),
                  small(bias_c), small(bias_r), small(a_c), small(a_r),
                  pl.BlockSpec((1, 1, SSD_GW), lambda bi, g, c: (g, 0, 0))],
        out_specs=[pl.BlockSpec((1, rows, SSD_GW), fx), pl.BlockSpec((1, rows, SSD_GW), bx)],
        out_shape=[jax.ShapeDtypeStruct((b, t, SSD_INNER), out_dtype)] * 2,
        scratch_shapes=[pltpu.VMEM((SSD_N, SSD_GW), F32), pltpu.VMEM((SSD_N, SSD_GW), F32)],
        compiler_params=_params("parallel", "parallel", "arbitrary"),
        name="ssd_scan",
    )(xc, xc, xc, xc, xc, xc, dt_col, dt_row, dt_col, dt_row, bias_c, bias_r, a_c, a_r, d_full)


MERGE_TN = 512


def _merge_kernel(ya_ref, ga_ref, yb_ref, gb_ref, yf_ref, ybk_ref, z_ref, nrm_ref,
                  mg_ref, wa_ref, wb_ref, wc_ref, u_ref, c_sc):
    a_in = (ya_ref[0].astype(F32) * _silu(ga_ref[0].astype(F32))).astype(BF16)
    b_in = (yb_ref[0].astype(F32) * _silu(gb_ref[0].astype(F32))).astype(BF16)
    v = (yf_ref[0].astype(F32) + ybk_ref[0].astype(F32)) * _silu(z_ref[0].astype(F32))
    for g in range(SSD_G):
        sl = slice(g * SSD_GW, (g + 1) * SSD_GW)
        c_sc[:, sl] = (_rms(v[:, sl]) * nrm_ref[:, sl]).astype(BF16)
    c_in = c_sc[...]
    d = u_ref.shape[-1]
    for j in range(0, d, MERGE_TN):
        sl = slice(j, j + MERGE_TN)
        br_a = jnp.dot(a_in, wa_ref[:, sl], preferred_element_type=F32)
        br_b = jnp.dot(b_in, wb_ref[:, sl], preferred_element_type=F32)
        br_c = jnp.dot(c_in, wc_ref[:, sl], preferred_element_type=F32)
        u = (jax.nn.sigmoid(mg_ref[0, :, j:j + MERGE_TN].astype(F32)) * br_a
             + jax.nn.sigmoid(mg_ref[0, :, d + j:d + j + MERGE_TN].astype(F32)) * br_b
             + jax.nn.sigmoid(mg_ref[0, :, 2 * d + j:2 * d + j + MERGE_TN].astype(F32)) * br_c)
        u_ref[0, :, sl] = u.astype(u_ref.dtype)


def _merge(ya, yb, yf, ybk, p3, ssd_norm, wa, wb, wc, row_off):
    b, rows, _ = ya.shape
    tm = _tile(rows, 256)
    d = D_MODEL
    ro = row_off // tm
    assert P_OFF['mg'] == 0

    def col(name, width):
        blk = P_OFF[name] // width
        return pl.BlockSpec((1, tm, width), lambda bi, i: (bi, i + ro, blk))

    loc = lambda w: pl.BlockSpec((1, tm, w), lambda bi, i: (bi, i, 0))
    glob = lambda w: pl.BlockSpec((1, tm, w), lambda bi, i: (bi, i + ro, 0))
    wspec = lambda k: pl.BlockSpec((k, d), lambda bi, i: (0, 0), pipeline_mode=pl.Buffered(1))
    return pl.pallas_call(
        _merge_kernel,
        grid=(b, rows // tm),
        in_specs=[loc(MLA_WIDTH), col('ga', MLA_WIDTH), loc(GQA_WIDTH), col('gb', GQA_WIDTH),
                  glob(SSD_INNER), glob(SSD_INNER), col('z', SSD_INNER),
                  pl.BlockSpec((1, SSD_INNER), lambda bi, i: (0, 0)),
                  glob(N_BRANCH * d), wspec(MLA_WIDTH), wspec(GQA_WIDTH), wspec(SSD_INNER)],
        out_specs=pl.BlockSpec((1, tm, d), lambda bi, i: (bi, i, 0)),
        out_shape=jax.ShapeDtypeStruct((b, rows, d), BF16),
        scratch_shapes=[pltpu.VMEM((tm, SSD_INNER), BF16)],
        compiler_params=_params("parallel", "parallel"),
        name="merge",
    )(ya, p3, yb, p3, yf, ybk, p3, ssd_norm.reshape(1, SSD_INNER), p3, wa, wb, wc)


def _out_ln_kernel(u_ref, w_ref, x_ref, gate_ref, g_ref, b_ref, *rest, with_next):
    out = jnp.dot(u_ref[0], w_ref[...], preferred_element_type=F32)
    r = DEEPNORM_ALPHA * x_ref[0] + gate_ref[0] * out
    xn = _layer_norm(r) * g_ref[...] + b_ref[...]
    if with_next:
        sh_ref, sc_ref, xo_ref, xm_ref = rest
        xo_ref[0] = xn
        xm_ref[0] = (_layer_norm(xn) * (1.0 + sc_ref[0]) + sh_ref[0]).astype(xm_ref.dtype)
    else:
        (xo_ref,) = rest
        xo_ref[0] = xn


def _out_ln(u, w_out, xc, tab, ln_g, ln_b, next_tab, nb, ctx_len, row_off):
    b, rows, d = u.shape
    tm = _tile(ctx_len, 256)
    ro = row_off // tm
    row = _mod_row_index(nb, ctx_len // tm)
    with_next = next_tab is not None
    loc = pl.BlockSpec((1, tm, d), lambda bi, i: (bi, i, 0))
    vec = pl.BlockSpec((1, d), lambda bi, i: (0, 0))
    in_specs = [loc, pl.BlockSpec((d, d), lambda bi, i: (0, 0)),
                pl.BlockSpec((1, tm, d), lambda bi, i: (bi, i + ro, 0)),
                pl.BlockSpec((1, 1, d), lambda bi, i: (row(bi, i + ro), 0, 2)), vec, vec]
    args = [u, w_out, xc, tab, ln_g.reshape(1, d), ln_b.reshape(1, d)]
    out_specs = [loc]
    out_shape = [jax.ShapeDtypeStruct((b, rows, d), F32)]
    if with_next:
        in_specs += [pl.BlockSpec((1, 1, d), lambda bi, i: (row(bi, i + ro), 0, 0)),
                     pl.BlockSpec((1, 1, d), lambda bi, i: (row(bi, i + ro), 0, 1))]
        args += [next_tab, next_tab]
        out_specs.append(loc)
        out_shape.append(jax.ShapeDtypeStruct((b, rows, d), BF16))
    return pl.pallas_call(
        functools.partial(_out_ln_kernel, with_next=with_next),
        grid=(b, rows // tm),
        in_specs=in_specs,
        out_specs=out_specs,
        out_shape=out_shape,
        compiler_params=_params("parallel", "parallel"),
        name="out_ln",
    )(*args)


def _rope_angles(rows, dim):
    row, col = jnp.meshgrid(jnp.arange(rows, dtype=F32), jnp.arange(GRID_W, dtype=F32), indexing='ij')
    half = dim // 2
    inv_freq = ROPE_THETA ** (-jnp.arange(0, half, 2, dtype=F32) / half)
    ang_r = row.reshape(-1, 1) * inv_freq
    ang_c = col.reshape(-1, 1) * inv_freq
    return jnp.concatenate([ang_r, ang_r, ang_c, ang_c], axis=-1)


def _rope_tables(seq, ctx_len, dim):
    ang = _rope_angles(seq // GRID_W, dim)
    cos = jnp.concatenate([jnp.ones((ctx_len, dim), F32), jnp.cos(ang)], axis=0)
    sin = jnp.concatenate([jnp.zeros((ctx_len, dim), F32), jnp.sin(ang)], axis=0)
    return cos, sin


def _roll_tables(cos, sin, dim, scale):
    t = cos.shape[0]
    quarter = dim // 4
    first = (jnp.arange(dim) % (2 * quarter)) < quarter
    s1 = jnp.where(first, -sin, 0.0)
    s2 = jnp.where(first, 0.0, sin)
    pad = lambda a: jnp.pad(a * scale, ((0, 0), (0, 128 - dim)))
    return pad(cos), pad(s1), pad(s2)


def _rot_matrix(dim):
    quarter = dim // 4
    r = np.zeros((dim, dim), np.float32)
    for i in range(dim):
        blk = i // quarter
        if blk % 2 == 0:
            r[i + quarter, i] = -1.0
        else:
            r[i - quarter, i] = 1.0
    return jnp.asarray(r)


def _mla_q_weights(w_uq):
    w = w_uq.reshape(MLA_Q_LORA, MLA_HEADS, MLA_QK)
    nope, pe = w[..., :MLA_NOPE], w[..., MLA_NOPE:]
    pe_rot = jnp.einsum('khd,de->khe', pe, _rot_matrix(MLA_ROPE), precision=HIGHEST)
    zpad = jnp.zeros((MLA_Q_LORA, MLA_HEADS, MLA_QK_PAD - MLA_QK), F32)
    main = jnp.concatenate([nope, pe, zpad], axis=-1)
    rot = jnp.concatenate([jnp.zeros_like(nope), pe_rot, zpad], axis=-1)
    return (main.reshape(MLA_Q_LORA, -1).astype(BF16), rot.reshape(MLA_Q_LORA, -1).astype(BF16))


def _permute_w_in(w):
    wb = w.astype(BF16)
    parts = [wb[:, IN_OFFSETS[n]:IN_OFFSETS[n] + IN_WIDTHS[n]] for n in P_ORDER]
    parts.append(jnp.zeros((w.shape[0], P_WIDTH - P_USED), BF16))
    return jnp.concatenate(parts, axis=1)


P_DTYPE = BF16
Y_DTYPE = BF16


def kernel(x, c, ctx, c_ctx, w_mod, b_mod, w_in, mla_q_norm, mla_w_uq, mla_kv_norm, mla_w_ukv,
           gqa_q_norm, gqa_k_norm, ssd_conv_w, ssd_conv_b, ssd_a_log, ssd_dt_bias, ssd_d, ssd_norm,
           w_br_a, w_br_b, w_br_c, w_out, ln_g, ln_b):
    nb, seq, d = x.shape
    ctx_len = ctx.shape[1]
    t = ctx_len + seq
    depth = w_in.shape[0]
    assert d == D_MODEL and nb < 8 and seq % GRID_W == 0
    assert ctx_len % SSD_Q == 0 and seq % SSD_Q == 0

    cos_a, sin_a = _rope_tables(seq, ctx_len, MLA_ROPE)
    cos_b, sin_b = _rope_tables(seq, ctx_len, GQA_DIM)
    sq = MLA_QK ** -0.5 * LOG2E
    zq = jnp.zeros((t, MLA_QK_PAD - MLA_QK), F32)
    cosq = jnp.concatenate([jnp.full((t, MLA_NOPE), sq, F32), cos_a * sq, zq], axis=1)
    sinq = jnp.concatenate([jnp.zeros((t, MLA_NOPE), F32), sin_a * sq, zq], axis=1)
    mla_tabs = (cosq, sinq) + _roll_tables(cos_a, sin_a, MLA_ROPE, 1.0)
    gqa_tabs = (_roll_tables(cos_b, sin_b, GQA_DIM, GQA_DIM ** -0.5 * LOG2E)
                + _roll_tables(cos_b, sin_b, GQA_DIM, 1.0))

    c_rows = jnp.zeros((8, d), F32).at[:nb].set(c).at[nb].set(c_ctx)
    tabs = [_mod_rows(c_rows, w_mod[l], b_mod[l]).reshape(8, 1, 3 * d) for l in range(depth)]

    xc = jnp.concatenate([ctx, x], axis=1)
    xm = _ln_mod(xc, tabs[0], nb, ctx_len)

    for l in range(depth):
        last = l == depth - 1
        wp = _permute_w_in(w_in[l])
        p3 = _in_proj(xm.reshape(nb * t, d), wp, P_DTYPE).reshape(nb, t, P_WIDTH)

        wqm, wqr = _mla_q_weights(mla_w_uq[l])
        qa, ka, va = _mla_prep(p3, mla_q_norm[l].reshape(1, -1), mla_kv_norm[l].reshape(1, -1),
                               wqm, wqr, mla_w_ukv[l].astype(BF16), mla_tabs, ctx_len)
        ya = _attention(qa, ka, va, ctx_len, 256, last, Y_DTYPE)
        qb, kb, vb = _gqa_prep(p3, gqa_q_norm[l].reshape(1, -1), gqa_k_norm[l].reshape(1, -1),
                               gqa_tabs, ctx_len)
        yb = _attention(qb, kb, vb, ctx_len, 128, last, Y_DTYPE)

        xconv = _conv(p3, ssd_conv_w[l], ssd_conv_b[l], ctx_len, Y_DTYPE)
        dtr = p3[:, :, P_OFF['dtr']:P_OFF['dtr'] + 2 * SSD_HEADS].astype(F32)
        dt5 = dtr.reshape(nb, t, 2, SSD_G, SSD_E)
        dt_col = jnp.transpose(dt5, (0, 2, 3, 1, 4))
        dt_row = jnp.transpose(dt5, (0, 2, 3, 4, 1))
        bias = ssd_dt_bias[l].astype(F32).reshape(2, SSD_G, SSD_E)
        a = -jnp.exp(ssd_a_log[l].astype(F32)).reshape(2, SSD_G, SSD_E)
        d_full = jnp.repeat(ssd_d[l].astype(F32), SSD_P).reshape(SSD_G, 1, SSD_GW)
        yf, ybk = _ssd(xconv, dt_col, dt_row, bias[:, :, None, :], bias[:, :, :, None],
                       a[:, :, None, :], a[:, :, :, None], d_full, ctx_len, Y_DTYPE)

        row_off = ctx_len if last else 0
        u = _merge(ya, yb, yf, ybk, p3, ssd_norm[l], w_br_a[l].astype(BF16), w_br_b[l].astype(BF16),
                   w_br_c[l].astype(BF16), row_off)
        if last:
            (xo,) = _out_ln(u, w_out[l].astype(BF16), xc, tabs[l], ln_g[l], ln_b[l], None, nb,
                            ctx_len, row_off)
            return xo
        xc, xm = _out_ln(u, w_out[l].astype(BF16), xc, tabs[l], ln_g[l], ln_b[l], tabs[l + 1], nb,
                         ctx_len, row_off)
```

```python
import functools
import math

import numpy as np
import jax
import jax.numpy as jnp
from jax import lax
from jax.experimental import pallas as pl
from jax.experimental.pallas import tpu as pltpu

F32 = jnp.float32
BF16 = jnp.bfloat16
HIGHEST = lax.Precision.HIGHEST

D_MODEL = 2048
DEPTH = 2
GRID_W = 64
ROPE_THETA = 10000.0
EPS = 1e-6

MLA_HEADS = 8
MLA_Q_LORA = 512
MLA_KV_LORA = 256
MLA_NOPE = 128
MLA_ROPE = 64
MLA_V = 128
MLA_QK = MLA_NOPE + MLA_ROPE
MLA_QK_PAD = 256
MLA_WIDTH = MLA_HEADS * MLA_V

GQA_HEADS = 8
GQA_KV_HEADS = 2
GQA_GROUP = GQA_HEADS // GQA_KV_HEADS
GQA_DIM = 128
GQA_WIDTH = GQA_HEADS * GQA_DIM
GQA_KV_WIDTH = GQA_KV_HEADS * GQA_DIM

SSD_INNER = D_MODEL
SSD_P = 64
SSD_HEADS = SSD_INNER // SSD_P
SSD_G = 4
SSD_E = SSD_HEADS // SSD_G
SSD_N = 128
SSD_CONV = 5
SSD_Q = 128
SSD_ROWS = 256
SSD_GW = SSD_E * SSD_P
SSD_CONV_DIM = SSD_INNER + 2 * SSD_G * SSD_N

N_BRANCH = 3
IN_SPLITS = (MLA_Q_LORA, MLA_KV_LORA, MLA_ROPE, MLA_WIDTH, GQA_WIDTH, GQA_KV_WIDTH, GQA_KV_WIDTH,
             GQA_WIDTH, SSD_INNER, SSD_CONV_DIM, 2 * SSD_HEADS, N_BRANCH * D_MODEL)
IN_NAMES = ('cq', 'ckv', 'kr', 'ga', 'gq', 'gk', 'gv', 'gb', 'z', 'xbc', 'dtr', 'mg')
IN_OFFSETS = dict(zip(IN_NAMES, np.concatenate([[0], np.cumsum(IN_SPLITS)[:-1]]).tolist()))
IN_WIDTHS = dict(zip(IN_NAMES, IN_SPLITS))
P_ORDER = ('mg', 'z', 'xbc', 'ga', 'gq', 'gb', 'cq', 'ckv', 'gk', 'gv', 'kr', 'dtr')
P_OFF = {}
_o = 0
for _n in P_ORDER:
    P_OFF[_n] = _o
    _o += IN_WIDTHS[_n]
P_USED = _o
P_TN = 512
P_WIDTH = -(-P_USED // P_TN) * P_TN

DEEPNORM_ALPHA = (2 * DEPTH) ** 0.25

VMEM_LIMIT = 56 * 2 ** 20


def _params(*sem):
    return pltpu.CompilerParams(dimension_semantics=sem, vmem_limit_bytes=VMEM_LIMIT)


def _tile(n, target, align=8):
    t = min(n, target)
    while t > align and (n % t or t % align):
        t -= align
    assert n % t == 0, (n, target)
    return t


def _silu(v):
    return v * jax.nn.sigmoid(v)


def _softplus(v):
    return jnp.maximum(v, 0.0) + jnp.log1p(jnp.exp(-jnp.abs(v)))


def _layer_norm(v):
    mu = jnp.mean(v, axis=-1, keepdims=True)
    vc = v - mu
    var = jnp.mean(vc * vc, axis=-1, keepdims=True)
    return vc * lax.rsqrt(var + EPS)


def _rms(v):
    return v * lax.rsqrt(jnp.mean(v * v, axis=-1, keepdims=True) + EPS)


def _mod_kernel(c_ref, w_ref, b_ref, o_ref):
    a = _silu(c_ref[...]).astype(BF16)
    o_ref[...] = jnp.dot(a, w_ref[...].astype(BF16), preferred_element_type=F32) + b_ref[...]


def _mod_rows(c_rows, w_mod, b_mod):
    r, d = c_rows.shape
    n = w_mod.shape[1]
    tn = 512
    return pl.pallas_call(
        _mod_kernel,
        grid=(n // tn,),
        in_specs=[pl.BlockSpec((r, d), lambda j: (0, 0)),
                  pl.BlockSpec((d, tn), lambda j: (0, j)),
                  pl.BlockSpec((1, tn), lambda j: (0, j))],
        out_specs=pl.BlockSpec((r, tn), lambda j: (0, j)),
        out_shape=jax.ShapeDtypeStruct((r, n), F32),
        compiler_params=_params("arbitrary"),
        name="mod_rows",
    )(c_rows, w_mod, b_mod.reshape(1, n))


def _ln_mod_kernel(x_ref, sh_ref, sc_ref, o_ref):
    y = _layer_norm(x_ref[0])
    o_ref[0] = (y * (1.0 + sc_ref[0]) + sh_ref[0]).astype(o_ref.dtype)


def _mod_row_index(nb, ctx_tiles):
    return lambda b, i: jnp.where(i < ctx_tiles, nb, b)


def _ln_mod(xc, tab, nb, ctx_len):
    b, t, d = xc.shape
    tm = _tile(ctx_len, 256)
    row = _mod_row_index(nb, ctx_len // tm)
    return pl.pallas_call(
        _ln_mod_kernel,
        grid=(b, t // tm),
        in_specs=[pl.BlockSpec((1, tm, d), lambda bi, i: (bi, i, 0)),
                  pl.BlockSpec((1, 1, d), lambda bi, i: (row(bi, i), 0, 0)),
                  pl.BlockSpec((1, 1, d), lambda bi, i: (row(bi, i), 0, 1))],
        out_specs=pl.BlockSpec((1, tm, d), lambda bi, i: (bi, i, 0)),
        out_shape=jax.ShapeDtypeStruct((b, t, d), BF16),
        compiler_params=_params("parallel", "parallel"),
        name="ln_mod",
    )(xc, tab, tab)


def _matmul_kernel(x_ref, w_ref, o_ref):
    o_ref[...] = jnp.dot(x_ref[...], w_ref[...], preferred_element_type=F32).astype(o_ref.dtype)


def _in_proj(xm2, wp, out_dtype):
    m, k = xm2.shape
    n = wp.shape[1]
    tm = _tile(m, 1024)
    tn = P_TN
    return pl.pallas_call(
        _matmul_kernel,
        grid=(m // tm, n // tn),
        in_specs=[pl.BlockSpec((tm, k), lambda i, j: (i, 0)),
                  pl.BlockSpec((k, tn), lambda i, j: (0, j))],
        out_specs=pl.BlockSpec((tm, tn), lambda i, j: (i, j)),
        out_shape=jax.ShapeDtypeStruct((m, n), out_dtype),
        compiler_params=_params("parallel", "arbitrary"),
        name="in_proj",
    )(xm2, wp)


def _mla_prep_kernel(cq_ref, ckv_ref, kr_ref, qn_ref, kvn_ref, wqm_ref, wqr_ref, wkv_ref,
                     cosq_ref, sinq_ref, ck_ref, s1_ref, s2_ref, qa_ref, ka_ref, va_ref):
    cqn = (_rms(cq_ref[0].astype(F32)) * qn_ref[...]).astype(BF16)
    qm = jnp.dot(cqn, wqm_ref[...], preferred_element_type=F32)
    qr = jnp.dot(cqn, wqr_ref[...], preferred_element_type=F32)
    cosq = cosq_ref[...]
    sinq = sinq_ref[...]
    for h in range(MLA_HEADS):
        sl = slice(h * MLA_QK_PAD, (h + 1) * MLA_QK_PAD)
        qa_ref[0, h] = (qm[:, sl] * cosq + qr[:, sl] * sinq).T.astype(qa_ref.dtype)
    ckvn = (_rms(ckv_ref[0].astype(F32)) * kvn_ref[...]).astype(BF16)
    kv = jnp.dot(ckvn, wkv_ref[...], preferred_element_type=F32)
    kr = kr_ref[0].astype(F32)
    kpe = (kr * ck_ref[...] + pltpu.roll(kr, 128 - 16, axis=1) * s1_ref[...]
           + pltpu.roll(kr, 16, axis=1) * s2_ref[...]).astype(ka_ref.dtype)
    for h in range(MLA_HEADS):
        base = h * (MLA_NOPE + MLA_V)
        ka_ref[0, h, :, 0:MLA_NOPE] = kv[:, base:base + MLA_NOPE].astype(ka_ref.dtype)
        ka_ref[0, h, :, MLA_NOPE:MLA_QK_PAD] = kpe
        va_ref[0, h] = kv[:, base + MLA_NOPE:base + MLA_NOPE + MLA_V].T.astype(va_ref.dtype)


def _mla_prep(p3, qn, kvn, wqm, wqr, wkv, tabs, ctx_len):
    b, t, _ = p3.shape
    tm = _tile(ctx_len, 256)
    cosq, sinq, ck, s1, s2 = tabs

    def col(name, width):
        blk = P_OFF[name] // width
        return pl.BlockSpec((1, tm, width), lambda bi, i: (bi, i, blk))

    def full(a):
        return pl.BlockSpec(a.shape, lambda bi, i: (0,) * a.ndim)

    def rows(a):
        return pl.BlockSpec((tm, a.shape[1]), lambda bi, i: (i, 0))

    hm = lambda w: pl.BlockSpec((1, MLA_HEADS, tm, w), lambda bi, i: (bi, 0, i, 0))
    return pl.pallas_call(
        _mla_prep_kernel,
        grid=(b, t // tm),
        in_specs=[col('cq', MLA_Q_LORA), col('ckv', MLA_KV_LORA), col('kr', 128),
                  full(qn), full(kvn), full(wqm), full(wqr), full(wkv),
                  rows(cosq), rows(sinq), rows(ck), rows(s1), rows(s2)],
        out_specs=[pl.BlockSpec((1, MLA_HEADS, MLA_QK_PAD, tm), lambda bi, i: (bi, 0, 0, i)),
                   hm(MLA_QK_PAD),
                   pl.BlockSpec((1, MLA_HEADS, MLA_V, tm), lambda bi, i: (bi, 0, 0, i))],
        out_shape=[jax.ShapeDtypeStruct((b, MLA_HEADS, MLA_QK_PAD, t), BF16),
                   jax.ShapeDtypeStruct((b, MLA_HEADS, t, MLA_QK_PAD), BF16),
                   jax.ShapeDtypeStruct((b, MLA_HEADS, MLA_V, t), BF16)],
        compiler_params=_params("parallel", "parallel"),
        name="mla_prep",
    )(p3, p3, p3, qn, kvn, wqm, wqr, wkv, cosq, sinq, ck, s1, s2)


def _rope128(y, c, s1, s2):
    return y * c + pltpu.roll(y, 128 - 32, axis=1) * s1 + pltpu.roll(y, 32, axis=1) * s2


def _gqa_prep_kernel(gq_ref, gk_ref, gv_ref, qn_ref, kn_ref, cq_ref, s1q_ref, s2q_ref,
                     ck_ref, s1k_ref, s2k_ref, qb_ref, kb_ref, vb_ref):
    gq = gq_ref[0].astype(F32)
    for h in range(GQA_HEADS):
        y = _rms(gq[:, h * GQA_DIM:(h + 1) * GQA_DIM]) * qn_ref[...]
        qb_ref[0, h] = _rope128(y, cq_ref[...], s1q_ref[...], s2q_ref[...]).T.astype(qb_ref.dtype)
    gk = gk_ref[0].astype(F32)
    gv = gv_ref[0].astype(F32)
    for h in range(GQA_KV_HEADS):
        y = _rms(gk[:, h * GQA_DIM:(h + 1) * GQA_DIM]) * kn_ref[...]
        kb_ref[0, h] = _rope128(y, ck_ref[...], s1k_ref[...], s2k_ref[...]).astype(kb_ref.dtype)
        vb_ref[0, h] = gv[:, h * GQA_DIM:(h + 1) * GQA_DIM].T.astype(vb_ref.dtype)


def _gqa_prep(p3, qn, kn, tabs, ctx_len):
    b, t, _ = p3.shape
    tm = _tile(ctx_len, 256)

    def col(name, width):
        blk = P_OFF[name] // width
        return pl.BlockSpec((1, tm, width), lambda bi, i: (bi, i, blk))

    def full(a):
        return pl.BlockSpec(a.shape, lambda bi, i: (0,) * a.ndim)

    def rows(a):
        return pl.BlockSpec((tm, a.shape[1]), lambda bi, i: (i, 0))

    hm = lambda nh: pl.BlockSpec((1, nh, tm, GQA_DIM), lambda bi, i: (bi, 0, i, 0))
    hmt = lambda nh: pl.BlockSpec((1, nh, GQA_DIM, tm), lambda bi, i: (bi, 0, 0, i))
    return pl.pallas_call(
        _gqa_prep_kernel,
        grid=(b, t // tm),
        in_specs=[col('gq', GQA_WIDTH), col('gk', GQA_KV_WIDTH), col('gv', GQA_KV_WIDTH),
                  full(qn), full(kn)] + [rows(a) for a in tabs],
        out_specs=[hmt(GQA_HEADS), hm(GQA_KV_HEADS), hmt(GQA_KV_HEADS)],
        out_shape=[jax.ShapeDtypeStruct((b, GQA_HEADS, GQA_DIM, t), BF16),
                   jax.ShapeDtypeStruct((b, GQA_KV_HEADS, t, GQA_DIM), BF16),
                   jax.ShapeDtypeStruct((b, GQA_KV_HEADS, GQA_DIM, t), BF16)],
        compiler_params=_params("parallel", "parallel"),
        name="gqa_prep",
    )(p3, p3, p3, qn, kn, *tabs)


ATTN_KC = 256
ATTN_QB = 128
LOG2E = math.log2(math.e)


def _attn_cols(qts, k_ref, vt_ref, s_sc, p_sc, l_sc, nkeys):
    nqb = qts[0].shape[1] // ATTN_QB
    for j, qt in enumerate(qts):
        st = jnp.dot(k_ref[0, j, 0:nkeys, :], qt, preferred_element_type=F32)
        for b in range(nqb):
            s_sc[j * nqb + b, 0:nkeys, :] = st[:, b * ATTN_QB:(b + 1) * ATTN_QB]

    def softmax_block(b, carry):
        parts = [jnp.max(s_sc[b, k0:k0 + ATTN_KC, :].reshape(ATTN_KC // 64, 8, 8, ATTN_QB), axis=0)
                 for k0 in range(0, nkeys, ATTN_KC)]
        while len(parts) > 1:
            parts = [jnp.maximum(parts[i], parts[i + 1]) if i + 1 < len(parts) else parts[i]
                     for i in range(0, len(parts), 2)]
        m = jnp.max(parts[0], axis=(0, 1), keepdims=True)[0]
        lacc = jnp.zeros((8, 8, ATTN_QB), F32)
        for k0 in range(0, nkeys, ATTN_KC):
            p = jnp.exp2(s_sc[b, k0:k0 + ATTN_KC, :] - m)
            lacc = lacc + jnp.sum(p.reshape(ATTN_KC // 64, 8, 8, ATTN_QB), axis=0)
            p_sc[b, k0:k0 + ATTN_KC, :] = p.astype(p_sc.dtype)
        l = jnp.sum(lacc, axis=(0, 1), keepdims=True)[0]
        l_sc[b] = jnp.broadcast_to(1.0 / l, (8, ATTN_QB))
        return carry

    lax.fori_loop(0, len(qts) * nqb, softmax_block, 0)
    outs = []
    for j in range(len(qts)):
        slabs = range(j * nqb, (j + 1) * nqb)
        pt = jnp.concatenate([p_sc[b, 0:nkeys, :] for b in slabs], axis=1)
        ot = jnp.dot(vt_ref[0, j, :, 0:nkeys], pt, preferred_element_type=F32)
        outs.append(ot * jnp.concatenate([l_sc[b, 0:1, :] for b in slabs], axis=1))
    return outs


def _attn_kernel(qt_ref, k_ref, vt_ref, o_ref, s_sc, p_sc, l_sc, *, group, tq, ctx_len, ctx_q_tiles, q_off):
    dv = vt_ref.shape[2]
    t = k_ref.shape[2]
    hps = k_ref.shape[1]
    qts = [jnp.concatenate([qt_ref[0, j * group + g] for g in range(group)], axis=1)
           for j in range(hps)]

    def run(nkeys):
        outs = _attn_cols(qts, k_ref, vt_ref, s_sc, p_sc, l_sc, nkeys)
        for j, ot in enumerate(outs):
            for g in range(group):
                c0 = (j * group + g) * dv
                o_ref[0, :, c0:c0 + dv] = ot[:, g * tq:(g + 1) * tq].T.astype(o_ref.dtype)

    if ctx_q_tiles > q_off:
        is_ctx = pl.program_id(2) + q_off < ctx_q_tiles
        pl.when(is_ctx)(lambda: run(ctx_len))
        pl.when(jnp.logical_not(is_ctx))(lambda: run(t))
    else:
        run(t)


def _attention(qt, k, vt, ctx_len, tq, hps, skip_ctx_queries, out_dtype):
    b, hq, dk, t = qt.shape
    hkv, dv = k.shape[1], vt.shape[2]
    group = hq // hkv
    tq = _tile(ctx_len, tq, 128)
    ctx_q_tiles = ctx_len // tq
    q_off = ctx_q_tiles if skip_ctx_queries else 0
    nq = t // tq - q_off
    kern = functools.partial(_attn_kernel, group=group, tq=tq, ctx_len=ctx_len,
                             ctx_q_tiles=ctx_q_tiles, q_off=q_off)
    rows = hps * group * tq
    assert rows % ATTN_QB == 0 and t % ATTN_KC == 0 and ctx_len % ATTN_KC == 0 and hkv % hps == 0
    return pl.pallas_call(
        kern,
        grid=(b, hkv // hps, nq),
        in_specs=[pl.BlockSpec((1, hps * group, dk, tq), lambda bi, h, i: (bi, h, 0, i + q_off)),
                  pl.BlockSpec((1, hps, t, dk), lambda bi, h, i: (bi, h, 0, 0)),
                  pl.BlockSpec((1, hps, dv, t), lambda bi, h, i: (bi, h, 0, 0))],
        out_specs=pl.BlockSpec((1, tq, hps * group * dv), lambda bi, h, i: (bi, i, h)),
        out_shape=jax.ShapeDtypeStruct((b, nq * tq, hq * dv), out_dtype),
        scratch_shapes=[pltpu.VMEM((rows // ATTN_QB, t, ATTN_QB), F32),
                        pltpu.VMEM((rows // ATTN_QB, t, ATTN_QB), BF16),
                        pltpu.VMEM((rows // ATTN_QB, 8, ATTN_QB), F32)],
        compiler_params=_params("parallel", "parallel", "arbitrary"),
        name="attention_dk%d" % dk,
    )(qt, k, vt)


def _conv_kernel(x_ref, w_ref, b_ref, o_ref, pad_sc, *, ctx_len, rows):
    t = x_ref.shape[1]
    nch = x_ref.shape[2]
    halo = 8
    segs = ((0, ctx_len), (ctx_len, t))
    zeros = jnp.zeros((halo, nch), F32)
    for si, (lo, hi) in enumerate(segs):
        pad_sc[lo + si * halo:lo + (si + 1) * halo, :] = zeros
        for r0 in range(lo, hi, rows):
            pad_sc[r0 + (si + 1) * halo:r0 + (si + 1) * halo + rows, :] = x_ref[0, r0:r0 + rows, :].astype(F32)
    pad_sc[t + 2 * halo:t + 3 * halo, :] = zeros
    w = w_ref[...]
    bias = b_ref[...]
    for si, (lo, hi) in enumerate(segs):
        for r0 in range(lo, hi, rows):
            base = r0 + (si + 1) * halo - SSD_CONV // 2
            acc = bias + w[0:1, :] * pad_sc[base:base + rows, :]
            for kk in range(1, SSD_CONV):
                acc = acc + w[kk:kk + 1, :] * pad_sc[base + kk:base + kk + rows, :]
            o_ref[0, r0:r0 + rows, :] = _silu(acc).astype(o_ref.dtype)


def _conv(p3, conv_w, conv_b, ctx_len, out_dtype):
    b, t, _ = p3.shape
    nch = 256
    rows = _tile(ctx_len, 256)
    blk0 = P_OFF['xbc'] // nch
    kern = functools.partial(_conv_kernel, ctx_len=ctx_len, rows=rows)
    return pl.pallas_call(
        kern,
        grid=(b, SSD_CONV_DIM // nch),
        in_specs=[pl.BlockSpec((1, t, nch), lambda bi, j: (bi, 0, blk0 + j)),
                  pl.BlockSpec((SSD_CONV, nch), lambda bi, j: (0, j)),
                  pl.BlockSpec((1, nch), lambda bi, j: (0, j))],
        out_specs=pl.BlockSpec((1, t, nch), lambda bi, j: (bi, 0, j)),
        out_shape=jax.ShapeDtypeStruct((b, t, SSD_CONV_DIM), out_dtype),
        scratch_shapes=[pltpu.VMEM((t + 24, nch), F32)],
        compiler_params=_params("parallel", "parallel"),
        name="ssd_conv",
    )(p3, conv_w, conv_b.reshape(1, SSD_CONV_DIM))


def _split3(v):
    hi = v.astype(BF16)
    r1 = v - hi.astype(F32)
    mid = r1.astype(BF16)
    lo = (r1 - mid.astype(F32)).astype(BF16)
    return hi, mid, lo


def _expand_heads(col):
    q = col.shape[0]
    first = lax.broadcasted_iota(jnp.int32, (q, 2 * SSD_P), 1) < SSD_P
    tiles = []
    for e in range(0, SSD_E, 2):
        tiles.append(jnp.where(first, jnp.broadcast_to(col[:, e:e + 1], (q, 2 * SSD_P)),
                               jnp.broadcast_to(col[:, e + 1:e + 2], (q, 2 * SSD_P))))
    return jnp.concatenate(tiles, axis=1)


def _ssd_direction(xs, bm, cm, dt_col_raw, dt_row_raw, bias_col, bias_row, a_col, a_row, h_ref,
                   backward):
    q = xs.shape[0]
    ri = lax.broadcasted_iota(jnp.int32, (q, q), 0)
    ci = lax.broadcasted_iota(jnp.int32, (q, q), 1)
    low = ri >= ci
    keep = (ri <= ci) if backward else low
    tri_col = jnp.where(keep, 1.0, 0.0).astype(BF16)
    tri_row = jnp.where((ri >= ci) if backward else (ri <= ci), 1.0, 0.0).astype(BF16)

    dt_c = _softplus(dt_col_raw + bias_col)
    dt_r = _softplus(dt_row_raw + bias_row)
    cum_c = sum(jnp.dot(tri_col, part, preferred_element_type=F32) for part in _split3(dt_c * a_col))
    cum_r = sum(jnp.dot(part, tri_row, preferred_element_type=F32) for part in _split3(dt_r * a_row))
    cum_full = _expand_heads(cum_c)
    dt_full = _expand_heads(dt_c)
    total = cum_full[0:1, :] if backward else cum_full[q - 1:q, :]

    xdt = xs * dt_full
    cmb = cm.astype(BF16)
    cb = lax.dot_general(cmb, bm.astype(BF16), (((1,), (1,)), ((), ())), preferred_element_type=F32)
    h = h_ref[...]
    y_off = jnp.dot(cmb, h.astype(BF16), preferred_element_type=F32) * jnp.exp(cum_full)
    wgt = (xdt * jnp.exp(total - cum_full)).astype(BF16)
    h_ref[...] = jnp.exp(total) * h + jnp.dot(bm.T.astype(BF16), wgt, preferred_element_type=F32)

    xdt_b = xdt.astype(BF16)
    lane = lax.broadcasted_iota(jnp.int32, (q, 2 * SSD_P), 1)
    parts = []
    for pair in range(SSD_E // 2):
        x_pair = xdt_b[:, pair * 2 * SSD_P:(pair + 1) * 2 * SSD_P]
        ys = []
        for e in (2 * pair, 2 * pair + 1):
            seg = cum_c[:, e:e + 1] - cum_r[e:e + 1, :]
            dec = jnp.exp(jnp.where(keep, seg, -jnp.inf))
            ys.append(jnp.dot((cb * dec).astype(BF16), x_pair, preferred_element_type=F32))
        parts.append(jnp.where(lane < SSD_P, ys[0], ys[1]))
    return jnp.concatenate(parts, axis=1) + y_off


def _ssd_kernel(xf_ref, bf_ref, cf_ref, xb_ref, bb_ref, cb_ref, dcf_ref, drf_ref, dcb_ref, drb_ref,
                bias_c_ref, bias_r_ref, a_c_ref, a_r_ref, d_ref, yf_ref, yb_ref, hf_sc, hb_sc):
    @pl.when(pl.program_id(2) == 0)
    def _():
        hf_sc[...] = jnp.zeros(hf_sc.shape, F32)
        hb_sc[...] = jnp.zeros(hb_sc.shape, F32)

    nsub = xf_ref.shape[1] // SSD_Q
    for i in range(nsub):
        rf = slice(i * SSD_Q, (i + 1) * SSD_Q)
        rb = slice((nsub - 1 - i) * SSD_Q, (nsub - i) * SSD_Q)
        xs = xf_ref[0, rf, :].astype(F32)
        yf = _ssd_direction(xs, bf_ref[0, rf, :].astype(F32), cf_ref[0, rf, :].astype(F32),
                            dcf_ref[0, 0, 0, rf, :], drf_ref[0, 0, 0, :, rf],
                            bias_c_ref[0, 0], bias_r_ref[0, 0], a_c_ref[0, 0], a_r_ref[0, 0], hf_sc, False)
        yf_ref[0, rf, :] = (yf + d_ref[0] * xs).astype(yf_ref.dtype)
        yb = _ssd_direction(xb_ref[0, rb, :].astype(F32), bb_ref[0, rb, :].astype(F32),
                            cb_ref[0, rb, :].astype(F32), dcb_ref[0, 0, 0, rb, :], drb_ref[0, 0, 0, :, rb],
                            bias_c_ref[1, 0], bias_r_ref[1, 0], a_c_ref[1, 0], a_r_ref[1, 0], hb_sc, True)
        yb_ref[0, rb, :] = yb.astype(yb_ref.dtype)


def _ssd(xc, dt_col, dt_row, bias_c, bias_r, a_c, a_r, d_full, ctx_len, out_dtype):
    b, t, _ = xc.shape
    rows = _tile(ctx_len, SSD_ROWS, SSD_Q)
    nblk = t // rows
    nctx = ctx_len // rows
    bblk = SSD_INNER // SSD_N
    cblk = bblk + SSD_G

    def bidx(c):
        return jnp.where(c < nctx, nctx - 1 - c, nblk - 1 - (c - nctx))

    fx = lambda bi, g, c: (bi, c, g)
    fb = lambda bi, g, c: (bi, c, bblk + g)
    fc = lambda bi, g, c: (bi, c, cblk + g)
    bx = lambda bi, g, c: (bi, bidx(c), g)
    bb = lambda bi, g, c: (bi, bidx(c), bblk + g)
    bc = lambda bi, g, c: (bi, bidx(c), cblk + g)
    small = lambda a: pl.BlockSpec((2, 1) + a.shape[2:], lambda bi, g, c: (0, g, 0, 0))
    return pl.pallas_call(
        _ssd_kernel,
        grid=(b, SSD_G, nblk),
        in_specs=[pl.BlockSpec((1, rows, SSD_GW), fx), pl.BlockSpec((1, rows, SSD_N), fb),
                  pl.BlockSpec((1, rows, SSD_N), fc),
                  pl.BlockSpec((1, rows, SSD_GW), bx), pl.BlockSpec((1, rows, SSD_N), bb),
                  pl.BlockSpec((1, rows, SSD_N), bc),
                  pl.BlockSpec((1, 1, 1, rows, SSD_E), lambda bi, g, c: (bi, 0, g, c, 0)),
                  pl.BlockSpec((1, 1, 1, SSD_E, rows), lambda bi, g, c: (bi, 0, g, 0, c)),
                  pl.BlockSpec((1, 1, 1, rows, SSD_E), lambda bi, g, c: (bi, 1, g, bidx(c), 0)),
                  pl.BlockSpec((1, 1, 1, SSD_E, rows), lambda bi, g, c: (bi, 1, g, 0, bidx(c))),
                  small(bias_c), small(bias_r), small(a_c), small(a_r),
                  pl.BlockSpec((1, 1, SSD_GW), lambda bi, g, c: (g, 0, 0))],
        out_specs=[pl.BlockSpec((1, rows, SSD_GW), fx), pl.BlockSpec((1, rows, SSD_GW), bx)],
        out_shape=[jax.ShapeDtypeStruct((b, t, SSD_INNER), out_dtype)] * 2,
        scratch_shapes=[pltpu.VMEM((SSD_N, SSD_GW), F32), pltpu.VMEM((SSD_N, SSD_GW), F32)],
        compiler_params=_params("parallel", "parallel", "arbitrary"),
        name="ssd_scan",
    )(xc, xc, xc, xc, xc, xc, dt_col, dt_row, dt_col, dt_row, bias_c, bias_r, a_c, a_r, d_full)


MERGE_TN = 512


def _merge_kernel(ya_ref, ga_ref, yb_ref, gb_ref, yf_ref, ybk_ref, z_ref, nrm_ref,
                  mg_ref, wa_ref, wb_ref, wc_ref, u_ref, c_sc):
    a_in = (ya_ref[0].astype(F32) * _silu(ga_ref[0].astype(F32))).astype(BF16)
    b_in = (yb_ref[0].astype(F32) * _silu(gb_ref[0].astype(F32))).astype(BF16)
    v = (yf_ref[0].astype(F32) + ybk_ref[0].astype(F32)) * _silu(z_ref[0].astype(F32))
    for g in range(SSD_G):
        sl = slice(g * SSD_GW, (g + 1) * SSD_GW)
        c_sc[:, sl] = (_rms(v[:, sl]) * nrm_ref[:, sl]).astype(BF16)
    c_in = c_sc[...]
    d = u_ref.shape[-1]
    for j in range(0, d, MERGE_TN):
        sl = slice(j, j + MERGE_TN)
        br_a = jnp.dot(a_in, wa_ref[:, sl], preferred_element_type=F32)
        br_b = jnp.dot(b_in, wb_ref[:, sl], preferred_element_type=F32)
        br_c = jnp.dot(c_in, wc_ref[:, sl], preferred_element_type=F32)
        u = (jax.nn.sigmoid(mg_ref[0, :, j:j + MERGE_TN].astype(F32)) * br_a
             + jax.nn.sigmoid(mg_ref[0, :, d + j:d + j + MERGE_TN].astype(F32)) * br_b
             + jax.nn.sigmoid(mg_ref[0, :, 2 * d + j:2 * d + j + MERGE_TN].astype(F32)) * br_c)
        u_ref[0, :, sl] = u.astype(u_ref.dtype)


def _merge(ya, yb, yf, ybk, p3, ssd_norm, wa, wb, wc, row_off):
    b, rows, _ = ya.shape
    tm = _tile(rows, 256)
    d = D_MODEL
    ro = row_off // tm
    assert P_OFF['mg'] == 0

    def col(name, width):
        blk = P_OFF[name] // width
        return pl.BlockSpec((1, tm, width), lambda bi, i: (bi, i + ro, blk))

    loc = lambda w: pl.BlockSpec((1, tm, w), lambda bi, i: (bi, i, 0))
    glob = lambda w: pl.BlockSpec((1, tm, w), lambda bi, i: (bi, i + ro, 0))
    wspec = lambda k: pl.BlockSpec((k, d), lambda bi, i: (0, 0), pipeline_mode=pl.Buffered(1))
    return pl.pallas_call(
        _merge_kernel,
        grid=(b, rows // tm),
        in_specs=[loc(MLA_WIDTH), col('ga', MLA_WIDTH), loc(GQA_WIDTH), col('gb', GQA_WIDTH),
                  glob(SSD_INNER), glob(SSD_INNER), col('z', SSD_INNER),
                  pl.BlockSpec((1, SSD_INNER), lambda bi, i: (0, 0)),
                  glob(N_BRANCH * d), wspec(MLA_WIDTH), wspec(GQA_WIDTH), wspec(SSD_INNER)],
        out_specs=pl.BlockSpec((1, tm, d), lambda bi, i: (bi, i, 0)),
        out_shape=jax.ShapeDtypeStruct((b, rows, d), BF16),
        scratch_shapes=[pltpu.VMEM((tm, SSD_INNER), BF16)],
        compiler_params=_params("parallel", "parallel"),
        name="merge",
    )(ya, p3, yb, p3, yf, ybk, p3, ssd_norm.reshape(1, SSD_INNER), p3, wa, wb, wc)


def _out_ln_kernel(u_ref, w_ref, x_ref, gate_ref, g_ref, b_ref, *rest, with_next):
    out = jnp.dot(u_ref[0], w_ref[...], preferred_element_type=F32)
    r = DEEPNORM_ALPHA * x_ref[0] + gate_ref[0] * out
    xn = _layer_norm(r) * g_ref[...] + b_ref[...]
    if with_next:
        sh_ref, sc_ref, xo_ref, xm_ref = rest
        xo_ref[0] = xn
        xm_ref[0] = (_layer_norm(xn) * (1.0 + sc_ref[0]) + sh_ref[0]).astype(xm_ref.dtype)
    else:
        (xo_ref,) = rest
        xo_ref[0] = xn


def _out_ln(u, w_out, xc, tab, ln_g, ln_b, next_tab, nb, ctx_len, row_off):
    b, rows, d = u.shape
    tm = _tile(ctx_len, 256)
    ro = row_off // tm
    row = _mod_row_index(nb, ctx_len // tm)
    with_next = next_tab is not None
    loc = pl.BlockSpec((1, tm, d), lambda bi, i: (bi, i, 0))
    vec = pl.BlockSpec((1, d), lambda bi, i: (0, 0))
    in_specs = [loc, pl.BlockSpec((d, d), lambda bi, i: (0, 0)),
                pl.BlockSpec((1, tm, d), lambda bi, i: (bi, i + ro, 0)),
                pl.BlockSpec((1, 1, d), lambda bi, i: (row(bi, i + ro), 0, 2)), vec, vec]
    args = [u, w_out, xc, tab, ln_g.reshape(1, d), ln_b.reshape(1, d)]
    out_specs = [loc]
    out_shape = [jax.ShapeDtypeStruct((b, rows, d), F32)]
    if with_next:
        in_specs += [pl.BlockSpec((1, 1, d), lambda bi, i: (row(bi, i + ro), 0, 0)),
                     pl.BlockSpec((1, 1, d), lambda bi, i: (row(bi, i + ro), 0, 1))]
        args += [next_tab, next_tab]
        out_specs.append(loc)
        out_shape.append(jax.ShapeDtypeStruct((b, rows, d), BF16))
    return pl.pallas_call(
        functools.partial(_out_ln_kernel, with_next=with_next),
        grid=(b, rows // tm),
        in_specs=in_specs,
        out_specs=out_specs,
        out_shape=out_shape,
        compiler_params=_params("parallel", "parallel"),
        name="out_ln",
    )(*args)


def _rope_angles(rows, dim):
    row, col = jnp.meshgrid(jnp.arange(rows, dtype=F32), jnp.arange(GRID_W, dtype=F32), indexing='ij')
    half = dim // 2
    inv_freq = ROPE_THETA ** (-jnp.arange(0, half, 2, dtype=F32) / half)
    ang_r = row.reshape(-1, 1) * inv_freq
    ang_c = col.reshape(-1, 1) * inv_freq
    return jnp.concatenate([ang_r, ang_r, ang_c, ang_c], axis=-1)


def _rope_tables(seq, ctx_len, dim):
    ang = _rope_angles(seq // GRID_W, dim)
    cos = jnp.concatenate([jnp.ones((ctx_len, dim), F32), jnp.cos(ang)], axis=0)
    sin = jnp.concatenate([jnp.zeros((ctx_len, dim), F32), jnp.sin(ang)], axis=0)
    return cos, sin


def _roll_tables(cos, sin, dim, scale):
    t = cos.shape[0]
    quarter = dim // 4
    first = (jnp.arange(dim) % (2 * quarter)) < quarter
    s1 = jnp.where(first, -sin, 0.0)
    s2 = jnp.where(first, 0.0, sin)
    pad = lambda a: jnp.pad(a * scale, ((0, 0), (0, 128 - dim)))
    return pad(cos), pad(s1), pad(s2)


def _rot_matrix(dim):
    quarter = dim // 4
    r = np.zeros((dim, dim), np.float32)
    for i in range(dim):
        blk = i // quarter
        if blk % 2 == 0:
            r[i + quarter, i] = -1.0
        else:
            r[i - quarter, i] = 1.0
    return jnp.asarray(r)


def _mla_q_weights(w_uq):
    w = w_uq.reshape(MLA_Q_LORA, MLA_HEADS, MLA_QK)
    nope, pe = w[..., :MLA_NOPE], w[..., MLA_NOPE:]
    pe_rot = jnp.einsum('khd,de->khe', pe, _rot_matrix(MLA_ROPE), precision=HIGHEST)
    zpad = jnp.zeros((MLA_Q_LORA, MLA_HEADS, MLA_QK_PAD - MLA_QK), F32)
    main = jnp.concatenate([nope, pe, zpad], axis=-1)
    rot = jnp.concatenate([jnp.zeros_like(nope), pe_rot, zpad], axis=-1)
    return (main.reshape(MLA_Q_LORA, -1).astype(BF16), rot.reshape(MLA_Q_LORA, -1).astype(BF16))


def _permute_w_in(w):
    wb = w.astype(BF16)
    parts = [wb[:, IN_OFFSETS[n]:IN_OFFSETS[n] + IN_WIDTHS[n]] for n in P_ORDER]
    parts.append(jnp.zeros((w.shape[0], P_WIDTH - P_USED), BF16))
    return jnp.concatenate(parts, axis=1)


P_DTYPE = BF16
Y_DTYPE = BF16


def kernel(x, c, ctx, c_ctx, w_mod, b_mod, w_in, mla_q_norm, mla_w_uq, mla_kv_norm, mla_w_ukv,
           gqa_q_norm, gqa_k_norm, ssd_conv_w, ssd_conv_b, ssd_a_log, ssd_dt_bias, ssd_d, ssd_norm,
           w_br_a, w_br_b, w_br_c, w_out, ln_g, ln_b):
    nb, seq, d = x.shape
    ctx_len = ctx.shape[1]
    t = ctx_len + seq
    depth = w_in.shape[0]
    assert d == D_MODEL and nb < 8 and seq % GRID_W == 0
    assert ctx_len % SSD_Q == 0 and seq % SSD_Q == 0

    cos_a, sin_a = _rope_tables(seq, ctx_len, MLA_ROPE)
    cos_b, sin_b = _rope_tables(seq, ctx_len, GQA_DIM)
    sq = MLA_QK ** -0.5 * LOG2E
    zq = jnp.zeros((t, MLA_QK_PAD - MLA_QK), F32)
    cosq = jnp.concatenate([jnp.full((t, MLA_NOPE), sq, F32), cos_a * sq, zq], axis=1)
    sinq = jnp.concatenate([jnp.zeros((t, MLA_NOPE), F32), sin_a * sq, zq], axis=1)
    mla_tabs = (cosq, sinq) + _roll_tables(cos_a, sin_a, MLA_ROPE, 1.0)
    gqa_tabs = (_roll_tables(cos_b, sin_b, GQA_DIM, GQA_DIM ** -0.5 * LOG2E)
                + _roll_tables(cos_b, sin_b, GQA_DIM, 1.0))

    c_rows = jnp.zeros((8, d), F32).at[:nb].set(c).at[nb].set(c_ctx)
    tabs = [_mod_rows(c_rows, w_mod[l], b_mod[l]).reshape(8, 1, 3 * d) for l in range(depth)]

    xc = jnp.concatenate([ctx, x], axis=1)
    xm = _ln_mod(xc, tabs[0], nb, ctx_len)

    for l in range(depth):
        last = l == depth - 1
        wp = _permute_w_in(w_in[l])
        p3 = _in_proj(xm.reshape(nb * t, d), wp, P_DTYPE).reshape(nb, t, P_WIDTH)

        wqm, wqr = _mla_q_weights(mla_w_uq[l])
        qa, ka, va = _mla_prep(p3, mla_q_norm[l].reshape(1, -1), mla_kv_norm[l].reshape(1, -1),
                               wqm, wqr, mla_w_ukv[l].astype(BF16), mla_tabs, ctx_len)
        ya = _attention(qa, ka, va, ctx_len, 256, 2, last, Y_DTYPE)
        qb, kb, vb = _gqa_prep(p3, gqa_q_norm[l].reshape(1, -1), gqa_k_norm[l].reshape(1, -1),
                               gqa_tabs, ctx_len)
        yb = _attention(qb, kb, vb, ctx_len, 128, 1, last, Y_DTYPE)

        xconv = _conv(p3, ssd_conv_w[l], ssd_conv_b[l], ctx_len, Y_DTYPE)
        dtr = p3[:, :, P_OFF['dtr']:P_OFF['dtr'] + 2 * SSD_HEADS].astype(F32)
        dt5 = dtr.reshape(nb, t, 2, SSD_G, SSD_E)
        dt_col = jnp.transpose(dt5, (0, 2, 3, 1, 4))
        dt_row = jnp.transpose(dt5, (0, 2, 3, 4, 1))
        bias = ssd_dt_bias[l].astype(F32).reshape(2, SSD_G, SSD_E)
        a = -jnp.exp(ssd_a_log[l].astype(F32)).reshape(2, SSD_G, SSD_E)
        d_full = jnp.repeat(ssd_d[l].astype(F32), SSD_P).reshape(SSD_G, 1, SSD_GW)
        yf, ybk = _ssd(xconv, dt_col, dt_row, bias[:, :, None, :], bias[:, :, :, None],
                       a[:, :, None, :], a[:, :, :, None], d_full, ctx_len, Y_DTYPE)

        row_off = ctx_len if last else 0
        u = _merge(ya, yb, yf, ybk, p3, ssd_norm[l], w_br_a[l].astype(BF16), w_br_b[l].astype(BF16),
                   w_br_c[l].astype(BF16), row_off)
        if last:
            (xo,) = _out_ln(u, w_out[l].astype(BF16), xc, tabs[l], ln_g[l], ln_b[l], None, nb,
                            ctx_len, row_off)
            return xo
        xc, xm = _out_ln(u, w_out[l].astype(BF16), xc, tabs[l], ln_g[l], ln_b[l], tabs[l + 1], nb,
                         ctx_len, row_off)
```

```python
import functools
import math

import numpy as np
import jax
import jax.numpy as jnp
from jax import lax
from jax.experimental import pallas as pl
from jax.experimental.pallas import tpu as pltpu

F32 = jnp.float32
BF16 = jnp.bfloat16
HIGHEST = lax.Precision.HIGHEST

D_MODEL = 2048
DEPTH = 2
GRID_W = 64
ROPE_THETA = 10000.0
EPS = 1e-6

MLA_HEADS = 8
MLA_Q_LORA = 512
MLA_KV_LORA = 256
MLA_NOPE = 128
MLA_ROPE = 64
MLA_V = 128
MLA_QK = MLA_NOPE + MLA_ROPE
MLA_QK_PAD = 256
MLA_WIDTH = MLA_HEADS * MLA_V

GQA_HEADS = 8
GQA_KV_HEADS = 2
GQA_GROUP = GQA_HEADS // GQA_KV_HEADS
GQA_DIM = 128
GQA_WIDTH = GQA_HEADS * GQA_DIM
GQA_KV_WIDTH = GQA_KV_HEADS * GQA_DIM

SSD_INNER = D_MODEL
SSD_P = 64
SSD_HEADS = SSD_INNER // SSD_P
SSD_G = 4
SSD_E = SSD_HEADS // SSD_G
SSD_N = 128
SSD_CONV = 5
SSD_Q = 128
SSD_ROWS = 256
SSD_GW = SSD_E * SSD_P
SSD_CONV_DIM = SSD_INNER + 2 * SSD_G * SSD_N

N_BRANCH = 3
IN_SPLITS = (MLA_Q_LORA, MLA_KV_LORA, MLA_ROPE, MLA_WIDTH, GQA_WIDTH, GQA_KV_WIDTH, GQA_KV_WIDTH,
             GQA_WIDTH, SSD_INNER, SSD_CONV_DIM, 2 * SSD_HEADS, N_BRANCH * D_MODEL)
IN_NAMES = ('cq', 'ckv', 'kr', 'ga', 'gq', 'gk', 'gv', 'gb', 'z', 'xbc', 'dtr', 'mg')
IN_OFFSETS = dict(zip(IN_NAMES, np.concatenate([[0], np.cumsum(IN_SPLITS)[:-1]]).tolist()))
IN_WIDTHS = dict(zip(IN_NAMES, IN_SPLITS))
P_ORDER = ('mg', 'z', 'xbc', 'ga', 'gq', 'gb', 'cq', 'ckv', 'gk', 'gv', 'kr', 'dtr')
P_OFF = {}
_o = 0
for _n in P_ORDER:
    P_OFF[_n] = _o
    _o += IN_WIDTHS[_n]
P_USED = _o
P_TN = 512
P_WIDTH = -(-P_USED // P_TN) * P_TN

DEEPNORM_ALPHA = (2 * DEPTH) ** 0.25

VMEM_LIMIT = 56 * 2 ** 20


def _params(*sem):
    return pltpu.CompilerParams(dimension_semantics=sem, vmem_limit_bytes=VMEM_LIMIT)


def _tile(n, target, align=8):
    t = min(n, target)
    while t > align and (n % t or t % align):
        t -= align
    assert n % t == 0, (n, target)
    return t


def _silu(v):
    return v * jax.nn.sigmoid(v)


def _softplus(v):
    return jnp.maximum(v, 0.0) + jnp.log1p(jnp.exp(-jnp.abs(v)))


def _layer_norm(v):
    mu = jnp.mean(v, axis=-1, keepdims=True)
    vc = v - mu
    var = jnp.mean(vc * vc, axis=-1, keepdims=True)
    return vc * lax.rsqrt(var + EPS)


def _rms(v):
    return v * lax.rsqrt(jnp.mean(v * v, axis=-1, keepdims=True) + EPS)


def _mod_kernel(c_ref, w_ref, b_ref, o_ref):
    a = _silu(c_ref[...]).astype(BF16)
    o_ref[...] = jnp.dot(a, w_ref[...].astype(BF16), preferred_element_type=F32) + b_ref[...]


def _mod_rows(c_rows, w_mod, b_mod):
    r, d = c_rows.shape
    n = w_mod.shape[1]
    tn = 512
    return pl.pallas_call(
        _mod_kernel,
        grid=(n // tn,),
        in_specs=[pl.BlockSpec((r, d), lambda j: (0, 0)),
                  pl.BlockSpec((d, tn), lambda j: (0, j)),
                  pl.BlockSpec((1, tn), lambda j: (0, j))],
        out_specs=pl.BlockSpec((r, tn), lambda j: (0, j)),
        out_shape=jax.ShapeDtypeStruct((r, n), F32),
        compiler_params=_params("arbitrary"),
        name="mod_rows",
    )(c_rows, w_mod, b_mod.reshape(1, n))


def _ln_mod_kernel(x_ref, sh_ref, sc_ref, o_ref):
    y = _layer_norm(x_ref[0])
    o_ref[0] = (y * (1.0 + sc_ref[0]) + sh_ref[0]).astype(o_ref.dtype)


def _mod_row_index(nb, ctx_tiles):
    return lambda b, i: jnp.where(i < ctx_tiles, nb, b)


def _ln_mod(xc, tab, nb, ctx_len):
    b, t, d = xc.shape
    tm = _tile(ctx_len, 256)
    row = _mod_row_index(nb, ctx_len // tm)
    return pl.pallas_call(
        _ln_mod_kernel,
        grid=(b, t // tm),
        in_specs=[pl.BlockSpec((1, tm, d), lambda bi, i: (bi, i, 0)),
                  pl.BlockSpec((1, 1, d), lambda bi, i: (row(bi, i), 0, 0)),
                  pl.BlockSpec((1, 1, d), lambda bi, i: (row(bi, i), 0, 1))],
        out_specs=pl.BlockSpec((1, tm, d), lambda bi, i: (bi, i, 0)),
        out_shape=jax.ShapeDtypeStruct((b, t, d), BF16),
        compiler_params=_params("parallel", "parallel"),
        name="ln_mod",
    )(xc, tab, tab)


def _matmul_kernel(x_ref, w_ref, o_ref):
    o_ref[...] = jnp.dot(x_ref[...], w_ref[...], preferred_element_type=F32).astype(o_ref.dtype)


def _in_proj(xm2, wp, out_dtype):
    m, k = xm2.shape
    n = wp.shape[1]
    tm = _tile(m, 1024)
    tn = P_TN
    return pl.pallas_call(
        _matmul_kernel,
        grid=(m // tm, n // tn),
        in_specs=[pl.BlockSpec((tm, k), lambda i, j: (i, 0)),
                  pl.BlockSpec((k, tn), lambda i, j: (0, j))],
        out_specs=pl.BlockSpec((tm, tn), lambda i, j: (i, j)),
        out_shape=jax.ShapeDtypeStruct((m, n), out_dtype),
        compiler_params=_params("parallel", "arbitrary"),
        name="in_proj",
    )(xm2, wp)


def _mla_prep_kernel(cq_ref, ckv_ref, kr_ref, qn_ref, kvn_ref, wqm_ref, wqr_ref, wkv_ref,
                     cosq_ref, sinq_ref, ck_ref, s1_ref, s2_ref, qa_ref, ka_ref, va_ref):
    cqn = (_rms(cq_ref[0].astype(F32)) * qn_ref[...]).astype(BF16)
    qm = jnp.dot(cqn, wqm_ref[...], preferred_element_type=F32)
    qr = jnp.dot(cqn, wqr_ref[...], preferred_element_type=F32)
    cosq = cosq_ref[...]
    sinq = sinq_ref[...]
    for h in range(MLA_HEADS):
        sl = slice(h * MLA_QK_PAD, (h + 1) * MLA_QK_PAD)
        qa_ref[0, h] = (qm[:, sl] * cosq + qr[:, sl] * sinq).T.astype(qa_ref.dtype)
    ckvn = (_rms(ckv_ref[0].astype(F32)) * kvn_ref[...]).astype(BF16)
    kv = jnp.dot(ckvn, wkv_ref[...], preferred_element_type=F32)
    kr = kr_ref[0].astype(F32)
    kpe = (kr * ck_ref[...] + pltpu.roll(kr, 128 - 16, axis=1) * s1_ref[...]
           + pltpu.roll(kr, 16, axis=1) * s2_ref[...]).astype(ka_ref.dtype)
    for h in range(MLA_HEADS):
        base = h * (MLA_NOPE + MLA_V)
        ka_ref[0, h, :, 0:MLA_NOPE] = kv[:, base:base + MLA_NOPE].astype(ka_ref.dtype)
        ka_ref[0, h, :, MLA_NOPE:MLA_QK_PAD] = kpe
        va_ref[0, h] = kv[:, base + MLA_NOPE:base + MLA_NOPE + MLA_V].T.astype(va_ref.dtype)


def _mla_prep(p3, qn, kvn, wqm, wqr, wkv, tabs, ctx_len):
    b, t, _ = p3.shape
    tm = _tile(ctx_len, 256)
    cosq, sinq, ck, s1, s2 = tabs

    def col(name, width):
        blk = P_OFF[name] // width
        return pl.BlockSpec((1, tm, width), lambda bi, i: (bi, i, blk))

    def full(a):
        return pl.BlockSpec(a.shape, lambda bi, i: (0,) * a.ndim)

    def rows(a):
        return pl.BlockSpec((tm, a.shape[1]), lambda bi, i: (i, 0))

    hm = lambda w: pl.BlockSpec((1, MLA_HEADS, tm, w), lambda bi, i: (bi, 0, i, 0))
    return pl.pallas_call(
        _mla_prep_kernel,
        grid=(b, t // tm),
        in_specs=[col('cq', MLA_Q_LORA), col('ckv', MLA_KV_LORA), col('kr', 128),
                  full(qn), full(kvn), full(wqm), full(wqr), full(wkv),
                  rows(cosq), rows(sinq), rows(ck), rows(s1), rows(s2)],
        out_specs=[pl.BlockSpec((1, MLA_HEADS, MLA_QK_PAD, tm), lambda bi, i: (bi, 0, 0, i)),
                   hm(MLA_QK_PAD),
                   pl.BlockSpec((1, MLA_HEADS, MLA_V, tm), lambda bi, i: (bi, 0, 0, i))],
        out_shape=[jax.ShapeDtypeStruct((b, MLA_HEADS, MLA_QK_PAD, t), BF16),
                   jax.ShapeDtypeStruct((b, MLA_HEADS, t, MLA_QK_PAD), BF16),
                   jax.ShapeDtypeStruct((b, MLA_HEADS, MLA_V, t), BF16)],
        compiler_params=_params("parallel", "parallel"),
        name="mla_prep",
    )(p3, p3, p3, qn, kvn, wqm, wqr, wkv, cosq, sinq, ck, s1, s2)


def _rope128(y, c, s1, s2):
    return y * c + pltpu.roll(y, 128 - 32, axis=1) * s1 + pltpu.roll(y, 32, axis=1) * s2


def _gqa_prep_kernel(gq_ref, gk_ref, gv_ref, qn_ref, kn_ref, cq_ref, s1q_ref, s2q_ref,
                     ck_ref, s1k_ref, s2k_ref, qb_ref, kb_ref, vb_ref):
    gq = gq_ref[0].astype(F32)
    for h in range(GQA_HEADS):
        y = _rms(gq[:, h * GQA_DIM:(h + 1) * GQA_DIM]) * qn_ref[...]
        qb_ref[0, h] = _rope128(y, cq_ref[...], s1q_ref[...], s2q_ref[...]).T.astype(qb_ref.dtype)
    gk = gk_ref[0].astype(F32)
    gv = gv_ref[0].astype(F32)
    for h in range(GQA_KV_HEADS):
        y = _rms(gk[:, h * GQA_DIM:(h + 1) * GQA_DIM]) * kn_ref[...]
        kb_ref[0, h] = _rope128(y, ck_ref[...], s1k_ref[...], s2k_ref[...]).astype(kb_ref.dtype)
        vb_ref[0, h] = gv[:, h * GQA_DIM:(h + 1) * GQA_DIM].T.astype(vb_ref.dtype)


def _gqa_prep(p3, qn, kn, tabs, ctx_len):
    b, t, _ = p3.shape
    tm = _tile(ctx_len, 256)

    def col(name, width):
        blk = P_OFF[name] // width
        return pl.BlockSpec((1, tm, width), lambda bi, i: (bi, i, blk))

    def full(a):
        return pl.BlockSpec(a.shape, lambda bi, i: (0,) * a.ndim)

    def rows(a):
        return pl.BlockSpec((tm, a.shape[1]), lambda bi, i: (i, 0))

    hm = lambda nh: pl.BlockSpec((1, nh, tm, GQA_DIM), lambda bi, i: (bi, 0, i, 0))
    hmt = lambda nh: pl.BlockSpec((1, nh, GQA_DIM, tm), lambda bi, i: (bi, 0, 0, i))
    return pl.pallas_call(
        _gqa_prep_kernel,
        grid=(b, t // tm),
        in_specs=[col('gq', GQA_WIDTH), col('gk', GQA_KV_WIDTH), col('gv', GQA_KV_WIDTH),
                  full(qn), full(kn)] + [rows(a) for a in tabs],
        out_specs=[hmt(GQA_HEADS), hm(GQA_KV_HEADS), hmt(GQA_KV_HEADS)],
        out_shape=[jax.ShapeDtypeStruct((b, GQA_HEADS, GQA_DIM, t), BF16),
                   jax.ShapeDtypeStruct((b, GQA_KV_HEADS, t, GQA_DIM), BF16),
                   jax.ShapeDtypeStruct((b, GQA_KV_HEADS, GQA_DIM, t), BF16)],
        compiler_params=_params("parallel", "parallel"),
        name="gqa_prep",
    )(p3, p3, p3, qn, kn, *tabs)


ATTN_KC = 256
ATTN_QB = 128
LOG2E = math.log2(math.e)


def _attn_logits(qts, k_ref, s_buf, nkeys):
    nqb = qts[0].shape[1] // ATTN_QB
    for j, qt in enumerate(qts):
        st = jnp.dot(k_ref[0, j, 0:nkeys, :], qt, preferred_element_type=F32)
        for b in range(nqb):
            s_buf[j * nqb + b, 0:nkeys, :] = st[:, b * ATTN_QB:(b + 1) * ATTN_QB]


def _attn_softmax_slab(s_buf, p_buf, l_buf, b, nkeys):
    parts = [jnp.max(s_buf[b, k0:k0 + ATTN_KC, :].reshape(ATTN_KC // 64, 8, 8, ATTN_QB), axis=0)
             for k0 in range(0, nkeys, ATTN_KC)]
    while len(parts) > 1:
        parts = [jnp.maximum(parts[i], parts[i + 1]) if i + 1 < len(parts) else parts[i]
                 for i in range(0, len(parts), 2)]
    m = jnp.max(parts[0], axis=(0, 1), keepdims=True)[0]
    lacc = jnp.zeros((8, 8, ATTN_QB), F32)
    for k0 in range(0, nkeys, ATTN_KC):
        p = jnp.exp2(s_buf[b, k0:k0 + ATTN_KC, :] - m)
        lacc = lacc + jnp.sum(p.reshape(ATTN_KC // 64, 8, 8, ATTN_QB), axis=0)
        p_buf[b, k0:k0 + ATTN_KC, :] = p.astype(p_buf.dtype)
    l = jnp.sum(lacc, axis=(0, 1), keepdims=True)[0]
    l_buf[b] = jnp.broadcast_to(1.0 / l, (8, ATTN_QB))


def _attn_output(vt_ref, p_buf, l_buf, o_ref, nkeys, hps, group, tq):
    dv = vt_ref.shape[2]
    nqb = group * tq // ATTN_QB
    for j in range(hps):
        slabs = range(j * nqb, (j + 1) * nqb)
        pt = jnp.concatenate([p_buf[b, 0:nkeys, :] for b in slabs], axis=1)
        ot = jnp.dot(vt_ref[0, j, :, 0:nkeys], pt, preferred_element_type=F32)
        ot = ot * jnp.concatenate([l_buf[b, 0:1, :] for b in slabs], axis=1)
        for g in range(group):
            c0 = (j * group + g) * dv
            o_ref[0, :, c0:c0 + dv] = ot[:, g * tq:(g + 1) * tq].T.astype(o_ref.dtype)


def _attn_queries(qt_ref, hps, group):
    return [jnp.concatenate([qt_ref[0, j * group + g] for g in range(group)], axis=1)
            for j in range(hps)]


def _attn_ctx_kernel(qt_ref, k_ref, vt_ref, o_ref, s_sc, p_sc, l_sc, *, group, tq):
    hps, nkeys = k_ref.shape[1], k_ref.shape[2]
    nslab = hps * group * tq // ATTN_QB
    _attn_logits(_attn_queries(qt_ref, hps, group), k_ref, s_sc, nkeys)

    def slab(b, carry):
        _attn_softmax_slab(s_sc, p_sc, l_sc, b, nkeys)
        return carry

    lax.fori_loop(0, nslab, slab, 0)
    _attn_output(vt_ref, p_sc, l_sc, o_ref, nkeys, hps, group, tq)


def _attn_lat_kernel(qt_ref, k_ref, vt_ref, o_ref, s0, s1, p0, p1, l0, l1, *, group, tq):
    hps, nkeys = k_ref.shape[1], k_ref.shape[2]
    nslab = hps * group * tq // ATTN_QB
    g = pl.program_id(2)

    @pl.when((pl.program_id(0) == 0) & (pl.program_id(1) == 0) & (g == 0))
    def _():
        for buf in (s0, s1, p0, p1, l0, l1):
            buf[...] = jnp.zeros(buf.shape, buf.dtype)

    def step(s_a, s_b, p_b, p_c, l_b, l_c):
        _attn_logits(_attn_queries(qt_ref, hps, group), k_ref, s_a, nkeys)
        for b in range(nslab):
            _attn_softmax_slab(s_b, p_b, l_b, b, nkeys)
        _attn_output(vt_ref, p_c, l_c, o_ref, nkeys, hps, group, tq)

    pl.when(g % 2 == 0)(lambda: step(s0, s1, p1, p0, l1, l0))
    pl.when(g % 2 == 1)(lambda: step(s1, s0, p0, p1, l0, l1))


def _attention(qt, k, vt, ctx_len, tq, hps, with_ctx_queries, out_dtype):
    b, hq, dk, t = qt.shape
    hkv, dv = k.shape[1], vt.shape[2]
    group = hq // hkv
    tq = _tile(ctx_len, tq, 128)
    nslab = hps * group * tq // ATTN_QB
    ctx_tiles = ctx_len // tq
    nq = t // tq - ctx_tiles
    width = hps * group * dv
    assert t % ATTN_KC == 0 and ctx_len % ATTN_KC == 0 and hkv % hps == 0

    def scratch(nkeys):
        return [pltpu.VMEM((nslab, nkeys, ATTN_QB), F32), pltpu.VMEM((nslab, nkeys, ATTN_QB), BF16),
                pltpu.VMEM((nslab, 8, ATTN_QB), F32)]

    s_lat, p_lat, l_lat = scratch(t)
    y_lat = pl.pallas_call(
        functools.partial(_attn_lat_kernel, group=group, tq=tq),
        grid=(b, hkv // hps, nq + 2),
        in_specs=[pl.BlockSpec((1, hps * group, dk, tq),
                               lambda bi, h, i: (bi, h, 0, jnp.minimum(i, nq - 1) + ctx_tiles)),
                  pl.BlockSpec((1, hps, t, dk), lambda bi, h, i: (bi, h, 0, 0)),
                  pl.BlockSpec((1, hps, dv, t), lambda bi, h, i: (bi, h, 0, 0))],
        out_specs=pl.BlockSpec((1, tq, width), lambda bi, h, i: (bi, jnp.maximum(i - 2, 0), h)),
        out_shape=jax.ShapeDtypeStruct((b, nq * tq, hq * dv), out_dtype),
        scratch_shapes=[s_lat, s_lat, p_lat, p_lat, l_lat, l_lat],
        compiler_params=_params("arbitrary", "arbitrary", "arbitrary"),
        name="attention_lat_dk%d" % dk,
    )(qt, k, vt)
    if not with_ctx_queries:
        return y_lat
    y_ctx = pl.pallas_call(
        functools.partial(_attn_ctx_kernel, group=group, tq=tq),
        grid=(b, hkv // hps, ctx_tiles),
        in_specs=[pl.BlockSpec((1, hps * group, dk, tq), lambda bi, h, i: (bi, h, 0, i)),
                  pl.BlockSpec((1, hps, ctx_len, dk), lambda bi, h, i: (bi, h, 0, 0)),
                  pl.BlockSpec((1, hps, dv, ctx_len), lambda bi, h, i: (bi, h, 0, 0))],
        out_specs=pl.BlockSpec((1, tq, width), lambda bi, h, i: (bi, i, h)),
        out_shape=jax.ShapeDtypeStruct((b, ctx_len, hq * dv), out_dtype),
        scratch_shapes=scratch(ctx_len),
        compiler_params=_params("parallel", "parallel", "arbitrary"),
        name="attention_ctx_dk%d" % dk,
    )(qt, k, vt)
    return jnp.concatenate([y_ctx, y_lat], axis=1)


def _conv_kernel(x_ref, w_ref, b_ref, o_ref, pad_sc, *, ctx_len, rows):
    t = x_ref.shape[1]
    nch = x_ref.shape[2]
    halo = 8
    segs = ((0, ctx_len), (ctx_len, t))
    zeros = jnp.zeros((halo, nch), F32)
    for si, (lo, hi) in enumerate(segs):
        pad_sc[lo + si * halo:lo + (si + 1) * halo, :] = zeros
        for r0 in range(lo, hi, rows):
            pad_sc[r0 + (si + 1) * halo:r0 + (si + 1) * halo + rows, :] = x_ref[0, r0:r0 + rows, :].astype(F32)
    pad_sc[t + 2 * halo:t + 3 * halo, :] = zeros
    w = w_ref[...]
    bias = b_ref[...]
    for si, (lo, hi) in enumerate(segs):
        for r0 in range(lo, hi, rows):
            base = r0 + (si + 1) * halo - SSD_CONV // 2
            acc = bias + w[0:1, :] * pad_sc[base:base + rows, :]
            for kk in range(1, SSD_CONV):
                acc = acc + w[kk:kk + 1, :] * pad_sc[base + kk:base + kk + rows, :]
            o_ref[0, r0:r0 + rows, :] = _silu(acc).astype(o_ref.dtype)


def _conv(p3, conv_w, conv_b, ctx_len, out_dtype):
    b, t, _ = p3.shape
    nch = 256
    rows = _tile(ctx_len, 256)
    blk0 = P_OFF['xbc'] // nch
    kern = functools.partial(_conv_kernel, ctx_len=ctx_len, rows=rows)
    return pl.pallas_call(
        kern,
        grid=(b, SSD_CONV_DIM // nch),
        in_specs=[pl.BlockSpec((1, t, nch), lambda bi, j: (bi, 0, blk0 + j)),
                  pl.BlockSpec((SSD_CONV, nch), lambda bi, j: (0, j)),
                  pl.BlockSpec((1, nch), lambda bi, j: (0, j))],
        out_specs=pl.BlockSpec((1, t, nch), lambda bi, j: (bi, 0, j)),
        out_shape=jax.ShapeDtypeStruct((b, t, SSD_CONV_DIM), out_dtype),
        scratch_shapes=[pltpu.VMEM((t + 24, nch), F32)],
        compiler_params=_params("parallel", "parallel"),
        name="ssd_conv",
    )(p3, conv_w, conv_b.reshape(1, SSD_CONV_DIM))


def _split3(v):
    hi = v.astype(BF16)
    r1 = v - hi.astype(F32)
    mid = r1.astype(BF16)
    lo = (r1 - mid.astype(F32)).astype(BF16)
    return hi, mid, lo


def _expand_heads(col):
    q = col.shape[0]
    first = lax.broadcasted_iota(jnp.int32, (q, 2 * SSD_P), 1) < SSD_P
    tiles = []
    for e in range(0, SSD_E, 2):
        tiles.append(jnp.where(first, jnp.broadcast_to(col[:, e:e + 1], (q, 2 * SSD_P)),
                               jnp.broadcast_to(col[:, e + 1:e + 2], (q, 2 * SSD_P))))
    return jnp.concatenate(tiles, axis=1)


def _ssd_direction(xs, bm, cm, dt_col_raw, dt_row_raw, bias_col, bias_row, a_col, a_row, h_ref,
                   backward):
    q = xs.shape[0]
    ri = lax.broadcasted_iota(jnp.int32, (q, q), 0)
    ci = lax.broadcasted_iota(jnp.int32, (q, q), 1)
    low = ri >= ci
    keep = (ri <= ci) if backward else low
    tri_col = jnp.where(keep, 1.0, 0.0).astype(BF16)
    tri_row = jnp.where((ri >= ci) if backward else (ri <= ci), 1.0, 0.0).astype(BF16)

    dt_c = _softplus(dt_col_raw + bias_col)
    dt_r = _softplus(dt_row_raw + bias_row)
    cum_c = sum(jnp.dot(tri_col, part, preferred_element_type=F32) for part in _split3(dt_c * a_col))
    cum_r = sum(jnp.dot(part, tri_row, preferred_element_type=F32) for part in _split3(dt_r * a_row))
    cum_full = _expand_heads(cum_c)
    dt_full = _expand_heads(dt_c)
    total = cum_full[0:1, :] if backward else cum_full[q - 1:q, :]

    xdt = xs * dt_full
    cmb = cm.astype(BF16)
    cb = lax.dot_general(cmb, bm.astype(BF16), (((1,), (1,)), ((), ())), preferred_element_type=F32)
    h = h_ref[...]
    y_off = jnp.dot(cmb, h.astype(BF16), preferred_element_type=F32) * jnp.exp(cum_full)
    wgt = (xdt * jnp.exp(total - cum_full)).astype(BF16)
    h_ref[...] = jnp.exp(total) * h + jnp.dot(bm.T.astype(BF16), wgt, preferred_element_type=F32)

    xdt_b = xdt.astype(BF16)
    lane = lax.broadcasted_iota(jnp.int32, (q, 2 * SSD_P), 1)
    parts = []
    for pair in range(SSD_E // 2):
        x_pair = xdt_b[:, pair * 2 * SSD_P:(pair + 1) * 2 * SSD_P]
        ys = []
        for e in (2 * pair, 2 * pair + 1):
            seg = cum_c[:, e:e + 1] - cum_r[e:e + 1, :]
            dec = jnp.exp(jnp.where(keep, seg, -jnp.inf))
            ys.append(jnp.dot((cb * dec).astype(BF16), x_pair, preferred_element_type=F32))
        parts.append(jnp.where(lane < SSD_P, ys[0], ys[1]))
    return jnp.concatenate(parts, axis=1) + y_off


def _ssd_kernel(xf_ref, bf_ref, cf_ref, xb_ref, bb_ref, cb_ref, dcf_ref, drf_ref, dcb_ref, drb_ref,
                bias_c_ref, bias_r_ref, a_c_ref, a_r_ref, d_ref, yf_ref, yb_ref, hf_sc, hb_sc):
    @pl.when(pl.program_id(2) == 0)
    def _():
        hf_sc[...] = jnp.zeros(hf_sc.shape, F32)
        hb_sc[...] = jnp.zeros(hb_sc.shape, F32)

    nsub = xf_ref.shape[1] // SSD_Q
    for i in range(nsub):
        rf = slice(i * SSD_Q, (i + 1) * SSD_Q)
        rb = slice((nsub - 1 - i) * SSD_Q, (nsub - i) * SSD_Q)
        xs = xf_ref[0, rf, :].astype(F32)
        yf = _ssd_direction(xs, bf_ref[0, rf, :].astype(F32), cf_ref[0, rf, :].astype(F32),
                            dcf_ref[0, 0, 0, rf, :], drf_ref[0, 0, 0, :, rf],
                            bias_c_ref[0, 0], bias_r_ref[0, 0], a_c_ref[0, 0], a_r_ref[0, 0], hf_sc, False)
        yf_ref[0, rf, :] = (yf + d_ref[0] * xs).astype(yf_ref.dtype)
        yb = _ssd_direction(xb_ref[0, rb, :].astype(F32), bb_ref[0, rb, :].astype(F32),
                            cb_ref[0, rb, :].astype(F32), dcb_ref[0, 0, 0, rb, :], drb_ref[0, 0, 0, :, rb],
                            bias_c_ref[1, 0], bias_r_ref[1, 0], a_c_ref[1, 0], a_r_ref[1, 0], hb_sc, True)
        yb_ref[0, rb, :] = yb.astype(yb_ref.dtype)


def _ssd(xc, dt_col, dt_row, bias_c, bias_r, a_c, a_r, d_full, ctx_len, out_dtype):
    b, t, _ = xc.shape
    rows = _tile(ctx_len, SSD_ROWS, SSD_Q)
    nblk = t // rows
    nctx = ctx_len // rows
    bblk = SSD_INNER // SSD_N
    cblk = bblk + SSD_G

    def bidx(c):
        return jnp.where(c < nctx, nctx - 1 - c, nblk - 1 - (c - nctx))

    fx = lambda bi, g, c: (bi, c, g)
    fb = lambda bi, g, c: (bi, c, bblk + g)
    fc = lambda bi, g, c: (bi, c, cblk + g)
    bx = lambda bi, g, c: (bi, bidx(c), g)
    bb = lambda bi, g, c: (bi, bidx(c), bblk + g)
    bc = lambda bi, g, c: (bi, bidx(c), cblk + g)
    small = lambda a: pl.BlockSpec((2, 1) + a.shape[2:], lambda bi, g, c: (0, g, 0, 0))
    return pl.pallas_call(
        _ssd_kernel,
        grid=(b, SSD_G, nblk),
        in_specs=[pl.BlockSpec((1, rows, SSD_GW), fx), pl.BlockSpec((1, rows, SSD_N), fb),
                  pl.BlockSpec((1, rows, SSD_N), fc),
                  pl.BlockSpec((1, rows, SSD_GW), bx), pl.BlockSpec((1, rows, SSD_N), bb),
                  pl.BlockSpec((1, rows, SSD_N), bc),
                  pl.BlockSpec((1, 1, 1, rows, SSD_E), lambda bi, g, c: (bi, 0, g, c, 0)),
                  pl.BlockSpec((1, 1, 1, SSD_E, rows), lambda bi, g, c: (bi, 0, g, 0, c)),
                  pl.BlockSpec((1, 1, 1, rows, SSD_E), lambda bi, g, c: (bi, 1, g, bidx(c), 0)),
                  pl.BlockSpec((1, 1, 1, SSD_E, rows), lambda bi, g, c: (bi, 1, g, 0, bidx(c))),
                  small(bias_c), small(bias_r), small(a_c), small(a_r),
                  pl.BlockSpec((1, 1, SSD_GW), lambda bi, g, c: (g, 0, 0))],
        out_specs=[pl.BlockSpec((1, rows, SSD_GW), fx), pl.BlockSpec((1, rows, SSD_GW), bx)],
        out_shape=[jax.ShapeDtypeStruct((b, t, SSD_INNER), out_dtype)] * 2,
        scratch_shapes=[pltpu.VMEM((SSD_N, SSD_GW), F32), pltpu.VMEM((SSD_N, SSD_GW), F32)],
        compiler_params=_params("parallel", "parallel", "arbitrary"),
        name="ssd_scan",
    )(xc, xc, xc, xc, xc, xc, dt_col, dt_row, dt_col, dt_row, bias_c, bias_r, a_c, a_r, d_full)


MERGE_TN = 512


def _merge_kernel(ya_ref, ga_ref, yb_ref, gb_ref, yf_ref, ybk_ref, z_ref, nrm_ref,
                  mg_ref, wa_ref, wb_ref, wc_ref, u_ref, c_sc):
    a_in = (ya_ref[0].astype(F32) * _silu(ga_ref[0].astype(F32))).astype(BF16)
    b_in = (yb_ref[0].astype(F32) * _silu(gb_ref[0].astype(F32))).astype(BF16)
    v = (yf_ref[0].astype(F32) + ybk_ref[0].astype(F32)) * _silu(z_ref[0].astype(F32))
    for g in range(SSD_G):
        sl = slice(g * SSD_GW, (g + 1) * SSD_GW)
        c_sc[:, sl] = (_rms(v[:, sl]) * nrm_ref[:, sl]).astype(BF16)
    c_in = c_sc[...]
    d = u_ref.shape[-1]
    for j in range(0, d, MERGE_TN):
        sl = slice(j, j + MERGE_TN)
        br_a = jnp.dot(a_in, wa_ref[:, sl], preferred_element_type=F32)
        br_b = jnp.dot(b_in, wb_ref[:, sl], preferred_element_type=F32)
        br_c = jnp.dot(c_in, wc_ref[:, sl], preferred_element_type=F32)
        u = (jax.nn.sigmoid(mg_ref[0, :, j:j + MERGE_TN].astype(F32)) * br_a
             + jax.nn.sigmoid(mg_ref[0, :, d + j:d + j + MERGE_TN].astype(F32)) * br_b
             + jax.nn.sigmoid(mg_ref[0, :, 2 * d + j:2 * d + j + MERGE_TN].astype(F32)) * br_c)
        u_ref[0, :, sl] = u.astype(u_ref.dtype)


def _merge(ya, yb, yf, ybk, p3, ssd_norm, wa, wb, wc, row_off):
    b, rows, _ = ya.shape
    tm = _tile(rows, 256)
    d = D_MODEL
    ro = row_off // tm
    assert P_OFF['mg'] == 0

    def col(name, width):
        blk = P_OFF[name] // width
        return pl.BlockSpec((1, tm, width), lambda bi, i: (bi, i + ro, blk))

    loc = lambda w: pl.BlockSpec((1, tm, w), lambda bi, i: (bi, i, 0))
    glob = lambda w: pl.BlockSpec((1, tm, w), lambda bi, i: (bi, i + ro, 0))
    wspec = lambda k: pl.BlockSpec((k, d), lambda bi, i: (0, 0), pipeline_mode=pl.Buffered(1))
    return pl.pallas_call(
        _merge_kernel,
        grid=(b, rows // tm),
        in_specs=[loc(MLA_WIDTH), col('ga', MLA_WIDTH), loc(GQA_WIDTH), col('gb', GQA_WIDTH),
                  glob(SSD_INNER), glob(SSD_INNER), col('z', SSD_INNER),
                  pl.BlockSpec((1, SSD_INNER), lambda bi, i: (0, 0)),
                  glob(N_BRANCH * d), wspec(MLA_WIDTH), wspec(GQA_WIDTH), wspec(SSD_INNER)],
        out_specs=pl.BlockSpec((1, tm, d), lambda bi, i: (bi, i, 0)),
        out_shape=jax.ShapeDtypeStruct((b, rows, d), BF16),
        scratch_shapes=[pltpu.VMEM((tm, SSD_INNER), BF16)],
        compiler_params=_params("parallel", "parallel"),
        name="merge",
    )(ya, p3, yb, p3, yf, ybk, p3, ssd_norm.reshape(1, SSD_INNER), p3, wa, wb, wc)


def _out_ln_kernel(u_ref, w_ref, x_ref, gate_ref, g_ref, b_ref, *rest, with_next):
    out = jnp.dot(u_ref[0], w_ref[...], preferred_element_type=F32)
    r = DEEPNORM_ALPHA * x_ref[0] + gate_ref[0] * out
    xn = _layer_norm(r) * g_ref[...] + b_ref[...]
    if with_next:
        sh_ref, sc_ref, xo_ref, xm_ref = rest
        xo_ref[0] = xn
        xm_ref[0] = (_layer_norm(xn) * (1.0 + sc_ref[0]) + sh_ref[0]).astype(xm_ref.dtype)
    else:
        (xo_ref,) = rest
        xo_ref[0] = xn


def _out_ln(u, w_out, xc, tab, ln_g, ln_b, next_tab, nb, ctx_len, row_off):
    b, rows, d = u.shape
    tm = _tile(ctx_len, 256)
    ro = row_off // tm
    row = _mod_row_index(nb, ctx_len // tm)
    with_next = next_tab is not None
    loc = pl.BlockSpec((1, tm, d), lambda bi, i: (bi, i, 0))
    vec = pl.BlockSpec((1, d), lambda bi, i: (0, 0))
    in_specs = [loc, pl.BlockSpec((d, d), lambda bi, i: (0, 0)),
                pl.BlockSpec((1, tm, d), lambda bi, i: (bi, i + ro, 0)),
                pl.BlockSpec((1, 1, d), lambda bi, i: (row(bi, i + ro), 0, 2)), vec, vec]
    args = [u, w_out, xc, tab, ln_g.reshape(1, d), ln_b.reshape(1, d)]
    out_specs = [loc]
    out_shape = [jax.ShapeDtypeStruct((b, rows, d), F32)]
    if with_next:
        in_specs += [pl.BlockSpec((1, 1, d), lambda bi, i: (row(bi, i + ro), 0, 0)),
                     pl.BlockSpec((1, 1, d), lambda bi, i: (row(bi, i + ro), 0, 1))]
        args += [next_tab, next_tab]
        out_specs.append(loc)
        out_shape.append(jax.ShapeDtypeStruct((b, rows, d), BF16))
    return pl.pallas_call(
        functools.partial(_out_ln_kernel, with_next=with_next),
        grid=(b, rows // tm),
        in_specs=in_specs,
        out_specs=out_specs,
        out_shape=out_shape,
        compiler_params=_params("parallel", "parallel"),
        name="out_ln",
    )(*args)


def _rope_angles(rows, dim):
    row, col = jnp.meshgrid(jnp.arange(rows, dtype=F32), jnp.arange(GRID_W, dtype=F32), indexing='ij')
    half = dim // 2
    inv_freq = ROPE_THETA ** (-jnp.arange(0, half, 2, dtype=F32) / half)
    ang_r = row.reshape(-1, 1) * inv_freq
    ang_c = col.reshape(-1, 1) * inv_freq
    return jnp.concatenate([ang_r, ang_r, ang_c, ang_c], axis=-1)


def _rope_tables(seq, ctx_len, dim):
    ang = _rope_angles(seq // GRID_W, dim)
    cos = jnp.concatenate([jnp.ones((ctx_len, dim), F32), jnp.cos(ang)], axis=0)
    sin = jnp.concatenate([jnp.zeros((ctx_len, dim), F32), jnp.sin(ang)], axis=0)
    return cos, sin


def _roll_tables(cos, sin, dim, scale):
    t = cos.shape[0]
    quarter = dim // 4
    first = (jnp.arange(dim) % (2 * quarter)) < quarter
    s1 = jnp.where(first, -sin, 0.0)
    s2 = jnp.where(first, 0.0, sin)
    pad = lambda a: jnp.pad(a * scale, ((0, 0), (0, 128 - dim)))
    return pad(cos), pad(s1), pad(s2)


def _rot_matrix(dim):
    quarter = dim // 4
    r = np.zeros((dim, dim), np.float32)
    for i in range(dim):
        blk = i // quarter
        if blk % 2 == 0:
            r[i + quarter, i] = -1.0
        else:
            r[i - quarter, i] = 1.0
    return jnp.asarray(r)


def _mla_q_weights(w_uq):
    w = w_uq.reshape(MLA_Q_LORA, MLA_HEADS, MLA_QK)
    nope, pe = w[..., :MLA_NOPE], w[..., MLA_NOPE:]
    pe_rot = jnp.einsum('khd,de->khe', pe, _rot_matrix(MLA_ROPE), precision=HIGHEST)
    zpad = jnp.zeros((MLA_Q_LORA, MLA_HEADS, MLA_QK_PAD - MLA_QK), F32)
    main = jnp.concatenate([nope, pe, zpad], axis=-1)
    rot = jnp.concatenate([jnp.zeros_like(nope), pe_rot, zpad], axis=-1)
    return (main.reshape(MLA_Q_LORA, -1).astype(BF16), rot.reshape(MLA_Q_LORA, -1).astype(BF16))


def _permute_w_in(w):
    wb = w.astype(BF16)
    parts = [wb[:, IN_OFFSETS[n]:IN_OFFSETS[n] + IN_WIDTHS[n]] for n in P_ORDER]
    parts.append(jnp.zeros((w.shape[0], P_WIDTH - P_USED), BF16))
    return jnp.concatenate(parts, axis=1)


P_DTYPE = BF16
Y_DTYPE = BF16


def kernel(x, c, ctx, c_ctx, w_mod, b_mod, w_in, mla_q_norm, mla_w_uq, mla_kv_norm, mla_w_ukv,
           gqa_q_norm, gqa_k_norm, ssd_conv_w, ssd_conv_b, ssd_a_log, ssd_dt_bias, ssd_d, ssd_norm,
           w_br_a, w_br_b, w_br_c, w_out, ln_g, ln_b):
    nb, seq, d = x.shape
    ctx_len = ctx.shape[1]
    t = ctx_len + seq
    depth = w_in.shape[0]
    assert d == D_MODEL and nb < 8 and seq % GRID_W == 0
    assert ctx_len % SSD_Q == 0 and seq % SSD_Q == 0

    cos_a, sin_a = _rope_tables(seq, ctx_len, MLA_ROPE)
    cos_b, sin_b = _rope_tables(seq, ctx_len, GQA_DIM)
    sq = MLA_QK ** -0.5 * LOG2E
    zq = jnp.zeros((t, MLA_QK_PAD - MLA_QK), F32)
    cosq = jnp.concatenate([jnp.full((t, MLA_NOPE), sq, F32), cos_a * sq, zq], axis=1)
    sinq = jnp.concatenate([jnp.zeros((t, MLA_NOPE), F32), sin_a * sq, zq], axis=1)
    mla_tabs = (cosq, sinq) + _roll_tables(cos_a, sin_a, MLA_ROPE, 1.0)
    gqa_tabs = (_roll_tables(cos_b, sin_b, GQA_DIM, GQA_DIM ** -0.5 * LOG2E)
                + _roll_tables(cos_b, sin_b, GQA_DIM, 1.0))

    c_rows = jnp.zeros((8, d), F32).at[:nb].set(c).at[nb].set(c_ctx)
    tabs = [_mod_rows(c_rows, w_mod[l], b_mod[l]).reshape(8, 1, 3 * d) for l in range(depth)]

    xc = jnp.concatenate([ctx, x], axis=1)
    xm = _ln_mod(xc, tabs[0], nb, ctx_len)

    for l in range(depth):
        last = l == depth - 1
        wp = _permute_w_in(w_in[l])
        p3 = _in_proj(xm.reshape(nb * t, d), wp, P_DTYPE).reshape(nb, t, P_WIDTH)

        wqm, wqr = _mla_q_weights(mla_w_uq[l])
        qa, ka, va = _mla_prep(p3, mla_q_norm[l].reshape(1, -1), mla_kv_norm[l].reshape(1, -1),
                               wqm, wqr, mla_w_ukv[l].astype(BF16), mla_tabs, ctx_len)
        ya = _attention(qa, ka, va, ctx_len, 256, 2, not last, Y_DTYPE)
        qb, kb, vb = _gqa_prep(p3, gqa_q_norm[l].reshape(1, -1), gqa_k_norm[l].reshape(1, -1),
                               gqa_tabs, ctx_len)
        yb = _attention(qb, kb, vb, ctx_len, 128, 1, not last, Y_DTYPE)

        xconv = _conv(p3, ssd_conv_w[l], ssd_conv_b[l], ctx_len, Y_DTYPE)
        dtr = p3[:, :, P_OFF['dtr']:P_OFF['dtr'] + 2 * SSD_HEADS].astype(F32)
        dt5 = dtr.reshape(nb, t, 2, SSD_G, SSD_E)
        dt_col = jnp.transpose(dt5, (0, 2, 3, 1, 4))
        dt_row = jnp.transpose(dt5, (0, 2, 3, 4, 1))
        bias = ssd_dt_bias[l].astype(F32).reshape(2, SSD_G, SSD_E)
        a = -jnp.exp(ssd_a_log[l].astype(F32)).reshape(2, SSD_G, SSD_E)
        d_full = jnp.repeat(ssd_d[l].astype(F32), SSD_P).reshape(SSD_G, 1, SSD_GW)
        yf, ybk = _ssd(xconv, dt_col, dt_row, bias[:, :, None, :], bias[:, :, :, None],
                       a[:, :, None, :], a[:, :, :, None], d_full, ctx_len, Y_DTYPE)

        row_off = ctx_len if last else 0
        u = _merge(ya, yb, yf, ybk, p3, ssd_norm[l], w_br_a[l].astype(BF16), w_br_b[l].astype(BF16),
                   w_br_c[l].astype(BF16), row_off)
        if last:
            (xo,) = _out_ln(u, w_out[l].astype(BF16), xc, tabs[l], ln_g[l], ln_b[l], None, nb,
                            ctx_len, row_off)
            return xo
        xc, xm = _out_ln(u, w_out[l].astype(BF16), xc, tabs[l], ln_g[l], ln_b[l], tabs[l + 1], nb,
                         ctx_len, row_off)
```

```python
import functools
import math

import numpy as np
import jax
import jax.numpy as jnp
from jax import lax
from jax.experimental import pallas as pl
from jax.experimental.pallas import tpu as pltpu

F32 = jnp.float32
BF16 = jnp.bfloat16
HIGHEST = lax.Precision.HIGHEST

D_MODEL = 2048
DEPTH = 2
GRID_W = 64
ROPE_THETA = 10000.0
EPS = 1e-6

MLA_HEADS = 8
MLA_Q_LORA = 512
MLA_KV_LORA = 256
MLA_NOPE = 128
MLA_ROPE = 64
MLA_V = 128
MLA_QK = MLA_NOPE + MLA_ROPE
MLA_QK_PAD = 256
MLA_WIDTH = MLA_HEADS * MLA_V

GQA_HEADS = 8
GQA_KV_HEADS = 2
GQA_GROUP = GQA_HEADS // GQA_KV_HEADS
GQA_DIM = 128
GQA_WIDTH = GQA_HEADS * GQA_DIM
GQA_KV_WIDTH = GQA_KV_HEADS * GQA_DIM

SSD_INNER = D_MODEL
SSD_P = 64
SSD_HEADS = SSD_INNER // SSD_P
SSD_G = 4
SSD_E = SSD_HEADS // SSD_G
SSD_N = 128
SSD_CONV = 5
SSD_Q = 128
SSD_ROWS = 256
SSD_GW = SSD_E * SSD_P
SSD_CONV_DIM = SSD_INNER + 2 * SSD_G * SSD_N

N_BRANCH = 3
IN_SPLITS = (MLA_Q_LORA, MLA_KV_LORA, MLA_ROPE, MLA_WIDTH, GQA_WIDTH, GQA_KV_WIDTH, GQA_KV_WIDTH,
             GQA_WIDTH, SSD_INNER, SSD_CONV_DIM, 2 * SSD_HEADS, N_BRANCH * D_MODEL)
IN_NAMES = ('cq', 'ckv', 'kr', 'ga', 'gq', 'gk', 'gv', 'gb', 'z', 'xbc', 'dtr', 'mg')
IN_OFFSETS = dict(zip(IN_NAMES, np.concatenate([[0], np.cumsum(IN_SPLITS)[:-1]]).tolist()))
IN_WIDTHS = dict(zip(IN_NAMES, IN_SPLITS))
P_ORDER = ('mg', 'z', 'xbc', 'ga', 'gq', 'gb', 'cq', 'ckv', 'gk', 'gv', 'kr', 'dtr')
P_OFF = {}
_o = 0
for _n in P_ORDER:
    P_OFF[_n] = _o
    _o += IN_WIDTHS[_n]
P_USED = _o
P_TN = 512
P_WIDTH = -(-P_USED // P_TN) * P_TN

DEEPNORM_ALPHA = (2 * DEPTH) ** 0.25

VMEM_LIMIT = 56 * 2 ** 20


def _params(*sem):
    return pltpu.CompilerParams(dimension_semantics=sem, vmem_limit_bytes=VMEM_LIMIT)


def _tile(n, target, align=8):
    t = min(n, target)
    while t > align and (n % t or t % align):
        t -= align
    assert n % t == 0, (n, target)
    return t


def _silu(v):
    return v * jax.nn.sigmoid(v)


def _softplus(v):
    return jnp.maximum(v, 0.0) + jnp.log1p(jnp.exp(-jnp.abs(v)))


def _layer_norm(v):
    mu = jnp.mean(v, axis=-1, keepdims=True)
    vc = v - mu
    var = jnp.mean(vc * vc, axis=-1, keepdims=True)
    return vc * lax.rsqrt(var + EPS)


def _rms(v):
    return v * lax.rsqrt(jnp.mean(v * v, axis=-1, keepdims=True) + EPS)


def _mod_kernel(c_ref, w_ref, b_ref, o_ref):
    a = _silu(c_ref[...]).astype(BF16)
    o_ref[...] = jnp.dot(a, w_ref[...].astype(BF16), preferred_element_type=F32) + b_ref[...]


def _mod_rows(c_rows, w_mod, b_mod):
    r, d = c_rows.shape
    n = w_mod.shape[1]
    tn = 512
    return pl.pallas_call(
        _mod_kernel,
        grid=(n // tn,),
        in_specs=[pl.BlockSpec((r, d), lambda j: (0, 0)),
                  pl.BlockSpec((d, tn), lambda j: (0, j)),
                  pl.BlockSpec((1, tn), lambda j: (0, j))],
        out_specs=pl.BlockSpec((r, tn), lambda j: (0, j)),
        out_shape=jax.ShapeDtypeStruct((r, n), F32),
        compiler_params=_params("arbitrary"),
        name="mod_rows",
    )(c_rows, w_mod, b_mod.reshape(1, n))


def _residual_rows(ctx_ref, x_ref, ctx_tiles, ro):
    return jnp.where(pl.program_id(1) + ro < ctx_tiles, ctx_ref[0], x_ref[0])


def _residual_specs(tm, d, ctx_tiles, ro):
    return [pl.BlockSpec((1, tm, d), lambda bi, i: (bi, jnp.minimum(i + ro, ctx_tiles - 1), 0)),
            pl.BlockSpec((1, tm, d), lambda bi, i: (bi, jnp.maximum(i + ro - ctx_tiles, 0), 0))]


def _ln_mod_kernel(ctx_ref, x_ref, sh_ref, sc_ref, o_ref, *, ctx_tiles):
    y = _layer_norm(_residual_rows(ctx_ref, x_ref, ctx_tiles, 0))
    o_ref[0] = (y * (1.0 + sc_ref[0]) + sh_ref[0]).astype(o_ref.dtype)


def _mod_row_index(nb, ctx_tiles):
    return lambda b, i: jnp.where(i < ctx_tiles, nb, b)


def _ln_mod(ctx, x, tab, nb):
    b, seq, d = x.shape
    ctx_len = ctx.shape[1]
    tm = _tile(ctx_len, 256)
    ctx_tiles = ctx_len // tm
    row = _mod_row_index(nb, ctx_tiles)
    return pl.pallas_call(
        functools.partial(_ln_mod_kernel, ctx_tiles=ctx_tiles),
        grid=(b, (ctx_len + seq) // tm),
        in_specs=_residual_specs(tm, d, ctx_tiles, 0)
        + [pl.BlockSpec((1, 1, d), lambda bi, i: (row(bi, i), 0, 0)),
           pl.BlockSpec((1, 1, d), lambda bi, i: (row(bi, i), 0, 1))],
        out_specs=pl.BlockSpec((1, tm, d), lambda bi, i: (bi, i, 0)),
        out_shape=jax.ShapeDtypeStruct((b, ctx_len + seq, d), BF16),
        compiler_params=_params("parallel", "parallel"),
        name="ln_mod",
    )(ctx, x, tab, tab)


def _in_proj_kernel(x_ref, w_ref, o_ref, dt_ref, *, dt_tile, dt_col):
    acc = jnp.dot(x_ref[...], w_ref[...], preferred_element_type=F32)
    o_ref[...] = acc.astype(o_ref.dtype)

    @pl.when(pl.program_id(1) == dt_tile)
    def _():
        dt_ref[...] = acc[:, dt_col:dt_col + 128]


def _in_proj(xm2, wp, out_dtype):
    m, k = xm2.shape
    n = wp.shape[1]
    tm = _tile(m, 1024)
    tn = P_TN
    dt_tile, dt_col = divmod(P_OFF['kr'], tn)
    assert P_OFF['dtr'] == P_OFF['kr'] + 64 and dt_col % 128 == 0
    return pl.pallas_call(
        functools.partial(_in_proj_kernel, dt_tile=dt_tile, dt_col=dt_col),
        grid=(m // tm, n // tn),
        in_specs=[pl.BlockSpec((tm, k), lambda i, j: (i, 0)),
                  pl.BlockSpec((k, tn), lambda i, j: (0, j))],
        out_specs=[pl.BlockSpec((tm, tn), lambda i, j: (i, j)),
                   pl.BlockSpec((tm, 128), lambda i, j: (i, 0))],
        out_shape=[jax.ShapeDtypeStruct((m, n), out_dtype), jax.ShapeDtypeStruct((m, 128), F32)],
        compiler_params=_params("parallel", "arbitrary"),
        name="in_proj",
    )(xm2, wp)


def _mla_prep_kernel(cq_ref, ckv_ref, kr_ref, qn_ref, kvn_ref, wqm_ref, wqr_ref, wkv_ref,
                     cosq_ref, sinq_ref, ck_ref, s1_ref, s2_ref, qa_ref, ka_ref, va_ref):
    cqn = (_rms(cq_ref[0].astype(F32)) * qn_ref[...]).astype(BF16)
    qm = jnp.dot(cqn, wqm_ref[...], preferred_element_type=F32)
    qr = jnp.dot(cqn, wqr_ref[...], preferred_element_type=F32)
    cosq = cosq_ref[...]
    sinq = sinq_ref[...]
    for h in range(MLA_HEADS):
        sl = slice(h * MLA_QK_PAD, (h + 1) * MLA_QK_PAD)
        qa_ref[0, h] = (qm[:, sl] * cosq + qr[:, sl] * sinq).T.astype(qa_ref.dtype)
    ckvn = (_rms(ckv_ref[0].astype(F32)) * kvn_ref[...]).astype(BF16)
    kv = jnp.dot(ckvn, wkv_ref[...], preferred_element_type=F32)
    kr = kr_ref[0].astype(F32)
    kpe = (kr * ck_ref[...] + pltpu.roll(kr, 128 - 16, axis=1) * s1_ref[...]
           + pltpu.roll(kr, 16, axis=1) * s2_ref[...]).astype(ka_ref.dtype)
    for h in range(MLA_HEADS):
        base = h * (MLA_NOPE + MLA_V)
        ka_ref[0, h, :, 0:MLA_NOPE] = kv[:, base:base + MLA_NOPE].astype(ka_ref.dtype)
        ka_ref[0, h, :, MLA_NOPE:MLA_QK_PAD] = kpe
        va_ref[0, h] = kv[:, base + MLA_NOPE:base + MLA_NOPE + MLA_V].T.astype(va_ref.dtype)


def _mla_prep(p3, qn, kvn, wqm, wqr, wkv, tabs, ctx_len):
    b, t, _ = p3.shape
    tm = _tile(ctx_len, 256)
    cosq, sinq, ck, s1, s2 = tabs

    def col(name, width):
        blk = P_OFF[name] // width
        return pl.BlockSpec((1, tm, width), lambda bi, i: (bi, i, blk))

    def full(a):
        return pl.BlockSpec(a.shape, lambda bi, i: (0,) * a.ndim)

    def rows(a):
        return pl.BlockSpec((tm, a.shape[1]), lambda bi, i: (i, 0))

    hm = lambda w: pl.BlockSpec((1, MLA_HEADS, tm, w), lambda bi, i: (bi, 0, i, 0))
    return pl.pallas_call(
        _mla_prep_kernel,
        grid=(b, t // tm),
        in_specs=[col('cq', MLA_Q_LORA), col('ckv', MLA_KV_LORA), col('kr', 128),
                  full(qn), full(kvn), full(wqm), full(wqr), full(wkv),
                  rows(cosq), rows(sinq), rows(ck), rows(s1), rows(s2)],
        out_specs=[pl.BlockSpec((1, MLA_HEADS, MLA_QK_PAD, tm), lambda bi, i: (bi, 0, 0, i)),
                   hm(MLA_QK_PAD),
                   pl.BlockSpec((1, MLA_HEADS, MLA_V, tm), lambda bi, i: (bi, 0, 0, i))],
        out_shape=[jax.ShapeDtypeStruct((b, MLA_HEADS, MLA_QK_PAD, t), BF16),
                   jax.ShapeDtypeStruct((b, MLA_HEADS, t, MLA_QK_PAD), BF16),
                   jax.ShapeDtypeStruct((b, MLA_HEADS, MLA_V, t), BF16)],
        compiler_params=_params("parallel", "parallel"),
        name="mla_prep",
    )(p3, p3, p3, qn, kvn, wqm, wqr, wkv, cosq, sinq, ck, s1, s2)


def _rope128(y, c, s1, s2):
    return y * c + pltpu.roll(y, 128 - 32, axis=1) * s1 + pltpu.roll(y, 32, axis=1) * s2


def _gqa_prep_kernel(gq_ref, gk_ref, gv_ref, qn_ref, kn_ref, cq_ref, s1q_ref, s2q_ref,
                     ck_ref, s1k_ref, s2k_ref, qb_ref, kb_ref, vb_ref):
    gq = gq_ref[0].astype(F32)
    for h in range(GQA_HEADS):
        y = _rms(gq[:, h * GQA_DIM:(h + 1) * GQA_DIM]) * qn_ref[...]
        qb_ref[0, h] = _rope128(y, cq_ref[...], s1q_ref[...], s2q_ref[...]).T.astype(qb_ref.dtype)
    gk = gk_ref[0].astype(F32)
    gv = gv_ref[0].astype(F32)
    for h in range(GQA_KV_HEADS):
        y = _rms(gk[:, h * GQA_DIM:(h + 1) * GQA_DIM]) * kn_ref[...]
        kb_ref[0, h] = _rope128(y, ck_ref[...], s1k_ref[...], s2k_ref[...]).astype(kb_ref.dtype)
        vb_ref[0, h] = gv[:, h * GQA_DIM:(h + 1) * GQA_DIM].T.astype(vb_ref.dtype)


def _gqa_prep(p3, qn, kn, tabs, ctx_len):
    b, t, _ = p3.shape
    tm = _tile(ctx_len, 256)

    def col(name, width):
        blk = P_OFF[name] // width
        return pl.BlockSpec((1, tm, width), lambda bi, i: (bi, i, blk))

    def full(a):
        return pl.BlockSpec(a.shape, lambda bi, i: (0,) * a.ndim)

    def rows(a):
        return pl.BlockSpec((tm, a.shape[1]), lambda bi, i: (i, 0))

    hm = lambda nh: pl.BlockSpec((1, nh, tm, GQA_DIM), lambda bi, i: (bi, 0, i, 0))
    hmt = lambda nh: pl.BlockSpec((1, nh, GQA_DIM, tm), lambda bi, i: (bi, 0, 0, i))
    return pl.pallas_call(
        _gqa_prep_kernel,
        grid=(b, t // tm),
        in_specs=[col('gq', GQA_WIDTH), col('gk', GQA_KV_WIDTH), col('gv', GQA_KV_WIDTH),
                  full(qn), full(kn)] + [rows(a) for a in tabs],
        out_specs=[hmt(GQA_HEADS), hm(GQA_KV_HEADS), hmt(GQA_KV_HEADS)],
        out_shape=[jax.ShapeDtypeStruct((b, GQA_HEADS, GQA_DIM, t), BF16),
                   jax.ShapeDtypeStruct((b, GQA_KV_HEADS, t, GQA_DIM), BF16),
                   jax.ShapeDtypeStruct((b, GQA_KV_HEADS, GQA_DIM, t), BF16)],
        compiler_params=_params("parallel", "parallel"),
        name="gqa_prep",
    )(p3, p3, p3, qn, kn, *tabs)


ATTN_KC = 256
ATTN_QB = 128
LOG2E = math.log2(math.e)


def _attn_logits(qts, k_ref, s_buf, nkeys):
    nqb = qts[0].shape[1] // ATTN_QB
    for j, qt in enumerate(qts):
        st = jnp.dot(k_ref[0, j, 0:nkeys, :], qt, preferred_element_type=F32)
        for b in range(nqb):
            s_buf[j * nqb + b, 0:nkeys, :] = st[:, b * ATTN_QB:(b + 1) * ATTN_QB]


def _attn_softmax_slab(s_buf, p_buf, l_buf, b, nkeys):
    parts = [jnp.max(s_buf[b, k0:k0 + ATTN_KC, :].reshape(ATTN_KC // 64, 8, 8, ATTN_QB), axis=0)
             for k0 in range(0, nkeys, ATTN_KC)]
    while len(parts) > 1:
        parts = [jnp.maximum(parts[i], parts[i + 1]) if i + 1 < len(parts) else parts[i]
                 for i in range(0, len(parts), 2)]
    m = jnp.max(parts[0], axis=(0, 1), keepdims=True)[0]
    lacc = jnp.zeros((8, 8, ATTN_QB), F32)
    for k0 in range(0, nkeys, ATTN_KC):
        p = jnp.exp2(s_buf[b, k0:k0 + ATTN_KC, :] - m)
        lacc = lacc + jnp.sum(p.reshape(ATTN_KC // 64, 8, 8, ATTN_QB), axis=0)
        p_buf[b, k0:k0 + ATTN_KC, :] = p.astype(p_buf.dtype)
    l = jnp.sum(lacc, axis=(0, 1), keepdims=True)[0]
    l_buf[b] = jnp.broadcast_to(1.0 / l, (8, ATTN_QB))


def _attn_output(vt_ref, p_buf, l_buf, o_ref, nkeys, hps, group, tq):
    dv = vt_ref.shape[2]
    nqb = group * tq // ATTN_QB
    for j in range(hps):
        slabs = range(j * nqb, (j + 1) * nqb)
        pt = jnp.concatenate([p_buf[b, 0:nkeys, :] for b in slabs], axis=1)
        ot = jnp.dot(vt_ref[0, j, :, 0:nkeys], pt, preferred_element_type=F32)
        ot = ot * jnp.concatenate([l_buf[b, 0:1, :] for b in slabs], axis=1)
        for g in range(group):
            c0 = (j * group + g) * dv
            o_ref[0, :, c0:c0 + dv] = ot[:, g * tq:(g + 1) * tq].T.astype(o_ref.dtype)


def _attn_queries(qt_ref, hps, group):
    return [jnp.concatenate([qt_ref[0, j * group + g] for g in range(group)], axis=1)
            for j in range(hps)]


def _attn_ctx_kernel(qt_ref, k_ref, vt_ref, o_ref, s_sc, p_sc, l_sc, *, group, tq):
    hps, nkeys = k_ref.shape[1], k_ref.shape[2]
    nslab = hps * group * tq // ATTN_QB
    _attn_logits(_attn_queries(qt_ref, hps, group), k_ref, s_sc, nkeys)

    def slab(b, carry):
        _attn_softmax_slab(s_sc, p_sc, l_sc, b, nkeys)
        return carry

    lax.fori_loop(0, nslab, slab, 0)
    _attn_output(vt_ref, p_sc, l_sc, o_ref, nkeys, hps, group, tq)


def _attn_lat_kernel(qt_ref, k_ref, vt_ref, o_ref, s0, s1, p0, p1, l0, l1, *, group, tq):
    hps, nkeys = k_ref.shape[1], k_ref.shape[2]
    nslab = hps * group * tq // ATTN_QB
    g = pl.program_id(2)

    @pl.when((pl.program_id(0) == 0) & (pl.program_id(1) == 0) & (g == 0))
    def _():
        for buf in (s0, s1, p0, p1, l0, l1):
            buf[...] = jnp.zeros(buf.shape, buf.dtype)

    def step(s_a, s_b, p_b, p_c, l_b, l_c):
        _attn_logits(_attn_queries(qt_ref, hps, group), k_ref, s_a, nkeys)
        for b in range(nslab):
            _attn_softmax_slab(s_b, p_b, l_b, b, nkeys)
        _attn_output(vt_ref, p_c, l_c, o_ref, nkeys, hps, group, tq)

    pl.when(g % 2 == 0)(lambda: step(s0, s1, p1, p0, l1, l0))
    pl.when(g % 2 == 1)(lambda: step(s1, s0, p0, p1, l0, l1))


def _attention(qt, k, vt, ctx_len, tq, hps, with_ctx_queries, out_dtype):
    b, hq, dk, t = qt.shape
    hkv, dv = k.shape[1], vt.shape[2]
    group = hq // hkv
    tq = _tile(ctx_len, tq, 128)
    nslab = hps * group * tq // ATTN_QB
    ctx_tiles = ctx_len // tq
    nq = t // tq - ctx_tiles
    width = hps * group * dv
    assert t % ATTN_KC == 0 and ctx_len % ATTN_KC == 0 and hkv % hps == 0

    def scratch(nkeys):
        return [pltpu.VMEM((nslab, nkeys, ATTN_QB), F32), pltpu.VMEM((nslab, nkeys, ATTN_QB), BF16),
                pltpu.VMEM((nslab, 8, ATTN_QB), F32)]

    s_lat, p_lat, l_lat = scratch(t)
    y_lat = pl.pallas_call(
        functools.partial(_attn_lat_kernel, group=group, tq=tq),
        grid=(b, hkv // hps, nq + 2),
        in_specs=[pl.BlockSpec((1, hps * group, dk, tq),
                               lambda bi, h, i: (bi, h, 0, jnp.minimum(i, nq - 1) + ctx_tiles)),
                  pl.BlockSpec((1, hps, t, dk), lambda bi, h, i: (bi, h, 0, 0)),
                  pl.BlockSpec((1, hps, dv, t), lambda bi, h, i: (bi, h, 0, 0))],
        out_specs=pl.BlockSpec((1, tq, width), lambda bi, h, i: (bi, jnp.maximum(i - 2, 0), h)),
        out_shape=jax.ShapeDtypeStruct((b, nq * tq, hq * dv), out_dtype),
        scratch_shapes=[s_lat, s_lat, p_lat, p_lat, l_lat, l_lat],
        compiler_params=_params("arbitrary", "arbitrary", "arbitrary"),
        name="attention_lat_dk%d" % dk,
    )(qt, k, vt)
    if not with_ctx_queries:
        return y_lat
    y_ctx = pl.pallas_call(
        functools.partial(_attn_ctx_kernel, group=group, tq=tq),
        grid=(b, hkv // hps, ctx_tiles),
        in_specs=[pl.BlockSpec((1, hps * group, dk, tq), lambda bi, h, i: (bi, h, 0, i)),
                  pl.BlockSpec((1, hps, ctx_len, dk), lambda bi, h, i: (bi, h, 0, 0)),
                  pl.BlockSpec((1, hps, dv, ctx_len), lambda bi, h, i: (bi, h, 0, 0))],
        out_specs=pl.BlockSpec((1, tq, width), lambda bi, h, i: (bi, i, h)),
        out_shape=jax.ShapeDtypeStruct((b, ctx_len, hq * dv), out_dtype),
        scratch_shapes=scratch(ctx_len),
        compiler_params=_params("parallel", "parallel", "arbitrary"),
        name="attention_ctx_dk%d" % dk,
    )(qt, k, vt)
    return jnp.concatenate([y_ctx, y_lat], axis=1)


def _conv_kernel(x_ref, w_ref, b_ref, o_ref, pad_sc, *, ctx_len, rows):
    t = x_ref.shape[1]
    nch = x_ref.shape[2]
    halo = 8
    segs = ((0, ctx_len), (ctx_len, t))
    zeros = jnp.zeros((halo, nch), F32)
    for si, (lo, hi) in enumerate(segs):
        pad_sc[lo + si * halo:lo + (si + 1) * halo, :] = zeros
        for r0 in range(lo, hi, rows):
            pad_sc[r0 + (si + 1) * halo:r0 + (si + 1) * halo + rows, :] = x_ref[0, r0:r0 + rows, :].astype(F32)
    pad_sc[t + 2 * halo:t + 3 * halo, :] = zeros
    w = w_ref[...]
    bias = b_ref[...]
    for si, (lo, hi) in enumerate(segs):
        for r0 in range(lo, hi, rows):
            base = r0 + (si + 1) * halo - SSD_CONV // 2
            acc = bias + w[0:1, :] * pad_sc[base:base + rows, :]
            for kk in range(1, SSD_CONV):
                acc = acc + w[kk:kk + 1, :] * pad_sc[base + kk:base + kk + rows, :]
            o_ref[0, r0:r0 + rows, :] = _silu(acc).astype(o_ref.dtype)


def _conv(p3, conv_w, conv_b, ctx_len, out_dtype):
    b, t, _ = p3.shape
    nch = 256
    rows = _tile(ctx_len, 256)
    blk0 = P_OFF['xbc'] // nch
    kern = functools.partial(_conv_kernel, ctx_len=ctx_len, rows=rows)
    return pl.pallas_call(
        kern,
        grid=(b, SSD_CONV_DIM // nch),
        in_specs=[pl.BlockSpec((1, t, nch), lambda bi, j: (bi, 0, blk0 + j)),
                  pl.BlockSpec((SSD_CONV, nch), lambda bi, j: (0, j)),
                  pl.BlockSpec((1, nch), lambda bi, j: (0, j))],
        out_specs=pl.BlockSpec((1, t, nch), lambda bi, j: (bi, 0, j)),
        out_shape=jax.ShapeDtypeStruct((b, t, SSD_CONV_DIM), out_dtype),
        scratch_shapes=[pltpu.VMEM((t + 24, nch), F32)],
        compiler_params=_params("parallel", "parallel"),
        name="ssd_conv",
    )(p3, conv_w, conv_b.reshape(1, SSD_CONV_DIM))


def _split3(v):
    hi = v.astype(BF16)
    r1 = v - hi.astype(F32)
    mid = r1.astype(BF16)
    lo = (r1 - mid.astype(F32)).astype(BF16)
    return hi, mid, lo


def _expand_heads(col):
    q = col.shape[0]
    first = lax.broadcasted_iota(jnp.int32, (q, 2 * SSD_P), 1) < SSD_P
    tiles = []
    for e in range(0, SSD_E, 2):
        tiles.append(jnp.where(first, jnp.broadcast_to(col[:, e:e + 1], (q, 2 * SSD_P)),
                               jnp.broadcast_to(col[:, e + 1:e + 2], (q, 2 * SSD_P))))
    return jnp.concatenate(tiles, axis=1)


def _ssd_direction(xs, bm, cm, dt_col_raw, dt_row_raw, bias_col, bias_row, a_col, a_row, h_ref,
                   backward):
    q = xs.shape[0]
    ri = lax.broadcasted_iota(jnp.int32, (q, q), 0)
    ci = lax.broadcasted_iota(jnp.int32, (q, q), 1)
    low = ri >= ci
    keep = (ri <= ci) if backward else low
    tri_col = jnp.where(keep, 1.0, 0.0).astype(BF16)
    tri_row = jnp.where((ri >= ci) if backward else (ri <= ci), 1.0, 0.0).astype(BF16)

    dt_c = _softplus(dt_col_raw + bias_col)
    dt_r = _softplus(dt_row_raw + bias_row)
    cum_c = sum(jnp.dot(tri_col, part, preferred_element_type=F32) for part in _split3(dt_c * a_col))
    cum_r = sum(jnp.dot(part, tri_row, preferred_element_type=F32) for part in _split3(dt_r * a_row))
    cum_full = _expand_heads(cum_c)
    dt_full = _expand_heads(dt_c)
    total = cum_full[0:1, :] if backward else cum_full[q - 1:q, :]

    xdt = xs * dt_full
    cmb = cm.astype(BF16)
    cb = lax.dot_general(cmb, bm.astype(BF16), (((1,), (1,)), ((), ())), preferred_element_type=F32)
    h = h_ref[...]
    y_off = jnp.dot(cmb, h.astype(BF16), preferred_element_type=F32) * jnp.exp(cum_full)
    wgt = (xdt * jnp.exp(total - cum_full)).astype(BF16)
    h_ref[...] = jnp.exp(total) * h + jnp.dot(bm.T.astype(BF16), wgt, preferred_element_type=F32)

    xdt_b = xdt.astype(BF16)
    lane = lax.broadcasted_iota(jnp.int32, (q, 2 * SSD_P), 1)
    parts = []
    for pair in range(SSD_E // 2):
        x_pair = xdt_b[:, pair * 2 * SSD_P:(pair + 1) * 2 * SSD_P]
        ys = []
        for e in (2 * pair, 2 * pair + 1):
            seg = cum_c[:, e:e + 1] - cum_r[e:e + 1, :]
            dec = jnp.exp(jnp.where(keep, seg, -jnp.inf))
            ys.append(jnp.dot((cb * dec).astype(BF16), x_pair, preferred_element_type=F32))
        parts.append(jnp.where(lane < SSD_P, ys[0], ys[1]))
    return jnp.concatenate(parts, axis=1) + y_off


def _ssd_kernel(xf_ref, bf_ref, cf_ref, xb_ref, bb_ref, cb_ref, dcf_ref, drf_ref, dcb_ref, drb_ref,
                bias_c_ref, bias_r_ref, a_c_ref, a_r_ref, d_ref, yf_ref, yb_ref, hf_sc, hb_sc):
    @pl.when(pl.program_id(2) == 0)
    def _():
        hf_sc[...] = jnp.zeros(hf_sc.shape, F32)
        hb_sc[...] = jnp.zeros(hb_sc.shape, F32)

    nsub = xf_ref.shape[1] // SSD_Q
    for i in range(nsub):
        rf = slice(i * SSD_Q, (i + 1) * SSD_Q)
        rb = slice((nsub - 1 - i) * SSD_Q, (nsub - i) * SSD_Q)
        xs = xf_ref[0, rf, :].astype(F32)
        yf = _ssd_direction(xs, bf_ref[0, rf, :].astype(F32), cf_ref[0, rf, :].astype(F32),
                            dcf_ref[0, 0, 0, rf, :], drf_ref[0, 0, 0, :, rf],
                            bias_c_ref[0, 0], bias_r_ref[0, 0], a_c_ref[0, 0], a_r_ref[0, 0], hf_sc, False)
        yf_ref[0, rf, :] = (yf + d_ref[0] * xs).astype(yf_ref.dtype)
        yb = _ssd_direction(xb_ref[0, rb, :].astype(F32), bb_ref[0, rb, :].astype(F32),
                            cb_ref[0, rb, :].astype(F32), dcb_ref[0, 0, 0, rb, :], drb_ref[0, 0, 0, :, rb],
                            bias_c_ref[1, 0], bias_r_ref[1, 0], a_c_ref[1, 0], a_r_ref[1, 0], hb_sc, True)
        yb_ref[0, rb, :] = yb.astype(yb_ref.dtype)


def _ssd(xc, dt_col, dt_row, bias_c, bias_r, a_c, a_r, d_full, ctx_len, out_dtype):
    b, t, _ = xc.shape
    rows = _tile(ctx_len, SSD_ROWS, SSD_Q)
    nblk = t // rows
    nctx = ctx_len // rows
    bblk = SSD_INNER // SSD_N
    cblk = bblk + SSD_G

    def bidx(c):
        return jnp.where(c < nctx, nctx - 1 - c, nblk - 1 - (c - nctx))

    fx = lambda bi, g, c: (bi, c, g)
    fb = lambda bi, g, c: (bi, c, bblk + g)
    fc = lambda bi, g, c: (bi, c, cblk + g)
    bx = lambda bi, g, c: (bi, bidx(c), g)
    bb = lambda bi, g, c: (bi, bidx(c), bblk + g)
    bc = lambda bi, g, c: (bi, bidx(c), cblk + g)
    small = lambda a: pl.BlockSpec((2, 1) + a.shape[2:], lambda bi, g, c: (0, g, 0, 0))
    return pl.pallas_call(
        _ssd_kernel,
        grid=(b, SSD_G, nblk),
        in_specs=[pl.BlockSpec((1, rows, SSD_GW), fx), pl.BlockSpec((1, rows, SSD_N), fb),
                  pl.BlockSpec((1, rows, SSD_N), fc),
                  pl.BlockSpec((1, rows, SSD_GW), bx), pl.BlockSpec((1, rows, SSD_N), bb),
                  pl.BlockSpec((1, rows, SSD_N), bc),
                  pl.BlockSpec((1, 1, 1, rows, SSD_E), lambda bi, g, c: (bi, 0, g, c, 0)),
                  pl.BlockSpec((1, 1, 1, SSD_E, rows), lambda bi, g, c: (bi, 0, g, 0, c)),
                  pl.BlockSpec((1, 1, 1, rows, SSD_E), lambda bi, g, c: (bi, 1, g, bidx(c), 0)),
                  pl.BlockSpec((1, 1, 1, SSD_E, rows), lambda bi, g, c: (bi, 1, g, 0, bidx(c))),
                  small(bias_c), small(bias_r), small(a_c), small(a_r),
                  pl.BlockSpec((1, 1, SSD_GW), lambda bi, g, c: (g, 0, 0))],
        out_specs=[pl.BlockSpec((1, rows, SSD_GW), fx), pl.BlockSpec((1, rows, SSD_GW), bx)],
        out_shape=[jax.ShapeDtypeStruct((b, t, SSD_INNER), out_dtype)] * 2,
        scratch_shapes=[pltpu.VMEM((SSD_N, SSD_GW), F32), pltpu.VMEM((SSD_N, SSD_GW), F32)],
        compiler_params=_params("parallel", "parallel", "arbitrary"),
        name="ssd_scan",
    )(xc, xc, xc, xc, xc, xc, dt_col, dt_row, dt_col, dt_row, bias_c, bias_r, a_c, a_r, d_full)


MERGE_TN = 512


def _merge_kernel(ya_ref, ga_ref, yb_ref, gb_ref, yf_ref, ybk_ref, z_ref, nrm_ref,
                  mg_ref, wa_ref, wb_ref, wc_ref, u_ref, c_sc):
    a_in = (ya_ref[0].astype(F32) * _silu(ga_ref[0].astype(F32))).astype(BF16)
    b_in = (yb_ref[0].astype(F32) * _silu(gb_ref[0].astype(F32))).astype(BF16)
    v = (yf_ref[0].astype(F32) + ybk_ref[0].astype(F32)) * _silu(z_ref[0].astype(F32))
    for g in range(SSD_G):
        sl = slice(g * SSD_GW, (g + 1) * SSD_GW)
        c_sc[:, sl] = (_rms(v[:, sl]) * nrm_ref[:, sl]).astype(BF16)
    c_in = c_sc[...]
    d = u_ref.shape[-1]
    for j in range(0, d, MERGE_TN):
        sl = slice(j, j + MERGE_TN)
        br_a = jnp.dot(a_in, wa_ref[:, sl], preferred_element_type=F32)
        br_b = jnp.dot(b_in, wb_ref[:, sl], preferred_element_type=F32)
        br_c = jnp.dot(c_in, wc_ref[:, sl], preferred_element_type=F32)
        u = (jax.nn.sigmoid(mg_ref[0, :, j:j + MERGE_TN].astype(F32)) * br_a
             + jax.nn.sigmoid(mg_ref[0, :, d + j:d + j + MERGE_TN].astype(F32)) * br_b
             + jax.nn.sigmoid(mg_ref[0, :, 2 * d + j:2 * d + j + MERGE_TN].astype(F32)) * br_c)
        u_ref[0, :, sl] = u.astype(u_ref.dtype)


def _merge(ya, yb, yf, ybk, p3, ssd_norm, wa, wb, wc, row_off):
    b, rows, _ = ya.shape
    tm = _tile(rows, 256)
    d = D_MODEL
    ro = row_off // tm
    assert P_OFF['mg'] == 0

    def col(name, width):
        blk = P_OFF[name] // width
        return pl.BlockSpec((1, tm, width), lambda bi, i: (bi, i + ro, blk))

    loc = lambda w: pl.BlockSpec((1, tm, w), lambda bi, i: (bi, i, 0))
    glob = lambda w: pl.BlockSpec((1, tm, w), lambda bi, i: (bi, i + ro, 0))
    wspec = lambda k: pl.BlockSpec((k, d), lambda bi, i: (0, 0), pipeline_mode=pl.Buffered(1))
    return pl.pallas_call(
        _merge_kernel,
        grid=(b, rows // tm),
        in_specs=[loc(MLA_WIDTH), col('ga', MLA_WIDTH), loc(GQA_WIDTH), col('gb', GQA_WIDTH),
                  glob(SSD_INNER), glob(SSD_INNER), col('z', SSD_INNER),
                  pl.BlockSpec((1, SSD_INNER), lambda bi, i: (0, 0)),
                  glob(N_BRANCH * d), wspec(MLA_WIDTH), wspec(GQA_WIDTH), wspec(SSD_INNER)],
        out_specs=pl.BlockSpec((1, tm, d), lambda bi, i: (bi, i, 0)),
        out_shape=jax.ShapeDtypeStruct((b, rows, d), BF16),
        scratch_shapes=[pltpu.VMEM((tm, SSD_INNER), BF16)],
        compiler_params=_params("parallel", "parallel"),
        name="merge",
    )(ya, p3, yb, p3, yf, ybk, p3, ssd_norm.reshape(1, SSD_INNER), p3, wa, wb, wc)


def _out_ln_kernel(u_ref, w_ref, *rest, with_next, ctx_tiles, ro, split):
    if split:
        ctx_ref, x_ref, gate_ref, g_ref, b_ref, *rest = rest
        res = _residual_rows(ctx_ref, x_ref, ctx_tiles, ro)
    else:
        x_ref, gate_ref, g_ref, b_ref, *rest = rest
        res = x_ref[0]
    out = jnp.dot(u_ref[0], w_ref[...], preferred_element_type=F32)
    r = DEEPNORM_ALPHA * res + gate_ref[0] * out
    xn = _layer_norm(r) * g_ref[...] + b_ref[...]
    if with_next:
        sh_ref, sc_ref, xo_ref, xm_ref = rest
        xo_ref[0] = xn
        xm_ref[0] = (_layer_norm(xn) * (1.0 + sc_ref[0]) + sh_ref[0]).astype(xm_ref.dtype)
    else:
        (xo_ref,) = rest
        xo_ref[0] = xn


def _out_ln(u, w_out, xc, tab, ln_g, ln_b, next_tab, nb, ctx_len, row_off):
    b, rows, d = u.shape
    tm = _tile(ctx_len, 256)
    ro = row_off // tm
    ctx_tiles = ctx_len // tm
    row = _mod_row_index(nb, ctx_tiles)
    with_next = next_tab is not None
    split = isinstance(xc, tuple)
    loc = pl.BlockSpec((1, tm, d), lambda bi, i: (bi, i, 0))
    vec = pl.BlockSpec((1, d), lambda bi, i: (0, 0))
    res_specs = (_residual_specs(tm, d, ctx_tiles, ro) if split
                 else [pl.BlockSpec((1, tm, d), lambda bi, i: (bi, i + ro, 0))])
    in_specs = [loc, pl.BlockSpec((d, d), lambda bi, i: (0, 0))] + res_specs + [
        pl.BlockSpec((1, 1, d), lambda bi, i: (row(bi, i + ro), 0, 2)), vec, vec]
    args = [u, w_out] + (list(xc) if split else [xc]) + [tab, ln_g.reshape(1, d), ln_b.reshape(1, d)]
    out_specs = [loc]
    out_shape = [jax.ShapeDtypeStruct((b, rows, d), F32)]
    if with_next:
        in_specs += [pl.BlockSpec((1, 1, d), lambda bi, i: (row(bi, i + ro), 0, 0)),
                     pl.BlockSpec((1, 1, d), lambda bi, i: (row(bi, i + ro), 0, 1))]
        args += [next_tab, next_tab]
        out_specs.append(loc)
        out_shape.append(jax.ShapeDtypeStruct((b, rows, d), BF16))
    return pl.pallas_call(
        functools.partial(_out_ln_kernel, with_next=with_next, ctx_tiles=ctx_tiles, ro=ro, split=split),
        grid=(b, rows // tm),
        in_specs=in_specs,
        out_specs=out_specs,
        out_shape=out_shape,
        compiler_params=_params("parallel", "parallel"),
        name="out_ln",
    )(*args)


def _rope_angles(rows, dim):
    row, col = jnp.meshgrid(jnp.arange(rows, dtype=F32), jnp.arange(GRID_W, dtype=F32), indexing='ij')
    half = dim // 2
    inv_freq = ROPE_THETA ** (-jnp.arange(0, half, 2, dtype=F32) / half)
    ang_r = row.reshape(-1, 1) * inv_freq
    ang_c = col.reshape(-1, 1) * inv_freq
    return jnp.concatenate([ang_r, ang_r, ang_c, ang_c], axis=-1)


def _rope_tables(seq, ctx_len, dim):
    ang = _rope_angles(seq // GRID_W, dim)
    cos = jnp.concatenate([jnp.ones((ctx_len, dim), F32), jnp.cos(ang)], axis=0)
    sin = jnp.concatenate([jnp.zeros((ctx_len, dim), F32), jnp.sin(ang)], axis=0)
    return cos, sin


def _roll_tables(cos, sin, dim, scale):
    t = cos.shape[0]
    quarter = dim // 4
    first = (jnp.arange(dim) % (2 * quarter)) < quarter
    s1 = jnp.where(first, -sin, 0.0)
    s2 = jnp.where(first, 0.0, sin)
    pad = lambda a: jnp.pad(a * scale, ((0, 0), (0, 128 - dim)))
    return pad(cos), pad(s1), pad(s2)


def _rot_matrix(dim):
    quarter = dim // 4
    r = np.zeros((dim, dim), np.float32)
    for i in range(dim):
        blk = i // quarter
        if blk % 2 == 0:
            r[i + quarter, i] = -1.0
        else:
            r[i - quarter, i] = 1.0
    return jnp.asarray(r)


def _mla_q_weights(w_uq):
    w = w_uq.reshape(MLA_Q_LORA, MLA_HEADS, MLA_QK)
    nope, pe = w[..., :MLA_NOPE], w[..., MLA_NOPE:]
    pe_rot = jnp.einsum('khd,de->khe', pe, _rot_matrix(MLA_ROPE), precision=HIGHEST)
    zpad = jnp.zeros((MLA_Q_LORA, MLA_HEADS, MLA_QK_PAD - MLA_QK), F32)
    main = jnp.concatenate([nope, pe, zpad], axis=-1)
    rot = jnp.concatenate([jnp.zeros_like(nope), pe_rot, zpad], axis=-1)
    return (main.reshape(MLA_Q_LORA, -1).astype(BF16), rot.reshape(MLA_Q_LORA, -1).astype(BF16))


def _permute_w_in(w):
    parts = [w[:, IN_OFFSETS[n]:IN_OFFSETS[n] + IN_WIDTHS[n]] for n in P_ORDER]
    parts.append(jnp.zeros((w.shape[0], P_WIDTH - P_USED), w.dtype))
    return jnp.concatenate(parts, axis=1).astype(BF16)


P_DTYPE = BF16
Y_DTYPE = BF16


def kernel(x, c, ctx, c_ctx, w_mod, b_mod, w_in, mla_q_norm, mla_w_uq, mla_kv_norm, mla_w_ukv,
           gqa_q_norm, gqa_k_norm, ssd_conv_w, ssd_conv_b, ssd_a_log, ssd_dt_bias, ssd_d, ssd_norm,
           w_br_a, w_br_b, w_br_c, w_out, ln_g, ln_b):
    nb, seq, d = x.shape
    ctx_len = ctx.shape[1]
    t = ctx_len + seq
    depth = w_in.shape[0]
    assert d == D_MODEL and nb < 8 and seq % GRID_W == 0
    assert ctx_len % SSD_Q == 0 and seq % SSD_Q == 0

    cos_a, sin_a = _rope_tables(seq, ctx_len, MLA_ROPE)
    cos_b, sin_b = _rope_tables(seq, ctx_len, GQA_DIM)
    sq = MLA_QK ** -0.5 * LOG2E
    zq = jnp.zeros((t, MLA_QK_PAD - MLA_QK), F32)
    cosq = jnp.concatenate([jnp.full((t, MLA_NOPE), sq, F32), cos_a * sq, zq], axis=1)
    sinq = jnp.concatenate([jnp.zeros((t, MLA_NOPE), F32), sin_a * sq, zq], axis=1)
    mla_tabs = (cosq, sinq) + _roll_tables(cos_a, sin_a, MLA_ROPE, 1.0)
    gqa_tabs = (_roll_tables(cos_b, sin_b, GQA_DIM, GQA_DIM ** -0.5 * LOG2E)
                + _roll_tables(cos_b, sin_b, GQA_DIM, 1.0))

    c_rows = jnp.zeros((8, d), F32).at[:nb].set(c).at[nb].set(c_ctx)
    tabs = [_mod_rows(c_rows, w_mod[l], b_mod[l]).reshape(8, 1, 3 * d) for l in range(depth)]

    xc = (ctx, x)
    xm = _ln_mod(ctx, x, tabs[0], nb)

    for l in range(depth):
        last = l == depth - 1
        wp = _permute_w_in(w_in[l])
        p2, krdt = _in_proj(xm.reshape(nb * t, d), wp, P_DTYPE)
        p3 = p2.reshape(nb, t, P_WIDTH)

        wqm, wqr = _mla_q_weights(mla_w_uq[l])
        qa, ka, va = _mla_prep(p3, mla_q_norm[l].reshape(1, -1), mla_kv_norm[l].reshape(1, -1),
                               wqm, wqr, mla_w_ukv[l].astype(BF16), mla_tabs, ctx_len)
        ya = _attention(qa, ka, va, ctx_len, 256, 2, not last, Y_DTYPE)
        qb, kb, vb = _gqa_prep(p3, gqa_q_norm[l].reshape(1, -1), gqa_k_norm[l].reshape(1, -1),
                               gqa_tabs, ctx_len)
        yb = _attention(qb, kb, vb, ctx_len, 128, 1, not last, Y_DTYPE)

        xconv = _conv(p3, ssd_conv_w[l], ssd_conv_b[l], ctx_len, Y_DTYPE)
        dt5 = krdt[:, 64:].reshape(nb, t, 2, SSD_G, SSD_E)
        dt_col = jnp.transpose(dt5, (0, 2, 3, 1, 4))
        dt_row = jnp.transpose(dt5, (0, 2, 3, 4, 1))
        bias = ssd_dt_bias[l].astype(F32).reshape(2, SSD_G, SSD_E)
        a = -jnp.exp(ssd_a_log[l].astype(F32)).reshape(2, SSD_G, SSD_E)
        d_full = jnp.repeat(ssd_d[l].astype(F32), SSD_P).reshape(SSD_G, 1, SSD_GW)
        yf, ybk = _ssd(xconv, dt_col, dt_row, bias[:, :, None, :], bias[:, :, :, None],
                       a[:, :, None, :], a[:, :, :, None], d_full, ctx_len, Y_DTYPE)

        row_off = ctx_len if last else 0
        u = _merge(ya, yb, yf, ybk, p3, ssd_norm[l], w_br_a[l].astype(BF16), w_br_b[l].astype(BF16),
                   w_br_c[l].astype(BF16), row_off)
        if last:
            (xo,) = _out_ln(u, w_out[l].astype(BF16), xc, tabs[l], ln_g[l], ln_b[l], None, nb,
                            ctx_len, row_off)
            return xo
        xc, xm = _out_ln(u, w_out[l].astype(BF16), xc, tabs[l], ln_g[l], ln_b[l], tabs[l + 1], nb,
                         ctx_len, row_off)
```

```python
import functools
import math

import numpy as np
import jax
import jax.numpy as jnp
from jax import lax
from jax.experimental import pallas as pl
from jax.experimental.pallas import tpu as pltpu

F32 = jnp.float32
BF16 = jnp.bfloat16
HIGHEST = lax.Precision.HIGHEST

D_MODEL = 2048
DEPTH = 2
GRID_W = 64
ROPE_THETA = 10000.0
EPS = 1e-6

MLA_HEADS = 8
MLA_Q_LORA = 512
MLA_KV_LORA = 256
MLA_NOPE = 128
MLA_ROPE = 64
MLA_V = 128
MLA_QK = MLA_NOPE + MLA_ROPE
MLA_QK_PAD = 256
MLA_WIDTH = MLA_HEADS * MLA_V

GQA_HEADS = 8
GQA_KV_HEADS = 2
GQA_GROUP = GQA_HEADS // GQA_KV_HEADS
GQA_DIM = 128
GQA_WIDTH = GQA_HEADS * GQA_DIM
GQA_KV_WIDTH = GQA_KV_HEADS * GQA_DIM

SSD_INNER = D_MODEL
SSD_P = 64
SSD_HEADS = SSD_INNER // SSD_P
SSD_G = 4
SSD_E = SSD_HEADS // SSD_G
SSD_N = 128
SSD_CONV = 5
SSD_Q = 128
SSD_ROWS = 256
SSD_GW = SSD_E * SSD_P
SSD_CONV_DIM = SSD_INNER + 2 * SSD_G * SSD_N

N_BRANCH = 3
IN_SPLITS = (MLA_Q_LORA, MLA_KV_LORA, MLA_ROPE, MLA_WIDTH, GQA_WIDTH, GQA_KV_WIDTH, GQA_KV_WIDTH,
             GQA_WIDTH, SSD_INNER, SSD_CONV_DIM, 2 * SSD_HEADS, N_BRANCH * D_MODEL)
IN_NAMES = ('cq', 'ckv', 'kr', 'ga', 'gq', 'gk', 'gv', 'gb', 'z', 'xbc', 'dtr', 'mg')
IN_OFFSETS = dict(zip(IN_NAMES, np.concatenate([[0], np.cumsum(IN_SPLITS)[:-1]]).tolist()))
IN_WIDTHS = dict(zip(IN_NAMES, IN_SPLITS))
P_ORDER = ('mg', 'z', 'xbc', 'ga', 'gq', 'gb', 'cq', 'ckv', 'gk', 'gv', 'kr', 'dtr')
P_OFF = {}
_o = 0
for _n in P_ORDER:
    P_OFF[_n] = _o
    _o += IN_WIDTHS[_n]
P_USED = _o
P_TN = 512
P_WIDTH = -(-P_USED // P_TN) * P_TN

DEEPNORM_ALPHA = (2 * DEPTH) ** 0.25

VMEM_LIMIT = 56 * 2 ** 20


def _params(*sem):
    return pltpu.CompilerParams(dimension_semantics=sem, vmem_limit_bytes=VMEM_LIMIT)


def _tile(n, target, align=8):
    t = min(n, target)
    while t > align and (n % t or t % align):
        t -= align
    assert n % t == 0, (n, target)
    return t


def _silu(v):
    return v * jax.nn.sigmoid(v)


def _softplus(v):
    return jnp.maximum(v, 0.0) + jnp.log1p(jnp.exp(-jnp.abs(v)))


def _layer_norm(v):
    mu = jnp.mean(v, axis=-1, keepdims=True)
    vc = v - mu
    var = jnp.mean(vc * vc, axis=-1, keepdims=True)
    return vc * lax.rsqrt(var + EPS)


def _rms(v):
    return v * lax.rsqrt(jnp.mean(v * v, axis=-1, keepdims=True) + EPS)


def _mod_kernel(c_ref, w_ref, b_ref, o_ref):
    a = _silu(c_ref[...]).astype(BF16)
    o_ref[...] = jnp.dot(a, w_ref[...].astype(BF16), preferred_element_type=F32) + b_ref[...]


def _mod_rows(c_rows, w_mod, b_mod):
    r, d = c_rows.shape
    n = w_mod.shape[1]
    tn = 512
    return pl.pallas_call(
        _mod_kernel,
        grid=(n // tn,),
        in_specs=[pl.BlockSpec((r, d), lambda j: (0, 0)),
                  pl.BlockSpec((d, tn), lambda j: (0, j)),
                  pl.BlockSpec((1, tn), lambda j: (0, j))],
        out_specs=pl.BlockSpec((r, tn), lambda j: (0, j)),
        out_shape=jax.ShapeDtypeStruct((r, n), F32),
        compiler_params=_params("arbitrary"),
        name="mod_rows",
    )(c_rows, w_mod, b_mod.reshape(1, n))


def _residual_rows(ctx_ref, x_ref, ctx_tiles, ro):
    return jnp.where(pl.program_id(1) + ro < ctx_tiles, ctx_ref[0], x_ref[0])


def _residual_specs(tm, d, ctx_tiles, ro):
    return [pl.BlockSpec((1, tm, d), lambda bi, i: (bi, jnp.minimum(i + ro, ctx_tiles - 1), 0)),
            pl.BlockSpec((1, tm, d), lambda bi, i: (bi, jnp.maximum(i + ro - ctx_tiles, 0), 0))]


def _ln_mod_kernel(ctx_ref, x_ref, sh_ref, sc_ref, o_ref, *, ctx_tiles):
    y = _layer_norm(_residual_rows(ctx_ref, x_ref, ctx_tiles, 0))
    o_ref[0] = (y * (1.0 + sc_ref[0]) + sh_ref[0]).astype(o_ref.dtype)


def _mod_row_index(nb, ctx_tiles):
    return lambda b, i: jnp.where(i < ctx_tiles, nb, b)


def _ln_mod(ctx, x, tab, nb):
    b, seq, d = x.shape
    ctx_len = ctx.shape[1]
    tm = _tile(ctx_len, 256)
    ctx_tiles = ctx_len // tm
    row = _mod_row_index(nb, ctx_tiles)
    return pl.pallas_call(
        functools.partial(_ln_mod_kernel, ctx_tiles=ctx_tiles),
        grid=(b, (ctx_len + seq) // tm),
        in_specs=_residual_specs(tm, d, ctx_tiles, 0)
        + [pl.BlockSpec((1, 1, d), lambda bi, i: (row(bi, i), 0, 0)),
           pl.BlockSpec((1, 1, d), lambda bi, i: (row(bi, i), 0, 1))],
        out_specs=pl.BlockSpec((1, tm, d), lambda bi, i: (bi, i, 0)),
        out_shape=jax.ShapeDtypeStruct((b, ctx_len + seq, d), BF16),
        compiler_params=_params("parallel", "parallel"),
        name="ln_mod",
    )(ctx, x, tab, tab)


def _in_proj_kernel(x_ref, w_ref, o_ref, dt_ref, *, dt_tile, dt_col):
    acc = jnp.dot(x_ref[...], w_ref[...], preferred_element_type=F32)
    o_ref[...] = acc.astype(o_ref.dtype)

    @pl.when(pl.program_id(1) == dt_tile)
    def _():
        dt_ref[...] = acc[:, dt_col:dt_col + 128]


def _in_proj(xm2, wp, out_dtype):
    m, k = xm2.shape
    n = wp.shape[1]
    tm = _tile(m, 1024)
    tn = P_TN
    dt_tile, dt_col = divmod(P_OFF['kr'], tn)
    assert P_OFF['dtr'] == P_OFF['kr'] + 64 and dt_col % 128 == 0
    return pl.pallas_call(
        functools.partial(_in_proj_kernel, dt_tile=dt_tile, dt_col=dt_col),
        grid=(m // tm, n // tn),
        in_specs=[pl.BlockSpec((tm, k), lambda i, j: (i, 0)),
                  pl.BlockSpec((k, tn), lambda i, j: (0, j))],
        out_specs=[pl.BlockSpec((tm, tn), lambda i, j: (i, j)),
                   pl.BlockSpec((tm, 128), lambda i, j: (i, 0))],
        out_shape=[jax.ShapeDtypeStruct((m, n), out_dtype), jax.ShapeDtypeStruct((m, 128), F32)],
        compiler_params=_params("parallel", "arbitrary"),
        name="in_proj",
    )(xm2, wp)


def _mla_prep_kernel(cq_ref, ckv_ref, kr_ref, qn_ref, kvn_ref, wqm_ref, wqr_ref, wkv_ref,
                     cosq_ref, sinq_ref, ck_ref, s1_ref, s2_ref, qa_ref, ka_ref, va_ref):
    cqn = (_rms(cq_ref[0].astype(F32)) * qn_ref[...]).astype(BF16)
    qm = jnp.dot(cqn, wqm_ref[...], preferred_element_type=F32)
    qr = jnp.dot(cqn, wqr_ref[...], preferred_element_type=F32)
    cosq = cosq_ref[...]
    sinq = sinq_ref[...]
    for h in range(MLA_HEADS):
        sl = slice(h * MLA_QK_PAD, (h + 1) * MLA_QK_PAD)
        qa_ref[0, h] = (qm[:, sl] * cosq + qr[:, sl] * sinq).T.astype(qa_ref.dtype)
    ckvn = (_rms(ckv_ref[0].astype(F32)) * kvn_ref[...]).astype(BF16)
    kv = jnp.dot(ckvn, wkv_ref[...], preferred_element_type=F32)
    kr = kr_ref[0].astype(F32)
    kpe = (kr * ck_ref[...] + pltpu.roll(kr, 128 - 16, axis=1) * s1_ref[...]
           + pltpu.roll(kr, 16, axis=1) * s2_ref[...]).astype(ka_ref.dtype)
    for h in range(MLA_HEADS):
        base = h * (MLA_NOPE + MLA_V)
        ka_ref[0, h, :, 0:MLA_NOPE] = kv[:, base:base + MLA_NOPE].astype(ka_ref.dtype)
        ka_ref[0, h, :, MLA_NOPE:MLA_QK_PAD] = kpe
        va_ref[0, h] = kv[:, base + MLA_NOPE:base + MLA_NOPE + MLA_V].T.astype(va_ref.dtype)


def _mla_prep(p3, qn, kvn, wqm, wqr, wkv, tabs, ctx_len):
    b, t, _ = p3.shape
    tm = _tile(ctx_len, 256)
    cosq, sinq, ck, s1, s2 = tabs

    def col(name, width):
        blk = P_OFF[name] // width
        return pl.BlockSpec((1, tm, width), lambda bi, i: (bi, i, blk))

    def full(a):
        return pl.BlockSpec(a.shape, lambda bi, i: (0,) * a.ndim)

    def rows(a):
        return pl.BlockSpec((tm, a.shape[1]), lambda bi, i: (i, 0))

    hm = lambda w: pl.BlockSpec((1, MLA_HEADS, tm, w), lambda bi, i: (bi, 0, i, 0))
    return pl.pallas_call(
        _mla_prep_kernel,
        grid=(b, t // tm),
        in_specs=[col('cq', MLA_Q_LORA), col('ckv', MLA_KV_LORA), col('kr', 128),
                  full(qn), full(kvn), full(wqm), full(wqr), full(wkv),
                  rows(cosq), rows(sinq), rows(ck), rows(s1), rows(s2)],
        out_specs=[pl.BlockSpec((1, MLA_HEADS, MLA_QK_PAD, tm), lambda bi, i: (bi, 0, 0, i)),
                   hm(MLA_QK_PAD),
                   pl.BlockSpec((1, MLA_HEADS, MLA_V, tm), lambda bi, i: (bi, 0, 0, i))],
        out_shape=[jax.ShapeDtypeStruct((b, MLA_HEADS, MLA_QK_PAD, t), BF16),
                   jax.ShapeDtypeStruct((b, MLA_HEADS, t, MLA_QK_PAD), BF16),
                   jax.ShapeDtypeStruct((b, MLA_HEADS, MLA_V, t), BF16)],
        compiler_params=_params("parallel", "parallel"),
        name="mla_prep",
    )(p3, p3, p3, qn, kvn, wqm, wqr, wkv, cosq, sinq, ck, s1, s2)


def _rope128(y, c, s1, s2):
    return y * c + pltpu.roll(y, 128 - 32, axis=1) * s1 + pltpu.roll(y, 32, axis=1) * s2


def _gqa_prep_kernel(gq_ref, gk_ref, gv_ref, qn_ref, kn_ref, cq_ref, s1q_ref, s2q_ref,
                     ck_ref, s1k_ref, s2k_ref, qb_ref, kb_ref, vb_ref):
    gq = gq_ref[0].astype(F32)
    for h in range(GQA_HEADS):
        y = _rms(gq[:, h * GQA_DIM:(h + 1) * GQA_DIM]) * qn_ref[...]
        qb_ref[0, h] = _rope128(y, cq_ref[...], s1q_ref[...], s2q_ref[...]).T.astype(qb_ref.dtype)
    gk = gk_ref[0].astype(F32)
    gv = gv_ref[0].astype(F32)
    for h in range(GQA_KV_HEADS):
        y = _rms(gk[:, h * GQA_DIM:(h + 1) * GQA_DIM]) * kn_ref[...]
        kb_ref[0, h] = _rope128(y, ck_ref[...], s1k_ref[...], s2k_ref[...]).astype(kb_ref.dtype)
        vb_ref[0, h] = gv[:, h * GQA_DIM:(h + 1) * GQA_DIM].T.astype(vb_ref.dtype)


def _gqa_prep(p3, qn, kn, tabs, ctx_len):
    b, t, _ = p3.shape
    tm = _tile(ctx_len, 256)

    def col(name, width):
        blk = P_OFF[name] // width
        return pl.BlockSpec((1, tm, width), lambda bi, i: (bi, i, blk))

    def full(a):
        return pl.BlockSpec(a.shape, lambda bi, i: (0,) * a.ndim)

    def rows(a):
        return pl.BlockSpec((tm, a.shape[1]), lambda bi, i: (i, 0))

    hm = lambda nh: pl.BlockSpec((1, nh, tm, GQA_DIM), lambda bi, i: (bi, 0, i, 0))
    hmt = lambda nh: pl.BlockSpec((1, nh, GQA_DIM, tm), lambda bi, i: (bi, 0, 0, i))
    return pl.pallas_call(
        _gqa_prep_kernel,
        grid=(b, t // tm),
        in_specs=[col('gq', GQA_WIDTH), col('gk', GQA_KV_WIDTH), col('gv', GQA_KV_WIDTH),
                  full(qn), full(kn)] + [rows(a) for a in tabs],
        out_specs=[hmt(GQA_HEADS), hm(GQA_KV_HEADS), hmt(GQA_KV_HEADS)],
        out_shape=[jax.ShapeDtypeStruct((b, GQA_HEADS, GQA_DIM, t), BF16),
                   jax.ShapeDtypeStruct((b, GQA_KV_HEADS, t, GQA_DIM), BF16),
                   jax.ShapeDtypeStruct((b, GQA_KV_HEADS, GQA_DIM, t), BF16)],
        compiler_params=_params("parallel", "parallel"),
        name="gqa_prep",
    )(p3, p3, p3, qn, kn, *tabs)


ATTN_KC = 256
ATTN_QB = 128
LOG2E = math.log2(math.e)


def _attn_logits(qts, k_ref, s_buf, nkeys):
    nqb = qts[0].shape[1] // ATTN_QB
    for j, qt in enumerate(qts):
        st = jnp.dot(k_ref[0, j, 0:nkeys, :], qt, preferred_element_type=F32)
        for b in range(nqb):
            s_buf[j * nqb + b, 0:nkeys, :] = st[:, b * ATTN_QB:(b + 1) * ATTN_QB]


def _attn_softmax_slab(s_buf, p_buf, l_buf, b, nkeys):
    parts = [jnp.max(s_buf[b, k0:k0 + ATTN_KC, :].reshape(ATTN_KC // 64, 8, 8, ATTN_QB), axis=0)
             for k0 in range(0, nkeys, ATTN_KC)]
    while len(parts) > 1:
        parts = [jnp.maximum(parts[i], parts[i + 1]) if i + 1 < len(parts) else parts[i]
                 for i in range(0, len(parts), 2)]
    m = jnp.max(parts[0], axis=(0, 1), keepdims=True)[0]
    lacc = jnp.zeros((8, 8, ATTN_QB), F32)
    for k0 in range(0, nkeys, ATTN_KC):
        p = jnp.exp2(s_buf[b, k0:k0 + ATTN_KC, :] - m)
        lacc = lacc + jnp.sum(p.reshape(ATTN_KC // 64, 8, 8, ATTN_QB), axis=0)
        p_buf[b, k0:k0 + ATTN_KC, :] = p.astype(p_buf.dtype)
    l = jnp.sum(lacc, axis=(0, 1), keepdims=True)[0]
    l_buf[b] = jnp.broadcast_to(1.0 / l, (8, ATTN_QB))


def _attn_output(vt_ref, p_buf, l_buf, o_ref, nkeys, hps, group, tq):
    dv = vt_ref.shape[2]
    nqb = group * tq // ATTN_QB
    for j in range(hps):
        slabs = range(j * nqb, (j + 1) * nqb)
        pt = jnp.concatenate([p_buf[b, 0:nkeys, :] for b in slabs], axis=1)
        ot = jnp.dot(vt_ref[0, j, :, 0:nkeys], pt, preferred_element_type=F32)
        ot = ot * jnp.concatenate([l_buf[b, 0:1, :] for b in slabs], axis=1)
        for g in range(group):
            c0 = (j * group + g) * dv
            o_ref[0, :, c0:c0 + dv] = ot[:, g * tq:(g + 1) * tq].T.astype(o_ref.dtype)


def _attn_queries(qt_ref, hps, group):
    return [jnp.concatenate([qt_ref[0, j * group + g] for g in range(group)], axis=1)
            for j in range(hps)]


def _attn_ctx_kernel(qt_ref, k_ref, vt_ref, o_ref, s_sc, p_sc, l_sc, *, group, tq):
    hps, nkeys = k_ref.shape[1], k_ref.shape[2]
    nslab = hps * group * tq // ATTN_QB
    _attn_logits(_attn_queries(qt_ref, hps, group), k_ref, s_sc, nkeys)

    def slab(b, carry):
        _attn_softmax_slab(s_sc, p_sc, l_sc, b, nkeys)
        return carry

    lax.fori_loop(0, nslab, slab, 0)
    _attn_output(vt_ref, p_sc, l_sc, o_ref, nkeys, hps, group, tq)


def _attn_lat_kernel(qt_ref, k_ref, vt_ref, o_ref, s0, s1, p0, p1, l0, l1, *, group, tq):
    hps, nkeys = k_ref.shape[1], k_ref.shape[2]
    nslab = hps * group * tq // ATTN_QB
    g = pl.program_id(0)

    @pl.when(g == 0)
    def _():
        for buf in (s0, s1, p0, p1, l0, l1):
            buf[...] = jnp.zeros(buf.shape, buf.dtype)

    def step(s_a, s_b, p_b, p_c, l_b, l_c):
        _attn_logits(_attn_queries(qt_ref, hps, group), k_ref, s_a, nkeys)
        for b in range(nslab):
            _attn_softmax_slab(s_b, p_b, l_b, b, nkeys)
        _attn_output(vt_ref, p_c, l_c, o_ref, nkeys, hps, group, tq)

    pl.when(g % 2 == 0)(lambda: step(s0, s1, p1, p0, l1, l0))
    pl.when(g % 2 == 1)(lambda: step(s1, s0, p0, p1, l0, l1))


def _attention(qt, k, vt, ctx_len, tq, hps, with_ctx_queries, out_dtype):
    b, hq, dk, t = qt.shape
    hkv, dv = k.shape[1], vt.shape[2]
    group = hq // hkv
    tq = _tile(ctx_len, tq, 128)
    nslab = hps * group * tq // ATTN_QB
    ctx_tiles = ctx_len // tq
    nq = t // tq - ctx_tiles
    width = hps * group * dv
    assert t % ATTN_KC == 0 and ctx_len % ATTN_KC == 0 and hkv % hps == 0

    def scratch(nkeys):
        return [pltpu.VMEM((nslab, nkeys, ATTN_QB), F32), pltpu.VMEM((nslab, nkeys, ATTN_QB), BF16),
                pltpu.VMEM((nslab, 8, ATTN_QB), F32)]

    s_lat, p_lat, l_lat = scratch(t)
    nh = hkv // hps
    items = b * nh * nq

    def item(step):
        i = jnp.clip(step, 0, items - 1)
        return i // (nh * nq), (i // nq) % nh, i % nq

    def q_map(g):
        bi, h, qi = item(g)
        return bi, h, 0, qi + ctx_tiles

    def k_map(g):
        bi, h, _ = item(g)
        return bi, h, 0, 0

    def v_map(g):
        bi, h, _ = item(g - 2)
        return bi, h, 0, 0

    def o_map(g):
        bi, h, qi = item(g - 2)
        return bi, qi, h

    y_lat = pl.pallas_call(
        functools.partial(_attn_lat_kernel, group=group, tq=tq),
        grid=(items + 2,),
        in_specs=[pl.BlockSpec((1, hps * group, dk, tq), q_map),
                  pl.BlockSpec((1, hps, t, dk), k_map),
                  pl.BlockSpec((1, hps, dv, t), v_map)],
        out_specs=pl.BlockSpec((1, tq, width), o_map),
        out_shape=jax.ShapeDtypeStruct((b, nq * tq, hq * dv), out_dtype),
        scratch_shapes=[s_lat, s_lat, p_lat, p_lat, l_lat, l_lat],
        compiler_params=_params("arbitrary"),
        name="attention_lat_dk%d" % dk,
    )(qt, k, vt)
    if not with_ctx_queries:
        return y_lat
    y_ctx = pl.pallas_call(
        functools.partial(_attn_ctx_kernel, group=group, tq=tq),
        grid=(b, hkv // hps, ctx_tiles),
        in_specs=[pl.BlockSpec((1, hps * group, dk, tq), lambda bi, h, i: (bi, h, 0, i)),
                  pl.BlockSpec((1, hps, ctx_len, dk), lambda bi, h, i: (bi, h, 0, 0)),
                  pl.BlockSpec((1, hps, dv, ctx_len), lambda bi, h, i: (bi, h, 0, 0))],
        out_specs=pl.BlockSpec((1, tq, width), lambda bi, h, i: (bi, i, h)),
        out_shape=jax.ShapeDtypeStruct((b, ctx_len, hq * dv), out_dtype),
        scratch_shapes=scratch(ctx_len),
        compiler_params=_params("parallel", "parallel", "arbitrary"),
        name="attention_ctx_dk%d" % dk,
    )(qt, k, vt)
    return jnp.concatenate([y_ctx, y_lat], axis=1)


def _conv_kernel(x_ref, w_ref, b_ref, o_ref, pad_sc, *, ctx_len, rows):
    t = x_ref.shape[1]
    nch = x_ref.shape[2]
    halo = 8
    segs = ((0, ctx_len), (ctx_len, t))
    zeros = jnp.zeros((halo, nch), F32)
    for si, (lo, hi) in enumerate(segs):
        pad_sc[lo + si * halo:lo + (si + 1) * halo, :] = zeros
        for r0 in range(lo, hi, rows):
            pad_sc[r0 + (si + 1) * halo:r0 + (si + 1) * halo + rows, :] = x_ref[0, r0:r0 + rows, :].astype(F32)
    pad_sc[t + 2 * halo:t + 3 * halo, :] = zeros
    w = w_ref[...]
    bias = b_ref[...]
    for si, (lo, hi) in enumerate(segs):
        for r0 in range(lo, hi, rows):
            base = r0 + (si + 1) * halo - SSD_CONV // 2
            acc = bias + w[0:1, :] * pad_sc[base:base + rows, :]
            for kk in range(1, SSD_CONV):
                acc = acc + w[kk:kk + 1, :] * pad_sc[base + kk:base + kk + rows, :]
            o_ref[0, r0:r0 + rows, :] = _silu(acc).astype(o_ref.dtype)


def _conv(p3, conv_w, conv_b, ctx_len, out_dtype):
    b, t, _ = p3.shape
    nch = 256
    rows = _tile(ctx_len, 256)
    blk0 = P_OFF['xbc'] // nch
    kern = functools.partial(_conv_kernel, ctx_len=ctx_len, rows=rows)
    return pl.pallas_call(
        kern,
        grid=(b, SSD_CONV_DIM // nch),
        in_specs=[pl.BlockSpec((1, t, nch), lambda bi, j: (bi, 0, blk0 + j)),
                  pl.BlockSpec((SSD_CONV, nch), lambda bi, j: (0, j)),
                  pl.BlockSpec((1, nch), lambda bi, j: (0, j))],
        out_specs=pl.BlockSpec((1, t, nch), lambda bi, j: (bi, 0, j)),
        out_shape=jax.ShapeDtypeStruct((b, t, SSD_CONV_DIM), out_dtype),
        scratch_shapes=[pltpu.VMEM((t + 24, nch), F32)],
        compiler_params=_params("parallel", "parallel"),
        name="ssd_conv",
    )(p3, conv_w, conv_b.reshape(1, SSD_CONV_DIM))


def _split3(v):
    hi = v.astype(BF16)
    r1 = v - hi.astype(F32)
    mid = r1.astype(BF16)
    lo = (r1 - mid.astype(F32)).astype(BF16)
    return hi, mid, lo


def _expand_heads(col):
    q = col.shape[0]
    first = lax.broadcasted_iota(jnp.int32, (q, 2 * SSD_P), 1) < SSD_P
    tiles = []
    for e in range(0, SSD_E, 2):
        tiles.append(jnp.where(first, jnp.broadcast_to(col[:, e:e + 1], (q, 2 * SSD_P)),
                               jnp.broadcast_to(col[:, e + 1:e + 2], (q, 2 * SSD_P))))
    return jnp.concatenate(tiles, axis=1)


def _ssd_direction(xs, bm, cm, dt_col_raw, dt_row_raw, bias_col, bias_row, a_col, a_row, h_ref,
                   backward):
    q = xs.shape[0]
    ri = lax.broadcasted_iota(jnp.int32, (q, q), 0)
    ci = lax.broadcasted_iota(jnp.int32, (q, q), 1)
    low = ri >= ci
    keep = (ri <= ci) if backward else low
    tri_col = jnp.where(keep, 1.0, 0.0).astype(BF16)
    tri_row = jnp.where((ri >= ci) if backward else (ri <= ci), 1.0, 0.0).astype(BF16)

    dt_c = _softplus(dt_col_raw + bias_col)
    dt_r = _softplus(dt_row_raw + bias_row)
    cum_c = sum(jnp.dot(tri_col, part, preferred_element_type=F32) for part in _split3(dt_c * a_col))
    cum_r = sum(jnp.dot(part, tri_row, preferred_element_type=F32) for part in _split3(dt_r * a_row))
    cum_full = _expand_heads(cum_c)
    dt_full = _expand_heads(dt_c)
    total = cum_full[0:1, :] if backward else cum_full[q - 1:q, :]

    xdt = xs * dt_full
    cmb = cm.astype(BF16)
    cb = lax.dot_general(cmb, bm.astype(BF16), (((1,), (1,)), ((), ())), preferred_element_type=F32)
    h = h_ref[...]
    y_off = jnp.dot(cmb, h.astype(BF16), preferred_element_type=F32) * jnp.exp(cum_full)
    wgt = (xdt * jnp.exp(total - cum_full)).astype(BF16)
    h_ref[...] = jnp.exp(total) * h + jnp.dot(bm.T.astype(BF16), wgt, preferred_element_type=F32)

    xdt_b = xdt.astype(BF16)
    lane = lax.broadcasted_iota(jnp.int32, (q, 2 * SSD_P), 1)
    parts = []
    for pair in range(SSD_E // 2):
        x_pair = xdt_b[:, pair * 2 * SSD_P:(pair + 1) * 2 * SSD_P]
        ys = []
        for e in (2 * pair, 2 * pair + 1):
            seg = cum_c[:, e:e + 1] - cum_r[e:e + 1, :]
            dec = jnp.exp(jnp.where(keep, seg, -jnp.inf))
            ys.append(jnp.dot((cb * dec).astype(BF16), x_pair, preferred_element_type=F32))
        parts.append(jnp.where(lane < SSD_P, ys[0], ys[1]))
    return jnp.concatenate(parts, axis=1) + y_off


def _ssd_kernel(xf_ref, bf_ref, cf_ref, xb_ref, bb_ref, cb_ref, dcf_ref, drf_ref, dcb_ref, drb_ref,
                bias_c_ref, bias_r_ref, a_c_ref, a_r_ref, d_ref, yf_ref, yb_ref, hf_sc, hb_sc):
    @pl.when(pl.program_id(2) == 0)
    def _():
        hf_sc[...] = jnp.zeros(hf_sc.shape, F32)
        hb_sc[...] = jnp.zeros(hb_sc.shape, F32)

    nsub = xf_ref.shape[1] // SSD_Q
    for i in range(nsub):
        rf = slice(i * SSD_Q, (i + 1) * SSD_Q)
        rb = slice((nsub - 1 - i) * SSD_Q, (nsub - i) * SSD_Q)
        xs = xf_ref[0, rf, :].astype(F32)
        yf = _ssd_direction(xs, bf_ref[0, rf, :].astype(F32), cf_ref[0, rf, :].astype(F32),
                            dcf_ref[0, 0, 0, rf, :], drf_ref[0, 0, 0, :, rf],
                            bias_c_ref[0, 0], bias_r_ref[0, 0], a_c_ref[0, 0], a_r_ref[0, 0], hf_sc, False)
        yf_ref[0, rf, :] = (yf + d_ref[0] * xs).astype(yf_ref.dtype)
        yb = _ssd_direction(xb_ref[0, rb, :].astype(F32), bb_ref[0, rb, :].astype(F32),
                            cb_ref[0, rb, :].astype(F32), dcb_ref[0, 0, 0, rb, :], drb_ref[0, 0, 0, :, rb],
                            bias_c_ref[1, 0], bias_r_ref[1, 0], a_c_ref[1, 0], a_r_ref[1, 0], hb_sc, True)
        yb_ref[0, rb, :] = yb.astype(yb_ref.dtype)


def _ssd(xc, dt_col, dt_row, bias_c, bias_r, a_c, a_r, d_full, ctx_len, out_dtype):
    b, t, _ = xc.shape
    rows = _tile(ctx_len, SSD_ROWS, SSD_Q)
    nblk = t // rows
    nctx = ctx_len // rows
    bblk = SSD_INNER // SSD_N
    cblk = bblk + SSD_G

    def bidx(c):
        return jnp.where(c < nctx, nctx - 1 - c, nblk - 1 - (c - nctx))

    fx = lambda bi, g, c: (bi, c, g)
    fb = lambda bi, g, c: (bi, c, bblk + g)
    fc = lambda bi, g, c: (bi, c, cblk + g)
    bx = lambda bi, g, c: (bi, bidx(c), g)
    bb = lambda bi, g, c: (bi, bidx(c), bblk + g)
    bc = lambda bi, g, c: (bi, bidx(c), cblk + g)
    small = lambda a: pl.BlockSpec((2, 1) + a.shape[2:], lambda bi, g, c: (0, g, 0, 0))
    return pl.pallas_call(
        _ssd_kernel,
        grid=(b, SSD_G, nblk),
        in_specs=[pl.BlockSpec((1, rows, SSD_GW), fx), pl.BlockSpec((1, rows, SSD_N), fb),
                  pl.BlockSpec((1, rows, SSD_N), fc),
                  pl.BlockSpec((1, rows, SSD_GW), bx), pl.BlockSpec((1, rows, SSD_N), bb),
                  pl.BlockSpec((1, rows, SSD_N), bc),
                  pl.BlockSpec((1, 1, 1, rows, SSD_E), lambda bi, g, c: (bi, 0, g, c, 0)),
                  pl.BlockSpec((1, 1, 1, SSD_E, rows), lambda bi, g, c: (bi, 0, g, 0, c)),
                  pl.BlockSpec((1, 1, 1, rows, SSD_E), lambda bi, g, c: (bi, 1, g, bidx(c), 0)),
                  pl.BlockSpec((1, 1, 1, SSD_E, rows), lambda bi, g, c: (bi, 1, g, 0, bidx(c))),
                  small(bias_c), small(bias_r), small(a_c), small(a_r),
                  pl.BlockSpec((1, 1, SSD_GW), lambda bi, g, c: (g, 0, 0))],
        out_specs=[pl.BlockSpec((1, rows, SSD_GW), fx), pl.BlockSpec((1, rows, SSD_GW), bx)],
        out_shape=[jax.ShapeDtypeStruct((b, t, SSD_INNER), out_dtype)] * 2,
        scratch_shapes=[pltpu.VMEM((SSD_N, SSD_GW), F32), pltpu.VMEM((SSD_N, SSD_GW), F32)],
        compiler_params=_params("parallel", "parallel", "arbitrary"),
        name="ssd_scan",
    )(xc, xc, xc, xc, xc, xc, dt_col, dt_row, dt_col, dt_row, bias_c, bias_r, a_c, a_r, d_full)


MERGE_TN = 512


def _merge_kernel(ya_ref, ga_ref, yb_ref, gb_ref, yf_ref, ybk_ref, z_ref, nrm_ref,
                  mg_ref, wa_ref, wb_ref, wc_ref, u_ref, c_sc):
    a_in = (ya_ref[0].astype(F32) * _silu(ga_ref[0].astype(F32))).astype(BF16)
    b_in = (yb_ref[0].astype(F32) * _silu(gb_ref[0].astype(F32))).astype(BF16)
    v = (yf_ref[0].astype(F32) + ybk_ref[0].astype(F32)) * _silu(z_ref[0].astype(F32))
    for g in range(SSD_G):
        sl = slice(g * SSD_GW, (g + 1) * SSD_GW)
        c_sc[:, sl] = (_rms(v[:, sl]) * nrm_ref[:, sl]).astype(BF16)
    c_in = c_sc[...]
    d = u_ref.shape[-1]
    for j in range(0, d, MERGE_TN):
        sl = slice(j, j + MERGE_TN)
        br_a = jnp.dot(a_in, wa_ref[:, sl], preferred_element_type=F32)
        br_b = jnp.dot(b_in, wb_ref[:, sl], preferred_element_type=F32)
        br_c = jnp.dot(c_in, wc_ref[:, sl], preferred_element_type=F32)
        u = (jax.nn.sigmoid(mg_ref[0, :, j:j + MERGE_TN].astype(F32)) * br_a
             + jax.nn.sigmoid(mg_ref[0, :, d + j:d + j + MERGE_TN].astype(F32)) * br_b
             + jax.nn.sigmoid(mg_ref[0, :, 2 * d + j:2 * d + j + MERGE_TN].astype(F32)) * br_c)
        u_ref[0, :, sl] = u.astype(u_ref.dtype)


def _merge(ya, yb, yf, ybk, p3, ssd_norm, wa, wb, wc, row_off):
    b, rows, _ = ya.shape
    tm = _tile(rows, 256)
    d = D_MODEL
    ro = row_off // tm
    assert P_OFF['mg'] == 0

    def col(name, width):
        blk = P_OFF[name] // width
        return pl.BlockSpec((1, tm, width), lambda bi, i: (bi, i + ro, blk))

    loc = lambda w: pl.BlockSpec((1, tm, w), lambda bi, i: (bi, i, 0))
    glob = lambda w: pl.BlockSpec((1, tm, w), lambda bi, i: (bi, i + ro, 0))
    wspec = lambda k: pl.BlockSpec((k, d), lambda bi, i: (0, 0), pipeline_mode=pl.Buffered(1))
    return pl.pallas_call(
        _merge_kernel,
        grid=(b, rows // tm),
        in_specs=[loc(MLA_WIDTH), col('ga', MLA_WIDTH), loc(GQA_WIDTH), col('gb', GQA_WIDTH),
                  glob(SSD_INNER), glob(SSD_INNER), col('z', SSD_INNER),
                  pl.BlockSpec((1, SSD_INNER), lambda bi, i: (0, 0)),
                  glob(N_BRANCH * d), wspec(MLA_WIDTH), wspec(GQA_WIDTH), wspec(SSD_INNER)],
        out_specs=pl.BlockSpec((1, tm, d), lambda bi, i: (bi, i, 0)),
        out_shape=jax.ShapeDtypeStruct((b, rows, d), BF16),
        scratch_shapes=[pltpu.VMEM((tm, SSD_INNER), BF16)],
        compiler_params=_params("parallel", "parallel"),
        name="merge",
    )(ya, p3, yb, p3, yf, ybk, p3, ssd_norm.reshape(1, SSD_INNER), p3, wa, wb, wc)


def _out_ln_kernel(u_ref, w_ref, *rest, with_next, ctx_tiles, ro, split):
    if split:
        ctx_ref, x_ref, gate_ref, g_ref, b_ref, *rest = rest
        res = _residual_rows(ctx_ref, x_ref, ctx_tiles, ro)
    else:
        x_ref, gate_ref, g_ref, b_ref, *rest = rest
        res = x_ref[0]
    out = jnp.dot(u_ref[0], w_ref[...], preferred_element_type=F32)
    r = DEEPNORM_ALPHA * res + gate_ref[0] * out
    xn = _layer_norm(r) * g_ref[...] + b_ref[...]
    if with_next:
        sh_ref, sc_ref, xo_ref, xm_ref = rest
        xo_ref[0] = xn
        xm_ref[0] = (_layer_norm(xn) * (1.0 + sc_ref[0]) + sh_ref[0]).astype(xm_ref.dtype)
    else:
        (xo_ref,) = rest
        xo_ref[0] = xn


def _out_ln(u, w_out, xc, tab, ln_g, ln_b, next_tab, nb, ctx_len, row_off):
    b, rows, d = u.shape
    tm = _tile(ctx_len, 256)
    ro = row_off // tm
    ctx_tiles = ctx_len // tm
    row = _mod_row_index(nb, ctx_tiles)
    with_next = next_tab is not None
    split = isinstance(xc, tuple)
    loc = pl.BlockSpec((1, tm, d), lambda bi, i: (bi, i, 0))
    vec = pl.BlockSpec((1, d), lambda bi, i: (0, 0))
    res_specs = (_residual_specs(tm, d, ctx_tiles, ro) if split
                 else [pl.BlockSpec((1, tm, d), lambda bi, i: (bi, i + ro, 0))])
    in_specs = [loc, pl.BlockSpec((d, d), lambda bi, i: (0, 0))] + res_specs + [
        pl.BlockSpec((1, 1, d), lambda bi, i: (row(bi, i + ro), 0, 2)), vec, vec]
    args = [u, w_out] + (list(xc) if split else [xc]) + [tab, ln_g.reshape(1, d), ln_b.reshape(1, d)]
    out_specs = [loc]
    out_shape = [jax.ShapeDtypeStruct((b, rows, d), F32)]
    if with_next:
        in_specs += [pl.BlockSpec((1, 1, d), lambda bi, i: (row(bi, i + ro), 0, 0)),
                     pl.BlockSpec((1, 1, d), lambda bi, i: (row(bi, i + ro), 0, 1))]
        args += [next_tab, next_tab]
        out_specs.append(loc)
        out_shape.append(jax.ShapeDtypeStruct((b, rows, d), BF16))
    return pl.pallas_call(
        functools.partial(_out_ln_kernel, with_next=with_next, ctx_tiles=ctx_tiles, ro=ro, split=split),
        grid=(b, rows // tm),
        in_specs=in_specs,
        out_specs=out_specs,
        out_shape=out_shape,
        compiler_params=_params("parallel", "parallel"),
        name="out_ln",
    )(*args)


def _rope_angles(rows, dim):
    row, col = jnp.meshgrid(jnp.arange(rows, dtype=F32), jnp.arange(GRID_W, dtype=F32), indexing='ij')
    half = dim // 2
    inv_freq = ROPE_THETA ** (-jnp.arange(0, half, 2, dtype=F32) / half)
    ang_r = row.reshape(-1, 1) * inv_freq
    ang_c = col.reshape(-1, 1) * inv_freq
    return jnp.concatenate([ang_r, ang_r, ang_c, ang_c], axis=-1)


def _rope_tables(seq, ctx_len, dim):
    ang = _rope_angles(seq // GRID_W, dim)
    cos = jnp.concatenate([jnp.ones((ctx_len, dim), F32), jnp.cos(ang)], axis=0)
    sin = jnp.concatenate([jnp.zeros((ctx_len, dim), F32), jnp.sin(ang)], axis=0)
    return cos, sin


def _roll_tables(cos, sin, dim, scale):
    t = cos.shape[0]
    quarter = dim // 4
    first = (jnp.arange(dim) % (2 * quarter)) < quarter
    s1 = jnp.where(first, -sin, 0.0)
    s2 = jnp.where(first, 0.0, sin)
    pad = lambda a: jnp.pad(a * scale, ((0, 0), (0, 128 - dim)))
    return pad(cos), pad(s1), pad(s2)


def _rot_matrix(dim):
    quarter = dim // 4
    r = np.zeros((dim, dim), np.float32)
    for i in range(dim):
        blk = i // quarter
        if blk % 2 == 0:
            r[i + quarter, i] = -1.0
        else:
            r[i - quarter, i] = 1.0
    return jnp.asarray(r)


def _mla_q_weights(w_uq):
    w = w_uq.reshape(MLA_Q_LORA, MLA_HEADS, MLA_QK)
    nope, pe = w[..., :MLA_NOPE], w[..., MLA_NOPE:]
    pe_rot = jnp.einsum('khd,de->khe', pe, _rot_matrix(MLA_ROPE), precision=HIGHEST)
    zpad = jnp.zeros((MLA_Q_LORA, MLA_HEADS, MLA_QK_PAD - MLA_QK), F32)
    main = jnp.concatenate([nope, pe, zpad], axis=-1)
    rot = jnp.concatenate([jnp.zeros_like(nope), pe_rot, zpad], axis=-1)
    return (main.reshape(MLA_Q_LORA, -1).astype(BF16), rot.reshape(MLA_Q_LORA, -1).astype(BF16))


def _permute_w_in(w):
    parts = [w[:, IN_OFFSETS[n]:IN_OFFSETS[n] + IN_WIDTHS[n]] for n in P_ORDER]
    parts.append(jnp.zeros((w.shape[0], P_WIDTH - P_USED), w.dtype))
    return jnp.concatenate(parts, axis=1).astype(BF16)


P_DTYPE = BF16
Y_DTYPE = BF16


def kernel(x, c, ctx, c_ctx, w_mod, b_mod, w_in, mla_q_norm, mla_w_uq, mla_kv_norm, mla_w_ukv,
           gqa_q_norm, gqa_k_norm, ssd_conv_w, ssd_conv_b, ssd_a_log, ssd_dt_bias, ssd_d, ssd_norm,
           w_br_a, w_br_b, w_br_c, w_out, ln_g, ln_b):
    nb, seq, d = x.shape
    ctx_len = ctx.shape[1]
    t = ctx_len + seq
    depth = w_in.shape[0]
    assert d == D_MODEL and nb < 8 and seq % GRID_W == 0
    assert ctx_len % SSD_Q == 0 and seq % SSD_Q == 0

    cos_a, sin_a = _rope_tables(seq, ctx_len, MLA_ROPE)
    cos_b, sin_b = _rope_tables(seq, ctx_len, GQA_DIM)
    sq = MLA_QK ** -0.5 * LOG2E
    zq = jnp.zeros((t, MLA_QK_PAD - MLA_QK), F32)
    cosq = jnp.concatenate([jnp.full((t, MLA_NOPE), sq, F32), cos_a * sq, zq], axis=1)
    sinq = jnp.concatenate([jnp.zeros((t, MLA_NOPE), F32), sin_a * sq, zq], axis=1)
    mla_tabs = (cosq, sinq) + _roll_tables(cos_a, sin_a, MLA_ROPE, 1.0)
    gqa_tabs = (_roll_tables(cos_b, sin_b, GQA_DIM, GQA_DIM ** -0.5 * LOG2E)
                + _roll_tables(cos_b, sin_b, GQA_DIM, 1.0))

    c_rows = jnp.zeros((8, d), F32).at[:nb].set(c).at[nb].set(c_ctx)
    tabs = [_mod_rows(c_rows, w_mod[l], b_mod[l]).reshape(8, 1, 3 * d) for l in range(depth)]

    xc = (ctx, x)
    xm = _ln_mod(ctx, x, tabs[0], nb)

    for l in range(depth):
        last = l == depth - 1
        wp = _permute_w_in(w_in[l])
        p2, krdt = _in_proj(xm.reshape(nb * t, d), wp, P_DTYPE)
        p3 = p2.reshape(nb, t, P_WIDTH)

        wqm, wqr = _mla_q_weights(mla_w_uq[l])
        qa, ka, va = _mla_prep(p3, mla_q_norm[l].reshape(1, -1), mla_kv_norm[l].reshape(1, -1),
                               wqm, wqr, mla_w_ukv[l].astype(BF16), mla_tabs, ctx_len)
        ya = _attention(qa, ka, va, ctx_len, 256, 2, not last, Y_DTYPE)
        qb, kb, vb = _gqa_prep(p3, gqa_q_norm[l].reshape(1, -1), gqa_k_norm[l].reshape(1, -1),
                               gqa_tabs, ctx_len)
        yb = _attention(qb, kb, vb, ctx_len, 128, 1, not last, Y_DTYPE)

        xconv = _conv(p3, ssd_conv_w[l], ssd_conv_b[l], ctx_len, Y_DTYPE)
        dt5 = krdt[:, 64:].reshape(nb, t, 2, SSD_G, SSD_E)
        dt_col = jnp.transpose(dt5, (0, 2, 3, 1, 4))
        dt_row = jnp.transpose(dt5, (0, 2, 3, 4, 1))
        bias = ssd_dt_bias[l].astype(F32).reshape(2, SSD_G, SSD_E)
        a = -jnp.exp(ssd_a_log[l].astype(F32)).reshape(2, SSD_G, SSD_E)
        d_full = jnp.repeat(ssd_d[l].astype(F32), SSD_P).reshape(SSD_G, 1, SSD_GW)
        yf, ybk = _ssd(xconv, dt_col, dt_row, bias[:, :, None, :], bias[:, :, :, None],
                       a[:, :, None, :], a[:, :, :, None], d_full, ctx_len, Y_DTYPE)

        row_off = ctx_len if last else 0
        u = _merge(ya, yb, yf, ybk, p3, ssd_norm[l], w_br_a[l].astype(BF16), w_br_b[l].astype(BF16),
                   w_br_c[l].astype(BF16), row_off)
        if last:
            (xo,) = _out_ln(u, w_out[l].astype(BF16), xc, tabs[l], ln_g[l], ln_b[l], None, nb,
                            ctx_len, row_off)
            return xo
        xc, xm = _out_ln(u, w_out[l].astype(BF16), xc, tabs[l], ln_g[l], ln_b[l], tabs[l + 1], nb,
                         ctx_len, row_off)
```

```python
import functools
import math

import numpy as np
import jax
import jax.numpy as jnp
from jax import lax
from jax.experimental import pallas as pl
from jax.experimental.pallas import tpu as pltpu

F32 = jnp.float32
BF16 = jnp.bfloat16
HIGHEST = lax.Precision.HIGHEST

D_MODEL = 2048
DEPTH = 2
GRID_W = 64
ROPE_THETA = 10000.0
EPS = 1e-6

MLA_HEADS = 8
MLA_Q_LORA = 512
MLA_KV_LORA = 256
MLA_NOPE = 128
MLA_ROPE = 64
MLA_V = 128
MLA_QK = MLA_NOPE + MLA_ROPE
MLA_QK_PAD = 256
MLA_WIDTH = MLA_HEADS * MLA_V

GQA_HEADS = 8
GQA_KV_HEADS = 2
GQA_GROUP = GQA_HEADS // GQA_KV_HEADS
GQA_DIM = 128
GQA_WIDTH = GQA_HEADS * GQA_DIM
GQA_KV_WIDTH = GQA_KV_HEADS * GQA_DIM

SSD_INNER = D_MODEL
SSD_P = 64
SSD_HEADS = SSD_INNER // SSD_P
SSD_G = 4
SSD_E = SSD_HEADS // SSD_G
SSD_N = 128
SSD_CONV = 5
SSD_Q = 128
SSD_ROWS = 256
SSD_GW = SSD_E * SSD_P
SSD_CONV_DIM = SSD_INNER + 2 * SSD_G * SSD_N

N_BRANCH = 3
IN_SPLITS = (MLA_Q_LORA, MLA_KV_LORA, MLA_ROPE, MLA_WIDTH, GQA_WIDTH, GQA_KV_WIDTH, GQA_KV_WIDTH,
             GQA_WIDTH, SSD_INNER, SSD_CONV_DIM, 2 * SSD_HEADS, N_BRANCH * D_MODEL)
IN_NAMES = ('cq', 'ckv', 'kr', 'ga', 'gq', 'gk', 'gv', 'gb', 'z', 'xbc', 'dtr', 'mg')
IN_OFFSETS = dict(zip(IN_NAMES, np.concatenate([[0], np.cumsum(IN_SPLITS)[:-1]]).tolist()))
IN_WIDTHS = dict(zip(IN_NAMES, IN_SPLITS))
P_ORDER = ('mg', 'z', 'xbc', 'ga', 'gq', 'gb', 'cq', 'ckv', 'gk', 'gv', 'kr', 'dtr')
P_OFF = {}
_o = 0
for _n in P_ORDER:
    P_OFF[_n] = _o
    _o += IN_WIDTHS[_n]
P_USED = _o
P_TN = 512
P_WIDTH = -(-P_USED // P_TN) * P_TN

DEEPNORM_ALPHA = (2 * DEPTH) ** 0.25

VMEM_LIMIT = 56 * 2 ** 20


def _params(*sem):
    return pltpu.CompilerParams(dimension_semantics=sem, vmem_limit_bytes=VMEM_LIMIT)


def _tile(n, target, align=8):
    t = min(n, target)
    while t > align and (n % t or t % align):
        t -= align
    assert n % t == 0, (n, target)
    return t


def _silu(v):
    return v * jax.nn.sigmoid(v)


def _softplus(v):
    return jnp.maximum(v, 0.0) + jnp.log1p(jnp.exp(-jnp.abs(v)))


def _layer_norm(v):
    mu = jnp.mean(v, axis=-1, keepdims=True)
    vc = v - mu
    var = jnp.mean(vc * vc, axis=-1, keepdims=True)
    return vc * lax.rsqrt(var + EPS)


def _rms(v):
    return v * lax.rsqrt(jnp.mean(v * v, axis=-1, keepdims=True) + EPS)


def _mod_kernel(c_ref, w_ref, b_ref, o_ref):
    a = _silu(c_ref[...]).astype(BF16)
    o_ref[...] = jnp.dot(a, w_ref[...].astype(BF16), preferred_element_type=F32) + b_ref[...]


def _mod_rows(c_rows, w_mod, b_mod):
    r, d = c_rows.shape
    n = w_mod.shape[1]
    tn = 512
    return pl.pallas_call(
        _mod_kernel,
        grid=(n // tn,),
        in_specs=[pl.BlockSpec((r, d), lambda j: (0, 0)),
                  pl.BlockSpec((d, tn), lambda j: (0, j)),
                  pl.BlockSpec((1, tn), lambda j: (0, j))],
        out_specs=pl.BlockSpec((r, tn), lambda j: (0, j)),
        out_shape=jax.ShapeDtypeStruct((r, n), F32),
        compiler_params=_params("arbitrary"),
        name="mod_rows",
    )(c_rows, w_mod, b_mod.reshape(1, n))


def _residual_rows(ctx_ref, x_ref, ctx_tiles, ro):
    return jnp.where(pl.program_id(1) + ro < ctx_tiles, ctx_ref[0], x_ref[0])


def _residual_specs(tm, d, ctx_tiles, ro):
    return [pl.BlockSpec((1, tm, d), lambda bi, i: (bi, jnp.minimum(i + ro, ctx_tiles - 1), 0)),
            pl.BlockSpec((1, tm, d), lambda bi, i: (bi, jnp.maximum(i + ro - ctx_tiles, 0), 0))]


def _ln_mod_kernel(ctx_ref, x_ref, sh_ref, sc_ref, o_ref, *, ctx_tiles):
    y = _layer_norm(_residual_rows(ctx_ref, x_ref, ctx_tiles, 0))
    o_ref[0] = (y * (1.0 + sc_ref[0]) + sh_ref[0]).astype(o_ref.dtype)


def _mod_row_index(nb, ctx_tiles):
    return lambda b, i: jnp.where(i < ctx_tiles, nb, b)


def _ln_mod(ctx, x, tab, nb):
    b, seq, d = x.shape
    ctx_len = ctx.shape[1]
    tm = _tile(ctx_len, 256)
    ctx_tiles = ctx_len // tm
    row = _mod_row_index(nb, ctx_tiles)
    return pl.pallas_call(
        functools.partial(_ln_mod_kernel, ctx_tiles=ctx_tiles),
        grid=(b, (ctx_len + seq) // tm),
        in_specs=_residual_specs(tm, d, ctx_tiles, 0)
        + [pl.BlockSpec((1, 1, d), lambda bi, i: (row(bi, i), 0, 0)),
           pl.BlockSpec((1, 1, d), lambda bi, i: (row(bi, i), 0, 1))],
        out_specs=pl.BlockSpec((1, tm, d), lambda bi, i: (bi, i, 0)),
        out_shape=jax.ShapeDtypeStruct((b, ctx_len + seq, d), BF16),
        compiler_params=_params("parallel", "parallel"),
        name="ln_mod",
    )(ctx, x, tab, tab)


def _in_proj_kernel(x_ref, w_ref, o_ref, dt_ref, *, dt_tile, dt_col):
    acc = jnp.dot(x_ref[...], w_ref[...], preferred_element_type=F32)
    o_ref[...] = acc.astype(o_ref.dtype)

    @pl.when(pl.program_id(1) == dt_tile)
    def _():
        dt_ref[...] = acc[:, dt_col:dt_col + 128]


def _in_proj(xm2, wp, out_dtype):
    m, k = xm2.shape
    n = wp.shape[1]
    tm = _tile(m, 1024)
    tn = P_TN
    dt_tile, dt_col = divmod(P_OFF['kr'], tn)
    assert P_OFF['dtr'] == P_OFF['kr'] + 64 and dt_col % 128 == 0
    return pl.pallas_call(
        functools.partial(_in_proj_kernel, dt_tile=dt_tile, dt_col=dt_col),
        grid=(m // tm, n // tn),
        in_specs=[pl.BlockSpec((tm, k), lambda i, j: (i, 0)),
                  pl.BlockSpec((k, tn), lambda i, j: (0, j))],
        out_specs=[pl.BlockSpec((tm, tn), lambda i, j: (i, j)),
                   pl.BlockSpec((tm, 128), lambda i, j: (i, 0))],
        out_shape=[jax.ShapeDtypeStruct((m, n), out_dtype), jax.ShapeDtypeStruct((m, 128), F32)],
        compiler_params=_params("parallel", "arbitrary"),
        name="in_proj",
    )(xm2, wp)


def _mla_prep_kernel(cq_ref, ckv_ref, kr_ref, qn_ref, kvn_ref, wqm_ref, wqr_ref, wkv_ref,
                     cosq_ref, sinq_ref, ck_ref, s1_ref, s2_ref, qa_ref, ka_ref, va_ref):
    cqn = (_rms(cq_ref[0].astype(F32)) * qn_ref[...]).astype(BF16)
    qm = jnp.dot(cqn, wqm_ref[...], preferred_element_type=F32)
    qr = jnp.dot(cqn, wqr_ref[...], preferred_element_type=F32)
    cosq = cosq_ref[...]
    sinq = sinq_ref[...]
    for h in range(MLA_HEADS):
        sl = slice(h * MLA_QK_PAD, (h + 1) * MLA_QK_PAD)
        qa_ref[0, h] = (qm[:, sl] * cosq + qr[:, sl] * sinq).T.astype(qa_ref.dtype)
    ckvn = (_rms(ckv_ref[0].astype(F32)) * kvn_ref[...]).astype(BF16)
    kv = jnp.dot(ckvn, wkv_ref[...], preferred_element_type=F32)
    kr = kr_ref[0].astype(F32)
    kpe = (kr * ck_ref[...] + pltpu.roll(kr, 128 - 16, axis=1) * s1_ref[...]
           + pltpu.roll(kr, 16, axis=1) * s2_ref[...]).astype(ka_ref.dtype)
    for h in range(MLA_HEADS):
        base = h * (MLA_NOPE + MLA_V)
        ka_ref[0, h, :, 0:MLA_NOPE] = kv[:, base:base + MLA_NOPE].astype(ka_ref.dtype)
        ka_ref[0, h, :, MLA_NOPE:MLA_QK_PAD] = kpe
        va_ref[0, h] = kv[:, base + MLA_NOPE:base + MLA_NOPE + MLA_V].T.astype(va_ref.dtype)


def _mla_prep(p3, qn, kvn, wqm, wqr, wkv, tabs, ctx_len):
    b, t, _ = p3.shape
    tm = _tile(ctx_len, 256)
    cosq, sinq, ck, s1, s2 = tabs

    def col(name, width):
        blk = P_OFF[name] // width
        return pl.BlockSpec((1, tm, width), lambda bi, i: (bi, i, blk))

    def full(a):
        return pl.BlockSpec(a.shape, lambda bi, i: (0,) * a.ndim)

    def rows(a):
        return pl.BlockSpec((tm, a.shape[1]), lambda bi, i: (i, 0))

    hm = lambda w: pl.BlockSpec((1, MLA_HEADS, tm, w), lambda bi, i: (bi, 0, i, 0))
    return pl.pallas_call(
        _mla_prep_kernel,
        grid=(b, t // tm),
        in_specs=[col('cq', MLA_Q_LORA), col('ckv', MLA_KV_LORA), col('kr', 128),
                  full(qn), full(kvn), full(wqm), full(wqr), full(wkv),
                  rows(cosq), rows(sinq), rows(ck), rows(s1), rows(s2)],
        out_specs=[pl.BlockSpec((1, MLA_HEADS, MLA_QK_PAD, tm), lambda bi, i: (bi, 0, 0, i)),
                   hm(MLA_QK_PAD),
                   pl.BlockSpec((1, MLA_HEADS, MLA_V, tm), lambda bi, i: (bi, 0, 0, i))],
        out_shape=[jax.ShapeDtypeStruct((b, MLA_HEADS, MLA_QK_PAD, t), BF16),
                   jax.ShapeDtypeStruct((b, MLA_HEADS, t, MLA_QK_PAD), BF16),
                   jax.ShapeDtypeStruct((b, MLA_HEADS, MLA_V, t), BF16)],
        compiler_params=_params("parallel", "parallel"),
        name="mla_prep",
    )(p3, p3, p3, qn, kvn, wqm, wqr, wkv, cosq, sinq, ck, s1, s2)


def _rope128(y, c, s1, s2):
    return y * c + pltpu.roll(y, 128 - 32, axis=1) * s1 + pltpu.roll(y, 32, axis=1) * s2


def _gqa_prep_kernel(gq_ref, gk_ref, gv_ref, qn_ref, kn_ref, cq_ref, s1q_ref, s2q_ref,
                     ck_ref, s1k_ref, s2k_ref, qb_ref, kb_ref, vb_ref):
    gq = gq_ref[0].astype(F32)
    for h in range(GQA_HEADS):
        y = _rms(gq[:, h * GQA_DIM:(h + 1) * GQA_DIM]) * qn_ref[...]
        qb_ref[0, h] = _rope128(y, cq_ref[...], s1q_ref[...], s2q_ref[...]).T.astype(qb_ref.dtype)
    gk = gk_ref[0].astype(F32)
    gv = gv_ref[0].astype(F32)
    for h in range(GQA_KV_HEADS):
        y = _rms(gk[:, h * GQA_DIM:(h + 1) * GQA_DIM]) * kn_ref[...]
        kb_ref[0, h] = _rope128(y, ck_ref[...], s1k_ref[...], s2k_ref[...]).astype(kb_ref.dtype)
        vb_ref[0, h] = gv[:, h * GQA_DIM:(h + 1) * GQA_DIM].T.astype(vb_ref.dtype)


def _gqa_prep(p3, qn, kn, tabs, ctx_len):
    b, t, _ = p3.shape
    tm = _tile(ctx_len, 256)

    def col(name, width):
        blk = P_OFF[name] // width
        return pl.BlockSpec((1, tm, width), lambda bi, i: (bi, i, blk))

    def full(a):
        return pl.BlockSpec(a.shape, lambda bi, i: (0,) * a.ndim)

    def rows(a):
        return pl.BlockSpec((tm, a.shape[1]), lambda bi, i: (i, 0))

    hm = lambda nh: pl.BlockSpec((1, nh, tm, GQA_DIM), lambda bi, i: (bi, 0, i, 0))
    hmt = lambda nh: pl.BlockSpec((1, nh, GQA_DIM, tm), lambda bi, i: (bi, 0, 0, i))
    return pl.pallas_call(
        _gqa_prep_kernel,
        grid=(b, t // tm),
        in_specs=[col('gq', GQA_WIDTH), col('gk', GQA_KV_WIDTH), col('gv', GQA_KV_WIDTH),
                  full(qn), full(kn)] + [rows(a) for a in tabs],
        out_specs=[hmt(GQA_HEADS), hm(GQA_KV_HEADS), hmt(GQA_KV_HEADS)],
        out_shape=[jax.ShapeDtypeStruct((b, GQA_HEADS, GQA_DIM, t), BF16),
                   jax.ShapeDtypeStruct((b, GQA_KV_HEADS, t, GQA_DIM), BF16),
                   jax.ShapeDtypeStruct((b, GQA_KV_HEADS, GQA_DIM, t), BF16)],
        compiler_params=_params("parallel", "parallel"),
        name="gqa_prep",
    )(p3, p3, p3, qn, kn, *tabs)


ATTN_KC = 256
ATTN_QB = 128
LOG2E = math.log2(math.e)


def _attn_logits(qts, k_ref, s_buf, nkeys):
    nqb = qts[0].shape[1] // ATTN_QB
    for j, qt in enumerate(qts):
        st = jnp.dot(k_ref[0, j, 0:nkeys, :], qt, preferred_element_type=F32)
        for b in range(nqb):
            s_buf[j * nqb + b, 0:nkeys, :] = st[:, b * ATTN_QB:(b + 1) * ATTN_QB]


def _attn_softmax_slab(s_buf, p_buf, l_buf, b, nkeys):
    parts = [jnp.max(s_buf[b, k0:k0 + ATTN_KC, :].reshape(ATTN_KC // 64, 8, 8, ATTN_QB), axis=0)
             for k0 in range(0, nkeys, ATTN_KC)]
    while len(parts) > 1:
        parts = [jnp.maximum(parts[i], parts[i + 1]) if i + 1 < len(parts) else parts[i]
                 for i in range(0, len(parts), 2)]
    m = jnp.max(parts[0], axis=(0, 1), keepdims=True)[0]
    lacc = jnp.zeros((8, 8, ATTN_QB), F32)
    for k0 in range(0, nkeys, ATTN_KC):
        p = jnp.exp2(s_buf[b, k0:k0 + ATTN_KC, :] - m)
        lacc = lacc + jnp.sum(p.reshape(ATTN_KC // 64, 8, 8, ATTN_QB), axis=0)
        p_buf[b, k0:k0 + ATTN_KC, :] = p.astype(p_buf.dtype)
    l = jnp.sum(lacc, axis=(0, 1), keepdims=True)[0]
    l_buf[b] = jnp.broadcast_to(1.0 / l, (8, ATTN_QB))


def _attn_output(vt_ref, p_buf, l_buf, o_ref, nkeys, hps, group, tq):
    dv = vt_ref.shape[2]
    nqb = group * tq // ATTN_QB
    for j in range(hps):
        slabs = range(j * nqb, (j + 1) * nqb)
        pt = jnp.concatenate([p_buf[b, 0:nkeys, :] for b in slabs], axis=1)
        ot = jnp.dot(vt_ref[0, j, :, 0:nkeys], pt, preferred_element_type=F32)
        ot = ot * jnp.concatenate([l_buf[b, 0:1, :] for b in slabs], axis=1)
        for g in range(group):
            c0 = (j * group + g) * dv
            o_ref[0, :, c0:c0 + dv] = ot[:, g * tq:(g + 1) * tq].T.astype(o_ref.dtype)


def _attn_queries(qt_ref, hps, group):
    return [jnp.concatenate([qt_ref[0, j * group + g] for g in range(group)], axis=1)
            for j in range(hps)]


def _attn_ctx_kernel(qt_ref, k_ref, vt_ref, o_ref, s_sc, p_sc, l_sc, *, group, tq):
    hps, nkeys = k_ref.shape[1], k_ref.shape[2]
    nslab = hps * group * tq // ATTN_QB
    _attn_logits(_attn_queries(qt_ref, hps, group), k_ref, s_sc, nkeys)

    def slab(b, carry):
        _attn_softmax_slab(s_sc, p_sc, l_sc, b, nkeys)
        return carry

    lax.fori_loop(0, nslab, slab, 0)
    _attn_output(vt_ref, p_sc, l_sc, o_ref, nkeys, hps, group, tq)


def _attn_lat_kernel(qt_ref, k_ref, vt_ref, o_ref, s0, s1, p0, p1, l0, l1, *, group, tq):
    hps, nkeys = k_ref.shape[1], k_ref.shape[2]
    nslab = hps * group * tq // ATTN_QB
    g = pl.program_id(0)

    @pl.when(g == 0)
    def _():
        for buf in (s0, s1, p0, p1, l0, l1):
            buf[...] = jnp.zeros(buf.shape, buf.dtype)

    def step(s_a, s_b, p_b, p_c, l_b, l_c):
        _attn_logits(_attn_queries(qt_ref, hps, group), k_ref, s_a, nkeys)
        for b in range(nslab):
            _attn_softmax_slab(s_b, p_b, l_b, b, nkeys)
        _attn_output(vt_ref, p_c, l_c, o_ref, nkeys, hps, group, tq)

    pl.when(g % 2 == 0)(lambda: step(s0, s1, p1, p0, l1, l0))
    pl.when(g % 2 == 1)(lambda: step(s1, s0, p0, p1, l0, l1))


def _attention(qt, k, vt, ctx_len, tq, hps, with_ctx_queries, out_dtype):
    b, hq, dk, t = qt.shape
    hkv, dv = k.shape[1], vt.shape[2]
    group = hq // hkv
    tq = _tile(ctx_len, tq, 128)
    nslab = hps * group * tq // ATTN_QB
    ctx_tiles = ctx_len // tq
    nq = t // tq - ctx_tiles
    width = hps * group * dv
    assert t % ATTN_KC == 0 and ctx_len % ATTN_KC == 0 and hkv % hps == 0

    def scratch(nkeys):
        return [pltpu.VMEM((nslab, nkeys, ATTN_QB), F32), pltpu.VMEM((nslab, nkeys, ATTN_QB), BF16),
                pltpu.VMEM((nslab, 8, ATTN_QB), F32)]

    s_lat, p_lat, l_lat = scratch(t)
    nh = hkv // hps
    items = b * nh * nq

    def item(step):
        i = jnp.clip(step, 0, items - 1)
        return i // (nh * nq), (i // nq) % nh, i % nq

    def q_map(g):
        bi, h, qi = item(g)
        return bi, h, 0, qi + ctx_tiles

    def k_map(g):
        bi, h, _ = item(g)
        return bi, h, 0, 0

    def v_map(g):
        bi, h, _ = item(g - 2)
        return bi, h, 0, 0

    def o_map(g):
        bi, h, qi = item(g - 2)
        return bi, qi, h

    y_lat = pl.pallas_call(
        functools.partial(_attn_lat_kernel, group=group, tq=tq),
        grid=(items + 2,),
        in_specs=[pl.BlockSpec((1, hps * group, dk, tq), q_map),
                  pl.BlockSpec((1, hps, t, dk), k_map),
                  pl.BlockSpec((1, hps, dv, t), v_map)],
        out_specs=pl.BlockSpec((1, tq, width), o_map),
        out_shape=jax.ShapeDtypeStruct((b, nq * tq, hq * dv), out_dtype),
        scratch_shapes=[s_lat, s_lat, p_lat, p_lat, l_lat, l_lat],
        compiler_params=_params("arbitrary"),
        name="attention_lat_dk%d" % dk,
    )(qt, k, vt)
    if not with_ctx_queries:
        return y_lat
    y_ctx = pl.pallas_call(
        functools.partial(_attn_ctx_kernel, group=group, tq=tq),
        grid=(b, hkv // hps, ctx_tiles),
        in_specs=[pl.BlockSpec((1, hps * group, dk, tq), lambda bi, h, i: (bi, h, 0, i)),
                  pl.BlockSpec((1, hps, ctx_len, dk), lambda bi, h, i: (bi, h, 0, 0)),
                  pl.BlockSpec((1, hps, dv, ctx_len), lambda bi, h, i: (bi, h, 0, 0))],
        out_specs=pl.BlockSpec((1, tq, width), lambda bi, h, i: (bi, i, h)),
        out_shape=jax.ShapeDtypeStruct((b, ctx_len, hq * dv), out_dtype),
        scratch_shapes=scratch(ctx_len),
        compiler_params=_params("parallel", "parallel", "arbitrary"),
        name="attention_ctx_dk%d" % dk,
    )(qt, k, vt)
    return jnp.concatenate([y_ctx, y_lat], axis=1)


def _conv_kernel(x_ref, w_ref, b_ref, xt_ref, bc_ref, pad_sc, *, ctx_len, rows, x_blocks):
    t = x_ref.shape[1]
    nch = x_ref.shape[2]
    halo = 8
    segs = ((0, ctx_len), (ctx_len, t))
    zeros = jnp.zeros((halo, nch), F32)
    for si, (lo, hi) in enumerate(segs):
        pad_sc[lo + si * halo:lo + (si + 1) * halo, :] = zeros
        for r0 in range(lo, hi, rows):
            pad_sc[r0 + (si + 1) * halo:r0 + (si + 1) * halo + rows, :] = x_ref[0, r0:r0 + rows, :].astype(F32)
    pad_sc[t + 2 * halo:t + 3 * halo, :] = zeros
    w = w_ref[...]
    bias = b_ref[...]

    def emit(transposed):
        for si, (lo, hi) in enumerate(segs):
            for r0 in range(lo, hi, rows):
                base = r0 + (si + 1) * halo - SSD_CONV // 2
                acc = bias + w[0:1, :] * pad_sc[base:base + rows, :]
                for kk in range(1, SSD_CONV):
                    acc = acc + w[kk:kk + 1, :] * pad_sc[base + kk:base + kk + rows, :]
                y = _silu(acc)
                if transposed:
                    xt_ref[0, :, r0:r0 + rows] = y.T.astype(xt_ref.dtype)
                else:
                    bc_ref[0, r0:r0 + rows, :] = y.astype(bc_ref.dtype)

    is_x = pl.program_id(1) < x_blocks
    pl.when(is_x)(lambda: emit(True))
    pl.when(jnp.logical_not(is_x))(lambda: emit(False))


def _conv(p3, conv_w, conv_b, ctx_len, out_dtype):
    b, t, _ = p3.shape
    nch = 256
    rows = _tile(ctx_len, 256)
    blk0 = P_OFF['xbc'] // nch
    x_blocks = SSD_INNER // nch
    kern = functools.partial(_conv_kernel, ctx_len=ctx_len, rows=rows, x_blocks=x_blocks)
    return pl.pallas_call(
        kern,
        grid=(b, SSD_CONV_DIM // nch),
        in_specs=[pl.BlockSpec((1, t, nch), lambda bi, j: (bi, 0, blk0 + j)),
                  pl.BlockSpec((SSD_CONV, nch), lambda bi, j: (0, j)),
                  pl.BlockSpec((1, nch), lambda bi, j: (0, j))],
        out_specs=[pl.BlockSpec((1, nch, t), lambda bi, j: (bi, jnp.minimum(j, x_blocks - 1), 0)),
                   pl.BlockSpec((1, t, nch), lambda bi, j: (bi, 0, jnp.maximum(j - x_blocks, 0)))],
        out_shape=[jax.ShapeDtypeStruct((b, SSD_INNER, t), out_dtype),
                   jax.ShapeDtypeStruct((b, t, SSD_CONV_DIM - SSD_INNER), out_dtype)],
        scratch_shapes=[pltpu.VMEM((t + 24, nch), F32)],
        compiler_params=_params("parallel", "arbitrary"),
        name="ssd_conv",
    )(p3, conv_w, conv_b.reshape(1, SSD_CONV_DIM))


def _split3(v):
    hi = v.astype(BF16)
    r1 = v - hi.astype(F32)
    mid = r1.astype(BF16)
    lo = (r1 - mid.astype(F32)).astype(BF16)
    return hi, mid, lo


def _expand_rows(v):
    q = v.shape[1]
    return jnp.concatenate([jnp.broadcast_to(v[e:e + 1, :], (SSD_P, q)) for e in range(SSD_E)], axis=0)


def _ssd_direction(xt, bm, cm, dt_col_raw, dt_row_raw, bias_col, bias_row, a_col, a_row, h_ref,
                   backward):
    q = xt.shape[1]
    ri = lax.broadcasted_iota(jnp.int32, (q, q), 0)
    ci = lax.broadcasted_iota(jnp.int32, (q, q), 1)
    tri_col = jnp.where((ri <= ci) if backward else (ri >= ci), 1.0, 0.0).astype(BF16)
    tri_row = jnp.where((ri >= ci) if backward else (ri <= ci), 1.0, 0.0).astype(BF16)
    keep_t = (ci <= ri) if backward else (ci >= ri)

    dt_c = _softplus(dt_col_raw + bias_col)
    dt_r = _softplus(dt_row_raw + bias_row)
    cum_c = sum(jnp.dot(tri_col, part, preferred_element_type=F32) for part in _split3(dt_c * a_col))
    cum_r = sum(jnp.dot(part, tri_row, preferred_element_type=F32) for part in _split3(dt_r * a_row))
    total = jnp.broadcast_to(cum_r[:, 0:1] if backward else cum_r[:, q - 1:q], (SSD_E, q))

    bmb = bm.astype(BF16)
    cmb = cm.astype(BF16)
    nt = (((1,), (1,)), ((), ()))
    cbt = lax.dot_general(bmb, cmb, nt, preferred_element_type=F32)
    h = h_ref[...]
    y_off = lax.dot_general(h.astype(BF16), cmb, nt, preferred_element_type=F32) * _expand_rows(jnp.exp(cum_r))
    wgt = (xt * _expand_rows(dt_r * jnp.exp(total - cum_r))).astype(BF16)
    h_ref[...] = _expand_rows(jnp.exp(total)) * h + jnp.dot(wgt, bmb, preferred_element_type=F32)

    xdt = (xt * _expand_rows(dt_r)).astype(BF16)
    parts = []
    for e in range(SSD_E):
        seg = cum_r[e:e + 1, :] - cum_c[:, e:e + 1]
        dec = jnp.exp(jnp.where(keep_t, seg, -jnp.inf))
        parts.append(jnp.dot(xdt[e * SSD_P:(e + 1) * SSD_P, :], (cbt * dec).astype(BF16),
                             preferred_element_type=F32))
    return jnp.concatenate(parts, axis=0) + y_off


def _ssd_kernel(xf_ref, bf_ref, cf_ref, xb_ref, bb_ref, cb_ref, dcf_ref, drf_ref, dcb_ref, drb_ref,
                bias_c_ref, bias_r_ref, a_c_ref, a_r_ref, d_ref, yf_ref, yb_ref, hf_sc, hb_sc):
    @pl.when(pl.program_id(2) == 0)
    def _():
        hf_sc[...] = jnp.zeros(hf_sc.shape, F32)
        hb_sc[...] = jnp.zeros(hb_sc.shape, F32)

    nsub = xf_ref.shape[2] // SSD_Q
    for i in range(nsub):
        rf = slice(i * SSD_Q, (i + 1) * SSD_Q)
        rb = slice((nsub - 1 - i) * SSD_Q, (nsub - i) * SSD_Q)
        xt = xf_ref[0, :, rf].astype(F32)
        yf = _ssd_direction(xt, bf_ref[0, rf, :].astype(F32), cf_ref[0, rf, :].astype(F32),
                            dcf_ref[0, 0, 0, rf, :], drf_ref[0, 0, 0, :, rf],
                            bias_c_ref[0, 0], bias_r_ref[0, 0], a_c_ref[0, 0], a_r_ref[0, 0], hf_sc, False)
        yf_ref[0, :, rf] = (yf + d_ref[0] * xt).astype(yf_ref.dtype)
        yb = _ssd_direction(xb_ref[0, :, rb].astype(F32), bb_ref[0, rb, :].astype(F32),
                            cb_ref[0, rb, :].astype(F32), dcb_ref[0, 0, 0, rb, :], drb_ref[0, 0, 0, :, rb],
                            bias_c_ref[1, 0], bias_r_ref[1, 0], a_c_ref[1, 0], a_r_ref[1, 0], hb_sc, True)
        yb_ref[0, :, rb] = yb.astype(yb_ref.dtype)


def _ssd(xt, bc, dt_col, dt_row, bias_c, bias_r, a_c, a_r, d_rows, ctx_len, out_dtype):
    b, _, t = xt.shape
    assert SSD_Q == SSD_N
    rows = _tile(ctx_len, SSD_ROWS, SSD_Q)
    nblk = t // rows
    nctx = ctx_len // rows

    def bidx(c):
        return jnp.where(c < nctx, nctx - 1 - c, nblk - 1 - (c - nctx))

    fx = lambda bi, g, c: (bi, g, c)
    fb = lambda bi, g, c: (bi, c, g)
    fc = lambda bi, g, c: (bi, c, SSD_G + g)
    bx = lambda bi, g, c: (bi, g, bidx(c))
    bb = lambda bi, g, c: (bi, bidx(c), g)
    bcm = lambda bi, g, c: (bi, bidx(c), SSD_G + g)
    small = lambda a: pl.BlockSpec((2, 1) + a.shape[2:], lambda bi, g, c: (0, g, 0, 0))
    return pl.pallas_call(
        _ssd_kernel,
        grid=(b, SSD_G, nblk),
        in_specs=[pl.BlockSpec((1, SSD_GW, rows), fx), pl.BlockSpec((1, rows, SSD_N), fb),
                  pl.BlockSpec((1, rows, SSD_N), fc),
                  pl.BlockSpec((1, SSD_GW, rows), bx), pl.BlockSpec((1, rows, SSD_N), bb),
                  pl.BlockSpec((1, rows, SSD_N), bcm),
                  pl.BlockSpec((1, 1, 1, rows, SSD_E), lambda bi, g, c: (bi, 0, g, c, 0)),
                  pl.BlockSpec((1, 1, 1, SSD_E, rows), lambda bi, g, c: (bi, 0, g, 0, c)),
                  pl.BlockSpec((1, 1, 1, rows, SSD_E), lambda bi, g, c: (bi, 1, g, bidx(c), 0)),
                  pl.BlockSpec((1, 1, 1, SSD_E, rows), lambda bi, g, c: (bi, 1, g, 0, bidx(c))),
                  small(bias_c), small(bias_r), small(a_c), small(a_r),
                  pl.BlockSpec((1, SSD_GW, SSD_Q), lambda bi, g, c: (g, 0, 0))],
        out_specs=[pl.BlockSpec((1, SSD_GW, rows), fx), pl.BlockSpec((1, SSD_GW, rows), bx)],
        out_shape=[jax.ShapeDtypeStruct((b, SSD_INNER, t), out_dtype)] * 2,
        scratch_shapes=[pltpu.VMEM((SSD_GW, SSD_N), F32), pltpu.VMEM((SSD_GW, SSD_N), F32)],
        compiler_params=_params("parallel", "parallel", "arbitrary"),
        name="ssd_scan",
    )(xt, bc, bc, xt, bc, bc, dt_col, dt_row, dt_col, dt_row, bias_c, bias_r, a_c, a_r, d_rows)


MERGE_TN = 512


def _merge_kernel(ya_ref, ga_ref, yb_ref, gb_ref, yf_ref, ybk_ref, z_ref, nrm_ref,
                  mg_ref, wa_ref, wb_ref, wc_ref, u_ref, c_sc):
    a_in = (ya_ref[0].astype(F32) * _silu(ga_ref[0].astype(F32))).astype(BF16)
    b_in = (yb_ref[0].astype(F32) * _silu(gb_ref[0].astype(F32))).astype(BF16)
    v = (yf_ref[0].astype(F32) + ybk_ref[0].astype(F32)).T * _silu(z_ref[0].astype(F32))
    for g in range(SSD_G):
        sl = slice(g * SSD_GW, (g + 1) * SSD_GW)
        c_sc[:, sl] = (_rms(v[:, sl]) * nrm_ref[:, sl]).astype(BF16)
    c_in = c_sc[...]
    d = u_ref.shape[-1]
    for j in range(0, d, MERGE_TN):
        sl = slice(j, j + MERGE_TN)
        br_a = jnp.dot(a_in, wa_ref[:, sl], preferred_element_type=F32)
        br_b = jnp.dot(b_in, wb_ref[:, sl], preferred_element_type=F32)
        br_c = jnp.dot(c_in, wc_ref[:, sl], preferred_element_type=F32)
        u = (jax.nn.sigmoid(mg_ref[0, :, j:j + MERGE_TN].astype(F32)) * br_a
             + jax.nn.sigmoid(mg_ref[0, :, d + j:d + j + MERGE_TN].astype(F32)) * br_b
             + jax.nn.sigmoid(mg_ref[0, :, 2 * d + j:2 * d + j + MERGE_TN].astype(F32)) * br_c)
        u_ref[0, :, sl] = u.astype(u_ref.dtype)


def _merge(ya, yb, yf, ybk, p3, ssd_norm, wa, wb, wc, row_off):
    b, rows, _ = ya.shape
    tm = _tile(rows, 256)
    d = D_MODEL
    ro = row_off // tm
    assert P_OFF['mg'] == 0

    def col(name, width):
        blk = P_OFF[name] // width
        return pl.BlockSpec((1, tm, width), lambda bi, i: (bi, i + ro, blk))

    loc = lambda w: pl.BlockSpec((1, tm, w), lambda bi, i: (bi, i, 0))
    glob = lambda w: pl.BlockSpec((1, tm, w), lambda bi, i: (bi, i + ro, 0))
    globt = pl.BlockSpec((1, SSD_INNER, tm), lambda bi, i: (bi, 0, i + ro))
    wspec = lambda k: pl.BlockSpec((k, d), lambda bi, i: (0, 0), pipeline_mode=pl.Buffered(1))
    return pl.pallas_call(
        _merge_kernel,
        grid=(b, rows // tm),
        in_specs=[loc(MLA_WIDTH), col('ga', MLA_WIDTH), loc(GQA_WIDTH), col('gb', GQA_WIDTH),
                  globt, globt, col('z', SSD_INNER),
                  pl.BlockSpec((1, SSD_INNER), lambda bi, i: (0, 0)),
                  glob(N_BRANCH * d), wspec(MLA_WIDTH), wspec(GQA_WIDTH), wspec(SSD_INNER)],
        out_specs=pl.BlockSpec((1, tm, d), lambda bi, i: (bi, i, 0)),
        out_shape=jax.ShapeDtypeStruct((b, rows, d), BF16),
        scratch_shapes=[pltpu.VMEM((tm, SSD_INNER), BF16)],
        compiler_params=_params("parallel", "parallel"),
        name="merge",
    )(ya, p3, yb, p3, yf, ybk, p3, ssd_norm.reshape(1, SSD_INNER), p3, wa, wb, wc)


def _out_ln_kernel(u_ref, w_ref, *rest, with_next, ctx_tiles, ro, split):
    if split:
        ctx_ref, x_ref, gate_ref, g_ref, b_ref, *rest = rest
        res = _residual_rows(ctx_ref, x_ref, ctx_tiles, ro)
    else:
        x_ref, gate_ref, g_ref, b_ref, *rest = rest
        res = x_ref[0]
    out = jnp.dot(u_ref[0], w_ref[...], preferred_element_type=F32)
    r = DEEPNORM_ALPHA * res + gate_ref[0] * out
    xn = _layer_norm(r) * g_ref[...] + b_ref[...]
    if with_next:
        sh_ref, sc_ref, xo_ref, xm_ref = rest
        xo_ref[0] = xn
        xm_ref[0] = (_layer_norm(xn) * (1.0 + sc_ref[0]) + sh_ref[0]).astype(xm_ref.dtype)
    else:
        (xo_ref,) = rest
        xo_ref[0] = xn


def _out_ln(u, w_out, xc, tab, ln_g, ln_b, next_tab, nb, ctx_len, row_off):
    b, rows, d = u.shape
    tm = _tile(ctx_len, 256)
    ro = row_off // tm
    ctx_tiles = ctx_len // tm
    row = _mod_row_index(nb, ctx_tiles)
    with_next = next_tab is not None
    split = isinstance(xc, tuple)
    loc = pl.BlockSpec((1, tm, d), lambda bi, i: (bi, i, 0))
    vec = pl.BlockSpec((1, d), lambda bi, i: (0, 0))
    res_specs = (_residual_specs(tm, d, ctx_tiles, ro) if split
                 else [pl.BlockSpec((1, tm, d), lambda bi, i: (bi, i + ro, 0))])
    in_specs = [loc, pl.BlockSpec((d, d), lambda bi, i: (0, 0))] + res_specs + [
        pl.BlockSpec((1, 1, d), lambda bi, i: (row(bi, i + ro), 0, 2)), vec, vec]
    args = [u, w_out] + (list(xc) if split else [xc]) + [tab, ln_g.reshape(1, d), ln_b.reshape(1, d)]
    out_specs = [loc]
    out_shape = [jax.ShapeDtypeStruct((b, rows, d), F32)]
    if with_next:
        in_specs += [pl.BlockSpec((1, 1, d), lambda bi, i: (row(bi, i + ro), 0, 0)),
                     pl.BlockSpec((1, 1, d), lambda bi, i: (row(bi, i + ro), 0, 1))]
        args += [next_tab, next_tab]
        out_specs.append(loc)
        out_shape.append(jax.ShapeDtypeStruct((b, rows, d), BF16))
    return pl.pallas_call(
        functools.partial(_out_ln_kernel, with_next=with_next, ctx_tiles=ctx_tiles, ro=ro, split=split),
        grid=(b, rows // tm),
        in_specs=in_specs,
        out_specs=out_specs,
        out_shape=out_shape,
        compiler_params=_params("parallel", "parallel"),
        name="out_ln",
    )(*args)


def _rope_angles(rows, dim):
    row, col = jnp.meshgrid(jnp.arange(rows, dtype=F32), jnp.arange(GRID_W, dtype=F32), indexing='ij')
    half = dim // 2
    inv_freq = ROPE_THETA ** (-jnp.arange(0, half, 2, dtype=F32) / half)
    ang_r = row.reshape(-1, 1) * inv_freq
    ang_c = col.reshape(-1, 1) * inv_freq
    return jnp.concatenate([ang_r, ang_r, ang_c, ang_c], axis=-1)


def _rope_tables(seq, ctx_len, dim):
    ang = _rope_angles(seq // GRID_W, dim)
    cos = jnp.concatenate([jnp.ones((ctx_len, dim), F32), jnp.cos(ang)], axis=0)
    sin = jnp.concatenate([jnp.zeros((ctx_len, dim), F32), jnp.sin(ang)], axis=0)
    return cos, sin


def _roll_tables(cos, sin, dim, scale):
    t = cos.shape[0]
    quarter = dim // 4
    first = (jnp.arange(dim) % (2 * quarter)) < quarter
    s1 = jnp.where(first, -sin, 0.0)
    s2 = jnp.where(first, 0.0, sin)
    pad = lambda a: jnp.pad(a * scale, ((0, 0), (0, 128 - dim)))
    return pad(cos), pad(s1), pad(s2)


def _rot_matrix(dim):
    quarter = dim // 4
    r = np.zeros((dim, dim), np.float32)
    for i in range(dim):
        blk = i // quarter
        if blk % 2 == 0:
            r[i + quarter, i] = -1.0
        else:
            r[i - quarter, i] = 1.0
    return jnp.asarray(r)


def _mla_q_weights(w_uq):
    w = w_uq.reshape(MLA_Q_LORA, MLA_HEADS, MLA_QK)
    nope, pe = w[..., :MLA_NOPE], w[..., MLA_NOPE:]
    pe_rot = jnp.einsum('khd,de->khe', pe, _rot_matrix(MLA_ROPE), precision=HIGHEST)
    zpad = jnp.zeros((MLA_Q_LORA, MLA_HEADS, MLA_QK_PAD - MLA_QK), F32)
    main = jnp.concatenate([nope, pe, zpad], axis=-1)
    rot = jnp.concatenate([jnp.zeros_like(nope), pe_rot, zpad], axis=-1)
    return (main.reshape(MLA_Q_LORA, -1).astype(BF16), rot.reshape(MLA_Q_LORA, -1).astype(BF16))


def _permute_w_in(w):
    parts = [w[:, IN_OFFSETS[n]:IN_OFFSETS[n] + IN_WIDTHS[n]] for n in P_ORDER]
    parts.append(jnp.zeros((w.shape[0], P_WIDTH - P_USED), w.dtype))
    return jnp.concatenate(parts, axis=1).astype(BF16)


P_DTYPE = BF16
Y_DTYPE = BF16


def kernel(x, c, ctx, c_ctx, w_mod, b_mod, w_in, mla_q_norm, mla_w_uq, mla_kv_norm, mla_w_ukv,
           gqa_q_norm, gqa_k_norm, ssd_conv_w, ssd_conv_b, ssd_a_log, ssd_dt_bias, ssd_d, ssd_norm,
           w_br_a, w_br_b, w_br_c, w_out, ln_g, ln_b):
    nb, seq, d = x.shape
    ctx_len = ctx.shape[1]
    t = ctx_len + seq
    depth = w_in.shape[0]
    assert d == D_MODEL and nb < 8 and seq % GRID_W == 0
    assert ctx_len % SSD_Q == 0 and seq % SSD_Q == 0

    cos_a, sin_a = _rope_tables(seq, ctx_len, MLA_ROPE)
    cos_b, sin_b = _rope_tables(seq, ctx_len, GQA_DIM)
    sq = MLA_QK ** -0.5 * LOG2E
    zq = jnp.zeros((t, MLA_QK_PAD - MLA_QK), F32)
    cosq = jnp.concatenate([jnp.full((t, MLA_NOPE), sq, F32), cos_a * sq, zq], axis=1)
    sinq = jnp.concatenate([jnp.zeros((t, MLA_NOPE), F32), sin_a * sq, zq], axis=1)
    mla_tabs = (cosq, sinq) + _roll_tables(cos_a, sin_a, MLA_ROPE, 1.0)
    gqa_tabs = (_roll_tables(cos_b, sin_b, GQA_DIM, GQA_DIM ** -0.5 * LOG2E)
                + _roll_tables(cos_b, sin_b, GQA_DIM, 1.0))

    c_rows = jnp.zeros((8, d), F32).at[:nb].set(c).at[nb].set(c_ctx)
    tabs = [_mod_rows(c_rows, w_mod[l], b_mod[l]).reshape(8, 1, 3 * d) for l in range(depth)]

    xc = (ctx, x)
    xm = _ln_mod(ctx, x, tabs[0], nb)

    for l in range(depth):
        last = l == depth - 1
        wp = _permute_w_in(w_in[l])
        p2, krdt = _in_proj(xm.reshape(nb * t, d), wp, P_DTYPE)
        p3 = p2.reshape(nb, t, P_WIDTH)

        wqm, wqr = _mla_q_weights(mla_w_uq[l])
        qa, ka, va = _mla_prep(p3, mla_q_norm[l].reshape(1, -1), mla_kv_norm[l].reshape(1, -1),
                               wqm, wqr, mla_w_ukv[l].astype(BF16), mla_tabs, ctx_len)
        ya = _attention(qa, ka, va, ctx_len, 256, 2, not last, Y_DTYPE)
        qb, kb, vb = _gqa_prep(p3, gqa_q_norm[l].reshape(1, -1), gqa_k_norm[l].reshape(1, -1),
                               gqa_tabs, ctx_len)
        yb = _attention(qb, kb, vb, ctx_len, 128, 1, not last, Y_DTYPE)

        xconv_t, bconv = _conv(p3, ssd_conv_w[l], ssd_conv_b[l], ctx_len, Y_DTYPE)
        dt5 = krdt[:, 64:].reshape(nb, t, 2, SSD_G, SSD_E)
        dt_col = jnp.transpose(dt5, (0, 2, 3, 1, 4))
        dt_row = jnp.transpose(dt5, (0, 2, 3, 4, 1))
        bias = ssd_dt_bias[l].astype(F32).reshape(2, SSD_G, SSD_E)
        a = -jnp.exp(ssd_a_log[l].astype(F32)).reshape(2, SSD_G, SSD_E)
        d_rows = jnp.broadcast_to(jnp.repeat(ssd_d[l].astype(F32), SSD_P).reshape(SSD_G, SSD_GW, 1),
                                  (SSD_G, SSD_GW, SSD_Q))
        yf, ybk = _ssd(xconv_t, bconv, dt_col, dt_row, bias[:, :, None, :], bias[:, :, :, None],
                       a[:, :, None, :], a[:, :, :, None], d_rows, ctx_len, Y_DTYPE)

        row_off = ctx_len if last else 0
        u = _merge(ya, yb, yf, ybk, p3, ssd_norm[l], w_br_a[l].astype(BF16), w_br_b[l].astype(BF16),
                   w_br_c[l].astype(BF16), row_off)
        if last:
            (xo,) = _out_ln(u, w_out[l].astype(BF16), xc, tabs[l], ln_g[l], ln_b[l], None, nb,
                            ctx_len, row_off)
            return xo
        xc, xm = _out_ln(u, w_out[l].astype(BF16), xc, tabs[l], ln_g[l], ln_b[l], tabs[l + 1], nb,
                         ctx_len, row_off)
```

```python
import functools
import math

import numpy as np
import jax
import jax.numpy as jnp
from jax import lax
from jax.experimental import pallas as pl
from jax.experimental.pallas import tpu as pltpu

F32 = jnp.float32
BF16 = jnp.bfloat16
HIGHEST = lax.Precision.HIGHEST

D_MODEL = 2048
DEPTH = 2
GRID_W = 64
ROPE_THETA = 10000.0
EPS = 1e-6

MLA_HEADS = 8
MLA_Q_LORA = 512
MLA_KV_LORA = 256
MLA_NOPE = 128
MLA_ROPE = 64
MLA_V = 128
MLA_QK = MLA_NOPE + MLA_ROPE
MLA_QK_PAD = 256
MLA_WIDTH = MLA_HEADS * MLA_V

GQA_HEADS = 8
GQA_KV_HEADS = 2
GQA_GROUP = GQA_HEADS // GQA_KV_HEADS
GQA_DIM = 128
GQA_WIDTH = GQA_HEADS * GQA_DIM
GQA_KV_WIDTH = GQA_KV_HEADS * GQA_DIM

SSD_INNER = D_MODEL
SSD_P = 64
SSD_HEADS = SSD_INNER // SSD_P
SSD_G = 4
SSD_E = SSD_HEADS // SSD_G
SSD_N = 128
SSD_CONV = 5
SSD_Q = 128
SSD_ROWS = 256
SSD_GW = SSD_E * SSD_P
SSD_CONV_DIM = SSD_INNER + 2 * SSD_G * SSD_N

N_BRANCH = 3
IN_SPLITS = (MLA_Q_LORA, MLA_KV_LORA, MLA_ROPE, MLA_WIDTH, GQA_WIDTH, GQA_KV_WIDTH, GQA_KV_WIDTH,
             GQA_WIDTH, SSD_INNER, SSD_CONV_DIM, 2 * SSD_HEADS, N_BRANCH * D_MODEL)
IN_NAMES = ('cq', 'ckv', 'kr', 'ga', 'gq', 'gk', 'gv', 'gb', 'z', 'xbc', 'dtr', 'mg')
IN_OFFSETS = dict(zip(IN_NAMES, np.concatenate([[0], np.cumsum(IN_SPLITS)[:-1]]).tolist()))
IN_WIDTHS = dict(zip(IN_NAMES, IN_SPLITS))
P_ORDER = ('mg', 'z', 'xbc', 'ga', 'gq', 'gb', 'cq', 'ckv', 'gk', 'gv', 'kr', 'dtr')
P_OFF = {}
_o = 0
for _n in P_ORDER:
    P_OFF[_n] = _o
    _o += IN_WIDTHS[_n]
P_USED = _o
P_TN = 512
IN_TM = 2176
P_WIDTH = -(-P_USED // P_TN) * P_TN

DEEPNORM_ALPHA = (2 * DEPTH) ** 0.25

VMEM_LIMIT = 56 * 2 ** 20


def _params(*sem):
    return pltpu.CompilerParams(dimension_semantics=sem, vmem_limit_bytes=VMEM_LIMIT)


def _tile(n, target, align=8):
    t = min(n, target)
    while t > align and (n % t or t % align):
        t -= align
    assert n % t == 0, (n, target)
    return t


def _silu(v):
    return v * jax.nn.sigmoid(v)


def _softplus(v):
    return jnp.maximum(v, 0.0) + jnp.log1p(jnp.exp(-jnp.abs(v)))


def _layer_norm(v):
    mu = jnp.mean(v, axis=-1, keepdims=True)
    vc = v - mu
    var = jnp.mean(vc * vc, axis=-1, keepdims=True)
    return vc * lax.rsqrt(var + EPS)


def _rms(v):
    return v * lax.rsqrt(jnp.mean(v * v, axis=-1, keepdims=True) + EPS)


def _mod_kernel(c_ref, w_ref, b_ref, o_ref):
    a = _silu(c_ref[...]).astype(BF16)
    o_ref[...] = jnp.dot(a, w_ref[...].astype(BF16), preferred_element_type=F32) + b_ref[...]


def _mod_rows(c_rows, w_mod, b_mod):
    r, d = c_rows.shape
    n = w_mod.shape[1]
    tn = 512
    return pl.pallas_call(
        _mod_kernel,
        grid=(n // tn,),
        in_specs=[pl.BlockSpec((r, d), lambda j: (0, 0)),
                  pl.BlockSpec((d, tn), lambda j: (0, j)),
                  pl.BlockSpec((1, tn), lambda j: (0, j))],
        out_specs=pl.BlockSpec((r, tn), lambda j: (0, j)),
        out_shape=jax.ShapeDtypeStruct((r, n), F32),
        compiler_params=_params("arbitrary"),
        name="mod_rows",
    )(c_rows, w_mod, b_mod.reshape(1, n))


def _residual_rows(ctx_ref, x_ref, ctx_tiles, ro):
    return jnp.where(pl.program_id(1) + ro < ctx_tiles, ctx_ref[0], x_ref[0])


def _residual_specs(tm, d, ctx_tiles, ro):
    return [pl.BlockSpec((1, tm, d), lambda bi, i: (bi, jnp.minimum(i + ro, ctx_tiles - 1), 0)),
            pl.BlockSpec((1, tm, d), lambda bi, i: (bi, jnp.maximum(i + ro - ctx_tiles, 0), 0))]


def _ln_mod_kernel(ctx_ref, x_ref, sh_ref, sc_ref, o_ref, *, ctx_tiles):
    y = _layer_norm(_residual_rows(ctx_ref, x_ref, ctx_tiles, 0))
    o_ref[0] = (y * (1.0 + sc_ref[0]) + sh_ref[0]).astype(o_ref.dtype)


def _mod_row_index(nb, ctx_tiles):
    return lambda b, i: jnp.where(i < ctx_tiles, nb, b)


def _ln_mod(ctx, x, tab, nb):
    b, seq, d = x.shape
    ctx_len = ctx.shape[1]
    tm = _tile(ctx_len, 256)
    ctx_tiles = ctx_len // tm
    row = _mod_row_index(nb, ctx_tiles)
    return pl.pallas_call(
        functools.partial(_ln_mod_kernel, ctx_tiles=ctx_tiles),
        grid=(b, (ctx_len + seq) // tm),
        in_specs=_residual_specs(tm, d, ctx_tiles, 0)
        + [pl.BlockSpec((1, 1, d), lambda bi, i: (row(bi, i), 0, 0)),
           pl.BlockSpec((1, 1, d), lambda bi, i: (row(bi, i), 0, 1))],
        out_specs=pl.BlockSpec((1, tm, d), lambda bi, i: (bi, i, 0)),
        out_shape=jax.ShapeDtypeStruct((b, ctx_len + seq, d), BF16),
        compiler_params=_params("parallel", "parallel"),
        name="ln_mod",
    )(ctx, x, tab, tab)


def _in_proj_kernel(x_ref, w_ref, o_ref, dt_ref, *, dt_tile, dt_col):
    acc = jnp.dot(x_ref[...], w_ref[...], preferred_element_type=F32)
    o_ref[...] = acc.astype(o_ref.dtype)

    @pl.when(pl.program_id(1) == dt_tile)
    def _():
        dt_ref[...] = acc[:, dt_col:dt_col + 128]


def _in_proj(xm2, wp, out_dtype):
    m, k = xm2.shape
    n = wp.shape[1]
    tm = _tile(m, IN_TM)
    tn = P_TN
    dt_tile, dt_col = divmod(P_OFF['kr'], tn)
    assert P_OFF['dtr'] == P_OFF['kr'] + 64 and dt_col % 128 == 0
    return pl.pallas_call(
        functools.partial(_in_proj_kernel, dt_tile=dt_tile, dt_col=dt_col),
        grid=(m // tm, n // tn),
        in_specs=[pl.BlockSpec((tm, k), lambda i, j: (i, 0)),
                  pl.BlockSpec((k, tn), lambda i, j: (0, j))],
        out_specs=[pl.BlockSpec((tm, tn), lambda i, j: (i, j)),
                   pl.BlockSpec((tm, 128), lambda i, j: (i, 0))],
        out_shape=[jax.ShapeDtypeStruct((m, n), out_dtype), jax.ShapeDtypeStruct((m, 128), F32)],
        compiler_params=_params("parallel", "arbitrary"),
        name="in_proj",
    )(xm2, wp)


def _mla_prep_kernel(cq_ref, ckv_ref, kr_ref, qn_ref, kvn_ref, wqm_ref, wqr_ref, wkv_ref,
                     cosq_ref, sinq_ref, ck_ref, s1_ref, s2_ref, qa_ref, ka_ref, va_ref):
    cqn = (_rms(cq_ref[0].astype(F32)) * qn_ref[...]).astype(BF16)
    qm = jnp.dot(cqn, wqm_ref[...], preferred_element_type=F32)
    qr = jnp.dot(cqn, wqr_ref[...], preferred_element_type=F32)
    cosq = cosq_ref[...]
    sinq = sinq_ref[...]
    for h in range(MLA_HEADS):
        sl = slice(h * MLA_QK_PAD, (h + 1) * MLA_QK_PAD)
        qa_ref[0, h] = (qm[:, sl] * cosq + qr[:, sl] * sinq).T.astype(qa_ref.dtype)
    ckvn = (_rms(ckv_ref[0].astype(F32)) * kvn_ref[...]).astype(BF16)
    kv = jnp.dot(ckvn, wkv_ref[...], preferred_element_type=F32)
    kr = kr_ref[0].astype(F32)
    kpe = (kr * ck_ref[...] + pltpu.roll(kr, 128 - 16, axis=1) * s1_ref[...]
           + pltpu.roll(kr, 16, axis=1) * s2_ref[...]).astype(ka_ref.dtype)
    for h in range(MLA_HEADS):
        base = h * (MLA_NOPE + MLA_V)
        ka_ref[0, h, :, 0:MLA_NOPE] = kv[:, base:base + MLA_NOPE].astype(ka_ref.dtype)
        ka_ref[0, h, :, MLA_NOPE:MLA_QK_PAD] = kpe
        va_ref[0, h] = kv[:, base + MLA_NOPE:base + MLA_NOPE + MLA_V].T.astype(va_ref.dtype)


def _mla_prep(p3, qn, kvn, wqm, wqr, wkv, tabs, ctx_len):
    b, t, _ = p3.shape
    tm = _tile(ctx_len, 256)
    cosq, sinq, ck, s1, s2 = tabs

    def col(name, width):
        blk = P_OFF[name] // width
        return pl.BlockSpec((1, tm, width), lambda bi, i: (bi, i, blk))

    def full(a):
        return pl.BlockSpec(a.shape, lambda bi, i: (0,) * a.ndim)

    def rows(a):
        return pl.BlockSpec((tm, a.shape[1]), lambda bi, i: (i, 0))

    hm = lambda w: pl.BlockSpec((1, MLA_HEADS, tm, w), lambda bi, i: (bi, 0, i, 0))
    return pl.pallas_call(
        _mla_prep_kernel,
        grid=(b, t // tm),
        in_specs=[col('cq', MLA_Q_LORA), col('ckv', MLA_KV_LORA), col('kr', 128),
                  full(qn), full(kvn), full(wqm), full(wqr), full(wkv),
                  rows(cosq), rows(sinq), rows(ck), rows(s1), rows(s2)],
        out_specs=[pl.BlockSpec((1, MLA_HEADS, MLA_QK_PAD, tm), lambda bi, i: (bi, 0, 0, i)),
                   hm(MLA_QK_PAD),
                   pl.BlockSpec((1, MLA_HEADS, MLA_V, tm), lambda bi, i: (bi, 0, 0, i))],
        out_shape=[jax.ShapeDtypeStruct((b, MLA_HEADS, MLA_QK_PAD, t), BF16),
                   jax.ShapeDtypeStruct((b, MLA_HEADS, t, MLA_QK_PAD), BF16),
                   jax.ShapeDtypeStruct((b, MLA_HEADS, MLA_V, t), BF16)],
        compiler_params=_params("parallel", "parallel"),
        name="mla_prep",
    )(p3, p3, p3, qn, kvn, wqm, wqr, wkv, cosq, sinq, ck, s1, s2)


def _rope128(y, c, s1, s2):
    return y * c + pltpu.roll(y, 128 - 32, axis=1) * s1 + pltpu.roll(y, 32, axis=1) * s2


def _gqa_prep_kernel(gq_ref, gk_ref, gv_ref, qn_ref, kn_ref, cq_ref, s1q_ref, s2q_ref,
                     ck_ref, s1k_ref, s2k_ref, qb_ref, kb_ref, vb_ref):
    gq = gq_ref[0].astype(F32)
    for h in range(GQA_HEADS):
        y = _rms(gq[:, h * GQA_DIM:(h + 1) * GQA_DIM]) * qn_ref[...]
        qb_ref[0, h] = _rope128(y, cq_ref[...], s1q_ref[...], s2q_ref[...]).T.astype(qb_ref.dtype)
    gk = gk_ref[0].astype(F32)
    gv = gv_ref[0].astype(F32)
    for h in range(GQA_KV_HEADS):
        y = _rms(gk[:, h * GQA_DIM:(h + 1) * GQA_DIM]) * kn_ref[...]
        kb_ref[0, h] = _rope128(y, ck_ref[...], s1k_ref[...], s2k_ref[...]).astype(kb_ref.dtype)
        vb_ref[0, h] = gv[:, h * GQA_DIM:(h + 1) * GQA_DIM].T.astype(vb_ref.dtype)


def _gqa_prep(p3, qn, kn, tabs, ctx_len):
    b, t, _ = p3.shape
    tm = _tile(ctx_len, 256)

    def col(name, width):
        blk = P_OFF[name] // width
        return pl.BlockSpec((1, tm, width), lambda bi, i: (bi, i, blk))

    def full(a):
        return pl.BlockSpec(a.shape, lambda bi, i: (0,) * a.ndim)

    def rows(a):
        return pl.BlockSpec((tm, a.shape[1]), lambda bi, i: (i, 0))

    hm = lambda nh: pl.BlockSpec((1, nh, tm, GQA_DIM), lambda bi, i: (bi, 0, i, 0))
    hmt = lambda nh: pl.BlockSpec((1, nh, GQA_DIM, tm), lambda bi, i: (bi, 0, 0, i))
    return pl.pallas_call(
        _gqa_prep_kernel,
        grid=(b, t // tm),
        in_specs=[col('gq', GQA_WIDTH), col('gk', GQA_KV_WIDTH), col('gv', GQA_KV_WIDTH),
                  full(qn), full(kn)] + [rows(a) for a in tabs],
        out_specs=[hmt(GQA_HEADS), hm(GQA_KV_HEADS), hmt(GQA_KV_HEADS)],
        out_shape=[jax.ShapeDtypeStruct((b, GQA_HEADS, GQA_DIM, t), BF16),
                   jax.ShapeDtypeStruct((b, GQA_KV_HEADS, t, GQA_DIM), BF16),
                   jax.ShapeDtypeStruct((b, GQA_KV_HEADS, GQA_DIM, t), BF16)],
        compiler_params=_params("parallel", "parallel"),
        name="gqa_prep",
    )(p3, p3, p3, qn, kn, *tabs)


ATTN_KC = 256
ATTN_QB = 128
LOG2E = math.log2(math.e)


def _attn_logits(qts, k_ref, s_buf, nkeys):
    nqb = qts[0].shape[1] // ATTN_QB
    for j, qt in enumerate(qts):
        st = jnp.dot(k_ref[0, j, 0:nkeys, :], qt, preferred_element_type=F32)
        for b in range(nqb):
            s_buf[j * nqb + b, 0:nkeys, :] = st[:, b * ATTN_QB:(b + 1) * ATTN_QB]


def _attn_softmax_slab(s_buf, p_buf, l_buf, b, nkeys):
    parts = [jnp.max(s_buf[b, k0:k0 + ATTN_KC, :].reshape(ATTN_KC // 64, 8, 8, ATTN_QB), axis=0)
             for k0 in range(0, nkeys, ATTN_KC)]
    while len(parts) > 1:
        parts = [jnp.maximum(parts[i], parts[i + 1]) if i + 1 < len(parts) else parts[i]
                 for i in range(0, len(parts), 2)]
    m = jnp.max(parts[0], axis=(0, 1), keepdims=True)[0]
    lacc = jnp.zeros((8, 8, ATTN_QB), F32)
    for k0 in range(0, nkeys, ATTN_KC):
        p = jnp.exp2(s_buf[b, k0:k0 + ATTN_KC, :] - m)
        lacc = lacc + jnp.sum(p.reshape(ATTN_KC // 64, 8, 8, ATTN_QB), axis=0)
        p_buf[b, k0:k0 + ATTN_KC, :] = p.astype(p_buf.dtype)
    l = jnp.sum(lacc, axis=(0, 1), keepdims=True)[0]
    l_buf[b] = jnp.broadcast_to(1.0 / l, (8, ATTN_QB))


def _attn_output(vt_ref, p_buf, l_buf, o_ref, nkeys, hps, group, tq):
    dv = vt_ref.shape[2]
    nqb = group * tq // ATTN_QB
    for j in range(hps):
        slabs = range(j * nqb, (j + 1) * nqb)
        pt = jnp.concatenate([p_buf[b, 0:nkeys, :] for b in slabs], axis=1)
        ot = jnp.dot(vt_ref[0, j, :, 0:nkeys], pt, preferred_element_type=F32)
        ot = ot * jnp.concatenate([l_buf[b, 0:1, :] for b in slabs], axis=1)
        for g in range(group):
            c0 = (j * group + g) * dv
            o_ref[0, :, c0:c0 + dv] = ot[:, g * tq:(g + 1) * tq].T.astype(o_ref.dtype)


def _attn_queries(qt_ref, hps, group):
    return [jnp.concatenate([qt_ref[0, j * group + g] for g in range(group)], axis=1)
            for j in range(hps)]


def _attn_ctx_kernel(qt_ref, k_ref, vt_ref, o_ref, s_sc, p_sc, l_sc, *, group, tq):
    hps, nkeys = k_ref.shape[1], k_ref.shape[2]
    nslab = hps * group * tq // ATTN_QB
    _attn_logits(_attn_queries(qt_ref, hps, group), k_ref, s_sc, nkeys)

    def slab(b, carry):
        _attn_softmax_slab(s_sc, p_sc, l_sc, b, nkeys)
        return carry

    lax.fori_loop(0, nslab, slab, 0)
    _attn_output(vt_ref, p_sc, l_sc, o_ref, nkeys, hps, group, tq)


def _attn_lat_kernel(qt_ref, k_ref, vt_ref, o_ref, s0, s1, p0, p1, l0, l1, *, group, tq):
    hps, nkeys = k_ref.shape[1], k_ref.shape[2]
    nslab = hps * group * tq // ATTN_QB
    g = pl.program_id(0)

    @pl.when(g == 0)
    def _():
        for buf in (s0, s1, p0, p1, l0, l1):
            buf[...] = jnp.zeros(buf.shape, buf.dtype)

    def step(s_a, s_b, p_b, p_c, l_b, l_c):
        _attn_logits(_attn_queries(qt_ref, hps, group), k_ref, s_a, nkeys)
        for b in range(nslab):
            _attn_softmax_slab(s_b, p_b, l_b, b, nkeys)
        _attn_output(vt_ref, p_c, l_c, o_ref, nkeys, hps, group, tq)

    pl.when(g % 2 == 0)(lambda: step(s0, s1, p1, p0, l1, l0))
    pl.when(g % 2 == 1)(lambda: step(s1, s0, p0, p1, l0, l1))


def _attention(qt, k, vt, ctx_len, tq, hps, with_ctx_queries, out_dtype):
    b, hq, dk, t = qt.shape
    hkv, dv = k.shape[1], vt.shape[2]
    group = hq // hkv
    tq = _tile(ctx_len, tq, 128)
    nslab = hps * group * tq // ATTN_QB
    ctx_tiles = ctx_len // tq
    nq = t // tq - ctx_tiles
    width = hps * group * dv
    assert t % ATTN_KC == 0 and ctx_len % ATTN_KC == 0 and hkv % hps == 0

    def scratch(nkeys):
        return [pltpu.VMEM((nslab, nkeys, ATTN_QB), F32), pltpu.VMEM((nslab, nkeys, ATTN_QB), BF16),
                pltpu.VMEM((nslab, 8, ATTN_QB), F32)]

    s_lat, p_lat, l_lat = scratch(t)
    nh = hkv // hps
    items = b * nh * nq

    def item(step):
        i = jnp.clip(step, 0, items - 1)
        return i // (nh * nq), (i // nq) % nh, i % nq

    def q_map(g):
        bi, h, qi = item(g)
        return bi, h, 0, qi + ctx_tiles

    def k_map(g):
        bi, h, _ = item(g)
        return bi, h, 0, 0

    def v_map(g):
        bi, h, _ = item(g - 2)
        return bi, h, 0, 0

    def o_map(g):
        bi, h, qi = item(g - 2)
        return bi, qi, h

    y_lat = pl.pallas_call(
        functools.partial(_attn_lat_kernel, group=group, tq=tq),
        grid=(items + 2,),
        in_specs=[pl.BlockSpec((1, hps * group, dk, tq), q_map),
                  pl.BlockSpec((1, hps, t, dk), k_map),
                  pl.BlockSpec((1, hps, dv, t), v_map)],
        out_specs=pl.BlockSpec((1, tq, width), o_map),
        out_shape=jax.ShapeDtypeStruct((b, nq * tq, hq * dv), out_dtype),
        scratch_shapes=[s_lat, s_lat, p_lat, p_lat, l_lat, l_lat],
        compiler_params=_params("arbitrary"),
        name="attention_lat_dk%d" % dk,
    )(qt, k, vt)
    if not with_ctx_queries:
        return y_lat
    y_ctx = pl.pallas_call(
        functools.partial(_attn_ctx_kernel, group=group, tq=tq),
        grid=(b, hkv // hps, ctx_tiles),
        in_specs=[pl.BlockSpec((1, hps * group, dk, tq), lambda bi, h, i: (bi, h, 0, i)),
                  pl.BlockSpec((1, hps, ctx_len, dk), lambda bi, h, i: (bi, h, 0, 0)),
                  pl.BlockSpec((1, hps, dv, ctx_len), lambda bi, h, i: (bi, h, 0, 0))],
        out_specs=pl.BlockSpec((1, tq, width), lambda bi, h, i: (bi, i, h)),
        out_shape=jax.ShapeDtypeStruct((b, ctx_len, hq * dv), out_dtype),
        scratch_shapes=scratch(ctx_len),
        compiler_params=_params("parallel", "parallel", "arbitrary"),
        name="attention_ctx_dk%d" % dk,
    )(qt, k, vt)
    return jnp.concatenate([y_ctx, y_lat], axis=1)


def _conv_kernel(x_ref, w_ref, b_ref, xt_ref, bc_ref, pad_sc, *, ctx_len, rows, x_blocks):
    t = x_ref.shape[1]
    nch = x_ref.shape[2]
    halo = 8
    segs = ((0, ctx_len), (ctx_len, t))
    zeros = jnp.zeros((halo, nch), F32)
    for si, (lo, hi) in enumerate(segs):
        pad_sc[lo + si * halo:lo + (si + 1) * halo, :] = zeros
        for r0 in range(lo, hi, rows):
            pad_sc[r0 + (si + 1) * halo:r0 + (si + 1) * halo + rows, :] = x_ref[0, r0:r0 + rows, :].astype(F32)
    pad_sc[t + 2 * halo:t + 3 * halo, :] = zeros
    w = w_ref[...]
    bias = b_ref[...]

    def emit(transposed):
        for si, (lo, hi) in enumerate(segs):
            for r0 in range(lo, hi, rows):
                base = r0 + (si + 1) * halo - SSD_CONV // 2
                acc = bias + w[0:1, :] * pad_sc[base:base + rows, :]
                for kk in range(1, SSD_CONV):
                    acc = acc + w[kk:kk + 1, :] * pad_sc[base + kk:base + kk + rows, :]
                y = _silu(acc)
                if transposed:
                    xt_ref[0, :, r0:r0 + rows] = y.T.astype(xt_ref.dtype)
                else:
                    bc_ref[0, r0:r0 + rows, :] = y.astype(bc_ref.dtype)

    is_x = pl.program_id(1) < x_blocks
    pl.when(is_x)(lambda: emit(True))
    pl.when(jnp.logical_not(is_x))(lambda: emit(False))


def _conv(p3, conv_w, conv_b, ctx_len, out_dtype):
    b, t, _ = p3.shape
    nch = 256
    rows = _tile(ctx_len, 256)
    blk0 = P_OFF['xbc'] // nch
    x_blocks = SSD_INNER // nch
    kern = functools.partial(_conv_kernel, ctx_len=ctx_len, rows=rows, x_blocks=x_blocks)
    return pl.pallas_call(
        kern,
        grid=(b, SSD_CONV_DIM // nch),
        in_specs=[pl.BlockSpec((1, t, nch), lambda bi, j: (bi, 0, blk0 + j)),
                  pl.BlockSpec((SSD_CONV, nch), lambda bi, j: (0, j)),
                  pl.BlockSpec((1, nch), lambda bi, j: (0, j))],
        out_specs=[pl.BlockSpec((1, nch, t), lambda bi, j: (bi, jnp.minimum(j, x_blocks - 1), 0)),
                   pl.BlockSpec((1, t, nch), lambda bi, j: (bi, 0, jnp.maximum(j - x_blocks, 0)))],
        out_shape=[jax.ShapeDtypeStruct((b, SSD_INNER, t), out_dtype),
                   jax.ShapeDtypeStruct((b, t, SSD_CONV_DIM - SSD_INNER), out_dtype)],
        scratch_shapes=[pltpu.VMEM((t + 24, nch), F32)],
        compiler_params=_params("parallel", "arbitrary"),
        name="ssd_conv",
    )(p3, conv_w, conv_b.reshape(1, SSD_CONV_DIM))


def _split3(v):
    hi = v.astype(BF16)
    r1 = v - hi.astype(F32)
    mid = r1.astype(BF16)
    lo = (r1 - mid.astype(F32)).astype(BF16)
    return hi, mid, lo


def _expand_rows(v):
    q = v.shape[1]
    return jnp.concatenate([jnp.broadcast_to(v[e:e + 1, :], (SSD_P, q)) for e in range(SSD_E)], axis=0)


def _ssd_direction(xt, bm, cm, dt_col_raw, dt_row_raw, bias_col, bias_row, a_col, a_row, h_ref,
                   backward):
    q = xt.shape[1]
    ri = lax.broadcasted_iota(jnp.int32, (q, q), 0)
    ci = lax.broadcasted_iota(jnp.int32, (q, q), 1)
    tri_col = jnp.where((ri <= ci) if backward else (ri >= ci), 1.0, 0.0).astype(BF16)
    tri_row = jnp.where((ri >= ci) if backward else (ri <= ci), 1.0, 0.0).astype(BF16)
    keep_t = (ci <= ri) if backward else (ci >= ri)

    dt_c = _softplus(dt_col_raw + bias_col)
    dt_r = _softplus(dt_row_raw + bias_row)
    cum_c = sum(jnp.dot(tri_col, part, preferred_element_type=F32) for part in _split3(dt_c * a_col))
    cum_r = sum(jnp.dot(part, tri_row, preferred_element_type=F32) for part in _split3(dt_r * a_row))
    total = jnp.broadcast_to(cum_r[:, 0:1] if backward else cum_r[:, q - 1:q], (SSD_E, q))

    bmb = bm.astype(BF16)
    cmb = cm.astype(BF16)
    nt = (((1,), (1,)), ((), ()))
    cbt = lax.dot_general(bmb, cmb, nt, preferred_element_type=F32)
    h = h_ref[...]
    y_off = lax.dot_general(h.astype(BF16), cmb, nt, preferred_element_type=F32) * _expand_rows(jnp.exp(cum_r))
    wgt = (xt * _expand_rows(dt_r * jnp.exp(total - cum_r))).astype(BF16)
    h_ref[...] = _expand_rows(jnp.exp(total)) * h + jnp.dot(wgt, bmb, preferred_element_type=F32)

    xdt = (xt * _expand_rows(dt_r)).astype(BF16)
    parts = []
    for e in range(SSD_E):
        seg = cum_r[e:e + 1, :] - cum_c[:, e:e + 1]
        dec = jnp.exp(jnp.where(keep_t, seg, -jnp.inf))
        parts.append(jnp.dot(xdt[e * SSD_P:(e + 1) * SSD_P, :], (cbt * dec).astype(BF16),
                             preferred_element_type=F32))
    return jnp.concatenate(parts, axis=0) + y_off


def _ssd_kernel(xf_ref, bf_ref, cf_ref, xb_ref, bb_ref, cb_ref, dcf_ref, drf_ref, dcb_ref, drb_ref,
                bias_c_ref, bias_r_ref, a_c_ref, a_r_ref, d_ref, yf_ref, yb_ref, hf_sc, hb_sc):
    @pl.when(pl.program_id(2) == 0)
    def _():
        hf_sc[...] = jnp.zeros(hf_sc.shape, F32)
        hb_sc[...] = jnp.zeros(hb_sc.shape, F32)

    nsub = xf_ref.shape[2] // SSD_Q
    for i in range(nsub):
        rf = slice(i * SSD_Q, (i + 1) * SSD_Q)
        rb = slice((nsub - 1 - i) * SSD_Q, (nsub - i) * SSD_Q)
        xt = xf_ref[0, :, rf].astype(F32)
        yf = _ssd_direction(xt, bf_ref[0, rf, :].astype(F32), cf_ref[0, rf, :].astype(F32),
                            dcf_ref[0, 0, 0, rf, :], drf_ref[0, 0, 0, :, rf],
                            bias_c_ref[0, 0], bias_r_ref[0, 0], a_c_ref[0, 0], a_r_ref[0, 0], hf_sc, False)
        yf_ref[0, :, rf] = (yf + d_ref[0] * xt).astype(yf_ref.dtype)
        yb = _ssd_direction(xb_ref[0, :, rb].astype(F32), bb_ref[0, rb, :].astype(F32),
                            cb_ref[0, rb, :].astype(F32), dcb_ref[0, 0, 0, rb, :], drb_ref[0, 0, 0, :, rb],
                            bias_c_ref[1, 0], bias_r_ref[1, 0], a_c_ref[1, 0], a_r_ref[1, 0], hb_sc, True)
        yb_ref[0, :, rb] = yb.astype(yb_ref.dtype)


def _ssd(xt, bc, dt_col, dt_row, bias_c, bias_r, a_c, a_r, d_rows, ctx_len, out_dtype):
    b, _, t = xt.shape
    assert SSD_Q == SSD_N
    rows = _tile(ctx_len, SSD_ROWS, SSD_Q)
    nblk = t // rows
    nctx = ctx_len // rows

    def bidx(c):
        return jnp.where(c < nctx, nctx - 1 - c, nblk - 1 - (c - nctx))

    fx = lambda bi, g, c: (bi, g, c)
    fb = lambda bi, g, c: (bi, c, g)
    fc = lambda bi, g, c: (bi, c, SSD_G + g)
    bx = lambda bi, g, c: (bi, g, bidx(c))
    bb = lambda bi, g, c: (bi, bidx(c), g)
    bcm = lambda bi, g, c: (bi, bidx(c), SSD_G + g)
    small = lambda a: pl.BlockSpec((2, 1) + a.shape[2:], lambda bi, g, c: (0, g, 0, 0))
    return pl.pallas_call(
        _ssd_kernel,
        grid=(b, SSD_G, nblk),
        in_specs=[pl.BlockSpec((1, SSD_GW, rows), fx), pl.BlockSpec((1, rows, SSD_N), fb),
                  pl.BlockSpec((1, rows, SSD_N), fc),
                  pl.BlockSpec((1, SSD_GW, rows), bx), pl.BlockSpec((1, rows, SSD_N), bb),
                  pl.BlockSpec((1, rows, SSD_N), bcm),
                  pl.BlockSpec((1, 1, 1, rows, SSD_E), lambda bi, g, c: (bi, 0, g, c, 0)),
                  pl.BlockSpec((1, 1, 1, SSD_E, rows), lambda bi, g, c: (bi, 0, g, 0, c)),
                  pl.BlockSpec((1, 1, 1, rows, SSD_E), lambda bi, g, c: (bi, 1, g, bidx(c), 0)),
                  pl.BlockSpec((1, 1, 1, SSD_E, rows), lambda bi, g, c: (bi, 1, g, 0, bidx(c))),
                  small(bias_c), small(bias_r), small(a_c), small(a_r),
                  pl.BlockSpec((1, SSD_GW, SSD_Q), lambda bi, g, c: (g, 0, 0))],
        out_specs=[pl.BlockSpec((1, SSD_GW, rows), fx), pl.BlockSpec((1, SSD_GW, rows), bx)],
        out_shape=[jax.ShapeDtypeStruct((b, SSD_INNER, t), out_dtype)] * 2,
        scratch_shapes=[pltpu.VMEM((SSD_GW, SSD_N), F32), pltpu.VMEM((SSD_GW, SSD_N), F32)],
        compiler_params=_params("parallel", "parallel", "arbitrary"),
        name="ssd_scan",
    )(xt, bc, bc, xt, bc, bc, dt_col, dt_row, dt_col, dt_row, bias_c, bias_r, a_c, a_r, d_rows)


MERGE_TN = 512


def _merge_kernel(ya_ref, ga_ref, yb_ref, gb_ref, yf_ref, ybk_ref, z_ref, nrm_ref,
                  mg_ref, wa_ref, wb_ref, wc_ref, u_ref, a0, a1, b0, b1, c0, c1):
    g = pl.program_id(0)

    @pl.when(g == 0)
    def _():
        for buf in (a0, a1, b0, b1, c0, c1):
            buf[...] = jnp.zeros(buf.shape, buf.dtype)

    def step(a_w, b_w, c_w, a_r, b_r, c_r):
        a_in, b_in, c_in = a_r[...], b_r[...], c_r[...]
        tm = a_w.shape[0]
        d = u_ref.shape[-1]
        every = (d // MERGE_TN) // (tm // 128)

        def prepare(rs):
            a_w[rs, :] = (ya_ref[0, rs, :].astype(F32) * _silu(ga_ref[0, rs, :].astype(F32))).astype(BF16)
            b_w[rs, :] = (yb_ref[0, rs, :].astype(F32) * _silu(gb_ref[0, rs, :].astype(F32))).astype(BF16)
            v = ((yf_ref[0, :, rs].astype(F32) + ybk_ref[0, :, rs].astype(F32)).T
                 * _silu(z_ref[0, rs, :].astype(F32)))
            for grp in range(SSD_G):
                sl = slice(grp * SSD_GW, (grp + 1) * SSD_GW)
                c_w[rs, sl] = (_rms(v[:, sl]) * nrm_ref[:, sl]).astype(BF16)

        for k, j in enumerate(range(0, d, MERGE_TN)):
            if k % every == 0:
                prepare(slice((k // every) * 128, (k // every + 1) * 128))
            sl = slice(j, j + MERGE_TN)
            br_a = jnp.dot(a_in, wa_ref[:, sl], preferred_element_type=F32)
            br_b = jnp.dot(b_in, wb_ref[:, sl], preferred_element_type=F32)
            br_c = jnp.dot(c_in, wc_ref[:, sl], preferred_element_type=F32)
            u = (jax.nn.sigmoid(mg_ref[0, :, j:j + MERGE_TN].astype(F32)) * br_a
                 + jax.nn.sigmoid(mg_ref[0, :, d + j:d + j + MERGE_TN].astype(F32)) * br_b
                 + jax.nn.sigmoid(mg_ref[0, :, 2 * d + j:2 * d + j + MERGE_TN].astype(F32)) * br_c)
            u_ref[0, :, sl] = u.astype(u_ref.dtype)

    pl.when(g % 2 == 0)(lambda: step(a0, b0, c0, a1, b1, c1))
    pl.when(g % 2 == 1)(lambda: step(a1, b1, c1, a0, b0, c0))


def _merge(ya, yb, yf, ybk, p3, ssd_norm, wa, wb, wc, row_off):
    b, rows, _ = ya.shape
    tm = _tile(rows, 256)
    d = D_MODEL
    ro = row_off // tm
    nt = rows // tm
    items = b * nt
    assert P_OFF['mg'] == 0

    def item(step):
        it = jnp.clip(step, 0, items - 1)
        return it // nt, it % nt

    def prep(col_blk, off):
        def index(g):
            bi, i = item(g)
            return bi, i + off, col_blk
        return index

    def prep_t(g):
        bi, i = item(g)
        return bi, 0, i + ro

    def proj(off):
        def index(g):
            bi, i = item(g - 1)
            return bi, i + off, 0
        return index

    col = lambda name, width: pl.BlockSpec((1, tm, width), prep(P_OFF[name] // width, ro))
    loc = lambda w: pl.BlockSpec((1, tm, w), prep(0, 0))
    globt = pl.BlockSpec((1, SSD_INNER, tm), prep_t)
    wspec = lambda k: pl.BlockSpec((k, d), lambda g: (0, 0), pipeline_mode=pl.Buffered(1))
    return pl.pallas_call(
        _merge_kernel,
        grid=(items + 1,),
        in_specs=[loc(MLA_WIDTH), col('ga', MLA_WIDTH), loc(GQA_WIDTH), col('gb', GQA_WIDTH),
                  globt, globt, col('z', SSD_INNER),
                  pl.BlockSpec((1, SSD_INNER), lambda g: (0, 0)),
                  pl.BlockSpec((1, tm, N_BRANCH * d), proj(ro)),
                  wspec(MLA_WIDTH), wspec(GQA_WIDTH), wspec(SSD_INNER)],
        out_specs=pl.BlockSpec((1, tm, d), proj(0)),
        out_shape=jax.ShapeDtypeStruct((b, rows, d), BF16),
        scratch_shapes=[pltpu.VMEM((tm, MLA_WIDTH), BF16)] * 2 + [pltpu.VMEM((tm, GQA_WIDTH), BF16)] * 2
        + [pltpu.VMEM((tm, SSD_INNER), BF16)] * 2,
        compiler_params=_params("arbitrary"),
        name="merge",
    )(ya, p3, yb, p3, yf, ybk, p3, ssd_norm.reshape(1, SSD_INNER), p3, wa, wb, wc)


OUT_LN_GROUPS = 8


def _out_ln_kernel(u_ref, w_ref, *rest, with_next, ctx_tiles, ro, split, nt):
    *rest, acc0, acc1 = rest
    g = pl.program_id(0)
    tile = (jnp.maximum(g - 1, 0) % nt) + ro
    if split:
        ctx_ref, x_ref, gate_ref, g_ref, b_ref, *rest = rest
    else:
        x_ref, gate_ref, g_ref, b_ref, *rest = rest

    @pl.when(g == 0)
    def _():
        acc0[...] = jnp.zeros(acc0.shape, F32)
        acc1[...] = jnp.zeros(acc1.shape, F32)

    def step(acc_a, acc_b):
        tm, d = acc_a.shape
        u = u_ref[0]
        is_ctx = tile < ctx_tiles
        for j in range(OUT_LN_GROUPS):
            cs = slice(j * d // OUT_LN_GROUPS, (j + 1) * d // OUT_LN_GROUPS)
            rs = slice(j * tm // OUT_LN_GROUPS, (j + 1) * tm // OUT_LN_GROUPS)
            acc_a[:, cs] = jnp.dot(u, w_ref[:, cs], preferred_element_type=F32)
            res = jnp.where(is_ctx, ctx_ref[0, rs, :], x_ref[0, rs, :]) if split else x_ref[0, rs, :]
            r = DEEPNORM_ALPHA * res + gate_ref[0] * acc_b[rs, :]
            xn = _layer_norm(r) * g_ref[...] + b_ref[...]
            if with_next:
                sh_ref, sc_ref, xo_ref, xm_ref = rest
                xo_ref[0, rs, :] = xn
                xm_ref[0, rs, :] = (_layer_norm(xn) * (1.0 + sc_ref[0]) + sh_ref[0]).astype(xm_ref.dtype)
            else:
                (xo_ref,) = rest
                xo_ref[0, rs, :] = xn

    pl.when(g % 2 == 0)(lambda: step(acc0, acc1))
    pl.when(g % 2 == 1)(lambda: step(acc1, acc0))


def _out_ln(u, w_out, xc, tab, ln_g, ln_b, next_tab, nb, ctx_len, row_off):
    b, rows, d = u.shape
    tm = _tile(ctx_len, 256)
    ro = row_off // tm
    nt = rows // tm
    items = b * nt
    ctx_tiles = ctx_len // tm
    row = _mod_row_index(nb, ctx_tiles)
    with_next = next_tab is not None
    split = isinstance(xc, tuple)

    def item(step):
        it = jnp.clip(step, 0, items - 1)
        return it // nt, it % nt

    def mm(g):
        bi, i = item(g)
        return bi, i, 0

    def ln(g):
        bi, i = item(g - 1)
        return bi, i, 0

    def ln_glob(g):
        bi, i = item(g - 1)
        return bi, i + ro, 0

    def mod(chunk):
        def index(g):
            bi, i = item(g - 1)
            return row(bi, i + ro), 0, chunk
        return pl.BlockSpec((1, 1, d), index)

    vec = pl.BlockSpec((1, d), lambda g: (0, 0))
    if split:
        def ctx_map(g):
            bi, i = item(g - 1)
            return bi, jnp.minimum(i + ro, ctx_tiles - 1), 0

        def lat_map(g):
            bi, i = item(g - 1)
            return bi, jnp.maximum(i + ro - ctx_tiles, 0), 0
        res_specs = [pl.BlockSpec((1, tm, d), ctx_map), pl.BlockSpec((1, tm, d), lat_map)]
    else:
        res_specs = [pl.BlockSpec((1, tm, d), ln_glob)]
    in_specs = [pl.BlockSpec((1, tm, d), mm), pl.BlockSpec((d, d), lambda g: (0, 0))] + res_specs + [
        mod(2), vec, vec]
    args = [u, w_out] + (list(xc) if split else [xc]) + [tab, ln_g.reshape(1, d), ln_b.reshape(1, d)]
    out_specs = [pl.BlockSpec((1, tm, d), ln)]
    out_shape = [jax.ShapeDtypeStruct((b, rows, d), F32)]
    if with_next:
        in_specs += [mod(0), mod(1)]
        args += [next_tab, next_tab]
        out_specs.append(pl.BlockSpec((1, tm, d), ln))
        out_shape.append(jax.ShapeDtypeStruct((b, rows, d), BF16))
    return pl.pallas_call(
        functools.partial(_out_ln_kernel, with_next=with_next, ctx_tiles=ctx_tiles, ro=ro, split=split,
                          nt=nt),
        grid=(items + 1,),
        in_specs=in_specs,
        out_specs=out_specs,
        out_shape=out_shape,
        scratch_shapes=[pltpu.VMEM((tm, d), F32), pltpu.VMEM((tm, d), F32)],
        compiler_params=_params("arbitrary"),
        name="out_ln",
    )(*args)


def _rope_angles(rows, dim):
    row, col = jnp.meshgrid(jnp.arange(rows, dtype=F32), jnp.arange(GRID_W, dtype=F32), indexing='ij')
    half = dim // 2
    inv_freq = ROPE_THETA ** (-jnp.arange(0, half, 2, dtype=F32) / half)
    ang_r = row.reshape(-1, 1) * inv_freq
    ang_c = col.reshape(-1, 1) * inv_freq
    return jnp.concatenate([ang_r, ang_r, ang_c, ang_c], axis=-1)


def _rope_tables(seq, ctx_len, dim):
    ang = _rope_angles(seq // GRID_W, dim)
    cos = jnp.concatenate([jnp.ones((ctx_len, dim), F32), jnp.cos(ang)], axis=0)
    sin = jnp.concatenate([jnp.zeros((ctx_len, dim), F32), jnp.sin(ang)], axis=0)
    return cos, sin


def _roll_tables(cos, sin, dim, scale):
    t = cos.shape[0]
    quarter = dim // 4
    first = (jnp.arange(dim) % (2 * quarter)) < quarter
    s1 = jnp.where(first, -sin, 0.0)
    s2 = jnp.where(first, 0.0, sin)
    pad = lambda a: jnp.pad(a * scale, ((0, 0), (0, 128 - dim)))
    return pad(cos), pad(s1), pad(s2)


def _rot_matrix(dim):
    quarter = dim // 4
    r = np.zeros((dim, dim), np.float32)
    for i in range(dim):
        blk = i // quarter
        if blk % 2 == 0:
            r[i + quarter, i] = -1.0
        else:
            r[i - quarter, i] = 1.0
    return jnp.asarray(r)


def _mla_q_weights(w_uq):
    w = w_uq.reshape(MLA_Q_LORA, MLA_HEADS, MLA_QK)
    nope, pe = w[..., :MLA_NOPE], w[..., MLA_NOPE:]
    pe_rot = jnp.einsum('khd,de->khe', pe, _rot_matrix(MLA_ROPE), precision=HIGHEST)
    zpad = jnp.zeros((MLA_Q_LORA, MLA_HEADS, MLA_QK_PAD - MLA_QK), F32)
    main = jnp.concatenate([nope, pe, zpad], axis=-1)
    rot = jnp.concatenate([jnp.zeros_like(nope), pe_rot, zpad], axis=-1)
    return (main.reshape(MLA_Q_LORA, -1).astype(BF16), rot.reshape(MLA_Q_LORA, -1).astype(BF16))


def _permute_w_in(w):
    parts = [w[:, IN_OFFSETS[n]:IN_OFFSETS[n] + IN_WIDTHS[n]] for n in P_ORDER]
    parts.append(jnp.zeros((w.shape[0], P_WIDTH - P_USED), w.dtype))
    return jnp.concatenate(parts, axis=1).astype(BF16)


P_DTYPE = BF16
Y_DTYPE = BF16


def kernel(x, c, ctx, c_ctx, w_mod, b_mod, w_in, mla_q_norm, mla_w_uq, mla_kv_norm, mla_w_ukv,
           gqa_q_norm, gqa_k_norm, ssd_conv_w, ssd_conv_b, ssd_a_log, ssd_dt_bias, ssd_d, ssd_norm,
           w_br_a, w_br_b, w_br_c, w_out, ln_g, ln_b):
    nb, seq, d = x.shape
    ctx_len = ctx.shape[1]
    t = ctx_len + seq
    depth = w_in.shape[0]
    assert d == D_MODEL and nb < 8 and seq % GRID_W == 0
    assert ctx_len % SSD_Q == 0 and seq % SSD_Q == 0

    cos_a, sin_a = _rope_tables(seq, ctx_len, MLA_ROPE)
    cos_b, sin_b = _rope_tables(seq, ctx_len, GQA_DIM)
    sq = MLA_QK ** -0.5 * LOG2E
    zq = jnp.zeros((t, MLA_QK_PAD - MLA_QK), F32)
    cosq = jnp.concatenate([jnp.full((t, MLA_NOPE), sq, F32), cos_a * sq, zq], axis=1)
    sinq = jnp.concatenate([jnp.zeros((t, MLA_NOPE), F32), sin_a * sq, zq], axis=1)
    mla_tabs = (cosq, sinq) + _roll_tables(cos_a, sin_a, MLA_ROPE, 1.0)
    gqa_tabs = (_roll_tables(cos_b, sin_b, GQA_DIM, GQA_DIM ** -0.5 * LOG2E)
                + _roll_tables(cos_b, sin_b, GQA_DIM, 1.0))

    c_rows = jnp.zeros((8, d), F32).at[:nb].set(c).at[nb].set(c_ctx)
    tabs = [_mod_rows(c_rows, w_mod[l], b_mod[l]).reshape(8, 1, 3 * d) for l in range(depth)]

    xc = (ctx, x)
    xm = _ln_mod(ctx, x, tabs[0], nb)

    for l in range(depth):
        last = l == depth - 1
        wp = _permute_w_in(w_in[l])
        p2, krdt = _in_proj(xm.reshape(nb * t, d), wp, P_DTYPE)
        p3 = p2.reshape(nb, t, P_WIDTH)

        wqm, wqr = _mla_q_weights(mla_w_uq[l])
        qa, ka, va = _mla_prep(p3, mla_q_norm[l].reshape(1, -1), mla_kv_norm[l].reshape(1, -1),
                               wqm, wqr, mla_w_ukv[l].astype(BF16), mla_tabs, ctx_len)
        ya = _attention(qa, ka, va, ctx_len, 256, 2, not last, Y_DTYPE)
        qb, kb, vb = _gqa_prep(p3, gqa_q_norm[l].reshape(1, -1), gqa_k_norm[l].reshape(1, -1),
                               gqa_tabs, ctx_len)
        yb = _attention(qb, kb, vb, ctx_len, 128, 1, not last, Y_DTYPE)

        xconv_t, bconv = _conv(p3, ssd_conv_w[l], ssd_conv_b[l], ctx_len, Y_DTYPE)
        dt5 = krdt[:, 64:].reshape(nb, t, 2, SSD_G, SSD_E)
        dt_col = jnp.transpose(dt5, (0, 2, 3, 1, 4))
        dt_row = jnp.transpose(dt5, (0, 2, 3, 4, 1))
        bias = ssd_dt_bias[l].astype(F32).reshape(2, SSD_G, SSD_E)
        a = -jnp.exp(ssd_a_log[l].astype(F32)).reshape(2, SSD_G, SSD_E)
        d_rows = jnp.broadcast_to(jnp.repeat(ssd_d[l].astype(F32), SSD_P).reshape(SSD_G, SSD_GW, 1),
                                  (SSD_G, SSD_GW, SSD_Q))
        yf, ybk = _ssd(xconv_t, bconv, dt_col, dt_row, bias[:, :, None, :], bias[:, :, :, None],
                       a[:, :, None, :], a[:, :, :, None], d_rows, ctx_len, Y_DTYPE)

        row_off = ctx_len if last else 0
        u = _merge(ya, yb, yf, ybk, p3, ssd_norm[l], w_br_a[l].astype(BF16), w_br_b[l].astype(BF16),
                   w_br_c[l].astype(BF16), row_off)
        if last:
            (xo,) = _out_ln(u, w_out[l].astype(BF16), xc, tabs[l], ln_g[l], ln_b[l], None, nb,
                            ctx_len, row_off)
            return xo
        xc, xm = _out_ln(u, w_out[l].astype(BF16), xc, tabs[l], ln_g[l], ln_b[l], tabs[l + 1], nb,
                         ctx_len, row_off)
```

```python
import functools
import math

import numpy as np
import jax
import jax.numpy as jnp
from jax import lax
from jax.experimental import pallas as pl
from jax.experimental.pallas import tpu as pltpu

F32 = jnp.float32
BF16 = jnp.bfloat16
HIGHEST = lax.Precision.HIGHEST

D_MODEL = 2048
DEPTH = 2
GRID_W = 64
ROPE_THETA = 10000.0
EPS = 1e-6

MLA_HEADS = 8
MLA_Q_LORA = 512
MLA_KV_LORA = 256
MLA_NOPE = 128
MLA_ROPE = 64
MLA_V = 128
MLA_QK = MLA_NOPE + MLA_ROPE
MLA_QK_PAD = 256
MLA_WIDTH = MLA_HEADS * MLA_V

GQA_HEADS = 8
GQA_KV_HEADS = 2
GQA_GROUP = GQA_HEADS // GQA_KV_HEADS
GQA_DIM = 128
GQA_WIDTH = GQA_HEADS * GQA_DIM
GQA_KV_WIDTH = GQA_KV_HEADS * GQA_DIM

SSD_INNER = D_MODEL
SSD_P = 64
SSD_HEADS = SSD_INNER // SSD_P
SSD_G = 4
SSD_E = SSD_HEADS // SSD_G
SSD_N = 128
SSD_CONV = 5
SSD_Q = 128
SSD_ROWS = 256
SSD_GW = SSD_E * SSD_P
SSD_CONV_DIM = SSD_INNER + 2 * SSD_G * SSD_N

N_BRANCH = 3
IN_SPLITS = (MLA_Q_LORA, MLA_KV_LORA, MLA_ROPE, MLA_WIDTH, GQA_WIDTH, GQA_KV_WIDTH, GQA_KV_WIDTH,
             GQA_WIDTH, SSD_INNER, SSD_CONV_DIM, 2 * SSD_HEADS, N_BRANCH * D_MODEL)
IN_NAMES = ('cq', 'ckv', 'kr', 'ga', 'gq', 'gk', 'gv', 'gb', 'z', 'xbc', 'dtr', 'mg')
IN_OFFSETS = dict(zip(IN_NAMES, np.concatenate([[0], np.cumsum(IN_SPLITS)[:-1]]).tolist()))
IN_WIDTHS = dict(zip(IN_NAMES, IN_SPLITS))
P_ORDER = ('mg', 'z', 'xbc', 'ga', 'gq', 'gb', 'cq', 'ckv', 'gk', 'gv', 'kr', 'dtr')
P_OFF = {}
_o = 0
for _n in P_ORDER:
    P_OFF[_n] = _o
    _o += IN_WIDTHS[_n]
P_USED = _o
P_TN = 512
IN_TM = 4352
P_WIDTH = -(-P_USED // P_TN) * P_TN

DEEPNORM_ALPHA = (2 * DEPTH) ** 0.25

VMEM_LIMIT = 56 * 2 ** 20


def _params(*sem):
    return pltpu.CompilerParams(dimension_semantics=sem, vmem_limit_bytes=VMEM_LIMIT)


def _tile(n, target, align=8):
    t = min(n, target)
    while t > align and (n % t or t % align):
        t -= align
    assert n % t == 0, (n, target)
    return t


def _silu(v):
    return v * jax.nn.sigmoid(v)


def _softplus(v):
    return jnp.maximum(v, 0.0) + jnp.log1p(jnp.exp(-jnp.abs(v)))


def _layer_norm(v):
    mu = jnp.mean(v, axis=-1, keepdims=True)
    vc = v - mu
    var = jnp.mean(vc * vc, axis=-1, keepdims=True)
    return vc * lax.rsqrt(var + EPS)


def _rms(v):
    return v * lax.rsqrt(jnp.mean(v * v, axis=-1, keepdims=True) + EPS)


def _mod_kernel(c_ref, w_ref, b_ref, o_ref):
    a = _silu(c_ref[...]).astype(BF16)
    o_ref[...] = jnp.dot(a, w_ref[...].astype(BF16), preferred_element_type=F32) + b_ref[...]


def _mod_rows(c_rows, w_mod, b_mod):
    r, d = c_rows.shape
    n = w_mod.shape[1]
    tn = 512
    return pl.pallas_call(
        _mod_kernel,
        grid=(n // tn,),
        in_specs=[pl.BlockSpec((r, d), lambda j: (0, 0)),
                  pl.BlockSpec((d, tn), lambda j: (0, j)),
                  pl.BlockSpec((1, tn), lambda j: (0, j))],
        out_specs=pl.BlockSpec((r, tn), lambda j: (0, j)),
        out_shape=jax.ShapeDtypeStruct((r, n), F32),
        compiler_params=_params("arbitrary"),
        name="mod_rows",
    )(c_rows, w_mod, b_mod.reshape(1, n))


def _residual_rows(ctx_ref, x_ref, ctx_tiles, ro):
    return jnp.where(pl.program_id(1) + ro < ctx_tiles, ctx_ref[0], x_ref[0])


def _residual_specs(tm, d, ctx_tiles, ro):
    return [pl.BlockSpec((1, tm, d), lambda bi, i: (bi, jnp.minimum(i + ro, ctx_tiles - 1), 0)),
            pl.BlockSpec((1, tm, d), lambda bi, i: (bi, jnp.maximum(i + ro - ctx_tiles, 0), 0))]


def _ln_mod_kernel(ctx_ref, x_ref, sh_ref, sc_ref, o_ref, *, ctx_tiles):
    y = _layer_norm(_residual_rows(ctx_ref, x_ref, ctx_tiles, 0))
    o_ref[0] = (y * (1.0 + sc_ref[0]) + sh_ref[0]).astype(o_ref.dtype)


def _mod_row_index(nb, ctx_tiles):
    return lambda b, i: jnp.where(i < ctx_tiles, nb, b)


def _ln_mod(ctx, x, tab, nb):
    b, seq, d = x.shape
    ctx_len = ctx.shape[1]
    tm = _tile(ctx_len, 256)
    ctx_tiles = ctx_len // tm
    row = _mod_row_index(nb, ctx_tiles)
    return pl.pallas_call(
        functools.partial(_ln_mod_kernel, ctx_tiles=ctx_tiles),
        grid=(b, (ctx_len + seq) // tm),
        in_specs=_residual_specs(tm, d, ctx_tiles, 0)
        + [pl.BlockSpec((1, 1, d), lambda bi, i: (row(bi, i), 0, 0)),
           pl.BlockSpec((1, 1, d), lambda bi, i: (row(bi, i), 0, 1))],
        out_specs=pl.BlockSpec((1, tm, d), lambda bi, i: (bi, i, 0)),
        out_shape=jax.ShapeDtypeStruct((b, ctx_len + seq, d), BF16),
        compiler_params=_params("parallel", "parallel"),
        name="ln_mod",
    )(ctx, x, tab, tab)


def _in_proj_kernel(x_ref, w_ref, o_ref, dt_ref, *, dt_tile, dt_col):
    acc = jnp.dot(x_ref[...], w_ref[...], preferred_element_type=F32)
    o_ref[...] = acc.astype(o_ref.dtype)

    @pl.when(pl.program_id(1) == dt_tile)
    def _():
        dt_ref[...] = acc[:, dt_col:dt_col + 128]


def _in_proj(xm2, wp, out_dtype):
    m, k = xm2.shape
    n = wp.shape[1]
    tm = _tile(m, IN_TM)
    tn = P_TN
    dt_tile, dt_col = divmod(P_OFF['kr'], tn)
    assert P_OFF['dtr'] == P_OFF['kr'] + 64 and dt_col % 128 == 0
    return pl.pallas_call(
        functools.partial(_in_proj_kernel, dt_tile=dt_tile, dt_col=dt_col),
        grid=(m // tm, n // tn),
        in_specs=[pl.BlockSpec((tm, k), lambda i, j: (i, 0), pipeline_mode=pl.Buffered(1)),
                  pl.BlockSpec((k, tn), lambda i, j: (0, j))],
        out_specs=[pl.BlockSpec((tm, tn), lambda i, j: (i, j)),
                   pl.BlockSpec((tm, 128), lambda i, j: (i, 0))],
        out_shape=[jax.ShapeDtypeStruct((m, n), out_dtype), jax.ShapeDtypeStruct((m, 128), F32)],
        compiler_params=_params("parallel", "arbitrary"),
        name="in_proj",
    )(xm2, wp)


def _mla_prep_kernel(cq_ref, ckv_ref, kr_ref, qn_ref, kvn_ref, wqm_ref, wqr_ref, wkv_ref,
                     cosq_ref, sinq_ref, ck_ref, s1_ref, s2_ref, qa_ref, ka_ref, va_ref):
    cqn = (_rms(cq_ref[0].astype(F32)) * qn_ref[...]).astype(BF16)
    qm = jnp.dot(cqn, wqm_ref[...], preferred_element_type=F32)
    qr = jnp.dot(cqn, wqr_ref[...], preferred_element_type=F32)
    cosq = cosq_ref[...]
    sinq = sinq_ref[...]
    for h in range(MLA_HEADS):
        sl = slice(h * MLA_QK_PAD, (h + 1) * MLA_QK_PAD)
        qa_ref[0, h] = (qm[:, sl] * cosq + qr[:, sl] * sinq).astype(qa_ref.dtype)
    ckvn = (_rms(ckv_ref[0].astype(F32)) * kvn_ref[...]).astype(BF16)
    kv = jnp.dot(ckvn, wkv_ref[...], preferred_element_type=F32)
    kr = kr_ref[0].astype(F32)
    kpe = (kr * ck_ref[...] + pltpu.roll(kr, 128 - 16, axis=1) * s1_ref[...]
           + pltpu.roll(kr, 16, axis=1) * s2_ref[...]).astype(ka_ref.dtype)
    for h in range(MLA_HEADS):
        base = h * (MLA_NOPE + MLA_V)
        ka_ref[0, h, :, 0:MLA_NOPE] = kv[:, base:base + MLA_NOPE].astype(ka_ref.dtype)
        ka_ref[0, h, :, MLA_NOPE:MLA_QK_PAD] = kpe
        va_ref[0, h] = kv[:, base + MLA_NOPE:base + MLA_NOPE + MLA_V].T.astype(va_ref.dtype)


def _mla_prep(p3, qn, kvn, wqm, wqr, wkv, tabs, ctx_len):
    b, t, _ = p3.shape
    tm = _tile(ctx_len, 256)
    cosq, sinq, ck, s1, s2 = tabs

    def col(name, width):
        blk = P_OFF[name] // width
        return pl.BlockSpec((1, tm, width), lambda bi, i: (bi, i, blk))

    def full(a):
        return pl.BlockSpec(a.shape, lambda bi, i: (0,) * a.ndim)

    def rows(a):
        return pl.BlockSpec((tm, a.shape[1]), lambda bi, i: (i, 0))

    hm = lambda w: pl.BlockSpec((1, MLA_HEADS, tm, w), lambda bi, i: (bi, 0, i, 0))
    return pl.pallas_call(
        _mla_prep_kernel,
        grid=(b, t // tm),
        in_specs=[col('cq', MLA_Q_LORA), col('ckv', MLA_KV_LORA), col('kr', 128),
                  full(qn), full(kvn), full(wqm), full(wqr), full(wkv),
                  rows(cosq), rows(sinq), rows(ck), rows(s1), rows(s2)],
        out_specs=[hm(MLA_QK_PAD), hm(MLA_QK_PAD),
                   pl.BlockSpec((1, MLA_HEADS, MLA_V, tm), lambda bi, i: (bi, 0, 0, i))],
        out_shape=[jax.ShapeDtypeStruct((b, MLA_HEADS, t, MLA_QK_PAD), BF16),
                   jax.ShapeDtypeStruct((b, MLA_HEADS, t, MLA_QK_PAD), BF16),
                   jax.ShapeDtypeStruct((b, MLA_HEADS, MLA_V, t), BF16)],
        compiler_params=_params("parallel", "parallel"),
        name="mla_prep",
    )(p3, p3, p3, qn, kvn, wqm, wqr, wkv, cosq, sinq, ck, s1, s2)


def _rope128(y, c, s1, s2):
    return y * c + pltpu.roll(y, 128 - 32, axis=1) * s1 + pltpu.roll(y, 32, axis=1) * s2


def _gqa_prep_kernel(gq_ref, gk_ref, gv_ref, qn_ref, kn_ref, cq_ref, s1q_ref, s2q_ref,
                     ck_ref, s1k_ref, s2k_ref, qb_ref, kb_ref, vb_ref):
    gq = gq_ref[0].astype(F32)
    for h in range(GQA_HEADS):
        y = _rms(gq[:, h * GQA_DIM:(h + 1) * GQA_DIM]) * qn_ref[...]
        qb_ref[0, h] = _rope128(y, cq_ref[...], s1q_ref[...], s2q_ref[...]).astype(qb_ref.dtype)
    gk = gk_ref[0].astype(F32)
    gv = gv_ref[0].astype(F32)
    for h in range(GQA_KV_HEADS):
        y = _rms(gk[:, h * GQA_DIM:(h + 1) * GQA_DIM]) * kn_ref[...]
        kb_ref[0, h] = _rope128(y, ck_ref[...], s1k_ref[...], s2k_ref[...]).astype(kb_ref.dtype)
        vb_ref[0, h] = gv[:, h * GQA_DIM:(h + 1) * GQA_DIM].T.astype(vb_ref.dtype)


def _gqa_prep(p3, qn, kn, tabs, ctx_len):
    b, t, _ = p3.shape
    tm = _tile(ctx_len, 256)

    def col(name, width):
        blk = P_OFF[name] // width
        return pl.BlockSpec((1, tm, width), lambda bi, i: (bi, i, blk))

    def full(a):
        return pl.BlockSpec(a.shape, lambda bi, i: (0,) * a.ndim)

    def rows(a):
        return pl.BlockSpec((tm, a.shape[1]), lambda bi, i: (i, 0))

    hm = lambda nh: pl.BlockSpec((1, nh, tm, GQA_DIM), lambda bi, i: (bi, 0, i, 0))
    hmt = lambda nh: pl.BlockSpec((1, nh, GQA_DIM, tm), lambda bi, i: (bi, 0, 0, i))
    return pl.pallas_call(
        _gqa_prep_kernel,
        grid=(b, t // tm),
        in_specs=[col('gq', GQA_WIDTH), col('gk', GQA_KV_WIDTH), col('gv', GQA_KV_WIDTH),
                  full(qn), full(kn)] + [rows(a) for a in tabs],
        out_specs=[hm(GQA_HEADS), hm(GQA_KV_HEADS), hmt(GQA_KV_HEADS)],
        out_shape=[jax.ShapeDtypeStruct((b, GQA_HEADS, t, GQA_DIM), BF16),
                   jax.ShapeDtypeStruct((b, GQA_KV_HEADS, t, GQA_DIM), BF16),
                   jax.ShapeDtypeStruct((b, GQA_KV_HEADS, GQA_DIM, t), BF16)],
        compiler_params=_params("parallel", "parallel"),
        name="gqa_prep",
    )(p3, p3, p3, qn, kn, *tabs)


ATTN_KC = 256
ATTN_QB = 128
LOG2E = math.log2(math.e)


def _attn_logits(qts, k_ref, s_buf, nkeys):
    nqb = qts[0].shape[0] // ATTN_QB
    for j, q in enumerate(qts):
        st = lax.dot_general(k_ref[0, j, 0:nkeys, :], q, (((1,), (1,)), ((), ())),
                             preferred_element_type=F32)
        for b in range(nqb):
            s_buf[j * nqb + b, 0:nkeys, :] = st[:, b * ATTN_QB:(b + 1) * ATTN_QB]


def _attn_softmax_slab(s_buf, p_buf, l_buf, b, nkeys):
    parts = [jnp.max(s_buf[b, k0:k0 + ATTN_KC, :].reshape(ATTN_KC // 64, 8, 8, ATTN_QB), axis=0)
             for k0 in range(0, nkeys, ATTN_KC)]
    while len(parts) > 1:
        parts = [jnp.maximum(parts[i], parts[i + 1]) if i + 1 < len(parts) else parts[i]
                 for i in range(0, len(parts), 2)]
    m = jnp.max(parts[0], axis=(0, 1), keepdims=True)[0]
    lacc = jnp.zeros((8, 8, ATTN_QB), F32)
    for k0 in range(0, nkeys, ATTN_KC):
        p = jnp.exp2(s_buf[b, k0:k0 + ATTN_KC, :] - m)
        lacc = lacc + jnp.sum(p.reshape(ATTN_KC // 64, 8, 8, ATTN_QB), axis=0)
        p_buf[b, k0:k0 + ATTN_KC, :] = p.astype(p_buf.dtype)
    l = jnp.sum(lacc, axis=(0, 1), keepdims=True)[0]
    l_buf[b] = jnp.broadcast_to(1.0 / l, (8, ATTN_QB))


def _attn_output(vt_ref, p_buf, l_buf, o_ref, nkeys, hps, group, tq):
    dv = vt_ref.shape[2]
    nqb = group * tq // ATTN_QB
    for j in range(hps):
        slabs = range(j * nqb, (j + 1) * nqb)
        pt = jnp.concatenate([p_buf[b, 0:nkeys, :] for b in slabs], axis=1)
        ot = jnp.dot(vt_ref[0, j, :, 0:nkeys], pt, preferred_element_type=F32)
        ot = ot * jnp.concatenate([l_buf[b, 0:1, :] for b in slabs], axis=1)
        for g in range(group):
            c0 = (j * group + g) * dv
            o_ref[0, :, c0:c0 + dv] = ot[:, g * tq:(g + 1) * tq].T.astype(o_ref.dtype)


def _attn_queries(q_ref, hps, group):
    return [jnp.concatenate([q_ref[0, j * group + g] for g in range(group)], axis=0)
            for j in range(hps)]


def _attn_ctx_kernel(qt_ref, k_ref, vt_ref, o_ref, s_sc, p_sc, l_sc, *, group, tq):
    hps, nkeys = k_ref.shape[1], k_ref.shape[2]
    nslab = hps * group * tq // ATTN_QB
    _attn_logits(_attn_queries(qt_ref, hps, group), k_ref, s_sc, nkeys)

    def slab(b, carry):
        _attn_softmax_slab(s_sc, p_sc, l_sc, b, nkeys)
        return carry

    lax.fori_loop(0, nslab, slab, 0)
    _attn_output(vt_ref, p_sc, l_sc, o_ref, nkeys, hps, group, tq)


def _attn_lat_kernel(qt_ref, k_ref, vt_ref, o_ref, s0, s1, p0, p1, l0, l1, *, group, tq):
    hps, nkeys = k_ref.shape[1], k_ref.shape[2]
    nslab = hps * group * tq // ATTN_QB
    g = pl.program_id(0)

    @pl.when(g == 0)
    def _():
        for buf in (s0, s1, p0, p1, l0, l1):
            buf[...] = jnp.zeros(buf.shape, buf.dtype)

    def step(s_a, s_b, p_b, p_c, l_b, l_c):
        _attn_logits(_attn_queries(qt_ref, hps, group), k_ref, s_a, nkeys)
        for b in range(nslab):
            _attn_softmax_slab(s_b, p_b, l_b, b, nkeys)
        _attn_output(vt_ref, p_c, l_c, o_ref, nkeys, hps, group, tq)

    pl.when(g % 2 == 0)(lambda: step(s0, s1, p1, p0, l1, l0))
    pl.when(g % 2 == 1)(lambda: step(s1, s0, p0, p1, l0, l1))


def _attention(qt, k, vt, ctx_len, tq, hps, with_ctx_queries, out_dtype):
    b, hq, t, dk = qt.shape
    hkv, dv = k.shape[1], vt.shape[2]
    group = hq // hkv
    tq = _tile(ctx_len, tq, 128)
    nslab = hps * group * tq // ATTN_QB
    ctx_tiles = ctx_len // tq
    nq = t // tq - ctx_tiles
    width = hps * group * dv
    assert t % ATTN_KC == 0 and ctx_len % ATTN_KC == 0 and hkv % hps == 0

    def scratch(nkeys):
        return [pltpu.VMEM((nslab, nkeys, ATTN_QB), F32), pltpu.VMEM((nslab, nkeys, ATTN_QB), BF16),
                pltpu.VMEM((nslab, 8, ATTN_QB), F32)]

    s_lat, p_lat, l_lat = scratch(t)
    nh = hkv // hps
    items = b * nh * nq

    def item(step):
        i = jnp.clip(step, 0, items - 1)
        return i // (nh * nq), (i // nq) % nh, i % nq

    def q_map(g):
        bi, h, qi = item(g)
        return bi, h, qi + ctx_tiles, 0

    def k_map(g):
        bi, h, _ = item(g)
        return bi, h, 0, 0

    def v_map(g):
        bi, h, _ = item(g - 2)
        return bi, h, 0, 0

    def o_map(g):
        bi, h, qi = item(g - 2)
        return bi, qi, h

    y_lat = pl.pallas_call(
        functools.partial(_attn_lat_kernel, group=group, tq=tq),
        grid=(items + 2,),
        in_specs=[pl.BlockSpec((1, hps * group, tq, dk), q_map),
                  pl.BlockSpec((1, hps, t, dk), k_map),
                  pl.BlockSpec((1, hps, dv, t), v_map)],
        out_specs=pl.BlockSpec((1, tq, width), o_map),
        out_shape=jax.ShapeDtypeStruct((b, nq * tq, hq * dv), out_dtype),
        scratch_shapes=[s_lat, s_lat, p_lat, p_lat, l_lat, l_lat],
        compiler_params=_params("arbitrary"),
        name="attention_lat_dk%d" % dk,
    )(qt, k, vt)
    if not with_ctx_queries:
        return y_lat
    y_ctx = pl.pallas_call(
        functools.partial(_attn_ctx_kernel, group=group, tq=tq),
        grid=(b, hkv // hps, ctx_tiles),
        in_specs=[pl.BlockSpec((1, hps * group, tq, dk), lambda bi, h, i: (bi, h, i, 0)),
                  pl.BlockSpec((1, hps, ctx_len, dk), lambda bi, h, i: (bi, h, 0, 0)),
                  pl.BlockSpec((1, hps, dv, ctx_len), lambda bi, h, i: (bi, h, 0, 0))],
        out_specs=pl.BlockSpec((1, tq, width), lambda bi, h, i: (bi, i, h)),
        out_shape=jax.ShapeDtypeStruct((b, ctx_len, hq * dv), out_dtype),
        scratch_shapes=scratch(ctx_len),
        compiler_params=_params("parallel", "parallel", "arbitrary"),
        name="attention_ctx_dk%d" % dk,
    )(qt, k, vt)
    return jnp.concatenate([y_ctx, y_lat], axis=1)


def _conv_kernel(x_ref, w_ref, b_ref, xt_ref, bc_ref, pad_sc, *, ctx_len, rows, x_blocks):
    t = x_ref.shape[1]
    nch = x_ref.shape[2]
    halo = 8
    segs = ((0, ctx_len), (ctx_len, t))
    zeros = jnp.zeros((halo, nch), F32)
    for si, (lo, hi) in enumerate(segs):
        pad_sc[lo + si * halo:lo + (si + 1) * halo, :] = zeros
        for r0 in range(lo, hi, rows):
            pad_sc[r0 + (si + 1) * halo:r0 + (si + 1) * halo + rows, :] = x_ref[0, r0:r0 + rows, :].astype(F32)
    pad_sc[t + 2 * halo:t + 3 * halo, :] = zeros
    w = w_ref[...]
    bias = b_ref[...]

    def emit(transposed):
        for si, (lo, hi) in enumerate(segs):
            for r0 in range(lo, hi, rows):
                base = r0 + (si + 1) * halo - SSD_CONV // 2
                acc = bias + w[0:1, :] * pad_sc[base:base + rows, :]
                for kk in range(1, SSD_CONV):
                    acc = acc + w[kk:kk + 1, :] * pad_sc[base + kk:base + kk + rows, :]
                y = _silu(acc)
                if transposed:
                    xt_ref[0, :, r0:r0 + rows] = y.T.astype(xt_ref.dtype)
                else:
                    bc_ref[0, r0:r0 + rows, :] = y.astype(bc_ref.dtype)

    is_x = pl.program_id(1) < x_blocks
    pl.when(is_x)(lambda: emit(True))
    pl.when(jnp.logical_not(is_x))(lambda: emit(False))


def _conv(p3, conv_w, conv_b, ctx_len, out_dtype):
    b, t, _ = p3.shape
    nch = 256
    rows = _tile(ctx_len, 256)
    blk0 = P_OFF['xbc'] // nch
    x_blocks = SSD_INNER // nch
    kern = functools.partial(_conv_kernel, ctx_len=ctx_len, rows=rows, x_blocks=x_blocks)
    return pl.pallas_call(
        kern,
        grid=(b, SSD_CONV_DIM // nch),
        in_specs=[pl.BlockSpec((1, t, nch), lambda bi, j: (bi, 0, blk0 + j)),
                  pl.BlockSpec((SSD_CONV, nch), lambda bi, j: (0, j)),
                  pl.BlockSpec((1, nch), lambda bi, j: (0, j))],
        out_specs=[pl.BlockSpec((1, nch, t), lambda bi, j: (bi, jnp.minimum(j, x_blocks - 1), 0)),
                   pl.BlockSpec((1, t, nch), lambda bi, j: (bi, 0, jnp.maximum(j - x_blocks, 0)))],
        out_shape=[jax.ShapeDtypeStruct((b, SSD_INNER, t), out_dtype),
                   jax.ShapeDtypeStruct((b, t, SSD_CONV_DIM - SSD_INNER), out_dtype)],
        scratch_shapes=[pltpu.VMEM((t + 24, nch), F32)],
        compiler_params=_params("parallel", "arbitrary"),
        name="ssd_conv",
    )(p3, conv_w, conv_b.reshape(1, SSD_CONV_DIM))


def _split3(v):
    hi = v.astype(BF16)
    r1 = v - hi.astype(F32)
    mid = r1.astype(BF16)
    lo = (r1 - mid.astype(F32)).astype(BF16)
    return hi, mid, lo


def _expand_rows(v):
    q = v.shape[1]
    return jnp.concatenate([jnp.broadcast_to(v[e:e + 1, :], (SSD_P, q)) for e in range(SSD_E)], axis=0)


def _ssd_direction(xt, bm, cm, dt_col_raw, dt_row_raw, bias_col, bias_row, a_col, a_row, h_ref,
                   backward):
    q = xt.shape[1]
    ri = lax.broadcasted_iota(jnp.int32, (q, q), 0)
    ci = lax.broadcasted_iota(jnp.int32, (q, q), 1)
    tri_col = jnp.where((ri <= ci) if backward else (ri >= ci), 1.0, 0.0).astype(BF16)
    tri_row = jnp.where((ri >= ci) if backward else (ri <= ci), 1.0, 0.0).astype(BF16)
    keep_t = (ci <= ri) if backward else (ci >= ri)

    dt_c = _softplus(dt_col_raw + bias_col)
    dt_r = _softplus(dt_row_raw + bias_row)
    cum_c = sum(jnp.dot(tri_col, part, preferred_element_type=F32) for part in _split3(dt_c * a_col))
    cum_r = sum(jnp.dot(part, tri_row, preferred_element_type=F32) for part in _split3(dt_r * a_row))
    total = jnp.broadcast_to(cum_r[:, 0:1] if backward else cum_r[:, q - 1:q], (SSD_E, q))

    bmb = bm.astype(BF16)
    cmb = cm.astype(BF16)
    nt = (((1,), (1,)), ((), ()))
    cbt = lax.dot_general(bmb, cmb, nt, preferred_element_type=F32)
    h = h_ref[...]
    y_off = lax.dot_general(h.astype(BF16), cmb, nt, preferred_element_type=F32) * _expand_rows(jnp.exp(cum_r))
    wgt = (xt * _expand_rows(dt_r * jnp.exp(total - cum_r))).astype(BF16)
    h_ref[...] = _expand_rows(jnp.exp(total)) * h + jnp.dot(wgt, bmb, preferred_element_type=F32)

    xdt = (xt * _expand_rows(dt_r)).astype(BF16)
    parts = []
    for e in range(SSD_E):
        seg = cum_r[e:e + 1, :] - cum_c[:, e:e + 1]
        dec = jnp.exp(jnp.where(keep_t, seg, -jnp.inf))
        parts.append(jnp.dot(xdt[e * SSD_P:(e + 1) * SSD_P, :], (cbt * dec).astype(BF16),
                             preferred_element_type=F32))
    return jnp.concatenate(parts, axis=0) + y_off


def _ssd_kernel(xf_ref, bf_ref, cf_ref, xb_ref, bb_ref, cb_ref, dcf_ref, drf_ref, dcb_ref, drb_ref,
                bias_c_ref, bias_r_ref, a_c_ref, a_r_ref, d_ref, yf_ref, yb_ref, hf_sc, hb_sc):
    @pl.when(pl.program_id(2) == 0)
    def _():
        hf_sc[...] = jnp.zeros(hf_sc.shape, F32)
        hb_sc[...] = jnp.zeros(hb_sc.shape, F32)

    nsub = xf_ref.shape[2] // SSD_Q
    for i in range(nsub):
        rf = slice(i * SSD_Q, (i + 1) * SSD_Q)
        rb = slice((nsub - 1 - i) * SSD_Q, (nsub - i) * SSD_Q)
        xt = xf_ref[0, :, rf].astype(F32)
        yf = _ssd_direction(xt, bf_ref[0, rf, :].astype(F32), cf_ref[0, rf, :].astype(F32),
                            dcf_ref[0, 0, 0, rf, :], drf_ref[0, 0, 0, :, rf],
                            bias_c_ref[0, 0], bias_r_ref[0, 0], a_c_ref[0, 0], a_r_ref[0, 0], hf_sc, False)
        yf_ref[0, :, rf] = (yf + d_ref[0] * xt).astype(yf_ref.dtype)
        yb = _ssd_direction(xb_ref[0, :, rb].astype(F32), bb_ref[0, rb, :].astype(F32),
                            cb_ref[0, rb, :].astype(F32), dcb_ref[0, 0, 0, rb, :], drb_ref[0, 0, 0, :, rb],
                            bias_c_ref[1, 0], bias_r_ref[1, 0], a_c_ref[1, 0], a_r_ref[1, 0], hb_sc, True)
        yb_ref[0, :, rb] = yb.astype(yb_ref.dtype)


def _ssd(xt, bc, dt_col, dt_row, bias_c, bias_r, a_c, a_r, d_rows, ctx_len, out_dtype):
    b, _, t = xt.shape
    assert SSD_Q == SSD_N
    rows = _tile(ctx_len, SSD_ROWS, SSD_Q)
    nblk = t // rows
    nctx = ctx_len // rows

    def bidx(c):
        return jnp.where(c < nctx, nctx - 1 - c, nblk - 1 - (c - nctx))

    fx = lambda bi, g, c: (bi, g, c)
    fb = lambda bi, g, c: (bi, c, g)
    fc = lambda bi, g, c: (bi, c, SSD_G + g)
    bx = lambda bi, g, c: (bi, g, bidx(c))
    bb = lambda bi, g, c: (bi, bidx(c), g)
    bcm = lambda bi, g, c: (bi, bidx(c), SSD_G + g)
    small = lambda a: pl.BlockSpec((2, 1) + a.shape[2:], lambda bi, g, c: (0, g, 0, 0))
    return pl.pallas_call(
        _ssd_kernel,
        grid=(b, SSD_G, nblk),
        in_specs=[pl.BlockSpec((1, SSD_GW, rows), fx), pl.BlockSpec((1, rows, SSD_N), fb),
                  pl.BlockSpec((1, rows, SSD_N), fc),
                  pl.BlockSpec((1, SSD_GW, rows), bx), pl.BlockSpec((1, rows, SSD_N), bb),
                  pl.BlockSpec((1, rows, SSD_N), bcm),
                  pl.BlockSpec((1, 1, 1, rows, SSD_E), lambda bi, g, c: (bi, 0, g, c, 0)),
                  pl.BlockSpec((1, 1, 1, SSD_E, rows), lambda bi, g, c: (bi, 0, g, 0, c)),
                  pl.BlockSpec((1, 1, 1, rows, SSD_E), lambda bi, g, c: (bi, 1, g, bidx(c), 0)),
                  pl.BlockSpec((1, 1, 1, SSD_E, rows), lambda bi, g, c: (bi, 1, g, 0, bidx(c))),
                  small(bias_c), small(bias_r), small(a_c), small(a_r),
                  pl.BlockSpec((1, SSD_GW, SSD_Q), lambda bi, g, c: (g, 0, 0))],
        out_specs=[pl.BlockSpec((1, SSD_GW, rows), fx), pl.BlockSpec((1, SSD_GW, rows), bx)],
        out_shape=[jax.ShapeDtypeStruct((b, SSD_INNER, t), out_dtype)] * 2,
        scratch_shapes=[pltpu.VMEM((SSD_GW, SSD_N), F32), pltpu.VMEM((SSD_GW, SSD_N), F32)],
        compiler_params=_params("parallel", "parallel", "arbitrary"),
        name="ssd_scan",
    )(xt, bc, bc, xt, bc, bc, dt_col, dt_row, dt_col, dt_row, bias_c, bias_r, a_c, a_r, d_rows)


MERGE_TN = 512


def _merge_kernel(ya_ref, ga_ref, yb_ref, gb_ref, yf_ref, ybk_ref, z_ref, nrm_ref,
                  mg_ref, wa_ref, wb_ref, wc_ref, u_ref, c_sc):
    a_in = (ya_ref[0].astype(F32) * _silu(ga_ref[0].astype(F32))).astype(BF16)
    b_in = (yb_ref[0].astype(F32) * _silu(gb_ref[0].astype(F32))).astype(BF16)
    v = (yf_ref[0].astype(F32) + ybk_ref[0].astype(F32)).T * _silu(z_ref[0].astype(F32))
    for g in range(SSD_G):
        sl = slice(g * SSD_GW, (g + 1) * SSD_GW)
        c_sc[:, sl] = (_rms(v[:, sl]) * nrm_ref[:, sl]).astype(BF16)
    c_in = c_sc[...]
    d = u_ref.shape[-1]
    for j in range(0, d, MERGE_TN):
        sl = slice(j, j + MERGE_TN)
        br_a = jnp.dot(a_in, wa_ref[:, sl], preferred_element_type=F32)
        br_b = jnp.dot(b_in, wb_ref[:, sl], preferred_element_type=F32)
        br_c = jnp.dot(c_in, wc_ref[:, sl], preferred_element_type=F32)
        u = (jax.nn.sigmoid(mg_ref[0, :, j:j + MERGE_TN].astype(F32)) * br_a
             + jax.nn.sigmoid(mg_ref[0, :, d + j:d + j + MERGE_TN].astype(F32)) * br_b
             + jax.nn.sigmoid(mg_ref[0, :, 2 * d + j:2 * d + j + MERGE_TN].astype(F32)) * br_c)
        u_ref[0, :, sl] = u.astype(u_ref.dtype)


def _merge(ya, yb, yf, ybk, p3, ssd_norm, wa, wb, wc, row_off):
    b, rows, _ = ya.shape
    tm = _tile(rows, 256)
    d = D_MODEL
    ro = row_off // tm
    assert P_OFF['mg'] == 0

    def col(name, width):
        blk = P_OFF[name] // width
        return pl.BlockSpec((1, tm, width), lambda bi, i: (bi, i + ro, blk))

    loc = lambda w: pl.BlockSpec((1, tm, w), lambda bi, i: (bi, i, 0))
    glob = lambda w: pl.BlockSpec((1, tm, w), lambda bi, i: (bi, i + ro, 0))
    globt = pl.BlockSpec((1, SSD_INNER, tm), lambda bi, i: (bi, 0, i + ro))
    wspec = lambda k: pl.BlockSpec((k, d), lambda bi, i: (0, 0), pipeline_mode=pl.Buffered(1))
    return pl.pallas_call(
        _merge_kernel,
        grid=(b, rows // tm),
        in_specs=[loc(MLA_WIDTH), col('ga', MLA_WIDTH), loc(GQA_WIDTH), col('gb', GQA_WIDTH),
                  globt, globt, col('z', SSD_INNER),
                  pl.BlockSpec((1, SSD_INNER), lambda bi, i: (0, 0)),
                  glob(N_BRANCH * d), wspec(MLA_WIDTH), wspec(GQA_WIDTH), wspec(SSD_INNER)],
        out_specs=pl.BlockSpec((1, tm, d), lambda bi, i: (bi, i, 0)),
        out_shape=jax.ShapeDtypeStruct((b, rows, d), BF16),
        scratch_shapes=[pltpu.VMEM((tm, SSD_INNER), BF16)],
        compiler_params=_params("parallel", "parallel"),
        name="merge",
    )(ya, p3, yb, p3, yf, ybk, p3, ssd_norm.reshape(1, SSD_INNER), p3, wa, wb, wc)


def _out_ln_kernel(u_ref, w_ref, *rest, with_next, ctx_tiles, ro, split):
    if split:
        ctx_ref, x_ref, gate_ref, g_ref, b_ref, *rest = rest
        res = _residual_rows(ctx_ref, x_ref, ctx_tiles, ro)
    else:
        x_ref, gate_ref, g_ref, b_ref, *rest = rest
        res = x_ref[0]
    out = jnp.dot(u_ref[0], w_ref[...], preferred_element_type=F32)
    r = DEEPNORM_ALPHA * res + gate_ref[0] * out
    xn = _layer_norm(r) * g_ref[...] + b_ref[...]
    if with_next:
        sh_ref, sc_ref, xo_ref, xm_ref = rest
        xo_ref[0] = xn
        xm_ref[0] = (_layer_norm(xn) * (1.0 + sc_ref[0]) + sh_ref[0]).astype(xm_ref.dtype)
    else:
        (xo_ref,) = rest
        xo_ref[0] = xn


def _out_ln(u, w_out, xc, tab, ln_g, ln_b, next_tab, nb, ctx_len, row_off):
    b, rows, d = u.shape
    tm = _tile(ctx_len, 256)
    ro = row_off // tm
    ctx_tiles = ctx_len // tm
    row = _mod_row_index(nb, ctx_tiles)
    with_next = next_tab is not None
    split = isinstance(xc, tuple)
    loc = pl.BlockSpec((1, tm, d), lambda bi, i: (bi, i, 0))
    vec = pl.BlockSpec((1, d), lambda bi, i: (0, 0))
    res_specs = (_residual_specs(tm, d, ctx_tiles, ro) if split
                 else [pl.BlockSpec((1, tm, d), lambda bi, i: (bi, i + ro, 0))])
    in_specs = [loc, pl.BlockSpec((d, d), lambda bi, i: (0, 0))] + res_specs + [
        pl.BlockSpec((1, 1, d), lambda bi, i: (row(bi, i + ro), 0, 2)), vec, vec]
    args = [u, w_out] + (list(xc) if split else [xc]) + [tab, ln_g.reshape(1, d), ln_b.reshape(1, d)]
    out_specs = [loc]
    out_shape = [jax.ShapeDtypeStruct((b, rows, d), F32)]
    if with_next:
        in_specs += [pl.BlockSpec((1, 1, d), lambda bi, i: (row(bi, i + ro), 0, 0)),
                     pl.BlockSpec((1, 1, d), lambda bi, i: (row(bi, i + ro), 0, 1))]
        args += [next_tab, next_tab]
        out_specs.append(loc)
        out_shape.append(jax.ShapeDtypeStruct((b, rows, d), BF16))
    return pl.pallas_call(
        functools.partial(_out_ln_kernel, with_next=with_next, ctx_tiles=ctx_tiles, ro=ro, split=split),
        grid=(b, rows // tm),
        in_specs=in_specs,
        out_specs=out_specs,
        out_shape=out_shape,
        compiler_params=_params("parallel", "parallel"),
        name="out_ln",
    )(*args)


def _rope_angles(rows, dim):
    row, col = jnp.meshgrid(jnp.arange(rows, dtype=F32), jnp.arange(GRID_W, dtype=F32), indexing='ij')
    half = dim // 2
    inv_freq = ROPE_THETA ** (-jnp.arange(0, half, 2, dtype=F32) / half)
    ang_r = row.reshape(-1, 1) * inv_freq
    ang_c = col.reshape(-1, 1) * inv_freq
    return jnp.concatenate([ang_r, ang_r, ang_c, ang_c], axis=-1)


def _rope_tables(seq, ctx_len, dim):
    ang = _rope_angles(seq // GRID_W, dim)
    cos = jnp.concatenate([jnp.ones((ctx_len, dim), F32), jnp.cos(ang)], axis=0)
    sin = jnp.concatenate([jnp.zeros((ctx_len, dim), F32), jnp.sin(ang)], axis=0)
    return cos, sin


def _roll_tables(cos, sin, dim, scale):
    t = cos.shape[0]
    quarter = dim // 4
    first = (jnp.arange(dim) % (2 * quarter)) < quarter
    s1 = jnp.where(first, -sin, 0.0)
    s2 = jnp.where(first, 0.0, sin)
    pad = lambda a: jnp.pad(a * scale, ((0, 0), (0, 128 - dim)))
    return pad(cos), pad(s1), pad(s2)


def _rot_matrix(dim):
    quarter = dim // 4
    r = np.zeros((dim, dim), np.float32)
    for i in range(dim):
        blk = i // quarter
        if blk % 2 == 0:
            r[i + quarter, i] = -1.0
        else:
            r[i - quarter, i] = 1.0
    return jnp.asarray(r)


def _mla_q_weights(w_uq):
    w = w_uq.reshape(MLA_Q_LORA, MLA_HEADS, MLA_QK)
    nope, pe = w[..., :MLA_NOPE], w[..., MLA_NOPE:]
    pe_rot = jnp.einsum('khd,de->khe', pe, _rot_matrix(MLA_ROPE), precision=HIGHEST)
    zpad = jnp.zeros((MLA_Q_LORA, MLA_HEADS, MLA_QK_PAD - MLA_QK), F32)
    main = jnp.concatenate([nope, pe, zpad], axis=-1)
    rot = jnp.concatenate([jnp.zeros_like(nope), pe_rot, zpad], axis=-1)
    return (main.reshape(MLA_Q_LORA, -1).astype(BF16), rot.reshape(MLA_Q_LORA, -1).astype(BF16))


def _permute_w_in(w):
    parts = [w[:, IN_OFFSETS[n]:IN_OFFSETS[n] + IN_WIDTHS[n]] for n in P_ORDER]
    parts.append(jnp.zeros((w.shape[0], P_WIDTH - P_USED), w.dtype))
    return jnp.concatenate(parts, axis=1).astype(BF16)


P_DTYPE = BF16
Y_DTYPE = BF16


def kernel(x, c, ctx, c_ctx, w_mod, b_mod, w_in, mla_q_norm, mla_w_uq, mla_kv_norm, mla_w_ukv,
           gqa_q_norm, gqa_k_norm, ssd_conv_w, ssd_conv_b, ssd_a_log, ssd_dt_bias, ssd_d, ssd_norm,
           w_br_a, w_br_b, w_br_c, w_out, ln_g, ln_b):
    nb, seq, d = x.shape
    ctx_len = ctx.shape[1]
    t = ctx_len + seq
    depth = w_in.shape[0]
    assert d == D_MODEL and nb < 8 and seq % GRID_W == 0
    assert ctx_len % SSD_Q == 0 and seq % SSD_Q == 0

    cos_a, sin_a = _rope_tables(seq, ctx_len, MLA_ROPE)
    cos_b, sin_b = _rope_tables(seq, ctx_len, GQA_DIM)
    sq = MLA_QK ** -0.5 * LOG2E
    zq = jnp.zeros((t, MLA_QK_PAD - MLA_QK), F32)
    cosq = jnp.concatenate([jnp.full((t, MLA_NOPE), sq, F32), cos_a * sq, zq], axis=1)
    sinq = jnp.concatenate([jnp.zeros((t, MLA_NOPE), F32), sin_a * sq, zq], axis=1)
    mla_tabs = (cosq, sinq) + _roll_tables(cos_a, sin_a, MLA_ROPE, 1.0)
    gqa_tabs = (_roll_tables(cos_b, sin_b, GQA_DIM, GQA_DIM ** -0.5 * LOG2E)
                + _roll_tables(cos_b, sin_b, GQA_DIM, 1.0))

    c_rows = jnp.zeros((8, d), F32).at[:nb].set(c).at[nb].set(c_ctx)
    tabs = [_mod_rows(c_rows, w_mod[l], b_mod[l]).reshape(8, 1, 3 * d) for l in range(depth)]

    xc = (ctx, x)
    xm = _ln_mod(ctx, x, tabs[0], nb)

    for l in range(depth):
        last = l == depth - 1
        wp = _permute_w_in(w_in[l])
        p2, krdt = _in_proj(xm.reshape(nb * t, d), wp, P_DTYPE)
        p3 = p2.reshape(nb, t, P_WIDTH)

        wqm, wqr = _mla_q_weights(mla_w_uq[l])
        qa, ka, va = _mla_prep(p3, mla_q_norm[l].reshape(1, -1), mla_kv_norm[l].reshape(1, -1),
                               wqm, wqr, mla_w_ukv[l].astype(BF16), mla_tabs, ctx_len)
        ya = _attention(qa, ka, va, ctx_len, 256, 2, not last, Y_DTYPE)
        qb, kb, vb = _gqa_prep(p3, gqa_q_norm[l].reshape(1, -1), gqa_k_norm[l].reshape(1, -1),
                               gqa_tabs, ctx_len)
        yb = _attention(qb, kb, vb, ctx_len, 128, 1, not last, Y_DTYPE)

        xconv_t, bconv = _conv(p3, ssd_conv_w[l], ssd_conv_b[l], ctx_len, Y_DTYPE)
        dt5 = krdt[:, 64:].reshape(nb, t, 2, SSD_G, SSD_E)
        dt_col = jnp.transpose(dt5, (0, 2, 3, 1, 4))
        dt_row = jnp.transpose(dt5, (0, 2, 3, 4, 1))
        bias = ssd_dt_bias[l].astype(F32).reshape(2, SSD_G, SSD_E)
        a = -jnp.exp(ssd_a_log[l].astype(F32)).reshape(2, SSD_G, SSD_E)
        d_rows = jnp.broadcast_to(jnp.repeat(ssd_d[l].astype(F32), SSD_P).reshape(SSD_G, SSD_GW, 1),
                                  (SSD_G, SSD_GW, SSD_Q))
        yf, ybk = _ssd(xconv_t, bconv, dt_col, dt_row, bias[:, :, None, :], bias[:, :, :, None],
                       a[:, :, None, :], a[:, :, :, None], d_rows, ctx_len, Y_DTYPE)

        row_off = ctx_len if last else 0
        u = _merge(ya, yb, yf, ybk, p3, ssd_norm[l], w_br_a[l].astype(BF16), w_br_b[l].astype(BF16),
                   w_br_c[l].astype(BF16), row_off)
        if last:
            (xo,) = _out_ln(u, w_out[l].astype(BF16), xc, tabs[l], ln_g[l], ln_b[l], None, nb,
                            ctx_len, row_off)
            return xo
        xc, xm = _out_ln(u, w_out[l].astype(BF16), xc, tabs[l], ln_g[l], ln_b[l], tabs[l + 1], nb,
                         ctx_len, row_off)
```

```python
import functools
import math

import numpy as np
import jax
import jax.numpy as jnp
from jax import lax
from jax.experimental import pallas as pl
from jax.experimental.pallas import tpu as pltpu

F32 = jnp.float32
BF16 = jnp.bfloat16
HIGHEST = lax.Precision.HIGHEST

D_MODEL = 2048
DEPTH = 2
GRID_W = 64
ROPE_THETA = 10000.0
EPS = 1e-6

MLA_HEADS = 8
MLA_Q_LORA = 512
MLA_KV_LORA = 256
MLA_NOPE = 128
MLA_ROPE = 64
MLA_V = 128
MLA_QK = MLA_NOPE + MLA_ROPE
MLA_QK_PAD = 256
MLA_WIDTH = MLA_HEADS * MLA_V

GQA_HEADS = 8
GQA_KV_HEADS = 2
GQA_GROUP = GQA_HEADS // GQA_KV_HEADS
GQA_DIM = 128
GQA_WIDTH = GQA_HEADS * GQA_DIM
GQA_KV_WIDTH = GQA_KV_HEADS * GQA_DIM

SSD_INNER = D_MODEL
SSD_P = 64
SSD_HEADS = SSD_INNER // SSD_P
SSD_G = 4
SSD_E = SSD_HEADS // SSD_G
SSD_N = 128
SSD_CONV = 5
SSD_Q = 128
SSD_ROWS = 256
SSD_GW = SSD_E * SSD_P
SSD_CONV_DIM = SSD_INNER + 2 * SSD_G * SSD_N

N_BRANCH = 3
IN_SPLITS = (MLA_Q_LORA, MLA_KV_LORA, MLA_ROPE, MLA_WIDTH, GQA_WIDTH, GQA_KV_WIDTH, GQA_KV_WIDTH,
             GQA_WIDTH, SSD_INNER, SSD_CONV_DIM, 2 * SSD_HEADS, N_BRANCH * D_MODEL)
IN_NAMES = ('cq', 'ckv', 'kr', 'ga', 'gq', 'gk', 'gv', 'gb', 'z', 'xbc', 'dtr', 'mg')
IN_OFFSETS = dict(zip(IN_NAMES, np.concatenate([[0], np.cumsum(IN_SPLITS)[:-1]]).tolist()))
IN_WIDTHS = dict(zip(IN_NAMES, IN_SPLITS))
P_ORDER = ('mg', 'z', 'xbc', 'ga', 'gq', 'gb', 'cq', 'ckv', 'gk', 'gv', 'kr', 'dtr')
P_OFF = {}
_o = 0
for _n in P_ORDER:
    P_OFF[_n] = _o
    _o += IN_WIDTHS[_n]
P_USED = _o
P_TN = 512
IN_TM = 4352
P_WIDTH = -(-P_USED // P_TN) * P_TN

DEEPNORM_ALPHA = (2 * DEPTH) ** 0.25

VMEM_LIMIT = 56 * 2 ** 20


def _params(*sem):
    return pltpu.CompilerParams(dimension_semantics=sem, vmem_limit_bytes=VMEM_LIMIT)


def _tile(n, target, align=8):
    t = min(n, target)
    while t > align and (n % t or t % align):
        t -= align
    assert n % t == 0, (n, target)
    return t


def _silu(v):
    return v * jax.nn.sigmoid(v)


def _softplus(v):
    return jnp.maximum(v, 0.0) + jnp.log1p(jnp.exp(-jnp.abs(v)))


def _layer_norm(v):
    mu = jnp.mean(v, axis=-1, keepdims=True)
    vc = v - mu
    var = jnp.mean(vc * vc, axis=-1, keepdims=True)
    return vc * lax.rsqrt(var + EPS)


def _rms(v):
    return v * lax.rsqrt(jnp.mean(v * v, axis=-1, keepdims=True) + EPS)


def _mod_kernel(c_ref, w_ref, b_ref, o_ref):
    a = _silu(c_ref[...]).astype(BF16)
    o_ref[...] = jnp.dot(a, w_ref[...].astype(BF16), preferred_element_type=F32) + b_ref[...]


def _mod_rows(c_rows, w_mod, b_mod):
    r, d = c_rows.shape
    n = w_mod.shape[1]
    tn = 512
    return pl.pallas_call(
        _mod_kernel,
        grid=(n // tn,),
        in_specs=[pl.BlockSpec((r, d), lambda j: (0, 0)),
                  pl.BlockSpec((d, tn), lambda j: (0, j)),
                  pl.BlockSpec((1, tn), lambda j: (0, j))],
        out_specs=pl.BlockSpec((r, tn), lambda j: (0, j)),
        out_shape=jax.ShapeDtypeStruct((r, n), F32),
        compiler_params=_params("arbitrary"),
        name="mod_rows",
    )(c_rows, w_mod, b_mod.reshape(1, n))


def _residual_rows(ctx_ref, x_ref, ctx_tiles, ro):
    return jnp.where(pl.program_id(1) + ro < ctx_tiles, ctx_ref[0], x_ref[0])


def _residual_specs(tm, d, ctx_tiles, ro):
    return [pl.BlockSpec((1, tm, d), lambda bi, i: (bi, jnp.minimum(i + ro, ctx_tiles - 1), 0)),
            pl.BlockSpec((1, tm, d), lambda bi, i: (bi, jnp.maximum(i + ro - ctx_tiles, 0), 0))]


def _ln_mod_kernel(ctx_ref, x_ref, sh_ref, sc_ref, o_ref, *, ctx_tiles):
    y = _layer_norm(_residual_rows(ctx_ref, x_ref, ctx_tiles, 0))
    o_ref[0] = (y * (1.0 + sc_ref[0]) + sh_ref[0]).astype(o_ref.dtype)


def _mod_row_index(nb, ctx_tiles):
    return lambda b, i: jnp.where(i < ctx_tiles, nb, b)


def _ln_mod(ctx, x, tab, nb):
    b, seq, d = x.shape
    ctx_len = ctx.shape[1]
    tm = _tile(ctx_len, 256)
    ctx_tiles = ctx_len // tm
    row = _mod_row_index(nb, ctx_tiles)
    return pl.pallas_call(
        functools.partial(_ln_mod_kernel, ctx_tiles=ctx_tiles),
        grid=(b, (ctx_len + seq) // tm),
        in_specs=_residual_specs(tm, d, ctx_tiles, 0)
        + [pl.BlockSpec((1, 1, d), lambda bi, i: (row(bi, i), 0, 0)),
           pl.BlockSpec((1, 1, d), lambda bi, i: (row(bi, i), 0, 1))],
        out_specs=pl.BlockSpec((1, tm, d), lambda bi, i: (bi, i, 0)),
        out_shape=jax.ShapeDtypeStruct((b, ctx_len + seq, d), BF16),
        compiler_params=_params("parallel", "parallel"),
        name="ln_mod",
    )(ctx, x, tab, tab)


def _in_proj_kernel(x_ref, w_ref, o_ref, dt_ref, *, dt_tile, dt_col):
    acc = jnp.dot(x_ref[...], w_ref[...], preferred_element_type=F32)
    o_ref[...] = acc.astype(o_ref.dtype)

    @pl.when(pl.program_id(1) == dt_tile)
    def _():
        dt_ref[...] = acc[:, dt_col:dt_col + 128]


def _in_proj(xm2, wp, out_dtype):
    m, k = xm2.shape
    n = wp.shape[1]
    tm = _tile(m, IN_TM)
    tn = P_TN
    dt_tile, dt_col = divmod(P_OFF['kr'], tn)
    assert P_OFF['dtr'] == P_OFF['kr'] + 64 and dt_col % 128 == 0
    return pl.pallas_call(
        functools.partial(_in_proj_kernel, dt_tile=dt_tile, dt_col=dt_col),
        grid=(m // tm, n // tn),
        in_specs=[pl.BlockSpec((tm, k), lambda i, j: (i, 0), pipeline_mode=pl.Buffered(1)),
                  pl.BlockSpec((k, tn), lambda i, j: (0, j))],
        out_specs=[pl.BlockSpec((tm, tn), lambda i, j: (i, j)),
                   pl.BlockSpec((tm, 128), lambda i, j: (i, 0))],
        out_shape=[jax.ShapeDtypeStruct((m, n), out_dtype), jax.ShapeDtypeStruct((m, 128), F32)],
        compiler_params=_params("parallel", "arbitrary"),
        name="in_proj",
    )(xm2, wp)


def _mla_prep_kernel(cq_ref, ckv_ref, kr_ref, qn_ref, kvn_ref, wqm_ref, wqr_ref, wkv_ref,
                     cosq_ref, sinq_ref, ck_ref, s1_ref, s2_ref, qa_ref, ka_ref, va_ref):
    cqn = (_rms(cq_ref[0].astype(F32)) * qn_ref[...]).astype(BF16)
    qm = jnp.dot(cqn, wqm_ref[...], preferred_element_type=F32)
    qr = jnp.dot(cqn, wqr_ref[...], preferred_element_type=F32)
    cosq = cosq_ref[...]
    sinq = sinq_ref[...]
    for h in range(MLA_HEADS):
        sl = slice(h * MLA_QK_PAD, (h + 1) * MLA_QK_PAD)
        qa_ref[0, h] = (qm[:, sl] * cosq + qr[:, sl] * sinq).astype(qa_ref.dtype)
    ckvn = (_rms(ckv_ref[0].astype(F32)) * kvn_ref[...]).astype(BF16)
    kv = jnp.dot(ckvn, wkv_ref[...], preferred_element_type=F32)
    kr = kr_ref[0].astype(F32)
    kpe = (kr * ck_ref[...] + pltpu.roll(kr, 128 - 16, axis=1) * s1_ref[...]
           + pltpu.roll(kr, 16, axis=1) * s2_ref[...]).astype(ka_ref.dtype)
    for h in range(MLA_HEADS):
        base = h * (MLA_NOPE + MLA_V)
        ka_ref[0, h, :, 0:MLA_NOPE] = kv[:, base:base + MLA_NOPE].astype(ka_ref.dtype)
        ka_ref[0, h, :, MLA_NOPE:MLA_QK_PAD] = kpe
        va_ref[0, h] = kv[:, base + MLA_NOPE:base + MLA_NOPE + MLA_V].T.astype(va_ref.dtype)


def _mla_prep(p3, qn, kvn, wqm, wqr, wkv, tabs, ctx_len):
    b, t, _ = p3.shape
    tm = _tile(ctx_len, 256)
    cosq, sinq, ck, s1, s2 = tabs

    def col(name, width):
        blk = P_OFF[name] // width
        return pl.BlockSpec((1, tm, width), lambda bi, i: (bi, i, blk))

    def full(a):
        return pl.BlockSpec(a.shape, lambda bi, i: (0,) * a.ndim)

    def rows(a):
        return pl.BlockSpec((tm, a.shape[1]), lambda bi, i: (i, 0))

    hm = lambda w: pl.BlockSpec((1, MLA_HEADS, tm, w), lambda bi, i: (bi, 0, i, 0))
    return pl.pallas_call(
        _mla_prep_kernel,
        grid=(b, t // tm),
        in_specs=[col('cq', MLA_Q_LORA), col('ckv', MLA_KV_LORA), col('kr', 128),
                  full(qn), full(kvn), full(wqm), full(wqr), full(wkv),
                  rows(cosq), rows(sinq), rows(ck), rows(s1), rows(s2)],
        out_specs=[hm(MLA_QK_PAD), hm(MLA_QK_PAD),
                   pl.BlockSpec((1, MLA_HEADS, MLA_V, tm), lambda bi, i: (bi, 0, 0, i))],
        out_shape=[jax.ShapeDtypeStruct((b, MLA_HEADS, t, MLA_QK_PAD), BF16),
                   jax.ShapeDtypeStruct((b, MLA_HEADS, t, MLA_QK_PAD), BF16),
                   jax.ShapeDtypeStruct((b, MLA_HEADS, MLA_V, t), BF16)],
        compiler_params=_params("parallel", "parallel"),
        name="mla_prep",
    )(p3, p3, p3, qn, kvn, wqm, wqr, wkv, cosq, sinq, ck, s1, s2)


def _rope128(y, c, s1, s2):
    return y * c + pltpu.roll(y, 128 - 32, axis=1) * s1 + pltpu.roll(y, 32, axis=1) * s2


def _gqa_prep_kernel(gq_ref, gk_ref, gv_ref, qn_ref, kn_ref, cq_ref, s1q_ref, s2q_ref,
                     ck_ref, s1k_ref, s2k_ref, qb_ref, kb_ref, vb_ref):
    gq = gq_ref[0].astype(F32)
    for h in range(GQA_HEADS):
        y = _rms(gq[:, h * GQA_DIM:(h + 1) * GQA_DIM]) * qn_ref[...]
        qb_ref[0, h] = _rope128(y, cq_ref[...], s1q_ref[...], s2q_ref[...]).astype(qb_ref.dtype)
    gk = gk_ref[0].astype(F32)
    gv = gv_ref[0].astype(F32)
    for h in range(GQA_KV_HEADS):
        y = _rms(gk[:, h * GQA_DIM:(h + 1) * GQA_DIM]) * kn_ref[...]
        kb_ref[0, h] = _rope128(y, ck_ref[...], s1k_ref[...], s2k_ref[...]).astype(kb_ref.dtype)
        vb_ref[0, h] = gv[:, h * GQA_DIM:(h + 1) * GQA_DIM].T.astype(vb_ref.dtype)


def _gqa_prep(p3, qn, kn, tabs, ctx_len):
    b, t, _ = p3.shape
    tm = _tile(ctx_len, 256)

    def col(name, width):
        blk = P_OFF[name] // width
        return pl.BlockSpec((1, tm, width), lambda bi, i: (bi, i, blk))

    def full(a):
        return pl.BlockSpec(a.shape, lambda bi, i: (0,) * a.ndim)

    def rows(a):
        return pl.BlockSpec((tm, a.shape[1]), lambda bi, i: (i, 0))

    hm = lambda nh: pl.BlockSpec((1, nh, tm, GQA_DIM), lambda bi, i: (bi, 0, i, 0))
    hmt = lambda nh: pl.BlockSpec((1, nh, GQA_DIM, tm), lambda bi, i: (bi, 0, 0, i))
    return pl.pallas_call(
        _gqa_prep_kernel,
        grid=(b, t // tm),
        in_specs=[col('gq', GQA_WIDTH), col('gk', GQA_KV_WIDTH), col('gv', GQA_KV_WIDTH),
                  full(qn), full(kn)] + [rows(a) for a in tabs],
        out_specs=[hm(GQA_HEADS), hm(GQA_KV_HEADS), hmt(GQA_KV_HEADS)],
        out_shape=[jax.ShapeDtypeStruct((b, GQA_HEADS, t, GQA_DIM), BF16),
                   jax.ShapeDtypeStruct((b, GQA_KV_HEADS, t, GQA_DIM), BF16),
                   jax.ShapeDtypeStruct((b, GQA_KV_HEADS, GQA_DIM, t), BF16)],
        compiler_params=_params("parallel", "parallel"),
        name="gqa_prep",
    )(p3, p3, p3, qn, kn, *tabs)


ATTN_KC = 256
ATTN_QB = 128
LOG2E = math.log2(math.e)


def _attn_logits(qts, k_ref, s_buf, nkeys):
    nqb = qts[0].shape[0] // ATTN_QB
    for j, q in enumerate(qts):
        st = lax.dot_general(k_ref[0, j, 0:nkeys, :], q, (((1,), (1,)), ((), ())),
                             preferred_element_type=F32)
        for b in range(nqb):
            s_buf[j * nqb + b, 0:nkeys, :] = st[:, b * ATTN_QB:(b + 1) * ATTN_QB]


def _attn_softmax_slab(s_buf, p_buf, l_buf, b, nkeys):
    parts = [jnp.max(s_buf[b, k0:k0 + ATTN_KC, :].reshape(ATTN_KC // 64, 8, 8, ATTN_QB), axis=0)
             for k0 in range(0, nkeys, ATTN_KC)]
    while len(parts) > 1:
        parts = [jnp.maximum(parts[i], parts[i + 1]) if i + 1 < len(parts) else parts[i]
                 for i in range(0, len(parts), 2)]
    m = jnp.max(parts[0], axis=(0, 1), keepdims=True)[0]
    lacc = jnp.zeros((8, 8, ATTN_QB), F32)
    for k0 in range(0, nkeys, ATTN_KC):
        p = jnp.exp2(s_buf[b, k0:k0 + ATTN_KC, :] - m)
        lacc = lacc + jnp.sum(p.reshape(ATTN_KC // 64, 8, 8, ATTN_QB), axis=0)
        p_buf[b, k0:k0 + ATTN_KC, :] = p.astype(p_buf.dtype)
    l = jnp.sum(lacc, axis=(0, 1), keepdims=True)[0]
    l_buf[b] = jnp.broadcast_to(1.0 / l, (8, ATTN_QB))


def _attn_output(vt_ref, p_buf, l_buf, o_ref, nkeys, hps, group, tq):
    dv = vt_ref.shape[2]
    nqb = group * tq // ATTN_QB
    for j in range(hps):
        slabs = range(j * nqb, (j + 1) * nqb)
        pt = jnp.concatenate([p_buf[b, 0:nkeys, :] for b in slabs], axis=1)
        ot = jnp.dot(vt_ref[0, j, :, 0:nkeys], pt, preferred_element_type=F32)
        ot = ot * jnp.concatenate([l_buf[b, 0:1, :] for b in slabs], axis=1)
        for g in range(group):
            c0 = (j * group + g) * dv
            o_ref[0, :, c0:c0 + dv] = ot[:, g * tq:(g + 1) * tq].T.astype(o_ref.dtype)


def _attn_queries(q_ref, hps, group):
    return [jnp.concatenate([q_ref[0, j * group + g] for g in range(group)], axis=0)
            for j in range(hps)]


def _attn_ctx_kernel(q_ref, k_ref, vt_ref, o_ref, s_sc, p_sc, l_sc, *, group, tq):
    hps, nkeys = k_ref.shape[1], k_ref.shape[2]
    nslab = hps * group * tq // ATTN_QB
    _attn_logits(_attn_queries(q_ref, hps, group), k_ref, s_sc, nkeys)

    def slab(b, carry):
        _attn_softmax_slab(s_sc, p_sc, l_sc, b, nkeys)
        return carry

    lax.fori_loop(0, nslab, slab, 0)
    _attn_output(vt_ref, p_sc, l_sc, o_ref, nkeys, hps, group, tq)


def _attn_lat_kernel(q_ref, k_ref, vt_ref, o_ref, s0, s1, p0, p1, l0, l1, *, group, tq):
    hps, nkeys = k_ref.shape[1], k_ref.shape[2]
    nslab = hps * group * tq // ATTN_QB
    g = pl.program_id(0)

    @pl.when(g == 0)
    def _():
        for buf in (s0, s1, p0, p1, l0, l1):
            buf[...] = jnp.zeros(buf.shape, buf.dtype)

    def step(s_a, s_b, p_b, p_c, l_b, l_c):
        _attn_logits(_attn_queries(q_ref, hps, group), k_ref, s_a, nkeys)
        for b in range(nslab):
            _attn_softmax_slab(s_b, p_b, l_b, b, nkeys)
        _attn_output(vt_ref, p_c, l_c, o_ref, nkeys, hps, group, tq)

    pl.when(g % 2 == 0)(lambda: step(s0, s1, p1, p0, l1, l0))
    pl.when(g % 2 == 1)(lambda: step(s1, s0, p0, p1, l0, l1))


def _attention(q, k, vt, ctx_len, tq, hps, with_ctx_queries, out_dtype):
    b, hq, t, dk = q.shape
    hkv, dv = k.shape[1], vt.shape[2]
    group = hq // hkv
    tq = _tile(ctx_len, tq, 128)
    nslab = hps * group * tq // ATTN_QB
    ctx_tiles = ctx_len // tq
    nq = t // tq - ctx_tiles
    width = hps * group * dv
    assert t % ATTN_KC == 0 and ctx_len % ATTN_KC == 0 and hkv % hps == 0

    def scratch(nkeys):
        return [pltpu.VMEM((nslab, nkeys, ATTN_QB), F32), pltpu.VMEM((nslab, nkeys, ATTN_QB), BF16),
                pltpu.VMEM((nslab, 8, ATTN_QB), F32)]

    s_lat, p_lat, l_lat = scratch(t)
    nh = hkv // hps
    items = b * nh * nq

    def item(step):
        i = jnp.clip(step, 0, items - 1)
        return i // (nh * nq), (i // nq) % nh, i % nq

    def q_map(g):
        bi, h, qi = item(g)
        return bi, h, qi + ctx_tiles, 0

    def k_map(g):
        bi, h, _ = item(g)
        return bi, h, 0, 0

    def v_map(g):
        bi, h, _ = item(g - 2)
        return bi, h, 0, 0

    def o_map(g):
        bi, h, qi = item(g - 2)
        return bi, qi, h

    y_lat = pl.pallas_call(
        functools.partial(_attn_lat_kernel, group=group, tq=tq),
        grid=(items + 2,),
        in_specs=[pl.BlockSpec((1, hps * group, tq, dk), q_map),
                  pl.BlockSpec((1, hps, t, dk), k_map),
                  pl.BlockSpec((1, hps, dv, t), v_map)],
        out_specs=pl.BlockSpec((1, tq, width), o_map),
        out_shape=jax.ShapeDtypeStruct((b, nq * tq, hq * dv), out_dtype),
        scratch_shapes=[s_lat, s_lat, p_lat, p_lat, l_lat, l_lat],
        compiler_params=_params("arbitrary"),
        name="attention_lat_dk%d" % dk,
    )(q, k, vt)
    if not with_ctx_queries:
        return y_lat
    y_ctx = pl.pallas_call(
        functools.partial(_attn_ctx_kernel, group=group, tq=tq),
        grid=(b, hkv // hps, ctx_tiles),
        in_specs=[pl.BlockSpec((1, hps * group, tq, dk), lambda bi, h, i: (bi, h, i, 0)),
                  pl.BlockSpec((1, hps, ctx_len, dk), lambda bi, h, i: (bi, h, 0, 0)),
                  pl.BlockSpec((1, hps, dv, ctx_len), lambda bi, h, i: (bi, h, 0, 0))],
        out_specs=pl.BlockSpec((1, tq, width), lambda bi, h, i: (bi, i, h)),
        out_shape=jax.ShapeDtypeStruct((b, ctx_len, hq * dv), out_dtype),
        scratch_shapes=scratch(ctx_len),
        compiler_params=_params("parallel", "parallel", "arbitrary"),
        name="attention_ctx_dk%d" % dk,
    )(q, k, vt)
    return jnp.concatenate([y_ctx, y_lat], axis=1)


def _conv_kernel(x_ref, w_ref, b_ref, xt_ref, bc_ref, pad_sc, *, ctx_len, rows, x_blocks):
    t = x_ref.shape[1]
    nch = x_ref.shape[2]
    halo = 8
    segs = ((0, ctx_len), (ctx_len, t))
    zeros = jnp.zeros((halo, nch), F32)
    for si, (lo, hi) in enumerate(segs):
        pad_sc[lo + si * halo:lo + (si + 1) * halo, :] = zeros
        for r0 in range(lo, hi, rows):
            pad_sc[r0 + (si + 1) * halo:r0 + (si + 1) * halo + rows, :] = x_ref[0, r0:r0 + rows, :].astype(F32)
    pad_sc[t + 2 * halo:t + 3 * halo, :] = zeros
    w = w_ref[...]
    bias = b_ref[...]

    def emit(transposed):
        for si, (lo, hi) in enumerate(segs):
            for r0 in range(lo, hi, rows):
                base = r0 + (si + 1) * halo - SSD_CONV // 2
                acc = bias + w[0:1, :] * pad_sc[base:base + rows, :]
                for kk in range(1, SSD_CONV):
                    acc = acc + w[kk:kk + 1, :] * pad_sc[base + kk:base + kk + rows, :]
                y = _silu(acc)
                if transposed:
                    xt_ref[0, :, r0:r0 + rows] = y.T.astype(xt_ref.dtype)
                else:
                    bc_ref[0, r0:r0 + rows, :] = y.astype(bc_ref.dtype)

    is_x = pl.program_id(1) < x_blocks
    pl.when(is_x)(lambda: emit(True))
    pl.when(jnp.logical_not(is_x))(lambda: emit(False))


def _conv(p3, conv_w, conv_b, ctx_len, out_dtype):
    b, t, _ = p3.shape
    nch = 256
    rows = _tile(ctx_len, 256)
    blk0 = P_OFF['xbc'] // nch
    x_blocks = SSD_INNER // nch
    kern = functools.partial(_conv_kernel, ctx_len=ctx_len, rows=rows, x_blocks=x_blocks)
    return pl.pallas_call(
        kern,
        grid=(b, SSD_CONV_DIM // nch),
        in_specs=[pl.BlockSpec((1, t, nch), lambda bi, j: (bi, 0, blk0 + j)),
                  pl.BlockSpec((SSD_CONV, nch), lambda bi, j: (0, j)),
                  pl.BlockSpec((1, nch), lambda bi, j: (0, j))],
        out_specs=[pl.BlockSpec((1, nch, t), lambda bi, j: (bi, jnp.minimum(j, x_blocks - 1), 0)),
                   pl.BlockSpec((1, t, nch), lambda bi, j: (bi, 0, jnp.maximum(j - x_blocks, 0)))],
        out_shape=[jax.ShapeDtypeStruct((b, SSD_INNER, t), out_dtype),
                   jax.ShapeDtypeStruct((b, t, SSD_CONV_DIM - SSD_INNER), out_dtype)],
        scratch_shapes=[pltpu.VMEM((t + 24, nch), F32)],
        compiler_params=_params("parallel", "arbitrary"),
        name="ssd_conv",
    )(p3, conv_w, conv_b.reshape(1, SSD_CONV_DIM))


def _split3(v):
    hi = v.astype(BF16)
    r1 = v - hi.astype(F32)
    mid = r1.astype(BF16)
    lo = (r1 - mid.astype(F32)).astype(BF16)
    return hi, mid, lo


def _expand_rows(v):
    q = v.shape[1]
    return jnp.concatenate([jnp.broadcast_to(v[e:e + 1, :], (SSD_P, q)) for e in range(SSD_E)], axis=0)


def _ssd_decays(dt_col_raw, dt_row_raw, bias_col, bias_row, a_col, a_row, backward):
    q = dt_col_raw.shape[0]
    ri = lax.broadcasted_iota(jnp.int32, (q, q), 0)
    ci = lax.broadcasted_iota(jnp.int32, (q, q), 1)
    tri_col = jnp.where((ri <= ci) if backward else (ri >= ci), 1.0, 0.0).astype(BF16)
    tri_row = jnp.where((ri >= ci) if backward else (ri <= ci), 1.0, 0.0).astype(BF16)
    dt_c = _softplus(dt_col_raw + bias_col)
    dt_r = _softplus(dt_row_raw + bias_row)
    cum_c = sum(jnp.dot(tri_col, part, preferred_element_type=F32) for part in _split3(dt_c * a_col))
    cum_r = sum(jnp.dot(part, tri_row, preferred_element_type=F32) for part in _split3(dt_r * a_row))
    total = jnp.broadcast_to(cum_r[:, 0:1] if backward else cum_r[:, q - 1:q], (SSD_E, q))
    return dt_r, cum_c, cum_r, total


def _ssd_direction(xt, bm, cm, decays, h_ref, backward):
    q = xt.shape[1]
    dt_r, cum_c, cum_r, total = decays
    ri = lax.broadcasted_iota(jnp.int32, (q, q), 0)
    ci = lax.broadcasted_iota(jnp.int32, (q, q), 1)
    keep_t = (ci <= ri) if backward else (ci >= ri)

    bmb = bm.astype(BF16)
    cmb = cm.astype(BF16)
    nt = (((1,), (1,)), ((), ()))
    cbt = lax.dot_general(bmb, cmb, nt, preferred_element_type=F32)
    h = h_ref[...]
    y_off = lax.dot_general(h.astype(BF16), cmb, nt, preferred_element_type=F32) * _expand_rows(jnp.exp(cum_r))
    wgt = (xt * _expand_rows(dt_r * jnp.exp(total - cum_r))).astype(BF16)
    h_ref[...] = _expand_rows(jnp.exp(total)) * h + jnp.dot(wgt, bmb, preferred_element_type=F32)

    xdt = (xt * _expand_rows(dt_r)).astype(BF16)
    parts = []
    for e in range(SSD_E):
        seg = cum_r[e:e + 1, :] - cum_c[:, e:e + 1]
        dec = jnp.exp(jnp.where(keep_t, seg, -jnp.inf))
        parts.append(jnp.dot(xdt[e * SSD_P:(e + 1) * SSD_P, :], (cbt * dec).astype(BF16),
                             preferred_element_type=F32))
    return jnp.concatenate(parts, axis=0) + y_off


def _ssd_kernel(xf_ref, bf_ref, cf_ref, xb_ref, bb_ref, cb_ref, dcf_ref, drf_ref, dcb_ref, drb_ref,
                bias_c_ref, bias_r_ref, a_c_ref, a_r_ref, d_ref, yf_ref, yb_ref, hf_sc, hb_sc):
    @pl.when(pl.program_id(2) == 0)
    def _():
        hf_sc[...] = jnp.zeros(hf_sc.shape, F32)
        hb_sc[...] = jnp.zeros(hb_sc.shape, F32)

    nsub = xf_ref.shape[2] // SSD_Q
    chunks = [slice(i * SSD_Q, (i + 1) * SSD_Q) for i in range(nsub)]
    dec_f = [_ssd_decays(dcf_ref[0, 0, 0, rs, :], drf_ref[0, 0, 0, :, rs], bias_c_ref[0, 0],
                         bias_r_ref[0, 0], a_c_ref[0, 0], a_r_ref[0, 0], False) for rs in chunks]
    dec_b = [_ssd_decays(dcb_ref[0, 0, 0, rs, :], drb_ref[0, 0, 0, :, rs], bias_c_ref[1, 0],
                         bias_r_ref[1, 0], a_c_ref[1, 0], a_r_ref[1, 0], True) for rs in chunks]
    for i in range(nsub):
        rf, rb = chunks[i], chunks[nsub - 1 - i]
        xt = xf_ref[0, :, rf].astype(F32)
        yf = _ssd_direction(xt, bf_ref[0, rf, :].astype(F32), cf_ref[0, rf, :].astype(F32),
                            dec_f[i], hf_sc, False)
        yf_ref[0, :, rf] = (yf + d_ref[0] * xt).astype(yf_ref.dtype)
        yb = _ssd_direction(xb_ref[0, :, rb].astype(F32), bb_ref[0, rb, :].astype(F32),
                            cb_ref[0, rb, :].astype(F32), dec_b[nsub - 1 - i], hb_sc, True)
        yb_ref[0, :, rb] = yb.astype(yb_ref.dtype)


def _ssd(xt, bc, dt_col, dt_row, bias_c, bias_r, a_c, a_r, d_rows, ctx_len, out_dtype):
    b, _, t = xt.shape
    assert SSD_Q == SSD_N
    rows = _tile(ctx_len, SSD_ROWS, SSD_Q)
    nblk = t // rows
    nctx = ctx_len // rows

    def bidx(c):
        return jnp.where(c < nctx, nctx - 1 - c, nblk - 1 - (c - nctx))

    fx = lambda bi, g, c: (bi, g, c)
    fb = lambda bi, g, c: (bi, c, g)
    fc = lambda bi, g, c: (bi, c, SSD_G + g)
    bx = lambda bi, g, c: (bi, g, bidx(c))
    bb = lambda bi, g, c: (bi, bidx(c), g)
    bcm = lambda bi, g, c: (bi, bidx(c), SSD_G + g)
    small = lambda a: pl.BlockSpec((2, 1) + a.shape[2:], lambda bi, g, c: (0, g, 0, 0))
    return pl.pallas_call(
        _ssd_kernel,
        grid=(b, SSD_G, nblk),
        in_specs=[pl.BlockSpec((1, SSD_GW, rows), fx), pl.BlockSpec((1, rows, SSD_N), fb),
                  pl.BlockSpec((1, rows, SSD_N), fc),
                  pl.BlockSpec((1, SSD_GW, rows), bx), pl.BlockSpec((1, rows, SSD_N), bb),
                  pl.BlockSpec((1, rows, SSD_N), bcm),
                  pl.BlockSpec((1, 1, 1, rows, SSD_E), lambda bi, g, c: (bi, 0, g, c, 0)),
                  pl.BlockSpec((1, 1, 1, SSD_E, rows), lambda bi, g, c: (bi, 0, g, 0, c)),
                  pl.BlockSpec((1, 1, 1, rows, SSD_E), lambda bi, g, c: (bi, 1, g, bidx(c), 0)),
                  pl.BlockSpec((1, 1, 1, SSD_E, rows), lambda bi, g, c: (bi, 1, g, 0, bidx(c))),
                  small(bias_c), small(bias_r), small(a_c), small(a_r),
                  pl.BlockSpec((1, SSD_GW, SSD_Q), lambda bi, g, c: (g, 0, 0))],
        out_specs=[pl.BlockSpec((1, SSD_GW, rows), fx), pl.BlockSpec((1, SSD_GW, rows), bx)],
        out_shape=[jax.ShapeDtypeStruct((b, SSD_INNER, t), out_dtype)] * 2,
        scratch_shapes=[pltpu.VMEM((SSD_GW, SSD_N), F32), pltpu.VMEM((SSD_GW, SSD_N), F32)],
        compiler_params=_params("parallel", "parallel", "arbitrary"),
        name="ssd_scan",
    )(xt, bc, bc, xt, bc, bc, dt_col, dt_row, dt_col, dt_row, bias_c, bias_r, a_c, a_r, d_rows)


MERGE_TN = 512


def _merge_kernel(ya_ref, ga_ref, yb_ref, gb_ref, yf_ref, ybk_ref, z_ref, nrm_ref,
                  mg_ref, wa_ref, wb_ref, wc_ref, u_ref, c_sc):
    a_in = (ya_ref[0].astype(F32) * _silu(ga_ref[0].astype(F32))).astype(BF16)
    b_in = (yb_ref[0].astype(F32) * _silu(gb_ref[0].astype(F32))).astype(BF16)
    v = (yf_ref[0].astype(F32) + ybk_ref[0].astype(F32)).T * _silu(z_ref[0].astype(F32))
    for g in range(SSD_G):
        sl = slice(g * SSD_GW, (g + 1) * SSD_GW)
        c_sc[:, sl] = (_rms(v[:, sl]) * nrm_ref[:, sl]).astype(BF16)
    c_in = c_sc[...]
    d = u_ref.shape[-1]
    for j in range(0, d, MERGE_TN):
        sl = slice(j, j + MERGE_TN)
        br_a = jnp.dot(a_in, wa_ref[:, sl], preferred_element_type=F32)
        br_b = jnp.dot(b_in, wb_ref[:, sl], preferred_element_type=F32)
        br_c = jnp.dot(c_in, wc_ref[:, sl], preferred_element_type=F32)
        u = (jax.nn.sigmoid(mg_ref[0, :, j:j + MERGE_TN].astype(F32)) * br_a
             + jax.nn.sigmoid(mg_ref[0, :, d + j:d + j + MERGE_TN].astype(F32)) * br_b
             + jax.nn.sigmoid(mg_ref[0, :, 2 * d + j:2 * d + j + MERGE_TN].astype(F32)) * br_c)
        u_ref[0, :, sl] = u.astype(u_ref.dtype)


def _merge(ya, yb, yf, ybk, p3, ssd_norm, wa, wb, wc, row_off):
    b, rows, _ = ya.shape
    tm = _tile(rows, 256)
    d = D_MODEL
    ro = row_off // tm
    assert P_OFF['mg'] == 0

    def col(name, width):
        blk = P_OFF[name] // width
        return pl.BlockSpec((1, tm, width), lambda bi, i: (bi, i + ro, blk))

    loc = lambda w: pl.BlockSpec((1, tm, w), lambda bi, i: (bi, i, 0))
    glob = lambda w: pl.BlockSpec((1, tm, w), lambda bi, i: (bi, i + ro, 0))
    globt = pl.BlockSpec((1, SSD_INNER, tm), lambda bi, i: (bi, 0, i + ro))
    wspec = lambda k: pl.BlockSpec((k, d), lambda bi, i: (0, 0), pipeline_mode=pl.Buffered(1))
    return pl.pallas_call(
        _merge_kernel,
        grid=(b, rows // tm),
        in_specs=[loc(MLA_WIDTH), col('ga', MLA_WIDTH), loc(GQA_WIDTH), col('gb', GQA_WIDTH),
                  globt, globt, col('z', SSD_INNER),
                  pl.BlockSpec((1, SSD_INNER), lambda bi, i: (0, 0)),
                  glob(N_BRANCH * d), wspec(MLA_WIDTH), wspec(GQA_WIDTH), wspec(SSD_INNER)],
        out_specs=pl.BlockSpec((1, tm, d), lambda bi, i: (bi, i, 0)),
        out_shape=jax.ShapeDtypeStruct((b, rows, d), BF16),
        scratch_shapes=[pltpu.VMEM((tm, SSD_INNER), BF16)],
        compiler_params=_params("parallel", "parallel"),
        name="merge",
    )(ya, p3, yb, p3, yf, ybk, p3, ssd_norm.reshape(1, SSD_INNER), p3, wa, wb, wc)


def _out_ln_kernel(u_ref, w_ref, *rest, with_next, ctx_tiles, ro, split):
    if split:
        ctx_ref, x_ref, gate_ref, g_ref, b_ref, *rest = rest
        res = _residual_rows(ctx_ref, x_ref, ctx_tiles, ro)
    else:
        x_ref, gate_ref, g_ref, b_ref, *rest = rest
        res = x_ref[0]
    out = jnp.dot(u_ref[0], w_ref[...], preferred_element_type=F32)
    r = DEEPNORM_ALPHA * res + gate_ref[0] * out
    xn = _layer_norm(r) * g_ref[...] + b_ref[...]
    if with_next:
        sh_ref, sc_ref, xo_ref, xm_ref = rest
        xo_ref[0] = xn
        xm_ref[0] = (_layer_norm(xn) * (1.0 + sc_ref[0]) + sh_ref[0]).astype(xm_ref.dtype)
    else:
        (xo_ref,) = rest
        xo_ref[0] = xn


def _out_ln(u, w_out, xc, tab, ln_g, ln_b, next_tab, nb, ctx_len, row_off):
    b, rows, d = u.shape
    tm = _tile(ctx_len, 256)
    ro = row_off // tm
    ctx_tiles = ctx_len // tm
    row = _mod_row_index(nb, ctx_tiles)
    with_next = next_tab is not None
    split = isinstance(xc, tuple)
    loc = pl.BlockSpec((1, tm, d), lambda bi, i: (bi, i, 0))
    vec = pl.BlockSpec((1, d), lambda bi, i: (0, 0))
    res_specs = (_residual_specs(tm, d, ctx_tiles, ro) if split
                 else [pl.BlockSpec((1, tm, d), lambda bi, i: (bi, i + ro, 0))])
    in_specs = [loc, pl.BlockSpec((d, d), lambda bi, i: (0, 0))] + res_specs + [
        pl.BlockSpec((1, 1, d), lambda bi, i: (row(bi, i + ro), 0, 2)), vec, vec]
    args = [u, w_out] + (list(xc) if split else [xc]) + [tab, ln_g.reshape(1, d), ln_b.reshape(1, d)]
    out_specs = [loc]
    out_shape = [jax.ShapeDtypeStruct((b, rows, d), F32)]
    if with_next:
        in_specs += [pl.BlockSpec((1, 1, d), lambda bi, i: (row(bi, i + ro), 0, 0)),
                     pl.BlockSpec((1, 1, d), lambda bi, i: (row(bi, i + ro), 0, 1))]
        args += [next_tab, next_tab]
        out_specs.append(loc)
        out_shape.append(jax.ShapeDtypeStruct((b, rows, d), BF16))
    return pl.pallas_call(
        functools.partial(_out_ln_kernel, with_next=with_next, ctx_tiles=ctx_tiles, ro=ro, split=split),
        grid=(b, rows // tm),
        in_specs=in_specs,
        out_specs=out_specs,
        out_shape=out_shape,
        compiler_params=_params("parallel", "parallel"),
        name="out_ln",
    )(*args)


def _rope_angles(rows, dim):
    row, col = jnp.meshgrid(jnp.arange(rows, dtype=F32), jnp.arange(GRID_W, dtype=F32), indexing='ij')
    half = dim // 2
    inv_freq = ROPE_THETA ** (-jnp.arange(0, half, 2, dtype=F32) / half)
    ang_r = row.reshape(-1, 1) * inv_freq
    ang_c = col.reshape(-1, 1) * inv_freq
    return jnp.concatenate([ang_r, ang_r, ang_c, ang_c], axis=-1)


def _rope_tables(seq, ctx_len, dim):
    ang = _rope_angles(seq // GRID_W, dim)
    cos = jnp.concatenate([jnp.ones((ctx_len, dim), F32), jnp.cos(ang)], axis=0)
    sin = jnp.concatenate([jnp.zeros((ctx_len, dim), F32), jnp.sin(ang)], axis=0)
    return cos, sin


def _roll_tables(cos, sin, dim, scale):
    t = cos.shape[0]
    quarter = dim // 4
    first = (jnp.arange(dim) % (2 * quarter)) < quarter
    s1 = jnp.where(first, -sin, 0.0)
    s2 = jnp.where(first, 0.0, sin)
    pad = lambda a: jnp.pad(a * scale, ((0, 0), (0, 128 - dim)))
    return pad(cos), pad(s1), pad(s2)


def _rot_matrix(dim):
    quarter = dim // 4
    r = np.zeros((dim, dim), np.float32)
    for i in range(dim):
        blk = i // quarter
        if blk % 2 == 0:
            r[i + quarter, i] = -1.0
        else:
            r[i - quarter, i] = 1.0
    return jnp.asarray(r)


def _mla_q_weights(w_uq):
    w = w_uq.reshape(MLA_Q_LORA, MLA_HEADS, MLA_QK)
    nope, pe = w[..., :MLA_NOPE], w[..., MLA_NOPE:]
    pe_rot = jnp.einsum('khd,de->khe', pe, _rot_matrix(MLA_ROPE), precision=HIGHEST)
    zpad = jnp.zeros((MLA_Q_LORA, MLA_HEADS, MLA_QK_PAD - MLA_QK), F32)
    main = jnp.concatenate([nope, pe, zpad], axis=-1)
    rot = jnp.concatenate([jnp.zeros_like(nope), pe_rot, zpad], axis=-1)
    return (main.reshape(MLA_Q_LORA, -1).astype(BF16), rot.reshape(MLA_Q_LORA, -1).astype(BF16))


def _permute_w_in(w):
    parts = [w[:, IN_OFFSETS[n]:IN_OFFSETS[n] + IN_WIDTHS[n]] for n in P_ORDER]
    parts.append(jnp.zeros((w.shape[0], P_WIDTH - P_USED), w.dtype))
    return jnp.concatenate(parts, axis=1).astype(BF16)


P_DTYPE = BF16
Y_DTYPE = BF16


def kernel(x, c, ctx, c_ctx, w_mod, b_mod, w_in, mla_q_norm, mla_w_uq, mla_kv_norm, mla_w_ukv,
           gqa_q_norm, gqa_k_norm, ssd_conv_w, ssd_conv_b, ssd_a_log, ssd_dt_bias, ssd_d, ssd_norm,
           w_br_a, w_br_b, w_br_c, w_out, ln_g, ln_b):
    nb, seq, d = x.shape
    ctx_len = ctx.shape[1]
    t = ctx_len + seq
    depth = w_in.shape[0]
    assert d == D_MODEL and nb < 8 and seq % GRID_W == 0
    assert ctx_len % SSD_Q == 0 and seq % SSD_Q == 0

    cos_a, sin_a = _rope_tables(seq, ctx_len, MLA_ROPE)
    cos_b, sin_b = _rope_tables(seq, ctx_len, GQA_DIM)
    sq = MLA_QK ** -0.5 * LOG2E
    zq = jnp.zeros((t, MLA_QK_PAD - MLA_QK), F32)
    cosq = jnp.concatenate([jnp.full((t, MLA_NOPE), sq, F32), cos_a * sq, zq], axis=1)
    sinq = jnp.concatenate([jnp.zeros((t, MLA_NOPE), F32), sin_a * sq, zq], axis=1)
    mla_tabs = (cosq, sinq) + _roll_tables(cos_a, sin_a, MLA_ROPE, 1.0)
    gqa_tabs = (_roll_tables(cos_b, sin_b, GQA_DIM, GQA_DIM ** -0.5 * LOG2E)
                + _roll_tables(cos_b, sin_b, GQA_DIM, 1.0))

    c_rows = jnp.zeros((8, d), F32).at[:nb].set(c).at[nb].set(c_ctx)
    tabs = [_mod_rows(c_rows, w_mod[l], b_mod[l]).reshape(8, 1, 3 * d) for l in range(depth)]

    xc = (ctx, x)
    xm = _ln_mod(ctx, x, tabs[0], nb)

    for l in range(depth):
        last = l == depth - 1
        wp = _permute_w_in(w_in[l])
        p2, krdt = _in_proj(xm.reshape(nb * t, d), wp, P_DTYPE)
        p3 = p2.reshape(nb, t, P_WIDTH)

        wqm, wqr = _mla_q_weights(mla_w_uq[l])
        qa, ka, va = _mla_prep(p3, mla_q_norm[l].reshape(1, -1), mla_kv_norm[l].reshape(1, -1),
                               wqm, wqr, mla_w_ukv[l].astype(BF16), mla_tabs, ctx_len)
        ya = _attention(qa, ka, va, ctx_len, 256, 2, not last, Y_DTYPE)
        qb, kb, vb = _gqa_prep(p3, gqa_q_norm[l].reshape(1, -1), gqa_k_norm[l].reshape(1, -1),
                               gqa_tabs, ctx_len)
        yb = _attention(qb, kb, vb, ctx_len, 128, 1, not last, Y_DTYPE)

        xconv_t, bconv = _conv(p3, ssd_conv_w[l], ssd_conv_b[l], ctx_len, Y_DTYPE)
        dt5 = krdt[:, 64:].reshape(nb, t, 2, SSD_G, SSD_E)
        dt_col = jnp.transpose(dt5, (0, 2, 3, 1, 4))
        dt_row = jnp.transpose(dt5, (0, 2, 3, 4, 1))
        bias = ssd_dt_bias[l].astype(F32).reshape(2, SSD_G, SSD_E)
        a = -jnp.exp(ssd_a_log[l].astype(F32)).reshape(2, SSD_G, SSD_E)
        d_rows = jnp.broadcast_to(jnp.repeat(ssd_d[l].astype(F32), SSD_P).reshape(SSD_G, SSD_GW, 1),
                                  (SSD_G, SSD_GW, SSD_Q))
        yf, ybk = _ssd(xconv_t, bconv, dt_col, dt_row, bias[:, :, None, :], bias[:, :, :, None],
                       a[:, :, None, :], a[:, :, :, None], d_rows, ctx_len, Y_DTYPE)

        row_off = ctx_len if last else 0
        u = _merge(ya, yb, yf, ybk, p3, ssd_norm[l], w_br_a[l].astype(BF16), w_br_b[l].astype(BF16),
                   w_br_c[l].astype(BF16), row_off)
        if last:
            (xo,) = _out_ln(u, w_out[l].astype(BF16), xc, tabs[l], ln_g[l], ln_b[l], None, nb,
                            ctx_len, row_off)
            return xo
        xc, xm = _out_ln(u, w_out[l].astype(BF16), xc, tabs[l], ln_g[l], ln_b[l], tabs[l + 1], nb,
                         ctx_len, row_off)
```

```python
import functools
import math

import numpy as np
import jax
import jax.numpy as jnp
from jax import lax
from jax.experimental import pallas as pl
from jax.experimental.pallas import tpu as pltpu

F32 = jnp.float32
BF16 = jnp.bfloat16
HIGHEST = lax.Precision.HIGHEST

D_MODEL = 2048
DEPTH = 2
GRID_W = 64
ROPE_THETA = 10000.0
EPS = 1e-6

MLA_HEADS = 8
MLA_Q_LORA = 512
MLA_KV_LORA = 256
MLA_NOPE = 128
MLA_ROPE = 64
MLA_V = 128
MLA_QK = MLA_NOPE + MLA_ROPE
MLA_QK_PAD = 256
MLA_WIDTH = MLA_HEADS * MLA_V

GQA_HEADS = 8
GQA_KV_HEADS = 2
GQA_GROUP = GQA_HEADS // GQA_KV_HEADS
GQA_DIM = 128
GQA_WIDTH = GQA_HEADS * GQA_DIM
GQA_KV_WIDTH = GQA_KV_HEADS * GQA_DIM

SSD_INNER = D_MODEL
SSD_P = 64
SSD_HEADS = SSD_INNER // SSD_P
SSD_G = 4
SSD_E = SSD_HEADS // SSD_G
SSD_N = 128
SSD_CONV = 5
SSD_Q = 128
SSD_ROWS = 256
SSD_GPS = 4
SSD_GW = SSD_E * SSD_P
SSD_CONV_DIM = SSD_INNER + 2 * SSD_G * SSD_N

N_BRANCH = 3
IN_SPLITS = (MLA_Q_LORA, MLA_KV_LORA, MLA_ROPE, MLA_WIDTH, GQA_WIDTH, GQA_KV_WIDTH, GQA_KV_WIDTH,
             GQA_WIDTH, SSD_INNER, SSD_CONV_DIM, 2 * SSD_HEADS, N_BRANCH * D_MODEL)
IN_NAMES = ('cq', 'ckv', 'kr', 'ga', 'gq', 'gk', 'gv', 'gb', 'z', 'xbc', 'dtr', 'mg')
IN_OFFSETS = dict(zip(IN_NAMES, np.concatenate([[0], np.cumsum(IN_SPLITS)[:-1]]).tolist()))
IN_WIDTHS = dict(zip(IN_NAMES, IN_SPLITS))
P_ORDER = ('mg', 'z', 'xbc', 'ga', 'gq', 'gb', 'cq', 'ckv', 'gk', 'gv', 'kr', 'dtr')
P_OFF = {}
_o = 0
for _n in P_ORDER:
    P_OFF[_n] = _o
    _o += IN_WIDTHS[_n]
P_USED = _o
P_TN = 512
IN_TM = 4352
P_WIDTH = -(-P_USED // P_TN) * P_TN

DEEPNORM_ALPHA = (2 * DEPTH) ** 0.25

VMEM_LIMIT = 56 * 2 ** 20


def _params(*sem):
    return pltpu.CompilerParams(dimension_semantics=sem, vmem_limit_bytes=VMEM_LIMIT)


def _tile(n, target, align=8):
    t = min(n, target)
    while t > align and (n % t or t % align):
        t -= align
    assert n % t == 0, (n, target)
    return t


def _silu(v):
    return v * jax.nn.sigmoid(v)


def _softplus(v):
    return jnp.maximum(v, 0.0) + jnp.log1p(jnp.exp(-jnp.abs(v)))


def _layer_norm(v):
    mu = jnp.mean(v, axis=-1, keepdims=True)
    vc = v - mu
    var = jnp.mean(vc * vc, axis=-1, keepdims=True)
    return vc * lax.rsqrt(var + EPS)


def _rms(v):
    return v * lax.rsqrt(jnp.mean(v * v, axis=-1, keepdims=True) + EPS)


def _mod_kernel(c_ref, w_ref, b_ref, o_ref):
    a = _silu(c_ref[...]).astype(BF16)
    o_ref[...] = jnp.dot(a, w_ref[...].astype(BF16), preferred_element_type=F32) + b_ref[...]


def _mod_rows(c_rows, w_mod, b_mod):
    r, d = c_rows.shape
    n = w_mod.shape[1]
    tn = 512
    return pl.pallas_call(
        _mod_kernel,
        grid=(n // tn,),
        in_specs=[pl.BlockSpec((r, d), lambda j: (0, 0)),
                  pl.BlockSpec((d, tn), lambda j: (0, j)),
                  pl.BlockSpec((1, tn), lambda j: (0, j))],
        out_specs=pl.BlockSpec((r, tn), lambda j: (0, j)),
        out_shape=jax.ShapeDtypeStruct((r, n), F32),
        compiler_params=_params("arbitrary"),
        name="mod_rows",
    )(c_rows, w_mod, b_mod.reshape(1, n))


def _residual_rows(ctx_ref, x_ref, ctx_tiles, ro):
    return jnp.where(pl.program_id(1) + ro < ctx_tiles, ctx_ref[0], x_ref[0])


def _residual_specs(tm, d, ctx_tiles, ro):
    return [pl.BlockSpec((1, tm, d), lambda bi, i: (bi, jnp.minimum(i + ro, ctx_tiles - 1), 0)),
            pl.BlockSpec((1, tm, d), lambda bi, i: (bi, jnp.maximum(i + ro - ctx_tiles, 0), 0))]


def _ln_mod_kernel(ctx_ref, x_ref, sh_ref, sc_ref, o_ref, *, ctx_tiles):
    y = _layer_norm(_residual_rows(ctx_ref, x_ref, ctx_tiles, 0))
    o_ref[0] = (y * (1.0 + sc_ref[0]) + sh_ref[0]).astype(o_ref.dtype)


def _mod_row_index(nb, ctx_tiles):
    return lambda b, i: jnp.where(i < ctx_tiles, nb, b)


def _ln_mod(ctx, x, tab, nb):
    b, seq, d = x.shape
    ctx_len = ctx.shape[1]
    tm = _tile(ctx_len, 256)
    ctx_tiles = ctx_len // tm
    row = _mod_row_index(nb, ctx_tiles)
    return pl.pallas_call(
        functools.partial(_ln_mod_kernel, ctx_tiles=ctx_tiles),
        grid=(b, (ctx_len + seq) // tm),
        in_specs=_residual_specs(tm, d, ctx_tiles, 0)
        + [pl.BlockSpec((1, 1, d), lambda bi, i: (row(bi, i), 0, 0)),
           pl.BlockSpec((1, 1, d), lambda bi, i: (row(bi, i), 0, 1))],
        out_specs=pl.BlockSpec((1, tm, d), lambda bi, i: (bi, i, 0)),
        out_shape=jax.ShapeDtypeStruct((b, ctx_len + seq, d), BF16),
        compiler_params=_params("parallel", "parallel"),
        name="ln_mod",
    )(ctx, x, tab, tab)


def _in_proj_kernel(x_ref, w_ref, o_ref, dt_ref, *, dt_tile, dt_col):
    acc = jnp.dot(x_ref[...], w_ref[...], preferred_element_type=F32)
    o_ref[...] = acc.astype(o_ref.dtype)

    @pl.when(pl.program_id(1) == dt_tile)
    def _():
        dt_ref[...] = acc[:, dt_col:dt_col + 128]


def _in_proj(xm2, wp, out_dtype):
    m, k = xm2.shape
    n = wp.shape[1]
    tm = _tile(m, IN_TM)
    tn = P_TN
    dt_tile, dt_col = divmod(P_OFF['kr'], tn)
    assert P_OFF['dtr'] == P_OFF['kr'] + 64 and dt_col % 128 == 0
    return pl.pallas_call(
        functools.partial(_in_proj_kernel, dt_tile=dt_tile, dt_col=dt_col),
        grid=(m // tm, n // tn),
        in_specs=[pl.BlockSpec((tm, k), lambda i, j: (i, 0), pipeline_mode=pl.Buffered(1)),
                  pl.BlockSpec((k, tn), lambda i, j: (0, j))],
        out_specs=[pl.BlockSpec((tm, tn), lambda i, j: (i, j)),
                   pl.BlockSpec((tm, 128), lambda i, j: (i, 0))],
        out_shape=[jax.ShapeDtypeStruct((m, n), out_dtype), jax.ShapeDtypeStruct((m, 128), F32)],
        compiler_params=_params("parallel", "arbitrary"),
        name="in_proj",
    )(xm2, wp)


def _mla_prep_kernel(cq_ref, ckv_ref, kr_ref, qn_ref, kvn_ref, wqm_ref, wqr_ref, wkv_ref,
                     cosq_ref, sinq_ref, ck_ref, s1_ref, s2_ref, qa_ref, ka_ref, va_ref):
    cqn = (_rms(cq_ref[0].astype(F32)) * qn_ref[...]).astype(BF16)
    qm = jnp.dot(cqn, wqm_ref[...], preferred_element_type=F32)
    qr = jnp.dot(cqn, wqr_ref[...], preferred_element_type=F32)
    cosq = cosq_ref[...]
    sinq = sinq_ref[...]
    for h in range(MLA_HEADS):
        sl = slice(h * MLA_QK_PAD, (h + 1) * MLA_QK_PAD)
        qa_ref[0, h] = (qm[:, sl] * cosq + qr[:, sl] * sinq).astype(qa_ref.dtype)
    ckvn = (_rms(ckv_ref[0].astype(F32)) * kvn_ref[...]).astype(BF16)
    kv = jnp.dot(ckvn, wkv_ref[...], preferred_element_type=F32)
    kr = kr_ref[0].astype(F32)
    kpe = (kr * ck_ref[...] + pltpu.roll(kr, 128 - 16, axis=1) * s1_ref[...]
           + pltpu.roll(kr, 16, axis=1) * s2_ref[...]).astype(ka_ref.dtype)
    for h in range(MLA_HEADS):
        base = h * (MLA_NOPE + MLA_V)
        ka_ref[0, h, :, 0:MLA_NOPE] = kv[:, base:base + MLA_NOPE].astype(ka_ref.dtype)
        ka_ref[0, h, :, MLA_NOPE:MLA_QK_PAD] = kpe
        va_ref[0, h] = kv[:, base + MLA_NOPE:base + MLA_NOPE + MLA_V].T.astype(va_ref.dtype)


def _mla_prep(p3, qn, kvn, wqm, wqr, wkv, tabs, ctx_len):
    b, t, _ = p3.shape
    tm = _tile(ctx_len, 256)
    cosq, sinq, ck, s1, s2 = tabs

    def col(name, width):
        blk = P_OFF[name] // width
        return pl.BlockSpec((1, tm, width), lambda bi, i: (bi, i, blk))

    def full(a):
        return pl.BlockSpec(a.shape, lambda bi, i: (0,) * a.ndim)

    def rows(a):
        return pl.BlockSpec((tm, a.shape[1]), lambda bi, i: (i, 0))

    hm = lambda w: pl.BlockSpec((1, MLA_HEADS, tm, w), lambda bi, i: (bi, 0, i, 0))
    return pl.pallas_call(
        _mla_prep_kernel,
        grid=(b, t // tm),
        in_specs=[col('cq', MLA_Q_LORA), col('ckv', MLA_KV_LORA), col('kr', 128),
                  full(qn), full(kvn), full(wqm), full(wqr), full(wkv),
                  rows(cosq), rows(sinq), rows(ck), rows(s1), rows(s2)],
        out_specs=[hm(MLA_QK_PAD), hm(MLA_QK_PAD),
                   pl.BlockSpec((1, MLA_HEADS, MLA_V, tm), lambda bi, i: (bi, 0, 0, i))],
        out_shape=[jax.ShapeDtypeStruct((b, MLA_HEADS, t, MLA_QK_PAD), BF16),
                   jax.ShapeDtypeStruct((b, MLA_HEADS, t, MLA_QK_PAD), BF16),
                   jax.ShapeDtypeStruct((b, MLA_HEADS, MLA_V, t), BF16)],
        compiler_params=_params("parallel", "parallel"),
        name="mla_prep",
    )(p3, p3, p3, qn, kvn, wqm, wqr, wkv, cosq, sinq, ck, s1, s2)


def _rope128(y, c, s1, s2):
    return y * c + pltpu.roll(y, 128 - 32, axis=1) * s1 + pltpu.roll(y, 32, axis=1) * s2


def _gqa_prep_kernel(gq_ref, gk_ref, gv_ref, qn_ref, kn_ref, cq_ref, s1q_ref, s2q_ref,
                     ck_ref, s1k_ref, s2k_ref, qb_ref, kb_ref, vb_ref):
    gq = gq_ref[0].astype(F32)
    for h in range(GQA_HEADS):
        y = _rms(gq[:, h * GQA_DIM:(h + 1) * GQA_DIM]) * qn_ref[...]
        qb_ref[0, h] = _rope128(y, cq_ref[...], s1q_ref[...], s2q_ref[...]).astype(qb_ref.dtype)
    gk = gk_ref[0].astype(F32)
    gv = gv_ref[0].astype(F32)
    for h in range(GQA_KV_HEADS):
        y = _rms(gk[:, h * GQA_DIM:(h + 1) * GQA_DIM]) * kn_ref[...]
        kb_ref[0, h] = _rope128(y, ck_ref[...], s1k_ref[...], s2k_ref[...]).astype(kb_ref.dtype)
        vb_ref[0, h] = gv[:, h * GQA_DIM:(h + 1) * GQA_DIM].T.astype(vb_ref.dtype)


def _gqa_prep(p3, qn, kn, tabs, ctx_len):
    b, t, _ = p3.shape
    tm = _tile(ctx_len, 256)

    def col(name, width):
        blk = P_OFF[name] // width
        return pl.BlockSpec((1, tm, width), lambda bi, i: (bi, i, blk))

    def full(a):
        return pl.BlockSpec(a.shape, lambda bi, i: (0,) * a.ndim)

    def rows(a):
        return pl.BlockSpec((tm, a.shape[1]), lambda bi, i: (i, 0))

    hm = lambda nh: pl.BlockSpec((1, nh, tm, GQA_DIM), lambda bi, i: (bi, 0, i, 0))
    hmt = lambda nh: pl.BlockSpec((1, nh, GQA_DIM, tm), lambda bi, i: (bi, 0, 0, i))
    return pl.pallas_call(
        _gqa_prep_kernel,
        grid=(b, t // tm),
        in_specs=[col('gq', GQA_WIDTH), col('gk', GQA_KV_WIDTH), col('gv', GQA_KV_WIDTH),
                  full(qn), full(kn)] + [rows(a) for a in tabs],
        out_specs=[hm(GQA_HEADS), hm(GQA_KV_HEADS), hmt(GQA_KV_HEADS)],
        out_shape=[jax.ShapeDtypeStruct((b, GQA_HEADS, t, GQA_DIM), BF16),
                   jax.ShapeDtypeStruct((b, GQA_KV_HEADS, t, GQA_DIM), BF16),
                   jax.ShapeDtypeStruct((b, GQA_KV_HEADS, GQA_DIM, t), BF16)],
        compiler_params=_params("parallel", "parallel"),
        name="gqa_prep",
    )(p3, p3, p3, qn, kn, *tabs)


ATTN_KC = 256
ATTN_QB = 128
LOG2E = math.log2(math.e)


def _attn_logits(qts, k_ref, s_buf, nkeys):
    nqb = qts[0].shape[0] // ATTN_QB
    for j, q in enumerate(qts):
        st = lax.dot_general(k_ref[0, j, 0:nkeys, :], q, (((1,), (1,)), ((), ())),
                             preferred_element_type=F32)
        for b in range(nqb):
            s_buf[j * nqb + b, 0:nkeys, :] = st[:, b * ATTN_QB:(b + 1) * ATTN_QB]


def _attn_softmax_slab(s_buf, p_buf, l_buf, b, nkeys):
    parts = [jnp.max(s_buf[b, k0:k0 + ATTN_KC, :].reshape(ATTN_KC // 64, 8, 8, ATTN_QB), axis=0)
             for k0 in range(0, nkeys, ATTN_KC)]
    while len(parts) > 1:
        parts = [jnp.maximum(parts[i], parts[i + 1]) if i + 1 < len(parts) else parts[i]
                 for i in range(0, len(parts), 2)]
    m = jnp.max(parts[0], axis=(0, 1), keepdims=True)[0]
    lacc = jnp.zeros((8, 8, ATTN_QB), F32)
    for k0 in range(0, nkeys, ATTN_KC):
        p = jnp.exp2(s_buf[b, k0:k0 + ATTN_KC, :] - m)
        lacc = lacc + jnp.sum(p.reshape(ATTN_KC // 64, 8, 8, ATTN_QB), axis=0)
        p_buf[b, k0:k0 + ATTN_KC, :] = p.astype(p_buf.dtype)
    l = jnp.sum(lacc, axis=(0, 1), keepdims=True)[0]
    l_buf[b] = jnp.broadcast_to(1.0 / l, (8, ATTN_QB))


def _attn_output(vt_ref, p_buf, l_buf, o_ref, nkeys, hps, group, tq):
    dv = vt_ref.shape[2]
    nqb = group * tq // ATTN_QB
    for j in range(hps):
        slabs = range(j * nqb, (j + 1) * nqb)
        pt = jnp.concatenate([p_buf[b, 0:nkeys, :] for b in slabs], axis=1)
        ot = jnp.dot(vt_ref[0, j, :, 0:nkeys], pt, preferred_element_type=F32)
        ot = ot * jnp.concatenate([l_buf[b, 0:1, :] for b in slabs], axis=1)
        for g in range(group):
            c0 = (j * group + g) * dv
            o_ref[0, :, c0:c0 + dv] = ot[:, g * tq:(g + 1) * tq].T.astype(o_ref.dtype)


def _attn_queries(q_ref, hps, group):
    return [jnp.concatenate([q_ref[0, j * group + g] for g in range(group)], axis=0)
            for j in range(hps)]


def _attn_ctx_kernel(q_ref, k_ref, vt_ref, o_ref, s_sc, p_sc, l_sc, *, group, tq):
    hps, nkeys = k_ref.shape[1], k_ref.shape[2]
    nslab = hps * group * tq // ATTN_QB
    _attn_logits(_attn_queries(q_ref, hps, group), k_ref, s_sc, nkeys)

    def slab(b, carry):
        _attn_softmax_slab(s_sc, p_sc, l_sc, b, nkeys)
        return carry

    lax.fori_loop(0, nslab, slab, 0)
    _attn_output(vt_ref, p_sc, l_sc, o_ref, nkeys, hps, group, tq)


def _attn_lat_kernel(q_ref, k_ref, vt_ref, o_ref, s0, s1, p0, p1, l0, l1, *, group, tq):
    hps, nkeys = k_ref.shape[1], k_ref.shape[2]
    nslab = hps * group * tq // ATTN_QB
    g = pl.program_id(0)

    @pl.when(g == 0)
    def _():
        for buf in (s0, s1, p0, p1, l0, l1):
            buf[...] = jnp.zeros(buf.shape, buf.dtype)

    def step(s_a, s_b, p_b, p_c, l_b, l_c):
        _attn_logits(_attn_queries(q_ref, hps, group), k_ref, s_a, nkeys)
        for b in range(nslab):
            _attn_softmax_slab(s_b, p_b, l_b, b, nkeys)
        _attn_output(vt_ref, p_c, l_c, o_ref, nkeys, hps, group, tq)

    pl.when(g % 2 == 0)(lambda: step(s0, s1, p1, p0, l1, l0))
    pl.when(g % 2 == 1)(lambda: step(s1, s0, p0, p1, l0, l1))


def _attention(q, k, vt, ctx_len, tq, hps, with_ctx_queries, out_dtype):
    b, hq, t, dk = q.shape
    hkv, dv = k.shape[1], vt.shape[2]
    group = hq // hkv
    tq = _tile(ctx_len, tq, 128)
    nslab = hps * group * tq // ATTN_QB
    ctx_tiles = ctx_len // tq
    nq = t // tq - ctx_tiles
    width = hps * group * dv
    assert t % ATTN_KC == 0 and ctx_len % ATTN_KC == 0 and hkv % hps == 0

    def scratch(nkeys):
        return [pltpu.VMEM((nslab, nkeys, ATTN_QB), F32), pltpu.VMEM((nslab, nkeys, ATTN_QB), BF16),
                pltpu.VMEM((nslab, 8, ATTN_QB), F32)]

    s_lat, p_lat, l_lat = scratch(t)
    nh = hkv // hps
    items = b * nh * nq

    def item(step):
        i = jnp.clip(step, 0, items - 1)
        return i // (nh * nq), (i // nq) % nh, i % nq

    def q_map(g):
        bi, h, qi = item(g)
        return bi, h, qi + ctx_tiles, 0

    def k_map(g):
        bi, h, _ = item(g)
        return bi, h, 0, 0

    def v_map(g):
        bi, h, _ = item(g - 2)
        return bi, h, 0, 0

    def o_map(g):
        bi, h, qi = item(g - 2)
        return bi, qi, h

    y_lat = pl.pallas_call(
        functools.partial(_attn_lat_kernel, group=group, tq=tq),
        grid=(items + 2,),
        in_specs=[pl.BlockSpec((1, hps * group, tq, dk), q_map),
                  pl.BlockSpec((1, hps, t, dk), k_map),
                  pl.BlockSpec((1, hps, dv, t), v_map)],
        out_specs=pl.BlockSpec((1, tq, width), o_map),
        out_shape=jax.ShapeDtypeStruct((b, nq * tq, hq * dv), out_dtype),
        scratch_shapes=[s_lat, s_lat, p_lat, p_lat, l_lat, l_lat],
        compiler_params=_params("arbitrary"),
        name="attention_lat_dk%d" % dk,
    )(q, k, vt)
    if not with_ctx_queries:
        return y_lat
    y_ctx = pl.pallas_call(
        functools.partial(_attn_ctx_kernel, group=group, tq=tq),
        grid=(b, hkv // hps, ctx_tiles),
        in_specs=[pl.BlockSpec((1, hps * group, tq, dk), lambda bi, h, i: (bi, h, i, 0)),
                  pl.BlockSpec((1, hps, ctx_len, dk), lambda bi, h, i: (bi, h, 0, 0)),
                  pl.BlockSpec((1, hps, dv, ctx_len), lambda bi, h, i: (bi, h, 0, 0))],
        out_specs=pl.BlockSpec((1, tq, width), lambda bi, h, i: (bi, i, h)),
        out_shape=jax.ShapeDtypeStruct((b, ctx_len, hq * dv), out_dtype),
        scratch_shapes=scratch(ctx_len),
        compiler_params=_params("parallel", "parallel", "arbitrary"),
        name="attention_ctx_dk%d" % dk,
    )(q, k, vt)
    return jnp.concatenate([y_ctx, y_lat], axis=1)


def _conv_kernel(x_ref, w_ref, b_ref, xt_ref, bc_ref, pad_sc, *, ctx_len, rows, x_blocks):
    t = x_ref.shape[1]
    nch = x_ref.shape[2]
    halo = 8
    segs = ((0, ctx_len), (ctx_len, t))
    zeros = jnp.zeros((halo, nch), F32)
    for si, (lo, hi) in enumerate(segs):
        pad_sc[lo + si * halo:lo + (si + 1) * halo, :] = zeros
        for r0 in range(lo, hi, rows):
            pad_sc[r0 + (si + 1) * halo:r0 + (si + 1) * halo + rows, :] = x_ref[0, r0:r0 + rows, :].astype(F32)
    pad_sc[t + 2 * halo:t + 3 * halo, :] = zeros
    w = w_ref[...]
    bias = b_ref[...]

    def emit(transposed):
        for si, (lo, hi) in enumerate(segs):
            for r0 in range(lo, hi, rows):
                base = r0 + (si + 1) * halo - SSD_CONV // 2
                acc = bias + w[0:1, :] * pad_sc[base:base + rows, :]
                for kk in range(1, SSD_CONV):
                    acc = acc + w[kk:kk + 1, :] * pad_sc[base + kk:base + kk + rows, :]
                y = _silu(acc)
                if transposed:
                    xt_ref[0, :, r0:r0 + rows] = y.T.astype(xt_ref.dtype)
                else:
                    bc_ref[0, r0:r0 + rows, :] = y.astype(bc_ref.dtype)

    is_x = pl.program_id(1) < x_blocks
    pl.when(is_x)(lambda: emit(True))
    pl.when(jnp.logical_not(is_x))(lambda: emit(False))


def _conv(p3, conv_w, conv_b, ctx_len, out_dtype):
    b, t, _ = p3.shape
    nch = 256
    rows = _tile(ctx_len, 256)
    blk0 = P_OFF['xbc'] // nch
    x_blocks = SSD_INNER // nch
    kern = functools.partial(_conv_kernel, ctx_len=ctx_len, rows=rows, x_blocks=x_blocks)
    return pl.pallas_call(
        kern,
        grid=(b, SSD_CONV_DIM // nch),
        in_specs=[pl.BlockSpec((1, t, nch), lambda bi, j: (bi, 0, blk0 + j)),
                  pl.BlockSpec((SSD_CONV, nch), lambda bi, j: (0, j)),
                  pl.BlockSpec((1, nch), lambda bi, j: (0, j))],
        out_specs=[pl.BlockSpec((1, nch, t), lambda bi, j: (bi, jnp.minimum(j, x_blocks - 1), 0)),
                   pl.BlockSpec((1, t, nch), lambda bi, j: (bi, 0, jnp.maximum(j - x_blocks, 0)))],
        out_shape=[jax.ShapeDtypeStruct((b, SSD_INNER, t), out_dtype),
                   jax.ShapeDtypeStruct((b, t, SSD_CONV_DIM - SSD_INNER), out_dtype)],
        scratch_shapes=[pltpu.VMEM((t + 24, nch), F32)],
        compiler_params=_params("parallel", "arbitrary"),
        name="ssd_conv",
    )(p3, conv_w, conv_b.reshape(1, SSD_CONV_DIM))


def _split3(v):
    hi = v.astype(BF16)
    r1 = v - hi.astype(F32)
    mid = r1.astype(BF16)
    lo = (r1 - mid.astype(F32)).astype(BF16)
    return hi, mid, lo


def _expand_rows(v):
    q = v.shape[1]
    return jnp.concatenate([jnp.broadcast_to(v[e:e + 1, :], (SSD_P, q)) for e in range(SSD_E)], axis=0)


def _ssd_decays(dt_col_raw, dt_row_raw, bias_col, bias_row, a_col, a_row, backward):
    q = dt_col_raw.shape[0]
    ri = lax.broadcasted_iota(jnp.int32, (q, q), 0)
    ci = lax.broadcasted_iota(jnp.int32, (q, q), 1)
    tri_col = jnp.where((ri <= ci) if backward else (ri >= ci), 1.0, 0.0).astype(BF16)
    tri_row = jnp.where((ri >= ci) if backward else (ri <= ci), 1.0, 0.0).astype(BF16)
    dt_c = _softplus(dt_col_raw + bias_col)
    dt_r = _softplus(dt_row_raw + bias_row)
    cum_c = sum(jnp.dot(tri_col, part, preferred_element_type=F32) for part in _split3(dt_c * a_col))
    cum_r = sum(jnp.dot(part, tri_row, preferred_element_type=F32) for part in _split3(dt_r * a_row))
    total = jnp.broadcast_to(cum_r[:, 0:1] if backward else cum_r[:, q - 1:q], (SSD_E, q))
    return dt_r, cum_c, cum_r, total


def _ssd_direction(xt, bm, cm, decays, h_ref, backward):
    q = xt.shape[1]
    dt_r, cum_c, cum_r, total = decays
    ri = lax.broadcasted_iota(jnp.int32, (q, q), 0)
    ci = lax.broadcasted_iota(jnp.int32, (q, q), 1)
    keep_t = (ci <= ri) if backward else (ci >= ri)

    bmb = bm.astype(BF16)
    cmb = cm.astype(BF16)
    nt = (((1,), (1,)), ((), ()))
    cbt = lax.dot_general(bmb, cmb, nt, preferred_element_type=F32)
    h = h_ref[...]
    y_off = lax.dot_general(h.astype(BF16), cmb, nt, preferred_element_type=F32) * _expand_rows(jnp.exp(cum_r))
    wgt = (xt * _expand_rows(dt_r * jnp.exp(total - cum_r))).astype(BF16)
    h_ref[...] = _expand_rows(jnp.exp(total)) * h + jnp.dot(wgt, bmb, preferred_element_type=F32)

    xdt = (xt * _expand_rows(dt_r)).astype(BF16)
    parts = []
    for e in range(SSD_E):
        seg = cum_r[e:e + 1, :] - cum_c[:, e:e + 1]
        dec = jnp.exp(jnp.where(keep_t, seg, -jnp.inf))
        parts.append(jnp.dot(xdt[e * SSD_P:(e + 1) * SSD_P, :], (cbt * dec).astype(BF16),
                             preferred_element_type=F32))
    return jnp.concatenate(parts, axis=0) + y_off


def _ssd_kernel(xf_ref, bf_ref, cf_ref, xb_ref, bb_ref, cb_ref, dcf_ref, drf_ref, dcb_ref, drb_ref,
                bias_c_ref, bias_r_ref, a_c_ref, a_r_ref, d_ref, yf_ref, yb_ref, hf_sc, hb_sc):
    @pl.when(pl.program_id(2) == 0)
    def _():
        hf_sc[...] = jnp.zeros(hf_sc.shape, F32)
        hb_sc[...] = jnp.zeros(hb_sc.shape, F32)

    nsub = xf_ref.shape[2] // SSD_Q
    gps = dcf_ref.shape[2]
    chunks = [slice(i * SSD_Q, (i + 1) * SSD_Q) for i in range(nsub)]
    dec_f = [[_ssd_decays(dcf_ref[0, 0, gi, rs, :], drf_ref[0, 0, gi, :, rs], bias_c_ref[0, gi],
                          bias_r_ref[0, gi], a_c_ref[0, gi], a_r_ref[0, gi], False) for rs in chunks]
             for gi in range(gps)]
    dec_b = [[_ssd_decays(dcb_ref[0, 0, gi, rs, :], drb_ref[0, 0, gi, :, rs], bias_c_ref[1, gi],
                          bias_r_ref[1, gi], a_c_ref[1, gi], a_r_ref[1, gi], True) for rs in chunks]
             for gi in range(gps)]
    for gi in range(gps):
        ch = slice(gi * SSD_GW, (gi + 1) * SSD_GW)
        st = slice(gi * SSD_N, (gi + 1) * SSD_N)
        for i in range(nsub):
            rf, rb = chunks[i], chunks[nsub - 1 - i]
            xt = xf_ref[0, ch, rf].astype(F32)
            yf = _ssd_direction(xt, bf_ref[0, rf, st].astype(F32), cf_ref[0, rf, st].astype(F32),
                                dec_f[gi][i], hf_sc.at[ch], False)
            yf_ref[0, ch, rf] = (yf + d_ref[gi] * xt).astype(yf_ref.dtype)
            yb = _ssd_direction(xb_ref[0, ch, rb].astype(F32), bb_ref[0, rb, st].astype(F32),
                                cb_ref[0, rb, st].astype(F32), dec_b[gi][nsub - 1 - i], hb_sc.at[ch], True)
            yb_ref[0, ch, rb] = yb.astype(yb_ref.dtype)


def _ssd(xt, bc, dt_col, dt_row, bias_c, bias_r, a_c, a_r, d_rows, ctx_len, out_dtype):
    b, _, t = xt.shape
    assert SSD_Q == SSD_N
    rows = _tile(ctx_len, SSD_ROWS, SSD_Q)
    nblk = t // rows
    nctx = ctx_len // rows

    def bidx(c):
        return jnp.where(c < nctx, nctx - 1 - c, nblk - 1 - (c - nctx))

    gps = SSD_GPS
    ngs = SSD_G // gps
    fx = lambda bi, g, c: (bi, g, c)
    fb = lambda bi, g, c: (bi, c, g)
    fc = lambda bi, g, c: (bi, c, ngs + g)
    bx = lambda bi, g, c: (bi, g, bidx(c))
    bb = lambda bi, g, c: (bi, bidx(c), g)
    bcm = lambda bi, g, c: (bi, bidx(c), ngs + g)
    small = lambda a: pl.BlockSpec((2, gps) + a.shape[2:], lambda bi, g, c: (0, g, 0, 0))
    return pl.pallas_call(
        _ssd_kernel,
        grid=(b, ngs, nblk),
        in_specs=[pl.BlockSpec((1, gps * SSD_GW, rows), fx), pl.BlockSpec((1, rows, gps * SSD_N), fb),
                  pl.BlockSpec((1, rows, gps * SSD_N), fc),
                  pl.BlockSpec((1, gps * SSD_GW, rows), bx), pl.BlockSpec((1, rows, gps * SSD_N), bb),
                  pl.BlockSpec((1, rows, gps * SSD_N), bcm),
                  pl.BlockSpec((1, 1, gps, rows, SSD_E), lambda bi, g, c: (bi, 0, g, c, 0)),
                  pl.BlockSpec((1, 1, gps, SSD_E, rows), lambda bi, g, c: (bi, 0, g, 0, c)),
                  pl.BlockSpec((1, 1, gps, rows, SSD_E), lambda bi, g, c: (bi, 1, g, bidx(c), 0)),
                  pl.BlockSpec((1, 1, gps, SSD_E, rows), lambda bi, g, c: (bi, 1, g, 0, bidx(c))),
                  small(bias_c), small(bias_r), small(a_c), small(a_r),
                  pl.BlockSpec((gps, SSD_GW, SSD_Q), lambda bi, g, c: (g, 0, 0))],
        out_specs=[pl.BlockSpec((1, gps * SSD_GW, rows), fx), pl.BlockSpec((1, gps * SSD_GW, rows), bx)],
        out_shape=[jax.ShapeDtypeStruct((b, SSD_INNER, t), out_dtype)] * 2,
        scratch_shapes=[pltpu.VMEM((gps * SSD_GW, SSD_N), F32), pltpu.VMEM((gps * SSD_GW, SSD_N), F32)],
        compiler_params=_params("parallel", "parallel", "arbitrary"),
        name="ssd_scan",
    )(xt, bc, bc, xt, bc, bc, dt_col, dt_row, dt_col, dt_row, bias_c, bias_r, a_c, a_r, d_rows)


MERGE_TN = 512


def _merge_kernel(ya_ref, ga_ref, yb_ref, gb_ref, yf_ref, ybk_ref, z_ref, nrm_ref,
                  mg_ref, wa_ref, wb_ref, wc_ref, u_ref, c_sc):
    a_in = (ya_ref[0].astype(F32) * _silu(ga_ref[0].astype(F32))).astype(BF16)
    b_in = (yb_ref[0].astype(F32) * _silu(gb_ref[0].astype(F32))).astype(BF16)
    v = (yf_ref[0].astype(F32) + ybk_ref[0].astype(F32)).T * _silu(z_ref[0].astype(F32))
    for g in range(SSD_G):
        sl = slice(g * SSD_GW, (g + 1) * SSD_GW)
        c_sc[:, sl] = (_rms(v[:, sl]) * nrm_ref[:, sl]).astype(BF16)
    c_in = c_sc[...]
    d = u_ref.shape[-1]
    for j in range(0, d, MERGE_TN):
        sl = slice(j, j + MERGE_TN)
        br_a = jnp.dot(a_in, wa_ref[:, sl], preferred_element_type=F32)
        br_b = jnp.dot(b_in, wb_ref[:, sl], preferred_element_type=F32)
        br_c = jnp.dot(c_in, wc_ref[:, sl], preferred_element_type=F32)
        u = (jax.nn.sigmoid(mg_ref[0, :, j:j + MERGE_TN].astype(F32)) * br_a
             + jax.nn.sigmoid(mg_ref[0, :, d + j:d + j + MERGE_TN].astype(F32)) * br_b
             + jax.nn.sigmoid(mg_ref[0, :, 2 * d + j:2 * d + j + MERGE_TN].astype(F32)) * br_c)
        u_ref[0, :, sl] = u.astype(u_ref.dtype)


def _merge(ya, yb, yf, ybk, p3, ssd_norm, wa, wb, wc, row_off):
    b, rows, _ = ya.shape
    tm = _tile(rows, 256)
    d = D_MODEL
    ro = row_off // tm
    assert P_OFF['mg'] == 0

    def col(name, width):
        blk = P_OFF[name] // width
        return pl.BlockSpec((1, tm, width), lambda bi, i: (bi, i + ro, blk))

    loc = lambda w: pl.BlockSpec((1, tm, w), lambda bi, i: (bi, i, 0))
    glob = lambda w: pl.BlockSpec((1, tm, w), lambda bi, i: (bi, i + ro, 0))
    globt = pl.BlockSpec((1, SSD_INNER, tm), lambda bi, i: (bi, 0, i + ro))
    wspec = lambda k: pl.BlockSpec((k, d), lambda bi, i: (0, 0), pipeline_mode=pl.Buffered(1))
    return pl.pallas_call(
        _merge_kernel,
        grid=(b, rows // tm),
        in_specs=[loc(MLA_WIDTH), col('ga', MLA_WIDTH), loc(GQA_WIDTH), col('gb', GQA_WIDTH),
                  globt, globt, col('z', SSD_INNER),
                  pl.BlockSpec((1, SSD_INNER), lambda bi, i: (0, 0)),
                  glob(N_BRANCH * d), wspec(MLA_WIDTH), wspec(GQA_WIDTH), wspec(SSD_INNER)],
        out_specs=pl.BlockSpec((1, tm, d), lambda bi, i: (bi, i, 0)),
        out_shape=jax.ShapeDtypeStruct((b, rows, d), BF16),
        scratch_shapes=[pltpu.VMEM((tm, SSD_INNER), BF16)],
        compiler_params=_params("parallel", "parallel"),
        name="merge",
    )(ya, p3, yb, p3, yf, ybk, p3, ssd_norm.reshape(1, SSD_INNER), p3, wa, wb, wc)


def _out_ln_kernel(u_ref, w_ref, *rest, with_next, ctx_tiles, ro, split):
    if split:
        ctx_ref, x_ref, gate_ref, g_ref, b_ref, *rest = rest
        res = _residual_rows(ctx_ref, x_ref, ctx_tiles, ro)
    else:
        x_ref, gate_ref, g_ref, b_ref, *rest = rest
        res = x_ref[0]
    out = jnp.dot(u_ref[0], w_ref[...], preferred_element_type=F32)
    r = DEEPNORM_ALPHA * res + gate_ref[0] * out
    xn = _layer_norm(r) * g_ref[...] + b_ref[...]
    if with_next:
        sh_ref, sc_ref, xo_ref, xm_ref = rest
        xo_ref[0] = xn
        xm_ref[0] = (_layer_norm(xn) * (1.0 + sc_ref[0]) + sh_ref[0]).astype(xm_ref.dtype)
    else:
        (xo_ref,) = rest
        xo_ref[0] = xn


def _out_ln(u, w_out, xc, tab, ln_g, ln_b, next_tab, nb, ctx_len, row_off):
    b, rows, d = u.shape
    tm = _tile(ctx_len, 256)
    ro = row_off // tm
    ctx_tiles = ctx_len // tm
    row = _mod_row_index(nb, ctx_tiles)
    with_next = next_tab is not None
    split = isinstance(xc, tuple)
    loc = pl.BlockSpec((1, tm, d), lambda bi, i: (bi, i, 0))
    vec = pl.BlockSpec((1, d), lambda bi, i: (0, 0))
    res_specs = (_residual_specs(tm, d, ctx_tiles, ro) if split
                 else [pl.BlockSpec((1, tm, d), lambda bi, i: (bi, i + ro, 0))])
    in_specs = [loc, pl.BlockSpec((d, d), lambda bi, i: (0, 0))] + res_specs + [
        pl.BlockSpec((1, 1, d), lambda bi, i: (row(bi, i + ro), 0, 2)), vec, vec]
    args = [u, w_out] + (list(xc) if split else [xc]) + [tab, ln_g.reshape(1, d), ln_b.reshape(1, d)]
    out_specs = [loc]
    out_shape = [jax.ShapeDtypeStruct((b, rows, d), F32)]
    if with_next:
        in_specs += [pl.BlockSpec((1, 1, d), lambda bi, i: (row(bi, i + ro), 0, 0)),
                     pl.BlockSpec((1, 1, d), lambda bi, i: (row(bi, i + ro), 0, 1))]
        args += [next_tab, next_tab]
        out_specs.append(loc)
        out_shape.append(jax.ShapeDtypeStruct((b, rows, d), BF16))
    return pl.pallas_call(
        functools.partial(_out_ln_kernel, with_next=with_next, ctx_tiles=ctx_tiles, ro=ro, split=split),
        grid=(b, rows // tm),
        in_specs=in_specs,
        out_specs=out_specs,
        out_shape=out_shape,
        compiler_params=_params("parallel", "parallel"),
        name="out_ln",
    )(*args)


def _rope_angles(rows, dim):
    row, col = jnp.meshgrid(jnp.arange(rows, dtype=F32), jnp.arange(GRID_W, dtype=F32), indexing='ij')
    half = dim // 2
    inv_freq = ROPE_THETA ** (-jnp.arange(0, half, 2, dtype=F32) / half)
    ang_r = row.reshape(-1, 1) * inv_freq
    ang_c = col.reshape(-1, 1) * inv_freq
    return jnp.concatenate([ang_r, ang_r, ang_c, ang_c], axis=-1)


def _rope_tables(seq, ctx_len, dim):
    ang = _rope_angles(seq // GRID_W, dim)
    cos = jnp.concatenate([jnp.ones((ctx_len, dim), F32), jnp.cos(ang)], axis=0)
    sin = jnp.concatenate([jnp.zeros((ctx_len, dim), F32), jnp.sin(ang)], axis=0)
    return cos, sin


def _roll_tables(cos, sin, dim, scale):
    t = cos.shape[0]
    quarter = dim // 4
    first = (jnp.arange(dim) % (2 * quarter)) < quarter
    s1 = jnp.where(first, -sin, 0.0)
    s2 = jnp.where(first, 0.0, sin)
    pad = lambda a: jnp.pad(a * scale, ((0, 0), (0, 128 - dim)))
    return pad(cos), pad(s1), pad(s2)


def _rot_matrix(dim):
    quarter = dim // 4
    r = np.zeros((dim, dim), np.float32)
    for i in range(dim):
        blk = i // quarter
        if blk % 2 == 0:
            r[i + quarter, i] = -1.0
        else:
            r[i - quarter, i] = 1.0
    return jnp.asarray(r)


def _mla_q_weights(w_uq):
    w = w_uq.reshape(MLA_Q_LORA, MLA_HEADS, MLA_QK)
    nope, pe = w[..., :MLA_NOPE], w[..., MLA_NOPE:]
    pe_rot = jnp.einsum('khd,de->khe', pe, _rot_matrix(MLA_ROPE), precision=HIGHEST)
    zpad = jnp.zeros((MLA_Q_LORA, MLA_HEADS, MLA_QK_PAD - MLA_QK), F32)
    main = jnp.concatenate([nope, pe, zpad], axis=-1)
    rot = jnp.concatenate([jnp.zeros_like(nope), pe_rot, zpad], axis=-1)
    return (main.reshape(MLA_Q_LORA, -1).astype(BF16), rot.reshape(MLA_Q_LORA, -1).astype(BF16))


def _permute_w_in(w):
    parts = [w[:, IN_OFFSETS[n]:IN_OFFSETS[n] + IN_WIDTHS[n]] for n in P_ORDER]
    parts.append(jnp.zeros((w.shape[0], P_WIDTH - P_USED), w.dtype))
    return jnp.concatenate(parts, axis=1).astype(BF16)


P_DTYPE = BF16
Y_DTYPE = BF16


def kernel(x, c, ctx, c_ctx, w_mod, b_mod, w_in, mla_q_norm, mla_w_uq, mla_kv_norm, mla_w_ukv,
           gqa_q_norm, gqa_k_norm, ssd_conv_w, ssd_conv_b, ssd_a_log, ssd_dt_bias, ssd_d, ssd_norm,
           w_br_a, w_br_b, w_br_c, w_out, ln_g, ln_b):
    nb, seq, d = x.shape
    ctx_len = ctx.shape[1]
    t = ctx_len + seq
    depth = w_in.shape[0]
    assert d == D_MODEL and nb < 8 and seq % GRID_W == 0
    assert ctx_len % SSD_Q == 0 and seq % SSD_Q == 0

    cos_a, sin_a = _rope_tables(seq, ctx_len, MLA_ROPE)
    cos_b, sin_b = _rope_tables(seq, ctx_len, GQA_DIM)
    sq = MLA_QK ** -0.5 * LOG2E
    zq = jnp.zeros((t, MLA_QK_PAD - MLA_QK), F32)
    cosq = jnp.concatenate([jnp.full((t, MLA_NOPE), sq, F32), cos_a * sq, zq], axis=1)
    sinq = jnp.concatenate([jnp.zeros((t, MLA_NOPE), F32), sin_a * sq, zq], axis=1)
    mla_tabs = (cosq, sinq) + _roll_tables(cos_a, sin_a, MLA_ROPE, 1.0)
    gqa_tabs = (_roll_tables(cos_b, sin_b, GQA_DIM, GQA_DIM ** -0.5 * LOG2E)
                + _roll_tables(cos_b, sin_b, GQA_DIM, 1.0))

    c_rows = jnp.zeros((8, d), F32).at[:nb].set(c).at[nb].set(c_ctx)
    tabs = [_mod_rows(c_rows, w_mod[l], b_mod[l]).reshape(8, 1, 3 * d) for l in range(depth)]

    xc = (ctx, x)
    xm = _ln_mod(ctx, x, tabs[0], nb)

    for l in range(depth):
        last = l == depth - 1
        wp = _permute_w_in(w_in[l])
        p2, krdt = _in_proj(xm.reshape(nb * t, d), wp, P_DTYPE)
        p3 = p2.reshape(nb, t, P_WIDTH)

        wqm, wqr = _mla_q_weights(mla_w_uq[l])
        qa, ka, va = _mla_prep(p3, mla_q_norm[l].reshape(1, -1), mla_kv_norm[l].reshape(1, -1),
                               wqm, wqr, mla_w_ukv[l].astype(BF16), mla_tabs, ctx_len)
        ya = _attention(qa, ka, va, ctx_len, 256, 2, not last, Y_DTYPE)
        qb, kb, vb = _gqa_prep(p3, gqa_q_norm[l].reshape(1, -1), gqa_k_norm[l].reshape(1, -1),
                               gqa_tabs, ctx_len)
        yb = _attention(qb, kb, vb, ctx_len, 128, 1, not last, Y_DTYPE)

        xconv_t, bconv = _conv(p3, ssd_conv_w[l], ssd_conv_b[l], ctx_len, Y_DTYPE)
        dt5 = krdt[:, 64:].reshape(nb, t, 2, SSD_G, SSD_E)
        dt_col = jnp.transpose(dt5, (0, 2, 3, 1, 4))
        dt_row = jnp.transpose(dt5, (0, 2, 3, 4, 1))
        bias = ssd_dt_bias[l].astype(F32).reshape(2, SSD_G, SSD_E)
        a = -jnp.exp(ssd_a_log[l].astype(F32)).reshape(2, SSD_G, SSD_E)
        d_rows = jnp.broadcast_to(jnp.repeat(ssd_d[l].astype(F32), SSD_P).reshape(SSD_G, SSD_GW, 1),
                                  (SSD_G, SSD_GW, SSD_Q))
        yf, ybk = _ssd(xconv_t, bconv, dt_col, dt_row, bias[:, :, None, :], bias[:, :, :, None],
                       a[:, :, None, :], a[:, :, :, None], d_rows, ctx_len, Y_DTYPE)

        row_off = ctx_len if last else 0
        u = _merge(ya, yb, yf, ybk, p3, ssd_norm[l], w_br_a[l].astype(BF16), w_br_b[l].astype(BF16),
                   w_br_c[l].astype(BF16), row_off)
        if last:
            (xo,) = _out_ln(u, w_out[l].astype(BF16), xc, tabs[l], ln_g[l], ln_b[l], None, nb,
                            ctx_len, row_off)
            return xo
        xc, xm = _out_ln(u, w_out[l].astype(BF16), xc, tabs[l], ln_g[l], ln_b[l], tabs[l + 1], nb,
                         ctx_len, row_off)
```

```python
import functools
import math

import numpy as np
import jax
import jax.numpy as jnp
from jax import lax
from jax.experimental import pallas as pl
from jax.experimental.pallas import tpu as pltpu

F32 = jnp.float32
BF16 = jnp.bfloat16
HIGHEST = lax.Precision.HIGHEST

D_MODEL = 2048
DEPTH = 2
GRID_W = 64
ROPE_THETA = 10000.0
EPS = 1e-6

MLA_HEADS = 8
MLA_Q_LORA = 512
MLA_KV_LORA = 256
MLA_NOPE = 128
MLA_ROPE = 64
MLA_V = 128
MLA_QK = MLA_NOPE + MLA_ROPE
MLA_QK_PAD = 256
MLA_WIDTH = MLA_HEADS * MLA_V

GQA_HEADS = 8
GQA_KV_HEADS = 2
GQA_GROUP = GQA_HEADS // GQA_KV_HEADS
GQA_DIM = 128
GQA_WIDTH = GQA_HEADS * GQA_DIM
GQA_KV_WIDTH = GQA_KV_HEADS * GQA_DIM

SSD_INNER = D_MODEL
SSD_P = 64
SSD_HEADS = SSD_INNER // SSD_P
SSD_G = 4
SSD_E = SSD_HEADS // SSD_G
SSD_N = 128
SSD_CONV = 5
SSD_Q = 128
SSD_ROWS = 256
SSD_GPS = 4
SSD_GW = SSD_E * SSD_P
SSD_CONV_DIM = SSD_INNER + 2 * SSD_G * SSD_N

N_BRANCH = 3
IN_SPLITS = (MLA_Q_LORA, MLA_KV_LORA, MLA_ROPE, MLA_WIDTH, GQA_WIDTH, GQA_KV_WIDTH, GQA_KV_WIDTH,
             GQA_WIDTH, SSD_INNER, SSD_CONV_DIM, 2 * SSD_HEADS, N_BRANCH * D_MODEL)
IN_NAMES = ('cq', 'ckv', 'kr', 'ga', 'gq', 'gk', 'gv', 'gb', 'z', 'xbc', 'dtr', 'mg')
IN_OFFSETS = dict(zip(IN_NAMES, np.concatenate([[0], np.cumsum(IN_SPLITS)[:-1]]).tolist()))
IN_WIDTHS = dict(zip(IN_NAMES, IN_SPLITS))
P_ORDER = ('mg', 'z', 'xbc', 'ga', 'gq', 'gb', 'cq', 'ckv', 'gk', 'gv', 'kr', 'dtr')
P_OFF = {}
_o = 0
for _n in P_ORDER:
    P_OFF[_n] = _o
    _o += IN_WIDTHS[_n]
P_USED = _o
P_TN = 512
IN_TM = 4352
P_WIDTH = -(-P_USED // P_TN) * P_TN

DEEPNORM_ALPHA = (2 * DEPTH) ** 0.25

VMEM_LIMIT = 56 * 2 ** 20


def _params(*sem):
    return pltpu.CompilerParams(dimension_semantics=sem, vmem_limit_bytes=VMEM_LIMIT)


def _tile(n, target, align=8):
    t = min(n, target)
    while t > align and (n % t or t % align):
        t -= align
    assert n % t == 0, (n, target)
    return t


def _silu(v):
    return v * jax.nn.sigmoid(v)


def _softplus(v):
    return jnp.maximum(v, 0.0) + jnp.log1p(jnp.exp(-jnp.abs(v)))


def _layer_norm(v):
    mu = jnp.mean(v, axis=-1, keepdims=True)
    vc = v - mu
    var = jnp.mean(vc * vc, axis=-1, keepdims=True)
    return vc * lax.rsqrt(var + EPS)


def _rms(v):
    return v * lax.rsqrt(jnp.mean(v * v, axis=-1, keepdims=True) + EPS)


def _mod_kernel(c_ref, w_ref, b_ref, o_ref):
    a = _silu(c_ref[...]).astype(BF16)
    o_ref[...] = jnp.dot(a, w_ref[...].astype(BF16), preferred_element_type=F32) + b_ref[...]


def _mod_rows(c_rows, w_mod, b_mod):
    r, d = c_rows.shape
    n = w_mod.shape[1]
    tn = 512
    return pl.pallas_call(
        _mod_kernel,
        grid=(n // tn,),
        in_specs=[pl.BlockSpec((r, d), lambda j: (0, 0)),
                  pl.BlockSpec((d, tn), lambda j: (0, j)),
                  pl.BlockSpec((1, tn), lambda j: (0, j))],
        out_specs=pl.BlockSpec((r, tn), lambda j: (0, j)),
        out_shape=jax.ShapeDtypeStruct((r, n), F32),
        compiler_params=_params("arbitrary"),
        name="mod_rows",
    )(c_rows, w_mod, b_mod.reshape(1, n))


def _residual_rows(ctx_ref, x_ref, ctx_tiles, ro):
    return jnp.where(pl.program_id(1) + ro < ctx_tiles, ctx_ref[0], x_ref[0])


def _residual_specs(tm, d, ctx_tiles, ro):
    return [pl.BlockSpec((1, tm, d), lambda bi, i: (bi, jnp.minimum(i + ro, ctx_tiles - 1), 0)),
            pl.BlockSpec((1, tm, d), lambda bi, i: (bi, jnp.maximum(i + ro - ctx_tiles, 0), 0))]


def _ln_mod_kernel(ctx_ref, x_ref, sh_ref, sc_ref, o_ref, *, ctx_tiles):
    y = _layer_norm(_residual_rows(ctx_ref, x_ref, ctx_tiles, 0))
    o_ref[0] = (y * (1.0 + sc_ref[0]) + sh_ref[0]).astype(o_ref.dtype)


def _mod_row_index(nb, ctx_tiles):
    return lambda b, i: jnp.where(i < ctx_tiles, nb, b)


def _ln_mod(ctx, x, tab, nb):
    b, seq, d = x.shape
    ctx_len = ctx.shape[1]
    tm = _tile(ctx_len, 256)
    ctx_tiles = ctx_len // tm
    row = _mod_row_index(nb, ctx_tiles)
    return pl.pallas_call(
        functools.partial(_ln_mod_kernel, ctx_tiles=ctx_tiles),
        grid=(b, (ctx_len + seq) // tm),
        in_specs=_residual_specs(tm, d, ctx_tiles, 0)
        + [pl.BlockSpec((1, 1, d), lambda bi, i: (row(bi, i), 0, 0)),
           pl.BlockSpec((1, 1, d), lambda bi, i: (row(bi, i), 0, 1))],
        out_specs=pl.BlockSpec((1, tm, d), lambda bi, i: (bi, i, 0)),
        out_shape=jax.ShapeDtypeStruct((b, ctx_len + seq, d), BF16),
        compiler_params=_params("parallel", "parallel"),
        name="ln_mod",
    )(ctx, x, tab, tab)


def _in_proj_kernel(x_ref, w_ref, o_ref, dt_ref, *, dt_tile, dt_col):
    acc = jnp.dot(x_ref[...], w_ref[...], preferred_element_type=F32)
    o_ref[...] = acc.astype(o_ref.dtype)

    @pl.when(pl.program_id(1) == dt_tile)
    def _():
        dt_ref[...] = acc[:, dt_col:dt_col + 128]


def _in_proj(xm2, wp, out_dtype):
    m, k = xm2.shape
    n = wp.shape[1]
    tm = _tile(m, IN_TM)
    tn = P_TN
    dt_tile, dt_col = divmod(P_OFF['kr'], tn)
    assert P_OFF['dtr'] == P_OFF['kr'] + 64 and dt_col % 128 == 0
    return pl.pallas_call(
        functools.partial(_in_proj_kernel, dt_tile=dt_tile, dt_col=dt_col),
        grid=(m // tm, n // tn),
        in_specs=[pl.BlockSpec((tm, k), lambda i, j: (i, 0), pipeline_mode=pl.Buffered(1)),
                  pl.BlockSpec((k, tn), lambda i, j: (0, j))],
        out_specs=[pl.BlockSpec((tm, tn), lambda i, j: (i, j)),
                   pl.BlockSpec((tm, 128), lambda i, j: (i, 0))],
        out_shape=[jax.ShapeDtypeStruct((m, n), out_dtype), jax.ShapeDtypeStruct((m, 128), F32)],
        compiler_params=_params("parallel", "arbitrary"),
        name="in_proj",
    )(xm2, wp)


def _mla_prep_kernel(cq_ref, ckv_ref, kr_ref, qn_ref, kvn_ref, wqm_ref, wqr_ref, wkv_ref,
                     cosq_ref, sinq_ref, ck_ref, s1_ref, s2_ref, qa_ref, ka_ref, va_ref):
    cqn = (_rms(cq_ref[0].astype(F32)) * qn_ref[...]).astype(BF16)
    qm = jnp.dot(cqn, wqm_ref[...], preferred_element_type=F32)
    qr = jnp.dot(cqn, wqr_ref[...], preferred_element_type=F32)
    cosq = cosq_ref[...]
    sinq = sinq_ref[...]
    for h in range(MLA_HEADS):
        sl = slice(h * MLA_QK_PAD, (h + 1) * MLA_QK_PAD)
        qa_ref[0, h] = (qm[:, sl] * cosq + qr[:, sl] * sinq).astype(qa_ref.dtype)
    ckvn = (_rms(ckv_ref[0].astype(F32)) * kvn_ref[...]).astype(BF16)
    kv = jnp.dot(ckvn, wkv_ref[...], preferred_element_type=F32)
    kr = kr_ref[0].astype(F32)
    kpe = (kr * ck_ref[...] + pltpu.roll(kr, 128 - 16, axis=1) * s1_ref[...]
           + pltpu.roll(kr, 16, axis=1) * s2_ref[...]).astype(ka_ref.dtype)
    for h in range(MLA_HEADS):
        base = h * (MLA_NOPE + MLA_V)
        ka_ref[0, h, :, 0:MLA_NOPE] = kv[:, base:base + MLA_NOPE].astype(ka_ref.dtype)
        ka_ref[0, h, :, MLA_NOPE:MLA_QK_PAD] = kpe
        va_ref[0, h] = kv[:, base + MLA_NOPE:base + MLA_NOPE + MLA_V].T.astype(va_ref.dtype)


def _mla_prep(p3, qn, kvn, wqm, wqr, wkv, tabs, ctx_len):
    b, t, _ = p3.shape
    tm = _tile(ctx_len, 256)
    cosq, sinq, ck, s1, s2 = tabs

    def col(name, width):
        blk = P_OFF[name] // width
        return pl.BlockSpec((1, tm, width), lambda bi, i: (bi, i, blk))

    def full(a):
        return pl.BlockSpec(a.shape, lambda bi, i: (0,) * a.ndim)

    def rows(a):
        return pl.BlockSpec((tm, a.shape[1]), lambda bi, i: (i, 0))

    hm = lambda w: pl.BlockSpec((1, MLA_HEADS, tm, w), lambda bi, i: (bi, 0, i, 0))
    return pl.pallas_call(
        _mla_prep_kernel,
        grid=(b, t // tm),
        in_specs=[col('cq', MLA_Q_LORA), col('ckv', MLA_KV_LORA), col('kr', 128),
                  full(qn), full(kvn), full(wqm), full(wqr), full(wkv),
                  rows(cosq), rows(sinq), rows(ck), rows(s1), rows(s2)],
        out_specs=[hm(MLA_QK_PAD), hm(MLA_QK_PAD),
                   pl.BlockSpec((1, MLA_HEADS, MLA_V, tm), lambda bi, i: (bi, 0, 0, i))],
        out_shape=[jax.ShapeDtypeStruct((b, MLA_HEADS, t, MLA_QK_PAD), BF16),
                   jax.ShapeDtypeStruct((b, MLA_HEADS, t, MLA_QK_PAD), BF16),
                   jax.ShapeDtypeStruct((b, MLA_HEADS, MLA_V, t), BF16)],
        compiler_params=_params("parallel", "parallel"),
        name="mla_prep",
    )(p3, p3, p3, qn, kvn, wqm, wqr, wkv, cosq, sinq, ck, s1, s2)


def _rope128(y, c, s1, s2):
    return y * c + pltpu.roll(y, 128 - 32, axis=1) * s1 + pltpu.roll(y, 32, axis=1) * s2


def _gqa_prep_kernel(gq_ref, gk_ref, gv_ref, qn_ref, kn_ref, cq_ref, s1q_ref, s2q_ref,
                     ck_ref, s1k_ref, s2k_ref, qb_ref, kb_ref, vb_ref):
    gq = gq_ref[0].astype(F32)
    for h in range(GQA_HEADS):
        y = _rms(gq[:, h * GQA_DIM:(h + 1) * GQA_DIM]) * qn_ref[...]
        qb_ref[0, h] = _rope128(y, cq_ref[...], s1q_ref[...], s2q_ref[...]).astype(qb_ref.dtype)
    gk = gk_ref[0].astype(F32)
    gv = gv_ref[0].astype(F32)
    for h in range(GQA_KV_HEADS):
        y = _rms(gk[:, h * GQA_DIM:(h + 1) * GQA_DIM]) * kn_ref[...]
        kb_ref[0, h] = _rope128(y, ck_ref[...], s1k_ref[...], s2k_ref[...]).astype(kb_ref.dtype)
        vb_ref[0, h] = gv[:, h * GQA_DIM:(h + 1) * GQA_DIM].T.astype(vb_ref.dtype)


def _gqa_prep(p3, qn, kn, tabs, ctx_len):
    b, t, _ = p3.shape
    tm = _tile(ctx_len, 256)

    def col(name, width):
        blk = P_OFF[name] // width
        return pl.BlockSpec((1, tm, width), lambda bi, i: (bi, i, blk))

    def full(a):
        return pl.BlockSpec(a.shape, lambda bi, i: (0,) * a.ndim)

    def rows(a):
        return pl.BlockSpec((tm, a.shape[1]), lambda bi, i: (i, 0))

    hm = lambda nh: pl.BlockSpec((1, nh, tm, GQA_DIM), lambda bi, i: (bi, 0, i, 0))
    hmt = lambda nh: pl.BlockSpec((1, nh, GQA_DIM, tm), lambda bi, i: (bi, 0, 0, i))
    return pl.pallas_call(
        _gqa_prep_kernel,
        grid=(b, t // tm),
        in_specs=[col('gq', GQA_WIDTH), col('gk', GQA_KV_WIDTH), col('gv', GQA_KV_WIDTH),
                  full(qn), full(kn)] + [rows(a) for a in tabs],
        out_specs=[hm(GQA_HEADS), hm(GQA_KV_HEADS), hmt(GQA_KV_HEADS)],
        out_shape=[jax.ShapeDtypeStruct((b, GQA_HEADS, t, GQA_DIM), BF16),
                   jax.ShapeDtypeStruct((b, GQA_KV_HEADS, t, GQA_DIM), BF16),
                   jax.ShapeDtypeStruct((b, GQA_KV_HEADS, GQA_DIM, t), BF16)],
        compiler_params=_params("parallel", "parallel"),
        name="gqa_prep",
    )(p3, p3, p3, qn, kn, *tabs)


ATTN_KC = 256
ATTN_QB = 128
LOG2E = math.log2(math.e)


def _attn_logits(qts, k_ref, s_buf, nkeys):
    nqb = qts[0].shape[0] // ATTN_QB
    for j, q in enumerate(qts):
        st = lax.dot_general(k_ref[0, j, 0:nkeys, :], q, (((1,), (1,)), ((), ())),
                             preferred_element_type=F32)
        for b in range(nqb):
            s_buf[j * nqb + b, 0:nkeys, :] = st[:, b * ATTN_QB:(b + 1) * ATTN_QB]


def _attn_softmax_slab(s_buf, p_buf, l_buf, b, nkeys):
    parts = [jnp.max(s_buf[b, k0:k0 + ATTN_KC, :].reshape(ATTN_KC // 64, 8, 8, ATTN_QB), axis=0)
             for k0 in range(0, nkeys, ATTN_KC)]
    while len(parts) > 1:
        parts = [jnp.maximum(parts[i], parts[i + 1]) if i + 1 < len(parts) else parts[i]
                 for i in range(0, len(parts), 2)]
    m = jnp.max(parts[0], axis=(0, 1), keepdims=True)[0]
    lacc = jnp.zeros((8, 8, ATTN_QB), F32)
    for k0 in range(0, nkeys, ATTN_KC):
        p = jnp.exp2(s_buf[b, k0:k0 + ATTN_KC, :] - m)
        lacc = lacc + jnp.sum(p.reshape(ATTN_KC // 64, 8, 8, ATTN_QB), axis=0)
        p_buf[b, k0:k0 + ATTN_KC, :] = p.astype(p_buf.dtype)
    l = jnp.sum(lacc, axis=(0, 1), keepdims=True)[0]
    l_buf[b] = jnp.broadcast_to(1.0 / l, (8, ATTN_QB))


def _attn_output(vt_ref, p_buf, l_buf, o_ref, nkeys, hps, group, tq):
    dv = vt_ref.shape[2]
    nqb = group * tq // ATTN_QB
    for j in range(hps):
        slabs = range(j * nqb, (j + 1) * nqb)
        pt = jnp.concatenate([p_buf[b, 0:nkeys, :] for b in slabs], axis=1)
        ot = jnp.dot(vt_ref[0, j, :, 0:nkeys], pt, preferred_element_type=F32)
        ot = ot * jnp.concatenate([l_buf[b, 0:1, :] for b in slabs], axis=1)
        for g in range(group):
            c0 = (j * group + g) * dv
            o_ref[0, :, c0:c0 + dv] = ot[:, g * tq:(g + 1) * tq].T.astype(o_ref.dtype)


def _attn_queries(q_ref, hps, group):
    return [jnp.concatenate([q_ref[0, j * group + g] for g in range(group)], axis=0)
            for j in range(hps)]


def _attn_ctx_kernel(q_ref, k_ref, vt_ref, o_ref, s_sc, p_sc, l_sc, *, group, tq):
    hps, nkeys = k_ref.shape[1], k_ref.shape[2]
    nslab = hps * group * tq // ATTN_QB
    _attn_logits(_attn_queries(q_ref, hps, group), k_ref, s_sc, nkeys)

    def slab(b, carry):
        _attn_softmax_slab(s_sc, p_sc, l_sc, b, nkeys)
        return carry

    lax.fori_loop(0, nslab, slab, 0)
    _attn_output(vt_ref, p_sc, l_sc, o_ref, nkeys, hps, group, tq)


def _attn_lat_kernel(q_ref, k_ref, vt_ref, o_ref, s0, s1, p0, p1, l0, l1, *, group, tq):
    hps, nkeys = k_ref.shape[1], k_ref.shape[2]
    nslab = hps * group * tq // ATTN_QB
    g = pl.program_id(0)

    @pl.when(g == 0)
    def _():
        for buf in (s0, s1, p0, p1, l0, l1):
            buf[...] = jnp.zeros(buf.shape, buf.dtype)

    def step(s_a, s_b, p_b, p_c, l_b, l_c):
        _attn_logits(_attn_queries(q_ref, hps, group), k_ref, s_a, nkeys)
        for b in range(nslab):
            _attn_softmax_slab(s_b, p_b, l_b, b, nkeys)
        _attn_output(vt_ref, p_c, l_c, o_ref, nkeys, hps, group, tq)

    pl.when(g % 2 == 0)(lambda: step(s0, s1, p1, p0, l1, l0))
    pl.when(g % 2 == 1)(lambda: step(s1, s0, p0, p1, l0, l1))


def _attention(q, k, vt, ctx_len, tq, hps, with_ctx_queries, out_dtype):
    b, hq, t, dk = q.shape
    hkv, dv = k.shape[1], vt.shape[2]
    group = hq // hkv
    tq = _tile(ctx_len, tq, 128)
    nslab = hps * group * tq // ATTN_QB
    ctx_tiles = ctx_len // tq
    nq = t // tq - ctx_tiles
    width = hps * group * dv
    assert t % ATTN_KC == 0 and ctx_len % ATTN_KC == 0 and hkv % hps == 0

    def scratch(nkeys):
        return [pltpu.VMEM((nslab, nkeys, ATTN_QB), F32), pltpu.VMEM((nslab, nkeys, ATTN_QB), BF16),
                pltpu.VMEM((nslab, 8, ATTN_QB), F32)]

    s_lat, p_lat, l_lat = scratch(t)
    nh = hkv // hps
    items = b * nh * nq

    def item(step):
        i = jnp.clip(step, 0, items - 1)
        return i // (nh * nq), (i // nq) % nh, i % nq

    def q_map(g):
        bi, h, qi = item(g)
        return bi, h, qi + ctx_tiles, 0

    def k_map(g):
        bi, h, _ = item(g)
        return bi, h, 0, 0

    def v_map(g):
        bi, h, _ = item(g - 2)
        return bi, h, 0, 0

    def o_map(g):
        bi, h, qi = item(g - 2)
        return bi, qi, h

    y_lat = pl.pallas_call(
        functools.partial(_attn_lat_kernel, group=group, tq=tq),
        grid=(items + 2,),
        in_specs=[pl.BlockSpec((1, hps * group, tq, dk), q_map),
                  pl.BlockSpec((1, hps, t, dk), k_map),
                  pl.BlockSpec((1, hps, dv, t), v_map)],
        out_specs=pl.BlockSpec((1, tq, width), o_map),
        out_shape=jax.ShapeDtypeStruct((b, nq * tq, hq * dv), out_dtype),
        scratch_shapes=[s_lat, s_lat, p_lat, p_lat, l_lat, l_lat],
        compiler_params=_params("arbitrary"),
        name="attention_lat_dk%d" % dk,
    )(q, k, vt)
    if not with_ctx_queries:
        return y_lat
    y_ctx = pl.pallas_call(
        functools.partial(_attn_ctx_kernel, group=group, tq=tq),
        grid=(b, hkv // hps, ctx_tiles),
        in_specs=[pl.BlockSpec((1, hps * group, tq, dk), lambda bi, h, i: (bi, h, i, 0)),
                  pl.BlockSpec((1, hps, ctx_len, dk), lambda bi, h, i: (bi, h, 0, 0)),
                  pl.BlockSpec((1, hps, dv, ctx_len), lambda bi, h, i: (bi, h, 0, 0))],
        out_specs=pl.BlockSpec((1, tq, width), lambda bi, h, i: (bi, i, h)),
        out_shape=jax.ShapeDtypeStruct((b, ctx_len, hq * dv), out_dtype),
        scratch_shapes=scratch(ctx_len),
        compiler_params=_params("parallel", "parallel", "arbitrary"),
        name="attention_ctx_dk%d" % dk,
    )(q, k, vt)
    return jnp.concatenate([y_ctx, y_lat], axis=1)


def _conv_kernel(x_ref, w_ref, b_ref, xt_ref, bc_ref, pad_sc, *, ctx_len, rows, x_blocks):
    t = x_ref.shape[1]
    nch = x_ref.shape[2]
    halo = 8
    segs = ((0, ctx_len), (ctx_len, t))
    zeros = jnp.zeros((halo, nch), F32)
    for si, (lo, hi) in enumerate(segs):
        pad_sc[lo + si * halo:lo + (si + 1) * halo, :] = zeros
        for r0 in range(lo, hi, rows):
            pad_sc[r0 + (si + 1) * halo:r0 + (si + 1) * halo + rows, :] = x_ref[0, r0:r0 + rows, :].astype(F32)
    pad_sc[t + 2 * halo:t + 3 * halo, :] = zeros
    w = w_ref[...]
    bias = b_ref[...]

    def emit(transposed):
        for si, (lo, hi) in enumerate(segs):
            for r0 in range(lo, hi, rows):
                base = r0 + (si + 1) * halo - SSD_CONV // 2
                acc = bias + w[0:1, :] * pad_sc[base:base + rows, :]
                for kk in range(1, SSD_CONV):
                    acc = acc + w[kk:kk + 1, :] * pad_sc[base + kk:base + kk + rows, :]
                y = _silu(acc)
                if transposed:
                    xt_ref[0, :, r0:r0 + rows] = y.T.astype(xt_ref.dtype)
                else:
                    bc_ref[0, r0:r0 + rows, :] = y.astype(bc_ref.dtype)

    is_x = pl.program_id(1) < x_blocks
    pl.when(is_x)(lambda: emit(True))
    pl.when(jnp.logical_not(is_x))(lambda: emit(False))


def _conv(p3, conv_w, conv_b, ctx_len, out_dtype):
    b, t, _ = p3.shape
    nch = 256
    rows = _tile(ctx_len, 256)
    blk0 = P_OFF['xbc'] // nch
    x_blocks = SSD_INNER // nch
    kern = functools.partial(_conv_kernel, ctx_len=ctx_len, rows=rows, x_blocks=x_blocks)
    return pl.pallas_call(
        kern,
        grid=(b, SSD_CONV_DIM // nch),
        in_specs=[pl.BlockSpec((1, t, nch), lambda bi, j: (bi, 0, blk0 + j)),
                  pl.BlockSpec((SSD_CONV, nch), lambda bi, j: (0, j)),
                  pl.BlockSpec((1, nch), lambda bi, j: (0, j))],
        out_specs=[pl.BlockSpec((1, nch, t), lambda bi, j: (bi, jnp.minimum(j, x_blocks - 1), 0)),
                   pl.BlockSpec((1, t, nch), lambda bi, j: (bi, 0, jnp.maximum(j - x_blocks, 0)))],
        out_shape=[jax.ShapeDtypeStruct((b, SSD_INNER, t), out_dtype),
                   jax.ShapeDtypeStruct((b, t, SSD_CONV_DIM - SSD_INNER), out_dtype)],
        scratch_shapes=[pltpu.VMEM((t + 24, nch), F32)],
        compiler_params=_params("parallel", "arbitrary"),
        name="ssd_conv",
    )(p3, conv_w, conv_b.reshape(1, SSD_CONV_DIM))


def _split3(v):
    hi = v.astype(BF16)
    r1 = v - hi.astype(F32)
    mid = r1.astype(BF16)
    lo = (r1 - mid.astype(F32)).astype(BF16)
    return hi, mid, lo


def _expand_rows(v):
    q = v.shape[1]
    return jnp.concatenate([jnp.broadcast_to(v[e:e + 1, :], (SSD_P, q)) for e in range(SSD_E)], axis=0)


def _ssd_decays(dt_col_raw, dt_row_raw, bias_col, bias_row, a_col, a_row, backward):
    q = dt_col_raw.shape[0]
    ri = lax.broadcasted_iota(jnp.int32, (q, q), 0)
    ci = lax.broadcasted_iota(jnp.int32, (q, q), 1)
    tri_col = jnp.where((ri <= ci) if backward else (ri >= ci), 1.0, 0.0).astype(BF16)
    tri_row = jnp.where((ri >= ci) if backward else (ri <= ci), 1.0, 0.0).astype(BF16)
    dt_c = _softplus(dt_col_raw + bias_col)
    dt_r = _softplus(dt_row_raw + bias_row)
    cum_c = sum(jnp.dot(tri_col, part, preferred_element_type=F32) for part in _split3(dt_c * a_col))
    cum_r = sum(jnp.dot(part, tri_row, preferred_element_type=F32) for part in _split3(dt_r * a_row))
    total = jnp.broadcast_to(cum_r[:, 0:1] if backward else cum_r[:, q - 1:q], (SSD_E, q))
    return dt_r, cum_c, cum_r, total


def _ssd_direction(xt, bm, cm, decays, h_ref, backward):
    q = xt.shape[1]
    dt_r, cum_c, cum_r, total = decays
    ri = lax.broadcasted_iota(jnp.int32, (q, q), 0)
    ci = lax.broadcasted_iota(jnp.int32, (q, q), 1)
    keep_t = (ci <= ri) if backward else (ci >= ri)

    bmb = bm.astype(BF16)
    cmb = cm.astype(BF16)
    nt = (((1,), (1,)), ((), ()))
    cbt = lax.dot_general(bmb, cmb, nt, preferred_element_type=F32)
    h = h_ref[...]
    y_off = lax.dot_general(h.astype(BF16), cmb, nt, preferred_element_type=F32) * _expand_rows(jnp.exp(cum_r))
    wgt = (xt * _expand_rows(dt_r * jnp.exp(total - cum_r))).astype(BF16)
    h_ref[...] = _expand_rows(jnp.exp(total)) * h + jnp.dot(wgt, bmb, preferred_element_type=F32)

    xdt = (xt * _expand_rows(dt_r)).astype(BF16)
    parts = []
    for e in range(SSD_E):
        seg = cum_r[e:e + 1, :] - cum_c[:, e:e + 1]
        dec = jnp.exp(jnp.where(keep_t, seg, -jnp.inf))
        parts.append(jnp.dot(xdt[e * SSD_P:(e + 1) * SSD_P, :], (cbt * dec).astype(BF16),
                             preferred_element_type=F32))
    return jnp.concatenate(parts, axis=0) + y_off


def _ssd_kernel(xf_ref, bf_ref, cf_ref, xb_ref, bb_ref, cb_ref, dcf_ref, drf_ref, dcb_ref, drb_ref,
                bias_c_ref, bias_r_ref, a_c_ref, a_r_ref, d_ref, yf_ref, yb_ref, hf_sc, hb_sc):
    @pl.when(pl.program_id(2) == 0)
    def _():
        hf_sc[...] = jnp.zeros(hf_sc.shape, F32)
        hb_sc[...] = jnp.zeros(hb_sc.shape, F32)

    nsub = xf_ref.shape[2] // SSD_Q
    gps = dcf_ref.shape[2]
    chunks = [slice(i * SSD_Q, (i + 1) * SSD_Q) for i in range(nsub)]
    dec_f = [[_ssd_decays(dcf_ref[0, 0, gi, rs, :], drf_ref[0, 0, gi, :, rs], bias_c_ref[0, gi],
                          bias_r_ref[0, gi], a_c_ref[0, gi], a_r_ref[0, gi], False) for rs in chunks]
             for gi in range(gps)]
    dec_b = [[_ssd_decays(dcb_ref[0, 0, gi, rs, :], drb_ref[0, 0, gi, :, rs], bias_c_ref[1, gi],
                          bias_r_ref[1, gi], a_c_ref[1, gi], a_r_ref[1, gi], True) for rs in chunks]
             for gi in range(gps)]
    for gi in range(gps):
        ch = slice(gi * SSD_GW, (gi + 1) * SSD_GW)
        st = slice(gi * SSD_N, (gi + 1) * SSD_N)
        for i in range(nsub):
            rf, rb = chunks[i], chunks[nsub - 1 - i]
            xt = xf_ref[0, ch, rf].astype(F32)
            yf = _ssd_direction(xt, bf_ref[0, rf, st].astype(F32), cf_ref[0, rf, st].astype(F32),
                                dec_f[gi][i], hf_sc.at[ch], False)
            yf_ref[0, ch, rf] = (yf + d_ref[gi] * xt).astype(yf_ref.dtype)
            yb = _ssd_direction(xb_ref[0, ch, rb].astype(F32), bb_ref[0, rb, st].astype(F32),
                                cb_ref[0, rb, st].astype(F32), dec_b[gi][nsub - 1 - i], hb_sc.at[ch], True)
            yb_ref[0, ch, rb] = yb.astype(yb_ref.dtype)


def _ssd(xt, bc, dt_col, dt_row, bias_c, bias_r, a_c, a_r, d_rows, ctx_len, out_dtype):
    b, _, t = xt.shape
    assert SSD_Q == SSD_N
    rows = _tile(ctx_len, SSD_ROWS, SSD_Q)
    nblk = t // rows
    nctx = ctx_len // rows

    def bidx(c):
        return jnp.where(c < nctx, nctx - 1 - c, nblk - 1 - (c - nctx))

    gps = SSD_GPS
    ngs = SSD_G // gps
    fx = lambda bi, g, c: (bi, g, c)
    fb = lambda bi, g, c: (bi, c, g)
    fc = lambda bi, g, c: (bi, c, ngs + g)
    bx = lambda bi, g, c: (bi, g, bidx(c))
    bb = lambda bi, g, c: (bi, bidx(c), g)
    bcm = lambda bi, g, c: (bi, bidx(c), ngs + g)
    small = lambda a: pl.BlockSpec((2, gps) + a.shape[2:], lambda bi, g, c: (0, g, 0, 0))
    return pl.pallas_call(
        _ssd_kernel,
        grid=(b, ngs, nblk),
        in_specs=[pl.BlockSpec((1, gps * SSD_GW, rows), fx), pl.BlockSpec((1, rows, gps * SSD_N), fb),
                  pl.BlockSpec((1, rows, gps * SSD_N), fc),
                  pl.BlockSpec((1, gps * SSD_GW, rows), bx), pl.BlockSpec((1, rows, gps * SSD_N), bb),
                  pl.BlockSpec((1, rows, gps * SSD_N), bcm),
                  pl.BlockSpec((1, 1, gps, rows, SSD_E), lambda bi, g, c: (bi, 0, g, c, 0)),
                  pl.BlockSpec((1, 1, gps, SSD_E, rows), lambda bi, g, c: (bi, 0, g, 0, c)),
                  pl.BlockSpec((1, 1, gps, rows, SSD_E), lambda bi, g, c: (bi, 1, g, bidx(c), 0)),
                  pl.BlockSpec((1, 1, gps, SSD_E, rows), lambda bi, g, c: (bi, 1, g, 0, bidx(c))),
                  small(bias_c), small(bias_r), small(a_c), small(a_r),
                  pl.BlockSpec((gps, SSD_GW, SSD_Q), lambda bi, g, c: (g, 0, 0))],
        out_specs=[pl.BlockSpec((1, gps * SSD_GW, rows), fx), pl.BlockSpec((1, gps * SSD_GW, rows), bx)],
        out_shape=[jax.ShapeDtypeStruct((b, SSD_INNER, t), out_dtype)] * 2,
        scratch_shapes=[pltpu.VMEM((gps * SSD_GW, SSD_N), F32), pltpu.VMEM((gps * SSD_GW, SSD_N), F32)],
        compiler_params=_params("parallel", "parallel", "arbitrary"),
        name="ssd_scan",
    )(xt, bc, bc, xt, bc, bc, dt_col, dt_row, dt_col, dt_row, bias_c, bias_r, a_c, a_r, d_rows)


MERGE_TN = 2048


def _merge_kernel(ya_ref, ga_ref, yb_ref, gb_ref, yf_ref, ybk_ref, z_ref, nrm_ref,
                  mg_ref, wa_ref, wb_ref, wc_ref, u_ref, c_sc):
    a_in = (ya_ref[0].astype(F32) * _silu(ga_ref[0].astype(F32))).astype(BF16)
    b_in = (yb_ref[0].astype(F32) * _silu(gb_ref[0].astype(F32))).astype(BF16)
    v = (yf_ref[0].astype(F32) + ybk_ref[0].astype(F32)).T * _silu(z_ref[0].astype(F32))
    for g in range(SSD_G):
        sl = slice(g * SSD_GW, (g + 1) * SSD_GW)
        c_sc[:, sl] = (_rms(v[:, sl]) * nrm_ref[:, sl]).astype(BF16)
    c_in = c_sc[...]
    d = u_ref.shape[-1]
    for j in range(0, d, MERGE_TN):
        sl = slice(j, j + MERGE_TN)
        br_a = jnp.dot(a_in, wa_ref[:, sl], preferred_element_type=F32)
        br_b = jnp.dot(b_in, wb_ref[:, sl], preferred_element_type=F32)
        br_c = jnp.dot(c_in, wc_ref[:, sl], preferred_element_type=F32)
        u = (jax.nn.sigmoid(mg_ref[0, :, j:j + MERGE_TN].astype(F32)) * br_a
             + jax.nn.sigmoid(mg_ref[0, :, d + j:d + j + MERGE_TN].astype(F32)) * br_b
             + jax.nn.sigmoid(mg_ref[0, :, 2 * d + j:2 * d + j + MERGE_TN].astype(F32)) * br_c)
        u_ref[0, :, sl] = u.astype(u_ref.dtype)


def _merge(ya, yb, yf, ybk, p3, ssd_norm, wa, wb, wc, row_off):
    b, rows, _ = ya.shape
    tm = _tile(rows, 256)
    d = D_MODEL
    ro = row_off // tm
    assert P_OFF['mg'] == 0

    def col(name, width):
        blk = P_OFF[name] // width
        return pl.BlockSpec((1, tm, width), lambda bi, i: (bi, i + ro, blk))

    loc = lambda w: pl.BlockSpec((1, tm, w), lambda bi, i: (bi, i, 0))
    glob = lambda w: pl.BlockSpec((1, tm, w), lambda bi, i: (bi, i + ro, 0))
    globt = pl.BlockSpec((1, SSD_INNER, tm), lambda bi, i: (bi, 0, i + ro))
    wspec = lambda k: pl.BlockSpec((k, d), lambda bi, i: (0, 0), pipeline_mode=pl.Buffered(1))
    return pl.pallas_call(
        _merge_kernel,
        grid=(b, rows // tm),
        in_specs=[loc(MLA_WIDTH), col('ga', MLA_WIDTH), loc(GQA_WIDTH), col('gb', GQA_WIDTH),
                  globt, globt, col('z', SSD_INNER),
                  pl.BlockSpec((1, SSD_INNER), lambda bi, i: (0, 0)),
                  glob(N_BRANCH * d), wspec(MLA_WIDTH), wspec(GQA_WIDTH), wspec(SSD_INNER)],
        out_specs=pl.BlockSpec((1, tm, d), lambda bi, i: (bi, i, 0)),
        out_shape=jax.ShapeDtypeStruct((b, rows, d), BF16),
        scratch_shapes=[pltpu.VMEM((tm, SSD_INNER), BF16)],
        compiler_params=_params("parallel", "parallel"),
        name="merge",
    )(ya, p3, yb, p3, yf, ybk, p3, ssd_norm.reshape(1, SSD_INNER), p3, wa, wb, wc)


def _out_ln_kernel(u_ref, w_ref, *rest, with_next, ctx_tiles, ro, split):
    if split:
        ctx_ref, x_ref, gate_ref, g_ref, b_ref, *rest = rest
        res = _residual_rows(ctx_ref, x_ref, ctx_tiles, ro)
    else:
        x_ref, gate_ref, g_ref, b_ref, *rest = rest
        res = x_ref[0]
    out = jnp.dot(u_ref[0], w_ref[...], preferred_element_type=F32)
    r = DEEPNORM_ALPHA * res + gate_ref[0] * out
    xn = _layer_norm(r) * g_ref[...] + b_ref[...]
    if with_next:
        sh_ref, sc_ref, xo_ref, xm_ref = rest
        xo_ref[0] = xn
        xm_ref[0] = (_layer_norm(xn) * (1.0 + sc_ref[0]) + sh_ref[0]).astype(xm_ref.dtype)
    else:
        (xo_ref,) = rest
        xo_ref[0] = xn


def _out_ln(u, w_out, xc, tab, ln_g, ln_b, next_tab, nb, ctx_len, row_off):
    b, rows, d = u.shape
    tm = _tile(ctx_len, 256)
    ro = row_off // tm
    ctx_tiles = ctx_len // tm
    row = _mod_row_index(nb, ctx_tiles)
    with_next = next_tab is not None
    split = isinstance(xc, tuple)
    loc = pl.BlockSpec((1, tm, d), lambda bi, i: (bi, i, 0))
    vec = pl.BlockSpec((1, d), lambda bi, i: (0, 0))
    res_specs = (_residual_specs(tm, d, ctx_tiles, ro) if split
                 else [pl.BlockSpec((1, tm, d), lambda bi, i: (bi, i + ro, 0))])
    in_specs = [loc, pl.BlockSpec((d, d), lambda bi, i: (0, 0))] + res_specs + [
        pl.BlockSpec((1, 1, d), lambda bi, i: (row(bi, i + ro), 0, 2)), vec, vec]
    args = [u, w_out] + (list(xc) if split else [xc]) + [tab, ln_g.reshape(1, d), ln_b.reshape(1, d)]
    out_specs = [loc]
    out_shape = [jax.ShapeDtypeStruct((b, rows, d), F32)]
    if with_next:
        in_specs += [pl.BlockSpec((1, 1, d), lambda bi, i: (row(bi, i + ro), 0, 0)),
                     pl.BlockSpec((1, 1, d), lambda bi, i: (row(bi, i + ro), 0, 1))]
        args += [next_tab, next_tab]
        out_specs.append(loc)
        out_shape.append(jax.ShapeDtypeStruct((b, rows, d), BF16))
    return pl.pallas_call(
        functools.partial(_out_ln_kernel, with_next=with_next, ctx_tiles=ctx_tiles, ro=ro, split=split),
        grid=(b, rows // tm),
        in_specs=in_specs,
        out_specs=out_specs,
        out_shape=out_shape,
        compiler_params=_params("parallel", "parallel"),
        name="out_ln",
    )(*args)


def _rope_angles(rows, dim):
    row, col = jnp.meshgrid(jnp.arange(rows, dtype=F32), jnp.arange(GRID_W, dtype=F32), indexing='ij')
    half = dim // 2
    inv_freq = ROPE_THETA ** (-jnp.arange(0, half, 2, dtype=F32) / half)
    ang_r = row.reshape(-1, 1) * inv_freq
    ang_c = col.reshape(-1, 1) * inv_freq
    return jnp.concatenate([ang_r, ang_r, ang_c, ang_c], axis=-1)


def _rope_tables(seq, ctx_len, dim):
    ang = _rope_angles(seq // GRID_W, dim)
    cos = jnp.concatenate([jnp.ones((ctx_len, dim), F32), jnp.cos(ang)], axis=0)
    sin = jnp.concatenate([jnp.zeros((ctx_len, dim), F32), jnp.sin(ang)], axis=0)
    return cos, sin


def _roll_tables(cos, sin, dim, scale):
    t = cos.shape[0]
    quarter = dim // 4
    first = (jnp.arange(dim) % (2 * quarter)) < quarter
    s1 = jnp.where(first, -sin, 0.0)
    s2 = jnp.where(first, 0.0, sin)
    pad = lambda a: jnp.pad(a * scale, ((0, 0), (0, 128 - dim)))
    return pad(cos), pad(s1), pad(s2)


def _rot_matrix(dim):
    quarter = dim // 4
    r = np.zeros((dim, dim), np.float32)
    for i in range(dim):
        blk = i // quarter
        if blk % 2 == 0:
            r[i + quarter, i] = -1.0
        else:
            r[i - quarter, i] = 1.0
    return jnp.asarray(r)


def _mla_q_weights(w_uq):
    w = w_uq.reshape(MLA_Q_LORA, MLA_HEADS, MLA_QK)
    nope, pe = w[..., :MLA_NOPE], w[..., MLA_NOPE:]
    pe_rot = jnp.einsum('khd,de->khe', pe, _rot_matrix(MLA_ROPE), precision=HIGHEST)
    zpad = jnp.zeros((MLA_Q_LORA, MLA_HEADS, MLA_QK_PAD - MLA_QK), F32)
    main = jnp.concatenate([nope, pe, zpad], axis=-1)
    rot = jnp.concatenate([jnp.zeros_like(nope), pe_rot, zpad], axis=-1)
    return (main.reshape(MLA_Q_LORA, -1).astype(BF16), rot.reshape(MLA_Q_LORA, -1).astype(BF16))


def _permute_w_in(w):
    parts = [w[:, IN_OFFSETS[n]:IN_OFFSETS[n] + IN_WIDTHS[n]] for n in P_ORDER]
    parts.append(jnp.zeros((w.shape[0], P_WIDTH - P_USED), w.dtype))
    return jnp.concatenate(parts, axis=1).astype(BF16)


P_DTYPE = BF16
Y_DTYPE = BF16


def kernel(x, c, ctx, c_ctx, w_mod, b_mod, w_in, mla_q_norm, mla_w_uq, mla_kv_norm, mla_w_ukv,
           gqa_q_norm, gqa_k_norm, ssd_conv_w, ssd_conv_b, ssd_a_log, ssd_dt_bias, ssd_d, ssd_norm,
           w_br_a, w_br_b, w_br_c, w_out, ln_g, ln_b):
    nb, seq, d = x.shape
    ctx_len = ctx.shape[1]
    t = ctx_len + seq
    depth = w_in.shape[0]
    assert d == D_MODEL and nb < 8 and seq % GRID_W == 0
    assert ctx_len % SSD_Q == 0 and seq % SSD_Q == 0

    cos_a, sin_a = _rope_tables(seq, ctx_len, MLA_ROPE)
    cos_b, sin_b = _rope_tables(seq, ctx_len, GQA_DIM)
    sq = MLA_QK ** -0.5 * LOG2E
    zq = jnp.zeros((t, MLA_QK_PAD - MLA_QK), F32)
    cosq = jnp.concatenate([jnp.full((t, MLA_NOPE), sq, F32), cos_a * sq, zq], axis=1)
    sinq = jnp.concatenate([jnp.zeros((t, MLA_NOPE), F32), sin_a * sq, zq], axis=1)
    mla_tabs = (cosq, sinq) + _roll_tables(cos_a, sin_a, MLA_ROPE, 1.0)
    gqa_tabs = (_roll_tables(cos_b, sin_b, GQA_DIM, GQA_DIM ** -0.5 * LOG2E)
                + _roll_tables(cos_b, sin_b, GQA_DIM, 1.0))

    c_rows = jnp.zeros((8, d), F32).at[:nb].set(c).at[nb].set(c_ctx)
    tabs = [_mod_rows(c_rows, w_mod[l], b_mod[l]).reshape(8, 1, 3 * d) for l in range(depth)]

    xc = (ctx, x)
    xm = _ln_mod(ctx, x, tabs[0], nb)

    for l in range(depth):
        last = l == depth - 1
        wp = _permute_w_in(w_in[l])
        p2, krdt = _in_proj(xm.reshape(nb * t, d), wp, P_DTYPE)
        p3 = p2.reshape(nb, t, P_WIDTH)

        wqm, wqr = _mla_q_weights(mla_w_uq[l])
        qa, ka, va = _mla_prep(p3, mla_q_norm[l].reshape(1, -1), mla_kv_norm[l].reshape(1, -1),
                               wqm, wqr, mla_w_ukv[l].astype(BF16), mla_tabs, ctx_len)
        ya = _attention(qa, ka, va, ctx_len, 256, 2, not last, Y_DTYPE)
        qb, kb, vb = _gqa_prep(p3, gqa_q_norm[l].reshape(1, -1), gqa_k_norm[l].reshape(1, -1),
                               gqa_tabs, ctx_len)
        yb = _attention(qb, kb, vb, ctx_len, 128, 1, not last, Y_DTYPE)

        xconv_t, bconv = _conv(p3, ssd_conv_w[l], ssd_conv_b[l], ctx_len, Y_DTYPE)
        dt5 = krdt[:, 64:].reshape(nb, t, 2, SSD_G, SSD_E)
        dt_col = jnp.transpose(dt5, (0, 2, 3, 1, 4))
        dt_row = jnp.transpose(dt5, (0, 2, 3, 4, 1))
        bias = ssd_dt_bias[l].astype(F32).reshape(2, SSD_G, SSD_E)
        a = -jnp.exp(ssd_a_log[l].astype(F32)).reshape(2, SSD_G, SSD_E)
        d_rows = jnp.broadcast_to(jnp.repeat(ssd_d[l].astype(F32), SSD_P).reshape(SSD_G, SSD_GW, 1),
                                  (SSD_G, SSD_GW, SSD_Q))
        yf, ybk = _ssd(xconv_t, bconv, dt_col, dt_row, bias[:, :, None, :], bias[:, :, :, None],
                       a[:, :, None, :], a[:, :, :, None], d_rows, ctx_len, Y_DTYPE)

        row_off = ctx_len if last else 0
        u = _merge(ya, yb, yf, ybk, p3, ssd_norm[l], w_br_a[l].astype(BF16), w_br_b[l].astype(BF16),
                   w_br_c[l].astype(BF16), row_off)
        if last:
            (xo,) = _out_ln(u, w_out[l].astype(BF16), xc, tabs[l], ln_g[l], ln_b[l], None, nb,
                            ctx_len, row_off)
            return xo
        xc, xm = _out_ln(u, w_out[l].astype(BF16), xc, tabs[l], ln_g[l], ln_b[l], tabs[l + 1], nb,
                         ctx_len, row_off)
```

```python
import functools
import math

import numpy as np
import jax
import jax.numpy as jnp
from jax import lax
from jax.experimental import pallas as pl
from jax.experimental.pallas import tpu as pltpu

F32 = jnp.float32
BF16 = jnp.bfloat16
HIGHEST = lax.Precision.HIGHEST

D_MODEL = 2048
DEPTH = 2
GRID_W = 64
ROPE_THETA = 10000.0
EPS = 1e-6

MLA_HEADS = 8
MLA_Q_LORA = 512
MLA_KV_LORA = 256
MLA_NOPE = 128
MLA_ROPE = 64
MLA_V = 128
MLA_QK = MLA_NOPE + MLA_ROPE
MLA_QK_PAD = 256
MLA_WIDTH = MLA_HEADS * MLA_V

GQA_HEADS = 8
GQA_KV_HEADS = 2
GQA_GROUP = GQA_HEADS // GQA_KV_HEADS
GQA_DIM = 128
GQA_WIDTH = GQA_HEADS * GQA_DIM
GQA_KV_WIDTH = GQA_KV_HEADS * GQA_DIM

SSD_INNER = D_MODEL
SSD_P = 64
SSD_HEADS = SSD_INNER // SSD_P
SSD_G = 4
SSD_E = SSD_HEADS // SSD_G
SSD_N = 128
SSD_CONV = 5
SSD_Q = 128
SSD_ROWS = 256
SSD_GPS = 4
SSD_GW = SSD_E * SSD_P
SSD_CONV_DIM = SSD_INNER + 2 * SSD_G * SSD_N

N_BRANCH = 3
IN_SPLITS = (MLA_Q_LORA, MLA_KV_LORA, MLA_ROPE, MLA_WIDTH, GQA_WIDTH, GQA_KV_WIDTH, GQA_KV_WIDTH,
             GQA_WIDTH, SSD_INNER, SSD_CONV_DIM, 2 * SSD_HEADS, N_BRANCH * D_MODEL)
IN_NAMES = ('cq', 'ckv', 'kr', 'ga', 'gq', 'gk', 'gv', 'gb', 'z', 'xbc', 'dtr', 'mg')
IN_OFFSETS = dict(zip(IN_NAMES, np.concatenate([[0], np.cumsum(IN_SPLITS)[:-1]]).tolist()))
IN_WIDTHS = dict(zip(IN_NAMES, IN_SPLITS))
P_ORDER = ('mg', 'z', 'xbc', 'ga', 'gq', 'gb', 'cq', 'ckv', 'gk', 'gv', 'kr', 'dtr')
P_OFF = {}
_o = 0
for _n in P_ORDER:
    P_OFF[_n] = _o
    _o += IN_WIDTHS[_n]
P_USED = _o
P_TN = 512
IN_TM = 4352
P_WIDTH = -(-P_USED // P_TN) * P_TN

DEEPNORM_ALPHA = (2 * DEPTH) ** 0.25

VMEM_LIMIT = 56 * 2 ** 20


def _params(*sem):
    return pltpu.CompilerParams(dimension_semantics=sem, vmem_limit_bytes=VMEM_LIMIT)


def _tile(n, target, align=8):
    t = min(n, target)
    while t > align and (n % t or t % align):
        t -= align
    assert n % t == 0, (n, target)
    return t


def _silu(v):
    return v * jax.nn.sigmoid(v)


def _softplus(v):
    return jnp.maximum(v, 0.0) + jnp.log1p(jnp.exp(-jnp.abs(v)))


def _layer_norm(v):
    mu = jnp.mean(v, axis=-1, keepdims=True)
    vc = v - mu
    var = jnp.mean(vc * vc, axis=-1, keepdims=True)
    return vc * lax.rsqrt(var + EPS)


def _rms(v):
    return v * lax.rsqrt(jnp.mean(v * v, axis=-1, keepdims=True) + EPS)


def _mod_kernel(c_ref, w_ref, b_ref, o_ref):
    a = _silu(c_ref[...]).astype(BF16)
    o_ref[...] = jnp.dot(a, w_ref[...].astype(BF16), preferred_element_type=F32) + b_ref[...]


def _mod_rows(c_rows, w_mod, b_mod):
    r, d = c_rows.shape
    n = w_mod.shape[1]
    tn = 512
    return pl.pallas_call(
        _mod_kernel,
        grid=(n // tn,),
        in_specs=[pl.BlockSpec((r, d), lambda j: (0, 0)),
                  pl.BlockSpec((d, tn), lambda j: (0, j)),
                  pl.BlockSpec((1, tn), lambda j: (0, j))],
        out_specs=pl.BlockSpec((r, tn), lambda j: (0, j)),
        out_shape=jax.ShapeDtypeStruct((r, n), F32),
        compiler_params=_params("arbitrary"),
        name="mod_rows",
    )(c_rows, w_mod, b_mod.reshape(1, n))


def _residual_rows(ctx_ref, x_ref, ctx_tiles, ro):
    return jnp.where(pl.program_id(1) + ro < ctx_tiles, ctx_ref[0], x_ref[0])


def _residual_specs(tm, d, ctx_tiles, ro):
    return [pl.BlockSpec((1, tm, d), lambda bi, i: (bi, jnp.minimum(i + ro, ctx_tiles - 1), 0)),
            pl.BlockSpec((1, tm, d), lambda bi, i: (bi, jnp.maximum(i + ro - ctx_tiles, 0), 0))]


def _ln_mod_kernel(ctx_ref, x_ref, sh_ref, sc_ref, o_ref, *, ctx_tiles):
    y = _layer_norm(_residual_rows(ctx_ref, x_ref, ctx_tiles, 0))
    o_ref[0] = (y * (1.0 + sc_ref[0]) + sh_ref[0]).astype(o_ref.dtype)


def _mod_row_index(nb, ctx_tiles):
    return lambda b, i: jnp.where(i < ctx_tiles, nb, b)


def _ln_mod(ctx, x, tab, nb):
    b, seq, d = x.shape
    ctx_len = ctx.shape[1]
    tm = _tile(ctx_len, 256)
    ctx_tiles = ctx_len // tm
    row = _mod_row_index(nb, ctx_tiles)
    return pl.pallas_call(
        functools.partial(_ln_mod_kernel, ctx_tiles=ctx_tiles),
        grid=(b, (ctx_len + seq) // tm),
        in_specs=_residual_specs(tm, d, ctx_tiles, 0)
        + [pl.BlockSpec((1, 1, d), lambda bi, i: (row(bi, i), 0, 0)),
           pl.BlockSpec((1, 1, d), lambda bi, i: (row(bi, i), 0, 1))],
        out_specs=pl.BlockSpec((1, tm, d), lambda bi, i: (bi, i, 0)),
        out_shape=jax.ShapeDtypeStruct((b, ctx_len + seq, d), BF16),
        compiler_params=_params("parallel", "parallel"),
        name="ln_mod",
    )(ctx, x, tab, tab)


def _in_proj_kernel(x_ref, w_ref, o_ref, dt_ref, *, dt_tile, dt_col):
    acc = jnp.dot(x_ref[...], w_ref[...], preferred_element_type=F32)
    o_ref[...] = acc.astype(o_ref.dtype)

    @pl.when(pl.program_id(1) == dt_tile)
    def _():
        dt_ref[...] = acc[:, dt_col:dt_col + 128]


def _in_proj(xm2, wp, out_dtype):
    m, k = xm2.shape
    n = wp.shape[1]
    tm = _tile(m, IN_TM)
    tn = P_TN
    dt_tile, dt_col = divmod(P_OFF['kr'], tn)
    assert P_OFF['dtr'] == P_OFF['kr'] + 64 and dt_col % 128 == 0
    return pl.pallas_call(
        functools.partial(_in_proj_kernel, dt_tile=dt_tile, dt_col=dt_col),
        grid=(m // tm, n // tn),
        in_specs=[pl.BlockSpec((tm, k), lambda i, j: (i, 0), pipeline_mode=pl.Buffered(1)),
                  pl.BlockSpec((k, tn), lambda i, j: (0, j))],
        out_specs=[pl.BlockSpec((tm, tn), lambda i, j: (i, j)),
                   pl.BlockSpec((tm, 128), lambda i, j: (i, 0))],
        out_shape=[jax.ShapeDtypeStruct((m, n), out_dtype), jax.ShapeDtypeStruct((m, 128), F32)],
        compiler_params=_params("parallel", "arbitrary"),
        name="in_proj",
    )(xm2, wp)


def _mla_prep_kernel(cq_ref, ckv_ref, kr_ref, qn_ref, kvn_ref, wqm_ref, wqr_ref, wkv_ref,
                     cosq_ref, sinq_ref, ck_ref, s1_ref, s2_ref, qa_ref, ka_ref, va_ref):
    cqn = (_rms(cq_ref[0].astype(F32)) * qn_ref[...]).astype(BF16)
    qm = jnp.dot(cqn, wqm_ref[...], preferred_element_type=F32)
    qr = jnp.dot(cqn, wqr_ref[...], preferred_element_type=F32)
    cosq = cosq_ref[...]
    sinq = sinq_ref[...]
    for h in range(MLA_HEADS):
        sl = slice(h * MLA_QK_PAD, (h + 1) * MLA_QK_PAD)
        qa_ref[0, h] = (qm[:, sl] * cosq + qr[:, sl] * sinq).astype(qa_ref.dtype)
    ckvn = (_rms(ckv_ref[0].astype(F32)) * kvn_ref[...]).astype(BF16)
    kv = jnp.dot(ckvn, wkv_ref[...], preferred_element_type=F32)
    kr = kr_ref[0].astype(F32)
    kpe = (kr * ck_ref[...] + pltpu.roll(kr, 128 - 16, axis=1) * s1_ref[...]
           + pltpu.roll(kr, 16, axis=1) * s2_ref[...]).astype(ka_ref.dtype)
    for h in range(MLA_HEADS):
        base = h * (MLA_NOPE + MLA_V)
        ka_ref[0, h, :, 0:MLA_NOPE] = kv[:, base:base + MLA_NOPE].astype(ka_ref.dtype)
        ka_ref[0, h, :, MLA_NOPE:MLA_QK_PAD] = kpe
        va_ref[0, h] = kv[:, base + MLA_NOPE:base + MLA_NOPE + MLA_V].T.astype(va_ref.dtype)


def _rope128(y, c, s1, s2):
    return y * c + pltpu.roll(y, 128 - 32, axis=1) * s1 + pltpu.roll(y, 32, axis=1) * s2


def _gqa_prep_kernel(gq_ref, gk_ref, gv_ref, qn_ref, kn_ref, cq_ref, s1q_ref, s2q_ref,
                     ck_ref, s1k_ref, s2k_ref, qb_ref, kb_ref, vb_ref):
    gq = gq_ref[0].astype(F32)
    for h in range(GQA_HEADS):
        y = _rms(gq[:, h * GQA_DIM:(h + 1) * GQA_DIM]) * qn_ref[...]
        qb_ref[0, h] = _rope128(y, cq_ref[...], s1q_ref[...], s2q_ref[...]).astype(qb_ref.dtype)
    gk = gk_ref[0].astype(F32)
    gv = gv_ref[0].astype(F32)
    for h in range(GQA_KV_HEADS):
        y = _rms(gk[:, h * GQA_DIM:(h + 1) * GQA_DIM]) * kn_ref[...]
        kb_ref[0, h] = _rope128(y, ck_ref[...], s1k_ref[...], s2k_ref[...]).astype(kb_ref.dtype)
        vb_ref[0, h] = gv[:, h * GQA_DIM:(h + 1) * GQA_DIM].T.astype(vb_ref.dtype)


N_MLA_IN, N_GQA_IN = 13, 11


def _qkv_prep_kernel(*refs):
    mla_in, gqa_in = refs[:N_MLA_IN], refs[N_MLA_IN:N_MLA_IN + N_GQA_IN]
    outs = refs[N_MLA_IN + N_GQA_IN:]
    _mla_prep_kernel(*mla_in, *outs[:3])
    _gqa_prep_kernel(*gqa_in, *outs[3:])


def _qkv_prep(p3, mla_qn, mla_kvn, wqm, wqr, wkv, mla_tabs, gqa_qn, gqa_kn, gqa_tabs, ctx_len):
    b, t, _ = p3.shape
    tm = _tile(ctx_len, 256)

    def col(name, width):
        blk = P_OFF[name] // width
        return pl.BlockSpec((1, tm, width), lambda bi, i: (bi, i, blk))

    def full(a):
        return pl.BlockSpec(a.shape, lambda bi, i: (0,) * a.ndim)

    def rows(a):
        return pl.BlockSpec((tm, a.shape[1]), lambda bi, i: (i, 0))

    hm = lambda nh, w: pl.BlockSpec((1, nh, tm, w), lambda bi, i: (bi, 0, i, 0))
    hmt = lambda nh, w: pl.BlockSpec((1, nh, w, tm), lambda bi, i: (bi, 0, 0, i))
    in_specs = ([col('cq', MLA_Q_LORA), col('ckv', MLA_KV_LORA), col('kr', 128),
                 full(mla_qn), full(mla_kvn), full(wqm), full(wqr), full(wkv)]
                + [rows(a) for a in mla_tabs]
                + [col('gq', GQA_WIDTH), col('gk', GQA_KV_WIDTH), col('gv', GQA_KV_WIDTH),
                   full(gqa_qn), full(gqa_kn)] + [rows(a) for a in gqa_tabs])
    args = (p3, p3, p3, mla_qn, mla_kvn, wqm, wqr, wkv, *mla_tabs, p3, p3, p3, gqa_qn, gqa_kn, *gqa_tabs)
    assert len(mla_tabs) + 8 == N_MLA_IN and len(gqa_tabs) + 5 == N_GQA_IN
    outs = pl.pallas_call(
        _qkv_prep_kernel,
        grid=(b, t // tm),
        in_specs=in_specs,
        out_specs=[hm(MLA_HEADS, MLA_QK_PAD), hm(MLA_HEADS, MLA_QK_PAD), hmt(MLA_HEADS, MLA_V),
                   hm(GQA_HEADS, GQA_DIM), hm(GQA_KV_HEADS, GQA_DIM), hmt(GQA_KV_HEADS, GQA_DIM)],
        out_shape=[jax.ShapeDtypeStruct((b, MLA_HEADS, t, MLA_QK_PAD), BF16),
                   jax.ShapeDtypeStruct((b, MLA_HEADS, t, MLA_QK_PAD), BF16),
                   jax.ShapeDtypeStruct((b, MLA_HEADS, MLA_V, t), BF16),
                   jax.ShapeDtypeStruct((b, GQA_HEADS, t, GQA_DIM), BF16),
                   jax.ShapeDtypeStruct((b, GQA_KV_HEADS, t, GQA_DIM), BF16),
                   jax.ShapeDtypeStruct((b, GQA_KV_HEADS, GQA_DIM, t), BF16)],
        compiler_params=_params("parallel", "parallel"),
        name="qkv_prep",
    )(*args)
    return outs[:3], outs[3:]


ATTN_KC = 256
ATTN_QB = 128
LOG2E = math.log2(math.e)


def _attn_logits(qts, k_ref, s_buf, nkeys):
    nqb = qts[0].shape[0] // ATTN_QB
    for j, q in enumerate(qts):
        st = lax.dot_general(k_ref[0, j, 0:nkeys, :], q, (((1,), (1,)), ((), ())),
                             preferred_element_type=F32)
        for b in range(nqb):
            s_buf[j * nqb + b, 0:nkeys, :] = st[:, b * ATTN_QB:(b + 1) * ATTN_QB]


def _attn_softmax_slab(s_buf, p_buf, l_buf, b, nkeys):
    parts = [jnp.max(s_buf[b, k0:k0 + ATTN_KC, :].reshape(ATTN_KC // 64, 8, 8, ATTN_QB), axis=0)
             for k0 in range(0, nkeys, ATTN_KC)]
    while len(parts) > 1:
        parts = [jnp.maximum(parts[i], parts[i + 1]) if i + 1 < len(parts) else parts[i]
                 for i in range(0, len(parts), 2)]
    m = jnp.max(parts[0], axis=(0, 1), keepdims=True)[0]
    lacc = jnp.zeros((8, 8, ATTN_QB), F32)
    for k0 in range(0, nkeys, ATTN_KC):
        p = jnp.exp2(s_buf[b, k0:k0 + ATTN_KC, :] - m)
        lacc = lacc + jnp.sum(p.reshape(ATTN_KC // 64, 8, 8, ATTN_QB), axis=0)
        p_buf[b, k0:k0 + ATTN_KC, :] = p.astype(p_buf.dtype)
    l = jnp.sum(lacc, axis=(0, 1), keepdims=True)[0]
    l_buf[b] = jnp.broadcast_to(1.0 / l, (8, ATTN_QB))


def _attn_output(vt_ref, p_buf, l_buf, o_ref, nkeys, hps, group, tq):
    dv = vt_ref.shape[2]
    nqb = group * tq // ATTN_QB
    for j in range(hps):
        slabs = range(j * nqb, (j + 1) * nqb)
        pt = jnp.concatenate([p_buf[b, 0:nkeys, :] for b in slabs], axis=1)
        ot = jnp.dot(vt_ref[0, j, :, 0:nkeys], pt, preferred_element_type=F32)
        ot = ot * jnp.concatenate([l_buf[b, 0:1, :] for b in slabs], axis=1)
        for g in range(group):
            c0 = (j * group + g) * dv
            o_ref[0, :, c0:c0 + dv] = ot[:, g * tq:(g + 1) * tq].T.astype(o_ref.dtype)


def _attn_queries(q_ref, hps, group):
    return [jnp.concatenate([q_ref[0, j * group + g] for g in range(group)], axis=0)
            for j in range(hps)]


def _attn_ctx_kernel(q_ref, k_ref, vt_ref, o_ref, s_sc, p_sc, l_sc, *, group, tq):
    hps, nkeys = k_ref.shape[1], k_ref.shape[2]
    nslab = hps * group * tq // ATTN_QB
    _attn_logits(_attn_queries(q_ref, hps, group), k_ref, s_sc, nkeys)

    def slab(b, carry):
        _attn_softmax_slab(s_sc, p_sc, l_sc, b, nkeys)
        return carry

    lax.fori_loop(0, nslab, slab, 0)
    _attn_output(vt_ref, p_sc, l_sc, o_ref, nkeys, hps, group, tq)


def _attn_lat_kernel(q_ref, k_ref, vt_ref, o_ref, s0, s1, p0, p1, l0, l1, *, group, tq):
    hps, nkeys = k_ref.shape[1], k_ref.shape[2]
    nslab = hps * group * tq // ATTN_QB
    g = pl.program_id(0)

    @pl.when(g == 0)
    def _():
        for buf in (s0, s1, p0, p1, l0, l1):
            buf[...] = jnp.zeros(buf.shape, buf.dtype)

    def step(s_a, s_b, p_b, p_c, l_b, l_c):
        _attn_logits(_attn_queries(q_ref, hps, group), k_ref, s_a, nkeys)
        for b in range(nslab):
            _attn_softmax_slab(s_b, p_b, l_b, b, nkeys)
        _attn_output(vt_ref, p_c, l_c, o_ref, nkeys, hps, group, tq)

    pl.when(g % 2 == 0)(lambda: step(s0, s1, p1, p0, l1, l0))
    pl.when(g % 2 == 1)(lambda: step(s1, s0, p0, p1, l0, l1))


def _attention(q, k, vt, ctx_len, tq, hps, with_ctx_queries, out_dtype):
    b, hq, t, dk = q.shape
    hkv, dv = k.shape[1], vt.shape[2]
    group = hq // hkv
    tq = _tile(ctx_len, tq, 128)
    nslab = hps * group * tq // ATTN_QB
    ctx_tiles = ctx_len // tq
    nq = t // tq - ctx_tiles
    width = hps * group * dv
    assert t % ATTN_KC == 0 and ctx_len % ATTN_KC == 0 and hkv % hps == 0

    def scratch(nkeys):
        return [pltpu.VMEM((nslab, nkeys, ATTN_QB), F32), pltpu.VMEM((nslab, nkeys, ATTN_QB), BF16),
                pltpu.VMEM((nslab, 8, ATTN_QB), F32)]

    s_lat, p_lat, l_lat = scratch(t)
    nh = hkv // hps
    items = b * nh * nq

    def item(step):
        i = jnp.clip(step, 0, items - 1)
        return i // (nh * nq), (i // nq) % nh, i % nq

    def q_map(g):
        bi, h, qi = item(g)
        return bi, h, qi + ctx_tiles, 0

    def k_map(g):
        bi, h, _ = item(g)
        return bi, h, 0, 0

    def v_map(g):
        bi, h, _ = item(g - 2)
        return bi, h, 0, 0

    def o_map(g):
        bi, h, qi = item(g - 2)
        return bi, qi, h

    y_lat = pl.pallas_call(
        functools.partial(_attn_lat_kernel, group=group, tq=tq),
        grid=(items + 2,),
        in_specs=[pl.BlockSpec((1, hps * group, tq, dk), q_map),
                  pl.BlockSpec((1, hps, t, dk), k_map),
                  pl.BlockSpec((1, hps, dv, t), v_map)],
        out_specs=pl.BlockSpec((1, tq, width), o_map),
        out_shape=jax.ShapeDtypeStruct((b, nq * tq, hq * dv), out_dtype),
        scratch_shapes=[s_lat, s_lat, p_lat, p_lat, l_lat, l_lat],
        compiler_params=_params("arbitrary"),
        name="attention_lat_dk%d" % dk,
    )(q, k, vt)
    if not with_ctx_queries:
        return y_lat
    y_ctx = pl.pallas_call(
        functools.partial(_attn_ctx_kernel, group=group, tq=tq),
        grid=(b, hkv // hps, ctx_tiles),
        in_specs=[pl.BlockSpec((1, hps * group, tq, dk), lambda bi, h, i: (bi, h, i, 0)),
                  pl.BlockSpec((1, hps, ctx_len, dk), lambda bi, h, i: (bi, h, 0, 0)),
                  pl.BlockSpec((1, hps, dv, ctx_len), lambda bi, h, i: (bi, h, 0, 0))],
        out_specs=pl.BlockSpec((1, tq, width), lambda bi, h, i: (bi, i, h)),
        out_shape=jax.ShapeDtypeStruct((b, ctx_len, hq * dv), out_dtype),
        scratch_shapes=scratch(ctx_len),
        compiler_params=_params("parallel", "parallel", "arbitrary"),
        name="attention_ctx_dk%d" % dk,
    )(q, k, vt)
    return jnp.concatenate([y_ctx, y_lat], axis=1)


def _conv_kernel(x_ref, w_ref, b_ref, xt_ref, bc_ref, pad_sc, *, ctx_len, rows, x_blocks):
    t = x_ref.shape[1]
    nch = x_ref.shape[2]
    halo = 8
    segs = ((0, ctx_len), (ctx_len, t))
    zeros = jnp.zeros((halo, nch), F32)
    for si, (lo, hi) in enumerate(segs):
        pad_sc[lo + si * halo:lo + (si + 1) * halo, :] = zeros
        for r0 in range(lo, hi, rows):
            pad_sc[r0 + (si + 1) * halo:r0 + (si + 1) * halo + rows, :] = x_ref[0, r0:r0 + rows, :].astype(F32)
    pad_sc[t + 2 * halo:t + 3 * halo, :] = zeros
    w = w_ref[...]
    bias = b_ref[...]

    def emit(transposed):
        for si, (lo, hi) in enumerate(segs):
            for r0 in range(lo, hi, rows):
                base = r0 + (si + 1) * halo - SSD_CONV // 2
                acc = bias + w[0:1, :] * pad_sc[base:base + rows, :]
                for kk in range(1, SSD_CONV):
                    acc = acc + w[kk:kk + 1, :] * pad_sc[base + kk:base + kk + rows, :]
                y = _silu(acc)
                if transposed:
                    xt_ref[0, :, r0:r0 + rows] = y.T.astype(xt_ref.dtype)
                else:
                    bc_ref[0, r0:r0 + rows, :] = y.astype(bc_ref.dtype)

    is_x = pl.program_id(1) < x_blocks
    pl.when(is_x)(lambda: emit(True))
    pl.when(jnp.logical_not(is_x))(lambda: emit(False))


def _conv(p3, conv_w, conv_b, ctx_len, out_dtype):
    b, t, _ = p3.shape
    nch = 256
    rows = _tile(ctx_len, 256)
    blk0 = P_OFF['xbc'] // nch
    x_blocks = SSD_INNER // nch
    kern = functools.partial(_conv_kernel, ctx_len=ctx_len, rows=rows, x_blocks=x_blocks)
    return pl.pallas_call(
        kern,
        grid=(b, SSD_CONV_DIM // nch),
        in_specs=[pl.BlockSpec((1, t, nch), lambda bi, j: (bi, 0, blk0 + j)),
                  pl.BlockSpec((SSD_CONV, nch), lambda bi, j: (0, j)),
                  pl.BlockSpec((1, nch), lambda bi, j: (0, j))],
        out_specs=[pl.BlockSpec((1, nch, t), lambda bi, j: (bi, jnp.minimum(j, x_blocks - 1), 0)),
                   pl.BlockSpec((1, t, nch), lambda bi, j: (bi, 0, jnp.maximum(j - x_blocks, 0)))],
        out_shape=[jax.ShapeDtypeStruct((b, SSD_INNER, t), out_dtype),
                   jax.ShapeDtypeStruct((b, t, SSD_CONV_DIM - SSD_INNER), out_dtype)],
        scratch_shapes=[pltpu.VMEM((t + 24, nch), F32)],
        compiler_params=_params("parallel", "arbitrary"),
        name="ssd_conv",
    )(p3, conv_w, conv_b.reshape(1, SSD_CONV_DIM))


def _split3(v):
    hi = v.astype(BF16)
    r1 = v - hi.astype(F32)
    mid = r1.astype(BF16)
    lo = (r1 - mid.astype(F32)).astype(BF16)
    return hi, mid, lo


def _expand_rows(v):
    q = v.shape[1]
    return jnp.concatenate([jnp.broadcast_to(v[e:e + 1, :], (SSD_P, q)) for e in range(SSD_E)], axis=0)


def _ssd_decays(dt_col_raw, dt_row_raw, bias_col, bias_row, a_col, a_row, backward):
    q = dt_col_raw.shape[0]
    ri = lax.broadcasted_iota(jnp.int32, (q, q), 0)
    ci = lax.broadcasted_iota(jnp.int32, (q, q), 1)
    tri_col = jnp.where((ri <= ci) if backward else (ri >= ci), 1.0, 0.0).astype(BF16)
    tri_row = jnp.where((ri >= ci) if backward else (ri <= ci), 1.0, 0.0).astype(BF16)
    dt_c = _softplus(dt_col_raw + bias_col)
    dt_r = _softplus(dt_row_raw + bias_row)
    cum_c = sum(jnp.dot(tri_col, part, preferred_element_type=F32) for part in _split3(dt_c * a_col))
    cum_r = sum(jnp.dot(part, tri_row, preferred_element_type=F32) for part in _split3(dt_r * a_row))
    total = jnp.broadcast_to(cum_r[:, 0:1] if backward else cum_r[:, q - 1:q], (SSD_E, q))
    return dt_r, cum_c, cum_r, total


def _ssd_direction(xt, bm, cm, decays, h_ref, backward):
    q = xt.shape[1]
    dt_r, cum_c, cum_r, total = decays
    ri = lax.broadcasted_iota(jnp.int32, (q, q), 0)
    ci = lax.broadcasted_iota(jnp.int32, (q, q), 1)
    keep_t = (ci <= ri) if backward else (ci >= ri)

    bmb = bm.astype(BF16)
    cmb = cm.astype(BF16)
    nt = (((1,), (1,)), ((), ()))
    cbt = lax.dot_general(bmb, cmb, nt, preferred_element_type=F32)
    h = h_ref[...]
    y_off = lax.dot_general(h.astype(BF16), cmb, nt, preferred_element_type=F32) * _expand_rows(jnp.exp(cum_r))
    wgt = (xt * _expand_rows(dt_r * jnp.exp(total - cum_r))).astype(BF16)
    h_ref[...] = _expand_rows(jnp.exp(total)) * h + jnp.dot(wgt, bmb, preferred_element_type=F32)

    xdt = (xt * _expand_rows(dt_r)).astype(BF16)
    parts = []
    for e in range(SSD_E):
        seg = cum_r[e:e + 1, :] - cum_c[:, e:e + 1]
        dec = jnp.exp(jnp.where(keep_t, seg, -jnp.inf))
        parts.append(jnp.dot(xdt[e * SSD_P:(e + 1) * SSD_P, :], (cbt * dec).astype(BF16),
                             preferred_element_type=F32))
    return jnp.concatenate(parts, axis=0) + y_off


def _ssd_kernel(xf_ref, bf_ref, cf_ref, xb_ref, bb_ref, cb_ref, dcf_ref, drf_ref, dcb_ref, drb_ref,
                bias_c_ref, bias_r_ref, a_c_ref, a_r_ref, d_ref, yf_ref, yb_ref, hf_sc, hb_sc):
    @pl.when(pl.program_id(2) == 0)
    def _():
        hf_sc[...] = jnp.zeros(hf_sc.shape, F32)
        hb_sc[...] = jnp.zeros(hb_sc.shape, F32)

    nsub = xf_ref.shape[2] // SSD_Q
    gps = dcf_ref.shape[2]
    chunks = [slice(i * SSD_Q, (i + 1) * SSD_Q) for i in range(nsub)]
    dec_f = [[_ssd_decays(dcf_ref[0, 0, gi, rs, :], drf_ref[0, 0, gi, :, rs], bias_c_ref[0, gi],
                          bias_r_ref[0, gi], a_c_ref[0, gi], a_r_ref[0, gi], False) for rs in chunks]
             for gi in range(gps)]
    dec_b = [[_ssd_decays(dcb_ref[0, 0, gi, rs, :], drb_ref[0, 0, gi, :, rs], bias_c_ref[1, gi],
                          bias_r_ref[1, gi], a_c_ref[1, gi], a_r_ref[1, gi], True) for rs in chunks]
             for gi in range(gps)]
    for gi in range(gps):
        ch = slice(gi * SSD_GW, (gi + 1) * SSD_GW)
        st = slice(gi * SSD_N, (gi + 1) * SSD_N)
        for i in range(nsub):
            rf, rb = chunks[i], chunks[nsub - 1 - i]
            xt = xf_ref[0, ch, rf].astype(F32)
            yf = _ssd_direction(xt, bf_ref[0, rf, st].astype(F32), cf_ref[0, rf, st].astype(F32),
                                dec_f[gi][i], hf_sc.at[ch], False)
            yf_ref[0, ch, rf] = (yf + d_ref[gi] * xt).astype(yf_ref.dtype)
            yb = _ssd_direction(xb_ref[0, ch, rb].astype(F32), bb_ref[0, rb, st].astype(F32),
                                cb_ref[0, rb, st].astype(F32), dec_b[gi][nsub - 1 - i], hb_sc.at[ch], True)
            yb_ref[0, ch, rb] = yb.astype(yb_ref.dtype)


def _ssd(xt, bc, dt_col, dt_row, bias_c, bias_r, a_c, a_r, d_rows, ctx_len, out_dtype):
    b, _, t = xt.shape
    assert SSD_Q == SSD_N
    rows = _tile(ctx_len, SSD_ROWS, SSD_Q)
    nblk = t // rows
    nctx = ctx_len // rows

    def bidx(c):
        return jnp.where(c < nctx, nctx - 1 - c, nblk - 1 - (c - nctx))

    gps = SSD_GPS
    ngs = SSD_G // gps
    fx = lambda bi, g, c: (bi, g, c)
    fb = lambda bi, g, c: (bi, c, g)
    fc = lambda bi, g, c: (bi, c, ngs + g)
    bx = lambda bi, g, c: (bi, g, bidx(c))
    bb = lambda bi, g, c: (bi, bidx(c), g)
    bcm = lambda bi, g, c: (bi, bidx(c), ngs + g)
    small = lambda a: pl.BlockSpec((2, gps) + a.shape[2:], lambda bi, g, c: (0, g, 0, 0))
    return pl.pallas_call(
        _ssd_kernel,
        grid=(b, ngs, nblk),
        in_specs=[pl.BlockSpec((1, gps * SSD_GW, rows), fx), pl.BlockSpec((1, rows, gps * SSD_N), fb),
                  pl.BlockSpec((1, rows, gps * SSD_N), fc),
                  pl.BlockSpec((1, gps * SSD_GW, rows), bx), pl.BlockSpec((1, rows, gps * SSD_N), bb),
                  pl.BlockSpec((1, rows, gps * SSD_N), bcm),
                  pl.BlockSpec((1, 1, gps, rows, SSD_E), lambda bi, g, c: (bi, 0, g, c, 0)),
                  pl.BlockSpec((1, 1, gps, SSD_E, rows), lambda bi, g, c: (bi, 0, g, 0, c)),
                  pl.BlockSpec((1, 1, gps, rows, SSD_E), lambda bi, g, c: (bi, 1, g, bidx(c), 0)),
                  pl.BlockSpec((1, 1, gps, SSD_E, rows), lambda bi, g, c: (bi, 1, g, 0, bidx(c))),
                  small(bias_c), small(bias_r), small(a_c), small(a_r),
                  pl.BlockSpec((gps, SSD_GW, SSD_Q), lambda bi, g, c: (g, 0, 0))],
        out_specs=[pl.BlockSpec((1, gps * SSD_GW, rows), fx), pl.BlockSpec((1, gps * SSD_GW, rows), bx)],
        out_shape=[jax.ShapeDtypeStruct((b, SSD_INNER, t), out_dtype)] * 2,
        scratch_shapes=[pltpu.VMEM((gps * SSD_GW, SSD_N), F32), pltpu.VMEM((gps * SSD_GW, SSD_N), F32)],
        compiler_params=_params("parallel", "parallel", "arbitrary"),
        name="ssd_scan",
    )(xt, bc, bc, xt, bc, bc, dt_col, dt_row, dt_col, dt_row, bias_c, bias_r, a_c, a_r, d_rows)


MERGE_TN = 2048


def _merge_kernel(ya_ref, ga_ref, yb_ref, gb_ref, yf_ref, ybk_ref, z_ref, nrm_ref,
                  mg_ref, wa_ref, wb_ref, wc_ref, u_ref, c_sc):
    a_in = (ya_ref[0].astype(F32) * _silu(ga_ref[0].astype(F32))).astype(BF16)
    b_in = (yb_ref[0].astype(F32) * _silu(gb_ref[0].astype(F32))).astype(BF16)
    v = (yf_ref[0].astype(F32) + ybk_ref[0].astype(F32)).T * _silu(z_ref[0].astype(F32))
    for g in range(SSD_G):
        sl = slice(g * SSD_GW, (g + 1) * SSD_GW)
        c_sc[:, sl] = (_rms(v[:, sl]) * nrm_ref[:, sl]).astype(BF16)
    c_in = c_sc[...]
    d = u_ref.shape[-1]
    for j in range(0, d, MERGE_TN):
        sl = slice(j, j + MERGE_TN)
        br_a = jnp.dot(a_in, wa_ref[:, sl], preferred_element_type=F32)
        br_b = jnp.dot(b_in, wb_ref[:, sl], preferred_element_type=F32)
        br_c = jnp.dot(c_in, wc_ref[:, sl], preferred_element_type=F32)
        u = (jax.nn.sigmoid(mg_ref[0, :, j:j + MERGE_TN].astype(F32)) * br_a
             + jax.nn.sigmoid(mg_ref[0, :, d + j:d + j + MERGE_TN].astype(F32)) * br_b
             + jax.nn.sigmoid(mg_ref[0, :, 2 * d + j:2 * d + j + MERGE_TN].astype(F32)) * br_c)
        u_ref[0, :, sl] = u.astype(u_ref.dtype)


def _merge(ya, yb, yf, ybk, p3, ssd_norm, wa, wb, wc, row_off):
    b, rows, _ = ya.shape
    tm = _tile(rows, 256)
    d = D_MODEL
    ro = row_off // tm
    assert P_OFF['mg'] == 0

    def col(name, width):
        blk = P_OFF[name] // width
        return pl.BlockSpec((1, tm, width), lambda bi, i: (bi, i + ro, blk))

    loc = lambda w: pl.BlockSpec((1, tm, w), lambda bi, i: (bi, i, 0))
    glob = lambda w: pl.BlockSpec((1, tm, w), lambda bi, i: (bi, i + ro, 0))
    globt = pl.BlockSpec((1, SSD_INNER, tm), lambda bi, i: (bi, 0, i + ro))
    wspec = lambda k: pl.BlockSpec((k, d), lambda bi, i: (0, 0), pipeline_mode=pl.Buffered(1))
    return pl.pallas_call(
        _merge_kernel,
        grid=(b, rows // tm),
        in_specs=[loc(MLA_WIDTH), col('ga', MLA_WIDTH), loc(GQA_WIDTH), col('gb', GQA_WIDTH),
                  globt, globt, col('z', SSD_INNER),
                  pl.BlockSpec((1, SSD_INNER), lambda bi, i: (0, 0)),
                  glob(N_BRANCH * d), wspec(MLA_WIDTH), wspec(GQA_WIDTH), wspec(SSD_INNER)],
        out_specs=pl.BlockSpec((1, tm, d), lambda bi, i: (bi, i, 0)),
        out_shape=jax.ShapeDtypeStruct((b, rows, d), BF16),
        scratch_shapes=[pltpu.VMEM((tm, SSD_INNER), BF16)],
        compiler_params=_params("parallel", "parallel"),
        name="merge",
    )(ya, p3, yb, p3, yf, ybk, p3, ssd_norm.reshape(1, SSD_INNER), p3, wa, wb, wc)


def _out_ln_kernel(u_ref, w_ref, *rest, with_next, ctx_tiles, ro, split):
    if split:
        ctx_ref, x_ref, gate_ref, g_ref, b_ref, *rest = rest
        res = _residual_rows(ctx_ref, x_ref, ctx_tiles, ro)
    else:
        x_ref, gate_ref, g_ref, b_ref, *rest = rest
        res = x_ref[0]
    out = jnp.dot(u_ref[0], w_ref[...], preferred_element_type=F32)
    r = DEEPNORM_ALPHA * res + gate_ref[0] * out
    xn = _layer_norm(r) * g_ref[...] + b_ref[...]
    if with_next:
        sh_ref, sc_ref, xo_ref, xm_ref = rest
        xo_ref[0] = xn
        xm_ref[0] = (_layer_norm(xn) * (1.0 + sc_ref[0]) + sh_ref[0]).astype(xm_ref.dtype)
    else:
        (xo_ref,) = rest
        xo_ref[0] = xn


def _out_ln(u, w_out, xc, tab, ln_g, ln_b, next_tab, nb, ctx_len, row_off):
    b, rows, d = u.shape
    tm = _tile(ctx_len, 256)
    ro = row_off // tm
    ctx_tiles = ctx_len // tm
    row = _mod_row_index(nb, ctx_tiles)
    with_next = next_tab is not None
    split = isinstance(xc, tuple)
    loc = pl.BlockSpec((1, tm, d), lambda bi, i: (bi, i, 0))
    vec = pl.BlockSpec((1, d), lambda bi, i: (0, 0))
    res_specs = (_residual_specs(tm, d, ctx_tiles, ro) if split
                 else [pl.BlockSpec((1, tm, d), lambda bi, i: (bi, i + ro, 0))])
    in_specs = [loc, pl.BlockSpec((d, d), lambda bi, i: (0, 0))] + res_specs + [
        pl.BlockSpec((1, 1, d), lambda bi, i: (row(bi, i + ro), 0, 2)), vec, vec]
    args = [u, w_out] + (list(xc) if split else [xc]) + [tab, ln_g.reshape(1, d), ln_b.reshape(1, d)]
    out_specs = [loc]
    out_shape = [jax.ShapeDtypeStruct((b, rows, d), F32)]
    if with_next:
        in_specs += [pl.BlockSpec((1, 1, d), lambda bi, i: (row(bi, i + ro), 0, 0)),
                     pl.BlockSpec((1, 1, d), lambda bi, i: (row(bi, i + ro), 0, 1))]
        args += [next_tab, next_tab]
        out_specs.append(loc)
        out_shape.append(jax.ShapeDtypeStruct((b, rows, d), BF16))
    return pl.pallas_call(
        functools.partial(_out_ln_kernel, with_next=with_next, ctx_tiles=ctx_tiles, ro=ro, split=split),
        grid=(b, rows // tm),
        in_specs=in_specs,
        out_specs=out_specs,
        out_shape=out_shape,
        compiler_params=_params("parallel", "parallel"),
        name="out_ln",
    )(*args)


def _rope_angles(rows, dim):
    row, col = jnp.meshgrid(jnp.arange(rows, dtype=F32), jnp.arange(GRID_W, dtype=F32), indexing='ij')
    half = dim // 2
    inv_freq = ROPE_THETA ** (-jnp.arange(0, half, 2, dtype=F32) / half)
    ang_r = row.reshape(-1, 1) * inv_freq
    ang_c = col.reshape(-1, 1) * inv_freq
    return jnp.concatenate([ang_r, ang_r, ang_c, ang_c], axis=-1)


def _rope_tables(seq, ctx_len, dim):
    ang = _rope_angles(seq // GRID_W, dim)
    cos = jnp.concatenate([jnp.ones((ctx_len, dim), F32), jnp.cos(ang)], axis=0)
    sin = jnp.concatenate([jnp.zeros((ctx_len, dim), F32), jnp.sin(ang)], axis=0)
    return cos, sin


def _roll_tables(cos, sin, dim, scale):
    t = cos.shape[0]
    quarter = dim // 4
    first = (jnp.arange(dim) % (2 * quarter)) < quarter
    s1 = jnp.where(first, -sin, 0.0)
    s2 = jnp.where(first, 0.0, sin)
    pad = lambda a: jnp.pad(a * scale, ((0, 0), (0, 128 - dim)))
    return pad(cos), pad(s1), pad(s2)


def _rot_matrix(dim):
    quarter = dim // 4
    r = np.zeros((dim, dim), np.float32)
    for i in range(dim):
        blk = i // quarter
        if blk % 2 == 0:
            r[i + quarter, i] = -1.0
        else:
            r[i - quarter, i] = 1.0
    return jnp.asarray(r)


def _mla_q_weights(w_uq):
    w = w_uq.reshape(MLA_Q_LORA, MLA_HEADS, MLA_QK)
    nope, pe = w[..., :MLA_NOPE], w[..., MLA_NOPE:]
    pe_rot = jnp.einsum('khd,de->khe', pe, _rot_matrix(MLA_ROPE), precision=HIGHEST)
    zpad = jnp.zeros((MLA_Q_LORA, MLA_HEADS, MLA_QK_PAD - MLA_QK), F32)
    main = jnp.concatenate([nope, pe, zpad], axis=-1)
    rot = jnp.concatenate([jnp.zeros_like(nope), pe_rot, zpad], axis=-1)
    return (main.reshape(MLA_Q_LORA, -1).astype(BF16), rot.reshape(MLA_Q_LORA, -1).astype(BF16))


def _permute_w_in(w):
    parts = [w[:, IN_OFFSETS[n]:IN_OFFSETS[n] + IN_WIDTHS[n]] for n in P_ORDER]
    parts.append(jnp.zeros((w.shape[0], P_WIDTH - P_USED), w.dtype))
    return jnp.concatenate(parts, axis=1).astype(BF16)


P_DTYPE = BF16
Y_DTYPE = BF16


def kernel(x, c, ctx, c_ctx, w_mod, b_mod, w_in, mla_q_norm, mla_w_uq, mla_kv_norm, mla_w_ukv,
           gqa_q_norm, gqa_k_norm, ssd_conv_w, ssd_conv_b, ssd_a_log, ssd_dt_bias, ssd_d, ssd_norm,
           w_br_a, w_br_b, w_br_c, w_out, ln_g, ln_b):
    nb, seq, d = x.shape
    ctx_len = ctx.shape[1]
    t = ctx_len + seq
    depth = w_in.shape[0]
    assert d == D_MODEL and nb < 8 and seq % GRID_W == 0
    assert ctx_len % SSD_Q == 0 and seq % SSD_Q == 0

    cos_a, sin_a = _rope_tables(seq, ctx_len, MLA_ROPE)
    cos_b, sin_b = _rope_tables(seq, ctx_len, GQA_DIM)
    sq = MLA_QK ** -0.5 * LOG2E
    zq = jnp.zeros((t, MLA_QK_PAD - MLA_QK), F32)
    cosq = jnp.concatenate([jnp.full((t, MLA_NOPE), sq, F32), cos_a * sq, zq], axis=1)
    sinq = jnp.concatenate([jnp.zeros((t, MLA_NOPE), F32), sin_a * sq, zq], axis=1)
    mla_tabs = (cosq, sinq) + _roll_tables(cos_a, sin_a, MLA_ROPE, 1.0)
    gqa_tabs = (_roll_tables(cos_b, sin_b, GQA_DIM, GQA_DIM ** -0.5 * LOG2E)
                + _roll_tables(cos_b, sin_b, GQA_DIM, 1.0))

    c_rows = jnp.zeros((8, d), F32).at[:nb].set(c).at[nb].set(c_ctx)
    tabs = [_mod_rows(c_rows, w_mod[l], b_mod[l]).reshape(8, 1, 3 * d) for l in range(depth)]

    xc = (ctx, x)
    xm = _ln_mod(ctx, x, tabs[0], nb)

    for l in range(depth):
        last = l == depth - 1
        wp = _permute_w_in(w_in[l])
        p2, krdt = _in_proj(xm.reshape(nb * t, d), wp, P_DTYPE)
        p3 = p2.reshape(nb, t, P_WIDTH)

        wqm, wqr = _mla_q_weights(mla_w_uq[l])
        (qa, ka, va), (qb, kb, vb) = _qkv_prep(
            p3, mla_q_norm[l].reshape(1, -1), mla_kv_norm[l].reshape(1, -1), wqm, wqr,
            mla_w_ukv[l].astype(BF16), mla_tabs, gqa_q_norm[l].reshape(1, -1),
            gqa_k_norm[l].reshape(1, -1), gqa_tabs, ctx_len)
        ya = _attention(qa, ka, va, ctx_len, 256, 2, not last, Y_DTYPE)
        yb = _attention(qb, kb, vb, ctx_len, 128, 1, not last, Y_DTYPE)

        xconv_t, bconv = _conv(p3, ssd_conv_w[l], ssd_conv_b[l], ctx_len, Y_DTYPE)
        dt5 = krdt[:, 64:].reshape(nb, t, 2, SSD_G, SSD_E)
        dt_col = jnp.transpose(dt5, (0, 2, 3, 1, 4))
        dt_row = jnp.transpose(dt5, (0, 2, 3, 4, 1))
        bias = ssd_dt_bias[l].astype(F32).reshape(2, SSD_G, SSD_E)
        a = -jnp.exp(ssd_a_log[l].astype(F32)).reshape(2, SSD_G, SSD_E)
        d_rows = jnp.broadcast_to(jnp.repeat(ssd_d[l].astype(F32), SSD_P).reshape(SSD_G, SSD_GW, 1),
                                  (SSD_G, SSD_GW, SSD_Q))
        yf, ybk = _ssd(xconv_t, bconv, dt_col, dt_row, bias[:, :, None, :], bias[:, :, :, None],
                       a[:, :, None, :], a[:, :, :, None], d_rows, ctx_len, Y_DTYPE)

        row_off = ctx_len if last else 0
        u = _merge(ya, yb, yf, ybk, p3, ssd_norm[l], w_br_a[l].astype(BF16), w_br_b[l].astype(BF16),
                   w_br_c[l].astype(BF16), row_off)
        if last:
            (xo,) = _out_ln(u, w_out[l].astype(BF16), xc, tabs[l], ln_g[l], ln_b[l], None, nb,
                            ctx_len, row_off)
            return xo
        xc, xm = _out_ln(u, w_out[l].astype(BF16), xc, tabs[l], ln_g[l], ln_b[l], tabs[l + 1], nb,
                         ctx_len, row_off)
```

```python
import functools
import math

import numpy as np
import jax
import jax.numpy as jnp
from jax import lax
from jax.experimental import pallas as pl
from jax.experimental.pallas import tpu as pltpu

F32 = jnp.float32
BF16 = jnp.bfloat16
HIGHEST = lax.Precision.HIGHEST

D_MODEL = 2048
DEPTH = 2
GRID_W = 64
ROPE_THETA = 10000.0
EPS = 1e-6

MLA_HEADS = 8
MLA_Q_LORA = 512
MLA_KV_LORA = 256
MLA_NOPE = 128
MLA_ROPE = 64
MLA_V = 128
MLA_QK = MLA_NOPE + MLA_ROPE
MLA_QK_PAD = 256
MLA_WIDTH = MLA_HEADS * MLA_V

GQA_HEADS = 8
GQA_KV_HEADS = 2
GQA_GROUP = GQA_HEADS // GQA_KV_HEADS
GQA_DIM = 128
GQA_WIDTH = GQA_HEADS * GQA_DIM
GQA_KV_WIDTH = GQA_KV_HEADS * GQA_DIM

SSD_INNER = D_MODEL
SSD_P = 64
SSD_HEADS = SSD_INNER // SSD_P
SSD_G = 4
SSD_E = SSD_HEADS // SSD_G
SSD_N = 128
SSD_CONV = 5
SSD_Q = 128
SSD_ROWS = 256
SSD_GPS = 4
SSD_GW = SSD_E * SSD_P
SSD_CONV_DIM = SSD_INNER + 2 * SSD_G * SSD_N

N_BRANCH = 3
IN_SPLITS = (MLA_Q_LORA, MLA_KV_LORA, MLA_ROPE, MLA_WIDTH, GQA_WIDTH, GQA_KV_WIDTH, GQA_KV_WIDTH,
             GQA_WIDTH, SSD_INNER, SSD_CONV_DIM, 2 * SSD_HEADS, N_BRANCH * D_MODEL)
IN_NAMES = ('cq', 'ckv', 'kr', 'ga', 'gq', 'gk', 'gv', 'gb', 'z', 'xbc', 'dtr', 'mg')
IN_OFFSETS = dict(zip(IN_NAMES, np.concatenate([[0], np.cumsum(IN_SPLITS)[:-1]]).tolist()))
IN_WIDTHS = dict(zip(IN_NAMES, IN_SPLITS))
P_ORDER = ('mg', 'z', 'xbc', 'ga', 'gq', 'gb', 'cq', 'ckv', 'gk', 'gv', 'kr', 'dtr')
P_OFF = {}
_o = 0
for _n in P_ORDER:
    P_OFF[_n] = _o
    _o += IN_WIDTHS[_n]
P_USED = _o
P_TN = 512
IN_TM = 4352
P_WIDTH = -(-P_USED // P_TN) * P_TN

DEEPNORM_ALPHA = (2 * DEPTH) ** 0.25

VMEM_LIMIT = 56 * 2 ** 20


def _params(*sem):
    return pltpu.CompilerParams(dimension_semantics=sem, vmem_limit_bytes=VMEM_LIMIT)


def _tile(n, target, align=8):
    t = min(n, target)
    while t > align and (n % t or t % align):
        t -= align
    assert n % t == 0, (n, target)
    return t


def _silu(v):
    return v * jax.nn.sigmoid(v)


def _softplus(v):
    return jnp.maximum(v, 0.0) + jnp.log1p(jnp.exp(-jnp.abs(v)))


def _layer_norm(v):
    mu = jnp.mean(v, axis=-1, keepdims=True)
    vc = v - mu
    var = jnp.mean(vc * vc, axis=-1, keepdims=True)
    return vc * lax.rsqrt(var + EPS)


def _rms(v):
    return v * lax.rsqrt(jnp.mean(v * v, axis=-1, keepdims=True) + EPS)


def _mod_kernel(c_ref, w_ref, b_ref, o_ref):
    a = _silu(c_ref[...]).astype(BF16)
    o_ref[...] = jnp.dot(a, w_ref[...].astype(BF16), preferred_element_type=F32) + b_ref[...]


def _mod_rows(c_rows, w_mod, b_mod):
    r, d = c_rows.shape
    n = w_mod.shape[1]
    tn = 512
    return pl.pallas_call(
        _mod_kernel,
        grid=(n // tn,),
        in_specs=[pl.BlockSpec((r, d), lambda j: (0, 0)),
                  pl.BlockSpec((d, tn), lambda j: (0, j)),
                  pl.BlockSpec((1, tn), lambda j: (0, j))],
        out_specs=pl.BlockSpec((r, tn), lambda j: (0, j)),
        out_shape=jax.ShapeDtypeStruct((r, n), F32),
        compiler_params=_params("arbitrary"),
        name="mod_rows",
    )(c_rows, w_mod, b_mod.reshape(1, n))


def _residual_rows(ctx_ref, x_ref, ctx_tiles, ro):
    return jnp.where(pl.program_id(1) + ro < ctx_tiles, ctx_ref[0], x_ref[0])


def _residual_specs(tm, d, ctx_tiles, ro):
    return [pl.BlockSpec((1, tm, d), lambda bi, i: (bi, jnp.minimum(i + ro, ctx_tiles - 1), 0)),
            pl.BlockSpec((1, tm, d), lambda bi, i: (bi, jnp.maximum(i + ro - ctx_tiles, 0), 0))]


def _ln_mod_kernel(ctx_ref, x_ref, sh_ref, sc_ref, o_ref, *, ctx_tiles):
    y = _layer_norm(_residual_rows(ctx_ref, x_ref, ctx_tiles, 0))
    o_ref[0] = (y * (1.0 + sc_ref[0]) + sh_ref[0]).astype(o_ref.dtype)


def _mod_row_index(nb, ctx_tiles):
    return lambda b, i: jnp.where(i < ctx_tiles, nb, b)


def _ln_mod(ctx, x, tab, nb):
    b, seq, d = x.shape
    ctx_len = ctx.shape[1]
    tm = _tile(ctx_len, 256)
    ctx_tiles = ctx_len // tm
    row = _mod_row_index(nb, ctx_tiles)
    return pl.pallas_call(
        functools.partial(_ln_mod_kernel, ctx_tiles=ctx_tiles),
        grid=(b, (ctx_len + seq) // tm),
        in_specs=_residual_specs(tm, d, ctx_tiles, 0)
        + [pl.BlockSpec((1, 1, d), lambda bi, i: (row(bi, i), 0, 0)),
           pl.BlockSpec((1, 1, d), lambda bi, i: (row(bi, i), 0, 1))],
        out_specs=pl.BlockSpec((1, tm, d), lambda bi, i: (bi, i, 0)),
        out_shape=jax.ShapeDtypeStruct((b, ctx_len + seq, d), BF16),
        compiler_params=_params("parallel", "parallel"),
        name="ln_mod",
    )(ctx, x, tab, tab)


def _in_proj_kernel(x_ref, w_ref, o_ref, dt_ref, *, dt_tile, dt_col):
    acc = jnp.dot(x_ref[...], w_ref[...], preferred_element_type=F32)
    o_ref[...] = acc.astype(o_ref.dtype)

    @pl.when(pl.program_id(1) == dt_tile)
    def _():
        dt_ref[...] = acc[:, dt_col:dt_col + 128]


def _in_proj(xm2, wp, out_dtype):
    m, k = xm2.shape
    n = wp.shape[1]
    tm = _tile(m, IN_TM)
    tn = P_TN
    dt_tile, dt_col = divmod(P_OFF['kr'], tn)
    assert P_OFF['dtr'] == P_OFF['kr'] + 64 and dt_col % 128 == 0
    return pl.pallas_call(
        functools.partial(_in_proj_kernel, dt_tile=dt_tile, dt_col=dt_col),
        grid=(m // tm, n // tn),
        in_specs=[pl.BlockSpec((tm, k), lambda i, j: (i, 0), pipeline_mode=pl.Buffered(1)),
                  pl.BlockSpec((k, tn), lambda i, j: (0, j))],
        out_specs=[pl.BlockSpec((tm, tn), lambda i, j: (i, j)),
                   pl.BlockSpec((tm, 128), lambda i, j: (i, 0))],
        out_shape=[jax.ShapeDtypeStruct((m, n), out_dtype), jax.ShapeDtypeStruct((m, 128), F32)],
        compiler_params=_params("parallel", "arbitrary"),
        name="in_proj",
    )(xm2, wp)


def _mla_prep_kernel(cq_ref, ckv_ref, kr_ref, qn_ref, kvn_ref, wqm_ref, wqr_ref, wkv_ref,
                     cosq_ref, sinq_ref, ck_ref, s1_ref, s2_ref, qa_ref, ka_ref, va_ref):
    cqn = (_rms(cq_ref[0].astype(F32)) * qn_ref[...]).astype(BF16)
    qm = jnp.dot(cqn, wqm_ref[...], preferred_element_type=F32)
    qr = jnp.dot(cqn, wqr_ref[...], preferred_element_type=F32)
    cosq = cosq_ref[...]
    sinq = sinq_ref[...]
    for h in range(MLA_HEADS):
        sl = slice(h * MLA_QK_PAD, (h + 1) * MLA_QK_PAD)
        qa_ref[0, h] = (qm[:, sl] * cosq + qr[:, sl] * sinq).astype(qa_ref.dtype)
    ckvn = (_rms(ckv_ref[0].astype(F32)) * kvn_ref[...]).astype(BF16)
    kv = jnp.dot(ckvn, wkv_ref[...], preferred_element_type=F32)
    kr = kr_ref[0].astype(F32)
    kpe = (kr * ck_ref[...] + pltpu.roll(kr, 128 - 16, axis=1) * s1_ref[...]
           + pltpu.roll(kr, 16, axis=1) * s2_ref[...]).astype(ka_ref.dtype)
    for h in range(MLA_HEADS):
        base = h * (MLA_NOPE + MLA_V)
        ka_ref[0, h, :, 0:MLA_NOPE] = kv[:, base:base + MLA_NOPE].astype(ka_ref.dtype)
        ka_ref[0, h, :, MLA_NOPE:MLA_QK_PAD] = kpe
        va_ref[0, h] = kv[:, base + MLA_NOPE:base + MLA_NOPE + MLA_V].T.astype(va_ref.dtype)


def _rope128(y, c, s1, s2):
    return y * c + pltpu.roll(y, 128 - 32, axis=1) * s1 + pltpu.roll(y, 32, axis=1) * s2


def _gqa_prep_kernel(gq_ref, gk_ref, gv_ref, qn_ref, kn_ref, cq_ref, s1q_ref, s2q_ref,
                     ck_ref, s1k_ref, s2k_ref, qb_ref, kb_ref, vb_ref):
    gq = gq_ref[0].astype(F32)
    for h in range(GQA_HEADS):
        y = _rms(gq[:, h * GQA_DIM:(h + 1) * GQA_DIM]) * qn_ref[...]
        qb_ref[0, h] = _rope128(y, cq_ref[...], s1q_ref[...], s2q_ref[...]).astype(qb_ref.dtype)
    gk = gk_ref[0].astype(F32)
    gv = gv_ref[0].astype(F32)
    for h in range(GQA_KV_HEADS):
        y = _rms(gk[:, h * GQA_DIM:(h + 1) * GQA_DIM]) * kn_ref[...]
        kb_ref[0, h] = _rope128(y, ck_ref[...], s1k_ref[...], s2k_ref[...]).astype(kb_ref.dtype)
        vb_ref[0, h] = gv[:, h * GQA_DIM:(h + 1) * GQA_DIM].T.astype(vb_ref.dtype)


N_MLA_IN, N_GQA_IN = 13, 11


def _qkv_prep_kernel(*refs):
    mla_in, gqa_in = refs[:N_MLA_IN], refs[N_MLA_IN:N_MLA_IN + N_GQA_IN]
    outs = refs[N_MLA_IN + N_GQA_IN:]
    _mla_prep_kernel(*mla_in, *outs[:3])
    _gqa_prep_kernel(*gqa_in, *outs[3:])


def _qkv_prep(p3, mla_qn, mla_kvn, wqm, wqr, wkv, mla_tabs, gqa_qn, gqa_kn, gqa_tabs, ctx_len):
    b, t, _ = p3.shape
    tm = _tile(ctx_len, 256)

    def col(name, width):
        blk = P_OFF[name] // width
        return pl.BlockSpec((1, tm, width), lambda bi, i: (bi, i, blk))

    def full(a):
        return pl.BlockSpec(a.shape, lambda bi, i: (0,) * a.ndim)

    def rows(a):
        return pl.BlockSpec((tm, a.shape[1]), lambda bi, i: (i, 0))

    hm = lambda nh, w: pl.BlockSpec((1, nh, tm, w), lambda bi, i: (bi, 0, i, 0))
    hmt = lambda nh, w: pl.BlockSpec((1, nh, w, tm), lambda bi, i: (bi, 0, 0, i))
    in_specs = ([col('cq', MLA_Q_LORA), col('ckv', MLA_KV_LORA), col('kr', 128),
                 full(mla_qn), full(mla_kvn), full(wqm), full(wqr), full(wkv)]
                + [rows(a) for a in mla_tabs]
                + [col('gq', GQA_WIDTH), col('gk', GQA_KV_WIDTH), col('gv', GQA_KV_WIDTH),
                   full(gqa_qn), full(gqa_kn)] + [rows(a) for a in gqa_tabs])
    args = (p3, p3, p3, mla_qn, mla_kvn, wqm, wqr, wkv, *mla_tabs, p3, p3, p3, gqa_qn, gqa_kn, *gqa_tabs)
    assert len(mla_tabs) + 8 == N_MLA_IN and len(gqa_tabs) + 5 == N_GQA_IN
    outs = pl.pallas_call(
        _qkv_prep_kernel,
        grid=(b, t // tm),
        in_specs=in_specs,
        out_specs=[hm(MLA_HEADS, MLA_QK_PAD), hm(MLA_HEADS, MLA_QK_PAD), hmt(MLA_HEADS, MLA_V),
                   hm(GQA_HEADS, GQA_DIM), hm(GQA_KV_HEADS, GQA_DIM), hmt(GQA_KV_HEADS, GQA_DIM)],
        out_shape=[jax.ShapeDtypeStruct((b, MLA_HEADS, t, MLA_QK_PAD), BF16),
                   jax.ShapeDtypeStruct((b, MLA_HEADS, t, MLA_QK_PAD), BF16),
                   jax.ShapeDtypeStruct((b, MLA_HEADS, MLA_V, t), BF16),
                   jax.ShapeDtypeStruct((b, GQA_HEADS, t, GQA_DIM), BF16),
                   jax.ShapeDtypeStruct((b, GQA_KV_HEADS, t, GQA_DIM), BF16),
                   jax.ShapeDtypeStruct((b, GQA_KV_HEADS, GQA_DIM, t), BF16)],
        compiler_params=_params("parallel", "parallel"),
        name="qkv_prep",
    )(*args)
    return outs[:3], outs[3:]


ATTN_KC = 256
ATTN_QB = 128
LOG2E = math.log2(math.e)


def _attn_logits(qts, k_ref, s_buf, nkeys):
    nqb = qts[0].shape[0] // ATTN_QB
    for j, q in enumerate(qts):
        st = lax.dot_general(k_ref[0, j, 0:nkeys, :], q, (((1,), (1,)), ((), ())),
                             preferred_element_type=F32)
        for b in range(nqb):
            s_buf[j * nqb + b, 0:nkeys, :] = st[:, b * ATTN_QB:(b + 1) * ATTN_QB]


def _attn_softmax_slab(s_buf, p_buf, l_buf, b, nkeys):
    parts = [jnp.max(s_buf[b, k0:k0 + ATTN_KC, :].reshape(ATTN_KC // 64, 8, 8, ATTN_QB), axis=0)
             for k0 in range(0, nkeys, ATTN_KC)]
    while len(parts) > 1:
        parts = [jnp.maximum(parts[i], parts[i + 1]) if i + 1 < len(parts) else parts[i]
                 for i in range(0, len(parts), 2)]
    m = jnp.max(parts[0], axis=(0, 1), keepdims=True)[0]
    lacc = jnp.zeros((8, 8, ATTN_QB), F32)
    for k0 in range(0, nkeys, ATTN_KC):
        p = jnp.exp2(s_buf[b, k0:k0 + ATTN_KC, :] - m)
        lacc = lacc + jnp.sum(p.reshape(ATTN_KC // 64, 8, 8, ATTN_QB), axis=0)
        p_buf[b, k0:k0 + ATTN_KC, :] = p.astype(p_buf.dtype)
    l = jnp.sum(lacc, axis=(0, 1), keepdims=True)[0]
    l_buf[b] = jnp.broadcast_to(1.0 / l, (8, ATTN_QB))


def _attn_output(vt_ref, p_buf, l_buf, o_ref, nkeys, hps, group, tq, row0=0):
    dv = vt_ref.shape[2]
    nqb = group * tq // ATTN_QB
    for j in range(hps):
        slabs = range(j * nqb, (j + 1) * nqb)
        pt = jnp.concatenate([p_buf[b, 0:nkeys, :] for b in slabs], axis=1)
        ot = jnp.dot(vt_ref[0, j, :, 0:nkeys], pt, preferred_element_type=F32)
        ot = ot * jnp.concatenate([l_buf[b, 0:1, :] for b in slabs], axis=1)
        for g in range(group):
            c0 = (j * group + g) * dv
            o_ref[0, row0:row0 + tq, c0:c0 + dv] = ot[:, g * tq:(g + 1) * tq].T.astype(o_ref.dtype)


def _attn_queries(q_ref, hps, group, tq, row0=0):
    return [jnp.concatenate([q_ref[0, j * group + g, row0:row0 + tq, :] for g in range(group)], axis=0)
            for j in range(hps)]


def _attn_ctx_kernel(q_ref, k_ref, vt_ref, o_ref, s_sc, p_sc, l_sc, *, group, tq):
    hps, nkeys = k_ref.shape[1], k_ref.shape[2]
    nslab = hps * group * tq // ATTN_QB
    _attn_logits(_attn_queries(q_ref, hps, group, tq), k_ref, s_sc, nkeys)

    def slab(b, carry):
        _attn_softmax_slab(s_sc, p_sc, l_sc, b, nkeys)
        return carry

    lax.fori_loop(0, nslab, slab, 0)
    _attn_output(vt_ref, p_sc, l_sc, o_ref, nkeys, hps, group, tq)


def _attn_lat_kernel(on_ref, q_ref, k_ref, vt_ref, o_ref, s0, s1, p0, p1, l0, l1, *, group, tq, per_step):
    hps, nkeys = k_ref.shape[1], k_ref.shape[2]
    nslab = hps * group * tq // ATTN_QB
    g = pl.program_id(0)

    @pl.when(g == 0)
    def _():
        for buf in (s0, s1, p0, p1, l0, l1):
            buf[...] = jnp.zeros(buf.shape, buf.dtype)

    def turn(row0, s_a, s_b, p_b, p_c, l_b, l_c):
        _attn_logits(_attn_queries(q_ref, hps, group, tq, row0), k_ref, s_a, nkeys)
        for b in range(nslab):
            _attn_softmax_slab(s_b, p_b, l_b, b, nkeys)
        _attn_output(vt_ref, p_c, l_c, o_ref, nkeys, hps, group, tq, row0)

    if per_step == 1:
        pl.when(g % 2 == 0)(lambda: turn(0, s0, s1, p1, p0, l1, l0))
        pl.when(g % 2 == 1)(lambda: turn(0, s1, s0, p0, p1, l0, l1))
    else:
        pl.when(on_ref[0] != 0)(lambda: turn(0, s0, s1, p1, p0, l1, l0))
        pl.when(on_ref[1] != 0)(lambda: turn(tq, s1, s0, p0, p1, l0, l1))


def _attention(q, k, vt, ctx_len, tq, hps, with_ctx_queries, out_dtype):
    b, hq, t, dk = q.shape
    hkv, dv = k.shape[1], vt.shape[2]
    group = hq // hkv
    tq = _tile(ctx_len, tq, 128)
    nslab = hps * group * tq // ATTN_QB
    ctx_tiles = ctx_len // tq
    nq = t // tq - ctx_tiles
    width = hps * group * dv
    assert t % ATTN_KC == 0 and ctx_len % ATTN_KC == 0 and hkv % hps == 0

    def scratch(nkeys):
        return [pltpu.VMEM((nslab, nkeys, ATTN_QB), F32), pltpu.VMEM((nslab, nkeys, ATTN_QB), BF16),
                pltpu.VMEM((nslab, 8, ATTN_QB), F32)]

    s_lat, p_lat, l_lat = scratch(t)
    nh = hkv // hps
    per_step = 2 if (nq % 2 == 0 and ctx_len % (2 * tq) == 0) else 1
    nqs = nq // per_step
    blocks = b * nh * nqs
    lag = 2 // per_step
    ctx_blocks = ctx_len // (per_step * tq)

    def block(step):
        i = jnp.clip(step, 0, blocks - 1)
        return i // (nh * nqs), (i // nqs) % nh, i % nqs

    def q_map(g, on):
        bi, h, qi = block(g)
        return bi, h, qi + ctx_blocks, 0

    def k_map(g, on):
        bi, h, _ = block(g)
        return bi, h, 0, 0

    def v_map(g, on):
        bi, h, _ = block(g - lag)
        return bi, h, 0, 0

    def o_map(g, on):
        bi, h, qi = block(g - lag)
        return bi, qi, h

    y_lat = pl.pallas_call(
        functools.partial(_attn_lat_kernel, group=group, tq=tq, per_step=per_step),
        grid_spec=pltpu.PrefetchScalarGridSpec(
            num_scalar_prefetch=1,
            grid=(blocks + lag,),
            in_specs=[pl.BlockSpec((1, hps * group, per_step * tq, dk), q_map),
                      pl.BlockSpec((1, hps, t, dk), k_map),
                      pl.BlockSpec((1, hps, dv, t), v_map)],
            out_specs=pl.BlockSpec((1, per_step * tq, width), o_map),
            scratch_shapes=[s_lat, s_lat, p_lat, p_lat, l_lat, l_lat]),
        out_shape=jax.ShapeDtypeStruct((b, nq * tq, hq * dv), out_dtype),
        compiler_params=_params("arbitrary"),
        name="attention_lat_dk%d" % dk,
    )(jnp.ones((2,), jnp.int32), q, k, vt)
    if not with_ctx_queries:
        return y_lat
    y_ctx = pl.pallas_call(
        functools.partial(_attn_ctx_kernel, group=group, tq=tq),
        grid=(b, hkv // hps, ctx_tiles),
        in_specs=[pl.BlockSpec((1, hps * group, tq, dk), lambda bi, h, i: (bi, h, i, 0)),
                  pl.BlockSpec((1, hps, ctx_len, dk), lambda bi, h, i: (bi, h, 0, 0)),
                  pl.BlockSpec((1, hps, dv, ctx_len), lambda bi, h, i: (bi, h, 0, 0))],
        out_specs=pl.BlockSpec((1, tq, width), lambda bi, h, i: (bi, i, h)),
        out_shape=jax.ShapeDtypeStruct((b, ctx_len, hq * dv), out_dtype),
        scratch_shapes=scratch(ctx_len),
        compiler_params=_params("parallel", "parallel", "arbitrary"),
        name="attention_ctx_dk%d" % dk,
    )(q, k, vt)
    return jnp.concatenate([y_ctx, y_lat], axis=1)


def _conv_kernel(x_ref, w_ref, b_ref, xt_ref, bc_ref, pad_sc, *, ctx_len, rows, x_blocks):
    t = x_ref.shape[1]
    nch = x_ref.shape[2]
    halo = 8
    segs = ((0, ctx_len), (ctx_len, t))
    zeros = jnp.zeros((halo, nch), F32)
    for si, (lo, hi) in enumerate(segs):
        pad_sc[lo + si * halo:lo + (si + 1) * halo, :] = zeros
        for r0 in range(lo, hi, rows):
            pad_sc[r0 + (si + 1) * halo:r0 + (si + 1) * halo + rows, :] = x_ref[0, r0:r0 + rows, :].astype(F32)
    pad_sc[t + 2 * halo:t + 3 * halo, :] = zeros
    w = w_ref[...]
    bias = b_ref[...]

    def emit(transposed):
        for si, (lo, hi) in enumerate(segs):
            for r0 in range(lo, hi, rows):
                base = r0 + (si + 1) * halo - SSD_CONV // 2
                acc = bias + w[0:1, :] * pad_sc[base:base + rows, :]
                for kk in range(1, SSD_CONV):
                    acc = acc + w[kk:kk + 1, :] * pad_sc[base + kk:base + kk + rows, :]
                y = _silu(acc)
                if transposed:
                    xt_ref[0, :, r0:r0 + rows] = y.T.astype(xt_ref.dtype)
                else:
                    bc_ref[0, r0:r0 + rows, :] = y.astype(bc_ref.dtype)

    is_x = pl.program_id(1) < x_blocks
    pl.when(is_x)(lambda: emit(True))
    pl.when(jnp.logical_not(is_x))(lambda: emit(False))


def _conv(p3, conv_w, conv_b, ctx_len, out_dtype):
    b, t, _ = p3.shape
    nch = 256
    rows = _tile(ctx_len, 256)
    blk0 = P_OFF['xbc'] // nch
    x_blocks = SSD_INNER // nch
    kern = functools.partial(_conv_kernel, ctx_len=ctx_len, rows=rows, x_blocks=x_blocks)
    return pl.pallas_call(
        kern,
        grid=(b, SSD_CONV_DIM // nch),
        in_specs=[pl.BlockSpec((1, t, nch), lambda bi, j: (bi, 0, blk0 + j)),
                  pl.BlockSpec((SSD_CONV, nch), lambda bi, j: (0, j)),
                  pl.BlockSpec((1, nch), lambda bi, j: (0, j))],
        out_specs=[pl.BlockSpec((1, nch, t), lambda bi, j: (bi, jnp.minimum(j, x_blocks - 1), 0)),
                   pl.BlockSpec((1, t, nch), lambda bi, j: (bi, 0, jnp.maximum(j - x_blocks, 0)))],
        out_shape=[jax.ShapeDtypeStruct((b, SSD_INNER, t), out_dtype),
                   jax.ShapeDtypeStruct((b, t, SSD_CONV_DIM - SSD_INNER), out_dtype)],
        scratch_shapes=[pltpu.VMEM((t + 24, nch), F32)],
        compiler_params=_params("parallel", "arbitrary"),
        name="ssd_conv",
    )(p3, conv_w, conv_b.reshape(1, SSD_CONV_DIM))


def _split3(v):
    hi = v.astype(BF16)
    r1 = v - hi.astype(F32)
    mid = r1.astype(BF16)
    lo = (r1 - mid.astype(F32)).astype(BF16)
    return hi, mid, lo


def _expand_rows(v):
    q = v.shape[1]
    return jnp.concatenate([jnp.broadcast_to(v[e:e + 1, :], (SSD_P, q)) for e in range(SSD_E)], axis=0)


def _ssd_decays(dt_col_raw, dt_row_raw, bias_col, bias_row, a_col, a_row, backward):
    q = dt_col_raw.shape[0]
    ri = lax.broadcasted_iota(jnp.int32, (q, q), 0)
    ci = lax.broadcasted_iota(jnp.int32, (q, q), 1)
    tri_col = jnp.where((ri <= ci) if backward else (ri >= ci), 1.0, 0.0).astype(BF16)
    tri_row = jnp.where((ri >= ci) if backward else (ri <= ci), 1.0, 0.0).astype(BF16)
    dt_c = _softplus(dt_col_raw + bias_col)
    dt_r = _softplus(dt_row_raw + bias_row)
    cum_c = sum(jnp.dot(tri_col, part, preferred_element_type=F32) for part in _split3(dt_c * a_col))
    cum_r = sum(jnp.dot(part, tri_row, preferred_element_type=F32) for part in _split3(dt_r * a_row))
    total = jnp.broadcast_to(cum_r[:, 0:1] if backward else cum_r[:, q - 1:q], (SSD_E, q))
    return dt_r, cum_c, cum_r, total


def _ssd_direction(xt, bm, cm, decays, h_ref, backward):
    q = xt.shape[1]
    dt_r, cum_c, cum_r, total = decays
    ri = lax.broadcasted_iota(jnp.int32, (q, q), 0)
    ci = lax.broadcasted_iota(jnp.int32, (q, q), 1)
    keep_t = (ci <= ri) if backward else (ci >= ri)

    bmb = bm.astype(BF16)
    cmb = cm.astype(BF16)
    nt = (((1,), (1,)), ((), ()))
    cbt = lax.dot_general(bmb, cmb, nt, preferred_element_type=F32)
    h = h_ref[...]
    y_off = lax.dot_general(h.astype(BF16), cmb, nt, preferred_element_type=F32) * _expand_rows(jnp.exp(cum_r))
    wgt = (xt * _expand_rows(dt_r * jnp.exp(total - cum_r))).astype(BF16)
    h_ref[...] = _expand_rows(jnp.exp(total)) * h + jnp.dot(wgt, bmb, preferred_element_type=F32)

    xdt = (xt * _expand_rows(dt_r)).astype(BF16)
    parts = []
    for e in range(SSD_E):
        seg = cum_r[e:e + 1, :] - cum_c[:, e:e + 1]
        dec = jnp.exp(jnp.where(keep_t, seg, -jnp.inf))
        parts.append(jnp.dot(xdt[e * SSD_P:(e + 1) * SSD_P, :], (cbt * dec).astype(BF16),
                             preferred_element_type=F32))
    return jnp.concatenate(parts, axis=0) + y_off


def _ssd_kernel(xf_ref, bf_ref, cf_ref, xb_ref, bb_ref, cb_ref, dcf_ref, drf_ref, dcb_ref, drb_ref,
                bias_c_ref, bias_r_ref, a_c_ref, a_r_ref, d_ref, yf_ref, yb_ref, hf_sc, hb_sc):
    @pl.when(pl.program_id(2) == 0)
    def _():
        hf_sc[...] = jnp.zeros(hf_sc.shape, F32)
        hb_sc[...] = jnp.zeros(hb_sc.shape, F32)

    nsub = xf_ref.shape[2] // SSD_Q
    gps = dcf_ref.shape[2]
    chunks = [slice(i * SSD_Q, (i + 1) * SSD_Q) for i in range(nsub)]
    dec_f = [[_ssd_decays(dcf_ref[0, 0, gi, rs, :], drf_ref[0, 0, gi, :, rs], bias_c_ref[0, gi],
                          bias_r_ref[0, gi], a_c_ref[0, gi], a_r_ref[0, gi], False) for rs in chunks]
             for gi in range(gps)]
    dec_b = [[_ssd_decays(dcb_ref[0, 0, gi, rs, :], drb_ref[0, 0, gi, :, rs], bias_c_ref[1, gi],
                          bias_r_ref[1, gi], a_c_ref[1, gi], a_r_ref[1, gi], True) for rs in chunks]
             for gi in range(gps)]
    for gi in range(gps):
        ch = slice(gi * SSD_GW, (gi + 1) * SSD_GW)
        st = slice(gi * SSD_N, (gi + 1) * SSD_N)
        for i in range(nsub):
            rf, rb = chunks[i], chunks[nsub - 1 - i]
            xt = xf_ref[0, ch, rf].astype(F32)
            yf = _ssd_direction(xt, bf_ref[0, rf, st].astype(F32), cf_ref[0, rf, st].astype(F32),
                                dec_f[gi][i], hf_sc.at[ch], False)
            yf_ref[0, ch, rf] = (yf + d_ref[gi] * xt).astype(yf_ref.dtype)
            yb = _ssd_direction(xb_ref[0, ch, rb].astype(F32), bb_ref[0, rb, st].astype(F32),
                                cb_ref[0, rb, st].astype(F32), dec_b[gi][nsub - 1 - i], hb_sc.at[ch], True)
            yb_ref[0, ch, rb] = yb.astype(yb_ref.dtype)


def _ssd(xt, bc, dt_col, dt_row, bias_c, bias_r, a_c, a_r, d_rows, ctx_len, out_dtype):
    b, _, t = xt.shape
    assert SSD_Q == SSD_N
    rows = _tile(ctx_len, SSD_ROWS, SSD_Q)
    nblk = t // rows
    nctx = ctx_len // rows

    def bidx(c):
        return jnp.where(c < nctx, nctx - 1 - c, nblk - 1 - (c - nctx))

    gps = SSD_GPS
    ngs = SSD_G // gps
    fx = lambda bi, g, c: (bi, g, c)
    fb = lambda bi, g, c: (bi, c, g)
    fc = lambda bi, g, c: (bi, c, ngs + g)
    bx = lambda bi, g, c: (bi, g, bidx(c))
    bb = lambda bi, g, c: (bi, bidx(c), g)
    bcm = lambda bi, g, c: (bi, bidx(c), ngs + g)
    small = lambda a: pl.BlockSpec((2, gps) + a.shape[2:], lambda bi, g, c: (0, g, 0, 0))
    return pl.pallas_call(
        _ssd_kernel,
        grid=(b, ngs, nblk),
        in_specs=[pl.BlockSpec((1, gps * SSD_GW, rows), fx), pl.BlockSpec((1, rows, gps * SSD_N), fb),
                  pl.BlockSpec((1, rows, gps * SSD_N), fc),
                  pl.BlockSpec((1, gps * SSD_GW, rows), bx), pl.BlockSpec((1, rows, gps * SSD_N), bb),
                  pl.BlockSpec((1, rows, gps * SSD_N), bcm),
                  pl.BlockSpec((1, 1, gps, rows, SSD_E), lambda bi, g, c: (bi, 0, g, c, 0)),
                  pl.BlockSpec((1, 1, gps, SSD_E, rows), lambda bi, g, c: (bi, 0, g, 0, c)),
                  pl.BlockSpec((1, 1, gps, rows, SSD_E), lambda bi, g, c: (bi, 1, g, bidx(c), 0)),
                  pl.BlockSpec((1, 1, gps, SSD_E, rows), lambda bi, g, c: (bi, 1, g, 0, bidx(c))),
                  small(bias_c), small(bias_r), small(a_c), small(a_r),
                  pl.BlockSpec((gps, SSD_GW, SSD_Q), lambda bi, g, c: (g, 0, 0))],
        out_specs=[pl.BlockSpec((1, gps * SSD_GW, rows), fx), pl.BlockSpec((1, gps * SSD_GW, rows), bx)],
        out_shape=[jax.ShapeDtypeStruct((b, SSD_INNER, t), out_dtype)] * 2,
        scratch_shapes=[pltpu.VMEM((gps * SSD_GW, SSD_N), F32), pltpu.VMEM((gps * SSD_GW, SSD_N), F32)],
        compiler_params=_params("parallel", "parallel", "arbitrary"),
        name="ssd_scan",
    )(xt, bc, bc, xt, bc, bc, dt_col, dt_row, dt_col, dt_row, bias_c, bias_r, a_c, a_r, d_rows)


MERGE_TN = 2048


def _merge_kernel(ya_ref, ga_ref, yb_ref, gb_ref, yf_ref, ybk_ref, z_ref, nrm_ref,
                  mg_ref, wa_ref, wb_ref, wc_ref, u_ref, c_sc):
    a_in = (ya_ref[0].astype(F32) * _silu(ga_ref[0].astype(F32))).astype(BF16)
    b_in = (yb_ref[0].astype(F32) * _silu(gb_ref[0].astype(F32))).astype(BF16)
    v = (yf_ref[0].astype(F32) + ybk_ref[0].astype(F32)).T * _silu(z_ref[0].astype(F32))
    for g in range(SSD_G):
        sl = slice(g * SSD_GW, (g + 1) * SSD_GW)
        c_sc[:, sl] = (_rms(v[:, sl]) * nrm_ref[:, sl]).astype(BF16)
    c_in = c_sc[...]
    d = u_ref.shape[-1]
    for j in range(0, d, MERGE_TN):
        sl = slice(j, j + MERGE_TN)
        br_a = jnp.dot(a_in, wa_ref[:, sl], preferred_element_type=F32)
        br_b = jnp.dot(b_in, wb_ref[:, sl], preferred_element_type=F32)
        br_c = jnp.dot(c_in, wc_ref[:, sl], preferred_element_type=F32)
        u = (jax.nn.sigmoid(mg_ref[0, :, j:j + MERGE_TN].astype(F32)) * br_a
             + jax.nn.sigmoid(mg_ref[0, :, d + j:d + j + MERGE_TN].astype(F32)) * br_b
             + jax.nn.sigmoid(mg_ref[0, :, 2 * d + j:2 * d + j + MERGE_TN].astype(F32)) * br_c)
        u_ref[0, :, sl] = u.astype(u_ref.dtype)


def _merge(ya, yb, yf, ybk, p3, ssd_norm, wa, wb, wc, row_off):
    b, rows, _ = ya.shape
    tm = _tile(rows, 256)
    d = D_MODEL
    ro = row_off // tm
    assert P_OFF['mg'] == 0

    def col(name, width):
        blk = P_OFF[name] // width
        return pl.BlockSpec((1, tm, width), lambda bi, i: (bi, i + ro, blk))

    loc = lambda w: pl.BlockSpec((1, tm, w), lambda bi, i: (bi, i, 0))
    glob = lambda w: pl.BlockSpec((1, tm, w), lambda bi, i: (bi, i + ro, 0))
    globt = pl.BlockSpec((1, SSD_INNER, tm), lambda bi, i: (bi, 0, i + ro))
    wspec = lambda k: pl.BlockSpec((k, d), lambda bi, i: (0, 0), pipeline_mode=pl.Buffered(1))
    return pl.pallas_call(
        _merge_kernel,
        grid=(b, rows // tm),
        in_specs=[loc(MLA_WIDTH), col('ga', MLA_WIDTH), loc(GQA_WIDTH), col('gb', GQA_WIDTH),
                  globt, globt, col('z', SSD_INNER),
                  pl.BlockSpec((1, SSD_INNER), lambda bi, i: (0, 0)),
                  glob(N_BRANCH * d), wspec(MLA_WIDTH), wspec(GQA_WIDTH), wspec(SSD_INNER)],
        out_specs=pl.BlockSpec((1, tm, d), lambda bi, i: (bi, i, 0)),
        out_shape=jax.ShapeDtypeStruct((b, rows, d), BF16),
        scratch_shapes=[pltpu.VMEM((tm, SSD_INNER), BF16)],
        compiler_params=_params("parallel", "parallel"),
        name="merge",
    )(ya, p3, yb, p3, yf, ybk, p3, ssd_norm.reshape(1, SSD_INNER), p3, wa, wb, wc)


def _out_ln_kernel(u_ref, w_ref, *rest, with_next, ctx_tiles, ro, split):
    if split:
        ctx_ref, x_ref, gate_ref, g_ref, b_ref, *rest = rest
        res = _residual_rows(ctx_ref, x_ref, ctx_tiles, ro)
    else:
        x_ref, gate_ref, g_ref, b_ref, *rest = rest
        res = x_ref[0]
    out = jnp.dot(u_ref[0], w_ref[...], preferred_element_type=F32)
    r = DEEPNORM_ALPHA * res + gate_ref[0] * out
    xn = _layer_norm(r) * g_ref[...] + b_ref[...]
    if with_next:
        sh_ref, sc_ref, xo_ref, xm_ref = rest
        xo_ref[0] = xn
        xm_ref[0] = (_layer_norm(xn) * (1.0 + sc_ref[0]) + sh_ref[0]).astype(xm_ref.dtype)
    else:
        (xo_ref,) = rest
        xo_ref[0] = xn


def _out_ln(u, w_out, xc, tab, ln_g, ln_b, next_tab, nb, ctx_len, row_off):
    b, rows, d = u.shape
    tm = _tile(ctx_len, 256)
    ro = row_off // tm
    ctx_tiles = ctx_len // tm
    row = _mod_row_index(nb, ctx_tiles)
    with_next = next_tab is not None
    split = isinstance(xc, tuple)
    loc = pl.BlockSpec((1, tm, d), lambda bi, i: (bi, i, 0))
    vec = pl.BlockSpec((1, d), lambda bi, i: (0, 0))
    res_specs = (_residual_specs(tm, d, ctx_tiles, ro) if split
                 else [pl.BlockSpec((1, tm, d), lambda bi, i: (bi, i + ro, 0))])
    in_specs = [loc, pl.BlockSpec((d, d), lambda bi, i: (0, 0))] + res_specs + [
        pl.BlockSpec((1, 1, d), lambda bi, i: (row(bi, i + ro), 0, 2)), vec, vec]
    args = [u, w_out] + (list(xc) if split else [xc]) + [tab, ln_g.reshape(1, d), ln_b.reshape(1, d)]
    out_specs = [loc]
    out_shape = [jax.ShapeDtypeStruct((b, rows, d), F32)]
    if with_next:
        in_specs += [pl.BlockSpec((1, 1, d), lambda bi, i: (row(bi, i + ro), 0, 0)),
                     pl.BlockSpec((1, 1, d), lambda bi, i: (row(bi, i + ro), 0, 1))]
        args += [next_tab, next_tab]
        out_specs.append(loc)
        out_shape.append(jax.ShapeDtypeStruct((b, rows, d), BF16))
    return pl.pallas_call(
        functools.partial(_out_ln_kernel, with_next=with_next, ctx_tiles=ctx_tiles, ro=ro, split=split),
        grid=(b, rows // tm),
        in_specs=in_specs,
        out_specs=out_specs,
        out_shape=out_shape,
        compiler_params=_params("parallel", "parallel"),
        name="out_ln",
    )(*args)


def _rope_angles(rows, dim):
    row, col = jnp.meshgrid(jnp.arange(rows, dtype=F32), jnp.arange(GRID_W, dtype=F32), indexing='ij')
    half = dim // 2
    inv_freq = ROPE_THETA ** (-jnp.arange(0, half, 2, dtype=F32) / half)
    ang_r = row.reshape(-1, 1) * inv_freq
    ang_c = col.reshape(-1, 1) * inv_freq
    return jnp.concatenate([ang_r, ang_r, ang_c, ang_c], axis=-1)


def _rope_tables(seq, ctx_len, dim):
    ang = _rope_angles(seq // GRID_W, dim)
    cos = jnp.concatenate([jnp.ones((ctx_len, dim), F32), jnp.cos(ang)], axis=0)
    sin = jnp.concatenate([jnp.zeros((ctx_len, dim), F32), jnp.sin(ang)], axis=0)
    return cos, sin


def _roll_tables(cos, sin, dim, scale):
    t = cos.shape[0]
    quarter = dim // 4
    first = (jnp.arange(dim) % (2 * quarter)) < quarter
    s1 = jnp.where(first, -sin, 0.0)
    s2 = jnp.where(first, 0.0, sin)
    pad = lambda a: jnp.pad(a * scale, ((0, 0), (0, 128 - dim)))
    return pad(cos), pad(s1), pad(s2)


def _rot_matrix(dim):
    quarter = dim // 4
    r = np.zeros((dim, dim), np.float32)
    for i in range(dim):
        blk = i // quarter
        if blk % 2 == 0:
            r[i + quarter, i] = -1.0
        else:
            r[i - quarter, i] = 1.0
    return jnp.asarray(r)


def _mla_q_weights(w_uq):
    w = w_uq.reshape(MLA_Q_LORA, MLA_HEADS, MLA_QK)
    nope, pe = w[..., :MLA_NOPE], w[..., MLA_NOPE:]
    pe_rot = jnp.einsum('khd,de->khe', pe, _rot_matrix(MLA_ROPE), precision=HIGHEST)
    zpad = jnp.zeros((MLA_Q_LORA, MLA_HEADS, MLA_QK_PAD - MLA_QK), F32)
    main = jnp.concatenate([nope, pe, zpad], axis=-1)
    rot = jnp.concatenate([jnp.zeros_like(nope), pe_rot, zpad], axis=-1)
    return (main.reshape(MLA_Q_LORA, -1).astype(BF16), rot.reshape(MLA_Q_LORA, -1).astype(BF16))


def _permute_w_in(w):
    parts = [w[:, IN_OFFSETS[n]:IN_OFFSETS[n] + IN_WIDTHS[n]] for n in P_ORDER]
    parts.append(jnp.zeros((w.shape[0], P_WIDTH - P_USED), w.dtype))
    return jnp.concatenate(parts, axis=1).astype(BF16)


P_DTYPE = BF16
Y_DTYPE = BF16


def kernel(x, c, ctx, c_ctx, w_mod, b_mod, w_in, mla_q_norm, mla_w_uq, mla_kv_norm, mla_w_ukv,
           gqa_q_norm, gqa_k_norm, ssd_conv_w, ssd_conv_b, ssd_a_log, ssd_dt_bias, ssd_d, ssd_norm,
           w_br_a, w_br_b, w_br_c, w_out, ln_g, ln_b):
    nb, seq, d = x.shape
    ctx_len = ctx.shape[1]
    t = ctx_len + seq
    depth = w_in.shape[0]
    assert d == D_MODEL and nb < 8 and seq % GRID_W == 0
    assert ctx_len % SSD_Q == 0 and seq % SSD_Q == 0

    cos_a, sin_a = _rope_tables(seq, ctx_len, MLA_ROPE)
    cos_b, sin_b = _rope_tables(seq, ctx_len, GQA_DIM)
    sq = MLA_QK ** -0.5 * LOG2E
    zq = jnp.zeros((t, MLA_QK_PAD - MLA_QK), F32)
    cosq = jnp.concatenate([jnp.full((t, MLA_NOPE), sq, F32), cos_a * sq, zq], axis=1)
    sinq = jnp.concatenate([jnp.zeros((t, MLA_NOPE), F32), sin_a * sq, zq], axis=1)
    mla_tabs = (cosq, sinq) + _roll_tables(cos_a, sin_a, MLA_ROPE, 1.0)
    gqa_tabs = (_roll_tables(cos_b, sin_b, GQA_DIM, GQA_DIM ** -0.5 * LOG2E)
                + _roll_tables(cos_b, sin_b, GQA_DIM, 1.0))

    c_rows = jnp.zeros((8, d), F32).at[:nb].set(c).at[nb].set(c_ctx)
    tabs = [_mod_rows(c_rows, w_mod[l], b_mod[l]).reshape(8, 1, 3 * d) for l in range(depth)]

    xc = (ctx, x)
    xm = _ln_mod(ctx, x, tabs[0], nb)

    for l in range(depth):
        last = l == depth - 1
        wp = _permute_w_in(w_in[l])
        p2, krdt = _in_proj(xm.reshape(nb * t, d), wp, P_DTYPE)
        p3 = p2.reshape(nb, t, P_WIDTH)

        wqm, wqr = _mla_q_weights(mla_w_uq[l])
        (qa, ka, va), (qb, kb, vb) = _qkv_prep(
            p3, mla_q_norm[l].reshape(1, -1), mla_kv_norm[l].reshape(1, -1), wqm, wqr,
            mla_w_ukv[l].astype(BF16), mla_tabs, gqa_q_norm[l].reshape(1, -1),
            gqa_k_norm[l].reshape(1, -1), gqa_tabs, ctx_len)
        ya = _attention(qa, ka, va, ctx_len, 256, 2, not last, Y_DTYPE)
        yb = _attention(qb, kb, vb, ctx_len, 128, 1, not last, Y_DTYPE)

        xconv_t, bconv = _conv(p3, ssd_conv_w[l], ssd_conv_b[l], ctx_len, Y_DTYPE)
        dt5 = krdt[:, 64:].reshape(nb, t, 2, SSD_G, SSD_E)
        dt_col = jnp.transpose(dt5, (0, 2, 3, 1, 4))
        dt_row = jnp.transpose(dt5, (0, 2, 3, 4, 1))
        bias = ssd_dt_bias[l].astype(F32).reshape(2, SSD_G, SSD_E)
        a = -jnp.exp(ssd_a_log[l].astype(F32)).reshape(2, SSD_G, SSD_E)
        d_rows = jnp.broadcast_to(jnp.repeat(ssd_d[l].astype(F32), SSD_P).reshape(SSD_G, SSD_GW, 1),
                                  (SSD_G, SSD_GW, SSD_Q))
        yf, ybk = _ssd(xconv_t, bconv, dt_col, dt_row, bias[:, :, None, :], bias[:, :, :, None],
                       a[:, :, None, :], a[:, :, :, None], d_rows, ctx_len, Y_DTYPE)

        row_off = ctx_len if last else 0
        u = _merge(ya, yb, yf, ybk, p3, ssd_norm[l], w_br_a[l].astype(BF16), w_br_b[l].astype(BF16),
                   w_br_c[l].astype(BF16), row_off)
        if last:
            (xo,) = _out_ln(u, w_out[l].astype(BF16), xc, tabs[l], ln_g[l], ln_b[l], None, nb,
                            ctx_len, row_off)
            return xo
        xc, xm = _out_ln(u, w_out[l].astype(BF16), xc, tabs[l], ln_g[l], ln_b[l], tabs[l + 1], nb,
                         ctx_len, row_off)
```

```python
import functools
import math

import numpy as np
import jax
import jax.numpy as jnp
from jax import lax
from jax.experimental import pallas as pl
from jax.experimental.pallas import tpu as pltpu

F32 = jnp.float32
BF16 = jnp.bfloat16
HIGHEST = lax.Precision.HIGHEST

D_MODEL = 2048
DEPTH = 2
GRID_W = 64
ROPE_THETA = 10000.0
EPS = 1e-6

MLA_HEADS = 8
MLA_Q_LORA = 512
MLA_KV_LORA = 256
MLA_NOPE = 128
MLA_ROPE = 64
MLA_V = 128
MLA_QK = MLA_NOPE + MLA_ROPE
MLA_QK_PAD = 256
MLA_WIDTH = MLA_HEADS * MLA_V

GQA_HEADS = 8
GQA_KV_HEADS = 2
GQA_GROUP = GQA_HEADS // GQA_KV_HEADS
GQA_DIM = 128
GQA_WIDTH = GQA_HEADS * GQA_DIM
GQA_KV_WIDTH = GQA_KV_HEADS * GQA_DIM

SSD_INNER = D_MODEL
SSD_P = 64
SSD_HEADS = SSD_INNER // SSD_P
SSD_G = 4
SSD_E = SSD_HEADS // SSD_G
SSD_N = 128
SSD_CONV = 5
SSD_Q = 128
SSD_ROWS = 256
SSD_GPS = 4
SSD_GW = SSD_E * SSD_P
SSD_CONV_DIM = SSD_INNER + 2 * SSD_G * SSD_N

N_BRANCH = 3
IN_SPLITS = (MLA_Q_LORA, MLA_KV_LORA, MLA_ROPE, MLA_WIDTH, GQA_WIDTH, GQA_KV_WIDTH, GQA_KV_WIDTH,
             GQA_WIDTH, SSD_INNER, SSD_CONV_DIM, 2 * SSD_HEADS, N_BRANCH * D_MODEL)
IN_NAMES = ('cq', 'ckv', 'kr', 'ga', 'gq', 'gk', 'gv', 'gb', 'z', 'xbc', 'dtr', 'mg')
IN_OFFSETS = dict(zip(IN_NAMES, np.concatenate([[0], np.cumsum(IN_SPLITS)[:-1]]).tolist()))
IN_WIDTHS = dict(zip(IN_NAMES, IN_SPLITS))
P_ORDER = ('mg', 'z', 'xbc', 'ga', 'gq', 'gb', 'cq', 'ckv', 'gk', 'gv', 'kr', 'dtr')
P_OFF = {}
_o = 0
for _n in P_ORDER:
    P_OFF[_n] = _o
    _o += IN_WIDTHS[_n]
P_USED = _o
P_TN = 512
IN_TM = 4352
P_WIDTH = -(-P_USED // P_TN) * P_TN

DEEPNORM_ALPHA = (2 * DEPTH) ** 0.25

VMEM_LIMIT = 56 * 2 ** 20
LANE = 128
SUBLANE = 8
ROW_TILE = 256


def _params(*sem):
    return pltpu.CompilerParams(dimension_semantics=sem, vmem_limit_bytes=VMEM_LIMIT)


def _tile(n, target, align=8):
    t = min(n, target)
    while t > align and (n % t or t % align):
        t -= align
    assert n % t == 0, (n, target)
    return t


def _silu(v):
    return v * jax.nn.sigmoid(v)


def _softplus(v):
    return jnp.maximum(v, 0.0) + jnp.log1p(jnp.exp(-jnp.abs(v)))


def _layer_norm(v):
    mu = jnp.mean(v, axis=-1, keepdims=True)
    vc = v - mu
    var = jnp.mean(vc * vc, axis=-1, keepdims=True)
    return vc * lax.rsqrt(var + EPS)


def _rms(v):
    return v * lax.rsqrt(jnp.mean(v * v, axis=-1, keepdims=True) + EPS)


def _mod_kernel(c_ref, w_ref, b_ref, o_ref):
    a = _silu(c_ref[...]).astype(BF16)
    o_ref[...] = jnp.dot(a, w_ref[...].astype(BF16), preferred_element_type=F32) + b_ref[...]


def _mod_rows(c_rows, w_mod, b_mod):
    r, d = c_rows.shape
    n = w_mod.shape[1]
    tn = 512
    return pl.pallas_call(
        _mod_kernel,
        grid=(n // tn,),
        in_specs=[pl.BlockSpec((r, d), lambda j: (0, 0)),
                  pl.BlockSpec((d, tn), lambda j: (0, j)),
                  pl.BlockSpec((1, tn), lambda j: (0, j))],
        out_specs=pl.BlockSpec((r, tn), lambda j: (0, j)),
        out_shape=jax.ShapeDtypeStruct((r, n), F32),
        compiler_params=_params("arbitrary"),
        name="mod_rows",
    )(c_rows, w_mod, b_mod.reshape(1, n))


def _residual_rows(ctx_ref, x_ref, ctx_tiles, ro):
    return jnp.where(pl.program_id(1) + ro < ctx_tiles, ctx_ref[0], x_ref[0])


def _residual_specs(tm, d, ctx_tiles, ro):
    return [pl.BlockSpec((1, tm, d), lambda bi, i: (bi, jnp.minimum(i + ro, ctx_tiles - 1), 0)),
            pl.BlockSpec((1, tm, d), lambda bi, i: (bi, jnp.maximum(i + ro - ctx_tiles, 0), 0))]


def _ln_mod_kernel(ctx_ref, x_ref, sh_ref, sc_ref, o_ref, *, ctx_tiles):
    y = _layer_norm(_residual_rows(ctx_ref, x_ref, ctx_tiles, 0))
    o_ref[0] = (y * (1.0 + sc_ref[0]) + sh_ref[0]).astype(o_ref.dtype)


def _mod_row_index(nb, ctx_tiles):
    return lambda b, i: jnp.where(i < ctx_tiles, nb, b)


def _ln_mod(ctx, x, tab, nb):
    b, seq, d = x.shape
    ctx_len = ctx.shape[1]
    tm = _tile(ctx_len, ROW_TILE)
    ctx_tiles = ctx_len // tm
    row = _mod_row_index(nb, ctx_tiles)
    return pl.pallas_call(
        functools.partial(_ln_mod_kernel, ctx_tiles=ctx_tiles),
        grid=(b, (ctx_len + seq) // tm),
        in_specs=_residual_specs(tm, d, ctx_tiles, 0)
        + [pl.BlockSpec((1, 1, d), lambda bi, i: (row(bi, i), 0, 0)),
           pl.BlockSpec((1, 1, d), lambda bi, i: (row(bi, i), 0, 1))],
        out_specs=pl.BlockSpec((1, tm, d), lambda bi, i: (bi, i, 0)),
        out_shape=jax.ShapeDtypeStruct((b, ctx_len + seq, d), BF16),
        compiler_params=_params("parallel", "parallel"),
        name="ln_mod",
    )(ctx, x, tab, tab)


def _in_proj_kernel(x_ref, w_ref, o_ref, dt_ref, *, dt_tile, dt_col):
    acc = jnp.dot(x_ref[...], w_ref[...], preferred_element_type=F32)
    o_ref[...] = acc.astype(o_ref.dtype)

    @pl.when(pl.program_id(1) == dt_tile)
    def _():
        dt_ref[...] = acc[:, dt_col:dt_col + LANE]


def _in_proj(xm2, wp, out_dtype):
    m, k = xm2.shape
    n = wp.shape[1]
    tm = _tile(m, IN_TM)
    tn = P_TN
    dt_tile, dt_col = divmod(P_OFF['kr'], tn)
    assert P_OFF['dtr'] == P_OFF['kr'] + MLA_ROPE and dt_col % LANE == 0
    return pl.pallas_call(
        functools.partial(_in_proj_kernel, dt_tile=dt_tile, dt_col=dt_col),
        grid=(m // tm, n // tn),
        in_specs=[pl.BlockSpec((tm, k), lambda i, j: (i, 0), pipeline_mode=pl.Buffered(1)),
                  pl.BlockSpec((k, tn), lambda i, j: (0, j))],
        out_specs=[pl.BlockSpec((tm, tn), lambda i, j: (i, j)),
                   pl.BlockSpec((tm, LANE), lambda i, j: (i, 0))],
        out_shape=[jax.ShapeDtypeStruct((m, n), out_dtype), jax.ShapeDtypeStruct((m, LANE), F32)],
        compiler_params=_params("parallel", "arbitrary"),
        name="in_proj",
    )(xm2, wp)


def _mla_prep_kernel(cq_ref, ckv_ref, kr_ref, qn_ref, kvn_ref, wqm_ref, wqr_ref, wkv_ref,
                     cosq_ref, sinq_ref, ck_ref, s1_ref, s2_ref, qa_ref, ka_ref, va_ref):
    cqn = (_rms(cq_ref[0].astype(F32)) * qn_ref[...]).astype(BF16)
    qm = jnp.dot(cqn, wqm_ref[...], preferred_element_type=F32)
    qr = jnp.dot(cqn, wqr_ref[...], preferred_element_type=F32)
    cosq = cosq_ref[...]
    sinq = sinq_ref[...]
    for h in range(MLA_HEADS):
        sl = slice(h * MLA_QK_PAD, (h + 1) * MLA_QK_PAD)
        qa_ref[0, h] = (qm[:, sl] * cosq + qr[:, sl] * sinq).astype(qa_ref.dtype)
    ckvn = (_rms(ckv_ref[0].astype(F32)) * kvn_ref[...]).astype(BF16)
    kv = jnp.dot(ckvn, wkv_ref[...], preferred_element_type=F32)
    kr = kr_ref[0].astype(F32)
    quarter = MLA_ROPE // 4
    kpe = (kr * ck_ref[...] + pltpu.roll(kr, LANE - quarter, axis=1) * s1_ref[...]
           + pltpu.roll(kr, quarter, axis=1) * s2_ref[...]).astype(ka_ref.dtype)
    for h in range(MLA_HEADS):
        base = h * (MLA_NOPE + MLA_V)
        ka_ref[0, h, :, 0:MLA_NOPE] = kv[:, base:base + MLA_NOPE].astype(ka_ref.dtype)
        ka_ref[0, h, :, MLA_NOPE:MLA_QK_PAD] = kpe
        va_ref[0, h] = kv[:, base + MLA_NOPE:base + MLA_NOPE + MLA_V].T.astype(va_ref.dtype)


def _rope128(y, c, s1, s2):
    quarter = GQA_DIM // 4
    return y * c + pltpu.roll(y, LANE - quarter, axis=1) * s1 + pltpu.roll(y, quarter, axis=1) * s2


def _gqa_prep_kernel(gq_ref, gk_ref, gv_ref, qn_ref, kn_ref, cq_ref, s1q_ref, s2q_ref,
                     ck_ref, s1k_ref, s2k_ref, qb_ref, kb_ref, vb_ref):
    gq = gq_ref[0].astype(F32)
    for h in range(GQA_HEADS):
        y = _rms(gq[:, h * GQA_DIM:(h + 1) * GQA_DIM]) * qn_ref[...]
        qb_ref[0, h] = _rope128(y, cq_ref[...], s1q_ref[...], s2q_ref[...]).astype(qb_ref.dtype)
    gk = gk_ref[0].astype(F32)
    gv = gv_ref[0].astype(F32)
    for h in range(GQA_KV_HEADS):
        y = _rms(gk[:, h * GQA_DIM:(h + 1) * GQA_DIM]) * kn_ref[...]
        kb_ref[0, h] = _rope128(y, ck_ref[...], s1k_ref[...], s2k_ref[...]).astype(kb_ref.dtype)
        vb_ref[0, h] = gv[:, h * GQA_DIM:(h + 1) * GQA_DIM].T.astype(vb_ref.dtype)


N_MLA_IN, N_GQA_IN = 13, 11


def _qkv_prep_kernel(*refs):
    mla_in, gqa_in = refs[:N_MLA_IN], refs[N_MLA_IN:N_MLA_IN + N_GQA_IN]
    outs = refs[N_MLA_IN + N_GQA_IN:]
    _mla_prep_kernel(*mla_in, *outs[:3])
    _gqa_prep_kernel(*gqa_in, *outs[3:])


def _qkv_prep(p3, mla_qn, mla_kvn, wqm, wqr, wkv, mla_tabs, gqa_qn, gqa_kn, gqa_tabs, ctx_len):
    b, t, _ = p3.shape
    tm = _tile(ctx_len, ROW_TILE)

    def col(name, width):
        blk = P_OFF[name] // width
        return pl.BlockSpec((1, tm, width), lambda bi, i: (bi, i, blk))

    def full(a):
        return pl.BlockSpec(a.shape, lambda bi, i: (0,) * a.ndim)

    def rows(a):
        return pl.BlockSpec((tm, a.shape[1]), lambda bi, i: (i, 0))

    hm = lambda nh, w: pl.BlockSpec((1, nh, tm, w), lambda bi, i: (bi, 0, i, 0))
    hmt = lambda nh, w: pl.BlockSpec((1, nh, w, tm), lambda bi, i: (bi, 0, 0, i))
    in_specs = ([col('cq', MLA_Q_LORA), col('ckv', MLA_KV_LORA), col('kr', LANE),
                 full(mla_qn), full(mla_kvn), full(wqm), full(wqr), full(wkv)]
                + [rows(a) for a in mla_tabs]
                + [col('gq', GQA_WIDTH), col('gk', GQA_KV_WIDTH), col('gv', GQA_KV_WIDTH),
                   full(gqa_qn), full(gqa_kn)] + [rows(a) for a in gqa_tabs])
    args = (p3, p3, p3, mla_qn, mla_kvn, wqm, wqr, wkv, *mla_tabs, p3, p3, p3, gqa_qn, gqa_kn, *gqa_tabs)
    assert len(mla_tabs) + 8 == N_MLA_IN and len(gqa_tabs) + 5 == N_GQA_IN
    outs = pl.pallas_call(
        _qkv_prep_kernel,
        grid=(b, t // tm),
        in_specs=in_specs,
        out_specs=[hm(MLA_HEADS, MLA_QK_PAD), hm(MLA_HEADS, MLA_QK_PAD), hmt(MLA_HEADS, MLA_V),
                   hm(GQA_HEADS, GQA_DIM), hm(GQA_KV_HEADS, GQA_DIM), hmt(GQA_KV_HEADS, GQA_DIM)],
        out_shape=[jax.ShapeDtypeStruct((b, MLA_HEADS, t, MLA_QK_PAD), BF16),
                   jax.ShapeDtypeStruct((b, MLA_HEADS, t, MLA_QK_PAD), BF16),
                   jax.ShapeDtypeStruct((b, MLA_HEADS, MLA_V, t), BF16),
                   jax.ShapeDtypeStruct((b, GQA_HEADS, t, GQA_DIM), BF16),
                   jax.ShapeDtypeStruct((b, GQA_KV_HEADS, t, GQA_DIM), BF16),
                   jax.ShapeDtypeStruct((b, GQA_KV_HEADS, GQA_DIM, t), BF16)],
        compiler_params=_params("parallel", "parallel"),
        name="qkv_prep",
    )(*args)
    return outs[:3], outs[3:]


ATTN_KC = 256
ATTN_QB = LANE
MLA_ATTN_STEP = (256, 2)
GQA_ATTN_STEP = (128, 1)
LOG2E = math.log2(math.e)


def _attn_logits(qts, k_ref, s_buf, nkeys):
    nqb = qts[0].shape[0] // ATTN_QB
    for j, q in enumerate(qts):
        st = lax.dot_general(k_ref[0, j, 0:nkeys, :], q, (((1,), (1,)), ((), ())),
                             preferred_element_type=F32)
        for b in range(nqb):
            s_buf[j * nqb + b, 0:nkeys, :] = st[:, b * ATTN_QB:(b + 1) * ATTN_QB]


def _attn_softmax_slab(s_buf, p_buf, l_buf, b, nkeys):
    parts = [jnp.max(s_buf[b, k0:k0 + ATTN_KC, :].reshape(ATTN_KC // 64, 8, 8, ATTN_QB), axis=0)
             for k0 in range(0, nkeys, ATTN_KC)]
    while len(parts) > 1:
        parts = [jnp.maximum(parts[i], parts[i + 1]) if i + 1 < len(parts) else parts[i]
                 for i in range(0, len(parts), 2)]
    m = jnp.max(parts[0], axis=(0, 1), keepdims=True)[0]
    lacc = jnp.zeros((8, 8, ATTN_QB), F32)
    for k0 in range(0, nkeys, ATTN_KC):
        p = jnp.exp2(s_buf[b, k0:k0 + ATTN_KC, :] - m)
        lacc = lacc + jnp.sum(p.reshape(ATTN_KC // 64, 8, 8, ATTN_QB), axis=0)
        p_buf[b, k0:k0 + ATTN_KC, :] = p.astype(p_buf.dtype)
    l = jnp.sum(lacc, axis=(0, 1), keepdims=True)[0]
    l_buf[b] = jnp.broadcast_to(1.0 / l, (8, ATTN_QB))


def _attn_output(vt_ref, p_buf, l_buf, o_ref, nkeys, hps, group, tq):
    dv = vt_ref.shape[2]
    nqb = group * tq // ATTN_QB
    for j in range(hps):
        slabs = range(j * nqb, (j + 1) * nqb)
        pt = jnp.concatenate([p_buf[b, 0:nkeys, :] for b in slabs], axis=1)
        ot = jnp.dot(vt_ref[0, j, :, 0:nkeys], pt, preferred_element_type=F32)
        ot = ot * jnp.concatenate([l_buf[b, 0:1, :] for b in slabs], axis=1)
        for g in range(group):
            c0 = (j * group + g) * dv
            o_ref[0, :, c0:c0 + dv] = ot[:, g * tq:(g + 1) * tq].T.astype(o_ref.dtype)


def _attn_queries(q_ref, hps, group):
    return [jnp.concatenate([q_ref[0, j * group + g] for g in range(group)], axis=0)
            for j in range(hps)]


def _attn_ctx_kernel(q_ref, k_ref, vt_ref, o_ref, s_sc, p_sc, l_sc, *, group, tq):
    hps, nkeys = k_ref.shape[1], k_ref.shape[2]
    nslab = hps * group * tq // ATTN_QB
    _attn_logits(_attn_queries(q_ref, hps, group), k_ref, s_sc, nkeys)

    def slab(b, carry):
        _attn_softmax_slab(s_sc, p_sc, l_sc, b, nkeys)
        return carry

    lax.fori_loop(0, nslab, slab, 0)
    _attn_output(vt_ref, p_sc, l_sc, o_ref, nkeys, hps, group, tq)


def _attn_lat_kernel(q_ref, k_ref, vt_ref, o_ref, s0, s1, p0, p1, l0, l1, *, group, tq):
    hps, nkeys = k_ref.shape[1], k_ref.shape[2]
    nslab = hps * group * tq // ATTN_QB
    g = pl.program_id(0)

    @pl.when(g == 0)
    def _():
        for buf in (s0, s1, p0, p1, l0, l1):
            buf[...] = jnp.zeros(buf.shape, buf.dtype)

    def step(s_a, s_b, p_b, p_c, l_b, l_c):
        _attn_logits(_attn_queries(q_ref, hps, group), k_ref, s_a, nkeys)
        for b in range(nslab):
            _attn_softmax_slab(s_b, p_b, l_b, b, nkeys)
        _attn_output(vt_ref, p_c, l_c, o_ref, nkeys, hps, group, tq)

    pl.when(g % 2 == 0)(lambda: step(s0, s1, p1, p0, l1, l0))
    pl.when(g % 2 == 1)(lambda: step(s1, s0, p0, p1, l0, l1))


def _attention(q, k, vt, ctx_len, tq, hps, with_ctx_queries, out_dtype):
    b, hq, t, dk = q.shape
    hkv, dv = k.shape[1], vt.shape[2]
    group = hq // hkv
    tq = _tile(ctx_len, tq, LANE)
    nslab = hps * group * tq // ATTN_QB
    ctx_tiles = ctx_len // tq
    nq = t // tq - ctx_tiles
    width = hps * group * dv
    assert t % ATTN_KC == 0 and ctx_len % ATTN_KC == 0 and hkv % hps == 0

    def scratch(nkeys):
        return [pltpu.VMEM((nslab, nkeys, ATTN_QB), F32), pltpu.VMEM((nslab, nkeys, ATTN_QB), BF16),
                pltpu.VMEM((nslab, 8, ATTN_QB), F32)]

    s_lat, p_lat, l_lat = scratch(t)
    nh = hkv // hps
    items = b * nh * nq

    def item(step):
        i = jnp.clip(step, 0, items - 1)
        return i // (nh * nq), (i // nq) % nh, i % nq

    def q_map(g):
        bi, h, qi = item(g)
        return bi, h, qi + ctx_tiles, 0

    def k_map(g):
        bi, h, _ = item(g)
        return bi, h, 0, 0

    def v_map(g):
        bi, h, _ = item(g - 2)
        return bi, h, 0, 0

    def o_map(g):
        bi, h, qi = item(g - 2)
        return bi, qi, h

    y_lat = pl.pallas_call(
        functools.partial(_attn_lat_kernel, group=group, tq=tq),
        grid=(items + 2,),
        in_specs=[pl.BlockSpec((1, hps * group, tq, dk), q_map),
                  pl.BlockSpec((1, hps, t, dk), k_map),
                  pl.BlockSpec((1, hps, dv, t), v_map)],
        out_specs=pl.BlockSpec((1, tq, width), o_map),
        out_shape=jax.ShapeDtypeStruct((b, nq * tq, hq * dv), out_dtype),
        scratch_shapes=[s_lat, s_lat, p_lat, p_lat, l_lat, l_lat],
        compiler_params=_params("arbitrary"),
        name="attention_lat_dk%d" % dk,
    )(q, k, vt)
    if not with_ctx_queries:
        return y_lat
    y_ctx = pl.pallas_call(
        functools.partial(_attn_ctx_kernel, group=group, tq=tq),
        grid=(b, hkv // hps, ctx_tiles),
        in_specs=[pl.BlockSpec((1, hps * group, tq, dk), lambda bi, h, i: (bi, h, i, 0)),
                  pl.BlockSpec((1, hps, ctx_len, dk), lambda bi, h, i: (bi, h, 0, 0)),
                  pl.BlockSpec((1, hps, dv, ctx_len), lambda bi, h, i: (bi, h, 0, 0))],
        out_specs=pl.BlockSpec((1, tq, width), lambda bi, h, i: (bi, i, h)),
        out_shape=jax.ShapeDtypeStruct((b, ctx_len, hq * dv), out_dtype),
        scratch_shapes=scratch(ctx_len),
        compiler_params=_params("parallel", "parallel", "arbitrary"),
        name="attention_ctx_dk%d" % dk,
    )(q, k, vt)
    return jnp.concatenate([y_ctx, y_lat], axis=1)


def _conv_kernel(x_ref, w_ref, b_ref, xt_ref, bc_ref, pad_sc, *, ctx_len, rows, x_blocks):
    t = x_ref.shape[1]
    nch = x_ref.shape[2]
    halo = SUBLANE
    segs = ((0, ctx_len), (ctx_len, t))
    zeros = jnp.zeros((halo, nch), F32)
    for si, (lo, hi) in enumerate(segs):
        pad_sc[lo + si * halo:lo + (si + 1) * halo, :] = zeros
        for r0 in range(lo, hi, rows):
            pad_sc[r0 + (si + 1) * halo:r0 + (si + 1) * halo + rows, :] = x_ref[0, r0:r0 + rows, :].astype(F32)
    pad_sc[t + 2 * halo:t + 3 * halo, :] = zeros
    w = w_ref[...]
    bias = b_ref[...]

    def emit(transposed):
        for si, (lo, hi) in enumerate(segs):
            for r0 in range(lo, hi, rows):
                base = r0 + (si + 1) * halo - SSD_CONV // 2
                acc = bias + w[0:1, :] * pad_sc[base:base + rows, :]
                for kk in range(1, SSD_CONV):
                    acc = acc + w[kk:kk + 1, :] * pad_sc[base + kk:base + kk + rows, :]
                y = _silu(acc)
                if transposed:
                    xt_ref[0, :, r0:r0 + rows] = y.T.astype(xt_ref.dtype)
                else:
                    bc_ref[0, r0:r0 + rows, :] = y.astype(bc_ref.dtype)

    is_x = pl.program_id(1) < x_blocks
    pl.when(is_x)(lambda: emit(True))
    pl.when(jnp.logical_not(is_x))(lambda: emit(False))


def _conv(p3, conv_w, conv_b, ctx_len, out_dtype):
    b, t, _ = p3.shape
    nch = 256
    rows = _tile(ctx_len, ROW_TILE)
    blk0 = P_OFF['xbc'] // nch
    x_blocks = SSD_INNER // nch
    kern = functools.partial(_conv_kernel, ctx_len=ctx_len, rows=rows, x_blocks=x_blocks)
    return pl.pallas_call(
        kern,
        grid=(b, SSD_CONV_DIM // nch),
        in_specs=[pl.BlockSpec((1, t, nch), lambda bi, j: (bi, 0, blk0 + j)),
                  pl.BlockSpec((SSD_CONV, nch), lambda bi, j: (0, j)),
                  pl.BlockSpec((1, nch), lambda bi, j: (0, j))],
        out_specs=[pl.BlockSpec((1, nch, t), lambda bi, j: (bi, jnp.minimum(j, x_blocks - 1), 0)),
                   pl.BlockSpec((1, t, nch), lambda bi, j: (bi, 0, jnp.maximum(j - x_blocks, 0)))],
        out_shape=[jax.ShapeDtypeStruct((b, SSD_INNER, t), out_dtype),
                   jax.ShapeDtypeStruct((b, t, SSD_CONV_DIM - SSD_INNER), out_dtype)],
        scratch_shapes=[pltpu.VMEM((t + 3 * SUBLANE, nch), F32)],
        compiler_params=_params("parallel", "arbitrary"),
        name="ssd_conv",
    )(p3, conv_w, conv_b.reshape(1, SSD_CONV_DIM))


def _split3(v):
    hi = v.astype(BF16)
    r1 = v - hi.astype(F32)
    mid = r1.astype(BF16)
    lo = (r1 - mid.astype(F32)).astype(BF16)
    return hi, mid, lo


def _expand_rows(v):
    q = v.shape[1]
    return jnp.concatenate([jnp.broadcast_to(v[e:e + 1, :], (SSD_P, q)) for e in range(SSD_E)], axis=0)


def _ssd_decays(dt_col_raw, dt_row_raw, bias_col, bias_row, a_col, a_row, backward):
    q = dt_col_raw.shape[0]
    ri = lax.broadcasted_iota(jnp.int32, (q, q), 0)
    ci = lax.broadcasted_iota(jnp.int32, (q, q), 1)
    tri_col = jnp.where((ri <= ci) if backward else (ri >= ci), 1.0, 0.0).astype(BF16)
    tri_row = jnp.where((ri >= ci) if backward else (ri <= ci), 1.0, 0.0).astype(BF16)
    dt_c = _softplus(dt_col_raw + bias_col)
    dt_r = _softplus(dt_row_raw + bias_row)
    cum_c = sum(jnp.dot(tri_col, part, preferred_element_type=F32) for part in _split3(dt_c * a_col))
    cum_r = sum(jnp.dot(part, tri_row, preferred_element_type=F32) for part in _split3(dt_r * a_row))
    total = jnp.broadcast_to(cum_r[:, 0:1] if backward else cum_r[:, q - 1:q], (SSD_E, q))
    return dt_r, cum_c, cum_r, total


def _ssd_direction(xt, bm, cm, decays, h_ref, backward):
    q = xt.shape[1]
    dt_r, cum_c, cum_r, total = decays
    ri = lax.broadcasted_iota(jnp.int32, (q, q), 0)
    ci = lax.broadcasted_iota(jnp.int32, (q, q), 1)
    keep_t = (ci <= ri) if backward else (ci >= ri)

    bmb = bm.astype(BF16)
    cmb = cm.astype(BF16)
    nt = (((1,), (1,)), ((), ()))
    cbt = lax.dot_general(bmb, cmb, nt, preferred_element_type=F32)
    h = h_ref[...]
    y_off = lax.dot_general(h.astype(BF16), cmb, nt, preferred_element_type=F32) * _expand_rows(jnp.exp(cum_r))
    wgt = (xt * _expand_rows(dt_r * jnp.exp(total - cum_r))).astype(BF16)
    h_ref[...] = _expand_rows(jnp.exp(total)) * h + jnp.dot(wgt, bmb, preferred_element_type=F32)

    xdt = (xt * _expand_rows(dt_r)).astype(BF16)
    parts = []
    for e in range(SSD_E):
        seg = cum_r[e:e + 1, :] - cum_c[:, e:e + 1]
        dec = jnp.exp(jnp.where(keep_t, seg, -jnp.inf))
        parts.append(jnp.dot(xdt[e * SSD_P:(e + 1) * SSD_P, :], (cbt * dec).astype(BF16),
                             preferred_element_type=F32))
    return jnp.concatenate(parts, axis=0) + y_off


def _ssd_kernel(xf_ref, bf_ref, cf_ref, xb_ref, bb_ref, cb_ref, dcf_ref, drf_ref, dcb_ref, drb_ref,
                bias_c_ref, bias_r_ref, a_c_ref, a_r_ref, d_ref, yf_ref, yb_ref, hf_sc, hb_sc):
    @pl.when(pl.program_id(2) == 0)
    def _():
        hf_sc[...] = jnp.zeros(hf_sc.shape, F32)
        hb_sc[...] = jnp.zeros(hb_sc.shape, F32)

    nsub = xf_ref.shape[2] // SSD_Q
    gps = dcf_ref.shape[2]
    chunks = [slice(i * SSD_Q, (i + 1) * SSD_Q) for i in range(nsub)]
    dec_f = [[_ssd_decays(dcf_ref[0, 0, gi, rs, :], drf_ref[0, 0, gi, :, rs], bias_c_ref[0, gi],
                          bias_r_ref[0, gi], a_c_ref[0, gi], a_r_ref[0, gi], False) for rs in chunks]
             for gi in range(gps)]
    dec_b = [[_ssd_decays(dcb_ref[0, 0, gi, rs, :], drb_ref[0, 0, gi, :, rs], bias_c_ref[1, gi],
                          bias_r_ref[1, gi], a_c_ref[1, gi], a_r_ref[1, gi], True) for rs in chunks]
             for gi in range(gps)]
    for gi in range(gps):
        ch = slice(gi * SSD_GW, (gi + 1) * SSD_GW)
        st = slice(gi * SSD_N, (gi + 1) * SSD_N)
        for i in range(nsub):
            rf, rb = chunks[i], chunks[nsub - 1 - i]
            xt = xf_ref[0, ch, rf].astype(F32)
            yf = _ssd_direction(xt, bf_ref[0, rf, st].astype(F32), cf_ref[0, rf, st].astype(F32),
                                dec_f[gi][i], hf_sc.at[ch], False)
            yf_ref[0, ch, rf] = (yf + d_ref[gi] * xt).astype(yf_ref.dtype)
            yb = _ssd_direction(xb_ref[0, ch, rb].astype(F32), bb_ref[0, rb, st].astype(F32),
                                cb_ref[0, rb, st].astype(F32), dec_b[gi][nsub - 1 - i], hb_sc.at[ch], True)
            yb_ref[0, ch, rb] = yb.astype(yb_ref.dtype)


def _ssd(xt, bc, dt_col, dt_row, bias_c, bias_r, a_c, a_r, d_rows, ctx_len, out_dtype):
    b, _, t = xt.shape
    assert SSD_Q == SSD_N
    rows = _tile(ctx_len, SSD_ROWS, SSD_Q)
    nblk = t // rows
    nctx = ctx_len // rows

    def bidx(c):
        return jnp.where(c < nctx, nctx - 1 - c, nblk - 1 - (c - nctx))

    gps = SSD_GPS
    ngs = SSD_G // gps
    fx = lambda bi, g, c: (bi, g, c)
    fb = lambda bi, g, c: (bi, c, g)
    fc = lambda bi, g, c: (bi, c, ngs + g)
    bx = lambda bi, g, c: (bi, g, bidx(c))
    bb = lambda bi, g, c: (bi, bidx(c), g)
    bcm = lambda bi, g, c: (bi, bidx(c), ngs + g)
    small = lambda a: pl.BlockSpec((2, gps) + a.shape[2:], lambda bi, g, c: (0, g, 0, 0))
    return pl.pallas_call(
        _ssd_kernel,
        grid=(b, ngs, nblk),
        in_specs=[pl.BlockSpec((1, gps * SSD_GW, rows), fx), pl.BlockSpec((1, rows, gps * SSD_N), fb),
                  pl.BlockSpec((1, rows, gps * SSD_N), fc),
                  pl.BlockSpec((1, gps * SSD_GW, rows), bx), pl.BlockSpec((1, rows, gps * SSD_N), bb),
                  pl.BlockSpec((1, rows, gps * SSD_N), bcm),
                  pl.BlockSpec((1, 1, gps, rows, SSD_E), lambda bi, g, c: (bi, 0, g, c, 0)),
                  pl.BlockSpec((1, 1, gps, SSD_E, rows), lambda bi, g, c: (bi, 0, g, 0, c)),
                  pl.BlockSpec((1, 1, gps, rows, SSD_E), lambda bi, g, c: (bi, 1, g, bidx(c), 0)),
                  pl.BlockSpec((1, 1, gps, SSD_E, rows), lambda bi, g, c: (bi, 1, g, 0, bidx(c))),
                  small(bias_c), small(bias_r), small(a_c), small(a_r),
                  pl.BlockSpec((gps, SSD_GW, SSD_Q), lambda bi, g, c: (g, 0, 0))],
        out_specs=[pl.BlockSpec((1, gps * SSD_GW, rows), fx), pl.BlockSpec((1, gps * SSD_GW, rows), bx)],
        out_shape=[jax.ShapeDtypeStruct((b, SSD_INNER, t), out_dtype)] * 2,
        scratch_shapes=[pltpu.VMEM((gps * SSD_GW, SSD_N), F32), pltpu.VMEM((gps * SSD_GW, SSD_N), F32)],
        compiler_params=_params("parallel", "parallel", "arbitrary"),
        name="ssd_scan",
    )(xt, bc, bc, xt, bc, bc, dt_col, dt_row, dt_col, dt_row, bias_c, bias_r, a_c, a_r, d_rows)


MERGE_TN = 2048


def _merge_kernel(ya_ref, ga_ref, yb_ref, gb_ref, yf_ref, ybk_ref, z_ref, nrm_ref,
                  mg_ref, wa_ref, wb_ref, wc_ref, u_ref, c_sc):
    a_in = (ya_ref[0].astype(F32) * _silu(ga_ref[0].astype(F32))).astype(BF16)
    b_in = (yb_ref[0].astype(F32) * _silu(gb_ref[0].astype(F32))).astype(BF16)
    v = (yf_ref[0].astype(F32) + ybk_ref[0].astype(F32)).T * _silu(z_ref[0].astype(F32))
    for g in range(SSD_G):
        sl = slice(g * SSD_GW, (g + 1) * SSD_GW)
        c_sc[:, sl] = (_rms(v[:, sl]) * nrm_ref[:, sl]).astype(BF16)
    c_in = c_sc[...]
    d = u_ref.shape[-1]
    for j in range(0, d, MERGE_TN):
        sl = slice(j, j + MERGE_TN)
        br_a = jnp.dot(a_in, wa_ref[:, sl], preferred_element_type=F32)
        br_b = jnp.dot(b_in, wb_ref[:, sl], preferred_element_type=F32)
        br_c = jnp.dot(c_in, wc_ref[:, sl], preferred_element_type=F32)
        u = (jax.nn.sigmoid(mg_ref[0, :, j:j + MERGE_TN].astype(F32)) * br_a
             + jax.nn.sigmoid(mg_ref[0, :, d + j:d + j + MERGE_TN].astype(F32)) * br_b
             + jax.nn.sigmoid(mg_ref[0, :, 2 * d + j:2 * d + j + MERGE_TN].astype(F32)) * br_c)
        u_ref[0, :, sl] = u.astype(u_ref.dtype)


def _merge(ya, yb, yf, ybk, p3, ssd_norm, wa, wb, wc, row_off):
    b, rows, _ = ya.shape
    tm = _tile(rows, ROW_TILE)
    d = D_MODEL
    ro = row_off // tm
    assert P_OFF['mg'] == 0

    def col(name, width):
        blk = P_OFF[name] // width
        return pl.BlockSpec((1, tm, width), lambda bi, i: (bi, i + ro, blk))

    loc = lambda w: pl.BlockSpec((1, tm, w), lambda bi, i: (bi, i, 0))
    glob = lambda w: pl.BlockSpec((1, tm, w), lambda bi, i: (bi, i + ro, 0))
    globt = pl.BlockSpec((1, SSD_INNER, tm), lambda bi, i: (bi, 0, i + ro))
    wspec = lambda k: pl.BlockSpec((k, d), lambda bi, i: (0, 0), pipeline_mode=pl.Buffered(1))
    return pl.pallas_call(
        _merge_kernel,
        grid=(b, rows // tm),
        in_specs=[loc(MLA_WIDTH), col('ga', MLA_WIDTH), loc(GQA_WIDTH), col('gb', GQA_WIDTH),
                  globt, globt, col('z', SSD_INNER),
                  pl.BlockSpec((1, SSD_INNER), lambda bi, i: (0, 0)),
                  glob(N_BRANCH * d), wspec(MLA_WIDTH), wspec(GQA_WIDTH), wspec(SSD_INNER)],
        out_specs=pl.BlockSpec((1, tm, d), lambda bi, i: (bi, i, 0)),
        out_shape=jax.ShapeDtypeStruct((b, rows, d), BF16),
        scratch_shapes=[pltpu.VMEM((tm, SSD_INNER), BF16)],
        compiler_params=_params("parallel", "parallel"),
        name="merge",
    )(ya, p3, yb, p3, yf, ybk, p3, ssd_norm.reshape(1, SSD_INNER), p3, wa, wb, wc)


def _out_ln_kernel(u_ref, w_ref, *rest, with_next, ctx_tiles, ro, split):
    if split:
        ctx_ref, x_ref, gate_ref, g_ref, b_ref, *rest = rest
        res = _residual_rows(ctx_ref, x_ref, ctx_tiles, ro)
    else:
        x_ref, gate_ref, g_ref, b_ref, *rest = rest
        res = x_ref[0]
    out = jnp.dot(u_ref[0], w_ref[...], preferred_element_type=F32)
    r = DEEPNORM_ALPHA * res + gate_ref[0] * out
    xn = _layer_norm(r) * g_ref[...] + b_ref[...]
    if with_next:
        sh_ref, sc_ref, xo_ref, xm_ref = rest
        xo_ref[0] = xn
        xm_ref[0] = (_layer_norm(xn) * (1.0 + sc_ref[0]) + sh_ref[0]).astype(xm_ref.dtype)
    else:
        (xo_ref,) = rest
        xo_ref[0] = xn


def _out_ln(u, w_out, xc, tab, ln_g, ln_b, next_tab, nb, ctx_len, row_off):
    b, rows, d = u.shape
    tm = _tile(ctx_len, ROW_TILE)
    ro = row_off // tm
    ctx_tiles = ctx_len // tm
    row = _mod_row_index(nb, ctx_tiles)
    with_next = next_tab is not None
    split = isinstance(xc, tuple)
    loc = pl.BlockSpec((1, tm, d), lambda bi, i: (bi, i, 0))
    vec = pl.BlockSpec((1, d), lambda bi, i: (0, 0))
    res_specs = (_residual_specs(tm, d, ctx_tiles, ro) if split
                 else [pl.BlockSpec((1, tm, d), lambda bi, i: (bi, i + ro, 0))])
    in_specs = [loc, pl.BlockSpec((d, d), lambda bi, i: (0, 0))] + res_specs + [
        pl.BlockSpec((1, 1, d), lambda bi, i: (row(bi, i + ro), 0, 2)), vec, vec]
    args = [u, w_out] + (list(xc) if split else [xc]) + [tab, ln_g.reshape(1, d), ln_b.reshape(1, d)]
    out_specs = [loc]
    out_shape = [jax.ShapeDtypeStruct((b, rows, d), F32)]
    if with_next:
        in_specs += [pl.BlockSpec((1, 1, d), lambda bi, i: (row(bi, i + ro), 0, 0)),
                     pl.BlockSpec((1, 1, d), lambda bi, i: (row(bi, i + ro), 0, 1))]
        args += [next_tab, next_tab]
        out_specs.append(loc)
        out_shape.append(jax.ShapeDtypeStruct((b, rows, d), BF16))
    return pl.pallas_call(
        functools.partial(_out_ln_kernel, with_next=with_next, ctx_tiles=ctx_tiles, ro=ro, split=split),
        grid=(b, rows // tm),
        in_specs=in_specs,
        out_specs=out_specs,
        out_shape=out_shape,
        compiler_params=_params("parallel", "parallel"),
        name="out_ln",
    )(*args)


def _rope_angles(rows, dim):
    row, col = jnp.meshgrid(jnp.arange(rows, dtype=F32), jnp.arange(GRID_W, dtype=F32), indexing='ij')
    half = dim // 2
    inv_freq = ROPE_THETA ** (-jnp.arange(0, half, 2, dtype=F32) / half)
    ang_r = row.reshape(-1, 1) * inv_freq
    ang_c = col.reshape(-1, 1) * inv_freq
    return jnp.concatenate([ang_r, ang_r, ang_c, ang_c], axis=-1)


def _rope_tables(seq, ctx_len, dim):
    ang = _rope_angles(seq // GRID_W, dim)
    cos = jnp.concatenate([jnp.ones((ctx_len, dim), F32), jnp.cos(ang)], axis=0)
    sin = jnp.concatenate([jnp.zeros((ctx_len, dim), F32), jnp.sin(ang)], axis=0)
    return cos, sin


def _roll_tables(cos, sin, dim, scale):
    t = cos.shape[0]
    quarter = dim // 4
    first = (jnp.arange(dim) % (2 * quarter)) < quarter
    s1 = jnp.where(first, -sin, 0.0)
    s2 = jnp.where(first, 0.0, sin)
    pad = lambda a: jnp.pad(a * scale, ((0, 0), (0, LANE - dim)))
    return pad(cos), pad(s1), pad(s2)


def _rot_matrix(dim):
    quarter = dim // 4
    r = np.zeros((dim, dim), np.float32)
    for i in range(dim):
        blk = i // quarter
        if blk % 2 == 0:
            r[i + quarter, i] = -1.0
        else:
            r[i - quarter, i] = 1.0
    return jnp.asarray(r)


def _mla_q_weights(w_uq):
    w = w_uq.reshape(MLA_Q_LORA, MLA_HEADS, MLA_QK)
    nope, pe = w[..., :MLA_NOPE], w[..., MLA_NOPE:]
    pe_rot = jnp.einsum('khd,de->khe', pe, _rot_matrix(MLA_ROPE), precision=HIGHEST)
    zpad = jnp.zeros((MLA_Q_LORA, MLA_HEADS, MLA_QK_PAD - MLA_QK), F32)
    main = jnp.concatenate([nope, pe, zpad], axis=-1)
    rot = jnp.concatenate([jnp.zeros_like(nope), pe_rot, zpad], axis=-1)
    return (main.reshape(MLA_Q_LORA, -1).astype(BF16), rot.reshape(MLA_Q_LORA, -1).astype(BF16))


def _permute_w_in(w):
    parts = [w[:, IN_OFFSETS[n]:IN_OFFSETS[n] + IN_WIDTHS[n]] for n in P_ORDER]
    parts.append(jnp.zeros((w.shape[0], P_WIDTH - P_USED), w.dtype))
    return jnp.concatenate(parts, axis=1).astype(BF16)


P_DTYPE = BF16
Y_DTYPE = BF16


def kernel(x, c, ctx, c_ctx, w_mod, b_mod, w_in, mla_q_norm, mla_w_uq, mla_kv_norm, mla_w_ukv,
           gqa_q_norm, gqa_k_norm, ssd_conv_w, ssd_conv_b, ssd_a_log, ssd_dt_bias, ssd_d, ssd_norm,
           w_br_a, w_br_b, w_br_c, w_out, ln_g, ln_b):
    nb, seq, d = x.shape
    ctx_len = ctx.shape[1]
    t = ctx_len + seq
    depth = w_in.shape[0]
    assert d == D_MODEL and nb < 8 and seq % GRID_W == 0
    assert ctx_len % SSD_Q == 0 and seq % SSD_Q == 0

    cos_a, sin_a = _rope_tables(seq, ctx_len, MLA_ROPE)
    cos_b, sin_b = _rope_tables(seq, ctx_len, GQA_DIM)
    sq = MLA_QK ** -0.5 * LOG2E
    zq = jnp.zeros((t, MLA_QK_PAD - MLA_QK), F32)
    cosq = jnp.concatenate([jnp.full((t, MLA_NOPE), sq, F32), cos_a * sq, zq], axis=1)
    sinq = jnp.concatenate([jnp.zeros((t, MLA_NOPE), F32), sin_a * sq, zq], axis=1)
    mla_tabs = (cosq, sinq) + _roll_tables(cos_a, sin_a, MLA_ROPE, 1.0)
    gqa_tabs = (_roll_tables(cos_b, sin_b, GQA_DIM, GQA_DIM ** -0.5 * LOG2E)
                + _roll_tables(cos_b, sin_b, GQA_DIM, 1.0))

    c_rows = jnp.zeros((8, d), F32).at[:nb].set(c).at[nb].set(c_ctx)
    tabs = [_mod_rows(c_rows, w_mod[l], b_mod[l]).reshape(8, 1, 3 * d) for l in range(depth)]

    xc = (ctx, x)
    xm = _ln_mod(ctx, x, tabs[0], nb)

    for l in range(depth):
        last = l == depth - 1
        wp = _permute_w_in(w_in[l])
        p2, krdt = _in_proj(xm.reshape(nb * t, d), wp, P_DTYPE)
        p3 = p2.reshape(nb, t, P_WIDTH)

        wqm, wqr = _mla_q_weights(mla_w_uq[l])
        (qa, ka, va), (qb, kb, vb) = _qkv_prep(
            p3, mla_q_norm[l].reshape(1, -1), mla_kv_norm[l].reshape(1, -1), wqm, wqr,
            mla_w_ukv[l].astype(BF16), mla_tabs, gqa_q_norm[l].reshape(1, -1),
            gqa_k_norm[l].reshape(1, -1), gqa_tabs, ctx_len)
        ya = _attention(qa, ka, va, ctx_len, *MLA_ATTN_STEP, not last, Y_DTYPE)
        yb = _attention(qb, kb, vb, ctx_len, *GQA_ATTN_STEP, not last, Y_DTYPE)

        xconv_t, bconv = _conv(p3, ssd_conv_w[l], ssd_conv_b[l], ctx_len, Y_DTYPE)
        dt5 = krdt[:, MLA_ROPE:].reshape(nb, t, 2, SSD_G, SSD_E)
        dt_col = jnp.transpose(dt5, (0, 2, 3, 1, 4))
        dt_row = jnp.transpose(dt5, (0, 2, 3, 4, 1))
        bias = ssd_dt_bias[l].astype(F32).reshape(2, SSD_G, SSD_E)
        a = -jnp.exp(ssd_a_log[l].astype(F32)).reshape(2, SSD_G, SSD_E)
        d_rows = jnp.broadcast_to(jnp.repeat(ssd_d[l].astype(F32), SSD_P).reshape(SSD_G, SSD_GW, 1),
                                  (SSD_G, SSD_GW, SSD_Q))
        yf, ybk = _ssd(xconv_t, bconv, dt_col, dt_row, bias[:, :, None, :], bias[:, :, :, None],
                       a[:, :, None, :], a[:, :, :, None], d_rows, ctx_len, Y_DTYPE)

        row_off = ctx_len if last else 0
        u = _merge(ya, yb, yf, ybk, p3, ssd_norm[l], w_br_a[l].astype(BF16), w_br_b[l].astype(BF16),
                   w_br_c[l].astype(BF16), row_off)
        if last:
            (xo,) = _out_ln(u, w_out[l].astype(BF16), xc, tabs[l], ln_g[l], ln_b[l], None, nb,
                            ctx_len, row_off)
            return xo
        xc, xm = _out_ln(u, w_out[l].astype(BF16), xc, tabs[l], ln_g[l], ln_b[l], tabs[l + 1], nb,
                         ctx_len, row_off)
```

```python
import functools
import math

import numpy as np
import jax
import jax.numpy as jnp
from jax import lax
from jax.experimental import pallas as pl
from jax.experimental.pallas import tpu as pltpu

F32 = jnp.float32
BF16 = jnp.bfloat16
HIGHEST = lax.Precision.HIGHEST

D_MODEL = 2048
DEPTH = 2
GRID_W = 64
ROPE_THETA = 10000.0
EPS = 1e-6

MLA_HEADS = 8
MLA_Q_LORA = 512
MLA_KV_LORA = 256
MLA_NOPE = 128
MLA_ROPE = 64
MLA_V = 128
MLA_QK = MLA_NOPE + MLA_ROPE
MLA_QK_PAD = 256
MLA_WIDTH = MLA_HEADS * MLA_V

GQA_HEADS = 8
GQA_KV_HEADS = 2
GQA_GROUP = GQA_HEADS // GQA_KV_HEADS
GQA_DIM = 128
GQA_WIDTH = GQA_HEADS * GQA_DIM
GQA_KV_WIDTH = GQA_KV_HEADS * GQA_DIM

SSD_INNER = D_MODEL
SSD_P = 64
SSD_HEADS = SSD_INNER // SSD_P
SSD_G = 4
SSD_E = SSD_HEADS // SSD_G
SSD_N = 128
SSD_CONV = 5
SSD_Q = 128
SSD_ROWS = 256
SSD_GPS = 4
SSD_GW = SSD_E * SSD_P
SSD_CONV_DIM = SSD_INNER + 2 * SSD_G * SSD_N

N_BRANCH = 3
IN_SPLITS = (MLA_Q_LORA, MLA_KV_LORA, MLA_ROPE, MLA_WIDTH, GQA_WIDTH, GQA_KV_WIDTH, GQA_KV_WIDTH,
             GQA_WIDTH, SSD_INNER, SSD_CONV_DIM, 2 * SSD_HEADS, N_BRANCH * D_MODEL)
IN_NAMES = ('cq', 'ckv', 'kr', 'ga', 'gq', 'gk', 'gv', 'gb', 'z', 'xbc', 'dtr', 'mg')
IN_OFFSETS = dict(zip(IN_NAMES, np.concatenate([[0], np.cumsum(IN_SPLITS)[:-1]]).tolist()))
IN_WIDTHS = dict(zip(IN_NAMES, IN_SPLITS))
P_ORDER = ('mg', 'z', 'xbc', 'ga', 'gq', 'gb', 'cq', 'ckv', 'gk', 'gv', 'kr', 'dtr')
P_OFF = {}
_o = 0
for _n in P_ORDER:
    P_OFF[_n] = _o
    _o += IN_WIDTHS[_n]
P_USED = _o
P_TN = 512
IN_TM = 4352
P_WIDTH = -(-P_USED // P_TN) * P_TN

DEEPNORM_ALPHA = (2 * DEPTH) ** 0.25

VMEM_LIMIT = 56 * 2 ** 20
LANE = 128
SUBLANE = 8
ROW_TILE = 256


def _params(*sem):
    return pltpu.CompilerParams(dimension_semantics=sem, vmem_limit_bytes=VMEM_LIMIT)


def _tile(n, target, align=8):
    t = min(n, target)
    while t > align and (n % t or t % align):
        t -= align
    assert n % t == 0, (n, target)
    return t


def _silu(v):
    return v * jax.nn.sigmoid(v)


def _softplus(v):
    return jnp.maximum(v, 0.0) + jnp.log1p(jnp.exp(-jnp.abs(v)))


def _layer_norm(v):
    mu = jnp.mean(v, axis=-1, keepdims=True)
    vc = v - mu
    var = jnp.mean(vc * vc, axis=-1, keepdims=True)
    return vc * lax.rsqrt(var + EPS)


def _rms(v):
    return v * lax.rsqrt(jnp.mean(v * v, axis=-1, keepdims=True) + EPS)


MOD_TN = 512


def _mod_kernel(c_ref, w_ref, b_ref, o_ref):
    a = _silu(c_ref[...]).astype(BF16)
    o_ref[0] = jnp.dot(a, w_ref[0].astype(BF16), preferred_element_type=F32) + b_ref[0]


def _mod_rows(c_rows, w_mod, b_mod):
    r, d = c_rows.shape
    nl, _, n = w_mod.shape
    return pl.pallas_call(
        _mod_kernel,
        grid=(nl, n // MOD_TN),
        in_specs=[pl.BlockSpec((r, d), lambda l, j: (0, 0)),
                  pl.BlockSpec((1, d, MOD_TN), lambda l, j: (l, 0, j)),
                  pl.BlockSpec((1, 1, MOD_TN), lambda l, j: (l, 0, j))],
        out_specs=pl.BlockSpec((1, r, MOD_TN), lambda l, j: (l, 0, j)),
        out_shape=jax.ShapeDtypeStruct((nl, r, n), F32),
        compiler_params=_params("parallel", "arbitrary"),
        name="mod_rows",
    )(c_rows, w_mod, b_mod.reshape(nl, 1, n))


def _residual_rows(ctx_ref, x_ref, ctx_tiles, ro):
    return jnp.where(pl.program_id(1) + ro < ctx_tiles, ctx_ref[0], x_ref[0])


def _residual_specs(tm, d, ctx_tiles, ro):
    return [pl.BlockSpec((1, tm, d), lambda bi, i: (bi, jnp.minimum(i + ro, ctx_tiles - 1), 0)),
            pl.BlockSpec((1, tm, d), lambda bi, i: (bi, jnp.maximum(i + ro - ctx_tiles, 0), 0))]


def _ln_mod_kernel(ctx_ref, x_ref, sh_ref, sc_ref, o_ref, *, ctx_tiles):
    y = _layer_norm(_residual_rows(ctx_ref, x_ref, ctx_tiles, 0))
    o_ref[0] = (y * (1.0 + sc_ref[0]) + sh_ref[0]).astype(o_ref.dtype)


def _mod_row_index(nb, ctx_tiles):
    return lambda b, i: jnp.where(i < ctx_tiles, nb, b)


def _ln_mod(ctx, x, tab, nb):
    b, seq, d = x.shape
    ctx_len = ctx.shape[1]
    tm = _tile(ctx_len, ROW_TILE)
    ctx_tiles = ctx_len // tm
    row = _mod_row_index(nb, ctx_tiles)
    return pl.pallas_call(
        functools.partial(_ln_mod_kernel, ctx_tiles=ctx_tiles),
        grid=(b, (ctx_len + seq) // tm),
        in_specs=_residual_specs(tm, d, ctx_tiles, 0)
        + [pl.BlockSpec((1, 1, d), lambda bi, i: (row(bi, i), 0, 0)),
           pl.BlockSpec((1, 1, d), lambda bi, i: (row(bi, i), 0, 1))],
        out_specs=pl.BlockSpec((1, tm, d), lambda bi, i: (bi, i, 0)),
        out_shape=jax.ShapeDtypeStruct((b, ctx_len + seq, d), BF16),
        compiler_params=_params("parallel", "parallel"),
        name="ln_mod",
    )(ctx, x, tab, tab)


def _in_proj_kernel(x_ref, w_ref, o_ref, dt_ref, *, dt_tile, dt_col):
    acc = jnp.dot(x_ref[...], w_ref[...], preferred_element_type=F32)
    o_ref[...] = acc.astype(o_ref.dtype)

    @pl.when(pl.program_id(1) == dt_tile)
    def _():
        dt_ref[...] = acc[:, dt_col:dt_col + LANE]


def _in_proj(xm2, wp, out_dtype):
    m, k = xm2.shape
    n = wp.shape[1]
    tm = _tile(m, IN_TM)
    tn = P_TN
    dt_tile, dt_col = divmod(P_OFF['kr'], tn)
    assert P_OFF['dtr'] == P_OFF['kr'] + MLA_ROPE and dt_col % LANE == 0
    return pl.pallas_call(
        functools.partial(_in_proj_kernel, dt_tile=dt_tile, dt_col=dt_col),
        grid=(m // tm, n // tn),
        in_specs=[pl.BlockSpec((tm, k), lambda i, j: (i, 0), pipeline_mode=pl.Buffered(1)),
                  pl.BlockSpec((k, tn), lambda i, j: (0, j))],
        out_specs=[pl.BlockSpec((tm, tn), lambda i, j: (i, j)),
                   pl.BlockSpec((tm, LANE), lambda i, j: (i, 0))],
        out_shape=[jax.ShapeDtypeStruct((m, n), out_dtype), jax.ShapeDtypeStruct((m, LANE), F32)],
        compiler_params=_params("parallel", "arbitrary"),
        name="in_proj",
    )(xm2, wp)


def _mla_prep_kernel(cq_ref, ckv_ref, kr_ref, qn_ref, kvn_ref, wqm_ref, wqr_ref, wkv_ref,
                     cosq_ref, sinq_ref, ck_ref, s1_ref, s2_ref, qa_ref, ka_ref, va_ref):
    cqn = (_rms(cq_ref[0].astype(F32)) * qn_ref[...]).astype(BF16)
    qm = jnp.dot(cqn, wqm_ref[...], preferred_element_type=F32)
    qr = jnp.dot(cqn, wqr_ref[...], preferred_element_type=F32)
    cosq = cosq_ref[...]
    sinq = sinq_ref[...]
    for h in range(MLA_HEADS):
        sl = slice(h * MLA_QK_PAD, (h + 1) * MLA_QK_PAD)
        qa_ref[0, h] = (qm[:, sl] * cosq + qr[:, sl] * sinq).astype(qa_ref.dtype)
    ckvn = (_rms(ckv_ref[0].astype(F32)) * kvn_ref[...]).astype(BF16)
    kv = jnp.dot(ckvn, wkv_ref[...], preferred_element_type=F32)
    kr = kr_ref[0].astype(F32)
    quarter = MLA_ROPE // 4
    kpe = (kr * ck_ref[...] + pltpu.roll(kr, LANE - quarter, axis=1) * s1_ref[...]
           + pltpu.roll(kr, quarter, axis=1) * s2_ref[...]).astype(ka_ref.dtype)
    for h in range(MLA_HEADS):
        base = h * (MLA_NOPE + MLA_V)
        ka_ref[0, h, :, 0:MLA_NOPE] = kv[:, base:base + MLA_NOPE].astype(ka_ref.dtype)
        ka_ref[0, h, :, MLA_NOPE:MLA_QK_PAD] = kpe
        va_ref[0, h] = kv[:, base + MLA_NOPE:base + MLA_NOPE + MLA_V].T.astype(va_ref.dtype)


def _rope128(y, c, s1, s2):
    quarter = GQA_DIM // 4
    return y * c + pltpu.roll(y, LANE - quarter, axis=1) * s1 + pltpu.roll(y, quarter, axis=1) * s2


def _gqa_prep_kernel(gq_ref, gk_ref, gv_ref, qn_ref, kn_ref, cq_ref, s1q_ref, s2q_ref,
                     ck_ref, s1k_ref, s2k_ref, qb_ref, kb_ref, vb_ref):
    gq = gq_ref[0].astype(F32)
    for h in range(GQA_HEADS):
        y = _rms(gq[:, h * GQA_DIM:(h + 1) * GQA_DIM]) * qn_ref[...]
        qb_ref[0, h] = _rope128(y, cq_ref[...], s1q_ref[...], s2q_ref[...]).astype(qb_ref.dtype)
    gk = gk_ref[0].astype(F32)
    gv = gv_ref[0].astype(F32)
    for h in range(GQA_KV_HEADS):
        y = _rms(gk[:, h * GQA_DIM:(h + 1) * GQA_DIM]) * kn_ref[...]
        kb_ref[0, h] = _rope128(y, ck_ref[...], s1k_ref[...], s2k_ref[...]).astype(kb_ref.dtype)
        vb_ref[0, h] = gv[:, h * GQA_DIM:(h + 1) * GQA_DIM].T.astype(vb_ref.dtype)


N_MLA_IN, N_GQA_IN = 13, 11


def _qkv_prep_kernel(*refs):
    mla_in, gqa_in = refs[:N_MLA_IN], refs[N_MLA_IN:N_MLA_IN + N_GQA_IN]
    outs = refs[N_MLA_IN + N_GQA_IN:]
    _mla_prep_kernel(*mla_in, *outs[:3])
    _gqa_prep_kernel(*gqa_in, *outs[3:])


def _qkv_prep(p3, mla_qn, mla_kvn, wqm, wqr, wkv, mla_tabs, gqa_qn, gqa_kn, gqa_tabs, ctx_len):
    b, t, _ = p3.shape
    tm = _tile(ctx_len, ROW_TILE)

    def col(name, width):
        blk = P_OFF[name] // width
        return pl.BlockSpec((1, tm, width), lambda bi, i: (bi, i, blk))

    def full(a):
        return pl.BlockSpec(a.shape, lambda bi, i: (0,) * a.ndim)

    def rows(a):
        return pl.BlockSpec((tm, a.shape[1]), lambda bi, i: (i, 0))

    hm = lambda nh, w: pl.BlockSpec((1, nh, tm, w), lambda bi, i: (bi, 0, i, 0))
    hmt = lambda nh, w: pl.BlockSpec((1, nh, w, tm), lambda bi, i: (bi, 0, 0, i))
    in_specs = ([col('cq', MLA_Q_LORA), col('ckv', MLA_KV_LORA), col('kr', LANE),
                 full(mla_qn), full(mla_kvn), full(wqm), full(wqr), full(wkv)]
                + [rows(a) for a in mla_tabs]
                + [col('gq', GQA_WIDTH), col('gk', GQA_KV_WIDTH), col('gv', GQA_KV_WIDTH),
                   full(gqa_qn), full(gqa_kn)] + [rows(a) for a in gqa_tabs])
    args = (p3, p3, p3, mla_qn, mla_kvn, wqm, wqr, wkv, *mla_tabs, p3, p3, p3, gqa_qn, gqa_kn, *gqa_tabs)
    assert len(mla_tabs) + 8 == N_MLA_IN and len(gqa_tabs) + 5 == N_GQA_IN
    outs = pl.pallas_call(
        _qkv_prep_kernel,
        grid=(b, t // tm),
        in_specs=in_specs,
        out_specs=[hm(MLA_HEADS, MLA_QK_PAD), hm(MLA_HEADS, MLA_QK_PAD), hmt(MLA_HEADS, MLA_V),
                   hm(GQA_HEADS, GQA_DIM), hm(GQA_KV_HEADS, GQA_DIM), hmt(GQA_KV_HEADS, GQA_DIM)],
        out_shape=[jax.ShapeDtypeStruct((b, MLA_HEADS, t, MLA_QK_PAD), BF16),
                   jax.ShapeDtypeStruct((b, MLA_HEADS, t, MLA_QK_PAD), BF16),
                   jax.ShapeDtypeStruct((b, MLA_HEADS, MLA_V, t), BF16),
                   jax.ShapeDtypeStruct((b, GQA_HEADS, t, GQA_DIM), BF16),
                   jax.ShapeDtypeStruct((b, GQA_KV_HEADS, t, GQA_DIM), BF16),
                   jax.ShapeDtypeStruct((b, GQA_KV_HEADS, GQA_DIM, t), BF16)],
        compiler_params=_params("parallel", "parallel"),
        name="qkv_prep",
    )(*args)
    return outs[:3], outs[3:]


ATTN_KC = 256
ATTN_QB = LANE
MLA_ATTN_STEP = (256, 2)
GQA_ATTN_STEP = (128, 1)
LOG2E = math.log2(math.e)


def _attn_logits(qts, k_ref, s_buf, nkeys):
    nqb = qts[0].shape[0] // ATTN_QB
    for j, q in enumerate(qts):
        st = lax.dot_general(k_ref[0, j, 0:nkeys, :], q, (((1,), (1,)), ((), ())),
                             preferred_element_type=F32)
        for b in range(nqb):
            s_buf[j * nqb + b, 0:nkeys, :] = st[:, b * ATTN_QB:(b + 1) * ATTN_QB]


def _attn_softmax_slab(s_buf, p_buf, l_buf, b, nkeys):
    parts = [jnp.max(s_buf[b, k0:k0 + ATTN_KC, :].reshape(ATTN_KC // 64, 8, 8, ATTN_QB), axis=0)
             for k0 in range(0, nkeys, ATTN_KC)]
    while len(parts) > 1:
        parts = [jnp.maximum(parts[i], parts[i + 1]) if i + 1 < len(parts) else parts[i]
                 for i in range(0, len(parts), 2)]
    m = jnp.max(parts[0], axis=(0, 1), keepdims=True)[0]
    lacc = jnp.zeros((8, 8, ATTN_QB), F32)
    for k0 in range(0, nkeys, ATTN_KC):
        p = jnp.exp2(s_buf[b, k0:k0 + ATTN_KC, :] - m)
        lacc = lacc + jnp.sum(p.reshape(ATTN_KC // 64, 8, 8, ATTN_QB), axis=0)
        p_buf[b, k0:k0 + ATTN_KC, :] = p.astype(p_buf.dtype)
    l = jnp.sum(lacc, axis=(0, 1), keepdims=True)[0]
    l_buf[b] = jnp.broadcast_to(1.0 / l, (8, ATTN_QB))


def _attn_output(vt_ref, p_buf, l_buf, o_ref, nkeys, hps, group, tq):
    dv = vt_ref.shape[2]
    nqb = group * tq // ATTN_QB
    for j in range(hps):
        slabs = range(j * nqb, (j + 1) * nqb)
        pt = jnp.concatenate([p_buf[b, 0:nkeys, :] for b in slabs], axis=1)
        ot = jnp.dot(vt_ref[0, j, :, 0:nkeys], pt, preferred_element_type=F32)
        ot = ot * jnp.concatenate([l_buf[b, 0:1, :] for b in slabs], axis=1)
        for g in range(group):
            c0 = (j * group + g) * dv
            o_ref[0, :, c0:c0 + dv] = ot[:, g * tq:(g + 1) * tq].T.astype(o_ref.dtype)


def _attn_queries(q_ref, hps, group):
    return [jnp.concatenate([q_ref[0, j * group + g] for g in range(group)], axis=0)
            for j in range(hps)]


def _attn_ctx_kernel(q_ref, k_ref, vt_ref, o_ref, s_sc, p_sc, l_sc, *, group, tq):
    hps, nkeys = k_ref.shape[1], k_ref.shape[2]
    nslab = hps * group * tq // ATTN_QB
    _attn_logits(_attn_queries(q_ref, hps, group), k_ref, s_sc, nkeys)

    def slab(b, carry):
        _attn_softmax_slab(s_sc, p_sc, l_sc, b, nkeys)
        return carry

    lax.fori_loop(0, nslab, slab, 0)
    _attn_output(vt_ref, p_sc, l_sc, o_ref, nkeys, hps, group, tq)


def _attn_lat_kernel(q_ref, k_ref, vt_ref, o_ref, s0, s1, p0, p1, l0, l1, *, group, tq):
    hps, nkeys = k_ref.shape[1], k_ref.shape[2]
    nslab = hps * group * tq // ATTN_QB
    g = pl.program_id(0)

    @pl.when(g == 0)
    def _():
        for buf in (s0, s1, p0, p1, l0, l1):
            buf[...] = jnp.zeros(buf.shape, buf.dtype)

    def step(s_a, s_b, p_b, p_c, l_b, l_c):
        _attn_logits(_attn_queries(q_ref, hps, group), k_ref, s_a, nkeys)
        for b in range(nslab):
            _attn_softmax_slab(s_b, p_b, l_b, b, nkeys)
        _attn_output(vt_ref, p_c, l_c, o_ref, nkeys, hps, group, tq)

    pl.when(g % 2 == 0)(lambda: step(s0, s1, p1, p0, l1, l0))
    pl.when(g % 2 == 1)(lambda: step(s1, s0, p0, p1, l0, l1))


def _attention(q, k, vt, ctx_len, tq, hps, with_ctx_queries, out_dtype):
    b, hq, t, dk = q.shape
    hkv, dv = k.shape[1], vt.shape[2]
    group = hq // hkv
    tq = _tile(ctx_len, tq, LANE)
    nslab = hps * group * tq // ATTN_QB
    ctx_tiles = ctx_len // tq
    nq = t // tq - ctx_tiles
    width = hps * group * dv
    assert t % ATTN_KC == 0 and ctx_len % ATTN_KC == 0 and hkv % hps == 0

    def scratch(nkeys):
        return [pltpu.VMEM((nslab, nkeys, ATTN_QB), F32), pltpu.VMEM((nslab, nkeys, ATTN_QB), BF16),
                pltpu.VMEM((nslab, 8, ATTN_QB), F32)]

    s_lat, p_lat, l_lat = scratch(t)
    nh = hkv // hps
    items = b * nh * nq

    def item(step):
        i = jnp.clip(step, 0, items - 1)
        return i // (nh * nq), (i // nq) % nh, i % nq

    def q_map(g):
        bi, h, qi = item(g)
        return bi, h, qi + ctx_tiles, 0

    def k_map(g):
        bi, h, _ = item(g)
        return bi, h, 0, 0

    def v_map(g):
        bi, h, _ = item(g - 2)
        return bi, h, 0, 0

    def o_map(g):
        bi, h, qi = item(g - 2)
        return bi, qi, h

    y_lat = pl.pallas_call(
        functools.partial(_attn_lat_kernel, group=group, tq=tq),
        grid=(items + 2,),
        in_specs=[pl.BlockSpec((1, hps * group, tq, dk), q_map),
                  pl.BlockSpec((1, hps, t, dk), k_map),
                  pl.BlockSpec((1, hps, dv, t), v_map)],
        out_specs=pl.BlockSpec((1, tq, width), o_map),
        out_shape=jax.ShapeDtypeStruct((b, nq * tq, hq * dv), out_dtype),
        scratch_shapes=[s_lat, s_lat, p_lat, p_lat, l_lat, l_lat],
        compiler_params=_params("arbitrary"),
        name="attention_lat_dk%d" % dk,
    )(q, k, vt)
    if not with_ctx_queries:
        return y_lat
    y_ctx = pl.pallas_call(
        functools.partial(_attn_ctx_kernel, group=group, tq=tq),
        grid=(b, hkv // hps, ctx_tiles),
        in_specs=[pl.BlockSpec((1, hps * group, tq, dk), lambda bi, h, i: (bi, h, i, 0)),
                  pl.BlockSpec((1, hps, ctx_len, dk), lambda bi, h, i: (bi, h, 0, 0)),
                  pl.BlockSpec((1, hps, dv, ctx_len), lambda bi, h, i: (bi, h, 0, 0))],
        out_specs=pl.BlockSpec((1, tq, width), lambda bi, h, i: (bi, i, h)),
        out_shape=jax.ShapeDtypeStruct((b, ctx_len, hq * dv), out_dtype),
        scratch_shapes=scratch(ctx_len),
        compiler_params=_params("parallel", "parallel", "arbitrary"),
        name="attention_ctx_dk%d" % dk,
    )(q, k, vt)
    return y_ctx, y_lat


def _conv_kernel(x_ref, w_ref, b_ref, xt_ref, bc_ref, pad_sc, *, ctx_len, rows, x_blocks):
    t = x_ref.shape[1]
    nch = x_ref.shape[2]
    halo = SUBLANE
    segs = ((0, ctx_len), (ctx_len, t))
    zeros = jnp.zeros((halo, nch), F32)
    for si, (lo, hi) in enumerate(segs):
        pad_sc[lo + si * halo:lo + (si + 1) * halo, :] = zeros
        for r0 in range(lo, hi, rows):
            pad_sc[r0 + (si + 1) * halo:r0 + (si + 1) * halo + rows, :] = x_ref[0, r0:r0 + rows, :].astype(F32)
    pad_sc[t + 2 * halo:t + 3 * halo, :] = zeros
    w = w_ref[...]
    bias = b_ref[...]

    def emit(transposed):
        for si, (lo, hi) in enumerate(segs):
            for r0 in range(lo, hi, rows):
                base = r0 + (si + 1) * halo - SSD_CONV // 2
                acc = bias + w[0:1, :] * pad_sc[base:base + rows, :]
                for kk in range(1, SSD_CONV):
                    acc = acc + w[kk:kk + 1, :] * pad_sc[base + kk:base + kk + rows, :]
                y = _silu(acc)
                if transposed:
                    xt_ref[0, :, r0:r0 + rows] = y.T.astype(xt_ref.dtype)
                else:
                    bc_ref[0, r0:r0 + rows, :] = y.astype(bc_ref.dtype)

    is_x = pl.program_id(1) < x_blocks
    pl.when(is_x)(lambda: emit(True))
    pl.when(jnp.logical_not(is_x))(lambda: emit(False))


def _conv(p3, conv_w, conv_b, ctx_len, out_dtype):
    b, t, _ = p3.shape
    nch = 256
    rows = _tile(ctx_len, ROW_TILE)
    blk0 = P_OFF['xbc'] // nch
    x_blocks = SSD_INNER // nch
    kern = functools.partial(_conv_kernel, ctx_len=ctx_len, rows=rows, x_blocks=x_blocks)
    return pl.pallas_call(
        kern,
        grid=(b, SSD_CONV_DIM // nch),
        in_specs=[pl.BlockSpec((1, t, nch), lambda bi, j: (bi, 0, blk0 + j)),
                  pl.BlockSpec((SSD_CONV, nch), lambda bi, j: (0, j)),
                  pl.BlockSpec((1, nch), lambda bi, j: (0, j))],
        out_specs=[pl.BlockSpec((1, nch, t), lambda bi, j: (bi, jnp.minimum(j, x_blocks - 1), 0)),
                   pl.BlockSpec((1, t, nch), lambda bi, j: (bi, 0, jnp.maximum(j - x_blocks, 0)))],
        out_shape=[jax.ShapeDtypeStruct((b, SSD_INNER, t), out_dtype),
                   jax.ShapeDtypeStruct((b, t, SSD_CONV_DIM - SSD_INNER), out_dtype)],
        scratch_shapes=[pltpu.VMEM((t + 3 * SUBLANE, nch), F32)],
        compiler_params=_params("parallel", "arbitrary"),
        name="ssd_conv",
    )(p3, conv_w, conv_b.reshape(1, SSD_CONV_DIM))


def _split3(v):
    hi = v.astype(BF16)
    r1 = v - hi.astype(F32)
    mid = r1.astype(BF16)
    lo = (r1 - mid.astype(F32)).astype(BF16)
    return hi, mid, lo


def _expand_rows(v):
    q = v.shape[1]
    return jnp.concatenate([jnp.broadcast_to(v[e:e + 1, :], (SSD_P, q)) for e in range(SSD_E)], axis=0)


def _ssd_decays(dt_col_raw, dt_row_raw, bias_col, bias_row, a_col, a_row, backward):
    q = dt_col_raw.shape[0]
    ri = lax.broadcasted_iota(jnp.int32, (q, q), 0)
    ci = lax.broadcasted_iota(jnp.int32, (q, q), 1)
    tri_col = jnp.where((ri <= ci) if backward else (ri >= ci), 1.0, 0.0).astype(BF16)
    tri_row = jnp.where((ri >= ci) if backward else (ri <= ci), 1.0, 0.0).astype(BF16)
    dt_c = _softplus(dt_col_raw + bias_col)
    dt_r = _softplus(dt_row_raw + bias_row)
    cum_c = sum(jnp.dot(tri_col, part, preferred_element_type=F32) for part in _split3(dt_c * a_col))
    cum_r = sum(jnp.dot(part, tri_row, preferred_element_type=F32) for part in _split3(dt_r * a_row))
    total = jnp.broadcast_to(cum_r[:, 0:1] if backward else cum_r[:, q - 1:q], (SSD_E, q))
    return dt_r, cum_c, cum_r, total


def _ssd_direction(xt, bm, cm, decays, h_ref, backward):
    q = xt.shape[1]
    dt_r, cum_c, cum_r, total = decays
    ri = lax.broadcasted_iota(jnp.int32, (q, q), 0)
    ci = lax.broadcasted_iota(jnp.int32, (q, q), 1)
    keep_t = (ci <= ri) if backward else (ci >= ri)

    bmb = bm.astype(BF16)
    cmb = cm.astype(BF16)
    nt = (((1,), (1,)), ((), ()))
    cbt = lax.dot_general(bmb, cmb, nt, preferred_element_type=F32)
    h = h_ref[...]
    y_off = lax.dot_general(h.astype(BF16), cmb, nt, preferred_element_type=F32) * _expand_rows(jnp.exp(cum_r))
    wgt = (xt * _expand_rows(dt_r * jnp.exp(total - cum_r))).astype(BF16)
    h_ref[...] = _expand_rows(jnp.exp(total)) * h + jnp.dot(wgt, bmb, preferred_element_type=F32)

    xdt = (xt * _expand_rows(dt_r)).astype(BF16)
    parts = []
    for e in range(SSD_E):
        seg = cum_r[e:e + 1, :] - cum_c[:, e:e + 1]
        dec = jnp.exp(jnp.where(keep_t, seg, -jnp.inf))
        parts.append(jnp.dot(xdt[e * SSD_P:(e + 1) * SSD_P, :], (cbt * dec).astype(BF16),
                             preferred_element_type=F32))
    return jnp.concatenate(parts, axis=0) + y_off


def _ssd_kernel(xf_ref, bf_ref, cf_ref, xb_ref, bb_ref, cb_ref, dcf_ref, drf_ref, dcb_ref, drb_ref,
                bias_c_ref, bias_r_ref, a_c_ref, a_r_ref, d_ref, yf_ref, yb_ref, hf_sc, hb_sc):
    @pl.when(pl.program_id(2) == 0)
    def _():
        hf_sc[...] = jnp.zeros(hf_sc.shape, F32)
        hb_sc[...] = jnp.zeros(hb_sc.shape, F32)

    nsub = xf_ref.shape[2] // SSD_Q
    gps = dcf_ref.shape[2]
    chunks = [slice(i * SSD_Q, (i + 1) * SSD_Q) for i in range(nsub)]
    dec_f = [[_ssd_decays(dcf_ref[0, 0, gi, rs, :], drf_ref[0, 0, gi, :, rs], bias_c_ref[0, gi],
                          bias_r_ref[0, gi], a_c_ref[0, gi], a_r_ref[0, gi], False) for rs in chunks]
             for gi in range(gps)]
    dec_b = [[_ssd_decays(dcb_ref[0, 0, gi, rs, :], drb_ref[0, 0, gi, :, rs], bias_c_ref[1, gi],
                          bias_r_ref[1, gi], a_c_ref[1, gi], a_r_ref[1, gi], True) for rs in chunks]
             for gi in range(gps)]
    for gi in range(gps):
        ch = slice(gi * SSD_GW, (gi + 1) * SSD_GW)
        st = slice(gi * SSD_N, (gi + 1) * SSD_N)
        for i in range(nsub):
            rf, rb = chunks[i], chunks[nsub - 1 - i]
            xt = xf_ref[0, ch, rf].astype(F32)
            yf = _ssd_direction(xt, bf_ref[0, rf, st].astype(F32), cf_ref[0, rf, st].astype(F32),
                                dec_f[gi][i], hf_sc.at[ch], False)
            yf_ref[0, ch, rf] = (yf + d_ref[gi] * xt).astype(yf_ref.dtype)
            yb = _ssd_direction(xb_ref[0, ch, rb].astype(F32), bb_ref[0, rb, st].astype(F32),
                                cb_ref[0, rb, st].astype(F32), dec_b[gi][nsub - 1 - i], hb_sc.at[ch], True)
            yb_ref[0, ch, rb] = yb.astype(yb_ref.dtype)


def _ssd(xt, bc, dt_col, dt_row, bias_c, bias_r, a_c, a_r, d_rows, ctx_len, out_dtype):
    b, _, t = xt.shape
    assert SSD_Q == SSD_N
    rows = _tile(ctx_len, SSD_ROWS, SSD_Q)
    nblk = t // rows
    nctx = ctx_len // rows

    def bidx(c):
        return jnp.where(c < nctx, nctx - 1 - c, nblk - 1 - (c - nctx))

    gps = SSD_GPS
    ngs = SSD_G // gps
    fx = lambda bi, g, c: (bi, g, c)
    fb = lambda bi, g, c: (bi, c, g)
    fc = lambda bi, g, c: (bi, c, ngs + g)
    bx = lambda bi, g, c: (bi, g, bidx(c))
    bb = lambda bi, g, c: (bi, bidx(c), g)
    bcm = lambda bi, g, c: (bi, bidx(c), ngs + g)
    small = lambda a: pl.BlockSpec((2, gps) + a.shape[2:], lambda bi, g, c: (0, g, 0, 0))
    return pl.pallas_call(
        _ssd_kernel,
        grid=(b, ngs, nblk),
        in_specs=[pl.BlockSpec((1, gps * SSD_GW, rows), fx), pl.BlockSpec((1, rows, gps * SSD_N), fb),
                  pl.BlockSpec((1, rows, gps * SSD_N), fc),
                  pl.BlockSpec((1, gps * SSD_GW, rows), bx), pl.BlockSpec((1, rows, gps * SSD_N), bb),
                  pl.BlockSpec((1, rows, gps * SSD_N), bcm),
                  pl.BlockSpec((1, 1, gps, rows, SSD_E), lambda bi, g, c: (bi, 0, g, c, 0)),
                  pl.BlockSpec((1, 1, gps, SSD_E, rows), lambda bi, g, c: (bi, 0, g, 0, c)),
                  pl.BlockSpec((1, 1, gps, rows, SSD_E), lambda bi, g, c: (bi, 1, g, bidx(c), 0)),
                  pl.BlockSpec((1, 1, gps, SSD_E, rows), lambda bi, g, c: (bi, 1, g, 0, bidx(c))),
                  small(bias_c), small(bias_r), small(a_c), small(a_r),
                  pl.BlockSpec((gps, SSD_GW, SSD_Q), lambda bi, g, c: (g, 0, 0))],
        out_specs=[pl.BlockSpec((1, gps * SSD_GW, rows), fx), pl.BlockSpec((1, gps * SSD_GW, rows), bx)],
        out_shape=[jax.ShapeDtypeStruct((b, SSD_INNER, t), out_dtype)] * 2,
        scratch_shapes=[pltpu.VMEM((gps * SSD_GW, SSD_N), F32), pltpu.VMEM((gps * SSD_GW, SSD_N), F32)],
        compiler_params=_params("parallel", "parallel", "arbitrary"),
        name="ssd_scan",
    )(xt, bc, bc, xt, bc, bc, dt_col, dt_row, dt_col, dt_row, bias_c, bias_r, a_c, a_r, d_rows)


MERGE_TN = 2048


def _merge_kernel(*refs, ctx_tiles):
    if ctx_tiles:
        ya_c, ya_l, yb_c, yb_l, *refs = refs
        is_ctx = pl.program_id(1) < ctx_tiles
        ya = jnp.where(is_ctx, ya_c[0], ya_l[0])
        yb = jnp.where(is_ctx, yb_c[0], yb_l[0])
    else:
        ya_ref, yb_ref, *refs = refs
        ya, yb = ya_ref[0], yb_ref[0]
    ga_ref, gb_ref, yf_ref, ybk_ref, z_ref, nrm_ref, mg_ref, wa_ref, wb_ref, wc_ref, u_ref, c_sc = refs
    a_in = (ya.astype(F32) * _silu(ga_ref[0].astype(F32))).astype(BF16)
    b_in = (yb.astype(F32) * _silu(gb_ref[0].astype(F32))).astype(BF16)
    v = (yf_ref[0].astype(F32) + ybk_ref[0].astype(F32)).T * _silu(z_ref[0].astype(F32))
    for g in range(SSD_G):
        sl = slice(g * SSD_GW, (g + 1) * SSD_GW)
        c_sc[:, sl] = (_rms(v[:, sl]) * nrm_ref[:, sl]).astype(BF16)
    c_in = c_sc[...]
    d = u_ref.shape[-1]
    for j in range(0, d, MERGE_TN):
        sl = slice(j, j + MERGE_TN)
        br_a = jnp.dot(a_in, wa_ref[:, sl], preferred_element_type=F32)
        br_b = jnp.dot(b_in, wb_ref[:, sl], preferred_element_type=F32)
        br_c = jnp.dot(c_in, wc_ref[:, sl], preferred_element_type=F32)
        u = (jax.nn.sigmoid(mg_ref[0, :, j:j + MERGE_TN].astype(F32)) * br_a
             + jax.nn.sigmoid(mg_ref[0, :, d + j:d + j + MERGE_TN].astype(F32)) * br_b
             + jax.nn.sigmoid(mg_ref[0, :, 2 * d + j:2 * d + j + MERGE_TN].astype(F32)) * br_c)
        u_ref[0, :, sl] = u.astype(u_ref.dtype)


def _merge(ya, yb, yf, ybk, p3, ssd_norm, wa, wb, wc, row_off):
    split = isinstance(ya, tuple)
    b = yf.shape[0]
    rows = sum(a.shape[1] for a in ya) if split else ya.shape[1]
    tm = _tile(ya[0].shape[1] if split else rows, ROW_TILE)
    d = D_MODEL
    ro = row_off // tm
    ctx_tiles = ya[0].shape[1] // tm if split else 0
    assert P_OFF['mg'] == 0 and not (split and row_off)

    def col(name, width):
        blk = P_OFF[name] // width
        return pl.BlockSpec((1, tm, width), lambda bi, i: (bi, i + ro, blk))

    loc = lambda w: pl.BlockSpec((1, tm, w), lambda bi, i: (bi, i, 0))
    glob = lambda w: pl.BlockSpec((1, tm, w), lambda bi, i: (bi, i + ro, 0))
    globt = pl.BlockSpec((1, SSD_INNER, tm), lambda bi, i: (bi, 0, i + ro))
    wspec = lambda k: pl.BlockSpec((k, d), lambda bi, i: (0, 0), pipeline_mode=pl.Buffered(1))
    if split:
        def pair(w):
            return [pl.BlockSpec((1, tm, w), lambda bi, i: (bi, jnp.minimum(i, ctx_tiles - 1), 0)),
                    pl.BlockSpec((1, tm, w), lambda bi, i: (bi, jnp.maximum(i - ctx_tiles, 0), 0))]
        y_specs, y_args = pair(MLA_WIDTH) + pair(GQA_WIDTH), [*ya, *yb]
    else:
        y_specs, y_args = [loc(MLA_WIDTH), loc(GQA_WIDTH)], [ya, yb]
    return pl.pallas_call(
        functools.partial(_merge_kernel, ctx_tiles=ctx_tiles),
        grid=(b, rows // tm),
        in_specs=y_specs + [col('ga', MLA_WIDTH), col('gb', GQA_WIDTH),
                            globt, globt, col('z', SSD_INNER),
                            pl.BlockSpec((1, SSD_INNER), lambda bi, i: (0, 0)),
                            glob(N_BRANCH * d), wspec(MLA_WIDTH), wspec(GQA_WIDTH), wspec(SSD_INNER)],
        out_specs=pl.BlockSpec((1, tm, d), lambda bi, i: (bi, i, 0)),
        out_shape=jax.ShapeDtypeStruct((b, rows, d), BF16),
        scratch_shapes=[pltpu.VMEM((tm, SSD_INNER), BF16)],
        compiler_params=_params("parallel", "parallel"),
        name="merge",
    )(*y_args, p3, p3, yf, ybk, p3, ssd_norm.reshape(1, SSD_INNER), p3, wa, wb, wc)


def _out_ln_kernel(u_ref, w_ref, *rest, with_next, ctx_tiles, ro, split):
    if split:
        ctx_ref, x_ref, gate_ref, g_ref, b_ref, *rest = rest
        res = _residual_rows(ctx_ref, x_ref, ctx_tiles, ro)
    else:
        x_ref, gate_ref, g_ref, b_ref, *rest = rest
        res = x_ref[0]
    out = jnp.dot(u_ref[0], w_ref[...], preferred_element_type=F32)
    r = DEEPNORM_ALPHA * res + gate_ref[0] * out
    xn = _layer_norm(r) * g_ref[...] + b_ref[...]
    if with_next:
        sh_ref, sc_ref, xo_ref, xm_ref = rest
        xo_ref[0] = xn
        xm_ref[0] = (_layer_norm(xn) * (1.0 + sc_ref[0]) + sh_ref[0]).astype(xm_ref.dtype)
    else:
        (xo_ref,) = rest
        xo_ref[0] = xn


def _out_ln(u, w_out, xc, tab, ln_g, ln_b, next_tab, nb, ctx_len, row_off):
    b, rows, d = u.shape
    tm = _tile(ctx_len, ROW_TILE)
    ro = row_off // tm
    ctx_tiles = ctx_len // tm
    row = _mod_row_index(nb, ctx_tiles)
    with_next = next_tab is not None
    split = isinstance(xc, tuple)
    loc = pl.BlockSpec((1, tm, d), lambda bi, i: (bi, i, 0))
    vec = pl.BlockSpec((1, d), lambda bi, i: (0, 0))
    res_specs = (_residual_specs(tm, d, ctx_tiles, ro) if split
                 else [pl.BlockSpec((1, tm, d), lambda bi, i: (bi, i + ro, 0))])
    in_specs = [loc, pl.BlockSpec((d, d), lambda bi, i: (0, 0))] + res_specs + [
        pl.BlockSpec((1, 1, d), lambda bi, i: (row(bi, i + ro), 0, 2)), vec, vec]
    args = [u, w_out] + (list(xc) if split else [xc]) + [tab, ln_g.reshape(1, d), ln_b.reshape(1, d)]
    out_specs = [loc]
    out_shape = [jax.ShapeDtypeStruct((b, rows, d), F32)]
    if with_next:
        in_specs += [pl.BlockSpec((1, 1, d), lambda bi, i: (row(bi, i + ro), 0, 0)),
                     pl.BlockSpec((1, 1, d), lambda bi, i: (row(bi, i + ro), 0, 1))]
        args += [next_tab, next_tab]
        out_specs.append(loc)
        out_shape.append(jax.ShapeDtypeStruct((b, rows, d), BF16))
    return pl.pallas_call(
        functools.partial(_out_ln_kernel, with_next=with_next, ctx_tiles=ctx_tiles, ro=ro, split=split),
        grid=(b, rows // tm),
        in_specs=in_specs,
        out_specs=out_specs,
        out_shape=out_shape,
        compiler_params=_params("parallel", "parallel"),
        name="out_ln",
    )(*args)


def _rope_angles(rows, dim):
    row, col = jnp.meshgrid(jnp.arange(rows, dtype=F32), jnp.arange(GRID_W, dtype=F32), indexing='ij')
    half = dim // 2
    inv_freq = ROPE_THETA ** (-jnp.arange(0, half, 2, dtype=F32) / half)
    ang_r = row.reshape(-1, 1) * inv_freq
    ang_c = col.reshape(-1, 1) * inv_freq
    return jnp.concatenate([ang_r, ang_r, ang_c, ang_c], axis=-1)


def _rope_tables(seq, ctx_len, dim):
    ang = _rope_angles(seq // GRID_W, dim)
    cos = jnp.concatenate([jnp.ones((ctx_len, dim), F32), jnp.cos(ang)], axis=0)
    sin = jnp.concatenate([jnp.zeros((ctx_len, dim), F32), jnp.sin(ang)], axis=0)
    return cos, sin


def _roll_tables(cos, sin, dim, scale):
    t = cos.shape[0]
    quarter = dim // 4
    first = (jnp.arange(dim) % (2 * quarter)) < quarter
    s1 = jnp.where(first, -sin, 0.0)
    s2 = jnp.where(first, 0.0, sin)
    pad = lambda a: jnp.pad(a * scale, ((0, 0), (0, LANE - dim)))
    return pad(cos), pad(s1), pad(s2)


def _rot_matrix(dim):
    quarter = dim // 4
    r = np.zeros((dim, dim), np.float32)
    for i in range(dim):
        blk = i // quarter
        if blk % 2 == 0:
            r[i + quarter, i] = -1.0
        else:
            r[i - quarter, i] = 1.0
    return jnp.asarray(r)


def _mla_q_weights(w_uq):
    w = w_uq.reshape(MLA_Q_LORA, MLA_HEADS, MLA_QK)
    nope, pe = w[..., :MLA_NOPE], w[..., MLA_NOPE:]
    pe_rot = jnp.einsum('khd,de->khe', pe, _rot_matrix(MLA_ROPE), precision=HIGHEST)
    zpad = jnp.zeros((MLA_Q_LORA, MLA_HEADS, MLA_QK_PAD - MLA_QK), F32)
    main = jnp.concatenate([nope, pe, zpad], axis=-1)
    rot = jnp.concatenate([jnp.zeros_like(nope), pe_rot, zpad], axis=-1)
    return (main.reshape(MLA_Q_LORA, -1).astype(BF16), rot.reshape(MLA_Q_LORA, -1).astype(BF16))


def _permute_w_in(w):
    parts = [w[:, IN_OFFSETS[n]:IN_OFFSETS[n] + IN_WIDTHS[n]] for n in P_ORDER]
    parts.append(jnp.zeros((w.shape[0], P_WIDTH - P_USED), w.dtype))
    return jnp.concatenate(parts, axis=1).astype(BF16)


P_DTYPE = BF16
Y_DTYPE = BF16


def kernel(x, c, ctx, c_ctx, w_mod, b_mod, w_in, mla_q_norm, mla_w_uq, mla_kv_norm, mla_w_ukv,
           gqa_q_norm, gqa_k_norm, ssd_conv_w, ssd_conv_b, ssd_a_log, ssd_dt_bias, ssd_d, ssd_norm,
           w_br_a, w_br_b, w_br_c, w_out, ln_g, ln_b):
    nb, seq, d = x.shape
    ctx_len = ctx.shape[1]
    t = ctx_len + seq
    depth = w_in.shape[0]
    assert d == D_MODEL and nb < 8 and seq % GRID_W == 0
    assert ctx_len % SSD_Q == 0 and seq % SSD_Q == 0

    cos_a, sin_a = _rope_tables(seq, ctx_len, MLA_ROPE)
    cos_b, sin_b = _rope_tables(seq, ctx_len, GQA_DIM)
    sq = MLA_QK ** -0.5 * LOG2E
    zq = jnp.zeros((t, MLA_QK_PAD - MLA_QK), F32)
    cosq = jnp.concatenate([jnp.full((t, MLA_NOPE), sq, F32), cos_a * sq, zq], axis=1)
    sinq = jnp.concatenate([jnp.zeros((t, MLA_NOPE), F32), sin_a * sq, zq], axis=1)
    mla_tabs = (cosq, sinq) + _roll_tables(cos_a, sin_a, MLA_ROPE, 1.0)
    gqa_tabs = (_roll_tables(cos_b, sin_b, GQA_DIM, GQA_DIM ** -0.5 * LOG2E)
                + _roll_tables(cos_b, sin_b, GQA_DIM, 1.0))

    c_rows = jnp.zeros((8, d), F32).at[:nb].set(c).at[nb].set(c_ctx)
    mods = _mod_rows(c_rows, w_mod, b_mod)
    tabs = [mods[l].reshape(8, 1, 3 * d) for l in range(depth)]

    xc = (ctx, x)
    xm = _ln_mod(ctx, x, tabs[0], nb)

    for l in range(depth):
        last = l == depth - 1
        wp = _permute_w_in(w_in[l])
        p2, krdt = _in_proj(xm.reshape(nb * t, d), wp, P_DTYPE)
        p3 = p2.reshape(nb, t, P_WIDTH)

        wqm, wqr = _mla_q_weights(mla_w_uq[l])
        (qa, ka, va), (qb, kb, vb) = _qkv_prep(
            p3, mla_q_norm[l].reshape(1, -1), mla_kv_norm[l].reshape(1, -1), wqm, wqr,
            mla_w_ukv[l].astype(BF16), mla_tabs, gqa_q_norm[l].reshape(1, -1),
            gqa_k_norm[l].reshape(1, -1), gqa_tabs, ctx_len)
        ya = _attention(qa, ka, va, ctx_len, *MLA_ATTN_STEP, not last, Y_DTYPE)
        yb = _attention(qb, kb, vb, ctx_len, *GQA_ATTN_STEP, not last, Y_DTYPE)

        xconv_t, bconv = _conv(p3, ssd_conv_w[l], ssd_conv_b[l], ctx_len, Y_DTYPE)
        dt5 = krdt[:, MLA_ROPE:].reshape(nb, t, 2, SSD_G, SSD_E)
        dt_col = jnp.transpose(dt5, (0, 2, 3, 1, 4))
        dt_row = jnp.transpose(dt5, (0, 2, 3, 4, 1))
        bias = ssd_dt_bias[l].astype(F32).reshape(2, SSD_G, SSD_E)
        a = -jnp.exp(ssd_a_log[l].astype(F32)).reshape(2, SSD_G, SSD_E)
        d_rows = jnp.broadcast_to(jnp.repeat(ssd_d[l].astype(F32), SSD_P).reshape(SSD_G, SSD_GW, 1),
                                  (SSD_G, SSD_GW, SSD_Q))
        yf, ybk = _ssd(xconv_t, bconv, dt_col, dt_row, bias[:, :, None, :], bias[:, :, :, None],
                       a[:, :, None, :], a[:, :, :, None], d_rows, ctx_len, Y_DTYPE)

        row_off = ctx_len if last else 0
        u = _merge(ya, yb, yf, ybk, p3, ssd_norm[l], w_br_a[l].astype(BF16), w_br_b[l].astype(BF16),
                   w_br_c[l].astype(BF16), row_off)
        if last:
            (xo,) = _out_ln(u, w_out[l].astype(BF16), xc, tabs[l], ln_g[l], ln_b[l], None, nb,
                            ctx_len, row_off)
            return xo
        xc, xm = _out_ln(u, w_out[l].astype(BF16), xc, tabs[l], ln_g[l], ln_b[l], tabs[l + 1], nb,
                         ctx_len, row_off)
```

```python
import functools
import math

import numpy as np
import jax
import jax.numpy as jnp
from jax import lax
from jax.experimental import pallas as pl
from jax.experimental.pallas import tpu as pltpu

F32 = jnp.float32
BF16 = jnp.bfloat16
HIGHEST = lax.Precision.HIGHEST

D_MODEL = 2048
DEPTH = 2
GRID_W = 64
ROPE_THETA = 10000.0
EPS = 1e-6

MLA_HEADS = 8
MLA_Q_LORA = 512
MLA_KV_LORA = 256
MLA_NOPE = 128
MLA_ROPE = 64
MLA_V = 128
MLA_QK = MLA_NOPE + MLA_ROPE
MLA_QK_PAD = 256
MLA_WIDTH = MLA_HEADS * MLA_V

GQA_HEADS = 8
GQA_KV_HEADS = 2
GQA_GROUP = GQA_HEADS // GQA_KV_HEADS
GQA_DIM = 128
GQA_WIDTH = GQA_HEADS * GQA_DIM
GQA_KV_WIDTH = GQA_KV_HEADS * GQA_DIM

SSD_INNER = D_MODEL
SSD_P = 64
SSD_HEADS = SSD_INNER // SSD_P
SSD_G = 4
SSD_E = SSD_HEADS // SSD_G
SSD_N = 128
SSD_CONV = 5
SSD_Q = 128
SSD_ROWS = 256
SSD_GPS = 4
SSD_GW = SSD_E * SSD_P
SSD_CONV_DIM = SSD_INNER + 2 * SSD_G * SSD_N

N_BRANCH = 3
IN_SPLITS = (MLA_Q_LORA, MLA_KV_LORA, MLA_ROPE, MLA_WIDTH, GQA_WIDTH, GQA_KV_WIDTH, GQA_KV_WIDTH,
             GQA_WIDTH, SSD_INNER, SSD_CONV_DIM, 2 * SSD_HEADS, N_BRANCH * D_MODEL)
IN_NAMES = ('cq', 'ckv', 'kr', 'ga', 'gq', 'gk', 'gv', 'gb', 'z', 'xbc', 'dtr', 'mg')
IN_OFFSETS = dict(zip(IN_NAMES, np.concatenate([[0], np.cumsum(IN_SPLITS)[:-1]]).tolist()))
IN_WIDTHS = dict(zip(IN_NAMES, IN_SPLITS))
P_ORDER = ('mg', 'z', 'xbc', 'ga', 'gq', 'gb', 'cq', 'ckv', 'gk', 'gv', 'kr', 'dtr')
P_OFF = {}
_o = 0
for _n in P_ORDER:
    P_OFF[_n] = _o
    _o += IN_WIDTHS[_n]
P_USED = _o
P_TN = 512
IN_TM = 4352
P_WIDTH = -(-P_USED // P_TN) * P_TN

DEEPNORM_ALPHA = (2 * DEPTH) ** 0.25

VMEM_LIMIT = 56 * 2 ** 20
LANE = 128
SUBLANE = 8
ROW_TILE = 256


def _params(*sem):
    return pltpu.CompilerParams(dimension_semantics=sem, vmem_limit_bytes=VMEM_LIMIT)


def _tile(n, target, align=8):
    t = min(n, target)
    while t > align and (n % t or t % align):
        t -= align
    assert n % t == 0, (n, target)
    return t


def _silu(v):
    return v * jax.nn.sigmoid(v)


def _softplus(v):
    return jnp.maximum(v, 0.0) + jnp.log1p(jnp.exp(-jnp.abs(v)))


def _layer_norm(v):
    mu = jnp.mean(v, axis=-1, keepdims=True)
    vc = v - mu
    var = jnp.mean(vc * vc, axis=-1, keepdims=True)
    return vc * lax.rsqrt(var + EPS)


def _rms(v):
    return v * lax.rsqrt(jnp.mean(v * v, axis=-1, keepdims=True) + EPS)


MOD_TN = 512


def _mod_kernel(c_ref, w_ref, b_ref, o_ref):
    a = _silu(c_ref[...]).astype(BF16)
    o_ref[0] = jnp.dot(a, w_ref[0].astype(BF16), preferred_element_type=F32) + b_ref[0]


def _mod_rows(c_rows, w_mod, b_mod):
    r, d = c_rows.shape
    nl, _, n = w_mod.shape
    return pl.pallas_call(
        _mod_kernel,
        grid=(nl, n // MOD_TN),
        in_specs=[pl.BlockSpec((r, d), lambda l, j: (0, 0)),
                  pl.BlockSpec((1, d, MOD_TN), lambda l, j: (l, 0, j)),
                  pl.BlockSpec((1, 1, MOD_TN), lambda l, j: (l, 0, j))],
        out_specs=pl.BlockSpec((1, r, MOD_TN), lambda l, j: (l, 0, j)),
        out_shape=jax.ShapeDtypeStruct((nl, r, n), F32),
        compiler_params=_params("parallel", "arbitrary"),
        name="mod_rows",
    )(c_rows, w_mod, b_mod.reshape(nl, 1, n))


def _residual_rows(ctx_ref, x_ref, ctx_tiles, ro):
    return jnp.where(pl.program_id(1) + ro < ctx_tiles, ctx_ref[0], x_ref[0])


def _residual_specs(tm, d, ctx_tiles, ro):
    return [pl.BlockSpec((1, tm, d), lambda bi, i: (bi, jnp.minimum(i + ro, ctx_tiles - 1), 0)),
            pl.BlockSpec((1, tm, d), lambda bi, i: (bi, jnp.maximum(i + ro - ctx_tiles, 0), 0))]


def _ln_mod_kernel(ctx_ref, x_ref, sh_ref, sc_ref, o_ref, *, ctx_tiles):
    y = _layer_norm(_residual_rows(ctx_ref, x_ref, ctx_tiles, 0))
    o_ref[0] = (y * (1.0 + sc_ref[0]) + sh_ref[0]).astype(o_ref.dtype)


def _mod_row_index(nb, ctx_tiles):
    return lambda b, i: jnp.where(i < ctx_tiles, nb, b)


def _ln_mod(ctx, x, tab, nb):
    b, seq, d = x.shape
    ctx_len = ctx.shape[1]
    tm = _tile(ctx_len, ROW_TILE)
    ctx_tiles = ctx_len // tm
    row = _mod_row_index(nb, ctx_tiles)
    return pl.pallas_call(
        functools.partial(_ln_mod_kernel, ctx_tiles=ctx_tiles),
        grid=(b, (ctx_len + seq) // tm),
        in_specs=_residual_specs(tm, d, ctx_tiles, 0)
        + [pl.BlockSpec((1, 1, d), lambda bi, i: (row(bi, i), 0, 0)),
           pl.BlockSpec((1, 1, d), lambda bi, i: (row(bi, i), 0, 1))],
        out_specs=pl.BlockSpec((1, tm, d), lambda bi, i: (bi, i, 0)),
        out_shape=jax.ShapeDtypeStruct((b, ctx_len + seq, d), BF16),
        compiler_params=_params("parallel", "parallel"),
        name="ln_mod",
    )(ctx, x, tab, tab)


def _in_proj_kernel(x_ref, w_ref, o_ref, dt_ref, *, dt_tile, dt_col):
    acc = jnp.dot(x_ref[...], w_ref[...], preferred_element_type=F32)
    o_ref[...] = acc.astype(o_ref.dtype)

    @pl.when(pl.program_id(1) == dt_tile)
    def _():
        dt_ref[...] = acc[:, dt_col:dt_col + LANE]


def _in_proj(xm2, wp, out_dtype):
    m, k = xm2.shape
    n = wp.shape[1]
    tm = _tile(m, IN_TM)
    tn = P_TN
    dt_tile, dt_col = divmod(P_OFF['kr'], tn)
    assert P_OFF['dtr'] == P_OFF['kr'] + MLA_ROPE and dt_col % LANE == 0
    return pl.pallas_call(
        functools.partial(_in_proj_kernel, dt_tile=dt_tile, dt_col=dt_col),
        grid=(m // tm, n // tn),
        in_specs=[pl.BlockSpec((tm, k), lambda i, j: (i, 0), pipeline_mode=pl.Buffered(1)),
                  pl.BlockSpec((k, tn), lambda i, j: (0, j))],
        out_specs=[pl.BlockSpec((tm, tn), lambda i, j: (i, j)),
                   pl.BlockSpec((tm, LANE), lambda i, j: (i, 0))],
        out_shape=[jax.ShapeDtypeStruct((m, n), out_dtype), jax.ShapeDtypeStruct((m, LANE), F32)],
        compiler_params=_params("parallel", "arbitrary"),
        name="in_proj",
    )(xm2, wp)


def _mla_prep_kernel(cq_ref, ckv_ref, kr_ref, qn_ref, kvn_ref, wqm_ref, wqr_ref, wkv_ref,
                     cosq_ref, sinq_ref, ck_ref, s1_ref, s2_ref, qa_ref, ka_ref, va_ref):
    cqn = (_rms(cq_ref[0].astype(F32)) * qn_ref[...]).astype(BF16)
    qm = jnp.dot(cqn, wqm_ref[...], preferred_element_type=F32)
    qr = jnp.dot(cqn, wqr_ref[...], preferred_element_type=F32)
    cosq = cosq_ref[...]
    sinq = sinq_ref[...]
    for h in range(MLA_HEADS):
        sl = slice(h * MLA_QK_PAD, (h + 1) * MLA_QK_PAD)
        qa_ref[0, h] = (qm[:, sl] * cosq + qr[:, sl] * sinq).astype(qa_ref.dtype)
    ckvn = (_rms(ckv_ref[0].astype(F32)) * kvn_ref[...]).astype(BF16)
    kv = jnp.dot(ckvn, wkv_ref[...], preferred_element_type=F32)
    kr = kr_ref[0].astype(F32)
    quarter = MLA_ROPE // 4
    kpe = (kr * ck_ref[...] + pltpu.roll(kr, LANE - quarter, axis=1) * s1_ref[...]
           + pltpu.roll(kr, quarter, axis=1) * s2_ref[...]).astype(ka_ref.dtype)
    for h in range(MLA_HEADS):
        base = h * (MLA_NOPE + MLA_V)
        ka_ref[0, h, :, 0:MLA_NOPE] = kv[:, base:base + MLA_NOPE].astype(ka_ref.dtype)
        ka_ref[0, h, :, MLA_NOPE:MLA_QK_PAD] = kpe
        va_ref[0, h] = kv[:, base + MLA_NOPE:base + MLA_NOPE + MLA_V].T.astype(va_ref.dtype)


def _rope128(y, c, s1, s2):
    quarter = GQA_DIM // 4
    return y * c + pltpu.roll(y, LANE - quarter, axis=1) * s1 + pltpu.roll(y, quarter, axis=1) * s2


def _gqa_prep_kernel(gq_ref, gk_ref, gv_ref, qn_ref, kn_ref, cq_ref, s1q_ref, s2q_ref,
                     ck_ref, s1k_ref, s2k_ref, qb_ref, kb_ref, vb_ref):
    gq = gq_ref[0].astype(F32)
    for h in range(GQA_HEADS):
        y = _rms(gq[:, h * GQA_DIM:(h + 1) * GQA_DIM]) * qn_ref[...]
        qb_ref[0, h] = _rope128(y, cq_ref[...], s1q_ref[...], s2q_ref[...]).astype(qb_ref.dtype)
    gk = gk_ref[0].astype(F32)
    gv = gv_ref[0].astype(F32)
    for h in range(GQA_KV_HEADS):
        y = _rms(gk[:, h * GQA_DIM:(h + 1) * GQA_DIM]) * kn_ref[...]
        kb_ref[0, h] = _rope128(y, ck_ref[...], s1k_ref[...], s2k_ref[...]).astype(kb_ref.dtype)
        vb_ref[0, h] = gv[:, h * GQA_DIM:(h + 1) * GQA_DIM].T.astype(vb_ref.dtype)


N_MLA_IN, N_GQA_IN = 13, 11


def _qkv_prep_kernel(*refs):
    mla_in, gqa_in = refs[:N_MLA_IN], refs[N_MLA_IN:N_MLA_IN + N_GQA_IN]
    outs = refs[N_MLA_IN + N_GQA_IN:]
    _mla_prep_kernel(*mla_in, *outs[:3])
    _gqa_prep_kernel(*gqa_in, *outs[3:])


def _qkv_prep(p3, mla_qn, mla_kvn, wqm, wqr, wkv, mla_tabs, gqa_qn, gqa_kn, gqa_tabs, ctx_len):
    b, t, _ = p3.shape
    tm = _tile(ctx_len, ROW_TILE)

    def col(name, width):
        blk = P_OFF[name] // width
        return pl.BlockSpec((1, tm, width), lambda bi, i: (bi, i, blk))

    def full(a):
        return pl.BlockSpec(a.shape, lambda bi, i: (0,) * a.ndim)

    def rows(a):
        return pl.BlockSpec((tm, a.shape[1]), lambda bi, i: (i, 0))

    hm = lambda nh, w: pl.BlockSpec((1, nh, tm, w), lambda bi, i: (bi, 0, i, 0))
    hmt = lambda nh, w: pl.BlockSpec((1, nh, w, tm), lambda bi, i: (bi, 0, 0, i))
    in_specs = ([col('cq', MLA_Q_LORA), col('ckv', MLA_KV_LORA), col('kr', LANE),
                 full(mla_qn), full(mla_kvn), full(wqm), full(wqr), full(wkv)]
                + [rows(a) for a in mla_tabs]
                + [col('gq', GQA_WIDTH), col('gk', GQA_KV_WIDTH), col('gv', GQA_KV_WIDTH),
                   full(gqa_qn), full(gqa_kn)] + [rows(a) for a in gqa_tabs])
    args = (p3, p3, p3, mla_qn, mla_kvn, wqm, wqr, wkv, *mla_tabs, p3, p3, p3, gqa_qn, gqa_kn, *gqa_tabs)
    assert len(mla_tabs) + 8 == N_MLA_IN and len(gqa_tabs) + 5 == N_GQA_IN
    outs = pl.pallas_call(
        _qkv_prep_kernel,
        grid=(b, t // tm),
        in_specs=in_specs,
        out_specs=[hm(MLA_HEADS, MLA_QK_PAD), hm(MLA_HEADS, MLA_QK_PAD), hmt(MLA_HEADS, MLA_V),
                   hm(GQA_HEADS, GQA_DIM), hm(GQA_KV_HEADS, GQA_DIM), hmt(GQA_KV_HEADS, GQA_DIM)],
        out_shape=[jax.ShapeDtypeStruct((b, MLA_HEADS, t, MLA_QK_PAD), BF16),
                   jax.ShapeDtypeStruct((b, MLA_HEADS, t, MLA_QK_PAD), BF16),
                   jax.ShapeDtypeStruct((b, MLA_HEADS, MLA_V, t), BF16),
                   jax.ShapeDtypeStruct((b, GQA_HEADS, t, GQA_DIM), BF16),
                   jax.ShapeDtypeStruct((b, GQA_KV_HEADS, t, GQA_DIM), BF16),
                   jax.ShapeDtypeStruct((b, GQA_KV_HEADS, GQA_DIM, t), BF16)],
        compiler_params=_params("parallel", "parallel"),
        name="qkv_prep",
    )(*args)
    return outs[:3], outs[3:]


ATTN_KC = 256
ATTN_QB = LANE
MLA_ATTN_STEP = (256, 2)
GQA_ATTN_STEP = (128, 1)
LOG2E = math.log2(math.e)


def _attn_logits(qts, k_ref, s_buf, nkeys):
    nqb = qts[0].shape[0] // ATTN_QB
    for j, q in enumerate(qts):
        st = lax.dot_general(k_ref[0, j, 0:nkeys, :], q, (((1,), (1,)), ((), ())),
                             preferred_element_type=F32)
        for b in range(nqb):
            s_buf[j * nqb + b, 0:nkeys, :] = st[:, b * ATTN_QB:(b + 1) * ATTN_QB]


def _attn_softmax_slab(s_buf, p_buf, l_buf, b, nkeys):
    parts = [jnp.max(s_buf[b, k0:k0 + ATTN_KC, :].reshape(ATTN_KC // 64, 8, 8, ATTN_QB), axis=0)
             for k0 in range(0, nkeys, ATTN_KC)]
    while len(parts) > 1:
        parts = [jnp.maximum(parts[i], parts[i + 1]) if i + 1 < len(parts) else parts[i]
                 for i in range(0, len(parts), 2)]
    m = jnp.max(parts[0], axis=(0, 1), keepdims=True)[0]
    lacc = jnp.zeros((8, 8, ATTN_QB), F32)
    for k0 in range(0, nkeys, ATTN_KC):
        p = jnp.exp2(s_buf[b, k0:k0 + ATTN_KC, :] - m)
        lacc = lacc + jnp.sum(p.reshape(ATTN_KC // 64, 8, 8, ATTN_QB), axis=0)
        p_buf[b, k0:k0 + ATTN_KC, :] = p.astype(p_buf.dtype)
    l = jnp.sum(lacc, axis=(0, 1), keepdims=True)[0]
    l_buf[b] = jnp.broadcast_to(1.0 / l, (8, ATTN_QB))


def _attn_output(vt_ref, p_buf, l_buf, o_ref, nkeys, hps, group, tq):
    dv = vt_ref.shape[2]
    nqb = group * tq // ATTN_QB
    for j in range(hps):
        slabs = range(j * nqb, (j + 1) * nqb)
        pt = jnp.concatenate([p_buf[b, 0:nkeys, :] for b in slabs], axis=1)
        ot = jnp.dot(vt_ref[0, j, :, 0:nkeys], pt, preferred_element_type=F32)
        ot = ot * jnp.concatenate([l_buf[b, 0:1, :] for b in slabs], axis=1)
        for g in range(group):
            c0 = (j * group + g) * dv
            o_ref[0, :, c0:c0 + dv] = ot[:, g * tq:(g + 1) * tq].T.astype(o_ref.dtype)


def _attn_queries(q_ref, hps, group):
    return [jnp.concatenate([q_ref[0, j * group + g] for g in range(group)], axis=0)
            for j in range(hps)]


def _attn_ctx_kernel(q_ref, k_ref, vt_ref, o_ref, s_sc, p_sc, l_sc, *, group, tq):
    hps, nkeys = k_ref.shape[1], k_ref.shape[2]
    nslab = hps * group * tq // ATTN_QB
    _attn_logits(_attn_queries(q_ref, hps, group), k_ref, s_sc, nkeys)

    def slab(b, carry):
        _attn_softmax_slab(s_sc, p_sc, l_sc, b, nkeys)
        return carry

    lax.fori_loop(0, nslab, slab, 0)
    _attn_output(vt_ref, p_sc, l_sc, o_ref, nkeys, hps, group, tq)


def _attn_lat_kernel(q_ref, k_ref, vt_ref, o_ref, s0, s1, p0, p1, l0, l1, *, group, tq):
    hps, nkeys = k_ref.shape[1], k_ref.shape[2]
    nslab = hps * group * tq // ATTN_QB
    g = pl.program_id(0)

    @pl.when(g == 0)
    def _():
        for buf in (s0, s1, p0, p1, l0, l1):
            buf[...] = jnp.zeros(buf.shape, buf.dtype)

    def step(s_a, s_b, p_b, p_c, l_b, l_c):
        _attn_logits(_attn_queries(q_ref, hps, group), k_ref, s_a, nkeys)
        for b in range(nslab):
            _attn_softmax_slab(s_b, p_b, l_b, b, nkeys)
        _attn_output(vt_ref, p_c, l_c, o_ref, nkeys, hps, group, tq)

    pl.when(g % 2 == 0)(lambda: step(s0, s1, p1, p0, l1, l0))
    pl.when(g % 2 == 1)(lambda: step(s1, s0, p0, p1, l0, l1))


def _attention(q, k, vt, ctx_len, tq, hps, with_ctx_queries, out_dtype):
    b, hq, t, dk = q.shape
    hkv, dv = k.shape[1], vt.shape[2]
    group = hq // hkv
    tq = _tile(ctx_len, tq, LANE)
    nslab = hps * group * tq // ATTN_QB
    ctx_tiles = ctx_len // tq
    nq = t // tq - ctx_tiles
    width = hps * group * dv
    assert t % ATTN_KC == 0 and ctx_len % ATTN_KC == 0 and hkv % hps == 0

    def scratch(nkeys):
        return [pltpu.VMEM((nslab, nkeys, ATTN_QB), F32), pltpu.VMEM((nslab, nkeys, ATTN_QB), BF16),
                pltpu.VMEM((nslab, 8, ATTN_QB), F32)]

    s_lat, p_lat, l_lat = scratch(t)
    nh = hkv // hps
    items = b * nh * nq

    def item(step):
        i = jnp.clip(step, 0, items - 1)
        return i // (nh * nq), (i // nq) % nh, i % nq

    def q_map(g):
        bi, h, qi = item(g)
        return bi, h, qi + ctx_tiles, 0

    def k_map(g):
        bi, h, _ = item(g)
        return bi, h, 0, 0

    def v_map(g):
        bi, h, _ = item(g - 2)
        return bi, h, 0, 0

    def o_map(g):
        bi, h, qi = item(g - 2)
        return bi, qi, h

    y_lat = pl.pallas_call(
        functools.partial(_attn_lat_kernel, group=group, tq=tq),
        grid=(items + 2,),
        in_specs=[pl.BlockSpec((1, hps * group, tq, dk), q_map),
                  pl.BlockSpec((1, hps, t, dk), k_map),
                  pl.BlockSpec((1, hps, dv, t), v_map)],
        out_specs=pl.BlockSpec((1, tq, width), o_map),
        out_shape=jax.ShapeDtypeStruct((b, nq * tq, hq * dv), out_dtype),
        scratch_shapes=[s_lat, s_lat, p_lat, p_lat, l_lat, l_lat],
        compiler_params=_params("arbitrary"),
        name="attention_lat_dk%d" % dk,
    )(q, k, vt)
    if not with_ctx_queries:
        return y_lat
    y_ctx = pl.pallas_call(
        functools.partial(_attn_ctx_kernel, group=group, tq=tq),
        grid=(b, hkv // hps, ctx_tiles),
        in_specs=[pl.BlockSpec((1, hps * group, tq, dk), lambda bi, h, i: (bi, h, i, 0)),
                  pl.BlockSpec((1, hps, ctx_len, dk), lambda bi, h, i: (bi, h, 0, 0)),
                  pl.BlockSpec((1, hps, dv, ctx_len), lambda bi, h, i: (bi, h, 0, 0))],
        out_specs=pl.BlockSpec((1, tq, width), lambda bi, h, i: (bi, i, h)),
        out_shape=jax.ShapeDtypeStruct((b, ctx_len, hq * dv), out_dtype),
        scratch_shapes=scratch(ctx_len),
        compiler_params=_params("parallel", "parallel", "arbitrary"),
        name="attention_ctx_dk%d" % dk,
    )(q, k, vt)
    return y_ctx, y_lat


def _conv_kernel(x_ref, w_ref, b_ref, xt_ref, bc_ref, pad_sc, *, ctx_len, rows, x_blocks):
    t = x_ref.shape[1]
    nch = x_ref.shape[2]
    halo = SUBLANE
    segs = ((0, ctx_len), (ctx_len, t))
    zeros = jnp.zeros((halo, nch), F32)
    for si, (lo, hi) in enumerate(segs):
        pad_sc[lo + si * halo:lo + (si + 1) * halo, :] = zeros
        for r0 in range(lo, hi, rows):
            pad_sc[r0 + (si + 1) * halo:r0 + (si + 1) * halo + rows, :] = x_ref[0, r0:r0 + rows, :].astype(F32)
    pad_sc[t + 2 * halo:t + 3 * halo, :] = zeros
    w = w_ref[...]
    bias = b_ref[...]

    def emit(transposed):
        for si, (lo, hi) in enumerate(segs):
            for r0 in range(lo, hi, rows):
                base = r0 + (si + 1) * halo - SSD_CONV // 2
                acc = bias + w[0:1, :] * pad_sc[base:base + rows, :]
                for kk in range(1, SSD_CONV):
                    acc = acc + w[kk:kk + 1, :] * pad_sc[base + kk:base + kk + rows, :]
                y = _silu(acc)
                if transposed:
                    xt_ref[0, :, r0:r0 + rows] = y.T.astype(xt_ref.dtype)
                else:
                    bc_ref[0, r0:r0 + rows, :] = y.astype(bc_ref.dtype)

    is_x = pl.program_id(1) < x_blocks
    pl.when(is_x)(lambda: emit(True))
    pl.when(jnp.logical_not(is_x))(lambda: emit(False))


def _conv(p3, conv_w, conv_b, ctx_len, out_dtype):
    b, t, _ = p3.shape
    nch = 256
    rows = _tile(ctx_len, ROW_TILE)
    blk0 = P_OFF['xbc'] // nch
    x_blocks = SSD_INNER // nch
    kern = functools.partial(_conv_kernel, ctx_len=ctx_len, rows=rows, x_blocks=x_blocks)
    return pl.pallas_call(
        kern,
        grid=(b, SSD_CONV_DIM // nch),
        in_specs=[pl.BlockSpec((1, t, nch), lambda bi, j: (bi, 0, blk0 + j)),
                  pl.BlockSpec((SSD_CONV, nch), lambda bi, j: (0, j)),
                  pl.BlockSpec((1, nch), lambda bi, j: (0, j))],
        out_specs=[pl.BlockSpec((1, nch, t), lambda bi, j: (bi, jnp.minimum(j, x_blocks - 1), 0)),
                   pl.BlockSpec((1, t, nch), lambda bi, j: (bi, 0, jnp.maximum(j - x_blocks, 0)))],
        out_shape=[jax.ShapeDtypeStruct((b, SSD_INNER, t), out_dtype),
                   jax.ShapeDtypeStruct((b, t, SSD_CONV_DIM - SSD_INNER), out_dtype)],
        scratch_shapes=[pltpu.VMEM((t + 3 * SUBLANE, nch), F32)],
        compiler_params=_params("parallel", "arbitrary"),
        name="ssd_conv",
    )(p3, conv_w, conv_b.reshape(1, SSD_CONV_DIM))


def _split3(v):
    hi = v.astype(BF16)
    r1 = v - hi.astype(F32)
    mid = r1.astype(BF16)
    lo = (r1 - mid.astype(F32)).astype(BF16)
    return hi, mid, lo


def _expand_rows(v):
    q = v.shape[1]
    return jnp.concatenate([jnp.broadcast_to(v[e:e + 1, :], (SSD_P, q)) for e in range(SSD_E)], axis=0)


def _ssd_decays(dt_col_raw, dt_row_raw, bias_col, bias_row, a_col, a_row, backward):
    q = dt_col_raw.shape[0]
    ri = lax.broadcasted_iota(jnp.int32, (q, q), 0)
    ci = lax.broadcasted_iota(jnp.int32, (q, q), 1)
    tri_col = jnp.where((ri <= ci) if backward else (ri >= ci), 1.0, 0.0).astype(BF16)
    tri_row = jnp.where((ri >= ci) if backward else (ri <= ci), 1.0, 0.0).astype(BF16)
    dt_c = _softplus(dt_col_raw + bias_col)
    dt_r = _softplus(dt_row_raw + bias_row)
    cum_c = sum(jnp.dot(tri_col, part, preferred_element_type=F32) for part in _split3(dt_c * a_col))
    cum_r = sum(jnp.dot(part, tri_row, preferred_element_type=F32) for part in _split3(dt_r * a_row))
    total = jnp.broadcast_to(cum_r[:, 0:1] if backward else cum_r[:, q - 1:q], (SSD_E, q))
    return dt_r, cum_c, cum_r, total


def _ssd_direction(xt, bmb, cmb, cbt, decays, h_ref, backward):
    q = xt.shape[1]
    dt_r, cum_c, cum_r, total = decays
    ri = lax.broadcasted_iota(jnp.int32, (q, q), 0)
    ci = lax.broadcasted_iota(jnp.int32, (q, q), 1)
    keep_t = (ci <= ri) if backward else (ci >= ri)

    nt = (((1,), (1,)), ((), ()))
    h = h_ref[...]
    y_off = lax.dot_general(h.astype(BF16), cmb, nt, preferred_element_type=F32) * _expand_rows(jnp.exp(cum_r))
    wgt = (xt * _expand_rows(dt_r * jnp.exp(total - cum_r))).astype(BF16)
    h_ref[...] = _expand_rows(jnp.exp(total)) * h + jnp.dot(wgt, bmb, preferred_element_type=F32)

    xdt = (xt * _expand_rows(dt_r)).astype(BF16)
    parts = []
    for e in range(SSD_E):
        seg = cum_r[e:e + 1, :] - cum_c[:, e:e + 1]
        dec = jnp.exp(jnp.where(keep_t, seg, -jnp.inf))
        parts.append(jnp.dot(xdt[e * SSD_P:(e + 1) * SSD_P, :], (cbt * dec).astype(BF16),
                             preferred_element_type=F32))
    return jnp.concatenate(parts, axis=0) + y_off


def _ssd_kernel(xf_ref, bf_ref, cf_ref, xb_ref, bb_ref, cb_ref, dcf_ref, drf_ref, dcb_ref, drb_ref,
                bias_c_ref, bias_r_ref, a_c_ref, a_r_ref, d_ref, yf_ref, yb_ref, hf_sc, hb_sc):
    @pl.when(pl.program_id(2) == 0)
    def _():
        hf_sc[...] = jnp.zeros(hf_sc.shape, F32)
        hb_sc[...] = jnp.zeros(hb_sc.shape, F32)

    nsub = xf_ref.shape[2] // SSD_Q
    gps = dcf_ref.shape[2]
    chunks = [slice(i * SSD_Q, (i + 1) * SSD_Q) for i in range(nsub)]
    dec_f = [[_ssd_decays(dcf_ref[0, 0, gi, rs, :], drf_ref[0, 0, gi, :, rs], bias_c_ref[0, gi],
                          bias_r_ref[0, gi], a_c_ref[0, gi], a_r_ref[0, gi], False) for rs in chunks]
             for gi in range(gps)]
    dec_b = [[_ssd_decays(dcb_ref[0, 0, gi, rs, :], drb_ref[0, 0, gi, :, rs], bias_c_ref[1, gi],
                          bias_r_ref[1, gi], a_c_ref[1, gi], a_r_ref[1, gi], True) for rs in chunks]
             for gi in range(gps)]
    nt = (((1,), (1,)), ((), ()))
    for gi in range(gps):
        ch = slice(gi * SSD_GW, (gi + 1) * SSD_GW)
        st = slice(gi * SSD_N, (gi + 1) * SSD_N)
        bcf = [lax.dot_general(bf_ref[0, rs, st], cf_ref[0, rs, st], nt, preferred_element_type=F32)
               for rs in chunks]
        bcb = [lax.dot_general(bb_ref[0, rs, st], cb_ref[0, rs, st], nt, preferred_element_type=F32)
               for rs in chunks]
        for i in range(nsub):
            rf, rb = chunks[i], chunks[nsub - 1 - i]
            xt = xf_ref[0, ch, rf].astype(F32)
            yf = _ssd_direction(xt, bf_ref[0, rf, st], cf_ref[0, rf, st], bcf[i],
                                dec_f[gi][i], hf_sc.at[ch], False)
            yf_ref[0, ch, rf] = (yf + d_ref[gi] * xt).astype(yf_ref.dtype)
            yb = _ssd_direction(xb_ref[0, ch, rb].astype(F32), bb_ref[0, rb, st], cb_ref[0, rb, st],
                                bcb[nsub - 1 - i], dec_b[gi][nsub - 1 - i], hb_sc.at[ch], True)
            yb_ref[0, ch, rb] = yb.astype(yb_ref.dtype)


def _ssd(xt, bc, dt_col, dt_row, bias_c, bias_r, a_c, a_r, d_rows, ctx_len, out_dtype):
    b, _, t = xt.shape
    assert SSD_Q == SSD_N
    rows = _tile(ctx_len, SSD_ROWS, SSD_Q)
    nblk = t // rows
    nctx = ctx_len // rows

    def bidx(c):
        return jnp.where(c < nctx, nctx - 1 - c, nblk - 1 - (c - nctx))

    gps = SSD_GPS
    ngs = SSD_G // gps
    fx = lambda bi, g, c: (bi, g, c)
    fb = lambda bi, g, c: (bi, c, g)
    fc = lambda bi, g, c: (bi, c, ngs + g)
    bx = lambda bi, g, c: (bi, g, bidx(c))
    bb = lambda bi, g, c: (bi, bidx(c), g)
    bcm = lambda bi, g, c: (bi, bidx(c), ngs + g)
    small = lambda a: pl.BlockSpec((2, gps) + a.shape[2:], lambda bi, g, c: (0, g, 0, 0))
    return pl.pallas_call(
        _ssd_kernel,
        grid=(b, ngs, nblk),
        in_specs=[pl.BlockSpec((1, gps * SSD_GW, rows), fx), pl.BlockSpec((1, rows, gps * SSD_N), fb),
                  pl.BlockSpec((1, rows, gps * SSD_N), fc),
                  pl.BlockSpec((1, gps * SSD_GW, rows), bx), pl.BlockSpec((1, rows, gps * SSD_N), bb),
                  pl.BlockSpec((1, rows, gps * SSD_N), bcm),
                  pl.BlockSpec((1, 1, gps, rows, SSD_E), lambda bi, g, c: (bi, 0, g, c, 0)),
                  pl.BlockSpec((1, 1, gps, SSD_E, rows), lambda bi, g, c: (bi, 0, g, 0, c)),
                  pl.BlockSpec((1, 1, gps, rows, SSD_E), lambda bi, g, c: (bi, 1, g, bidx(c), 0)),
                  pl.BlockSpec((1, 1, gps, SSD_E, rows), lambda bi, g, c: (bi, 1, g, 0, bidx(c))),
                  small(bias_c), small(bias_r), small(a_c), small(a_r),
                  pl.BlockSpec((gps, SSD_GW, SSD_Q), lambda bi, g, c: (g, 0, 0))],
        out_specs=[pl.BlockSpec((1, gps * SSD_GW, rows), fx), pl.BlockSpec((1, gps * SSD_GW, rows), bx)],
        out_shape=[jax.ShapeDtypeStruct((b, SSD_INNER, t), out_dtype)] * 2,
        scratch_shapes=[pltpu.VMEM((gps * SSD_GW, SSD_N), F32), pltpu.VMEM((gps * SSD_GW, SSD_N), F32)],
        compiler_params=_params("parallel", "parallel", "arbitrary"),
        name="ssd_scan",
    )(xt, bc, bc, xt, bc, bc, dt_col, dt_row, dt_col, dt_row, bias_c, bias_r, a_c, a_r, d_rows)


MERGE_TN = 2048


def _merge_kernel(*refs, ctx_tiles):
    if ctx_tiles:
        ya_c, ya_l, yb_c, yb_l, *refs = refs
        is_ctx = pl.program_id(1) < ctx_tiles
        ya = jnp.where(is_ctx, ya_c[0], ya_l[0])
        yb = jnp.where(is_ctx, yb_c[0], yb_l[0])
    else:
        ya_ref, yb_ref, *refs = refs
        ya, yb = ya_ref[0], yb_ref[0]
    ga_ref, gb_ref, yf_ref, ybk_ref, z_ref, nrm_ref, mg_ref, wa_ref, wb_ref, wc_ref, u_ref, c_sc = refs
    a_in = (ya.astype(F32) * _silu(ga_ref[0].astype(F32))).astype(BF16)
    b_in = (yb.astype(F32) * _silu(gb_ref[0].astype(F32))).astype(BF16)
    v = (yf_ref[0].astype(F32) + ybk_ref[0].astype(F32)).T * _silu(z_ref[0].astype(F32))
    for g in range(SSD_G):
        sl = slice(g * SSD_GW, (g + 1) * SSD_GW)
        c_sc[:, sl] = (_rms(v[:, sl]) * nrm_ref[:, sl]).astype(BF16)
    c_in = c_sc[...]
    d = u_ref.shape[-1]
    for j in range(0, d, MERGE_TN):
        sl = slice(j, j + MERGE_TN)
        br_a = jnp.dot(a_in, wa_ref[:, sl], preferred_element_type=F32)
        br_b = jnp.dot(b_in, wb_ref[:, sl], preferred_element_type=F32)
        br_c = jnp.dot(c_in, wc_ref[:, sl], preferred_element_type=F32)
        u = (jax.nn.sigmoid(mg_ref[0, :, j:j + MERGE_TN].astype(F32)) * br_a
             + jax.nn.sigmoid(mg_ref[0, :, d + j:d + j + MERGE_TN].astype(F32)) * br_b
             + jax.nn.sigmoid(mg_ref[0, :, 2 * d + j:2 * d + j + MERGE_TN].astype(F32)) * br_c)
        u_ref[0, :, sl] = u.astype(u_ref.dtype)


def _merge(ya, yb, yf, ybk, p3, ssd_norm, wa, wb, wc, row_off):
    split = isinstance(ya, tuple)
    b = yf.shape[0]
    rows = sum(a.shape[1] for a in ya) if split else ya.shape[1]
    tm = _tile(ya[0].shape[1] if split else rows, ROW_TILE)
    d = D_MODEL
    ro = row_off // tm
    ctx_tiles = ya[0].shape[1] // tm if split else 0
    assert P_OFF['mg'] == 0 and not (split and row_off)

    def col(name, width):
        blk = P_OFF[name] // width
        return pl.BlockSpec((1, tm, width), lambda bi, i: (bi, i + ro, blk))

    loc = lambda w: pl.BlockSpec((1, tm, w), lambda bi, i: (bi, i, 0))
    glob = lambda w: pl.BlockSpec((1, tm, w), lambda bi, i: (bi, i + ro, 0))
    globt = pl.BlockSpec((1, SSD_INNER, tm), lambda bi, i: (bi, 0, i + ro))
    wspec = lambda k: pl.BlockSpec((k, d), lambda bi, i: (0, 0), pipeline_mode=pl.Buffered(1))
    if split:
        def pair(w):
            return [pl.BlockSpec((1, tm, w), lambda bi, i: (bi, jnp.minimum(i, ctx_tiles - 1), 0)),
                    pl.BlockSpec((1, tm, w), lambda bi, i: (bi, jnp.maximum(i - ctx_tiles, 0), 0))]
        y_specs, y_args = pair(MLA_WIDTH) + pair(GQA_WIDTH), [*ya, *yb]
    else:
        y_specs, y_args = [loc(MLA_WIDTH), loc(GQA_WIDTH)], [ya, yb]
    return pl.pallas_call(
        functools.partial(_merge_kernel, ctx_tiles=ctx_tiles),
        grid=(b, rows // tm),
        in_specs=y_specs + [col('ga', MLA_WIDTH), col('gb', GQA_WIDTH),
                            globt, globt, col('z', SSD_INNER),
                            pl.BlockSpec((1, SSD_INNER), lambda bi, i: (0, 0)),
                            glob(N_BRANCH * d), wspec(MLA_WIDTH), wspec(GQA_WIDTH), wspec(SSD_INNER)],
        out_specs=pl.BlockSpec((1, tm, d), lambda bi, i: (bi, i, 0)),
        out_shape=jax.ShapeDtypeStruct((b, rows, d), BF16),
        scratch_shapes=[pltpu.VMEM((tm, SSD_INNER), BF16)],
        compiler_params=_params("parallel", "parallel"),
        name="merge",
    )(*y_args, p3, p3, yf, ybk, p3, ssd_norm.reshape(1, SSD_INNER), p3, wa, wb, wc)


def _out_ln_kernel(u_ref, w_ref, *rest, with_next, ctx_tiles, ro, split):
    if split:
        ctx_ref, x_ref, gate_ref, g_ref, b_ref, *rest = rest
        res = _residual_rows(ctx_ref, x_ref, ctx_tiles, ro)
    else:
        x_ref, gate_ref, g_ref, b_ref, *rest = rest
        res = x_ref[0]
    out = jnp.dot(u_ref[0], w_ref[...], preferred_element_type=F32)
    r = DEEPNORM_ALPHA * res + gate_ref[0] * out
    xn = _layer_norm(r) * g_ref[...] + b_ref[...]
    if with_next:
        sh_ref, sc_ref, xo_ref, xm_ref = rest
        xo_ref[0] = xn
        xm_ref[0] = (_layer_norm(xn) * (1.0 + sc_ref[0]) + sh_ref[0]).astype(xm_ref.dtype)
    else:
        (xo_ref,) = rest
        xo_ref[0] = xn


def _out_ln(u, w_out, xc, tab, ln_g, ln_b, next_tab, nb, ctx_len, row_off):
    b, rows, d = u.shape
    tm = _tile(ctx_len, ROW_TILE)
    ro = row_off // tm
    ctx_tiles = ctx_len // tm
    row = _mod_row_index(nb, ctx_tiles)
    with_next = next_tab is not None
    split = isinstance(xc, tuple)
    loc = pl.BlockSpec((1, tm, d), lambda bi, i: (bi, i, 0))
    vec = pl.BlockSpec((1, d), lambda bi, i: (0, 0))
    res_specs = (_residual_specs(tm, d, ctx_tiles, ro) if split
                 else [pl.BlockSpec((1, tm, d), lambda bi, i: (bi, i + ro, 0))])
    in_specs = [loc, pl.BlockSpec((d, d), lambda bi, i: (0, 0))] + res_specs + [
        pl.BlockSpec((1, 1, d), lambda bi, i: (row(bi, i + ro), 0, 2)), vec, vec]
    args = [u, w_out] + (list(xc) if split else [xc]) + [tab, ln_g.reshape(1, d), ln_b.reshape(1, d)]
    out_specs = [loc]
    out_shape = [jax.ShapeDtypeStruct((b, rows, d), F32)]
    if with_next:
        in_specs += [pl.BlockSpec((1, 1, d), lambda bi, i: (row(bi, i + ro), 0, 0)),
                     pl.BlockSpec((1, 1, d), lambda bi, i: (row(bi, i + ro), 0, 1))]
        args += [next_tab, next_tab]
        out_specs.append(loc)
        out_shape.append(jax.ShapeDtypeStruct((b, rows, d), BF16))
    return pl.pallas_call(
        functools.partial(_out_ln_kernel, with_next=with_next, ctx_tiles=ctx_tiles, ro=ro, split=split),
        grid=(b, rows // tm),
        in_specs=in_specs,
        out_specs=out_specs,
        out_shape=out_shape,
        compiler_params=_params("parallel", "parallel"),
        name="out_ln",
    )(*args)


def _rope_angles(rows, dim):
    row, col = jnp.meshgrid(jnp.arange(rows, dtype=F32), jnp.arange(GRID_W, dtype=F32), indexing='ij')
    half = dim // 2
    inv_freq = ROPE_THETA ** (-jnp.arange(0, half, 2, dtype=F32) / half)
    ang_r = row.reshape(-1, 1) * inv_freq
    ang_c = col.reshape(-1, 1) * inv_freq
    return jnp.concatenate([ang_r, ang_r, ang_c, ang_c], axis=-1)


def _rope_tables(seq, ctx_len, dim):
    ang = _rope_angles(seq // GRID_W, dim)
    cos = jnp.concatenate([jnp.ones((ctx_len, dim), F32), jnp.cos(ang)], axis=0)
    sin = jnp.concatenate([jnp.zeros((ctx_len, dim), F32), jnp.sin(ang)], axis=0)
    return cos, sin


def _roll_tables(cos, sin, dim, scale):
    t = cos.shape[0]
    quarter = dim // 4
    first = (jnp.arange(dim) % (2 * quarter)) < quarter
    s1 = jnp.where(first, -sin, 0.0)
    s2 = jnp.where(first, 0.0, sin)
    pad = lambda a: jnp.pad(a * scale, ((0, 0), (0, LANE - dim)))
    return pad(cos), pad(s1), pad(s2)


def _rot_matrix(dim):
    quarter = dim // 4
    r = np.zeros((dim, dim), np.float32)
    for i in range(dim):
        blk = i // quarter
        if blk % 2 == 0:
            r[i + quarter, i] = -1.0
        else:
            r[i - quarter, i] = 1.0
    return jnp.asarray(r)


def _mla_q_weights(w_uq):
    w = w_uq.reshape(MLA_Q_LORA, MLA_HEADS, MLA_QK)
    nope, pe = w[..., :MLA_NOPE], w[..., MLA_NOPE:]
    pe_rot = jnp.einsum('khd,de->khe', pe, _rot_matrix(MLA_ROPE), precision=HIGHEST)
    zpad = jnp.zeros((MLA_Q_LORA, MLA_HEADS, MLA_QK_PAD - MLA_QK), F32)
    main = jnp.concatenate([nope, pe, zpad], axis=-1)
    rot = jnp.concatenate([jnp.zeros_like(nope), pe_rot, zpad], axis=-1)
    return (main.reshape(MLA_Q_LORA, -1).astype(BF16), rot.reshape(MLA_Q_LORA, -1).astype(BF16))


def _permute_w_in(w):
    parts = [w[:, IN_OFFSETS[n]:IN_OFFSETS[n] + IN_WIDTHS[n]] for n in P_ORDER]
    parts.append(jnp.zeros((w.shape[0], P_WIDTH - P_USED), w.dtype))
    return jnp.concatenate(parts, axis=1).astype(BF16)


P_DTYPE = BF16
Y_DTYPE = BF16


def kernel(x, c, ctx, c_ctx, w_mod, b_mod, w_in, mla_q_norm, mla_w_uq, mla_kv_norm, mla_w_ukv,
           gqa_q_norm, gqa_k_norm, ssd_conv_w, ssd_conv_b, ssd_a_log, ssd_dt_bias, ssd_d, ssd_norm,
           w_br_a, w_br_b, w_br_c, w_out, ln_g, ln_b):
    nb, seq, d = x.shape
    ctx_len = ctx.shape[1]
    t = ctx_len + seq
    depth = w_in.shape[0]
    assert d == D_MODEL and nb < 8 and seq % GRID_W == 0
    assert ctx_len % SSD_Q == 0 and seq % SSD_Q == 0

    cos_a, sin_a = _rope_tables(seq, ctx_len, MLA_ROPE)
    cos_b, sin_b = _rope_tables(seq, ctx_len, GQA_DIM)
    sq = MLA_QK ** -0.5 * LOG2E
    zq = jnp.zeros((t, MLA_QK_PAD - MLA_QK), F32)
    cosq = jnp.concatenate([jnp.full((t, MLA_NOPE), sq, F32), cos_a * sq, zq], axis=1)
    sinq = jnp.concatenate([jnp.zeros((t, MLA_NOPE), F32), sin_a * sq, zq], axis=1)
    mla_tabs = (cosq, sinq) + _roll_tables(cos_a, sin_a, MLA_ROPE, 1.0)
    gqa_tabs = (_roll_tables(cos_b, sin_b, GQA_DIM, GQA_DIM ** -0.5 * LOG2E)
                + _roll_tables(cos_b, sin_b, GQA_DIM, 1.0))

    c_rows = jnp.zeros((8, d), F32).at[:nb].set(c).at[nb].set(c_ctx)
    mods = _mod_rows(c_rows, w_mod, b_mod)
    tabs = [mods[l].reshape(8, 1, 3 * d) for l in range(depth)]

    xc = (ctx, x)
    xm = _ln_mod(ctx, x, tabs[0], nb)

    for l in range(depth):
        last = l == depth - 1
        wp = _permute_w_in(w_in[l])
        p2, krdt = _in_proj(xm.reshape(nb * t, d), wp, P_DTYPE)
        p3 = p2.reshape(nb, t, P_WIDTH)

        wqm, wqr = _mla_q_weights(mla_w_uq[l])
        (qa, ka, va), (qb, kb, vb) = _qkv_prep(
            p3, mla_q_norm[l].reshape(1, -1), mla_kv_norm[l].reshape(1, -1), wqm, wqr,
            mla_w_ukv[l].astype(BF16), mla_tabs, gqa_q_norm[l].reshape(1, -1),
            gqa_k_norm[l].reshape(1, -1), gqa_tabs, ctx_len)
        ya = _attention(qa, ka, va, ctx_len, *MLA_ATTN_STEP, not last, Y_DTYPE)
        yb = _attention(qb, kb, vb, ctx_len, *GQA_ATTN_STEP, not last, Y_DTYPE)

        xconv_t, bconv = _conv(p3, ssd_conv_w[l], ssd_conv_b[l], ctx_len, Y_DTYPE)
        dt5 = krdt[:, MLA_ROPE:].reshape(nb, t, 2, SSD_G, SSD_E)
        dt_col = jnp.transpose(dt5, (0, 2, 3, 1, 4))
        dt_row = jnp.transpose(dt5, (0, 2, 3, 4, 1))
        bias = ssd_dt_bias[l].astype(F32).reshape(2, SSD_G, SSD_E)
        a = -jnp.exp(ssd_a_log[l].astype(F32)).reshape(2, SSD_G, SSD_E)
        d_rows = jnp.broadcast_to(jnp.repeat(ssd_d[l].astype(F32), SSD_P).reshape(SSD_G, SSD_GW, 1),
                                  (SSD_G, SSD_GW, SSD_Q))
        yf, ybk = _ssd(xconv_t, bconv, dt_col, dt_row, bias[:, :, None, :], bias[:, :, :, None],
                       a[:, :, None, :], a[:, :, :, None], d_rows, ctx_len, Y_DTYPE)

        row_off = ctx_len if last else 0
        u = _merge(ya, yb, yf, ybk, p3, ssd_norm[l], w_br_a[l].astype(BF16), w_br_b[l].astype(BF16),
                   w_br_c[l].astype(BF16), row_off)
        if last:
            (xo,) = _out_ln(u, w_out[l].astype(BF16), xc, tabs[l], ln_g[l], ln_b[l], None, nb,
                            ctx_len, row_off)
            return xo
        xc, xm = _out_ln(u, w_out[l].astype(BF16), xc, tabs[l], ln_g[l], ln_b[l], tabs[l + 1], nb,
                         ctx_len, row_off)
```
